```python
import jax, jax.numpy as jnp
from jax import lax
import numpy as np

D_MODEL = 1024
BATCH = 8
SEQ = 8192
DEPTH = 1

HEAD_DIM = 64
SWA_Q_HEADS = 8
SWA_KV_HEADS = 2
SWA_GROUP = SWA_Q_HEADS // SWA_KV_HEADS
SWA_WIDTH = SWA_Q_HEADS * HEAD_DIM
SWA_KV_WIDTH = SWA_KV_HEADS * HEAD_DIM
FOX_HEADS = 8
FOX_WIDTH = FOX_HEADS * HEAD_DIM
WINDOW = 128
BLOCK = 128
ROPE_THETA = 10000.0
NORM_EPS = 1e-6
FORGET_BIAS_INIT = 2.0

IN_SPLITS = (SWA_WIDTH, SWA_KV_WIDTH, SWA_KV_WIDTH, SWA_WIDTH,
             FOX_WIDTH, FOX_WIDTH, FOX_WIDTH, FOX_HEADS, FOX_WIDTH,
             D_MODEL, D_MODEL)
IN_WIDTH = sum(IN_SPLITS)

kernel_name = "hybrid_swa_sink_fox_gated_block"


def rms_norm(x, g):
    xf = x.astype(jnp.float32)
    y = xf * lax.rsqrt(jnp.mean(xf * xf, axis=-1, keepdims=True) + NORM_EPS)
    return (y * g.astype(jnp.float32)).astype(x.dtype)


def rope_tables(positions):
    inv_freq = ROPE_THETA ** (-jnp.arange(0, HEAD_DIM, 2, dtype=jnp.float32) / HEAD_DIM)
    ang = positions.astype(jnp.float32)[..., None] * inv_freq
    ang = jnp.concatenate([ang, ang], axis=-1)[:, :, None, :]
    return jnp.cos(ang), jnp.sin(ang)


def apply_rope(t, cos, sin):
    half = HEAD_DIM // 2
    tf = t.astype(jnp.float32)
    rot = jnp.concatenate([-tf[..., half:], tf[..., :half]], axis=-1)
    return (tf * cos + rot * sin).astype(t.dtype)


def split_cols(p, sizes):
    offs = np.cumsum(sizes)[:-1].tolist()
    return jnp.split(p, offs, axis=-1)


def sliding_window_gqa_sinks(q, k, v, sinks):
    b, s = q.shape[:2]
    nb = s // BLOCK
    q = q.reshape(b, nb, BLOCK, SWA_KV_HEADS, SWA_GROUP, HEAD_DIM)
    k = k.reshape(b, nb, BLOCK, SWA_KV_HEADS, HEAD_DIM)
    v = v.reshape(b, nb, BLOCK, SWA_KV_HEADS, HEAD_DIM)
    pad_k = jnp.zeros_like(k[:, :1])
    pad_v = jnp.zeros_like(v[:, :1])
    k_band = jnp.concatenate([jnp.concatenate([pad_k, k[:, :-1]], axis=1), k], axis=2)
    v_band = jnp.concatenate([jnp.concatenate([pad_v, v[:, :-1]], axis=1), v], axis=2)
    logits = jnp.einsum('bnqkgd,bnskd->bnkgqs', q, k_band).astype(jnp.float32) * (HEAD_DIM ** -0.5)
    blk = jnp.arange(nb)[:, None, None]
    qpos = blk * BLOCK + jnp.arange(BLOCK)[None, :, None]
    kpos = (blk - 1) * BLOCK + jnp.arange(2 * BLOCK)[None, None, :]
    rel = qpos - kpos
    mask = (rel >= 0) & (rel < WINDOW) & (kpos >= 0)
    logits = jnp.where(mask[None, :, None, None], logits, -jnp.inf)
    sink = sinks.astype(jnp.float32).reshape(SWA_KV_HEADS, SWA_GROUP)[None, None, :, :, None, None]
    m = jnp.maximum(jnp.max(logits, axis=-1, keepdims=True), sink)
    p = jnp.exp(logits - m)
    denom = jnp.sum(p, axis=-1, keepdims=True) + jnp.exp(sink - m)
    out = jnp.einsum('bnkgqs,bnskd->bnqkgd', (p / denom).astype(v.dtype), v_band)
    return out.reshape(b, s, SWA_WIDTH)


def forgetting_attention(q, k, v, cum):
    b, s = q.shape[:2]
    nb = s // BLOCK
    q_blocks = q.reshape(b, nb, BLOCK, FOX_HEADS, HEAD_DIM).transpose(1, 0, 2, 3, 4)
    cq_blocks = cum.reshape(b, nb, BLOCK, FOX_HEADS).transpose(1, 0, 3, 2)
    ck = cum.transpose(0, 2, 1)[:, :, None, :]
    kpos = jnp.arange(s)
    scale = HEAD_DIM ** -0.5

    def one_block(args):
        qb, cqb, n = args
        logits = jnp.einsum('bqhd,bshd->bhqs', qb, k).astype(jnp.float32) * scale
        logits = logits + cqb[..., None] - ck
        qpos = n * BLOCK + jnp.arange(BLOCK)
        mask = kpos[None, :] <= qpos[:, None]
        logits = jnp.where(mask, logits, -jnp.inf)
        p = jax.nn.softmax(logits, axis=-1)
        return jnp.einsum('bhqs,bshd->bqhd', p.astype(v.dtype), v)

    out = lax.map(one_block, (q_blocks, cq_blocks, jnp.arange(nb)))
    return out.transpose(1, 0, 2, 3, 4).reshape(b, s, FOX_WIDTH)


def _fwd_setup_inputs(seed: int = 0) -> dict:
    key = jax.random.key(seed)
    ks = jax.random.split(key, 14)
    f32 = jnp.float32
    x = jax.random.normal(ks[0], (BATCH, SEQ, D_MODEL), f32)
    c = jax.random.normal(ks[1], (BATCH, D_MODEL), f32)
    positions = jnp.broadcast_to(jnp.arange(SEQ, dtype=jnp.int32), (BATCH, SEQ))
    w_ada = jax.random.normal(ks[2], (DEPTH, D_MODEL, 3 * D_MODEL), f32) * (0.5 * D_MODEL ** -0.5)
    b_ada = jax.random.normal(ks[3], (DEPTH, 3 * D_MODEL), f32) * 0.02
    g_norm = 1.0 + 0.05 * jax.random.normal(ks[4], (DEPTH, D_MODEL), f32)
    w_in = jax.random.normal(ks[5], (DEPTH, D_MODEL, IN_WIDTH), f32) * D_MODEL ** -0.5
    b_f = FORGET_BIAS_INIT + 0.1 * jax.random.normal(ks[6], (DEPTH, FOX_HEADS), f32)
    sinks = jax.random.normal(ks[7], (DEPTH, SWA_Q_HEADS), f32) * 0.5
    w_o_swa = jax.random.normal(ks[8], (DEPTH, SWA_WIDTH, D_MODEL), f32) * SWA_WIDTH ** -0.5
    w_o_fox = jax.random.normal(ks[9], (DEPTH, FOX_WIDTH, D_MODEL), f32) * FOX_WIDTH ** -0.5
    w_out = jax.random.normal(ks[10], (DEPTH, D_MODEL, D_MODEL), f32) * D_MODEL ** -0.5
    g_final = 1.0 + 0.05 * jax.random.normal(ks[11], (D_MODEL,), f32)
    return {"x": x, "c": c, "positions": positions, "w_ada": w_ada, "b_ada": b_ada,
            "g_norm": g_norm, "w_in": w_in, "b_f": b_f, "sinks": sinks,
            "w_o_swa": w_o_swa, "w_o_fox": w_o_fox, "w_out": w_out, "g_final": g_final}


def _fwd_reference(x, c, positions, w_ada, b_ada, g_norm, w_in, b_f, sinks, w_o_swa, w_o_fox, w_out, g_final):
    b, s, _ = x.shape
    cos, sin = rope_tables(positions)
    for l in range(DEPTH):
        ada = c @ w_ada[l] + b_ada[l]
        shift, scale, gate = jnp.split(ada, 3, axis=-1)
        h = rms_norm(x, g_norm[l]) * (1.0 + scale[:, None, :]) + shift[:, None, :]
        proj = h @ w_in[l]
        qa, ka, va, za, qb, kb, vb, fb, zb, ga, gb = split_cols(proj, IN_SPLITS)
        qa = apply_rope(qa.reshape(b, s, SWA_Q_HEADS, HEAD_DIM), cos, sin)
        ka = apply_rope(ka.reshape(b, s, SWA_KV_HEADS, HEAD_DIM), cos, sin)
        va = va.reshape(b, s, SWA_KV_HEADS, HEAD_DIM)
        att_a = sliding_window_gqa_sinks(qa, ka, va, sinks[l])
        y_a = (att_a * jax.nn.silu(za)) @ w_o_swa[l]
        log_f = jax.nn.log_sigmoid(fb.astype(jnp.float32) + b_f[l].astype(jnp.float32))
        cum = jnp.cumsum(log_f, axis=1)
        att_b = forgetting_attention(qb.reshape(b, s, FOX_HEADS, HEAD_DIM),
                                     kb.reshape(b, s, FOX_HEADS, HEAD_DIM),
                                     vb.reshape(b, s, FOX_HEADS, HEAD_DIM), cum)
        y_b = (att_b * jax.nn.silu(zb)) @ w_o_fox[l]
        merged = jax.nn.sigmoid(ga) * y_a + jax.nn.sigmoid(gb) * y_b
        x = x + gate[:, None, :] * (merged @ w_out[l])
    return rms_norm(x, g_final)


import jax as _jax
import jax.numpy as _jnp

TWIN_FORMAT = 'train_step'
FWD_PARAMS = ['x', 'c', 'positions', 'w_ada', 'b_ada', 'g_norm', 'w_in', 'b_f', 'sinks', 'w_o_swa', 'w_o_fox', 'w_out', 'g_final']
TWIN_WEIGHTS = ['w_ada', 'b_ada', 'g_norm', 'w_in', 'b_f', 'sinks', 'w_o_swa', 'w_o_fox', 'w_out', 'g_final']
TWIN_DIFF_INPUT = 'x'
TWIN_INPUTS = ['x', 'c', 'positions', 'w_ada', 'b_ada', 'g_norm', 'w_in', 'b_f', 'sinks', 'w_o_swa', 'w_o_fox', 'w_out', 'g_final', 'loss_target', 'm_w_ada', 'm_b_ada', 'm_g_norm', 'm_w_in', 'm_b_f', 'm_sinks', 'm_w_o_swa', 'm_w_o_fox', 'm_w_out', 'm_g_final', 'v_w_ada', 'v_b_ada', 'v_g_norm', 'v_w_in', 'v_b_f', 'v_sinks', 'v_w_o_swa', 'v_w_o_fox', 'v_w_out', 'v_g_final']
TWIN_OUTPUTS = ['loss', 'grad_x', 'grad_w_ada', 'grad_b_ada', 'grad_g_norm', 'grad_w_in', 'grad_b_f', 'grad_sinks', 'grad_w_o_swa', 'grad_w_o_fox', 'grad_w_out', 'grad_g_final', 'delta_w_ada', 'delta_b_ada', 'delta_g_norm', 'delta_w_in', 'delta_b_f', 'delta_sinks', 'delta_w_o_swa', 'delta_w_o_fox', 'delta_w_out', 'delta_g_final', 'new_m_w_ada', 'new_m_b_ada', 'new_m_g_norm', 'new_m_w_in', 'new_m_b_f', 'new_m_sinks', 'new_m_w_o_swa', 'new_m_w_o_fox', 'new_m_w_out', 'new_m_g_final', 'new_v_w_ada', 'new_v_b_ada', 'new_v_g_norm', 'new_v_w_in', 'new_v_b_f', 'new_v_sinks', 'new_v_w_o_swa', 'new_v_w_o_fox', 'new_v_w_out', 'new_v_g_final']
TWIN_LEAF_KINDS = {'loss': 'loss', 'grad_x': 'grad_x', 'grad_w_ada': 'grad_w', 'grad_b_ada': 'grad_w', 'grad_g_norm': 'grad_w', 'grad_w_in': 'grad_w', 'grad_b_f': 'grad_w', 'grad_sinks': 'grad_w', 'grad_w_o_swa': 'grad_w', 'grad_w_o_fox': 'grad_w', 'grad_w_out': 'grad_w', 'grad_g_final': 'grad_w', 'delta_w_ada': 'delta_w', 'delta_b_ada': 'delta_w', 'delta_g_norm': 'delta_w', 'delta_w_in': 'delta_w', 'delta_b_f': 'delta_w', 'delta_sinks': 'delta_w', 'delta_w_o_swa': 'delta_w', 'delta_w_o_fox': 'delta_w', 'delta_w_out': 'delta_w', 'delta_g_final': 'delta_w', 'new_m_w_ada': 'new_m', 'new_m_b_ada': 'new_m', 'new_m_g_norm': 'new_m', 'new_m_w_in': 'new_m', 'new_m_b_f': 'new_m', 'new_m_sinks': 'new_m', 'new_m_w_o_swa': 'new_m', 'new_m_w_o_fox': 'new_m', 'new_m_w_out': 'new_m', 'new_m_g_final': 'new_m', 'new_v_w_ada': 'new_v', 'new_v_b_ada': 'new_v', 'new_v_g_norm': 'new_v', 'new_v_w_in': 'new_v', 'new_v_b_f': 'new_v', 'new_v_sinks': 'new_v', 'new_v_w_o_swa': 'new_v', 'new_v_w_o_fox': 'new_v', 'new_v_w_out': 'new_v', 'new_v_g_final': 'new_v'}


def _forward(args):
    return _fwd_reference(*[args[k] for k in FWD_PARAMS])


def _output_shape():
    def fwd():
        inp = _fwd_setup_inputs(0)
        return _fwd_reference(*[inp[k] for k in FWD_PARAMS])
    out = _jax.eval_shape(fwd)
    return out.shape, out.dtype

N_MICROBATCH = 1
ADAM_LR = 0.001
ADAM_B1 = 0.9
ADAM_B2 = 0.999
ADAM_EPS = 1e-08
ADAM_WD = 0.01
ADAM_STEP = 10
PER_EXAMPLE_BATCH_AXIS = {'x': 0, 'c': 0, 'positions': 0, 'loss_target': 0}
SHARED_INPUTS = []
_WEIGHT_DTYPES = {'w_ada': _jnp.float32, 'b_ada': _jnp.float32, 'g_norm': _jnp.float32, 'w_in': _jnp.float32, 'b_f': _jnp.float32, 'sinks': _jnp.float32, 'w_o_swa': _jnp.float32, 'w_o_fox': _jnp.float32, 'w_out': _jnp.float32, 'g_final': _jnp.float32}
MOMENT_SCALE = {'w_ada': 9.968676e-02, 'b_ada': 9.779601e-02, 'g_norm': 7.773808e-02, 'w_in': 3.931028e-02, 'b_f': 1.210593e-01, 'sinks': 1.584981e-02, 'w_o_swa': 3.152468e-02, 'w_o_fox': 4.002276e-02, 'w_out': 5.094022e-02, 'g_final': 6.414996e+01}


def _to_microbatches(a, axis):
    t = _jnp.moveaxis(a, axis, 0)
    t = t.reshape((N_MICROBATCH, t.shape[0] // N_MICROBATCH) + t.shape[1:])
    return _jnp.moveaxis(t, 1, axis + 1)


def setup_inputs(seed: int = 0) -> dict:
    inp = _fwd_setup_inputs(seed)
    key = _jax.random.fold_in(_jax.random.key(seed), 7919)
    shape, _ = _output_shape()
    out = dict(inp)
    out["loss_target"] = _jax.random.normal(_jax.random.fold_in(key, 0), shape, _jnp.float32)
    for i, name in enumerate(TWIN_WEIGHTS):
        w = inp[name].astype(_jnp.float32)
        if MOMENT_SCALE is None:
            s = _jnp.sqrt(_jnp.mean(_jnp.square(w)) + 1e-30)
        else:
            s = MOMENT_SCALE[name]
        km, kv = _jax.random.split(_jax.random.fold_in(key, i + 1))
        out[name] = w
        out["m_" + name] = s * _jax.random.normal(km, w.shape, _jnp.float32)
        out["v_" + name] = (s * s) * _jax.random.uniform(kv, w.shape, _jnp.float32, 0.5, 1.5)
    if N_MICROBATCH > 1:
        for name, axis in PER_EXAMPLE_BATCH_AXIS.items():
            out[name] = _to_microbatches(out[name], axis)
    return {'x': out['x'], 'c': out['c'], 'positions': out['positions'], 'w_ada': out['w_ada'], 'b_ada': out['b_ada'], 'g_norm': out['g_norm'], 'w_in': out['w_in'], 'b_f': out['b_f'], 'sinks': out['sinks'], 'w_o_swa': out['w_o_swa'], 'w_o_fox': out['w_o_fox'], 'w_out': out['w_out'], 'g_final': out['g_final'], 'loss_target': out['loss_target'], 'm_w_ada': out['m_w_ada'], 'm_b_ada': out['m_b_ada'], 'm_g_norm': out['m_g_norm'], 'm_w_in': out['m_w_in'], 'm_b_f': out['m_b_f'], 'm_sinks': out['m_sinks'], 'm_w_o_swa': out['m_w_o_swa'], 'm_w_o_fox': out['m_w_o_fox'], 'm_w_out': out['m_w_out'], 'm_g_final': out['m_g_final'], 'v_w_ada': out['v_w_ada'], 'v_b_ada': out['v_b_ada'], 'v_g_norm': out['v_g_norm'], 'v_w_in': out['v_w_in'], 'v_b_f': out['v_b_f'], 'v_sinks': out['v_sinks'], 'v_w_o_swa': out['v_w_o_swa'], 'v_w_o_fox': out['v_w_o_fox'], 'v_w_out': out['v_w_out'], 'v_g_final': out['v_g_final']}


def _loss(weights, diff, rest, loss_target):
    with _jax.named_scope("forward"):
        args = {**rest, TWIN_DIFF_INPUT: diff, **{k: w.astype(_WEIGHT_DTYPES[k]) for k, w in weights.items()}}
        y = _forward(args)
    with _jax.named_scope("loss_head"):
        err = _jnp.square(y.astype(_jnp.float32) - loss_target)
        return 0.5 * _jnp.sum(_jnp.mean(err, axis=-1)) if err.ndim else 0.5 * err


def _adamw(w, g, m, v):
    m = ADAM_B1 * m + (1.0 - ADAM_B1) * g
    v = ADAM_B2 * v + (1.0 - ADAM_B2) * _jnp.square(g)
    m_hat = m / (1.0 - ADAM_B1 ** ADAM_STEP)
    v_hat = v / (1.0 - ADAM_B2 ** ADAM_STEP)
    delta = -ADAM_LR * (m_hat / (_jnp.sqrt(v_hat) + ADAM_EPS) + ADAM_WD * w)
    return delta, m, v


def reference(x, c, positions, w_ada, b_ada, g_norm, w_in, b_f, sinks, w_o_swa, w_o_fox, w_out, g_final, loss_target, m_w_ada, m_b_ada, m_g_norm, m_w_in, m_b_f, m_sinks, m_w_o_swa, m_w_o_fox, m_w_out, m_g_final, v_w_ada, v_b_ada, v_g_norm, v_w_in, v_b_f, v_sinks, v_w_o_swa, v_w_o_fox, v_w_out, v_g_final):
    given = dict(x=x, c=c, positions=positions, w_ada=w_ada, b_ada=b_ada, g_norm=g_norm, w_in=w_in, b_f=b_f, sinks=sinks, w_o_swa=w_o_swa, w_o_fox=w_o_fox, w_out=w_out, g_final=g_final, loss_target=loss_target, m_w_ada=m_w_ada, m_b_ada=m_b_ada, m_g_norm=m_g_norm, m_w_in=m_w_in, m_b_f=m_b_f, m_sinks=m_sinks, m_w_o_swa=m_w_o_swa, m_w_o_fox=m_w_o_fox, m_w_out=m_w_out, m_g_final=m_g_final, v_w_ada=v_w_ada, v_b_ada=v_b_ada, v_g_norm=v_g_norm, v_w_in=v_w_in, v_b_f=v_b_f, v_sinks=v_sinks, v_w_o_swa=v_w_o_swa, v_w_o_fox=v_w_o_fox, v_w_out=v_w_out, v_g_final=v_g_final)
    weights = {n: given[n] for n in TWIN_WEIGHTS}
    shared = {n: given[n] for n in SHARED_INPUTS}
    per_example = {n: given[n] for n in ['x', 'c', 'positions']}
    grad_fn = _jax.value_and_grad(_loss, argnums=(0, 1))

    def one_microbatch(ex, loss_target):
        ex = dict(ex)
        diff = ex.pop(TWIN_DIFF_INPUT)
        return grad_fn(weights, diff, {**shared, **ex}, loss_target)

    if N_MICROBATCH == 1:
        loss, (grad_w, grad_x) = one_microbatch(per_example, given["loss_target"])
    else:
        def body(carry, xs):
            loss_sum, grad_sum = carry
            l_k, (gw_k, gx_k) = one_microbatch(xs[0], xs[1])
            with _jax.named_scope("update"):
                return (loss_sum + l_k, _jax.tree.map(_jnp.add, grad_sum, gw_k)), gx_k

        init = (_jnp.zeros((), _jnp.float32), _jax.tree.map(_jnp.zeros_like, weights))
        (loss, grad_w), grad_x = _jax.lax.scan(body, init, (per_example, given["loss_target"]))
    with _jax.named_scope("update"):
        delta_w, new_m, new_v = {}, {}, {}
        for n in TWIN_WEIGHTS:
            delta_w[n], new_m[n], new_v[n] = _adamw(weights[n], grad_w[n], given["m_" + n], given["v_" + n])
    return (loss, grad_x, *[grad_w[n] for n in TWIN_WEIGHTS], *[delta_w[n] for n in TWIN_WEIGHTS],
            *[new_m[n] for n in TWIN_WEIGHTS], *[new_v[n] for n in TWIN_WEIGHTS])
```

```python
import numpy as np
import jax
import jax.numpy as jnp
from jax import lax
from jax.experimental import pallas as pl
from jax.experimental.pallas import tpu as pltpu

F32 = jnp.float32
BF16 = jnp.bfloat16

D_MODEL = 1024
HEAD_DIM = 64
N_HEADS = 8
KV_GROUPS = 2
GROUP = N_HEADS // KV_GROUPS
WINDOW = 128
CHUNK = 128
AUG_W = N_HEADS * CHUNK
N_DEV = 8
IN_SHARD = 673
ADA_SHARD = 384
NORM_EPS = 1e-6
ROPE_THETA = 10000.0
Q_SCALE = HEAD_DIM ** -0.5
NEG = -1e30

ADAM_LR = 0.001
ADAM_B1 = 0.9
ADAM_B2 = 0.999
ADAM_EPS = 1e-08
ADAM_WD = 0.01
ADAM_STEP = 10
ADAM_C1 = 1.0 - ADAM_B1 ** ADAM_STEP
ADAM_C2 = 1.0 - ADAM_B2 ** ADAM_STEP

SEG = {}
_off = 0
for _name, _w in (("qa", 512), ("ka", 256), ("va", 256), ("za", 512), ("qb", 512), ("kb", 512),
                  ("vb", 512), ("zb", 512), ("ga", 1024), ("gb", 1024), ("f", 128)):
    SEG[_name] = (_off, _w)
    _off += _w
PROJ_W = _off

VMEM_LIMIT = 56 * 1024 * 1024


def _params(n_axes):
    return pltpu.CompilerParams(dimension_semantics=("arbitrary",) * n_axes, vmem_limit_bytes=VMEM_LIMIT)


def _dot(a, b):
    return jnp.dot(a, b, preferred_element_type=F32)


def _dot_nt(a, b):
    return lax.dot_general(a, b, (((1,), (1,)), ((), ())), preferred_element_type=F32)


def _dot_tn(a, b):
    return lax.dot_general(a, b, (((0,), (0,)), ((), ())), preferred_element_type=F32)


def _lane(n):
    return lax.broadcasted_iota(jnp.int32, (1, n), 1)


def _split3(x):
    hi = x.astype(BF16).astype(F32)
    r = x - hi
    mid = r.astype(BF16).astype(F32)
    lo = (r - mid).astype(BF16).astype(F32)
    return hi, mid, lo


def _dot_split(x, b):
    hi, mid, lo = _split3(x)
    return _dot(hi.astype(BF16), b) + _dot(mid.astype(BF16), b) + _dot(lo.astype(BF16), b)


def _place3(lane, base, x, other):
    hi, mid, lo = _split3(x)
    return jnp.where(lane == base, hi, jnp.where(lane == base + 1, mid, jnp.where(lane == base + 2, lo, other)))


def _lane_sum(x, lo, hi):
    lane = _lane(x.shape[1])
    return jnp.sum(jnp.where((lane >= lo) & (lane < hi), x, 0.0), axis=1, keepdims=True)


def _ones_lanes(lo, hi):
    lane = _lane(CHUNK)
    return jnp.where((lane >= lo) & (lane < hi), 1.0, 0.0).astype(F32)


def _rope(c, cos, sa, sb):
    return c * cos + pltpu.roll(c, CHUNK - 32, 1) * sa + pltpu.roll(c, 32, 1) * sb


def _rope_inv(d, cos, sa, sb):
    return d * cos - (pltpu.roll(d, CHUNK - 32, 1) * sa + pltpu.roll(d, 32, 1) * sb)


def _pair(c0, c1):
    return jnp.where(_lane(CHUNK) < HEAD_DIM, c0, pltpu.roll(c1, HEAD_DIM, 1))


def _sigmoid(x):
    return 1.0 / (1.0 + jnp.exp(-x))


def _exchange(arrays, gather, name):
    n = len(arrays)
    out_shape = []
    for a, g in zip(arrays, gather):
        shp = (N_DEV,) + a.shape if g else a.shape
        out_shape.append(jax.ShapeDtypeStruct(shp, a.dtype))

    def body(*refs):
        srcs, dsts = refs[:n], refs[n:2 * n]
        send_sems, recv_sems, loc_sems = refs[2 * n:]
        x, y, c = lax.axis_index("x"), lax.axis_index("y"), lax.axis_index("c")
        me = 4 * x + 2 * y + c

        def src_of(i, dev):
            return srcs[i] if gather[i] else srcs[i].at[dev]

        local = [pltpu.make_async_copy(src_of(i, me), dsts[i].at[me], loc_sems.at[i]) for i in range(n)]
        for cp in local:
            cp.start()
        sends, recvs = [], []
        for r in (1, 2, 4, 3, 5, 6, 7):
            px = 1 - x if r & 4 else x
            py = 1 - y if r & 2 else y
            pc = 1 - c if r & 1 else c
            pid = 4 * px + 2 * py + pc
            for i in range(n):
                sends.append(pltpu.make_async_remote_copy(
                    src_ref=src_of(i, pid), dst_ref=dsts[i].at[me],
                    send_sem=send_sems.at[i, r - 1], recv_sem=recv_sems.at[i, r - 1],
                    device_id=(px, py, pc), device_id_type=pl.DeviceIdType.MESH))
                recvs.append(pltpu.make_async_remote_copy(
                    src_ref=src_of(i, pid), dst_ref=dsts[i].at[pid],
                    send_sem=send_sems.at[i, r - 1], recv_sem=recv_sems.at[i, r - 1],
                    device_id=(px, py, pc), device_id_type=pl.DeviceIdType.MESH))
        for cp in sends:
            cp.start()
        for cp in recvs:
            cp.wait_recv()
        for cp in sends:
            cp.wait_send()
        for cp in local:
            cp.wait()

    any_spec = pl.BlockSpec(memory_space=pl.ANY)
    return pl.pallas_call(
        body, name=name, out_shape=out_shape,
        in_specs=[any_spec] * n, out_specs=[any_spec] * n,
        scratch_shapes=[pltpu.SemaphoreType.DMA((n, N_DEV - 1)), pltpu.SemaphoreType.DMA((n, N_DEV - 1)),
                        pltpu.SemaphoreType.DMA((n,))],
    )(*arrays)


def _ada_fwd(c_all, w_shard, b_shard):
    def body(c_ref, w_ref, b_ref, o_ref):
        ch, cm, cl = [t.astype(BF16) for t in _split3(c_ref[...])]
        wh, wm, wl = [t.astype(BF16) for t in _split3(w_ref[...])]
        acc = _dot(ch, wh) + _dot(ch, wm) + _dot(cm, wh) + _dot(ch, wl) + _dot(cl, wh) + _dot(cm, wm)
        o_ref[...] = acc + b_ref[...]

    return pl.pallas_call(body, name="ada_fwd", out_shape=jax.ShapeDtypeStruct((N_DEV, ADA_SHARD), F32),
                          compiler_params=_params(0))(c_all, w_shard, b_shard)


def _rope_tables(pos_col, inv_freq, tm):
    s = pos_col.shape[0]

    def body(p_ref, f_ref, cos_ref, sa_ref, sb_ref):
        ang = p_ref[...] * f_ref[...]
        sin = jnp.sin(ang)
        first_half = (_lane(CHUNK) & (HEAD_DIM - 1)) < HEAD_DIM // 2
        cos_ref[...] = jnp.cos(ang)
        sa_ref[...] = jnp.where(first_half, -sin, 0.0)
        sb_ref[...] = jnp.where(first_half, 0.0, sin)

    tab = jax.ShapeDtypeStruct((s, CHUNK), F32)
    blk = pl.BlockSpec((tm, CHUNK), lambda i: (i, 0))
    return pl.pallas_call(
        body, name="rope_tables", grid=(s // tm,), out_shape=[tab, tab, tab],
        in_specs=[pl.BlockSpec((tm, 1), lambda i: (i, 0)), pl.BlockSpec((1, CHUNK), lambda i: (0, 0))],
        out_specs=[blk, blk, blk], compiler_params=_params(1))(pos_col, inv_freq)


def _fwd_proj(x, mod, w_all, cos, sa, sb, bf_row, emat, tm):
    s = x.shape[0]

    def body(x_ref, mod_ref, w_ref, cos_ref, sa_ref, sb_ref, bf_ref, e_ref,
             h_ref, qa_ref, ka_ref, va_ref, za_ref, qb_ref, kb_ref, vb_ref, zb_ref, ga_ref, gb_ref, f_ref,
             carry_ref):
        @pl.when(pl.program_id(0) == 0)
        def _():
            carry_ref[...] = jnp.zeros_like(carry_ref)

        xv = x_ref[...]
        rstd = lax.rsqrt(jnp.mean(xv * xv, axis=-1, keepdims=True) + NORM_EPS)
        h = (xv * rstd * mod_ref[0:1, :]) * (1.0 + mod_ref[1:2, :]) + mod_ref[2:3, :]
        hb = h.astype(BF16)
        h_ref[...] = hb

        def seg(name):
            off, w = SEG[name]
            return _dot(hb, w_ref[:, off:off + w])

        lane = _lane(CHUNK)
        low = lane < HEAD_DIM
        q_ones = _ones_lanes(64, 67)
        k_ones = _ones_lanes(67, 70)
        cos_t, sa_t, sb_t = cos_ref[...], sa_ref[...], sb_ref[...]

        def write_heads(ref, nat, extra_of, rope, scale):
            for p in range(N_HEADS // 2):
                c = nat[:, CHUNK * p:CHUNK * (p + 1)]
                if rope:
                    c = _rope(c, cos_t, sa_t, sb_t)
                if scale != 1.0:
                    c = c * scale
                for hh in range(2):
                    hd = 2 * p + hh
                    src = c if hh == 0 else pltpu.roll(c, HEAD_DIM, 1)
                    ref[:, CHUNK * hd:CHUNK * (hd + 1)] = jnp.where(low, src, extra_of(hd)).astype(BF16)

        write_heads(qa_ref, seg("qa"), lambda hd: q_ones, True, Q_SCALE)
        ka = seg("ka")
        va = seg("va")
        for kv in range(KV_GROUPS):
            sl = slice(CHUNK * kv, CHUNK * (kv + 1))
            ka_ref[:, sl] = jnp.where(low, _rope(ka[:, sl], cos_t, sa_t, sb_t), k_ones).astype(BF16)
            va_ref[:, sl] = jnp.where(low, va[:, sl], q_ones).astype(BF16)
        za_ref[...] = seg("za")
        zb_ref[...] = seg("zb")
        ga_ref[...] = seg("ga")
        gb_ref[...] = seg("gb")

        xf = seg("f") + bf_ref[...]
        f_ref[...] = xf
        logf = jnp.minimum(xf, 0.0) - jnp.log1p(jnp.exp(-jnp.abs(xf)))
        row = lax.broadcasted_iota(jnp.int32, (tm, tm), 0)
        col = lax.broadcasted_iota(jnp.int32, (tm, tm), 1)
        tri = jnp.where(col <= row, 1.0, 0.0).astype(BF16)
        hi, mid, lo = _split3(logf)
        cum = _dot(tri, hi.astype(BF16)) + _dot(tri, mid.astype(BF16)) + _dot(tri, lo.astype(BF16))
        cum = cum + carry_ref[0:1, :]
        carry_ref[0:1, :] += jnp.sum(logf, axis=0, keepdims=True)
        spread = _dot_split(cum, e_ref[...])
        lane_all = _lane(AUG_W) & (CHUNK - 1)
        ones_all = jnp.where((lane_all >= 67) & (lane_all < 70), 1.0, 0.0)
        bias = _place3(lane_all, 64, -spread, ones_all)

        write_heads(qb_ref, seg("qb"), lambda hd: q_ones, False, Q_SCALE)
        write_heads(kb_ref, seg("kb"), lambda hd: bias[:, CHUNK * hd:CHUNK * (hd + 1)], False, 1.0)
        write_heads(vb_ref, seg("vb"), lambda hd: q_ones, False, 1.0)

    row_blk = lambda w: pl.BlockSpec((tm, w), lambda i: (i, 0))
    full = lambda a: pl.BlockSpec(a.shape, lambda i: (0,) * a.ndim)
    sds = lambda w, dt: jax.ShapeDtypeStruct((s, w), dt)
    out_shape = [sds(D_MODEL, BF16), sds(AUG_W, BF16), sds(KV_GROUPS * CHUNK, BF16), sds(KV_GROUPS * CHUNK, BF16),
                 sds(512, F32), sds(AUG_W, BF16), sds(AUG_W, BF16), sds(AUG_W, BF16), sds(512, F32),
                 sds(D_MODEL, F32), sds(D_MODEL, F32), sds(CHUNK, F32)]
    return pl.pallas_call(
        body, name="fwd_proj", grid=(s // tm,), out_shape=out_shape,
        in_specs=[row_blk(D_MODEL), full(mod), full(w_all), row_blk(CHUNK), row_blk(CHUNK), row_blk(CHUNK),
                  full(bf_row), full(emat)],
        out_specs=[row_blk(o.shape[1]) for o in out_shape],
        scratch_shapes=[pltpu.VMEM((8, CHUNK), F32)],
        compiler_params=_params(1))(x, mod, w_all, cos, sa, sb, bf_row, emat)


def _swa_fwd(q_aug, k_aug, v_aug, sink_rows, tq):
    s = q_aug.shape[0]
    r = tq // WINDOW
    gw = GROUP * CHUNK

    def body(q_ref, kc_ref, kp_ref, vc_ref, vp_ref, sink_ref, o_ref, qb_ref):
        i = pl.program_id(1)
        row = lax.broadcasted_iota(jnp.int32, (WINDOW, WINDOW), 0)
        col = lax.broadcasted_iota(jnp.int32, (WINDOW, WINDOW), 1)
        lane = _lane(CHUNK)
        for sub in range(r):
            rows = slice(WINDOW * sub, WINDOW * (sub + 1))
            prev = slice(WINDOW * (sub - 1), WINDOW * sub)
            k_c, v_c = kc_ref[rows, :], vc_ref[rows, :]
            k_p = kp_ref[...] if sub == 0 else kc_ref[prev, :]
            v_p = vp_ref[...] if sub == 0 else vc_ref[prev, :]
            valid_c = col <= row
            valid_p = col > row
            if sub == 0:
                valid_p = valid_p & (i > 0)
            outs = []
            for g in range(GROUP):
                cs = slice(CHUNK * g, CHUNK * (g + 1))
                q = q_ref[rows, cs]
                sink = jnp.max(sink_ref[0, g:g + 1, :], axis=1, keepdims=True)
                s_c = jnp.where(valid_c, _dot_nt(q, k_c), NEG)
                s_p = jnp.where(valid_p, _dot_nt(q, k_p), NEG)
                m = jnp.maximum(jnp.maximum(jnp.max(s_c, axis=1, keepdims=True),
                                            jnp.max(s_p, axis=1, keepdims=True)), sink)
                acc = _dot(jnp.exp(s_c - m).astype(BF16), v_c) + _dot(jnp.exp(s_p - m).astype(BF16), v_p)
                denom = _lane_sum(acc, 64, 65) + jnp.exp(sink - m)
                outs.append(acc / denom)
                lse = m + jnp.log(denom)
                qb_ref[rows, cs] = _place3(lane, 67, -lse, q.astype(F32)).astype(BF16)
            for pp in range(GROUP // 2):
                o_ref[rows, CHUNK * pp:CHUNK * (pp + 1)] = _pair(outs[2 * pp], outs[2 * pp + 1])

    return pl.pallas_call(
        body, name="swa_fwd", grid=(KV_GROUPS, s // tq),
        out_shape=[jax.ShapeDtypeStruct((s, 512), F32), jax.ShapeDtypeStruct((s, AUG_W), BF16)],
        in_specs=[pl.BlockSpec((tq, gw), lambda kv, i: (i, kv)),
                  pl.BlockSpec((tq, CHUNK), lambda kv, i: (i, kv)),
                  pl.BlockSpec((WINDOW, CHUNK), lambda kv, i: (jnp.maximum(i * r - 1, 0), kv)),
                  pl.BlockSpec((tq, CHUNK), lambda kv, i: (i, kv)),
                  pl.BlockSpec((WINDOW, CHUNK), lambda kv, i: (jnp.maximum(i * r - 1, 0), kv)),
                  pl.BlockSpec((1, 8, CHUNK), lambda kv, i: (kv, 0, 0))],
        out_specs=[pl.BlockSpec((tq, GROUP * HEAD_DIM), lambda kv, i: (i, kv)),
                   pl.BlockSpec((tq, gw), lambda kv, i: (i, kv))],
        compiler_params=_params(2))(q_aug, k_aug, k_aug, v_aug, v_aug, sink_rows)


def _swa_bwd(k_aug, v_aug, q_bwd, do_aug, cos, sa, sb, sink_rows, tk):
    s = k_aug.shape[0]
    r = tk // WINDOW
    nt = s // tk
    nb = s // WINDOW
    gw = GROUP * CHUNK

    def body(k_ref, v_ref, q_ref, qn_ref, do_ref, don_ref, cos_ref, sa_ref, sb_ref, sink_ref,
             dq_ref, dk_ref, dv_ref, dsink_ref, carry_ref, acc_ref):
        j = pl.program_id(1)

        @pl.when(j == 0)
        def _():
            carry_ref[...] = jnp.zeros_like(carry_ref)
            dsink_ref[...] = jnp.zeros_like(dsink_ref)

        acc_ref[...] = jnp.zeros_like(acc_ref)
        acc_ref[0:WINDOW, :] = carry_ref[...]
        row = lax.broadcasted_iota(jnp.int32, (WINDOW, WINDOW), 0)
        col = lax.broadcasted_iota(jnp.int32, (WINDOW, WINDOW), 1)
        lane = _lane(CHUNK)
        low = lane < HEAD_DIM
        cos_t, sa_t, sb_t = cos_ref[...], sa_ref[...], sb_ref[...]
        for sub in range(r):
            rows = slice(WINDOW * sub, WINDOW * (sub + 1))
            nxt = slice(WINDOW * (sub + 1), WINDOW * (sub + 2))
            k, v = k_ref[rows, :], v_ref[rows, :]
            dk = jnp.zeros((WINDOW, CHUNK), F32)
            dv = jnp.zeros((WINDOW, CHUNK), F32)
            valid_c = row <= col
            valid_n = row > col
            if sub == r - 1:
                valid_n = valid_n & (j < nt - 1)
            for g in range(GROUP):
                cs = slice(CHUNK * g, CHUNK * (g + 1))
                q, do = q_ref[rows, cs], do_ref[rows, cs]
                pt = jnp.exp(jnp.where(valid_c, _dot_nt(k, q), NEG))
                ds = (pt * _dot_nt(v, do)).astype(BF16)
                dv = dv + _dot(pt.astype(BF16), do)
                dk = dk + _dot(ds, q)
                acc_ref[rows, cs] += _dot_tn(ds, k)
                sink = jnp.max(sink_ref[0, g:g + 1, :], axis=1, keepdims=True)
                p_sink = jnp.exp(sink + _lane_sum(q.astype(F32), 67, 70))
                term = jnp.sum(p_sink * _lane_sum(do.astype(F32), 64, 67), axis=0, keepdims=True)
                dsink_ref[0, g:g + 1, :] += jnp.broadcast_to(term, (1, CHUNK))
                if sub < r - 1:
                    qn, don = q_ref[nxt, cs], do_ref[nxt, cs]
                else:
                    qn, don = qn_ref[:, cs], don_ref[:, cs]
                pt = jnp.exp(jnp.where(valid_n, _dot_nt(k, qn), NEG))
                ds = (pt * _dot_nt(v, don)).astype(BF16)
                dv = dv + _dot(pt.astype(BF16), don)
                dk = dk + _dot(ds, qn)
                dqn = _dot_tn(ds, k)
                if sub < r - 1:
                    acc_ref[nxt, cs] += dqn
                else:
                    carry_ref[:, cs] = dqn
            dk_ref[rows, :] = _rope_inv(jnp.where(low, dk, 0.0), cos_t[rows, :], sa_t[rows, :], sb_t[rows, :]).astype(BF16)
            dv_ref[rows, :] = jnp.where(low, dv, 0.0).astype(BF16)
        for pp in range(GROUP // 2):
            d = _pair(acc_ref[:, CHUNK * 2 * pp:CHUNK * (2 * pp + 1)], acc_ref[:, CHUNK * (2 * pp + 1):CHUNK * (2 * pp + 2)])
            dq_ref[:, CHUNK * pp:CHUNK * (pp + 1)] = (_rope_inv(d, cos_t, sa_t, sb_t) * Q_SCALE).astype(BF16)

    cur = lambda w: pl.BlockSpec((tk, w), lambda kv, j: (j, kv))
    nxt = pl.BlockSpec((WINDOW, gw), lambda kv, j: (jnp.minimum((j + 1) * r, nb - 1), kv))
    tab = pl.BlockSpec((tk, CHUNK), lambda kv, j: (j, 0))
    return pl.pallas_call(
        body, name="swa_bwd", grid=(KV_GROUPS, nt),
        out_shape=[jax.ShapeDtypeStruct((s, 512), BF16), jax.ShapeDtypeStruct((s, KV_GROUPS * CHUNK), BF16),
                   jax.ShapeDtypeStruct((s, KV_GROUPS * CHUNK), BF16), jax.ShapeDtypeStruct((KV_GROUPS, 8, CHUNK), F32)],
        in_specs=[cur(CHUNK), cur(CHUNK), cur(gw), nxt, cur(gw), nxt, tab, tab, tab,
                  pl.BlockSpec((1, 8, CHUNK), lambda kv, j: (kv, 0, 0))],
        out_specs=[pl.BlockSpec((tk, GROUP * HEAD_DIM), lambda kv, j: (j, kv)), cur(CHUNK), cur(CHUNK),
                   pl.BlockSpec((1, 8, CHUNK), lambda kv, j: (kv, 0, 0))],
        scratch_shapes=[pltpu.VMEM((WINDOW, gw), F32), pltpu.VMEM((tk, gw), F32)],
        compiler_params=_params(2))(k_aug, v_aug, q_bwd, q_bwd, do_aug, do_aug, cos, sa, sb, sink_rows)


def _fox_fwd(q_aug, k_aug, v_aug, t):
    s = q_aug.shape[0]
    pw = 2 * CHUNK

    def body(q_ref, k_ref, v_ref, o_ref, qb_ref):
        i = pl.program_id(1)
        row = lax.broadcasted_iota(jnp.int32, (t, t), 0)
        col = lax.broadcasted_iota(jnp.int32, (t, t), 1)
        lane = _lane(CHUNK)
        outs = []
        for hh in range(2):
            hs = slice(CHUNK * hh, CHUNK * (hh + 1))
            q = q_ref[:, hs]

            def step(jb, carry, masked):
                m, acc = carry
                rows = pl.ds(pl.multiple_of(jb * t, t), t)
                sc = _dot_nt(q, k_ref[rows, hs])
                if masked:
                    sc = jnp.where(col <= row, sc, NEG)
                m_new = jnp.maximum(m, jnp.max(sc, axis=1, keepdims=True))
                p = jnp.exp(sc - m_new).astype(BF16)
                acc = jnp.exp(m - m_new) * acc + _dot(p, v_ref[rows, hs])
                return m_new, acc

            init = (jnp.full((t, 1), NEG, F32), jnp.zeros((t, CHUNK), F32))
            carry = lax.fori_loop(0, i, lambda jb, c: step(jb, c, False), init)
            m, acc = step(i, carry, True)
            denom = _lane_sum(acc, 64, 65)
            outs.append(acc / denom)
            lse = m + jnp.log(denom)
            qb_ref[:, hs] = _place3(lane, 67, -lse, q.astype(F32)).astype(BF16)
        o_ref[...] = _pair(outs[0], outs[1])

    return pl.pallas_call(
        body, name="fox_fwd", grid=(N_HEADS // 2, s // t),
        out_shape=[jax.ShapeDtypeStruct((s, 512), F32), jax.ShapeDtypeStruct((s, AUG_W), BF16)],
        in_specs=[pl.BlockSpec((t, pw), lambda p, i: (i, p)),
                  pl.BlockSpec((s, pw), lambda p, i: (0, p)),
                  pl.BlockSpec((s, pw), lambda p, i: (0, p))],
        out_specs=[pl.BlockSpec((t, CHUNK), lambda p, i: (i, p)), pl.BlockSpec((t, pw), lambda p, i: (i, p))],
        compiler_params=_params(2))(q_aug, k_aug, v_aug)


def _fox_bwd(k_aug, v_aug, q_bwd, do_aug, t):
    s = k_aug.shape[0]
    n = s // t
    pw = 2 * CHUNK

    def body(k_ref, v_ref, q_ref, do_ref, dk_ref, dv_ref, dck_ref, dq_ref, dcq_ref, dq_scr):
        j = pl.program_id(1)

        @pl.when(j == 0)
        def _():
            dq_scr[...] = jnp.zeros_like(dq_scr)

        row = lax.broadcasted_iota(jnp.int32, (t, t), 0)
        col = lax.broadcasted_iota(jnp.int32, (t, t), 1)
        lane = _lane(CHUNK)
        dks, dvs = [], []
        for hh in range(2):
            hs = slice(CHUNK * hh, CHUNK * (hh + 1))
            k, v = k_ref[:, hs], v_ref[:, hs]

            def step(ib, carry, masked):
                dv, dk = carry
                rows = pl.ds(pl.multiple_of(ib * t, t), t)
                q, do = q_ref[rows, hs], do_ref[rows, hs]
                st = _dot_nt(k, q)
                if masked:
                    st = jnp.where(row <= col, st, NEG)
                pt = jnp.exp(st)
                ds = (pt * _dot_nt(v, do)).astype(BF16)
                dv = dv + _dot(pt.astype(BF16), do)
                dk = dk + _dot(ds, q)
                dq_scr[rows, hs] += _dot_tn(ds, k)
                return dv, dk

            zero = jnp.zeros((t, CHUNK), F32)
            carry = step(j, (zero, zero), True)
            dv, dk = lax.fori_loop(j + 1, n, lambda ib, c: step(ib, c, False), carry)
            dks.append(dk)
            dvs.append(dv)
        dk_ref[...] = _pair(dks[0], dks[1]).astype(BF16)
        dv_ref[...] = _pair(dvs[0], dvs[1]).astype(BF16)
        dck_ref[0] = jnp.where(lane == 0, pltpu.roll(dks[0], 64, 1),
                               jnp.where(lane == 1, pltpu.roll(dks[1], 65, 1), 0.0))

        @pl.when(j == n - 1)
        def _():
            for ib in range(n):
                rows = slice(t * ib, t * (ib + 1))
                d0, d1 = dq_scr[rows, 0:CHUNK], dq_scr[rows, CHUNK:pw]
                dq_ref[rows, :] = (_pair(d0, d1) * Q_SCALE).astype(BF16)
                dcq_ref[0, rows, :] = jnp.where(lane == 0, pltpu.roll(d0, CHUNK - 67, 1),
                                                jnp.where(lane == 1, pltpu.roll(d1, CHUNK - 66, 1), 0.0))

    return pl.pallas_call(
        body, name="fox_bwd", grid=(N_HEADS // 2, n),
        out_shape=[jax.ShapeDtypeStruct((s, 512), BF16), jax.ShapeDtypeStruct((s, 512), BF16),
                   jax.ShapeDtypeStruct((N_HEADS // 2, s, CHUNK), F32), jax.ShapeDtypeStruct((s, 512), BF16),
                   jax.ShapeDtypeStruct((N_HEADS // 2, s, CHUNK), F32)],
        in_specs=[pl.BlockSpec((t, pw), lambda p, j: (j, p)), pl.BlockSpec((t, pw), lambda p, j: (j, p)),
                  pl.BlockSpec((s, pw), lambda p, j: (0, p)), pl.BlockSpec((s, pw), lambda p, j: (0, p))],
        out_specs=[pl.BlockSpec((t, CHUNK), lambda p, j: (j, p)), pl.BlockSpec((t, CHUNK), lambda p, j: (j, p)),
                   pl.BlockSpec((1, t, CHUNK), lambda p, j: (p, j, 0)), pl.BlockSpec((s, CHUNK), lambda p, j: (0, p)),
                   pl.BlockSpec((1, s, CHUNK), lambda p, j: (p, 0, 0))],
        scratch_shapes=[pltpu.VMEM((s, pw), F32)],
        compiler_params=_params(2))(k_aug, v_aug, q_bwd, do_aug)


def _fgate_bwd(dcq, dck, xf, tm):
    s = xf.shape[0]
    nt = s // tm

    def body(dq_ref, dc_ref, xf_ref, df_ref, dbf_ref, carry_ref):
        @pl.when(pl.program_id(0) == 0)
        def _():
            carry_ref[...] = jnp.zeros_like(carry_ref)
            dbf_ref[...] = jnp.zeros_like(dbf_ref)

        row = lax.broadcasted_iota(jnp.int32, (tm, tm), 0)
        col = lax.broadcasted_iota(jnp.int32, (tm, tm), 1)
        tri = jnp.where(col >= row, 1.0, 0.0).astype(BF16)
        dcum = dq_ref[...] - dc_ref[...]
        hi, mid, lo = _split3(dcum)
        dlogf = _dot(tri, hi.astype(BF16)) + _dot(tri, mid.astype(BF16)) + _dot(tri, lo.astype(BF16))
        dlogf = dlogf + carry_ref[0:1, :]
        carry_ref[0:1, :] += jnp.sum(dcum, axis=0, keepdims=True)
        df = dlogf * _sigmoid(-xf_ref[...])
        df_ref[...] = df.astype(BF16)
        dbf_ref[0:1, :] += jnp.sum(df, axis=0, keepdims=True)

    rev = pl.BlockSpec((tm, CHUNK), lambda i: (nt - 1 - i, 0))
    return pl.pallas_call(
        body, name="fgate_bwd", grid=(nt,),
        out_shape=[jax.ShapeDtypeStruct((s, CHUNK), BF16), jax.ShapeDtypeStruct((8, CHUNK), F32)],
        in_specs=[rev, rev, rev], out_specs=[rev, pl.BlockSpec((8, CHUNK), lambda i: (0, 0))],
        scratch_shapes=[pltpu.VMEM((8, CHUNK), F32)],
        compiler_params=_params(1))(dcq, dck, xf)


def _post(x, tgt, att_a, att_b, za, zb, ga, gb, woa, wob, wout, woa_t, wob_t, wout_t, vec, ind, emat, tm):
    s = x.shape[0]

    def body(x_ref, t_ref, aa_ref, ab_ref, za_ref, zb_ref, ga_ref, gb_ref, woa_ref, wob_ref, wout_ref,
             woat_ref, wobt_ref, woutt_ref, vec_ref, ind_ref, e_ref,
             dx1_ref, ua_ref, ub_ref, mg_ref, do_ref, dya_ref, dyb_ref, doa_ref, dob_ref,
             dza_ref, dzb_ref, dga_ref, dgb_ref, acc_ref):
        @pl.when(pl.program_id(0) == 0)
        def _():
            acc_ref[...] = jnp.zeros_like(acc_ref)

        gate, gfin = vec_ref[0:1, :], vec_ref[1:2, :]
        inv_d = 1.0 / D_MODEL

        def branch_fwd(att_ref, z_ref, w_ref, u_ref):
            att, z = att_ref[...], z_ref[...]
            sz = _sigmoid(z)
            silu = z * sz
            u = (att * silu).astype(BF16)
            u_ref[...] = u
            return att, z, sz, silu, _dot(u, w_ref[...])

        att_a, z_a, sz_a, silu_a, y_a = branch_fwd(aa_ref, za_ref, woa_ref, ua_ref)
        att_b, z_b, sz_b, silu_b, y_b = branch_fwd(ab_ref, zb_ref, wob_ref, ub_ref)
        sg_a, sg_b = _sigmoid(ga_ref[...]), _sigmoid(gb_ref[...])
        merged = (sg_a * y_a + sg_b * y_b).astype(BF16)
        mg_ref[...] = merged
        o = _dot(merged, wout_ref[...])
        x1 = x_ref[...] + gate * o
        rstd = lax.rsqrt(jnp.mean(x1 * x1, axis=-1, keepdims=True) + NORM_EPS)
        xh = x1 * rstd
        diff = xh * gfin - t_ref[...]
        acc_ref[2:3, :] += (0.5 * inv_d) * jnp.sum(diff * diff, axis=0, keepdims=True)
        dy = diff * inv_d
        acc_ref[1:2, :] += jnp.sum(dy * xh, axis=0, keepdims=True)
        dyg = dy * gfin
        dx1 = rstd * (dyg - xh * jnp.mean(dyg * xh, axis=-1, keepdims=True))
        dx1_ref[...] = dx1
        acc_ref[0:1, :] += jnp.sum(dx1 * o, axis=0, keepdims=True)
        d_o = (dx1 * gate).astype(BF16)
        do_ref[...] = d_o
        dmg = _dot(d_o, woutt_ref[...])

        lane = _lane(CHUNK)
        low = lane < HEAD_DIM
        lane_all = _lane(AUG_W) & (CHUNK - 1)

        def branch_bwd(sg, y, att, z, sz, silu, wt_ref, dy_ref, dg_ref, dz_ref, doaug_ref):
            dyb = (dmg * sg).astype(BF16)
            dy_ref[...] = dyb
            dg_ref[...] = (dmg * y * sg * (1.0 - sg)).astype(BF16)
            du = _dot(dyb, wt_ref[...])
            datt = du * silu
            dz_ref[...] = (du * att * (sz * (1.0 + z * (1.0 - sz)))).astype(BF16)
            delta = _dot_split(datt * att, ind_ref[...])
            extra = _place3(lane_all, 64, -_dot_split(delta, e_ref[...]), 0.0)
            for p in range(N_HEADS // 2):
                c = datt[:, CHUNK * p:CHUNK * (p + 1)]
                for hh in range(2):
                    hd = 2 * p + hh
                    src = c if hh == 0 else pltpu.roll(c, HEAD_DIM, 1)
                    doaug_ref[:, CHUNK * hd:CHUNK * (hd + 1)] = jnp.where(
                        low, src, extra[:, CHUNK * hd:CHUNK * (hd + 1)]).astype(BF16)

        branch_bwd(sg_a, y_a, att_a, z_a, sz_a, silu_a, woat_ref, dya_ref, dga_ref, dza_ref, doa_ref)
        branch_bwd(sg_b, y_b, att_b, z_b, sz_b, silu_b, wobt_ref, dyb_ref, dgb_ref, dzb_ref, dob_ref)

    row_blk = lambda w: pl.BlockSpec((tm, w), lambda i: (i, 0))
    full = lambda a: pl.BlockSpec(a.shape, lambda i: (0,) * a.ndim)
    sds = lambda w, dt: jax.ShapeDtypeStruct((s, w), dt)
    out_shape = [sds(D_MODEL, F32), sds(512, BF16), sds(512, BF16), sds(D_MODEL, BF16), sds(D_MODEL, BF16),
                 sds(D_MODEL, BF16), sds(D_MODEL, BF16), sds(AUG_W, BF16), sds(AUG_W, BF16),
                 sds(512, BF16), sds(512, BF16), sds(D_MODEL, BF16), sds(D_MODEL, BF16),
                 jax.ShapeDtypeStruct((8, D_MODEL), F32)]
    ins = [x, tgt, att_a, att_b, za, zb, ga, gb, woa, wob, wout, woa_t, wob_t, wout_t, vec, ind, emat]
    in_specs = [row_blk(a.shape[1]) for a in ins[:8]] + [full(a) for a in ins[8:]]
    out_specs = [row_blk(o.shape[1]) for o in out_shape[:-1]] + [pl.BlockSpec((8, D_MODEL), lambda i: (0, 0))]
    return pl.pallas_call(body, name="post", grid=(s // tm,), out_shape=out_shape, in_specs=in_specs,
                          out_specs=out_specs, compiler_params=_params(1))(*ins)


def _bwd_pre(dproj, w_all_t, x, dx1, mod, tm):
    s = x.shape[0]

    def body(dp_ref, wt_ref, x_ref, dx1_ref, mod_ref, gx_ref, acc_ref):
        @pl.when(pl.program_id(0) == 0)
        def _():
            acc_ref[...] = jnp.zeros_like(acc_ref)

        dh = _dot(dp_ref[...], wt_ref[...])
        xv = x_ref[...]
        rstd = lax.rsqrt(jnp.mean(xv * xv, axis=-1, keepdims=True) + NORM_EPS)
        xh = xv * rstd
        gn = mod_ref[0:1, :]
        acc_ref[0:1, :] += jnp.sum(dh, axis=0, keepdims=True)
        acc_ref[1:2, :] += jnp.sum(dh * (xh * gn), axis=0, keepdims=True)
        dn = dh * (1.0 + mod_ref[1:2, :])
        acc_ref[2:3, :] += jnp.sum(dn * xh, axis=0, keepdims=True)
        dxh = dn * gn
        gx_ref[...] = dx1_ref[...] + rstd * (dxh - xh * jnp.mean(dxh * xh, axis=-1, keepdims=True))

    row_blk = lambda w: pl.BlockSpec((tm, w), lambda i: (i, 0))
    full = lambda a: pl.BlockSpec(a.shape, lambda i: (0,) * a.ndim)
    return pl.pallas_call(
        body, name="bwd_pre", grid=(s // tm,),
        out_shape=[jax.ShapeDtypeStruct((s, D_MODEL), F32), jax.ShapeDtypeStruct((8, D_MODEL), F32)],
        in_specs=[row_blk(PROJ_W), full(w_all_t), row_blk(D_MODEL), row_blk(D_MODEL), full(mod)],
        out_specs=[row_blk(D_MODEL), pl.BlockSpec((8, D_MODEL), lambda i: (0, 0))],
        compiler_params=_params(1))(dproj, w_all_t, x, dx1, mod)


def _tn_matmul(a, b, tn, tk, name):
    s, m = a.shape
    n = b.shape[1]

    def body(a_ref, b_ref, o_ref):
        @pl.when(pl.program_id(1) == 0)
        def _():
            o_ref[...] = jnp.zeros_like(o_ref)

        o_ref[...] += _dot_tn(a_ref[...], b_ref[...])

    return pl.pallas_call(
        body, name=name, grid=(n // tn, s // tk), out_shape=jax.ShapeDtypeStruct((m, n), F32),
        in_specs=[pl.BlockSpec((tk, m), lambda jn, k: (k, 0)), pl.BlockSpec((tk, tn), lambda jn, k: (k, jn))],
        out_specs=pl.BlockSpec((m, tn), lambda jn, k: (0, jn)),
        compiler_params=_params(2))(a, b)


def _adam_math(g, w, m, v):
    m2 = ADAM_B1 * m + (1.0 - ADAM_B1) * g
    v2 = ADAM_B2 * v + (1.0 - ADAM_B2) * (g * g)
    delta = -ADAM_LR * ((m2 / ADAM_C1) / (jnp.sqrt(v2 / ADAM_C2) + ADAM_EPS) + ADAM_WD * w)
    return delta, m2, v2


def _adamw_parts(parts, w, m, v, tr, name):
    rws, cols = w.shape

    def body(p_ref, w_ref, m_ref, v_ref, g_ref, d_ref, m2_ref, v2_ref):
        g = p_ref[0]
        for dev in range(1, N_DEV):
            g = g + p_ref[dev]
        g_ref[...] = g
        d_ref[...], m2_ref[...], v2_ref[...] = _adam_math(g, w_ref[...], m_ref[...], v_ref[...])

    blk = pl.BlockSpec((tr, cols), lambda i: (i, 0))
    o = jax.ShapeDtypeStruct((rws, cols), F32)
    return pl.pallas_call(
        body, name=name, grid=(rws // tr,), out_shape=[o, o, o, o],
        in_specs=[pl.BlockSpec((N_DEV, tr, cols), lambda i: (0, i, 0)), blk, blk, blk],
        out_specs=[blk, blk, blk, blk], compiler_params=_params(1))(parts, w, m, v)


def _adamw_ada(c_lanes, d_rows, w, m, v, tr):
    rws, cols = w.shape

    def body(c_ref, d_ref, w_ref, m_ref, v_ref, g_ref, dl_ref, m2_ref, v2_ref):
        for k in range(cols // CHUNK):
            cs = slice(CHUNK * k, CHUNK * (k + 1))
            g = c_ref[0] * d_ref[0:1, cs]
            for b in range(1, N_DEV):
                g = g + c_ref[b] * d_ref[b:b + 1, cs]
            g_ref[:, cs] = g
            dl_ref[:, cs], m2_ref[:, cs], v2_ref[:, cs] = _adam_math(g, w_ref[:, cs], m_ref[:, cs], v_ref[:, cs])

    blk = pl.BlockSpec((tr, cols), lambda i: (i, 0))
    o = jax.ShapeDtypeStruct((rws, cols), F32)
    return pl.pallas_call(
        body, name="adamw_ada", grid=(rws // tr,), out_shape=[o, o, o, o],
        in_specs=[pl.BlockSpec((N_DEV, tr, CHUNK), lambda i: (0, i, 0)), pl.BlockSpec((N_DEV, cols), lambda i: (0, 0)),
                  blk, blk, blk],
        out_specs=[blk, blk, blk, blk], compiler_params=_params(1))(c_lanes, d_rows, w, m, v)


def _adamw_small(packs, w, m, v):
    def body(p_ref, w_ref, m_ref, v_ref, g_ref, d_ref, m2_ref, v2_ref, loss_ref):
        g = p_ref[0]
        for dev in range(1, N_DEV):
            g = g + p_ref[dev]
        g_ref[...] = g
        d_ref[...], m2_ref[...], v2_ref[...] = _adam_math(g, w_ref[...], m_ref[...], v_ref[...])
        loss_ref[...] = jnp.broadcast_to(jnp.sum(g_ref[2:3, 0:D_MODEL], axis=1, keepdims=True), loss_ref.shape)

    o = jax.ShapeDtypeStruct(w.shape, F32)
    return pl.pallas_call(body, name="adamw_small", out_shape=[o, o, o, o, jax.ShapeDtypeStruct((8, CHUNK), F32)],
                          compiler_params=_params(0))(packs, w, m, v)


def _tile(s, want):
    return min(s, want)


def kernel(x, c, positions, w_ada, b_ada, g_norm, w_in, b_f, sinks, w_o_swa, w_o_fox, w_out, g_final, loss_target, m_w_ada, m_b_ada, m_g_norm, m_w_in, m_b_f, m_sinks, m_w_o_swa, m_w_o_fox, m_w_out, m_g_final, v_w_ada, v_b_ada, v_g_norm, v_w_in, v_b_f, v_sinks, v_w_o_swa, v_w_o_fox, v_w_out, v_g_final):
    s = x.shape[1]
    tm = _tile(s, 256)
    ta = _tile(s, 512)
    me = 4 * lax.axis_index("x") + 2 * lax.axis_index("y") + lax.axis_index("c")
    x2, tgt = x[0], loss_target[0]

    (c_all,) = _exchange([jnp.broadcast_to(c, (8, D_MODEL))], [True], "gather_c")
    c_all = c_all[:, 0, :]
    b_shard = lax.dynamic_slice(b_ada, (0, me * ADA_SHARD), (1, ADA_SHARD))
    ada_part = _ada_fwd(c_all, w_ada[0], b_shard)
    ada_g, win_g, woa_g, wob_g, wout_g = _exchange(
        [ada_part, w_in[0].astype(BF16), w_o_swa[0].astype(BF16), w_o_fox[0].astype(BF16), w_out[0].astype(BF16)],
        [True] * 5, "gather_weights")
    ada = lax.dynamic_index_in_dim(ada_g, me, axis=1, keepdims=False).reshape(3 * D_MODEL)
    shift, scale, gate = ada[:D_MODEL], ada[D_MODEL:2 * D_MODEL], ada[2 * D_MODEL:]
    w_in_full = win_g.transpose(1, 0, 2).reshape(D_MODEL, N_DEV * IN_SHARD)
    woa = woa_g.transpose(1, 0, 2).reshape(512, D_MODEL)
    wob = wob_g.transpose(1, 0, 2).reshape(512, D_MODEL)
    wout = wout_g.reshape(D_MODEL, D_MODEL)

    splits = np.cumsum([512, 128, 128, 512, 512, 512, 512, 8, 512, 1024])
    wqa, wka, wva, wza, wqb, wkb, wvb, wfb, wzb, wga, wgb = jnp.split(w_in_full, splits, axis=1)
    pad_kv = lambda w: jnp.pad(w.reshape(D_MODEL, KV_GROUPS, HEAD_DIM), ((0, 0), (0, 0), (0, CHUNK - HEAD_DIM))).reshape(D_MODEL, KV_GROUPS * CHUNK)
    w_all = jnp.concatenate([wqa, pad_kv(wka), pad_kv(wva), wza, wqb, wkb, wvb, wzb, wga, wgb,
                             jnp.pad(wfb, ((0, 0), (0, CHUNK - N_HEADS)))], axis=1)

    zrow = jnp.zeros((1, D_MODEL), F32)
    mod = jnp.concatenate([g_norm, scale[None], shift[None], zrow, zrow, zrow, zrow, zrow], axis=0)
    bf_row = jnp.pad(b_f, ((0, 0), (0, CHUNK - N_HEADS)))
    emat_np = np.zeros((CHUNK, AUG_W), np.float32)
    ind_np = np.zeros((512, CHUNK), np.float32)
    for hd in range(N_HEADS):
        emat_np[hd, CHUNK * hd + 64:CHUNK * hd + 67] = 1.0
        ind_np[HEAD_DIM * hd:HEAD_DIM * (hd + 1), hd] = 1.0
    emat, ind = jnp.asarray(emat_np, BF16), jnp.asarray(ind_np, BF16)
    inv_freq = np.power(np.float32(ROPE_THETA), -np.arange(0, HEAD_DIM, 2, dtype=np.float32) / HEAD_DIM)
    inv_freq = jnp.asarray(np.tile(inv_freq, CHUNK // (HEAD_DIM // 2))[None, :], F32)
    cos, sa, sb = _rope_tables(positions.reshape(s, 1).astype(F32), inv_freq, tm)
    sink_rows = jnp.broadcast_to(jnp.pad(sinks.reshape(KV_GROUPS, GROUP), ((0, 0), (0, 8 - GROUP)))[:, :, None],
                                 (KV_GROUPS, 8, CHUNK))

    (h, qa, ka, va, za, qb, kb, vb, zb, ga, gb, xf) = _fwd_proj(x2, mod, w_all, cos, sa, sb, bf_row, emat, tm)
    att_a, qa_bwd = _swa_fwd(qa, ka, va, sink_rows, ta)
    att_b, qb_bwd = _fox_fwd(qb, kb, vb, ta)
    vec = jnp.concatenate([gate[None], g_final[None], zrow, zrow, zrow, zrow, zrow, zrow], axis=0)
    (dx1, ua, ub, merged, d_o, dya, dyb, doa, dob, dza, dzb, dga, dgb, post_acc) = _post(
        x2, tgt, att_a, att_b, za, zb, ga, gb, woa, wob, wout, woa.T, wob.T, wout.T, vec, ind, emat, tm)

    dqa, dka, dva, dsink = _swa_bwd(ka, va, qa_bwd, doa, cos, sa, sb, sink_rows, ta)
    dkb, dvb, dck, dqb, dcq = _fox_bwd(kb, vb, qb_bwd, dob, ta)
    by_head = lambda a: jnp.pad(a[:, :, :2].transpose(1, 0, 2).reshape(s, N_HEADS), ((0, 0), (0, CHUNK - N_HEADS)))
    dfb, dbf_acc = _fgate_bwd(by_head(dcq), by_head(dck), xf, tm)
    dproj = jnp.concatenate([dqa, dka, dva, dza, dqb, dkb, dvb, dzb, dga, dgb, dfb], axis=1)
    grad_x, pre_acc = _bwd_pre(dproj, w_all.T, x2, dx1, mod, tm)
    dw_all = _tn_matmul(h, dproj, 1152, ta, "dw_in")
    dwoa = _tn_matmul(ua, dya, 512, ta, "dw_o_swa")
    dwob = _tn_matmul(ub, dyb, 512, ta, "dw_o_fox")
    dwout = _tn_matmul(merged, d_o, 512, ta, "dw_out")

    take = lambda name: dw_all[:, SEG[name][0]:SEG[name][0] + SEG[name][1]]
    unpad_kv = lambda g: g.reshape(D_MODEL, KV_GROUPS, CHUNK)[:, :, :HEAD_DIM].reshape(D_MODEL, KV_GROUPS * HEAD_DIM)
    dw_in_full = jnp.concatenate([take("qa"), unpad_kv(take("ka")), unpad_kv(take("va")), take("za"), take("qb"),
                                  take("kb"), take("vb"), take("f")[:, :N_HEADS], take("zb"), take("ga"), take("gb")], axis=1)

    misc = jnp.concatenate([dbf_acc[0, :N_HEADS], dsink[:, :GROUP, 0].reshape(N_HEADS), jnp.zeros((D_MODEL - 16,), F32)])
    zpad = jnp.zeros((2 * D_MODEL,), F32)
    pack = jnp.stack([jnp.concatenate([pre_acc[0], pre_acc[1], post_acc[0]]),
                      jnp.concatenate([pre_acc[2], post_acc[1], misc]),
                      jnp.concatenate([post_acc[2], zpad])] + [jnp.zeros((3 * D_MODEL,), F32)] * 5)
    p_win, p_woa, p_wob, p_wout, packs = _exchange(
        [dw_in_full.reshape(D_MODEL, N_DEV, IN_SHARD).transpose(1, 0, 2),
         dwoa.reshape(512, N_DEV, 128).transpose(1, 0, 2), dwob.reshape(512, N_DEV, 128).transpose(1, 0, 2),
         dwout.reshape(N_DEV, 128, D_MODEL), pack],
        [False, False, False, False, True], "exchange_grads")

    g_win, d_win, m_win, v_win = _adamw_parts(p_win, w_in[0], m_w_in[0], v_w_in[0], 128, "adamw_w_in")
    g_woa, d_woa, m_woa, v_woa = _adamw_parts(p_woa, w_o_swa[0], m_w_o_swa[0], v_w_o_swa[0], 128, "adamw_w_o_swa")
    g_wob, d_wob, m_wob, v_wob = _adamw_parts(p_wob, w_o_fox[0], m_w_o_fox[0], v_w_o_fox[0], 128, "adamw_w_o_fox")
    g_wout, d_wout, m_wout, v_wout = _adamw_parts(p_wout, w_out[0], m_w_out[0], v_w_out[0], 128, "adamw_w_out")
    d_ada_rows = lax.dynamic_slice(packs[:, 0, :], (0, me * ADA_SHARD), (N_DEV, ADA_SHARD))
    c_lanes = jnp.broadcast_to(c_all[:, :, None], (N_DEV, D_MODEL, CHUNK))
    g_wada, d_wada, m_wada, v_wada = _adamw_ada(c_lanes, d_ada_rows, w_ada[0], m_w_ada[0], v_w_ada[0], 256)

    def small_pack(bada, gn, gf, bfv, sk):
        misc_w = jnp.concatenate([bfv[0], sk[0], jnp.zeros((D_MODEL - 16,), F32)])
        return jnp.stack([bada[0], jnp.concatenate([gn[0], gf, misc_w])] + [jnp.zeros((3 * D_MODEL,), F32)] * 6)

    sm = _adamw_small(packs, small_pack(b_ada, g_norm, g_final, b_f, sinks),
                      small_pack(m_b_ada, m_g_norm, m_g_final, m_b_f, m_sinks),
                      small_pack(v_b_ada, v_g_norm, v_g_final, v_b_f, v_sinks))
    loss = sm[4][0, 0]

    def unpack(p):
        return (p[0][None], p[1, :D_MODEL][None], p[1, 2 * D_MODEL:2 * D_MODEL + 8][None],
                p[1, 2 * D_MODEL + 8:2 * D_MODEL + 16][None], p[1, D_MODEL:2 * D_MODEL])

    outs = []
    for big, small in (((g_wada, g_win, g_woa, g_wob, g_wout), sm[0]), ((d_wada, d_win, d_woa, d_wob, d_wout), sm[1]),
                       ((m_wada, m_win, m_woa, m_wob, m_wout), sm[2]), ((v_wada, v_win, v_woa, v_wob, v_wout), sm[3])):
        wada_o, win_o, woa_o, wob_o, wout_o = big
        bada_o, gn_o, bf_o, sk_o, gf_o = unpack(small)
        outs += [wada_o[None], bada_o, gn_o, win_o[None], bf_o, sk_o, woa_o[None], wob_o[None], wout_o[None], gf_o]
    return (loss, grad_x[None], *outs)
```

```python
import numpy as np
import jax
import jax.numpy as jnp
from jax import lax
from jax.experimental import pallas as pl
from jax.experimental.pallas import tpu as pltpu

F32 = jnp.float32
BF16 = jnp.bfloat16

D_MODEL = 1024
HEAD_DIM = 64
N_HEADS = 8
KV_GROUPS = 2
GROUP = N_HEADS // KV_GROUPS
WINDOW = 128
CHUNK = 128
AUG_W = N_HEADS * CHUNK
N_DEV = 8
IN_SHARD = 673
ADA_SHARD = 384
NORM_EPS = 1e-6
ROPE_THETA = 10000.0
Q_SCALE = HEAD_DIM ** -0.5
NEG = -1e30

ADAM_LR = 0.001
ADAM_B1 = 0.9
ADAM_B2 = 0.999
ADAM_EPS = 1e-08
ADAM_WD = 0.01
ADAM_STEP = 10
ADAM_C1 = 1.0 - ADAM_B1 ** ADAM_STEP
ADAM_C2 = 1.0 - ADAM_B2 ** ADAM_STEP

SEG = {}
_off = 0
for _name, _w in (("qa", 512), ("ka", 256), ("va", 256), ("za", 512), ("qb", 512), ("kb", 512),
                  ("vb", 512), ("zb", 512), ("ga", 1024), ("gb", 1024), ("f", 128)):
    SEG[_name] = (_off, _w)
    _off += _w
PROJ_W = _off

VMEM_LIMIT = 56 * 1024 * 1024


def _params(n_axes):
    return pltpu.CompilerParams(dimension_semantics=("arbitrary",) * n_axes, vmem_limit_bytes=VMEM_LIMIT)


def _dot(a, b):
    return jnp.dot(a, b, preferred_element_type=F32)


def _dot_nt(a, b):
    return lax.dot_general(a, b, (((1,), (1,)), ((), ())), preferred_element_type=F32)


def _dot_tn(a, b):
    return lax.dot_general(a, b, (((0,), (0,)), ((), ())), preferred_element_type=F32)


def _lane(n):
    return lax.broadcasted_iota(jnp.int32, (1, n), 1)


def _split3(x):
    hi = x.astype(BF16).astype(F32)
    r = x - hi
    mid = r.astype(BF16).astype(F32)
    lo = (r - mid).astype(BF16).astype(F32)
    return hi, mid, lo


def _dot_split(x, b):
    hi, mid, lo = _split3(x)
    return _dot(hi.astype(BF16), b) + _dot(mid.astype(BF16), b) + _dot(lo.astype(BF16), b)


def _place3(lane, base, x, other):
    hi, mid, lo = _split3(x)
    return jnp.where(lane == base, hi, jnp.where(lane == base + 1, mid, jnp.where(lane == base + 2, lo, other)))


def _lane_sum(x, lo, hi):
    lane = _lane(x.shape[1])
    return jnp.sum(jnp.where((lane >= lo) & (lane < hi), x, 0.0), axis=1, keepdims=True)


def _ones_lanes(lo, hi):
    lane = _lane(CHUNK)
    return jnp.where((lane >= lo) & (lane < hi), 1.0, 0.0).astype(F32)


def _rope(c, cos, sa, sb):
    return c * cos + pltpu.roll(c, CHUNK - 32, 1) * sa + pltpu.roll(c, 32, 1) * sb


def _rope_inv(d, cos, sa, sb):
    return d * cos - (pltpu.roll(d, CHUNK - 32, 1) * sa + pltpu.roll(d, 32, 1) * sb)


def _pair(c0, c1):
    return jnp.where(_lane(CHUNK) < HEAD_DIM, c0, pltpu.roll(c1, HEAD_DIM, 1))


def _sigmoid(x):
    return 1.0 / (1.0 + jnp.exp(-x))


N_CHIPS = 4
ANY_SPEC = pl.BlockSpec(memory_space=pl.ANY)


def _exchange(arrays, modes, name):
    n = len(arrays)
    out_shape = [jax.ShapeDtypeStruct((N_DEV,) + a.shape if md == "gather" else a.shape, a.dtype)
                 for a, md in zip(arrays, modes)]

    def body(*refs):
        srcs, dsts = refs[:n], refs[n:2 * n]
        send_sems, recv_sems, loc_sems = refs[2 * n:]
        x, y, c = lax.axis_index("x"), lax.axis_index("y"), lax.axis_index("c")

        def slot(i, px, py, pc):
            return 4 * px + 2 * py + pc if modes[i] == "gather" else 2 * px + py

        def src_of(i, px, py, pc):
            return srcs[i] if modes[i] == "gather" else srcs[i].at[slot(i, px, py, pc)]

        local = [pltpu.make_async_copy(src_of(i, x, y, c), dsts[i].at[slot(i, x, y, c)], loc_sems.at[i])
                 for i in range(n)]
        for cp in local:
            cp.start()
        sends, recvs = [], []
        for r in (1, 2, 4, 3, 5, 6, 7):
            px = 1 - x if r & 4 else x
            py = 1 - y if r & 2 else y
            pc = 1 - c if r & 1 else c
            for i in range(n):
                if modes[i] == "chips" and r & 1:
                    continue
                sems = dict(send_sem=send_sems.at[i, r - 1], recv_sem=recv_sems.at[i, r - 1],
                            device_id=(px, py, pc), device_id_type=pl.DeviceIdType.MESH)
                sends.append(pltpu.make_async_remote_copy(
                    src_ref=src_of(i, px, py, pc), dst_ref=dsts[i].at[slot(i, x, y, c)], **sems))
                recvs.append(pltpu.make_async_remote_copy(
                    src_ref=src_of(i, px, py, pc), dst_ref=dsts[i].at[slot(i, px, py, pc)], **sems))
        for cp in sends:
            cp.start()
        for cp in recvs:
            cp.wait_recv()
        for cp in sends:
            cp.wait_send()
        for cp in local:
            cp.wait()

    return pl.pallas_call(
        body, name=name, out_shape=out_shape,
        in_specs=[ANY_SPEC] * n, out_specs=[ANY_SPEC] * n,
        scratch_shapes=[pltpu.SemaphoreType.DMA((n, N_DEV - 1)), pltpu.SemaphoreType.DMA((n, N_DEV - 1)),
                        pltpu.SemaphoreType.DMA((n,))],
    )(*arrays)


def _gather_two_level(arrays, name):
    n = len(arrays)
    out_shape = [jax.ShapeDtypeStruct((N_DEV,) + a.shape, a.dtype) for a in arrays]

    def body(*refs):
        srcs, dsts = refs[:n], refs[n:2 * n]
        send_sems, recv_sems, loc_sems = refs[2 * n:]
        x, y, c = lax.axis_index("x"), lax.axis_index("y"), lax.axis_index("c")
        me, sibling = (x, y, c), (x, y, 1 - c)
        chips = [(1 - x, y), (x, 1 - y), (1 - x, 1 - y)]

        def rows(i, dev):
            return dsts[i].at[4 * dev[0] + 2 * dev[1] + dev[2]]

        def copy(i, k, block, to, src=None):
            return pltpu.make_async_remote_copy(
                src_ref=rows(i, block) if src is None else src, dst_ref=rows(i, block),
                send_sem=send_sems.at[i, k], recv_sem=recv_sems.at[i, k],
                device_id=to, device_id_type=pl.DeviceIdType.MESH)

        local = [pltpu.make_async_copy(srcs[i], rows(i, me), loc_sems.at[i]) for i in range(n)]
        for cp in local:
            cp.start()
        first = []
        for i in range(n):
            first.append(copy(i, 0, me, sibling, src=srcs[i]))
            first += [copy(i, 1 + j, me, (*chip, c), src=srcs[i]) for j, chip in enumerate(chips)]
        for cp in first:
            cp.start()
        passed = []
        for j, chip in enumerate(chips):
            for i in range(n):
                copy(i, 1 + j, (*chip, c), me).wait_recv()
                cp = copy(i, 4 + j, (*chip, c), sibling)
                cp.start()
                passed.append(cp)
        for i in range(n):
            copy(i, 0, sibling, me).wait_recv()
            for j, chip in enumerate(chips):
                copy(i, 4 + j, (*chip, 1 - c), me).wait_recv()
        for cp in first + passed:
            cp.wait_send()
        for cp in local:
            cp.wait()

    return pl.pallas_call(
        body, name=name, out_shape=out_shape,
        in_specs=[ANY_SPEC] * n, out_specs=[ANY_SPEC] * n,
        scratch_shapes=[pltpu.SemaphoreType.DMA((n, N_DEV - 1)), pltpu.SemaphoreType.DMA((n, N_DEV - 1)),
                        pltpu.SemaphoreType.DMA((n,))],
    )(*arrays)


def _swap_sibling(arrays, name):
    n = len(arrays)
    out_shape = [jax.ShapeDtypeStruct(a.shape[1:], a.dtype) for a in arrays]

    def body(*refs):
        srcs, dsts = refs[:n], refs[n:2 * n]
        send_sems, recv_sems = refs[2 * n:]
        x, y, c = lax.axis_index("x"), lax.axis_index("y"), lax.axis_index("c")
        copies = [pltpu.make_async_remote_copy(
            src_ref=srcs[i].at[1 - c], dst_ref=dsts[i], send_sem=send_sems.at[i], recv_sem=recv_sems.at[i],
            device_id=(x, y, 1 - c), device_id_type=pl.DeviceIdType.MESH) for i in range(n)]
        for cp in copies:
            cp.start()
        for cp in copies:
            cp.wait()

    return pl.pallas_call(
        body, name=name, out_shape=out_shape, in_specs=[ANY_SPEC] * n, out_specs=[ANY_SPEC] * n,
        scratch_shapes=[pltpu.SemaphoreType.DMA((n,)), pltpu.SemaphoreType.DMA((n,))],
    )(*arrays)


def _chip_partial(mine, theirs, tr, name):
    k, rws, cols = mine.shape

    def body(a_ref, b_ref, f_ref, h_ref):
        sm = a_ref[...] + b_ref[...]
        f_ref[...] = sm
        h_ref[...] = sm.astype(BF16)

    blk = pl.BlockSpec((1, tr, cols), lambda j, i: (j, i, 0))
    return pl.pallas_call(
        body, name=name, grid=(k, rws // tr),
        out_shape=[jax.ShapeDtypeStruct(mine.shape, F32), jax.ShapeDtypeStruct(mine.shape, BF16)],
        in_specs=[blk, blk], out_specs=[blk, blk], compiler_params=_params(2))(mine, theirs)


def _ada_fwd(c_all, w_shard, b_shard):
    def body(c_ref, w_ref, b_ref, o_ref):
        ch, cm, cl = [t.astype(BF16) for t in _split3(c_ref[...])]
        wh, wm, wl = [t.astype(BF16) for t in _split3(w_ref[...])]
        acc = _dot(ch, wh) + _dot(ch, wm) + _dot(cm, wh) + _dot(ch, wl) + _dot(cl, wh) + _dot(cm, wm)
        o_ref[...] = acc + b_ref[...]

    return pl.pallas_call(body, name="ada_fwd", out_shape=jax.ShapeDtypeStruct((N_DEV, ADA_SHARD), F32),
                          compiler_params=_params(0))(c_all, w_shard, b_shard)


def _rope_tables(pos_col, inv_freq, tm):
    s = pos_col.shape[0]

    def body(p_ref, f_ref, cos_ref, sa_ref, sb_ref):
        ang = p_ref[...] * f_ref[...]
        sin = jnp.sin(ang)
        first_half = (_lane(CHUNK) & (HEAD_DIM - 1)) < HEAD_DIM // 2
        cos_ref[...] = jnp.cos(ang)
        sa_ref[...] = jnp.where(first_half, -sin, 0.0)
        sb_ref[...] = jnp.where(first_half, 0.0, sin)

    tab = jax.ShapeDtypeStruct((s, CHUNK), F32)
    blk = pl.BlockSpec((tm, CHUNK), lambda i: (i, 0))
    return pl.pallas_call(
        body, name="rope_tables", grid=(s // tm,), out_shape=[tab, tab, tab],
        in_specs=[pl.BlockSpec((tm, 1), lambda i: (i, 0)), pl.BlockSpec((1, CHUNK), lambda i: (0, 0))],
        out_specs=[blk, blk, blk], compiler_params=_params(1))(pos_col, inv_freq)


def _fwd_proj(x, mod, w_all, cos, sa, sb, bf_row, emat, tm):
    s = x.shape[0]

    def body(x_ref, mod_ref, w_ref, cos_ref, sa_ref, sb_ref, bf_ref, e_ref,
             h_ref, qa_ref, ka_ref, va_ref, za_ref, qb_ref, kb_ref, vb_ref, zb_ref, ga_ref, gb_ref, f_ref,
             carry_ref):
        @pl.when(pl.program_id(0) == 0)
        def _():
            carry_ref[...] = jnp.zeros_like(carry_ref)

        xv = x_ref[...]
        rstd = lax.rsqrt(jnp.mean(xv * xv, axis=-1, keepdims=True) + NORM_EPS)
        h = (xv * rstd * mod_ref[0:1, :]) * (1.0 + mod_ref[1:2, :]) + mod_ref[2:3, :]
        hb = h.astype(BF16)
        h_ref[...] = hb

        def seg(name):
            off, w = SEG[name]
            return _dot(hb, w_ref[:, off:off + w])

        lane = _lane(CHUNK)
        low = lane < HEAD_DIM
        q_ones = _ones_lanes(64, 67)
        k_ones = _ones_lanes(67, 70)
        cos_t, sa_t, sb_t = cos_ref[...], sa_ref[...], sb_ref[...]

        def write_heads(ref, nat, extra_of, rope, scale):
            for p in range(N_HEADS // 2):
                c = nat[:, CHUNK * p:CHUNK * (p + 1)]
                if rope:
                    c = _rope(c, cos_t, sa_t, sb_t)
                if scale != 1.0:
                    c = c * scale
                for hh in range(2):
                    hd = 2 * p + hh
                    src = c if hh == 0 else pltpu.roll(c, HEAD_DIM, 1)
                    ref[:, CHUNK * hd:CHUNK * (hd + 1)] = jnp.where(low, src, extra_of(hd)).astype(BF16)

        write_heads(qa_ref, seg("qa"), lambda hd: q_ones, True, Q_SCALE)
        ka = seg("ka")
        va = seg("va")
        for kv in range(KV_GROUPS):
            sl = slice(CHUNK * kv, CHUNK * (kv + 1))
            ka_ref[:, sl] = jnp.where(low, _rope(ka[:, sl], cos_t, sa_t, sb_t), k_ones).astype(BF16)
            va_ref[:, sl] = jnp.where(low, va[:, sl], q_ones).astype(BF16)
        za_ref[...] = seg("za")
        zb_ref[...] = seg("zb")
        ga_ref[...] = seg("ga")
        gb_ref[...] = seg("gb")

        xf = seg("f") + bf_ref[...]
        f_ref[...] = xf
        logf = jnp.minimum(xf, 0.0) - jnp.log1p(jnp.exp(-jnp.abs(xf)))
        row = lax.broadcasted_iota(jnp.int32, (tm, tm), 0)
        col = lax.broadcasted_iota(jnp.int32, (tm, tm), 1)
        tri = jnp.where(col <= row, 1.0, 0.0).astype(BF16)
        hi, mid, lo = _split3(logf)
        cum = _dot(tri, hi.astype(BF16)) + _dot(tri, mid.astype(BF16)) + _dot(tri, lo.astype(BF16))
        cum = cum + carry_ref[0:1, :]
        carry_ref[0:1, :] += jnp.sum(logf, axis=0, keepdims=True)
        spread = _dot_split(cum, e_ref[...])
        lane_all = _lane(AUG_W) & (CHUNK - 1)
        ones_all = jnp.where((lane_all >= 67) & (lane_all < 70), 1.0, 0.0)
        bias = _place3(lane_all, 64, -spread, ones_all)

        write_heads(qb_ref, seg("qb"), lambda hd: q_ones, False, Q_SCALE)
        write_heads(kb_ref, seg("kb"), lambda hd: bias[:, CHUNK * hd:CHUNK * (hd + 1)], False, 1.0)
        write_heads(vb_ref, seg("vb"), lambda hd: q_ones, False, 1.0)

    row_blk = lambda w: pl.BlockSpec((tm, w), lambda i: (i, 0))
    full = lambda a: pl.BlockSpec(a.shape, lambda i: (0,) * a.ndim)
    sds = lambda w, dt: jax.ShapeDtypeStruct((s, w), dt)
    out_shape = [sds(D_MODEL, BF16), sds(AUG_W, BF16), sds(KV_GROUPS * CHUNK, BF16), sds(KV_GROUPS * CHUNK, BF16),
                 sds(512, F32), sds(AUG_W, BF16), sds(AUG_W, BF16), sds(AUG_W, BF16), sds(512, F32),
                 sds(D_MODEL, F32), sds(D_MODEL, F32), sds(CHUNK, F32)]
    return pl.pallas_call(
        body, name="fwd_proj", grid=(s // tm,), out_shape=out_shape,
        in_specs=[row_blk(D_MODEL), full(mod), full(w_all), row_blk(CHUNK), row_blk(CHUNK), row_blk(CHUNK),
                  full(bf_row), full(emat)],
        out_specs=[row_blk(o.shape[1]) for o in out_shape],
        scratch_shapes=[pltpu.VMEM((8, CHUNK), F32)],
        compiler_params=_params(1))(x, mod, w_all, cos, sa, sb, bf_row, emat)


def _swa_fwd(q_aug, k_aug, v_aug, sink_rows, tq):
    s = q_aug.shape[0]
    r = tq // WINDOW
    gw = GROUP * CHUNK

    def body(q_ref, kc_ref, kp_ref, vc_ref, vp_ref, sink_ref, o_ref, qb_ref):
        i = pl.program_id(1)
        row = lax.broadcasted_iota(jnp.int32, (WINDOW, WINDOW), 0)
        col = lax.broadcasted_iota(jnp.int32, (WINDOW, WINDOW), 1)
        lane = _lane(CHUNK)
        for sub in range(r):
            rows = slice(WINDOW * sub, WINDOW * (sub + 1))
            prev = slice(WINDOW * (sub - 1), WINDOW * sub)
            k_c, v_c = kc_ref[rows, :], vc_ref[rows, :]
            k_p = kp_ref[...] if sub == 0 else kc_ref[prev, :]
            v_p = vp_ref[...] if sub == 0 else vc_ref[prev, :]
            valid_c = col <= row
            valid_p = col > row
            if sub == 0:
                valid_p = valid_p & (i > 0)
            outs = []
            for g in range(GROUP):
                cs = slice(CHUNK * g, CHUNK * (g + 1))
                q = q_ref[rows, cs]
                sink = jnp.max(sink_ref[0, g:g + 1, :], axis=1, keepdims=True)
                s_c = jnp.where(valid_c, _dot_nt(q, k_c), NEG)
                s_p = jnp.where(valid_p, _dot_nt(q, k_p), NEG)
                m = jnp.maximum(jnp.maximum(jnp.max(s_c, axis=1, keepdims=True),
                                            jnp.max(s_p, axis=1, keepdims=True)), sink)
                acc = _dot(jnp.exp(s_c - m).astype(BF16), v_c) + _dot(jnp.exp(s_p - m).astype(BF16), v_p)
                denom = _lane_sum(acc, 64, 65) + jnp.exp(sink - m)
                outs.append(acc / denom)
                lse = m + jnp.log(denom)
                qb_ref[rows, cs] = _place3(lane, 67, -lse, q.astype(F32)).astype(BF16)
            for pp in range(GROUP // 2):
                o_ref[rows, CHUNK * pp:CHUNK * (pp + 1)] = _pair(outs[2 * pp], outs[2 * pp + 1])

    return pl.pallas_call(
        body, name="swa_fwd", grid=(KV_GROUPS, s // tq),
        out_shape=[jax.ShapeDtypeStruct((s, 512), F32), jax.ShapeDtypeStruct((s, AUG_W), BF16)],
        in_specs=[pl.BlockSpec((tq, gw), lambda kv, i: (i, kv)),
                  pl.BlockSpec((tq, CHUNK), lambda kv, i: (i, kv)),
                  pl.BlockSpec((WINDOW, CHUNK), lambda kv, i: (jnp.maximum(i * r - 1, 0), kv)),
                  pl.BlockSpec((tq, CHUNK), lambda kv, i: (i, kv)),
                  pl.BlockSpec((WINDOW, CHUNK), lambda kv, i: (jnp.maximum(i * r - 1, 0), kv)),
                  pl.BlockSpec((1, 8, CHUNK), lambda kv, i: (kv, 0, 0))],
        out_specs=[pl.BlockSpec((tq, GROUP * HEAD_DIM), lambda kv, i: (i, kv)),
                   pl.BlockSpec((tq, gw), lambda kv, i: (i, kv))],
        compiler_params=_params(2))(q_aug, k_aug, k_aug, v_aug, v_aug, sink_rows)


def _swa_bwd(k_aug, v_aug, q_bwd, do_aug, cos, sa, sb, sink_rows, tk):
    s = k_aug.shape[0]
    r = tk // WINDOW
    nt = s // tk
    nb = s // WINDOW
    gw = GROUP * CHUNK

    def body(k_ref, v_ref, q_ref, qn_ref, do_ref, don_ref, cos_ref, sa_ref, sb_ref, sink_ref,
             dq_ref, dk_ref, dv_ref, dsink_ref, carry_ref, acc_ref):
        j = pl.program_id(1)

        @pl.when(j == 0)
        def _():
            carry_ref[...] = jnp.zeros_like(carry_ref)
            dsink_ref[...] = jnp.zeros_like(dsink_ref)

        acc_ref[...] = jnp.zeros_like(acc_ref)
        acc_ref[0:WINDOW, :] = carry_ref[...]
        row = lax.broadcasted_iota(jnp.int32, (WINDOW, WINDOW), 0)
        col = lax.broadcasted_iota(jnp.int32, (WINDOW, WINDOW), 1)
        lane = _lane(CHUNK)
        low = lane < HEAD_DIM
        cos_t, sa_t, sb_t = cos_ref[...], sa_ref[...], sb_ref[...]
        for sub in range(r):
            rows = slice(WINDOW * sub, WINDOW * (sub + 1))
            nxt = slice(WINDOW * (sub + 1), WINDOW * (sub + 2))
            k, v = k_ref[rows, :], v_ref[rows, :]
            dk = jnp.zeros((WINDOW, CHUNK), F32)
            dv = jnp.zeros((WINDOW, CHUNK), F32)
            valid_c = row <= col
            valid_n = row > col
            if sub == r - 1:
                valid_n = valid_n & (j < nt - 1)
            for g in range(GROUP):
                cs = slice(CHUNK * g, CHUNK * (g + 1))
                q, do = q_ref[rows, cs], do_ref[rows, cs]
                pt = jnp.exp(jnp.where(valid_c, _dot_nt(k, q), NEG))
                ds = (pt * _dot_nt(v, do)).astype(BF16)
                dv = dv + _dot(pt.astype(BF16), do)
                dk = dk + _dot(ds, q)
                acc_ref[rows, cs] += _dot_tn(ds, k)
                sink = jnp.max(sink_ref[0, g:g + 1, :], axis=1, keepdims=True)
                p_sink = jnp.exp(sink + _lane_sum(q.astype(F32), 67, 70))
                term = jnp.sum(p_sink * _lane_sum(do.astype(F32), 64, 67), axis=0, keepdims=True)
                dsink_ref[0, g:g + 1, :] += jnp.broadcast_to(term, (1, CHUNK))
                if sub < r - 1:
                    qn, don = q_ref[nxt, cs], do_ref[nxt, cs]
                else:
                    qn, don = qn_ref[:, cs], don_ref[:, cs]
                pt = jnp.exp(jnp.where(valid_n, _dot_nt(k, qn), NEG))
                ds = (pt * _dot_nt(v, don)).astype(BF16)
                dv = dv + _dot(pt.astype(BF16), don)
                dk = dk + _dot(ds, qn)
                dqn = _dot_tn(ds, k)
                if sub < r - 1:
                    acc_ref[nxt, cs] += dqn
                else:
                    carry_ref[:, cs] = dqn
            dk_ref[rows, :] = _rope_inv(jnp.where(low, dk, 0.0), cos_t[rows, :], sa_t[rows, :], sb_t[rows, :]).astype(BF16)
            dv_ref[rows, :] = jnp.where(low, dv, 0.0).astype(BF16)
        for pp in range(GROUP // 2):
            d = _pair(acc_ref[:, CHUNK * 2 * pp:CHUNK * (2 * pp + 1)], acc_ref[:, CHUNK * (2 * pp + 1):CHUNK * (2 * pp + 2)])
            dq_ref[:, CHUNK * pp:CHUNK * (pp + 1)] = (_rope_inv(d, cos_t, sa_t, sb_t) * Q_SCALE).astype(BF16)

    cur = lambda w: pl.BlockSpec((tk, w), lambda kv, j: (j, kv))
    nxt = pl.BlockSpec((WINDOW, gw), lambda kv, j: (jnp.minimum((j + 1) * r, nb - 1), kv))
    tab = pl.BlockSpec((tk, CHUNK), lambda kv, j: (j, 0))
    return pl.pallas_call(
        body, name="swa_bwd", grid=(KV_GROUPS, nt),
        out_shape=[jax.ShapeDtypeStruct((s, 512), BF16), jax.ShapeDtypeStruct((s, KV_GROUPS * CHUNK), BF16),
                   jax.ShapeDtypeStruct((s, KV_GROUPS * CHUNK), BF16), jax.ShapeDtypeStruct((KV_GROUPS, 8, CHUNK), F32)],
        in_specs=[cur(CHUNK), cur(CHUNK), cur(gw), nxt, cur(gw), nxt, tab, tab, tab,
                  pl.BlockSpec((1, 8, CHUNK), lambda kv, j: (kv, 0, 0))],
        out_specs=[pl.BlockSpec((tk, GROUP * HEAD_DIM), lambda kv, j: (j, kv)), cur(CHUNK), cur(CHUNK),
                   pl.BlockSpec((1, 8, CHUNK), lambda kv, j: (kv, 0, 0))],
        scratch_shapes=[pltpu.VMEM((WINDOW, gw), F32), pltpu.VMEM((tk, gw), F32)],
        compiler_params=_params(2))(k_aug, v_aug, q_bwd, q_bwd, do_aug, do_aug, cos, sa, sb, sink_rows)


def _fox_fwd(q_aug, k_aug, v_aug, t):
    s = q_aug.shape[0]
    pw = 2 * CHUNK

    def body(q_ref, k_ref, v_ref, o_ref, qb_ref):
        i = pl.program_id(1)
        row = lax.broadcasted_iota(jnp.int32, (t, t), 0)
        col = lax.broadcasted_iota(jnp.int32, (t, t), 1)
        lane = _lane(CHUNK)
        heads = (slice(0, CHUNK), slice(CHUNK, pw))
        qs = [q_ref[:, hs] for hs in heads]

        def step(jb, carry, masked):
            rows = pl.ds(pl.multiple_of(jb * t, t), t)
            new = []
            for hh, hs in enumerate(heads):
                m, acc = carry[2 * hh], carry[2 * hh + 1]
                sc = _dot_nt(qs[hh], k_ref[rows, hs])
                if masked:
                    sc = jnp.where(col <= row, sc, NEG)
                m_new = jnp.maximum(m, jnp.max(sc, axis=1, keepdims=True))
                p = jnp.exp(sc - m_new).astype(BF16)
                new += [m_new, jnp.exp(m - m_new) * acc + _dot(p, v_ref[rows, hs])]
            return tuple(new)

        init = (jnp.full((t, 1), NEG, F32), jnp.zeros((t, CHUNK), F32)) * 2
        carry = step(i, lax.fori_loop(0, i, lambda jb, c: step(jb, c, False), init), True)
        outs = []
        for hh, hs in enumerate(heads):
            m, acc = carry[2 * hh], carry[2 * hh + 1]
            denom = _lane_sum(acc, 64, 65)
            outs.append(acc / denom)
            lse = m + jnp.log(denom)
            qb_ref[:, hs] = _place3(lane, 67, -lse, qs[hh].astype(F32)).astype(BF16)
        o_ref[...] = _pair(outs[0], outs[1])

    return pl.pallas_call(
        body, name="fox_fwd", grid=(N_HEADS // 2, s // t),
        out_shape=[jax.ShapeDtypeStruct((s, 512), F32), jax.ShapeDtypeStruct((s, AUG_W), BF16)],
        in_specs=[pl.BlockSpec((t, pw), lambda p, i: (i, p)),
                  pl.BlockSpec((s, pw), lambda p, i: (0, p)),
                  pl.BlockSpec((s, pw), lambda p, i: (0, p))],
        out_specs=[pl.BlockSpec((t, CHUNK), lambda p, i: (i, p)), pl.BlockSpec((t, pw), lambda p, i: (i, p))],
        compiler_params=_params(2))(q_aug, k_aug, v_aug)


def _fox_bwd(k_aug, v_aug, q_bwd, do_aug, t):
    s = k_aug.shape[0]
    n = s // t
    pw = 2 * CHUNK

    def body(k_ref, v_ref, q_ref, do_ref, dk_ref, dv_ref, dck_ref, dq_ref, dcq_ref, dq_scr):
        j = pl.program_id(1)

        @pl.when(j == 0)
        def _():
            dq_scr[...] = jnp.zeros_like(dq_scr)

        row = lax.broadcasted_iota(jnp.int32, (t, t), 0)
        col = lax.broadcasted_iota(jnp.int32, (t, t), 1)
        lane = _lane(CHUNK)
        heads = (slice(0, CHUNK), slice(CHUNK, pw))
        ks = [k_ref[:, hs] for hs in heads]
        vs = [v_ref[:, hs] for hs in heads]

        def step(ib, carry, masked):
            rows = pl.ds(pl.multiple_of(ib * t, t), t)
            new = []
            for hh, hs in enumerate(heads):
                dv, dk = carry[2 * hh], carry[2 * hh + 1]
                q, do = q_ref[rows, hs], do_ref[rows, hs]
                st = _dot_nt(ks[hh], q)
                if masked:
                    st = jnp.where(row <= col, st, NEG)
                pt = jnp.exp(st)
                ds = (pt * _dot_nt(vs[hh], do)).astype(BF16)
                new += [dv + _dot(pt.astype(BF16), do), dk + _dot(ds, q)]
                dq_scr[rows, hs] += _dot_tn(ds, ks[hh])
            return tuple(new)

        zero = jnp.zeros((t, CHUNK), F32)
        carry = lax.fori_loop(j + 1, n, lambda ib, c: step(ib, c, False), step(j, (zero,) * 4, True))
        dvs, dks = (carry[0], carry[2]), (carry[1], carry[3])
        dk_ref[...] = _pair(dks[0], dks[1]).astype(BF16)
        dv_ref[...] = _pair(dvs[0], dvs[1]).astype(BF16)
        dck_ref[0] = jnp.where(lane == 0, pltpu.roll(dks[0], 64, 1),
                               jnp.where(lane == 1, pltpu.roll(dks[1], 65, 1), 0.0))

        @pl.when(j == n - 1)
        def _():
            for ib in range(n):
                rows = slice(t * ib, t * (ib + 1))
                d0, d1 = dq_scr[rows, 0:CHUNK], dq_scr[rows, CHUNK:pw]
                dq_ref[rows, :] = (_pair(d0, d1) * Q_SCALE).astype(BF16)
                dcq_ref[0, rows, :] = jnp.where(lane == 0, pltpu.roll(d0, CHUNK - 67, 1),
                                                jnp.where(lane == 1, pltpu.roll(d1, CHUNK - 66, 1), 0.0))

    return pl.pallas_call(
        body, name="fox_bwd", grid=(N_HEADS // 2, n),
        out_shape=[jax.ShapeDtypeStruct((s, 512), BF16), jax.ShapeDtypeStruct((s, 512), BF16),
                   jax.ShapeDtypeStruct((N_HEADS // 2, s, CHUNK), F32), jax.ShapeDtypeStruct((s, 512), BF16),
                   jax.ShapeDtypeStruct((N_HEADS // 2, s, CHUNK), F32)],
        in_specs=[pl.BlockSpec((t, pw), lambda p, j: (j, p)), pl.BlockSpec((t, pw), lambda p, j: (j, p)),
                  pl.BlockSpec((s, pw), lambda p, j: (0, p)), pl.BlockSpec((s, pw), lambda p, j: (0, p))],
        out_specs=[pl.BlockSpec((t, CHUNK), lambda p, j: (j, p)), pl.BlockSpec((t, CHUNK), lambda p, j: (j, p)),
                   pl.BlockSpec((1, t, CHUNK), lambda p, j: (p, j, 0)), pl.BlockSpec((s, CHUNK), lambda p, j: (0, p)),
                   pl.BlockSpec((1, s, CHUNK), lambda p, j: (p, 0, 0))],
        scratch_shapes=[pltpu.VMEM((s, pw), F32)],
        compiler_params=_params(2))(k_aug, v_aug, q_bwd, do_aug)


def _fgate_bwd(dcq, dck, xf, tm):
    s = xf.shape[0]
    nt = s // tm

    def body(dq_ref, dc_ref, xf_ref, df_ref, dbf_ref, carry_ref):
        @pl.when(pl.program_id(0) == 0)
        def _():
            carry_ref[...] = jnp.zeros_like(carry_ref)
            dbf_ref[...] = jnp.zeros_like(dbf_ref)

        row = lax.broadcasted_iota(jnp.int32, (tm, tm), 0)
        col = lax.broadcasted_iota(jnp.int32, (tm, tm), 1)
        tri = jnp.where(col >= row, 1.0, 0.0).astype(BF16)
        dcum = dq_ref[...] - dc_ref[...]
        hi, mid, lo = _split3(dcum)
        dlogf = _dot(tri, hi.astype(BF16)) + _dot(tri, mid.astype(BF16)) + _dot(tri, lo.astype(BF16))
        dlogf = dlogf + carry_ref[0:1, :]
        carry_ref[0:1, :] += jnp.sum(dcum, axis=0, keepdims=True)
        df = dlogf * _sigmoid(-xf_ref[...])
        df_ref[...] = df.astype(BF16)
        dbf_ref[0:1, :] += jnp.sum(df, axis=0, keepdims=True)

    rev = pl.BlockSpec((tm, CHUNK), lambda i: (nt - 1 - i, 0))
    return pl.pallas_call(
        body, name="fgate_bwd", grid=(nt,),
        out_shape=[jax.ShapeDtypeStruct((s, CHUNK), BF16), jax.ShapeDtypeStruct((8, CHUNK), F32)],
        in_specs=[rev, rev, rev], out_specs=[rev, pl.BlockSpec((8, CHUNK), lambda i: (0, 0))],
        scratch_shapes=[pltpu.VMEM((8, CHUNK), F32)],
        compiler_params=_params(1))(dcq, dck, xf)


def _post(x, tgt, att_a, att_b, za, zb, ga, gb, woa, wob, wout, woa_t, wob_t, wout_t, vec, ind, emat, tm):
    s = x.shape[0]

    def body(x_ref, t_ref, aa_ref, ab_ref, za_ref, zb_ref, ga_ref, gb_ref, woa_ref, wob_ref, wout_ref,
             woat_ref, wobt_ref, woutt_ref, vec_ref, ind_ref, e_ref,
             dx1_ref, ua_ref, ub_ref, mg_ref, do_ref, dya_ref, dyb_ref, doa_ref, dob_ref,
             dza_ref, dzb_ref, dga_ref, dgb_ref, acc_ref):
        @pl.when(pl.program_id(0) == 0)
        def _():
            acc_ref[...] = jnp.zeros_like(acc_ref)

        gate, gfin = vec_ref[0:1, :], vec_ref[1:2, :]
        inv_d = 1.0 / D_MODEL

        def branch_fwd(att_ref, z_ref, w_ref, u_ref):
            att, z = att_ref[...], z_ref[...]
            sz = _sigmoid(z)
            silu = z * sz
            u = (att * silu).astype(BF16)
            u_ref[...] = u
            return att, z, sz, silu, _dot(u, w_ref[...])

        att_a, z_a, sz_a, silu_a, y_a = branch_fwd(aa_ref, za_ref, woa_ref, ua_ref)
        att_b, z_b, sz_b, silu_b, y_b = branch_fwd(ab_ref, zb_ref, wob_ref, ub_ref)
        sg_a, sg_b = _sigmoid(ga_ref[...]), _sigmoid(gb_ref[...])
        merged = (sg_a * y_a + sg_b * y_b).astype(BF16)
        mg_ref[...] = merged
        o = _dot(merged, wout_ref[...])
        x1 = x_ref[...] + gate * o
        rstd = lax.rsqrt(jnp.mean(x1 * x1, axis=-1, keepdims=True) + NORM_EPS)
        xh = x1 * rstd
        diff = xh * gfin - t_ref[...]
        acc_ref[2:3, :] += (0.5 * inv_d) * jnp.sum(diff * diff, axis=0, keepdims=True)
        dy = diff * inv_d
        acc_ref[1:2, :] += jnp.sum(dy * xh, axis=0, keepdims=True)
        dyg = dy * gfin
        dx1 = rstd * (dyg - xh * jnp.mean(dyg * xh, axis=-1, keepdims=True))
        dx1_ref[...] = dx1
        acc_ref[0:1, :] += jnp.sum(dx1 * o, axis=0, keepdims=True)
        d_o = (dx1 * gate).astype(BF16)
        do_ref[...] = d_o
        dmg = _dot(d_o, woutt_ref[...])

        lane = _lane(CHUNK)
        low = lane < HEAD_DIM
        lane_all = _lane(AUG_W) & (CHUNK - 1)

        def branch_bwd(sg, y, att, z, sz, silu, wt_ref, dy_ref, dg_ref, dz_ref, doaug_ref):
            dyb = (dmg * sg).astype(BF16)
            dy_ref[...] = dyb
            dg_ref[...] = (dmg * y * sg * (1.0 - sg)).astype(BF16)
            du = _dot(dyb, wt_ref[...])
            datt = du * silu
            dz_ref[...] = (du * att * (sz * (1.0 + z * (1.0 - sz)))).astype(BF16)
            delta = _dot_split(datt * att, ind_ref[...])
            extra = _place3(lane_all, 64, -_dot_split(delta, e_ref[...]), 0.0)
            for p in range(N_HEADS // 2):
                c = datt[:, CHUNK * p:CHUNK * (p + 1)]
                for hh in range(2):
                    hd = 2 * p + hh
                    src = c if hh == 0 else pltpu.roll(c, HEAD_DIM, 1)
                    doaug_ref[:, CHUNK * hd:CHUNK * (hd + 1)] = jnp.where(
                        low, src, extra[:, CHUNK * hd:CHUNK * (hd + 1)]).astype(BF16)

        branch_bwd(sg_a, y_a, att_a, z_a, sz_a, silu_a, woat_ref, dya_ref, dga_ref, dza_ref, doa_ref)
        branch_bwd(sg_b, y_b, att_b, z_b, sz_b, silu_b, wobt_ref, dyb_ref, dgb_ref, dzb_ref, dob_ref)

    row_blk = lambda w: pl.BlockSpec((tm, w), lambda i: (i, 0))
    full = lambda a: pl.BlockSpec(a.shape, lambda i: (0,) * a.ndim)
    sds = lambda w, dt: jax.ShapeDtypeStruct((s, w), dt)
    out_shape = [sds(D_MODEL, F32), sds(512, BF16), sds(512, BF16), sds(D_MODEL, BF16), sds(D_MODEL, BF16),
                 sds(D_MODEL, BF16), sds(D_MODEL, BF16), sds(AUG_W, BF16), sds(AUG_W, BF16),
                 sds(512, BF16), sds(512, BF16), sds(D_MODEL, BF16), sds(D_MODEL, BF16),
                 jax.ShapeDtypeStruct((8, D_MODEL), F32)]
    ins = [x, tgt, att_a, att_b, za, zb, ga, gb, woa, wob, wout, woa_t, wob_t, wout_t, vec, ind, emat]
    in_specs = [row_blk(a.shape[1]) for a in ins[:8]] + [full(a) for a in ins[8:]]
    out_specs = [row_blk(o.shape[1]) for o in out_shape[:-1]] + [pl.BlockSpec((8, D_MODEL), lambda i: (0, 0))]
    return pl.pallas_call(body, name="post", grid=(s // tm,), out_shape=out_shape, in_specs=in_specs,
                          out_specs=out_specs, compiler_params=_params(1))(*ins)


def _bwd_pre(dproj, w_all_t, x, dx1, mod, tm):
    s = x.shape[0]

    def body(dp_ref, wt_ref, x_ref, dx1_ref, mod_ref, gx_ref, acc_ref):
        @pl.when(pl.program_id(0) == 0)
        def _():
            acc_ref[...] = jnp.zeros_like(acc_ref)

        dh = _dot(dp_ref[...], wt_ref[...])
        xv = x_ref[...]
        rstd = lax.rsqrt(jnp.mean(xv * xv, axis=-1, keepdims=True) + NORM_EPS)
        xh = xv * rstd
        gn = mod_ref[0:1, :]
        acc_ref[0:1, :] += jnp.sum(dh, axis=0, keepdims=True)
        acc_ref[1:2, :] += jnp.sum(dh * (xh * gn), axis=0, keepdims=True)
        dn = dh * (1.0 + mod_ref[1:2, :])
        acc_ref[2:3, :] += jnp.sum(dn * xh, axis=0, keepdims=True)
        dxh = dn * gn
        gx_ref[...] = dx1_ref[...] + rstd * (dxh - xh * jnp.mean(dxh * xh, axis=-1, keepdims=True))

    row_blk = lambda w: pl.BlockSpec((tm, w), lambda i: (i, 0))
    full = lambda a: pl.BlockSpec(a.shape, lambda i: (0,) * a.ndim)
    return pl.pallas_call(
        body, name="bwd_pre", grid=(s // tm,),
        out_shape=[jax.ShapeDtypeStruct((s, D_MODEL), F32), jax.ShapeDtypeStruct((8, D_MODEL), F32)],
        in_specs=[row_blk(PROJ_W), full(w_all_t), row_blk(D_MODEL), row_blk(D_MODEL), full(mod)],
        out_specs=[row_blk(D_MODEL), pl.BlockSpec((8, D_MODEL), lambda i: (0, 0))],
        compiler_params=_params(1))(dproj, w_all_t, x, dx1, mod)


def _tn_matmul(a, b, tn, tk, name):
    s, m = a.shape
    n = b.shape[1]

    def body(a_ref, b_ref, o_ref):
        @pl.when(pl.program_id(1) == 0)
        def _():
            o_ref[...] = jnp.zeros_like(o_ref)

        o_ref[...] += _dot_tn(a_ref[...], b_ref[...])

    return pl.pallas_call(
        body, name=name, grid=(n // tn, s // tk), out_shape=jax.ShapeDtypeStruct((m, n), F32),
        in_specs=[pl.BlockSpec((tk, m), lambda jn, k: (k, 0)), pl.BlockSpec((tk, tn), lambda jn, k: (k, jn))],
        out_specs=pl.BlockSpec((m, tn), lambda jn, k: (0, jn)),
        compiler_params=_params(2))(a, b)


def _adam_math(g, w, m, v):
    m2 = ADAM_B1 * m + (1.0 - ADAM_B1) * g
    v2 = ADAM_B2 * v + (1.0 - ADAM_B2) * (g * g)
    delta = -ADAM_LR * ((m2 / ADAM_C1) / (jnp.sqrt(v2 / ADAM_C2) + ADAM_EPS) + ADAM_WD * w)
    return delta, m2, v2


def _adamw_parts(own, parts, others, w, m, v, tr, name):
    rws, cols = w.shape

    def body(oth_ref, own_ref, p_ref, w_ref, m_ref, v_ref, g_ref, d_ref, m2_ref, v2_ref):
        g = own_ref[...]
        for chip in range(N_CHIPS):
            g = g + oth_ref[chip] * p_ref[chip].astype(F32)
        g_ref[...] = g
        d_ref[...], m2_ref[...], v2_ref[...] = _adam_math(g, w_ref[...], m_ref[...], v_ref[...])

    blk = pl.BlockSpec((tr, cols), lambda i: (i, 0))
    o = jax.ShapeDtypeStruct((rws, cols), F32)
    return pl.pallas_call(
        body, name=name, grid=(rws // tr,), out_shape=[o, o, o, o],
        in_specs=[pl.BlockSpec(memory_space=pltpu.SMEM), blk,
                  pl.BlockSpec((N_CHIPS, tr, cols), lambda i: (0, i, 0)), blk, blk, blk],
        out_specs=[blk, blk, blk, blk], compiler_params=_params(1))(others, own, parts, w, m, v)


def _adamw_ada(c_lanes, d_rows, w, m, v, tr):
    rws, cols = w.shape

    def body(c_ref, d_ref, w_ref, m_ref, v_ref, g_ref, dl_ref, m2_ref, v2_ref):
        for k in range(cols // CHUNK):
            cs = slice(CHUNK * k, CHUNK * (k + 1))
            g = c_ref[0] * d_ref[0:1, cs]
            for b in range(1, N_DEV):
                g = g + c_ref[b] * d_ref[b:b + 1, cs]
            g_ref[:, cs] = g
            dl_ref[:, cs], m2_ref[:, cs], v2_ref[:, cs] = _adam_math(g, w_ref[:, cs], m_ref[:, cs], v_ref[:, cs])

    blk = pl.BlockSpec((tr, cols), lambda i: (i, 0))
    o = jax.ShapeDtypeStruct((rws, cols), F32)
    return pl.pallas_call(
        body, name="adamw_ada", grid=(rws // tr,), out_shape=[o, o, o, o],
        in_specs=[pl.BlockSpec((N_DEV, tr, CHUNK), lambda i: (0, i, 0)), pl.BlockSpec((N_DEV, cols), lambda i: (0, 0)),
                  blk, blk, blk],
        out_specs=[blk, blk, blk, blk], compiler_params=_params(1))(c_lanes, d_rows, w, m, v)


def _adamw_small(packs, w, m, v):
    def body(p_ref, w_ref, m_ref, v_ref, g_ref, d_ref, m2_ref, v2_ref, loss_ref):
        g = p_ref[0]
        for dev in range(1, N_DEV):
            g = g + p_ref[dev]
        g_ref[...] = g
        d_ref[...], m2_ref[...], v2_ref[...] = _adam_math(g, w_ref[...], m_ref[...], v_ref[...])
        loss_ref[...] = jnp.broadcast_to(jnp.sum(g_ref[2:3, 0:D_MODEL], axis=1, keepdims=True), loss_ref.shape)

    o = jax.ShapeDtypeStruct(w.shape, F32)
    return pl.pallas_call(body, name="adamw_small", out_shape=[o, o, o, o, jax.ShapeDtypeStruct((8, CHUNK), F32)],
                          compiler_params=_params(0))(packs, w, m, v)


def _tile(s, want):
    return min(s, want)


def kernel(x, c, positions, w_ada, b_ada, g_norm, w_in, b_f, sinks, w_o_swa, w_o_fox, w_out, g_final, loss_target, m_w_ada, m_b_ada, m_g_norm, m_w_in, m_b_f, m_sinks, m_w_o_swa, m_w_o_fox, m_w_out, m_g_final, v_w_ada, v_b_ada, v_g_norm, v_w_in, v_b_f, v_sinks, v_w_o_swa, v_w_o_fox, v_w_out, v_g_final):
    s = x.shape[1]
    tm = _tile(s, 256)
    ta = _tile(s, 512)
    me = 4 * lax.axis_index("x") + 2 * lax.axis_index("y") + lax.axis_index("c")
    x2, tgt = x[0], loss_target[0]

    (c_all,) = _exchange([jnp.broadcast_to(c, (8, D_MODEL))], ["gather"], "gather_c")
    c_all = c_all[:, 0, :]
    b_shard = lax.dynamic_slice(b_ada, (0, me * ADA_SHARD), (1, ADA_SHARD))
    ada_part = _ada_fwd(c_all, w_ada[0], b_shard)
    ada_g, win_g, woa_g, wob_g, wout_g = _gather_two_level(
        [ada_part, w_in[0].astype(BF16), w_o_swa[0].astype(BF16), w_o_fox[0].astype(BF16), w_out[0].astype(BF16)],
        "gather_weights")
    ada = lax.dynamic_index_in_dim(ada_g, me, axis=1, keepdims=False).reshape(3 * D_MODEL)
    shift, scale, gate = ada[:D_MODEL], ada[D_MODEL:2 * D_MODEL], ada[2 * D_MODEL:]
    w_in_full = win_g.transpose(1, 0, 2).reshape(D_MODEL, N_DEV * IN_SHARD)
    woa = woa_g.transpose(1, 0, 2).reshape(512, D_MODEL)
    wob = wob_g.transpose(1, 0, 2).reshape(512, D_MODEL)
    wout = wout_g.reshape(D_MODEL, D_MODEL)

    splits = np.cumsum([512, 128, 128, 512, 512, 512, 512, 8, 512, 1024])
    wqa, wka, wva, wza, wqb, wkb, wvb, wfb, wzb, wga, wgb = jnp.split(w_in_full, splits, axis=1)
    pad_kv = lambda w: jnp.pad(w.reshape(D_MODEL, KV_GROUPS, HEAD_DIM), ((0, 0), (0, 0), (0, CHUNK - HEAD_DIM))).reshape(D_MODEL, KV_GROUPS * CHUNK)
    w_all = jnp.concatenate([wqa, pad_kv(wka), pad_kv(wva), wza, wqb, wkb, wvb, wzb, wga, wgb,
                             jnp.pad(wfb, ((0, 0), (0, CHUNK - N_HEADS)))], axis=1)

    zrow = jnp.zeros((1, D_MODEL), F32)
    mod = jnp.concatenate([g_norm, scale[None], shift[None], zrow, zrow, zrow, zrow, zrow], axis=0)
    bf_row = jnp.pad(b_f, ((0, 0), (0, CHUNK - N_HEADS)))
    emat_np = np.zeros((CHUNK, AUG_W), np.float32)
    ind_np = np.zeros((512, CHUNK), np.float32)
    for hd in range(N_HEADS):
        emat_np[hd, CHUNK * hd + 64:CHUNK * hd + 67] = 1.0
        ind_np[HEAD_DIM * hd:HEAD_DIM * (hd + 1), hd] = 1.0
    emat, ind = jnp.asarray(emat_np, BF16), jnp.asarray(ind_np, BF16)
    inv_freq = np.power(np.float32(ROPE_THETA), -np.arange(0, HEAD_DIM, 2, dtype=np.float32) / HEAD_DIM)
    inv_freq = jnp.asarray(np.tile(inv_freq, CHUNK // (HEAD_DIM // 2))[None, :], F32)
    cos, sa, sb = _rope_tables(positions.reshape(s, 1).astype(F32), inv_freq, tm)
    sink_rows = jnp.broadcast_to(jnp.pad(sinks.reshape(KV_GROUPS, GROUP), ((0, 0), (0, 8 - GROUP)))[:, :, None],
                                 (KV_GROUPS, 8, CHUNK))

    (h, qa, ka, va, za, qb, kb, vb, zb, ga, gb, xf) = _fwd_proj(x2, mod, w_all, cos, sa, sb, bf_row, emat, tm)
    att_a, qa_bwd = _swa_fwd(qa, ka, va, sink_rows, ta)
    att_b, qb_bwd = _fox_fwd(qb, kb, vb, ta)
    vec = jnp.concatenate([gate[None], g_final[None], zrow, zrow, zrow, zrow, zrow, zrow], axis=0)
    (dx1, ua, ub, merged, d_o, dya, dyb, doa, dob, dza, dzb, dga, dgb, post_acc) = _post(
        x2, tgt, att_a, att_b, za, zb, ga, gb, woa, wob, wout, woa.T, wob.T, wout.T, vec, ind, emat, tm)

    dqa, dka, dva, dsink = _swa_bwd(ka, va, qa_bwd, doa, cos, sa, sb, sink_rows, ta)
    dkb, dvb, dck, dqb, dcq = _fox_bwd(kb, vb, qb_bwd, dob, ta)
    by_head = lambda a: jnp.pad(a[:, :, :2].transpose(1, 0, 2).reshape(s, N_HEADS), ((0, 0), (0, CHUNK - N_HEADS)))
    dfb, dbf_acc = _fgate_bwd(by_head(dcq), by_head(dck), xf, tm)
    dproj = jnp.concatenate([dqa, dka, dva, dza, dqb, dkb, dvb, dzb, dga, dgb, dfb], axis=1)
    grad_x, pre_acc = _bwd_pre(dproj, w_all.T, x2, dx1, mod, tm)
    dw_all = _tn_matmul(h, dproj, 1152, ta, "dw_in")
    dwoa = _tn_matmul(ua, dya, 512, ta, "dw_o_swa")
    dwob = _tn_matmul(ub, dyb, 512, ta, "dw_o_fox")
    dwout = _tn_matmul(merged, d_o, 512, ta, "dw_out")

    take = lambda name: dw_all[:, SEG[name][0]:SEG[name][0] + SEG[name][1]]
    unpad_kv = lambda g: g.reshape(D_MODEL, KV_GROUPS, CHUNK)[:, :, :HEAD_DIM].reshape(D_MODEL, KV_GROUPS * HEAD_DIM)
    dw_in_full = jnp.concatenate([take("qa"), unpad_kv(take("ka")), unpad_kv(take("va")), take("za"), take("qb"),
                                  take("kb"), take("vb"), take("f")[:, :N_HEADS], take("zb"), take("ga"), take("gb")], axis=1)

    misc = jnp.concatenate([dbf_acc[0, :N_HEADS], dsink[:, :GROUP, 0].reshape(N_HEADS), jnp.zeros((D_MODEL - 16,), F32)])
    zpad = jnp.zeros((2 * D_MODEL,), F32)
    pack = jnp.stack([jnp.concatenate([pre_acc[0], pre_acc[1], post_acc[0]]),
                      jnp.concatenate([pre_acc[2], post_acc[1], misc]),
                      jnp.concatenate([post_acc[2], zpad])] + [jnp.zeros((3 * D_MODEL,), F32)] * 5)
    core, chip = lax.axis_index("c"), 2 * lax.axis_index("x") + lax.axis_index("y")
    col_slots = lambda g, r, w: g.reshape(r, N_CHIPS, 2, w).transpose(2, 1, 0, 3)
    slots = [col_slots(dw_in_full, D_MODEL, IN_SHARD), col_slots(dwoa, 512, 128), col_slots(dwob, 512, 128),
             dwout.reshape(N_CHIPS, 2, 128, D_MODEL).transpose(1, 0, 2, 3)]
    theirs = _swap_sibling(slots, "grads_sibling")
    own, halves = [], []
    for sl, th, nm in zip(slots, theirs, ("w_in", "w_o_swa", "w_o_fox", "w_out")):
        full, half = _chip_partial(lax.dynamic_index_in_dim(sl, core, 0, keepdims=False), th, min(th.shape[1], 256),
                                   "chip_partial_" + nm)
        own.append(lax.dynamic_index_in_dim(full, chip, 0, keepdims=False))
        halves.append(half)
    p_win, p_woa, p_wob, p_wout, packs = _exchange(halves + [pack], ["chips"] * 4 + ["gather"], "exchange_grads")
    others = jnp.where(jnp.arange(N_CHIPS) == chip, 0.0, 1.0).astype(F32)

    g_win, d_win, m_win, v_win = _adamw_parts(own[0], p_win, others, w_in[0], m_w_in[0], v_w_in[0], 128, "adamw_w_in")
    g_woa, d_woa, m_woa, v_woa = _adamw_parts(own[1], p_woa, others, w_o_swa[0], m_w_o_swa[0], v_w_o_swa[0], 128, "adamw_w_o_swa")
    g_wob, d_wob, m_wob, v_wob = _adamw_parts(own[2], p_wob, others, w_o_fox[0], m_w_o_fox[0], v_w_o_fox[0], 128, "adamw_w_o_fox")
    g_wout, d_wout, m_wout, v_wout = _adamw_parts(own[3], p_wout, others, w_out[0], m_w_out[0], v_w_out[0], 128, "adamw_w_out")
    d_ada_rows = lax.dynamic_slice(packs[:, 0, :], (0, me * ADA_SHARD), (N_DEV, ADA_SHARD))
    c_lanes = jnp.broadcast_to(c_all[:, :, None], (N_DEV, D_MODEL, CHUNK))
    g_wada, d_wada, m_wada, v_wada = _adamw_ada(c_lanes, d_ada_rows, w_ada[0], m_w_ada[0], v_w_ada[0], 256)

    def small_pack(bada, gn, gf, bfv, sk):
        misc_w = jnp.concatenate([bfv[0], sk[0], jnp.zeros((D_MODEL - 16,), F32)])
        return jnp.stack([bada[0], jnp.concatenate([gn[0], gf, misc_w])] + [jnp.zeros((3 * D_MODEL,), F32)] * 6)

    sm = _adamw_small(packs, small_pack(b_ada, g_norm, g_final, b_f, sinks),
                      small_pack(m_b_ada, m_g_norm, m_g_final, m_b_f, m_sinks),
                      small_pack(v_b_ada, v_g_norm, v_g_final, v_b_f, v_sinks))
    loss = sm[4][0, 0]

    def unpack(p):
        return (p[0][None], p[1, :D_MODEL][None], p[1, 2 * D_MODEL:2 * D_MODEL + 8][None],
                p[1, 2 * D_MODEL + 8:2 * D_MODEL + 16][None], p[1, D_MODEL:2 * D_MODEL])

    outs = []
    for big, small in (((g_wada, g_win, g_woa, g_wob, g_wout), sm[0]), ((d_wada, d_win, d_woa, d_wob, d_wout), sm[1]),
                       ((m_wada, m_win, m_woa, m_wob, m_wout), sm[2]), ((v_wada, v_win, v_woa, v_wob, v_wout), sm[3])):
        wada_o, win_o, woa_o, wob_o, wout_o = big
        bada_o, gn_o, bf_o, sk_o, gf_o = unpack(small)
        outs += [wada_o[None], bada_o, gn_o, win_o[None], bf_o, sk_o, woa_o[None], wob_o[None], wout_o[None], gf_o]
    return (loss, grad_x[None], *outs)
```

```python
import numpy as np
import jax
import jax.numpy as jnp
from jax import lax
from jax.experimental import pallas as pl
from jax.experimental.pallas import tpu as pltpu

F32 = jnp.float32
BF16 = jnp.bfloat16

D_MODEL = 1024
HEAD_DIM = 64
N_HEADS = 8
KV_GROUPS = 2
GROUP = N_HEADS // KV_GROUPS
WINDOW = 128
CHUNK = 128
AUG_W = N_HEADS * CHUNK
N_DEV = 8
IN_SHARD = 673
ADA_SHARD = 384
NORM_EPS = 1e-6
ROPE_THETA = 10000.0
Q_SCALE = HEAD_DIM ** -0.5
NEG = -1e30

ADAM_LR = 0.001
ADAM_B1 = 0.9
ADAM_B2 = 0.999
ADAM_EPS = 1e-08
ADAM_WD = 0.01
ADAM_STEP = 10
ADAM_C1 = 1.0 - ADAM_B1 ** ADAM_STEP
ADAM_C2 = 1.0 - ADAM_B2 ** ADAM_STEP

SEG = {}
_off = 0
for _name, _w in (("qa", 512), ("ka", 256), ("va", 256), ("za", 512), ("qb", 512), ("kb", 512),
                  ("vb", 512), ("zb", 512), ("ga", 1024), ("gb", 1024), ("f", 128)):
    SEG[_name] = (_off, _w)
    _off += _w
PROJ_W = _off

VMEM_LIMIT = 56 * 1024 * 1024


def _params(n_axes):
    return pltpu.CompilerParams(dimension_semantics=("arbitrary",) * n_axes, vmem_limit_bytes=VMEM_LIMIT)


def _dot(a, b):
    return jnp.dot(a, b, preferred_element_type=F32)


def _dot_nt(a, b):
    return lax.dot_general(a, b, (((1,), (1,)), ((), ())), preferred_element_type=F32)


def _dot_tn(a, b):
    return lax.dot_general(a, b, (((0,), (0,)), ((), ())), preferred_element_type=F32)


def _lane(n):
    return lax.broadcasted_iota(jnp.int32, (1, n), 1)


def _split3(x):
    hi = x.astype(BF16).astype(F32)
    r = x - hi
    mid = r.astype(BF16).astype(F32)
    lo = (r - mid).astype(BF16).astype(F32)
    return hi, mid, lo


def _dot_split(x, b):
    hi, mid, lo = _split3(x)
    return _dot(hi.astype(BF16), b) + _dot(mid.astype(BF16), b) + _dot(lo.astype(BF16), b)


def _place3(lane, base, x, other):
    hi, mid, lo = _split3(x)
    return jnp.where(lane == base, hi, jnp.where(lane == base + 1, mid, jnp.where(lane == base + 2, lo, other)))


def _lane_sum(x, lo, hi):
    lane = _lane(x.shape[1])
    return jnp.sum(jnp.where((lane >= lo) & (lane < hi), x, 0.0), axis=1, keepdims=True)


def _ones_lanes(lo, hi):
    lane = _lane(CHUNK)
    return jnp.where((lane >= lo) & (lane < hi), 1.0, 0.0).astype(F32)


def _rope(c, cos, sa, sb):
    return c * cos + pltpu.roll(c, CHUNK - 32, 1) * sa + pltpu.roll(c, 32, 1) * sb


def _rope_inv(d, cos, sa, sb):
    return d * cos - (pltpu.roll(d, CHUNK - 32, 1) * sa + pltpu.roll(d, 32, 1) * sb)


def _pair(c0, c1):
    return jnp.where(_lane(CHUNK) < HEAD_DIM, c0, pltpu.roll(c1, HEAD_DIM, 1))


def _sigmoid(x):
    return 1.0 / (1.0 + jnp.exp(-x))


N_CHIPS = 4
ANY_SPEC = pl.BlockSpec(memory_space=pl.ANY)


def _exchange(arrays, modes, name):
    n = len(arrays)
    out_shape = [jax.ShapeDtypeStruct((N_DEV,) + a.shape if md == "gather" else a.shape, a.dtype)
                 for a, md in zip(arrays, modes)]

    def body(*refs):
        srcs, dsts = refs[:n], refs[n:2 * n]
        send_sems, recv_sems, loc_sems = refs[2 * n:]
        x, y, c = lax.axis_index("x"), lax.axis_index("y"), lax.axis_index("c")

        def slot(i, px, py, pc):
            return 4 * px + 2 * py + pc if modes[i] == "gather" else 2 * px + py

        def src_of(i, px, py, pc):
            return srcs[i] if modes[i] == "gather" else srcs[i].at[slot(i, px, py, pc)]

        local = [pltpu.make_async_copy(src_of(i, x, y, c), dsts[i].at[slot(i, x, y, c)], loc_sems.at[i])
                 for i in range(n)]
        for cp in local:
            cp.start()
        sends, recvs = [], []
        for r in (1, 2, 4, 3, 5, 6, 7):
            px = 1 - x if r & 4 else x
            py = 1 - y if r & 2 else y
            pc = 1 - c if r & 1 else c
            for i in range(n):
                if modes[i] == "chips" and r & 1:
                    continue
                sems = dict(send_sem=send_sems.at[i, r - 1], recv_sem=recv_sems.at[i, r - 1],
                            device_id=(px, py, pc), device_id_type=pl.DeviceIdType.MESH)
                sends.append(pltpu.make_async_remote_copy(
                    src_ref=src_of(i, px, py, pc), dst_ref=dsts[i].at[slot(i, x, y, c)], **sems))
                recvs.append(pltpu.make_async_remote_copy(
                    src_ref=src_of(i, px, py, pc), dst_ref=dsts[i].at[slot(i, px, py, pc)], **sems))
        for cp in sends:
            cp.start()
        for cp in recvs:
            cp.wait_recv()
        for cp in sends:
            cp.wait_send()
        for cp in local:
            cp.wait()

    return pl.pallas_call(
        body, name=name, out_shape=out_shape,
        in_specs=[ANY_SPEC] * n, out_specs=[ANY_SPEC] * n,
        scratch_shapes=[pltpu.SemaphoreType.DMA((n, N_DEV - 1)), pltpu.SemaphoreType.DMA((n, N_DEV - 1)),
                        pltpu.SemaphoreType.DMA((n,))],
    )(*arrays)


def _gather_two_level(arrays, name):
    n = len(arrays)
    out_shape = [jax.ShapeDtypeStruct((N_DEV,) + a.shape, a.dtype) for a in arrays]

    def body(*refs):
        srcs, dsts = refs[:n], refs[n:2 * n]
        send_sems, recv_sems, loc_sems = refs[2 * n:]
        x, y, c = lax.axis_index("x"), lax.axis_index("y"), lax.axis_index("c")
        me, sibling = (x, y, c), (x, y, 1 - c)
        chips = [(1 - x, y), (x, 1 - y), (1 - x, 1 - y)]

        def rows(i, dev):
            return dsts[i].at[4 * dev[0] + 2 * dev[1] + dev[2]]

        def copy(i, k, block, to, src=None):
            return pltpu.make_async_remote_copy(
                src_ref=rows(i, block) if src is None else src, dst_ref=rows(i, block),
                send_sem=send_sems.at[i, k], recv_sem=recv_sems.at[i, k],
                device_id=to, device_id_type=pl.DeviceIdType.MESH)

        local = [pltpu.make_async_copy(srcs[i], rows(i, me), loc_sems.at[i]) for i in range(n)]
        for cp in local:
            cp.start()
        first = []
        for i in range(n):
            first.append(copy(i, 0, me, sibling, src=srcs[i]))
            first += [copy(i, 1 + j, me, (*chip, c), src=srcs[i]) for j, chip in enumerate(chips)]
        for cp in first:
            cp.start()
        passed = []
        for j, chip in enumerate(chips):
            for i in range(n):
                copy(i, 1 + j, (*chip, c), me).wait_recv()
                cp = copy(i, 4 + j, (*chip, c), sibling)
                cp.start()
                passed.append(cp)
        for i in range(n):
            copy(i, 0, sibling, me).wait_recv()
            for j, chip in enumerate(chips):
                copy(i, 4 + j, (*chip, 1 - c), me).wait_recv()
        for cp in first + passed:
            cp.wait_send()
        for cp in local:
            cp.wait()

    return pl.pallas_call(
        body, name=name, out_shape=out_shape,
        in_specs=[ANY_SPEC] * n, out_specs=[ANY_SPEC] * n,
        scratch_shapes=[pltpu.SemaphoreType.DMA((n, N_DEV - 1)), pltpu.SemaphoreType.DMA((n, N_DEV - 1)),
                        pltpu.SemaphoreType.DMA((n,))],
    )(*arrays)


def _swap_sibling(arrays, name):
    n = len(arrays)
    out_shape = [jax.ShapeDtypeStruct(a.shape[1:], a.dtype) for a in arrays]

    def body(*refs):
        srcs, dsts = refs[:n], refs[n:2 * n]
        send_sems, recv_sems = refs[2 * n:]
        x, y, c = lax.axis_index("x"), lax.axis_index("y"), lax.axis_index("c")
        copies = [pltpu.make_async_remote_copy(
            src_ref=srcs[i].at[1 - c], dst_ref=dsts[i], send_sem=send_sems.at[i], recv_sem=recv_sems.at[i],
            device_id=(x, y, 1 - c), device_id_type=pl.DeviceIdType.MESH) for i in range(n)]
        for cp in copies:
            cp.start()
        for cp in copies:
            cp.wait()

    return pl.pallas_call(
        body, name=name, out_shape=out_shape, in_specs=[ANY_SPEC] * n, out_specs=[ANY_SPEC] * n,
        scratch_shapes=[pltpu.SemaphoreType.DMA((n,)), pltpu.SemaphoreType.DMA((n,))],
    )(*arrays)


def _chip_partial(mine, theirs, tr, name):
    k, rws, cols = mine.shape

    def body(a_ref, b_ref, f_ref, h_ref):
        sm = a_ref[...] + b_ref[...]
        f_ref[...] = sm
        h_ref[...] = sm.astype(BF16)

    blk = pl.BlockSpec((1, tr, cols), lambda j, i: (j, i, 0))
    return pl.pallas_call(
        body, name=name, grid=(k, rws // tr),
        out_shape=[jax.ShapeDtypeStruct(mine.shape, F32), jax.ShapeDtypeStruct(mine.shape, BF16)],
        in_specs=[blk, blk], out_specs=[blk, blk], compiler_params=_params(2))(mine, theirs)


def _ada_fwd(c_all, w_shard, b_shard):
    def body(c_ref, w_ref, b_ref, o_ref):
        ch, cm, cl = [t.astype(BF16) for t in _split3(c_ref[...])]
        wh, wm, wl = [t.astype(BF16) for t in _split3(w_ref[...])]
        acc = _dot(ch, wh) + _dot(ch, wm) + _dot(cm, wh) + _dot(ch, wl) + _dot(cl, wh) + _dot(cm, wm)
        o_ref[...] = acc + b_ref[...]

    return pl.pallas_call(body, name="ada_fwd", out_shape=jax.ShapeDtypeStruct((N_DEV, ADA_SHARD), F32),
                          compiler_params=_params(0))(c_all, w_shard, b_shard)


def _rope_tables(pos_col, inv_freq, tm):
    s = pos_col.shape[0]

    def body(p_ref, f_ref, cos_ref, sa_ref, sb_ref):
        ang = p_ref[...] * f_ref[...]
        sin = jnp.sin(ang)
        first_half = (_lane(CHUNK) & (HEAD_DIM - 1)) < HEAD_DIM // 2
        cos_ref[...] = jnp.cos(ang)
        sa_ref[...] = jnp.where(first_half, -sin, 0.0)
        sb_ref[...] = jnp.where(first_half, 0.0, sin)

    tab = jax.ShapeDtypeStruct((s, CHUNK), F32)
    blk = pl.BlockSpec((tm, CHUNK), lambda i: (i, 0))
    return pl.pallas_call(
        body, name="rope_tables", grid=(s // tm,), out_shape=[tab, tab, tab],
        in_specs=[pl.BlockSpec((tm, 1), lambda i: (i, 0)), pl.BlockSpec((1, CHUNK), lambda i: (0, 0))],
        out_specs=[blk, blk, blk], compiler_params=_params(1))(pos_col, inv_freq)


def _fwd_proj(x, mod, w_all, cos, sa, sb, bf_row, emat, ind, tm):
    s = x.shape[0]

    def body(x_ref, mod_ref, w_ref, cos_ref, sa_ref, sb_ref, bf_ref, e_ref, ind_ref,
             h_ref, qa_ref, ka_ref, va_ref, za_ref, qb_ref, kb_ref, vb_ref, zb_ref, ga_ref, gb_ref, f_ref,
             bnd_ref, nrm_ref, carry_ref):
        @pl.when(pl.program_id(0) == 0)
        def _():
            carry_ref[...] = jnp.zeros_like(carry_ref)

        xv = x_ref[...]
        rstd = lax.rsqrt(jnp.mean(xv * xv, axis=-1, keepdims=True) + NORM_EPS)
        h = (xv * rstd * mod_ref[0:1, :]) * (1.0 + mod_ref[1:2, :]) + mod_ref[2:3, :]
        hb = h.astype(BF16)
        h_ref[...] = hb

        def seg(name):
            off, w = SEG[name]
            return _dot(hb, w_ref[:, off:off + w])

        lane = _lane(CHUNK)
        low = lane < HEAD_DIM
        q_ones = _ones_lanes(64, 67)
        k_ones = _ones_lanes(67, 70)
        cos_t, sa_t, sb_t = cos_ref[...], sa_ref[...], sb_ref[...]

        def write_heads(ref, nat, extra_of, rope, scale):
            for p in range(N_HEADS // 2):
                c = nat[:, CHUNK * p:CHUNK * (p + 1)]
                if rope:
                    c = _rope(c, cos_t, sa_t, sb_t)
                if scale != 1.0:
                    c = c * scale
                for hh in range(2):
                    hd = 2 * p + hh
                    src = c if hh == 0 else pltpu.roll(c, HEAD_DIM, 1)
                    ref[:, CHUNK * hd:CHUNK * (hd + 1)] = jnp.where(low, src, extra_of(hd)).astype(BF16)

        write_heads(qa_ref, seg("qa"), lambda hd: q_ones, True, Q_SCALE)
        ka = seg("ka")
        va = seg("va")
        for kv in range(KV_GROUPS):
            sl = slice(CHUNK * kv, CHUNK * (kv + 1))
            ka_ref[:, sl] = jnp.where(low, _rope(ka[:, sl], cos_t, sa_t, sb_t), k_ones).astype(BF16)
            va_ref[:, sl] = jnp.where(low, va[:, sl], q_ones).astype(BF16)
        za_ref[...] = seg("za")
        zb_ref[...] = seg("zb")
        ga_ref[...] = seg("ga")
        gb_ref[...] = seg("gb")

        xf = seg("f") + bf_ref[...]
        f_ref[...] = xf
        logf = jnp.minimum(xf, 0.0) - jnp.log1p(jnp.exp(-jnp.abs(xf)))
        row = lax.broadcasted_iota(jnp.int32, (tm, tm), 0)
        col = lax.broadcasted_iota(jnp.int32, (tm, tm), 1)
        tri = jnp.where(col <= row, 1.0, 0.0).astype(BF16)
        hi, mid, lo = _split3(logf)
        cum = _dot(tri, hi.astype(BF16)) + _dot(tri, mid.astype(BF16)) + _dot(tri, lo.astype(BF16))
        cum = cum + carry_ref[0:1, :]
        carry_ref[0:1, :] += jnp.sum(logf, axis=0, keepdims=True)
        spread = _dot_split(cum, e_ref[...])
        lane_all = _lane(AUG_W) & (CHUNK - 1)
        ones_all = jnp.where((lane_all >= 67) & (lane_all < 70), 1.0, 0.0)
        bias = _place3(lane_all, 64, -spread, ones_all)

        qb, kb = seg("qb"), seg("kb")
        write_heads(qb_ref, qb, lambda hd: q_ones, False, Q_SCALE)
        write_heads(kb_ref, kb, lambda hd: bias[:, CHUNK * hd:CHUNK * (hd + 1)], False, 1.0)
        write_heads(vb_ref, seg("vb"), lambda hd: q_ones, False, 1.0)

        rid = lax.broadcasted_iota(jnp.int32, (tm, CHUNK), 0)
        bnd_ref[...] = jnp.zeros_like(bnd_ref)
        bnd_ref[0, 0:1, :] = jnp.sum(jnp.where(rid == 0, cum, 0.0), axis=0, keepdims=True)
        bnd_ref[0, 1:2, :] = jnp.sum(jnp.where(rid == tm - 1, cum, 0.0), axis=0, keepdims=True)

        @pl.when(pl.program_id(0) == 0)
        def _():
            nrm_ref[...] = jnp.zeros_like(nrm_ref)

        for r_, nat in ((0, qb * Q_SCALE), (1, kb)):
            sq = _dot((nat * nat).astype(BF16), ind_ref[...])
            nrm_ref[r_:r_ + 1, :] = jnp.maximum(nrm_ref[r_:r_ + 1, :], jnp.max(sq, axis=0, keepdims=True))

    row_blk = lambda w: pl.BlockSpec((tm, w), lambda i: (i, 0))
    full = lambda a: pl.BlockSpec(a.shape, lambda i: (0,) * a.ndim)
    sds = lambda w, dt: jax.ShapeDtypeStruct((s, w), dt)
    out_shape = [sds(D_MODEL, BF16), sds(AUG_W, BF16), sds(KV_GROUPS * CHUNK, BF16), sds(KV_GROUPS * CHUNK, BF16),
                 sds(512, F32), sds(AUG_W, BF16), sds(AUG_W, BF16), sds(AUG_W, BF16), sds(512, F32),
                 sds(D_MODEL, F32), sds(D_MODEL, F32), sds(CHUNK, F32)]
    small = [jax.ShapeDtypeStruct((s // tm, 8, CHUNK), F32), jax.ShapeDtypeStruct((8, CHUNK), F32)]
    return pl.pallas_call(
        body, name="fwd_proj", grid=(s // tm,), out_shape=out_shape + small,
        in_specs=[row_blk(D_MODEL), full(mod), full(w_all), row_blk(CHUNK), row_blk(CHUNK), row_blk(CHUNK),
                  full(bf_row), full(emat), full(ind)],
        out_specs=[row_blk(o.shape[1]) for o in out_shape] + [pl.BlockSpec((1, 8, CHUNK), lambda i: (i, 0, 0)),
                                                                 pl.BlockSpec((8, CHUNK), lambda i: (0, 0))],
        scratch_shapes=[pltpu.VMEM((8, CHUNK), F32)],
        compiler_params=_params(1))(x, mod, w_all, cos, sa, sb, bf_row, emat, ind)


def _swa_fwd(q_aug, k_aug, v_aug, sink_rows, tq):
    s = q_aug.shape[0]
    r = tq // WINDOW
    gw = GROUP * CHUNK

    def body(q_ref, kc_ref, kp_ref, vc_ref, vp_ref, sink_ref, o_ref, qb_ref):
        i = pl.program_id(1)
        row = lax.broadcasted_iota(jnp.int32, (WINDOW, WINDOW), 0)
        col = lax.broadcasted_iota(jnp.int32, (WINDOW, WINDOW), 1)
        lane = _lane(CHUNK)
        for sub in range(r):
            rows = slice(WINDOW * sub, WINDOW * (sub + 1))
            prev = slice(WINDOW * (sub - 1), WINDOW * sub)
            k_c, v_c = kc_ref[rows, :], vc_ref[rows, :]
            k_p = kp_ref[...] if sub == 0 else kc_ref[prev, :]
            v_p = vp_ref[...] if sub == 0 else vc_ref[prev, :]
            valid_c = col <= row
            valid_p = col > row
            if sub == 0:
                valid_p = valid_p & (i > 0)
            outs = []
            for g in range(GROUP):
                cs = slice(CHUNK * g, CHUNK * (g + 1))
                q = q_ref[rows, cs]
                sink = jnp.max(sink_ref[0, g:g + 1, :], axis=1, keepdims=True)
                s_c = jnp.where(valid_c, _dot_nt(q, k_c), NEG)
                s_p = jnp.where(valid_p, _dot_nt(q, k_p), NEG)
                m = jnp.maximum(jnp.maximum(jnp.max(s_c, axis=1, keepdims=True),
                                            jnp.max(s_p, axis=1, keepdims=True)), sink)
                acc = _dot(jnp.exp(s_c - m).astype(BF16), v_c) + _dot(jnp.exp(s_p - m).astype(BF16), v_p)
                denom = _lane_sum(acc, 64, 65) + jnp.exp(sink - m)
                outs.append(acc / denom)
                lse = m + jnp.log(denom)
                qb_ref[rows, cs] = _place3(lane, 67, -lse, q.astype(F32)).astype(BF16)
            for pp in range(GROUP // 2):
                o_ref[rows, CHUNK * pp:CHUNK * (pp + 1)] = _pair(outs[2 * pp], outs[2 * pp + 1])

    return pl.pallas_call(
        body, name="swa_fwd", grid=(KV_GROUPS, s // tq),
        out_shape=[jax.ShapeDtypeStruct((s, 512), F32), jax.ShapeDtypeStruct((s, AUG_W), BF16)],
        in_specs=[pl.BlockSpec((tq, gw), lambda kv, i: (i, kv)),
                  pl.BlockSpec((tq, CHUNK), lambda kv, i: (i, kv)),
                  pl.BlockSpec((WINDOW, CHUNK), lambda kv, i: (jnp.maximum(i * r - 1, 0), kv)),
                  pl.BlockSpec((tq, CHUNK), lambda kv, i: (i, kv)),
                  pl.BlockSpec((WINDOW, CHUNK), lambda kv, i: (jnp.maximum(i * r - 1, 0), kv)),
                  pl.BlockSpec((1, 8, CHUNK), lambda kv, i: (kv, 0, 0))],
        out_specs=[pl.BlockSpec((tq, GROUP * HEAD_DIM), lambda kv, i: (i, kv)),
                   pl.BlockSpec((tq, gw), lambda kv, i: (i, kv))],
        compiler_params=_params(2))(q_aug, k_aug, k_aug, v_aug, v_aug, sink_rows)


def _swa_bwd(k_aug, v_aug, q_bwd, do_aug, cos, sa, sb, sink_rows, tk):
    s = k_aug.shape[0]
    r = tk // WINDOW
    nt = s // tk
    nb = s // WINDOW
    gw = GROUP * CHUNK

    def body(k_ref, v_ref, q_ref, qn_ref, do_ref, don_ref, cos_ref, sa_ref, sb_ref, sink_ref,
             dq_ref, dk_ref, dv_ref, dsink_ref, carry_ref, acc_ref):
        j = pl.program_id(1)

        @pl.when(j == 0)
        def _():
            carry_ref[...] = jnp.zeros_like(carry_ref)
            dsink_ref[...] = jnp.zeros_like(dsink_ref)

        acc_ref[...] = jnp.zeros_like(acc_ref)
        acc_ref[0:WINDOW, :] = carry_ref[...]
        row = lax.broadcasted_iota(jnp.int32, (WINDOW, WINDOW), 0)
        col = lax.broadcasted_iota(jnp.int32, (WINDOW, WINDOW), 1)
        lane = _lane(CHUNK)
        low = lane < HEAD_DIM
        cos_t, sa_t, sb_t = cos_ref[...], sa_ref[...], sb_ref[...]
        for sub in range(r):
            rows = slice(WINDOW * sub, WINDOW * (sub + 1))
            nxt = slice(WINDOW * (sub + 1), WINDOW * (sub + 2))
            k, v = k_ref[rows, :], v_ref[rows, :]
            dk = jnp.zeros((WINDOW, CHUNK), F32)
            dv = jnp.zeros((WINDOW, CHUNK), F32)
            valid_c = row <= col
            valid_n = row > col
            if sub == r - 1:
                valid_n = valid_n & (j < nt - 1)
            for g in range(GROUP):
                cs = slice(CHUNK * g, CHUNK * (g + 1))
                q, do = q_ref[rows, cs], do_ref[rows, cs]
                pt = jnp.exp(jnp.where(valid_c, _dot_nt(k, q), NEG))
                ds = (pt * _dot_nt(v, do)).astype(BF16)
                dv = dv + _dot(pt.astype(BF16), do)
                dk = dk + _dot(ds, q)
                acc_ref[rows, cs] += _dot_tn(ds, k)
                sink = jnp.max(sink_ref[0, g:g + 1, :], axis=1, keepdims=True)
                p_sink = jnp.exp(sink + _lane_sum(q.astype(F32), 67, 70))
                term = jnp.sum(p_sink * _lane_sum(do.astype(F32), 64, 67), axis=0, keepdims=True)
                dsink_ref[0, g:g + 1, :] += jnp.broadcast_to(term, (1, CHUNK))
                if sub < r - 1:
                    qn, don = q_ref[nxt, cs], do_ref[nxt, cs]
                else:
                    qn, don = qn_ref[:, cs], don_ref[:, cs]
                pt = jnp.exp(jnp.where(valid_n, _dot_nt(k, qn), NEG))
                ds = (pt * _dot_nt(v, don)).astype(BF16)
                dv = dv + _dot(pt.astype(BF16), don)
                dk = dk + _dot(ds, qn)
                dqn = _dot_tn(ds, k)
                if sub < r - 1:
                    acc_ref[nxt, cs] += dqn
                else:
                    carry_ref[:, cs] = dqn
            dk_ref[rows, :] = _rope_inv(jnp.where(low, dk, 0.0), cos_t[rows, :], sa_t[rows, :], sb_t[rows, :]).astype(BF16)
            dv_ref[rows, :] = jnp.where(low, dv, 0.0).astype(BF16)
        for pp in range(GROUP // 2):
            d = _pair(acc_ref[:, CHUNK * 2 * pp:CHUNK * (2 * pp + 1)], acc_ref[:, CHUNK * (2 * pp + 1):CHUNK * (2 * pp + 2)])
            dq_ref[:, CHUNK * pp:CHUNK * (pp + 1)] = (_rope_inv(d, cos_t, sa_t, sb_t) * Q_SCALE).astype(BF16)

    cur = lambda w: pl.BlockSpec((tk, w), lambda kv, j: (j, kv))
    nxt = pl.BlockSpec((WINDOW, gw), lambda kv, j: (jnp.minimum((j + 1) * r, nb - 1), kv))
    tab = pl.BlockSpec((tk, CHUNK), lambda kv, j: (j, 0))
    return pl.pallas_call(
        body, name="swa_bwd", grid=(KV_GROUPS, nt),
        out_shape=[jax.ShapeDtypeStruct((s, 512), BF16), jax.ShapeDtypeStruct((s, KV_GROUPS * CHUNK), BF16),
                   jax.ShapeDtypeStruct((s, KV_GROUPS * CHUNK), BF16), jax.ShapeDtypeStruct((KV_GROUPS, 8, CHUNK), F32)],
        in_specs=[cur(CHUNK), cur(CHUNK), cur(gw), nxt, cur(gw), nxt, tab, tab, tab,
                  pl.BlockSpec((1, 8, CHUNK), lambda kv, j: (kv, 0, 0))],
        out_specs=[pl.BlockSpec((tk, GROUP * HEAD_DIM), lambda kv, j: (j, kv)), cur(CHUNK), cur(CHUNK),
                   pl.BlockSpec((1, 8, CHUNK), lambda kv, j: (kv, 0, 0))],
        scratch_shapes=[pltpu.VMEM((WINDOW, gw), F32), pltpu.VMEM((tk, gw), F32)],
        compiler_params=_params(2))(k_aug, v_aug, q_bwd, q_bwd, do_aug, do_aug, cos, sa, sb, sink_rows)


def _fox_fwd(first_key, q_aug, k_aug, v_aug, t):
    s = q_aug.shape[0]
    pw = 2 * CHUNK

    def body(lo_ref, q_ref, k_ref, v_ref, o_ref, qb_ref):
        i = pl.program_id(1)
        first = lo_ref[pl.program_id(0), i]
        row = lax.broadcasted_iota(jnp.int32, (t, t), 0)
        col = lax.broadcasted_iota(jnp.int32, (t, t), 1)
        lane = _lane(CHUNK)
        heads = (slice(0, CHUNK), slice(CHUNK, pw))
        qs = [q_ref[:, hs] for hs in heads]

        def step(jb, carry, masked):
            rows = pl.ds(pl.multiple_of(jb * t, t), t)
            new = []
            for hh, hs in enumerate(heads):
                m, acc = carry[2 * hh], carry[2 * hh + 1]
                sc = _dot_nt(qs[hh], k_ref[rows, hs])
                if masked:
                    sc = jnp.where(col <= row, sc, NEG)
                m_new = jnp.maximum(m, jnp.max(sc, axis=1, keepdims=True))
                p = jnp.exp(sc - m_new).astype(BF16)
                new += [m_new, jnp.exp(m - m_new) * acc + _dot(p, v_ref[rows, hs])]
            return tuple(new)

        init = (jnp.full((t, 1), NEG, F32), jnp.zeros((t, CHUNK), F32)) * 2
        carry = step(i, lax.fori_loop(first, i, lambda jb, c: step(jb, c, False), init), True)
        outs = []
        for hh, hs in enumerate(heads):
            m, acc = carry[2 * hh], carry[2 * hh + 1]
            denom = _lane_sum(acc, 64, 65)
            outs.append(acc / denom)
            lse = m + jnp.log(denom)
            qb_ref[:, hs] = _place3(lane, 67, -lse, qs[hh].astype(F32)).astype(BF16)
        o_ref[...] = _pair(outs[0], outs[1])

    return pl.pallas_call(
        body, name="fox_fwd", grid=(N_HEADS // 2, s // t),
        out_shape=[jax.ShapeDtypeStruct((s, 512), F32), jax.ShapeDtypeStruct((s, AUG_W), BF16)],
        in_specs=[pl.BlockSpec(memory_space=pltpu.SMEM),
                  pl.BlockSpec((t, pw), lambda p, i: (i, p)),
                  pl.BlockSpec((s, pw), lambda p, i: (0, p)),
                  pl.BlockSpec((s, pw), lambda p, i: (0, p))],
        out_specs=[pl.BlockSpec((t, CHUNK), lambda p, i: (i, p)), pl.BlockSpec((t, pw), lambda p, i: (i, p))],
        compiler_params=_params(2))(first_key, q_aug, k_aug, v_aug)


def _fox_block_ranges(bounds, norms, r):
    cum_first = bounds[0::r, 0, :N_HEADS]
    cum_last = bounds[r - 1::r, 1, :N_HEADS]
    n = cum_first.shape[0]
    reach = 2.0 * 1.02 * jnp.sqrt(norms[0, :N_HEADS] * norms[1, :N_HEADS]) + 105.0
    blk = jnp.arange(n, dtype=jnp.int32)
    decay = cum_last[None, :, :] - cum_first[:, None, :]
    skip = (decay > reach[None, None, :]) & (blk[None, :, None] < blk[:, None, None])
    first = jnp.min(jnp.where(skip, n, blk[None, :, None]), axis=1)
    first = jnp.minimum(first[:, 0::2], first[:, 1::2]).T
    needed = (first[:, :, None] <= blk[None, None, :]) & (blk[None, :, None] >= blk[None, None, :])
    last = jnp.max(jnp.where(needed, blk[None, :, None], blk[None, None, :]), axis=1)
    return first.astype(jnp.int32), last.astype(jnp.int32)


def _fox_bwd(last_query, k_aug, v_aug, q_bwd, do_aug, t):
    s = k_aug.shape[0]
    n = s // t
    pw = 2 * CHUNK

    def body(hi_ref, k_ref, v_ref, q_ref, do_ref, dk_ref, dv_ref, dck_ref, dq_ref, dcq_ref, dq_scr):
        j = pl.program_id(1)
        last = hi_ref[pl.program_id(0), j]

        @pl.when(j == 0)
        def _():
            dq_scr[...] = jnp.zeros_like(dq_scr)

        row = lax.broadcasted_iota(jnp.int32, (t, t), 0)
        col = lax.broadcasted_iota(jnp.int32, (t, t), 1)
        lane = _lane(CHUNK)
        heads = (slice(0, CHUNK), slice(CHUNK, pw))
        ks = [k_ref[:, hs] for hs in heads]
        vs = [v_ref[:, hs] for hs in heads]

        def step(ib, carry, masked):
            rows = pl.ds(pl.multiple_of(ib * t, t), t)
            new = []
            for hh, hs in enumerate(heads):
                dv, dk = carry[2 * hh], carry[2 * hh + 1]
                q, do = q_ref[rows, hs], do_ref[rows, hs]
                st = _dot_nt(ks[hh], q)
                if masked:
                    st = jnp.where(row <= col, st, NEG)
                pt = jnp.exp(st)
                ds = (pt * _dot_nt(vs[hh], do)).astype(BF16)
                new += [dv + _dot(pt.astype(BF16), do), dk + _dot(ds, q)]
                dq_scr[rows, hs] += _dot_tn(ds, ks[hh])
            return tuple(new)

        zero = jnp.zeros((t, CHUNK), F32)
        carry = lax.fori_loop(j + 1, last + 1, lambda ib, c: step(ib, c, False), step(j, (zero,) * 4, True))
        dvs, dks = (carry[0], carry[2]), (carry[1], carry[3])
        dk_ref[...] = _pair(dks[0], dks[1]).astype(BF16)
        dv_ref[...] = _pair(dvs[0], dvs[1]).astype(BF16)
        dck_ref[0] = jnp.where(lane == 0, pltpu.roll(dks[0], 64, 1),
                               jnp.where(lane == 1, pltpu.roll(dks[1], 65, 1), 0.0))

        @pl.when(j == n - 1)
        def _():
            for ib in range(n):
                rows = slice(t * ib, t * (ib + 1))
                d0, d1 = dq_scr[rows, 0:CHUNK], dq_scr[rows, CHUNK:pw]
                dq_ref[rows, :] = (_pair(d0, d1) * Q_SCALE).astype(BF16)
                dcq_ref[0, rows, :] = jnp.where(lane == 0, pltpu.roll(d0, CHUNK - 67, 1),
                                                jnp.where(lane == 1, pltpu.roll(d1, CHUNK - 66, 1), 0.0))

    return pl.pallas_call(
        body, name="fox_bwd", grid=(N_HEADS // 2, n),
        out_shape=[jax.ShapeDtypeStruct((s, 512), BF16), jax.ShapeDtypeStruct((s, 512), BF16),
                   jax.ShapeDtypeStruct((N_HEADS // 2, s, CHUNK), F32), jax.ShapeDtypeStruct((s, 512), BF16),
                   jax.ShapeDtypeStruct((N_HEADS // 2, s, CHUNK), F32)],
        in_specs=[pl.BlockSpec(memory_space=pltpu.SMEM),
                  pl.BlockSpec((t, pw), lambda p, j: (j, p)), pl.BlockSpec((t, pw), lambda p, j: (j, p)),
                  pl.BlockSpec((s, pw), lambda p, j: (0, p)), pl.BlockSpec((s, pw), lambda p, j: (0, p))],
        out_specs=[pl.BlockSpec((t, CHUNK), lambda p, j: (j, p)), pl.BlockSpec((t, CHUNK), lambda p, j: (j, p)),
                   pl.BlockSpec((1, t, CHUNK), lambda p, j: (p, j, 0)), pl.BlockSpec((s, CHUNK), lambda p, j: (0, p)),
                   pl.BlockSpec((1, s, CHUNK), lambda p, j: (p, 0, 0))],
        scratch_shapes=[pltpu.VMEM((s, pw), F32)],
        compiler_params=_params(2))(last_query, k_aug, v_aug, q_bwd, do_aug)


def _fgate_bwd(dcq, dck, xf, tm):
    s = xf.shape[0]
    nt = s // tm

    def body(dq_ref, dc_ref, xf_ref, df_ref, dbf_ref, carry_ref):
        @pl.when(pl.program_id(0) == 0)
        def _():
            carry_ref[...] = jnp.zeros_like(carry_ref)
            dbf_ref[...] = jnp.zeros_like(dbf_ref)

        row = lax.broadcasted_iota(jnp.int32, (tm, tm), 0)
        col = lax.broadcasted_iota(jnp.int32, (tm, tm), 1)
        tri = jnp.where(col >= row, 1.0, 0.0).astype(BF16)
        dcum = dq_ref[...] - dc_ref[...]
        hi, mid, lo = _split3(dcum)
        dlogf = _dot(tri, hi.astype(BF16)) + _dot(tri, mid.astype(BF16)) + _dot(tri, lo.astype(BF16))
        dlogf = dlogf + carry_ref[0:1, :]
        carry_ref[0:1, :] += jnp.sum(dcum, axis=0, keepdims=True)
        df = dlogf * _sigmoid(-xf_ref[...])
        df_ref[...] = df.astype(BF16)
        dbf_ref[0:1, :] += jnp.sum(df, axis=0, keepdims=True)

    rev = pl.BlockSpec((tm, CHUNK), lambda i: (nt - 1 - i, 0))
    return pl.pallas_call(
        body, name="fgate_bwd", grid=(nt,),
        out_shape=[jax.ShapeDtypeStruct((s, CHUNK), BF16), jax.ShapeDtypeStruct((8, CHUNK), F32)],
        in_specs=[rev, rev, rev], out_specs=[rev, pl.BlockSpec((8, CHUNK), lambda i: (0, 0))],
        scratch_shapes=[pltpu.VMEM((8, CHUNK), F32)],
        compiler_params=_params(1))(dcq, dck, xf)


def _post(x, tgt, att_a, att_b, za, zb, ga, gb, woa_t, wob_t, wout, vec, ind, emat, tm):
    s = x.shape[0]

    def body(x_ref, t_ref, aa_ref, ab_ref, za_ref, zb_ref, ga_ref, gb_ref, woat_ref, wobt_ref, wout_ref,
             vec_ref, ind_ref, e_ref,
             dx1_ref, ua_ref, ub_ref, mg_ref, do_ref, dya_ref, dyb_ref, doa_ref, dob_ref,
             dza_ref, dzb_ref, dga_ref, dgb_ref, acc_ref):
        @pl.when(pl.program_id(0) == 0)
        def _():
            acc_ref[...] = jnp.zeros_like(acc_ref)

        gate, gfin = vec_ref[0:1, :], vec_ref[1:2, :]
        inv_d = 1.0 / D_MODEL

        def branch_fwd(att_ref, z_ref, w_ref, u_ref):
            att, z = att_ref[...], z_ref[...]
            sz = _sigmoid(z)
            silu = z * sz
            u = (att * silu).astype(BF16)
            u_ref[...] = u
            return att, z, sz, silu, _dot_nt(u, w_ref[...])

        att_a, z_a, sz_a, silu_a, y_a = branch_fwd(aa_ref, za_ref, woat_ref, ua_ref)
        att_b, z_b, sz_b, silu_b, y_b = branch_fwd(ab_ref, zb_ref, wobt_ref, ub_ref)
        sg_a, sg_b = _sigmoid(ga_ref[...]), _sigmoid(gb_ref[...])
        merged = (sg_a * y_a + sg_b * y_b).astype(BF16)
        mg_ref[...] = merged
        o = _dot(merged, wout_ref[...])
        x1 = x_ref[...] + gate * o
        rstd = lax.rsqrt(jnp.mean(x1 * x1, axis=-1, keepdims=True) + NORM_EPS)
        xh = x1 * rstd
        diff = xh * gfin - t_ref[...]
        acc_ref[2:3, :] += (0.5 * inv_d) * jnp.sum(diff * diff, axis=0, keepdims=True)
        dy = diff * inv_d
        acc_ref[1:2, :] += jnp.sum(dy * xh, axis=0, keepdims=True)
        dyg = dy * gfin
        dx1 = rstd * (dyg - xh * jnp.mean(dyg * xh, axis=-1, keepdims=True))
        dx1_ref[...] = dx1
        acc_ref[0:1, :] += jnp.sum(dx1 * o, axis=0, keepdims=True)
        d_o = (dx1 * gate).astype(BF16)
        do_ref[...] = d_o
        dmg = _dot_nt(d_o, wout_ref[...])

        lane = _lane(CHUNK)
        low = lane < HEAD_DIM
        lane_all = _lane(AUG_W) & (CHUNK - 1)

        def branch_bwd(sg, y, att, z, sz, silu, wt_ref, dy_ref, dg_ref, dz_ref, doaug_ref):
            dyb = (dmg * sg).astype(BF16)
            dy_ref[...] = dyb
            dg_ref[...] = (dmg * y * sg * (1.0 - sg)).astype(BF16)
            du = _dot(dyb, wt_ref[...])
            datt = du * silu
            dz_ref[...] = (du * att * (sz * (1.0 + z * (1.0 - sz)))).astype(BF16)
            delta = _dot_split(datt * att, ind_ref[...])
            extra = _place3(lane_all, 64, -_dot_split(delta, e_ref[...]), 0.0)
            for p in range(N_HEADS // 2):
                c = datt[:, CHUNK * p:CHUNK * (p + 1)]
                for hh in range(2):
                    hd = 2 * p + hh
                    src = c if hh == 0 else pltpu.roll(c, HEAD_DIM, 1)
                    doaug_ref[:, CHUNK * hd:CHUNK * (hd + 1)] = jnp.where(
                        low, src, extra[:, CHUNK * hd:CHUNK * (hd + 1)]).astype(BF16)

        branch_bwd(sg_a, y_a, att_a, z_a, sz_a, silu_a, woat_ref, dya_ref, dga_ref, dza_ref, doa_ref)
        branch_bwd(sg_b, y_b, att_b, z_b, sz_b, silu_b, wobt_ref, dyb_ref, dgb_ref, dzb_ref, dob_ref)

    row_blk = lambda w: pl.BlockSpec((tm, w), lambda i: (i, 0))
    full = lambda a: pl.BlockSpec(a.shape, lambda i: (0,) * a.ndim)
    sds = lambda w, dt: jax.ShapeDtypeStruct((s, w), dt)
    out_shape = [sds(D_MODEL, F32), sds(512, BF16), sds(512, BF16), sds(D_MODEL, BF16), sds(D_MODEL, BF16),
                 sds(D_MODEL, BF16), sds(D_MODEL, BF16), sds(AUG_W, BF16), sds(AUG_W, BF16),
                 sds(512, BF16), sds(512, BF16), sds(D_MODEL, BF16), sds(D_MODEL, BF16),
                 jax.ShapeDtypeStruct((8, D_MODEL), F32)]
    ins = [x, tgt, att_a, att_b, za, zb, ga, gb, woa_t, wob_t, wout, vec, ind, emat]
    in_specs = [row_blk(a.shape[1]) for a in ins[:8]] + [full(a) for a in ins[8:]]
    out_specs = [row_blk(o.shape[1]) for o in out_shape[:-1]] + [pl.BlockSpec((8, D_MODEL), lambda i: (0, 0))]
    return pl.pallas_call(body, name="post", grid=(s // tm,), out_shape=out_shape, in_specs=in_specs,
                          out_specs=out_specs, compiler_params=_params(1))(*ins)


def _bwd_pre(dproj, w_all, x, dx1, mod, tm):
    s = x.shape[0]

    def body(dp_ref, wt_ref, x_ref, dx1_ref, mod_ref, gx_ref, acc_ref):
        @pl.when(pl.program_id(0) == 0)
        def _():
            acc_ref[...] = jnp.zeros_like(acc_ref)

        dh = _dot_nt(dp_ref[...], wt_ref[...])
        xv = x_ref[...]
        rstd = lax.rsqrt(jnp.mean(xv * xv, axis=-1, keepdims=True) + NORM_EPS)
        xh = xv * rstd
        gn = mod_ref[0:1, :]
        acc_ref[0:1, :] += jnp.sum(dh, axis=0, keepdims=True)
        acc_ref[1:2, :] += jnp.sum(dh * (xh * gn), axis=0, keepdims=True)
        dn = dh * (1.0 + mod_ref[1:2, :])
        acc_ref[2:3, :] += jnp.sum(dn * xh, axis=0, keepdims=True)
        dxh = dn * gn
        gx_ref[...] = dx1_ref[...] + rstd * (dxh - xh * jnp.mean(dxh * xh, axis=-1, keepdims=True))

    row_blk = lambda w: pl.BlockSpec((tm, w), lambda i: (i, 0))
    full = lambda a: pl.BlockSpec(a.shape, lambda i: (0,) * a.ndim)
    return pl.pallas_call(
        body, name="bwd_pre", grid=(s // tm,),
        out_shape=[jax.ShapeDtypeStruct((s, D_MODEL), F32), jax.ShapeDtypeStruct((8, D_MODEL), F32)],
        in_specs=[row_blk(PROJ_W), full(w_all), row_blk(D_MODEL), row_blk(D_MODEL), full(mod)],
        out_specs=[row_blk(D_MODEL), pl.BlockSpec((8, D_MODEL), lambda i: (0, 0))],
        compiler_params=_params(1))(dproj, w_all, x, dx1, mod)


def _tn_matmul(a, b, tn, tk, name):
    s, m = a.shape
    n = b.shape[1]

    def body(a_ref, b_ref, o_ref):
        @pl.when(pl.program_id(1) == 0)
        def _():
            o_ref[...] = jnp.zeros_like(o_ref)

        o_ref[...] += _dot_tn(a_ref[...], b_ref[...])

    return pl.pallas_call(
        body, name=name, grid=(n // tn, s // tk), out_shape=jax.ShapeDtypeStruct((m, n), F32),
        in_specs=[pl.BlockSpec((tk, m), lambda jn, k: (k, 0)), pl.BlockSpec((tk, tn), lambda jn, k: (k, jn))],
        out_specs=pl.BlockSpec((m, tn), lambda jn, k: (0, jn)),
        compiler_params=_params(2))(a, b)


def _adam_math(g, w, m, v):
    m2 = ADAM_B1 * m + (1.0 - ADAM_B1) * g
    v2 = ADAM_B2 * v + (1.0 - ADAM_B2) * (g * g)
    delta = -ADAM_LR * ((m2 / ADAM_C1) / (jnp.sqrt(v2 / ADAM_C2) + ADAM_EPS) + ADAM_WD * w)
    return delta, m2, v2


def _adamw_parts(own, parts, others, w, m, v, tr, name):
    rws, cols = w.shape

    def body(oth_ref, own_ref, p_ref, w_ref, m_ref, v_ref, g_ref, d_ref, m2_ref, v2_ref):
        g = own_ref[...]
        for chip in range(N_CHIPS):
            g = g + oth_ref[chip] * p_ref[chip].astype(F32)
        g_ref[...] = g
        d_ref[...], m2_ref[...], v2_ref[...] = _adam_math(g, w_ref[...], m_ref[...], v_ref[...])

    blk = pl.BlockSpec((tr, cols), lambda i: (i, 0))
    o = jax.ShapeDtypeStruct((rws, cols), F32)
    return pl.pallas_call(
        body, name=name, grid=(rws // tr,), out_shape=[o, o, o, o],
        in_specs=[pl.BlockSpec(memory_space=pltpu.SMEM), blk,
                  pl.BlockSpec((N_CHIPS, tr, cols), lambda i: (0, i, 0)), blk, blk, blk],
        out_specs=[blk, blk, blk, blk], compiler_params=_params(1))(others, own, parts, w, m, v)


def _adamw_ada(c_lanes, d_rows, w, m, v, tr):
    rws, cols = w.shape

    def body(c_ref, d_ref, w_ref, m_ref, v_ref, g_ref, dl_ref, m2_ref, v2_ref):
        for k in range(cols // CHUNK):
            cs = slice(CHUNK * k, CHUNK * (k + 1))
            g = c_ref[0] * d_ref[0:1, cs]
            for b in range(1, N_DEV):
                g = g + c_ref[b] * d_ref[b:b + 1, cs]
            g_ref[:, cs] = g
            dl_ref[:, cs], m2_ref[:, cs], v2_ref[:, cs] = _adam_math(g, w_ref[:, cs], m_ref[:, cs], v_ref[:, cs])

    blk = pl.BlockSpec((tr, cols), lambda i: (i, 0))
    o = jax.ShapeDtypeStruct((rws, cols), F32)
    return pl.pallas_call(
        body, name="adamw_ada", grid=(rws // tr,), out_shape=[o, o, o, o],
        in_specs=[pl.BlockSpec((N_DEV, tr, CHUNK), lambda i: (0, i, 0)), pl.BlockSpec((N_DEV, cols), lambda i: (0, 0)),
                  blk, blk, blk],
        out_specs=[blk, blk, blk, blk], compiler_params=_params(1))(c_lanes, d_rows, w, m, v)


def _adamw_small(packs, w, m, v):
    def body(p_ref, w_ref, m_ref, v_ref, g_ref, d_ref, m2_ref, v2_ref, loss_ref):
        g = p_ref[0]
        for dev in range(1, N_DEV):
            g = g + p_ref[dev]
        g_ref[...] = g
        d_ref[...], m2_ref[...], v2_ref[...] = _adam_math(g, w_ref[...], m_ref[...], v_ref[...])
        loss_ref[...] = jnp.broadcast_to(jnp.sum(g_ref[2:3, 0:D_MODEL], axis=1, keepdims=True), loss_ref.shape)

    o = jax.ShapeDtypeStruct(w.shape, F32)
    return pl.pallas_call(body, name="adamw_small", out_shape=[o, o, o, o, jax.ShapeDtypeStruct((8, CHUNK), F32)],
                          compiler_params=_params(0))(packs, w, m, v)


def _tile(s, want):
    return min(s, want)


def _layout_pieces():
    orig = {"qa": 0, "ka": 512, "va": 640, "za": 768, "qb": 1280, "kb": 1792, "vb": 2304, "f": 2816, "zb": 2824,
            "ga": 3336, "gb": 4360}
    pieces = []
    for name, (off, w) in SEG.items():
        if name in ("ka", "va"):
            pieces += [(orig[name] + HEAD_DIM * kv, orig[name] + HEAD_DIM * (kv + 1), off + CHUNK * kv) for kv in range(KV_GROUPS)]
        else:
            pieces.append((orig[name], orig[name] + (N_HEADS if name == "f" else w), off))
    return pieces


def _assemble_w_all(win_g):
    cols, pos = [], 0
    for lo, hi, new in sorted(_layout_pieces(), key=lambda t: t[2]):
        if new > pos:
            cols.append(jnp.zeros((D_MODEL, new - pos), win_g.dtype))
        col = lo
        while col < hi:
            dev = col // IN_SHARD
            end = min(hi, (dev + 1) * IN_SHARD)
            cols.append(win_g[dev, :, col - dev * IN_SHARD:end - dev * IN_SHARD])
            col = end
        pos = new + hi - lo
    cols.append(jnp.zeros((D_MODEL, PROJ_W - pos), win_g.dtype))
    return jnp.concatenate(cols, axis=1)


def _grad_slot(dw_all, dev):
    lo_d, hi_d = dev * IN_SHARD, (dev + 1) * IN_SHARD
    cols = []
    for lo, hi, new in sorted(_layout_pieces()):
        a, b = max(lo, lo_d), min(hi, hi_d)
        if a < b:
            cols.append(dw_all[:, new + a - lo:new + b - lo])
    return jnp.concatenate(cols, axis=1)


def kernel(x, c, positions, w_ada, b_ada, g_norm, w_in, b_f, sinks, w_o_swa, w_o_fox, w_out, g_final, loss_target, m_w_ada, m_b_ada, m_g_norm, m_w_in, m_b_f, m_sinks, m_w_o_swa, m_w_o_fox, m_w_out, m_g_final, v_w_ada, v_b_ada, v_g_norm, v_w_in, v_b_f, v_sinks, v_w_o_swa, v_w_o_fox, v_w_out, v_g_final):
    s = x.shape[1]
    tm = _tile(s, 256)
    ta = _tile(s, 512)
    me = 4 * lax.axis_index("x") + 2 * lax.axis_index("y") + lax.axis_index("c")
    x2, tgt = x[0], loss_target[0]

    (c_all,) = _exchange([jnp.broadcast_to(c, (8, D_MODEL))], ["gather"], "gather_c")
    c_all = c_all[:, 0, :]
    b_shard = lax.dynamic_slice(b_ada, (0, me * ADA_SHARD), (1, ADA_SHARD))
    ada_part = _ada_fwd(c_all, w_ada[0], b_shard)
    ada_g, win_g, woa_g, wob_g, wout_g = _gather_two_level(
        [ada_part, w_in[0].astype(BF16), w_o_swa[0].T.astype(BF16), w_o_fox[0].T.astype(BF16), w_out[0].astype(BF16)],
        "gather_weights")
    ada = lax.dynamic_index_in_dim(ada_g, me, axis=1, keepdims=False).reshape(3 * D_MODEL)
    shift, scale, gate = ada[:D_MODEL], ada[D_MODEL:2 * D_MODEL], ada[2 * D_MODEL:]
    woa_t = woa_g.reshape(D_MODEL, 512)
    wob_t = wob_g.reshape(D_MODEL, 512)
    wout = wout_g.reshape(D_MODEL, D_MODEL)
    w_all = _assemble_w_all(win_g)

    zrow = jnp.zeros((1, D_MODEL), F32)
    mod = jnp.concatenate([g_norm, scale[None], shift[None], zrow, zrow, zrow, zrow, zrow], axis=0)
    bf_row = jnp.pad(b_f, ((0, 0), (0, CHUNK - N_HEADS)))
    emat_np = np.zeros((CHUNK, AUG_W), np.float32)
    ind_np = np.zeros((512, CHUNK), np.float32)
    for hd in range(N_HEADS):
        emat_np[hd, CHUNK * hd + 64:CHUNK * hd + 67] = 1.0
        ind_np[HEAD_DIM * hd:HEAD_DIM * (hd + 1), hd] = 1.0
    emat, ind = jnp.asarray(emat_np, BF16), jnp.asarray(ind_np, BF16)
    inv_freq = np.power(np.float32(ROPE_THETA), -np.arange(0, HEAD_DIM, 2, dtype=np.float32) / HEAD_DIM)
    inv_freq = jnp.asarray(np.tile(inv_freq, CHUNK // (HEAD_DIM // 2))[None, :], F32)
    cos, sa, sb = _rope_tables(positions.reshape(s, 1).astype(F32), inv_freq, tm)
    sink_rows = jnp.broadcast_to(jnp.pad(sinks.reshape(KV_GROUPS, GROUP), ((0, 0), (0, 8 - GROUP)))[:, :, None],
                                 (KV_GROUPS, 8, CHUNK))

    (h, qa, ka, va, za, qb, kb, vb, zb, ga, gb, xf, bounds, norms) = _fwd_proj(
        x2, mod, w_all, cos, sa, sb, bf_row, emat, ind, tm)
    first_key, last_query = _fox_block_ranges(bounds, norms, ta // tm)
    att_a, qa_bwd = _swa_fwd(qa, ka, va, sink_rows, ta)
    att_b, qb_bwd = _fox_fwd(first_key, qb, kb, vb, ta)
    vec = jnp.concatenate([gate[None], g_final[None], zrow, zrow, zrow, zrow, zrow, zrow], axis=0)
    (dx1, ua, ub, merged, d_o, dya, dyb, doa, dob, dza, dzb, dga, dgb, post_acc) = _post(
        x2, tgt, att_a, att_b, za, zb, ga, gb, woa_t, wob_t, wout, vec, ind, emat, tm)

    dqa, dka, dva, dsink = _swa_bwd(ka, va, qa_bwd, doa, cos, sa, sb, sink_rows, ta)
    dkb, dvb, dck, dqb, dcq = _fox_bwd(last_query, kb, vb, qb_bwd, dob, ta)
    by_head = lambda a: jnp.pad(a[:, :, :2].transpose(1, 0, 2).reshape(s, N_HEADS), ((0, 0), (0, CHUNK - N_HEADS)))
    dfb, dbf_acc = _fgate_bwd(by_head(dcq), by_head(dck), xf, tm)
    dproj = jnp.concatenate([dqa, dka, dva, dza, dqb, dkb, dvb, dzb, dga, dgb, dfb], axis=1)
    grad_x, pre_acc = _bwd_pre(dproj, w_all, x2, dx1, mod, tm)
    dw_all = _tn_matmul(h, dproj, 1152, ta, "dw_in")
    dwoa = _tn_matmul(ua, dya, 512, ta, "dw_o_swa")
    dwob = _tn_matmul(ub, dyb, 512, ta, "dw_o_fox")
    dwout = _tn_matmul(merged, d_o, 512, ta, "dw_out")


    misc = jnp.concatenate([dbf_acc[0, :N_HEADS], dsink[:, :GROUP, 0].reshape(N_HEADS), jnp.zeros((D_MODEL - 16,), F32)])
    zpad = jnp.zeros((2 * D_MODEL,), F32)
    pack = jnp.stack([jnp.concatenate([pre_acc[0], pre_acc[1], post_acc[0]]),
                      jnp.concatenate([pre_acc[2], post_acc[1], misc]),
                      jnp.concatenate([post_acc[2], zpad])] + [jnp.zeros((3 * D_MODEL,), F32)] * 5)
    core, chip = lax.axis_index("c"), 2 * lax.axis_index("x") + lax.axis_index("y")
    col_slots = lambda g, r, w: g.reshape(r, N_CHIPS, 2, w).transpose(2, 1, 0, 3)
    win_slots = jnp.stack([jnp.stack([_grad_slot(dw_all, 2 * ch + co) for ch in range(N_CHIPS)]) for co in range(2)])
    slots = [win_slots, col_slots(dwoa, 512, 128), col_slots(dwob, 512, 128),
             dwout.reshape(N_CHIPS, 2, 128, D_MODEL).transpose(1, 0, 2, 3)]
    theirs = _swap_sibling(slots, "grads_sibling")
    own, halves = [], []
    for sl, th, nm in zip(slots, theirs, ("w_in", "w_o_swa", "w_o_fox", "w_out")):
        full, half = _chip_partial(lax.dynamic_index_in_dim(sl, core, 0, keepdims=False), th, min(th.shape[1], 256),
                                   "chip_partial_" + nm)
        own.append(lax.dynamic_index_in_dim(full, chip, 0, keepdims=False))
        halves.append(half)
    p_win, p_woa, p_wob, p_wout, packs = _exchange(halves + [pack], ["chips"] * 4 + ["gather"], "exchange_grads")
    others = jnp.where(jnp.arange(N_CHIPS) == chip, 0.0, 1.0).astype(F32)

    g_win, d_win, m_win, v_win = _adamw_parts(own[0], p_win, others, w_in[0], m_w_in[0], v_w_in[0], 128, "adamw_w_in")
    g_woa, d_woa, m_woa, v_woa = _adamw_parts(own[1], p_woa, others, w_o_swa[0], m_w_o_swa[0], v_w_o_swa[0], 128, "adamw_w_o_swa")
    g_wob, d_wob, m_wob, v_wob = _adamw_parts(own[2], p_wob, others, w_o_fox[0], m_w_o_fox[0], v_w_o_fox[0], 128, "adamw_w_o_fox")
    g_wout, d_wout, m_wout, v_wout = _adamw_parts(own[3], p_wout, others, w_out[0], m_w_out[0], v_w_out[0], 128, "adamw_w_out")
    d_ada_rows = lax.dynamic_slice(packs[:, 0, :], (0, me * ADA_SHARD), (N_DEV, ADA_SHARD))
    c_lanes = jnp.broadcast_to(c_all[:, :, None], (N_DEV, D_MODEL, CHUNK))
    g_wada, d_wada, m_wada, v_wada = _adamw_ada(c_lanes, d_ada_rows, w_ada[0], m_w_ada[0], v_w_ada[0], 256)

    def small_pack(bada, gn, gf, bfv, sk):
        misc_w = jnp.concatenate([bfv[0], sk[0], jnp.zeros((D_MODEL - 16,), F32)])
        return jnp.stack([bada[0], jnp.concatenate([gn[0], gf, misc_w])] + [jnp.zeros((3 * D_MODEL,), F32)] * 6)

    sm = _adamw_small(packs, small_pack(b_ada, g_norm, g_final, b_f, sinks),
                      small_pack(m_b_ada, m_g_norm, m_g_final, m_b_f, m_sinks),
                      small_pack(v_b_ada, v_g_norm, v_g_final, v_b_f, v_sinks))
    loss = sm[4][0, 0]

    def unpack(p):
        return (p[0][None], p[1, :D_MODEL][None], p[1, 2 * D_MODEL:2 * D_MODEL + 8][None],
                p[1, 2 * D_MODEL + 8:2 * D_MODEL + 16][None], p[1, D_MODEL:2 * D_MODEL])

    outs = []
    for big, small in (((g_wada, g_win, g_woa, g_wob, g_wout), sm[0]), ((d_wada, d_win, d_woa, d_wob, d_wout), sm[1]),
                       ((m_wada, m_win, m_woa, m_wob, m_wout), sm[2]), ((v_wada, v_win, v_woa, v_wob, v_wout), sm[3])):
        wada_o, win_o, woa_o, wob_o, wout_o = big
        bada_o, gn_o, bf_o, sk_o, gf_o = unpack(small)
        outs += [wada_o[None], bada_o, gn_o, win_o[None], bf_o, sk_o, woa_o[None], wob_o[None], wout_o[None], gf_o]
    return (loss, grad_x[None], *outs)
```

```python
import numpy as np
import jax
import jax.numpy as jnp
from jax import lax
from jax.experimental import pallas as pl
from jax.experimental.pallas import tpu as pltpu

F32 = jnp.float32
BF16 = jnp.bfloat16

D_MODEL = 1024
HEAD_DIM = 64
N_HEADS = 8
KV_GROUPS = 2
GROUP = N_HEADS // KV_GROUPS
WINDOW = 128
CHUNK = 128
AUG_W = N_HEADS * CHUNK
N_DEV = 8
IN_SHARD = 673
ADA_SHARD = 384
NORM_EPS = 1e-6
ROPE_THETA = 10000.0
Q_SCALE = HEAD_DIM ** -0.5
NEG = -1e30

ADAM_LR = 0.001
ADAM_B1 = 0.9
ADAM_B2 = 0.999
ADAM_EPS = 1e-08
ADAM_WD = 0.01
ADAM_STEP = 10
ADAM_C1 = 1.0 - ADAM_B1 ** ADAM_STEP
ADAM_C2 = 1.0 - ADAM_B2 ** ADAM_STEP

SEG = {}
_off = 0
for _name, _w in (("qa", 512), ("ka", 256), ("va", 256), ("qb", 512), ("kb", 512), ("vb", 512),
                  ("za", 512), ("zb", 512), ("ga", 1024), ("gb", 1024), ("f", 128)):
    SEG[_name] = (_off, _w)
    _off += _w
PROJ_W = _off
DPROJ_GROUPS = (("a", SEG["qa"][0], 1024), ("qb", SEG["qb"][0], 512), ("kb", SEG["kb"][0], 512),
                ("vb", SEG["vb"][0], 512), ("zg", SEG["za"][0], 3072), ("f", SEG["f"][0], 128))

VMEM_LIMIT = 56 * 1024 * 1024


def _params(n_axes):
    return pltpu.CompilerParams(dimension_semantics=("arbitrary",) * n_axes, vmem_limit_bytes=VMEM_LIMIT)


def _dot(a, b):
    return jnp.dot(a, b, preferred_element_type=F32)


def _dot_nt(a, b):
    return lax.dot_general(a, b, (((1,), (1,)), ((), ())), preferred_element_type=F32)


def _dot_tn(a, b):
    return lax.dot_general(a, b, (((0,), (0,)), ((), ())), preferred_element_type=F32)


def _lane(n):
    return lax.broadcasted_iota(jnp.int32, (1, n), 1)


def _split3(x):
    hi = x.astype(BF16).astype(F32)
    r = x - hi
    mid = r.astype(BF16).astype(F32)
    lo = (r - mid).astype(BF16).astype(F32)
    return hi, mid, lo


def _dot_split(x, b):
    hi, mid, lo = _split3(x)
    return _dot(hi.astype(BF16), b) + _dot(mid.astype(BF16), b) + _dot(lo.astype(BF16), b)


def _place3(lane, base, x, other):
    hi, mid, lo = _split3(x)
    return jnp.where(lane == base, hi, jnp.where(lane == base + 1, mid, jnp.where(lane == base + 2, lo, other)))


def _lane_sum(x, lo, hi):
    lane = _lane(x.shape[1])
    return jnp.sum(jnp.where((lane >= lo) & (lane < hi), x, 0.0), axis=1, keepdims=True)


def _ones_lanes(lo, hi):
    lane = _lane(CHUNK)
    return jnp.where((lane >= lo) & (lane < hi), 1.0, 0.0).astype(F32)


def _rope(c, cos, sa, sb):
    return c * cos + pltpu.roll(c, CHUNK - 32, 1) * sa + pltpu.roll(c, 32, 1) * sb


def _rope_inv(d, cos, sa, sb):
    return d * cos - (pltpu.roll(d, CHUNK - 32, 1) * sa + pltpu.roll(d, 32, 1) * sb)


def _pair(c0, c1):
    return jnp.where(_lane(CHUNK) < HEAD_DIM, c0, pltpu.roll(c1, HEAD_DIM, 1))


def _sigmoid(x):
    return 0.5 * jnp.tanh(0.5 * x) + 0.5


N_CHIPS = 4
ANY_SPEC = pl.BlockSpec(memory_space=pl.ANY)


def _exchange(arrays, modes, name):
    n = len(arrays)
    out_shape = [jax.ShapeDtypeStruct((N_DEV,) + a.shape if md == "gather" else a.shape, a.dtype)
                 for a, md in zip(arrays, modes)]

    def body(*refs):
        srcs, dsts = refs[:n], refs[n:2 * n]
        send_sems, recv_sems, loc_sems = refs[2 * n:]
        x, y, c = lax.axis_index("x"), lax.axis_index("y"), lax.axis_index("c")

        def slot(i, px, py, pc):
            return 4 * px + 2 * py + pc if modes[i] == "gather" else 2 * px + py

        def src_of(i, px, py, pc):
            return srcs[i] if modes[i] == "gather" else srcs[i].at[slot(i, px, py, pc)]

        local = [pltpu.make_async_copy(src_of(i, x, y, c), dsts[i].at[slot(i, x, y, c)], loc_sems.at[i])
                 for i in range(n)]
        for cp in local:
            cp.start()
        sends, recvs = [], []
        for r in (1, 2, 4, 3, 5, 6, 7):
            px = 1 - x if r & 4 else x
            py = 1 - y if r & 2 else y
            pc = 1 - c if r & 1 else c
            for i in range(n):
                if modes[i] == "chips" and r & 1:
                    continue
                sems = dict(send_sem=send_sems.at[i, r - 1], recv_sem=recv_sems.at[i, r - 1],
                            device_id=(px, py, pc), device_id_type=pl.DeviceIdType.MESH)
                sends.append(pltpu.make_async_remote_copy(
                    src_ref=src_of(i, px, py, pc), dst_ref=dsts[i].at[slot(i, x, y, c)], **sems))
                recvs.append(pltpu.make_async_remote_copy(
                    src_ref=src_of(i, px, py, pc), dst_ref=dsts[i].at[slot(i, px, py, pc)], **sems))
        for cp in sends:
            cp.start()
        for cp in recvs:
            cp.wait_recv()
        for cp in sends:
            cp.wait_send()
        for cp in local:
            cp.wait()

    return pl.pallas_call(
        body, name=name, out_shape=out_shape,
        in_specs=[ANY_SPEC] * n, out_specs=[ANY_SPEC] * n,
        scratch_shapes=[pltpu.SemaphoreType.DMA((n, N_DEV - 1)), pltpu.SemaphoreType.DMA((n, N_DEV - 1)),
                        pltpu.SemaphoreType.DMA((n,))],
    )(*arrays)


def _gather_two_level(arrays, name):
    n = len(arrays)
    out_shape = [jax.ShapeDtypeStruct((N_DEV,) + a.shape, a.dtype) for a in arrays]

    def body(*refs):
        srcs, dsts = refs[:n], refs[n:2 * n]
        send_sems, recv_sems, loc_sems = refs[2 * n:]
        x, y, c = lax.axis_index("x"), lax.axis_index("y"), lax.axis_index("c")
        me, sibling = (x, y, c), (x, y, 1 - c)
        chips = [(1 - x, y), (x, 1 - y), (1 - x, 1 - y)]

        def rows(i, dev):
            return dsts[i].at[4 * dev[0] + 2 * dev[1] + dev[2]]

        def copy(i, k, block, to, src=None):
            return pltpu.make_async_remote_copy(
                src_ref=rows(i, block) if src is None else src, dst_ref=rows(i, block),
                send_sem=send_sems.at[i, k], recv_sem=recv_sems.at[i, k],
                device_id=to, device_id_type=pl.DeviceIdType.MESH)

        local = [pltpu.make_async_copy(srcs[i], rows(i, me), loc_sems.at[i]) for i in range(n)]
        for cp in local:
            cp.start()
        first = []
        for i in range(n):
            first.append(copy(i, 0, me, sibling, src=srcs[i]))
            first += [copy(i, 1 + j, me, (*chip, c), src=srcs[i]) for j, chip in enumerate(chips)]
        for cp in first:
            cp.start()
        passed = []
        for j, chip in enumerate(chips):
            for i in range(n):
                copy(i, 1 + j, (*chip, c), me).wait_recv()
                cp = copy(i, 4 + j, (*chip, c), sibling)
                cp.start()
                passed.append(cp)
        for i in range(n):
            copy(i, 0, sibling, me).wait_recv()
            for j, chip in enumerate(chips):
                copy(i, 4 + j, (*chip, 1 - c), me).wait_recv()
        for cp in first + passed:
            cp.wait_send()
        for cp in local:
            cp.wait()

    return pl.pallas_call(
        body, name=name, out_shape=out_shape,
        in_specs=[ANY_SPEC] * n, out_specs=[ANY_SPEC] * n,
        scratch_shapes=[pltpu.SemaphoreType.DMA((n, N_DEV - 1)), pltpu.SemaphoreType.DMA((n, N_DEV - 1)),
                        pltpu.SemaphoreType.DMA((n,))],
    )(*arrays)


def _swap_sibling(arrays, name):
    n = len(arrays)
    out_shape = [jax.ShapeDtypeStruct(a.shape[1:], a.dtype) for a in arrays]

    def body(*refs):
        srcs, dsts = refs[:n], refs[n:2 * n]
        send_sems, recv_sems = refs[2 * n:]
        x, y, c = lax.axis_index("x"), lax.axis_index("y"), lax.axis_index("c")
        copies = [pltpu.make_async_remote_copy(
            src_ref=srcs[i].at[1 - c], dst_ref=dsts[i], send_sem=send_sems.at[i], recv_sem=recv_sems.at[i],
            device_id=(x, y, 1 - c), device_id_type=pl.DeviceIdType.MESH) for i in range(n)]
        for cp in copies:
            cp.start()
        for cp in copies:
            cp.wait()

    return pl.pallas_call(
        body, name=name, out_shape=out_shape, in_specs=[ANY_SPEC] * n, out_specs=[ANY_SPEC] * n,
        scratch_shapes=[pltpu.SemaphoreType.DMA((n,)), pltpu.SemaphoreType.DMA((n,))],
    )(*arrays)


def _chip_partial(mine, theirs, tr, name):
    k, rws, cols = mine.shape

    def body(a_ref, b_ref, f_ref, h_ref):
        sm = a_ref[...] + b_ref[...]
        f_ref[...] = sm
        h_ref[...] = sm.astype(BF16)

    blk = pl.BlockSpec((1, tr, cols), lambda j, i: (j, i, 0))
    return pl.pallas_call(
        body, name=name, grid=(k, rws // tr),
        out_shape=[jax.ShapeDtypeStruct(mine.shape, F32), jax.ShapeDtypeStruct(mine.shape, BF16)],
        in_specs=[blk, blk], out_specs=[blk, blk], compiler_params=_params(2))(mine, theirs)


def _ada_fwd(c_all, w_shard, b_shard):
    def body(c_ref, w_ref, b_ref, o_ref):
        ch, cm, cl = [t.astype(BF16) for t in _split3(c_ref[...])]
        wh, wm, wl = [t.astype(BF16) for t in _split3(w_ref[...])]
        acc = _dot(ch, wh) + _dot(ch, wm) + _dot(cm, wh) + _dot(ch, wl) + _dot(cl, wh) + _dot(cm, wm)
        o_ref[...] = acc + b_ref[...]

    return pl.pallas_call(body, name="ada_fwd", out_shape=jax.ShapeDtypeStruct((N_DEV, ADA_SHARD), F32),
                          compiler_params=_params(0))(c_all, w_shard, b_shard)


def _rope_tables(pos_col, inv_freq, tm):
    s = pos_col.shape[0]

    def body(p_ref, f_ref, cos_ref, sa_ref, sb_ref):
        ang = p_ref[...] * f_ref[...]
        sin = jnp.sin(ang)
        first_half = (_lane(CHUNK) & (HEAD_DIM - 1)) < HEAD_DIM // 2
        cos_ref[...] = jnp.cos(ang)
        sa_ref[...] = jnp.where(first_half, -sin, 0.0)
        sb_ref[...] = jnp.where(first_half, 0.0, sin)

    tab = jax.ShapeDtypeStruct((s, CHUNK), F32)
    blk = pl.BlockSpec((tm, CHUNK), lambda i: (i, 0))
    return pl.pallas_call(
        body, name="rope_tables", grid=(s // tm,), out_shape=[tab, tab, tab],
        in_specs=[pl.BlockSpec((tm, 1), lambda i: (i, 0)), pl.BlockSpec((1, CHUNK), lambda i: (0, 0))],
        out_specs=[blk, blk, blk], compiler_params=_params(1))(pos_col, inv_freq)


def _fwd_proj(x, mod, w_all, cos, sa, sb, bf_row, emat, ind, tm):
    s = x.shape[0]

    def body(x_ref, mod_ref, w_ref, cos_ref, sa_ref, sb_ref, bf_ref, e_ref, ind_ref,
             h_ref, qa_ref, ka_ref, va_ref, za_ref, qb_ref, kb_ref, vb_ref, zb_ref, ga_ref, gb_ref, f_ref,
             bnd_ref, nrm_ref, carry_ref):
        @pl.when(pl.program_id(0) == 0)
        def _():
            carry_ref[...] = jnp.zeros_like(carry_ref)

        xv = x_ref[...]
        rstd = lax.rsqrt(jnp.mean(xv * xv, axis=-1, keepdims=True) + NORM_EPS)
        h = (xv * rstd * mod_ref[0:1, :]) * (1.0 + mod_ref[1:2, :]) + mod_ref[2:3, :]
        hb = h.astype(BF16)
        h_ref[...] = hb

        def seg(name):
            off, w = SEG[name]
            return _dot(hb, w_ref[:, off:off + w])

        lane = _lane(CHUNK)
        low = lane < HEAD_DIM
        q_ones = _ones_lanes(64, 67)
        k_ones = _ones_lanes(67, 70)
        cos_t, sa_t, sb_t = cos_ref[...], sa_ref[...], sb_ref[...]

        def write_heads(ref, nat, extra_of, rope, scale):
            for p in range(N_HEADS // 2):
                c = nat[:, CHUNK * p:CHUNK * (p + 1)]
                if rope:
                    c = _rope(c, cos_t, sa_t, sb_t)
                if scale != 1.0:
                    c = c * scale
                for hh in range(2):
                    hd = 2 * p + hh
                    src = c if hh == 0 else pltpu.roll(c, HEAD_DIM, 1)
                    ref[:, CHUNK * hd:CHUNK * (hd + 1)] = jnp.where(low, src, extra_of(hd)).astype(BF16)

        write_heads(qa_ref, seg("qa"), lambda hd: q_ones, True, Q_SCALE)
        ka = seg("ka")
        va = seg("va")
        for kv in range(KV_GROUPS):
            sl = slice(CHUNK * kv, CHUNK * (kv + 1))
            ka_ref[:, sl] = jnp.where(low, _rope(ka[:, sl], cos_t, sa_t, sb_t), k_ones).astype(BF16)
            va_ref[:, sl] = jnp.where(low, va[:, sl], q_ones).astype(BF16)
        za_ref[...] = seg("za")
        zb_ref[...] = seg("zb")
        ga_ref[...] = seg("ga")
        gb_ref[...] = seg("gb")

        xf = seg("f") + bf_ref[...]
        f_ref[...] = xf
        logf = jnp.minimum(xf, 0.0) - jnp.log1p(jnp.exp(-jnp.abs(xf)))
        row = lax.broadcasted_iota(jnp.int32, (tm, tm), 0)
        col = lax.broadcasted_iota(jnp.int32, (tm, tm), 1)
        tri = jnp.where(col <= row, 1.0, 0.0).astype(BF16)
        hi, mid, lo = _split3(logf)
        cum = _dot(tri, hi.astype(BF16)) + _dot(tri, mid.astype(BF16)) + _dot(tri, lo.astype(BF16))
        cum = cum + carry_ref[0:1, :]
        carry_ref[0:1, :] += jnp.sum(logf, axis=0, keepdims=True)
        spread = _dot_split(cum, e_ref[...])
        lane_all = _lane(AUG_W) & (CHUNK - 1)
        ones_all = jnp.where((lane_all >= 67) & (lane_all < 70), 1.0, 0.0)
        bias = _place3(lane_all, 64, -spread, ones_all)

        qb, kb = seg("qb"), seg("kb")
        write_heads(qb_ref, qb, lambda hd: q_ones, False, Q_SCALE)
        write_heads(kb_ref, kb, lambda hd: bias[:, CHUNK * hd:CHUNK * (hd + 1)], False, 1.0)
        write_heads(vb_ref, seg("vb"), lambda hd: q_ones, False, 1.0)

        rid = lax.broadcasted_iota(jnp.int32, (tm, CHUNK), 0)
        bnd_ref[...] = jnp.zeros_like(bnd_ref)
        bnd_ref[0, 0:1, :] = jnp.sum(jnp.where(rid == 0, cum, 0.0), axis=0, keepdims=True)
        bnd_ref[0, 1:2, :] = jnp.sum(jnp.where(rid == tm - 1, cum, 0.0), axis=0, keepdims=True)

        @pl.when(pl.program_id(0) == 0)
        def _():
            nrm_ref[...] = jnp.zeros_like(nrm_ref)

        for r_, nat in ((0, qb * Q_SCALE), (1, kb)):
            sq = _dot((nat * nat).astype(BF16), ind_ref[...])
            nrm_ref[r_:r_ + 1, :] = jnp.maximum(nrm_ref[r_:r_ + 1, :], jnp.max(sq, axis=0, keepdims=True))

    row_blk = lambda w: pl.BlockSpec((tm, w), lambda i: (i, 0))
    full = lambda a: pl.BlockSpec(a.shape, lambda i: (0,) * a.ndim)
    sds = lambda w, dt: jax.ShapeDtypeStruct((s, w), dt)
    out_shape = [sds(D_MODEL, BF16), sds(AUG_W, BF16), sds(KV_GROUPS * CHUNK, BF16), sds(KV_GROUPS * CHUNK, BF16),
                 sds(512, F32), sds(AUG_W, BF16), sds(AUG_W, BF16), sds(AUG_W, BF16), sds(512, F32),
                 sds(D_MODEL, F32), sds(D_MODEL, F32), sds(CHUNK, F32)]
    small = [jax.ShapeDtypeStruct((s // tm, 8, CHUNK), F32), jax.ShapeDtypeStruct((8, CHUNK), F32)]
    return pl.pallas_call(
        body, name="fwd_proj", grid=(s // tm,), out_shape=out_shape + small,
        in_specs=[row_blk(D_MODEL), full(mod), full(w_all), row_blk(CHUNK), row_blk(CHUNK), row_blk(CHUNK),
                  full(bf_row), full(emat), full(ind)],
        out_specs=[row_blk(o.shape[1]) for o in out_shape] + [pl.BlockSpec((1, 8, CHUNK), lambda i: (i, 0, 0)),
                                                                 pl.BlockSpec((8, CHUNK), lambda i: (0, 0))],
        scratch_shapes=[pltpu.VMEM((8, CHUNK), F32)],
        compiler_params=_params(1))(x, mod, w_all, cos, sa, sb, bf_row, emat, ind)


def _swa_fwd(q_aug, k_aug, v_aug, sink_rows, tq):
    s = q_aug.shape[0]
    r = tq // WINDOW
    gw = GROUP * CHUNK

    def body(q_ref, kc_ref, kp_ref, vc_ref, vp_ref, sink_ref, o_ref, qb_ref):
        i = pl.program_id(1)
        st = GROUP * WINDOW
        qloc = lax.broadcasted_iota(jnp.int32, (st, 2 * WINDOW), 0) & (WINDOW - 1)
        col = lax.broadcasted_iota(jnp.int32, (st, 2 * WINDOW), 1)
        band = (col > qloc) & (col <= qloc + WINDOW)
        head = jnp.right_shift(lax.broadcasted_iota(jnp.int32, (st, 1), 0), 7)
        sink = jnp.zeros((st, 1), F32)
        for g in range(GROUP):
            sink = jnp.where(head == g, jnp.max(sink_ref[0, g:g + 1, :], axis=1, keepdims=True), sink)
        lane = _lane(CHUNK)
        for sub in range(r):
            rows = slice(WINDOW * sub, WINDOW * (sub + 1))
            q = jnp.concatenate([q_ref[rows, CHUNK * g:CHUNK * (g + 1)] for g in range(GROUP)], axis=0)
            if sub == 0:
                k = jnp.concatenate([kp_ref[...], kc_ref[rows, :]], axis=0)
                v = jnp.concatenate([vp_ref[...], vc_ref[rows, :]], axis=0)
                valid = band & ((col >= WINDOW) | (i > 0))
            else:
                both = slice(WINDOW * (sub - 1), WINDOW * (sub + 1))
                k, v, valid = kc_ref[both, :], vc_ref[both, :], band
            sc = jnp.where(valid, _dot_nt(q, k), NEG)
            m = jnp.maximum(jnp.max(sc, axis=1, keepdims=True), sink)
            acc = _dot(jnp.exp(sc - m).astype(BF16), v)
            denom = _lane_sum(acc, 64, 65) + jnp.exp(sink - m)
            out = acc / denom
            aug = _place3(lane, 67, -(m + jnp.log(denom)), q.astype(F32)).astype(BF16)
            hrows = lambda a, g: a[WINDOW * g:WINDOW * (g + 1), :]
            for g in range(GROUP):
                qb_ref[rows, CHUNK * g:CHUNK * (g + 1)] = hrows(aug, g)
            for pp in range(GROUP // 2):
                o_ref[rows, CHUNK * pp:CHUNK * (pp + 1)] = _pair(hrows(out, 2 * pp), hrows(out, 2 * pp + 1))

    return pl.pallas_call(
        body, name="swa_fwd", grid=(KV_GROUPS, s // tq),
        out_shape=[jax.ShapeDtypeStruct((s, 512), F32), jax.ShapeDtypeStruct((s, AUG_W), BF16)],
        in_specs=[pl.BlockSpec((tq, gw), lambda kv, i: (i, kv)),
                  pl.BlockSpec((tq, CHUNK), lambda kv, i: (i, kv)),
                  pl.BlockSpec((WINDOW, CHUNK), lambda kv, i: (jnp.maximum(i * r - 1, 0), kv)),
                  pl.BlockSpec((tq, CHUNK), lambda kv, i: (i, kv)),
                  pl.BlockSpec((WINDOW, CHUNK), lambda kv, i: (jnp.maximum(i * r - 1, 0), kv)),
                  pl.BlockSpec((1, 8, CHUNK), lambda kv, i: (kv, 0, 0))],
        out_specs=[pl.BlockSpec((tq, GROUP * HEAD_DIM), lambda kv, i: (i, kv)),
                   pl.BlockSpec((tq, gw), lambda kv, i: (i, kv))],
        compiler_params=_params(2))(q_aug, k_aug, k_aug, v_aug, v_aug, sink_rows)


def _swa_bwd(k_aug, v_aug, q_bwd, do_aug, cos, sa, sb, sink_rows, tk):
    s = k_aug.shape[0]
    r = tk // WINDOW
    nt = s // tk
    nb = s // WINDOW
    gw = GROUP * CHUNK

    def body(k_ref, v_ref, q_ref, qn_ref, do_ref, don_ref, cos_ref, sa_ref, sb_ref, sink_ref,
             da_ref, dsink_ref, carry_ref, acc_ref):
        j = pl.program_id(0)

        @pl.when(j == 0)
        def _():
            carry_ref[...] = jnp.zeros_like(carry_ref)
            dsink_ref[...] = jnp.zeros_like(dsink_ref)

        acc_ref[...] = jnp.zeros_like(acc_ref)
        acc_ref[0:WINDOW, :] = carry_ref[...]
        dk_col, dv_col = SEG["ka"][0] - SEG["qa"][0], SEG["va"][0] - SEG["qa"][0]
        st = GROUP * WINDOW
        key = lax.broadcasted_iota(jnp.int32, (WINDOW, 2 * st), 0)
        col = lax.broadcasted_iota(jnp.int32, (WINDOW, 2 * st), 1)
        qloc = col & (WINDOW - 1)
        band = ((col < st) & (key <= qloc)) | ((col >= st) & (key > qloc))
        lane = _lane(CHUNK)
        low = lane < HEAD_DIM
        cos_t, sa_t, sb_t = cos_ref[...], sa_ref[...], sb_ref[...]
        for kv, sub in [(kv, sub) for kv in range(KV_GROUPS) for sub in range(r)]:
            heads = [slice(CHUNK * (GROUP * kv + g), CHUNK * (GROUP * kv + g + 1)) for g in range(GROUP)]
            kvs = slice(CHUNK * kv, CHUNK * (kv + 1))
            rows = slice(WINDOW * sub, WINDOW * (sub + 1))
            nxt = slice(WINDOW * (sub + 1), WINDOW * (sub + 2))
            k, v = k_ref[rows, kvs], v_ref[rows, kvs]
            q_cur, do_cur = [q_ref[rows, cs] for cs in heads], [do_ref[rows, cs] for cs in heads]
            if sub < r - 1:
                q_nxt, do_nxt, valid = [q_ref[nxt, cs] for cs in heads], [do_ref[nxt, cs] for cs in heads], band
            else:
                q_nxt, do_nxt = [qn_ref[:, cs] for cs in heads], [don_ref[:, cs] for cs in heads]
                valid = band & ((col < st) | (j < nt - 1))
            q = jnp.concatenate(q_cur + q_nxt, axis=0)
            do = jnp.concatenate(do_cur + do_nxt, axis=0)
            pt = jnp.exp(jnp.where(valid, _dot_nt(k, q), NEG))
            ds = (pt * _dot_nt(v, do)).astype(BF16)
            dv = _dot(pt.astype(BF16), do)
            dk = _dot(ds, q)
            dq = _dot_tn(ds, k)
            for g, cs in enumerate(heads):
                acc_ref[rows, cs] += dq[WINDOW * g:WINDOW * (g + 1), :]
                dqn = dq[st + WINDOW * g:st + WINDOW * (g + 1), :]
                if sub < r - 1:
                    acc_ref[nxt, cs] += dqn
                else:
                    carry_ref[:, cs] = dqn
                sink = jnp.max(sink_ref[kv, g:g + 1, :], axis=1, keepdims=True)
                p_sink = jnp.exp(sink + _lane_sum(q_cur[g].astype(F32), 67, 70))
                term = jnp.sum(p_sink * _lane_sum(do_cur[g].astype(F32), 64, 67), axis=0, keepdims=True)
                dsink_ref[kv, g:g + 1, :] += jnp.broadcast_to(term, (1, CHUNK))
            da_ref[rows, dk_col + CHUNK * kv:dk_col + CHUNK * (kv + 1)] = _rope_inv(
                jnp.where(low, dk, 0.0), cos_t[rows, :], sa_t[rows, :], sb_t[rows, :]).astype(BF16)
            da_ref[rows, dv_col + CHUNK * kv:dv_col + CHUNK * (kv + 1)] = jnp.where(low, dv, 0.0).astype(BF16)
        for pp in range(N_HEADS // 2):
            d = _pair(acc_ref[:, CHUNK * 2 * pp:CHUNK * (2 * pp + 1)], acc_ref[:, CHUNK * (2 * pp + 1):CHUNK * (2 * pp + 2)])
            da_ref[:, CHUNK * pp:CHUNK * (pp + 1)] = (_rope_inv(d, cos_t, sa_t, sb_t) * Q_SCALE).astype(BF16)

    cur = lambda w: pl.BlockSpec((tk, w), lambda j: (j, 0))
    nxt = pl.BlockSpec((WINDOW, AUG_W), lambda j: (jnp.minimum((j + 1) * r, nb - 1), 0))
    whole = pl.BlockSpec((KV_GROUPS, 8, CHUNK), lambda j: (0, 0, 0))
    return pl.pallas_call(
        body, name="swa_bwd", grid=(nt,),
        out_shape=[jax.ShapeDtypeStruct((s, 1024), BF16), jax.ShapeDtypeStruct((KV_GROUPS, 8, CHUNK), F32)],
        in_specs=[cur(KV_GROUPS * CHUNK), cur(KV_GROUPS * CHUNK), cur(AUG_W), nxt, cur(AUG_W), nxt,
                  cur(CHUNK), cur(CHUNK), cur(CHUNK), whole],
        out_specs=[cur(1024), whole],
        scratch_shapes=[pltpu.VMEM((WINDOW, AUG_W), F32), pltpu.VMEM((tk, AUG_W), F32)],
        compiler_params=_params(1))(k_aug, v_aug, q_bwd, q_bwd, do_aug, do_aug, cos, sa, sb, sink_rows)


def _fox_fwd(first_key, q_aug, k_aug, v_aug, t):
    s = q_aug.shape[0]
    pw = 2 * CHUNK

    def body(lo_ref, q_ref, k_ref, v_ref, o_ref, qb_ref):
        i = pl.program_id(1)
        first = lo_ref[pl.program_id(0), i]
        row = lax.broadcasted_iota(jnp.int32, (t, t), 0)
        col = lax.broadcasted_iota(jnp.int32, (t, t), 1)
        lane = _lane(CHUNK)
        heads = (slice(0, CHUNK), slice(CHUNK, pw))
        qs = [q_ref[:, hs] for hs in heads]

        def step(jb, carry, masked):
            rows = pl.ds(pl.multiple_of(jb * t, t), t)
            new = []
            for hh, hs in enumerate(heads):
                m, acc = carry[2 * hh], carry[2 * hh + 1]
                sc = _dot_nt(qs[hh], k_ref[rows, hs])
                if masked:
                    sc = jnp.where(col <= row, sc, NEG)
                m_new = jnp.maximum(m, jnp.max(sc, axis=1, keepdims=True))
                p = jnp.exp(sc - m_new).astype(BF16)
                new += [m_new, jnp.exp(m - m_new) * acc + _dot(p, v_ref[rows, hs])]
            return tuple(new)

        init = (jnp.full((t, 1), NEG, F32), jnp.zeros((t, CHUNK), F32)) * 2
        carry = step(i, lax.fori_loop(first, i, lambda jb, c: step(jb, c, False), init), True)
        outs = []
        for hh, hs in enumerate(heads):
            m, acc = carry[2 * hh], carry[2 * hh + 1]
            denom = _lane_sum(acc, 64, 65)
            outs.append(acc / denom)
            lse = m + jnp.log(denom)
            qb_ref[:, hs] = _place3(lane, 67, -lse, qs[hh].astype(F32)).astype(BF16)
        o_ref[...] = _pair(outs[0], outs[1])

    return pl.pallas_call(
        body, name="fox_fwd", grid=(N_HEADS // 2, s // t),
        out_shape=[jax.ShapeDtypeStruct((s, 512), F32), jax.ShapeDtypeStruct((s, AUG_W), BF16)],
        in_specs=[pl.BlockSpec(memory_space=pltpu.SMEM),
                  pl.BlockSpec((t, pw), lambda p, i: (i, p)),
                  pl.BlockSpec((s, pw), lambda p, i: (0, p)),
                  pl.BlockSpec((s, pw), lambda p, i: (0, p))],
        out_specs=[pl.BlockSpec((t, CHUNK), lambda p, i: (i, p)), pl.BlockSpec((t, pw), lambda p, i: (i, p))],
        compiler_params=_params(2))(first_key, q_aug, k_aug, v_aug)


def _fox_block_ranges(bounds, norms, r):
    cum_first = bounds[0::r, 0, :N_HEADS]
    cum_last = bounds[r - 1::r, 1, :N_HEADS]
    n = cum_first.shape[0]
    reach = 2.0 * 1.02 * jnp.sqrt(norms[0, :N_HEADS] * norms[1, :N_HEADS]) + 105.0
    blk = jnp.arange(n, dtype=jnp.int32)
    decay = cum_last[None, :, :] - cum_first[:, None, :]
    skip = (decay > reach[None, None, :]) & (blk[None, :, None] < blk[:, None, None])
    first = jnp.min(jnp.where(skip, n, blk[None, :, None]), axis=1)
    first = jnp.minimum(first[:, 0::2], first[:, 1::2]).T
    needed = (first[:, :, None] <= blk[None, None, :]) & (blk[None, :, None] >= blk[None, None, :])
    last = jnp.max(jnp.where(needed, blk[None, :, None], blk[None, None, :]), axis=1)
    return first.astype(jnp.int32), last.astype(jnp.int32)


def _fox_bwd(last_query, k_aug, v_aug, q_bwd, do_aug, t):
    s = k_aug.shape[0]
    n = s // t
    pw = 2 * CHUNK

    def body(hi_ref, k_ref, v_ref, q_ref, do_ref, dk_ref, dv_ref, dck_ref, dq_ref, dcq_ref, dq_scr):
        j = pl.program_id(1)
        last = hi_ref[pl.program_id(0), j]

        @pl.when(j == 0)
        def _():
            dq_scr[...] = jnp.zeros_like(dq_scr)

        row = lax.broadcasted_iota(jnp.int32, (t, t), 0)
        col = lax.broadcasted_iota(jnp.int32, (t, t), 1)
        lane = _lane(CHUNK)
        heads = (slice(0, CHUNK), slice(CHUNK, pw))
        ks = [k_ref[:, hs] for hs in heads]
        vs = [v_ref[:, hs] for hs in heads]

        def step(ib, carry, masked):
            rows = pl.ds(pl.multiple_of(ib * t, t), t)
            new = []
            for hh, hs in enumerate(heads):
                dv, dk = carry[2 * hh], carry[2 * hh + 1]
                q, do = q_ref[rows, hs], do_ref[rows, hs]
                st = _dot_nt(ks[hh], q)
                if masked:
                    st = jnp.where(row <= col, st, NEG)
                pt = jnp.exp(st)
                ds = (pt * _dot_nt(vs[hh], do)).astype(BF16)
                new += [dv + _dot(pt.astype(BF16), do), dk + _dot(ds, q)]
                dq_scr[rows, hs] += _dot_tn(ds, ks[hh])
            return tuple(new)

        zero = jnp.zeros((t, CHUNK), F32)
        carry = lax.fori_loop(j + 1, last + 1, lambda ib, c: step(ib, c, False), step(j, (zero,) * 4, True))
        dvs, dks = (carry[0], carry[2]), (carry[1], carry[3])
        dk_ref[...] = _pair(dks[0], dks[1]).astype(BF16)
        dv_ref[...] = _pair(dvs[0], dvs[1]).astype(BF16)
        dck_ref[0] = jnp.where(lane == 0, pltpu.roll(dks[0], 64, 1),
                               jnp.where(lane == 1, pltpu.roll(dks[1], 65, 1), 0.0))

        @pl.when(j == n - 1)
        def _():
            for ib in range(n):
                rows = slice(t * ib, t * (ib + 1))
                d0, d1 = dq_scr[rows, 0:CHUNK], dq_scr[rows, CHUNK:pw]
                dq_ref[rows, :] = (_pair(d0, d1) * Q_SCALE).astype(BF16)
                dcq_ref[0, rows, :] = jnp.where(lane == 0, pltpu.roll(d0, CHUNK - 67, 1),
                                                jnp.where(lane == 1, pltpu.roll(d1, CHUNK - 66, 1), 0.0))

    return pl.pallas_call(
        body, name="fox_bwd", grid=(N_HEADS // 2, n),
        out_shape=[jax.ShapeDtypeStruct((s, 512), BF16), jax.ShapeDtypeStruct((s, 512), BF16),
                   jax.ShapeDtypeStruct((N_HEADS // 2, s, CHUNK), F32), jax.ShapeDtypeStruct((s, 512), BF16),
                   jax.ShapeDtypeStruct((N_HEADS // 2, s, CHUNK), F32)],
        in_specs=[pl.BlockSpec(memory_space=pltpu.SMEM),
                  pl.BlockSpec((t, pw), lambda p, j: (j, p)), pl.BlockSpec((t, pw), lambda p, j: (j, p)),
                  pl.BlockSpec((s, pw), lambda p, j: (0, p)), pl.BlockSpec((s, pw), lambda p, j: (0, p))],
        out_specs=[pl.BlockSpec((t, CHUNK), lambda p, j: (j, p)), pl.BlockSpec((t, CHUNK), lambda p, j: (j, p)),
                   pl.BlockSpec((1, t, CHUNK), lambda p, j: (p, j, 0)), pl.BlockSpec((s, CHUNK), lambda p, j: (0, p)),
                   pl.BlockSpec((1, s, CHUNK), lambda p, j: (p, 0, 0))],
        scratch_shapes=[pltpu.VMEM((s, pw), F32)],
        compiler_params=_params(2))(last_query, k_aug, v_aug, q_bwd, do_aug)


def _fgate_bwd(dcq, dck, xf, tm):
    s = xf.shape[0]
    nt = s // tm

    def body(dq_ref, dc_ref, xf_ref, df_ref, dbf_ref, carry_ref):
        @pl.when(pl.program_id(0) == 0)
        def _():
            carry_ref[...] = jnp.zeros_like(carry_ref)
            dbf_ref[...] = jnp.zeros_like(dbf_ref)

        row = lax.broadcasted_iota(jnp.int32, (tm, tm), 0)
        col = lax.broadcasted_iota(jnp.int32, (tm, tm), 1)
        tri = jnp.where(col >= row, 1.0, 0.0).astype(BF16)
        dcum = dq_ref[...] - dc_ref[...]
        hi, mid, lo = _split3(dcum)
        dlogf = _dot(tri, hi.astype(BF16)) + _dot(tri, mid.astype(BF16)) + _dot(tri, lo.astype(BF16))
        dlogf = dlogf + carry_ref[0:1, :]
        carry_ref[0:1, :] += jnp.sum(dcum, axis=0, keepdims=True)
        df = dlogf * _sigmoid(-xf_ref[...])
        df_ref[...] = df.astype(BF16)
        dbf_ref[0:1, :] += jnp.sum(df, axis=0, keepdims=True)

    rev = pl.BlockSpec((tm, CHUNK), lambda i: (nt - 1 - i, 0))
    return pl.pallas_call(
        body, name="fgate_bwd", grid=(nt,),
        out_shape=[jax.ShapeDtypeStruct((s, CHUNK), BF16), jax.ShapeDtypeStruct((8, CHUNK), F32)],
        in_specs=[rev, rev, rev], out_specs=[rev, pl.BlockSpec((8, CHUNK), lambda i: (0, 0))],
        scratch_shapes=[pltpu.VMEM((8, CHUNK), F32)],
        compiler_params=_params(1))(dcq, dck, xf)


def _post(x, tgt, att_a, att_b, za, zb, ga, gb, woa_t, wob_t, wout, vec, ind, emat, tm):
    s = x.shape[0]

    def body(x_ref, t_ref, aa_ref, ab_ref, za_ref, zb_ref, ga_ref, gb_ref, woat_ref, wobt_ref, wout_ref,
             vec_ref, ind_ref, e_ref,
             dx1_ref, ua_ref, ub_ref, mg_ref, do_ref, dya_ref, dyb_ref, doa_ref, dob_ref, dzg_ref, acc_ref):
        @pl.when(pl.program_id(0) == 0)
        def _():
            acc_ref[...] = jnp.zeros_like(acc_ref)

        gate, gfin = vec_ref[0:1, :], vec_ref[1:2, :]
        inv_d = 1.0 / D_MODEL

        def branch_fwd(att_ref, z_ref, w_ref, u_ref):
            att, z = att_ref[...], z_ref[...]
            sz = _sigmoid(z)
            silu = z * sz
            u = (att * silu).astype(BF16)
            u_ref[...] = u
            return att, z, sz, silu, _dot_nt(u, w_ref[...])

        att_a, z_a, sz_a, silu_a, y_a = branch_fwd(aa_ref, za_ref, woat_ref, ua_ref)
        att_b, z_b, sz_b, silu_b, y_b = branch_fwd(ab_ref, zb_ref, wobt_ref, ub_ref)
        sg_a, sg_b = _sigmoid(ga_ref[...]), _sigmoid(gb_ref[...])
        merged = (sg_a * y_a + sg_b * y_b).astype(BF16)
        mg_ref[...] = merged
        o = _dot(merged, wout_ref[...])
        x1 = x_ref[...] + gate * o
        rstd = lax.rsqrt(jnp.mean(x1 * x1, axis=-1, keepdims=True) + NORM_EPS)
        xh = x1 * rstd
        diff = xh * gfin - t_ref[...]
        acc_ref[2:3, :] += (0.5 * inv_d) * jnp.sum(diff * diff, axis=0, keepdims=True)
        dy = diff * inv_d
        acc_ref[1:2, :] += jnp.sum(dy * xh, axis=0, keepdims=True)
        dyg = dy * gfin
        dx1 = rstd * (dyg - xh * jnp.mean(dyg * xh, axis=-1, keepdims=True))
        dx1_ref[...] = dx1
        acc_ref[0:1, :] += jnp.sum(dx1 * o, axis=0, keepdims=True)
        d_o = (dx1 * gate).astype(BF16)
        do_ref[...] = d_o
        dmg = _dot_nt(d_o, wout_ref[...])

        lane = _lane(CHUNK)
        low = lane < HEAD_DIM
        lane_all = _lane(AUG_W) & (CHUNK - 1)

        zg0 = SEG["za"][0]

        def branch_bwd(sg, y, att, z, sz, silu, wt_ref, dy_ref, g_name, z_name, doaug_ref):
            dyb = (dmg * sg).astype(BF16)
            dy_ref[...] = dyb
            g_off, z_off = SEG[g_name][0] - zg0, SEG[z_name][0] - zg0
            dzg_ref[:, g_off:g_off + D_MODEL] = (dmg * y * sg * (1.0 - sg)).astype(BF16)
            du = _dot(dyb, wt_ref[...])
            datt = du * silu
            dzg_ref[:, z_off:z_off + 512] = (du * att * (sz * (1.0 + z * (1.0 - sz)))).astype(BF16)
            delta = _dot_split(datt * att, ind_ref[...])
            extra = _place3(lane_all, 64, -_dot_split(delta, e_ref[...]), 0.0)
            for p in range(N_HEADS // 2):
                c = datt[:, CHUNK * p:CHUNK * (p + 1)]
                for hh in range(2):
                    hd = 2 * p + hh
                    src = c if hh == 0 else pltpu.roll(c, HEAD_DIM, 1)
                    doaug_ref[:, CHUNK * hd:CHUNK * (hd + 1)] = jnp.where(
                        low, src, extra[:, CHUNK * hd:CHUNK * (hd + 1)]).astype(BF16)

        branch_bwd(sg_a, y_a, att_a, z_a, sz_a, silu_a, woat_ref, dya_ref, "ga", "za", doa_ref)
        branch_bwd(sg_b, y_b, att_b, z_b, sz_b, silu_b, wobt_ref, dyb_ref, "gb", "zb", dob_ref)

    row_blk = lambda w: pl.BlockSpec((tm, w), lambda i: (i, 0))
    full = lambda a: pl.BlockSpec(a.shape, lambda i: (0,) * a.ndim)
    sds = lambda w, dt: jax.ShapeDtypeStruct((s, w), dt)
    out_shape = [sds(D_MODEL, F32), sds(512, BF16), sds(512, BF16), sds(D_MODEL, BF16), sds(D_MODEL, BF16),
                 sds(D_MODEL, BF16), sds(D_MODEL, BF16), sds(AUG_W, BF16), sds(AUG_W, BF16), sds(3072, BF16),
                 jax.ShapeDtypeStruct((8, D_MODEL), F32)]
    ins = [x, tgt, att_a, att_b, za, zb, ga, gb, woa_t, wob_t, wout, vec, ind, emat]
    in_specs = [row_blk(a.shape[1]) for a in ins[:8]] + [full(a) for a in ins[8:]]
    out_specs = [row_blk(o.shape[1]) for o in out_shape[:-1]] + [pl.BlockSpec((8, D_MODEL), lambda i: (0, 0))]
    return pl.pallas_call(body, name="post", grid=(s // tm,), out_shape=out_shape, in_specs=in_specs,
                          out_specs=out_specs, compiler_params=_params(1))(*ins)


def _bwd_pre(dproj, w_all, x, dx1, mod, tm):
    s = x.shape[0]
    ng = len(DPROJ_GROUPS)

    def body(*refs):
        dp_refs = refs[:ng]
        wt_ref, x_ref, dx1_ref, mod_ref, gx_ref, acc_ref = refs[ng:]

        @pl.when(pl.program_id(0) == 0)
        def _():
            acc_ref[...] = jnp.zeros_like(acc_ref)

        dh = None
        for dp_ref, (_, off, w) in zip(dp_refs, DPROJ_GROUPS):
            part = _dot_nt(dp_ref[...], wt_ref[:, off:off + w])
            dh = part if dh is None else dh + part
        xv = x_ref[...]
        rstd = lax.rsqrt(jnp.mean(xv * xv, axis=-1, keepdims=True) + NORM_EPS)
        xh = xv * rstd
        gn = mod_ref[0:1, :]
        acc_ref[0:1, :] += jnp.sum(dh, axis=0, keepdims=True)
        acc_ref[1:2, :] += jnp.sum(dh * (xh * gn), axis=0, keepdims=True)
        dn = dh * (1.0 + mod_ref[1:2, :])
        acc_ref[2:3, :] += jnp.sum(dn * xh, axis=0, keepdims=True)
        dxh = dn * gn
        gx_ref[...] = dx1_ref[...] + rstd * (dxh - xh * jnp.mean(dxh * xh, axis=-1, keepdims=True))

    row_blk = lambda w: pl.BlockSpec((tm, w), lambda i: (i, 0))
    full = lambda a: pl.BlockSpec(a.shape, lambda i: (0,) * a.ndim)
    return pl.pallas_call(
        body, name="bwd_pre", grid=(s // tm,),
        out_shape=[jax.ShapeDtypeStruct((s, D_MODEL), F32), jax.ShapeDtypeStruct((8, D_MODEL), F32)],
        in_specs=[row_blk(w) for _, _, w in DPROJ_GROUPS] + [full(w_all), row_blk(D_MODEL), row_blk(D_MODEL), full(mod)],
        out_specs=[row_blk(D_MODEL), pl.BlockSpec((8, D_MODEL), lambda i: (0, 0))],
        compiler_params=_params(1))(*dproj, w_all, x, dx1, mod)


def _tn_matmul(a, b, tn, tk, name):
    s, m = a.shape
    n = b.shape[1]

    def body(a_ref, b_ref, o_ref):
        @pl.when(pl.program_id(1) == 0)
        def _():
            o_ref[...] = jnp.zeros_like(o_ref)

        o_ref[...] += _dot_tn(a_ref[...], b_ref[...])

    return pl.pallas_call(
        body, name=name, grid=(n // tn, s // tk), out_shape=jax.ShapeDtypeStruct((m, n), F32),
        in_specs=[pl.BlockSpec((tk, m), lambda jn, k: (k, 0)), pl.BlockSpec((tk, tn), lambda jn, k: (k, jn))],
        out_specs=pl.BlockSpec((m, tn), lambda jn, k: (0, jn)),
        compiler_params=_params(2))(a, b)


def _adam_math(g, w, m, v):
    m2 = ADAM_B1 * m + (1.0 - ADAM_B1) * g
    v2 = ADAM_B2 * v + (1.0 - ADAM_B2) * (g * g)
    delta = -ADAM_LR * ((m2 / ADAM_C1) / (jnp.sqrt(v2 / ADAM_C2) + ADAM_EPS) + ADAM_WD * w)
    return delta, m2, v2


def _adamw_parts(own, parts, others, w, m, v, tr, name):
    rws, cols = w.shape

    def body(oth_ref, own_ref, p_ref, w_ref, m_ref, v_ref, g_ref, d_ref, m2_ref, v2_ref):
        g = own_ref[...]
        for chip in range(N_CHIPS):
            g = g + oth_ref[chip] * p_ref[chip].astype(F32)
        g_ref[...] = g
        d_ref[...], m2_ref[...], v2_ref[...] = _adam_math(g, w_ref[...], m_ref[...], v_ref[...])

    blk = pl.BlockSpec((tr, cols), lambda i: (i, 0))
    o = jax.ShapeDtypeStruct((rws, cols), F32)
    return pl.pallas_call(
        body, name=name, grid=(rws // tr,), out_shape=[o, o, o, o],
        in_specs=[pl.BlockSpec(memory_space=pltpu.SMEM), blk,
                  pl.BlockSpec((N_CHIPS, tr, cols), lambda i: (0, i, 0)), blk, blk, blk],
        out_specs=[blk, blk, blk, blk], compiler_params=_params(1))(others, own, parts, w, m, v)


def _adamw_ada(c_lanes, d_rows, w, m, v, tr):
    rws, cols = w.shape

    def body(c_ref, d_ref, w_ref, m_ref, v_ref, g_ref, dl_ref, m2_ref, v2_ref):
        for k in range(cols // CHUNK):
            cs = slice(CHUNK * k, CHUNK * (k + 1))
            g = c_ref[0] * d_ref[0:1, cs]
            for b in range(1, N_DEV):
                g = g + c_ref[b] * d_ref[b:b + 1, cs]
            g_ref[:, cs] = g
            dl_ref[:, cs], m2_ref[:, cs], v2_ref[:, cs] = _adam_math(g, w_ref[:, cs], m_ref[:, cs], v_ref[:, cs])

    blk = pl.BlockSpec((tr, cols), lambda i: (i, 0))
    o = jax.ShapeDtypeStruct((rws, cols), F32)
    return pl.pallas_call(
        body, name="adamw_ada", grid=(rws // tr,), out_shape=[o, o, o, o],
        in_specs=[pl.BlockSpec((N_DEV, tr, CHUNK), lambda i: (0, i, 0)), pl.BlockSpec((N_DEV, cols), lambda i: (0, 0)),
                  blk, blk, blk],
        out_specs=[blk, blk, blk, blk], compiler_params=_params(1))(c_lanes, d_rows, w, m, v)


def _adamw_small(packs, w, m, v):
    def body(p_ref, w_ref, m_ref, v_ref, g_ref, d_ref, m2_ref, v2_ref, loss_ref):
        g = p_ref[0]
        for dev in range(1, N_DEV):
            g = g + p_ref[dev]
        g_ref[...] = g
        d_ref[...], m2_ref[...], v2_ref[...] = _adam_math(g, w_ref[...], m_ref[...], v_ref[...])
        loss_ref[...] = jnp.broadcast_to(jnp.sum(g_ref[2:3, 0:D_MODEL], axis=1, keepdims=True), loss_ref.shape)

    o = jax.ShapeDtypeStruct(w.shape, F32)
    return pl.pallas_call(body, name="adamw_small", out_shape=[o, o, o, o, jax.ShapeDtypeStruct((8, CHUNK), F32)],
                          compiler_params=_params(0))(packs, w, m, v)


def _tile(s, want):
    return min(s, want)


def _layout_pieces():
    orig = {"qa": 0, "ka": 512, "va": 640, "za": 768, "qb": 1280, "kb": 1792, "vb": 2304, "f": 2816, "zb": 2824,
            "ga": 3336, "gb": 4360}
    pieces = []
    for name, (off, w) in SEG.items():
        if name in ("ka", "va"):
            pieces += [(orig[name] + HEAD_DIM * kv, orig[name] + HEAD_DIM * (kv + 1), off + CHUNK * kv) for kv in range(KV_GROUPS)]
        else:
            pieces.append((orig[name], orig[name] + (N_HEADS if name == "f" else w), off))
    return pieces


def _assemble_w_all(win_g):
    cols, pos = [], 0
    for lo, hi, new in sorted(_layout_pieces(), key=lambda t: t[2]):
        if new > pos:
            cols.append(jnp.zeros((D_MODEL, new - pos), win_g.dtype))
        col = lo
        while col < hi:
            dev = col // IN_SHARD
            end = min(hi, (dev + 1) * IN_SHARD)
            cols.append(win_g[dev, :, col - dev * IN_SHARD:end - dev * IN_SHARD])
            col = end
        pos = new + hi - lo
    cols.append(jnp.zeros((D_MODEL, PROJ_W - pos), win_g.dtype))
    return jnp.concatenate(cols, axis=1)


def _grad_slot(dw_groups, dev):
    lo_d, hi_d = dev * IN_SHARD, (dev + 1) * IN_SHARD
    cols = []
    for lo, hi, new in sorted(_layout_pieces()):
        a, b = max(lo, lo_d), min(hi, hi_d)
        if a < b:
            arr, off = next((g, o) for g, (_, o, w) in zip(dw_groups, DPROJ_GROUPS) if o <= new < o + w)
            cols.append(arr[:, new - off + a - lo:new - off + b - lo])
    return jnp.concatenate(cols, axis=1)


def kernel(x, c, positions, w_ada, b_ada, g_norm, w_in, b_f, sinks, w_o_swa, w_o_fox, w_out, g_final, loss_target, m_w_ada, m_b_ada, m_g_norm, m_w_in, m_b_f, m_sinks, m_w_o_swa, m_w_o_fox, m_w_out, m_g_final, v_w_ada, v_b_ada, v_g_norm, v_w_in, v_b_f, v_sinks, v_w_o_swa, v_w_o_fox, v_w_out, v_g_final):
    s = x.shape[1]
    tm = _tile(s, 256)
    ta = _tile(s, 512)
    me = 4 * lax.axis_index("x") + 2 * lax.axis_index("y") + lax.axis_index("c")
    x2, tgt = x[0], loss_target[0]

    (c_all,) = _exchange([jnp.broadcast_to(c, (8, D_MODEL))], ["gather"], "gather_c")
    c_all = c_all[:, 0, :]
    b_shard = lax.dynamic_slice(b_ada, (0, me * ADA_SHARD), (1, ADA_SHARD))
    ada_part = _ada_fwd(c_all, w_ada[0], b_shard)
    ada_g, win_g, woa_g, wob_g, wout_g = _gather_two_level(
        [ada_part, w_in[0].astype(BF16), w_o_swa[0].T.astype(BF16), w_o_fox[0].T.astype(BF16), w_out[0].astype(BF16)],
        "gather_weights")
    ada = lax.dynamic_index_in_dim(ada_g, me, axis=1, keepdims=False).reshape(3 * D_MODEL)
    shift, scale, gate = ada[:D_MODEL], ada[D_MODEL:2 * D_MODEL], ada[2 * D_MODEL:]
    woa_t = woa_g.reshape(D_MODEL, 512)
    wob_t = wob_g.reshape(D_MODEL, 512)
    wout = wout_g.reshape(D_MODEL, D_MODEL)
    w_all = _assemble_w_all(win_g)

    zrow = jnp.zeros((1, D_MODEL), F32)
    mod = jnp.concatenate([g_norm, scale[None], shift[None], zrow, zrow, zrow, zrow, zrow], axis=0)
    bf_row = jnp.pad(b_f, ((0, 0), (0, CHUNK - N_HEADS)))
    emat_np = np.zeros((CHUNK, AUG_W), np.float32)
    ind_np = np.zeros((512, CHUNK), np.float32)
    for hd in range(N_HEADS):
        emat_np[hd, CHUNK * hd + 64:CHUNK * hd + 67] = 1.0
        ind_np[HEAD_DIM * hd:HEAD_DIM * (hd + 1), hd] = 1.0
    emat, ind = jnp.asarray(emat_np, BF16), jnp.asarray(ind_np, BF16)
    inv_freq = np.power(np.float32(ROPE_THETA), -np.arange(0, HEAD_DIM, 2, dtype=np.float32) / HEAD_DIM)
    inv_freq = jnp.asarray(np.tile(inv_freq, CHUNK // (HEAD_DIM // 2))[None, :], F32)
    cos, sa, sb = _rope_tables(positions.reshape(s, 1).astype(F32), inv_freq, tm)
    sink_rows = jnp.broadcast_to(jnp.pad(sinks.reshape(KV_GROUPS, GROUP), ((0, 0), (0, 8 - GROUP)))[:, :, None],
                                 (KV_GROUPS, 8, CHUNK))

    (h, qa, ka, va, za, qb, kb, vb, zb, ga, gb, xf, bounds, norms) = _fwd_proj(
        x2, mod, w_all, cos, sa, sb, bf_row, emat, ind, tm)
    first_key, last_query = _fox_block_ranges(bounds, norms, ta // tm)
    att_a, qa_bwd = _swa_fwd(qa, ka, va, sink_rows, ta)
    att_b, qb_bwd = _fox_fwd(first_key, qb, kb, vb, ta)
    vec = jnp.concatenate([gate[None], g_final[None], zrow, zrow, zrow, zrow, zrow, zrow], axis=0)
    (dx1, ua, ub, merged, d_o, dya, dyb, doa, dob, dzg, post_acc) = _post(
        x2, tgt, att_a, att_b, za, zb, ga, gb, woa_t, wob_t, wout, vec, ind, emat, tm)

    da, dsink = _swa_bwd(ka, va, qa_bwd, doa, cos, sa, sb, sink_rows, ta)
    dkb, dvb, dck, dqb, dcq = _fox_bwd(last_query, kb, vb, qb_bwd, dob, ta)
    by_head = lambda a: jnp.pad(a[:, :, :2].transpose(1, 0, 2).reshape(s, N_HEADS), ((0, 0), (0, CHUNK - N_HEADS)))
    dfb, dbf_acc = _fgate_bwd(by_head(dcq), by_head(dck), xf, tm)
    dproj = [da, dqb, dkb, dvb, dzg, dfb]
    grad_x, pre_acc = _bwd_pre(dproj, w_all, x2, dx1, mod, tm)
    dw_all = [_tn_matmul(h, d, min(w, 1024), ta, "dw_in_" + nm) for d, (nm, _, w) in zip(dproj, DPROJ_GROUPS)]
    dwoa = _tn_matmul(ua, dya, 512, ta, "dw_o_swa")
    dwob = _tn_matmul(ub, dyb, 512, ta, "dw_o_fox")
    dwout = _tn_matmul(merged, d_o, 512, ta, "dw_out")


    misc = jnp.concatenate([dbf_acc[0, :N_HEADS], dsink[:, :GROUP, 0].reshape(N_HEADS), jnp.zeros((D_MODEL - 16,), F32)])
    zpad = jnp.zeros((2 * D_MODEL,), F32)
    pack = jnp.stack([jnp.concatenate([pre_acc[0], pre_acc[1], post_acc[0]]),
                      jnp.concatenate([pre_acc[2], post_acc[1], misc]),
                      jnp.concatenate([post_acc[2], zpad])] + [jnp.zeros((3 * D_MODEL,), F32)] * 5)
    core, chip = lax.axis_index("c"), 2 * lax.axis_index("x") + lax.axis_index("y")
    col_slots = lambda g, r, w: g.reshape(r, N_CHIPS, 2, w).transpose(2, 1, 0, 3)
    win_slots = jnp.stack([jnp.stack([_grad_slot(dw_all, 2 * ch + co) for ch in range(N_CHIPS)]) for co in range(2)])
    slots = [win_slots, col_slots(dwoa, 512, 128), col_slots(dwob, 512, 128),
             dwout.reshape(N_CHIPS, 2, 128, D_MODEL).transpose(1, 0, 2, 3)]
    theirs = _swap_sibling(slots, "grads_sibling")
    own, halves = [], []
    for sl, th, nm in zip(slots, theirs, ("w_in", "w_o_swa", "w_o_fox", "w_out")):
        full, half = _chip_partial(lax.dynamic_index_in_dim(sl, core, 0, keepdims=False), th, min(th.shape[1], 256),
                                   "chip_partial_" + nm)
        own.append(lax.dynamic_index_in_dim(full, chip, 0, keepdims=False))
        halves.append(half)
    p_win, p_woa, p_wob, p_wout, packs = _exchange(halves + [pack], ["chips"] * 4 + ["gather"], "exchange_grads")
    others = jnp.where(jnp.arange(N_CHIPS) == chip, 0.0, 1.0).astype(F32)

    g_win, d_win, m_win, v_win = _adamw_parts(own[0], p_win, others, w_in[0], m_w_in[0], v_w_in[0], 128, "adamw_w_in")
    g_woa, d_woa, m_woa, v_woa = _adamw_parts(own[1], p_woa, others, w_o_swa[0], m_w_o_swa[0], v_w_o_swa[0], 128, "adamw_w_o_swa")
    g_wob, d_wob, m_wob, v_wob = _adamw_parts(own[2], p_wob, others, w_o_fox[0], m_w_o_fox[0], v_w_o_fox[0], 128, "adamw_w_o_fox")
    g_wout, d_wout, m_wout, v_wout = _adamw_parts(own[3], p_wout, others, w_out[0], m_w_out[0], v_w_out[0], 128, "adamw_w_out")
    d_ada_rows = lax.dynamic_slice(packs[:, 0, :], (0, me * ADA_SHARD), (N_DEV, ADA_SHARD))
    c_lanes = jnp.broadcast_to(c_all[:, :, None], (N_DEV, D_MODEL, CHUNK))
    g_wada, d_wada, m_wada, v_wada = _adamw_ada(c_lanes, d_ada_rows, w_ada[0], m_w_ada[0], v_w_ada[0], 256)

    def small_pack(bada, gn, gf, bfv, sk):
        misc_w = jnp.concatenate([bfv[0], sk[0], jnp.zeros((D_MODEL - 16,), F32)])
        return jnp.stack([bada[0], jnp.concatenate([gn[0], gf, misc_w])] + [jnp.zeros((3 * D_MODEL,), F32)] * 6)

    sm = _adamw_small(packs, small_pack(b_ada, g_norm, g_final, b_f, sinks),
                      small_pack(m_b_ada, m_g_norm, m_g_final, m_b_f, m_sinks),
                      small_pack(v_b_ada, v_g_norm, v_g_final, v_b_f, v_sinks))
    loss = sm[4][0, 0]

    def unpack(p):
        return (p[0][None], p[1, :D_MODEL][None], p[1, 2 * D_MODEL:2 * D_MODEL + 8][None],
                p[1, 2 * D_MODEL + 8:2 * D_MODEL + 16][None], p[1, D_MODEL:2 * D_MODEL])

    outs = []
    for big, small in (((g_wada, g_win, g_woa, g_wob, g_wout), sm[0]), ((d_wada, d_win, d_woa, d_wob, d_wout), sm[1]),
                       ((m_wada, m_win, m_woa, m_wob, m_wout), sm[2]), ((v_wada, v_win, v_woa, v_wob, v_wout), sm[3])):
        wada_o, win_o, woa_o, wob_o, wout_o = big
        bada_o, gn_o, bf_o, sk_o, gf_o = unpack(small)
        outs += [wada_o[None], bada_o, gn_o, win_o[None], bf_o, sk_o, woa_o[None], wob_o[None], wout_o[None], gf_o]
    return (loss, grad_x[None], *outs)
```

```python
import numpy as np
import jax
import jax.numpy as jnp
from jax import lax
from jax.experimental import pallas as pl
from jax.experimental.pallas import tpu as pltpu

F32 = jnp.float32
BF16 = jnp.bfloat16

D_MODEL = 1024
HEAD_DIM = 64
N_HEADS = 8
KV_GROUPS = 2
GROUP = N_HEADS // KV_GROUPS
WINDOW = 128
CHUNK = 128
AUG_W = N_HEADS * CHUNK
N_DEV = 8
IN_SHARD = 673
ADA_SHARD = 384
NORM_EPS = 1e-6
ROPE_THETA = 10000.0
Q_SCALE = HEAD_DIM ** -0.5
NEG = -1e30

ADAM_LR = 0.001
ADAM_B1 = 0.9
ADAM_B2 = 0.999
ADAM_EPS = 1e-08
ADAM_WD = 0.01
ADAM_STEP = 10
ADAM_C1 = 1.0 - ADAM_B1 ** ADAM_STEP
ADAM_C2 = 1.0 - ADAM_B2 ** ADAM_STEP

SEG = {}
_off = 0
for _name, _w in (("qa", 512), ("ka", 256), ("va", 256), ("qb", 512), ("kb", 512), ("vb", 512),
                  ("za", 512), ("zb", 512), ("ga", 1024), ("gb", 1024), ("f", 128)):
    SEG[_name] = (_off, _w)
    _off += _w
PROJ_W = _off
DPROJ_GROUPS = (("a", SEG["qa"][0], 1024), ("qb", SEG["qb"][0], 512), ("kb", SEG["kb"][0], 512),
                ("vb", SEG["vb"][0], 512), ("zg", SEG["za"][0], 3072), ("f", SEG["f"][0], 128))

VMEM_LIMIT = 56 * 1024 * 1024


def _params(n_axes):
    return pltpu.CompilerParams(dimension_semantics=("arbitrary",) * n_axes, vmem_limit_bytes=VMEM_LIMIT)


def _dot(a, b):
    return jnp.dot(a, b, preferred_element_type=F32)


def _dot_nt(a, b):
    return lax.dot_general(a, b, (((1,), (1,)), ((), ())), preferred_element_type=F32)


def _dot_tn(a, b):
    return lax.dot_general(a, b, (((0,), (0,)), ((), ())), preferred_element_type=F32)


def _lane(n):
    return lax.broadcasted_iota(jnp.int32, (1, n), 1)


def _split3(x):
    hi = x.astype(BF16).astype(F32)
    r = x - hi
    mid = r.astype(BF16).astype(F32)
    lo = (r - mid).astype(BF16).astype(F32)
    return hi, mid, lo


def _dot_split(x, b):
    hi, mid, lo = _split3(x)
    return _dot(hi.astype(BF16), b) + _dot(mid.astype(BF16), b) + _dot(lo.astype(BF16), b)


def _spread3(x, e3_ref):
    hi, mid, lo = _split3(x)
    return _dot(hi.astype(BF16), e3_ref[0]) + _dot(mid.astype(BF16), e3_ref[1]) + _dot(lo.astype(BF16), e3_ref[2])


def _place3(lane, base, x, other):
    hi, mid, lo = _split3(x)
    return jnp.where(lane == base, hi, jnp.where(lane == base + 1, mid, jnp.where(lane == base + 2, lo, other)))


def _lane_sum(x, lo, hi):
    lane = _lane(x.shape[1])
    return jnp.sum(jnp.where((lane >= lo) & (lane < hi), x, 0.0), axis=1, keepdims=True)


def _ones_lanes(lo, hi):
    lane = _lane(CHUNK)
    return jnp.where((lane >= lo) & (lane < hi), 1.0, 0.0).astype(F32)


def _rope(c, cos, sa, sb):
    return c * cos + pltpu.roll(c, CHUNK - 32, 1) * sa + pltpu.roll(c, 32, 1) * sb


def _rope_inv(d, cos, sa, sb):
    return d * cos - (pltpu.roll(d, CHUNK - 32, 1) * sa + pltpu.roll(d, 32, 1) * sb)


def _pair(c0, c1):
    return jnp.where(_lane(CHUNK) < HEAD_DIM, c0, pltpu.roll(c1, HEAD_DIM, 1))


def _sigmoid(x):
    return 0.5 * jnp.tanh(0.5 * x) + 0.5


N_CHIPS = 4
ANY_SPEC = pl.BlockSpec(memory_space=pl.ANY)


def _exchange(arrays, modes, name):
    n = len(arrays)
    out_shape = [jax.ShapeDtypeStruct((N_DEV,) + a.shape if md == "gather" else a.shape, a.dtype)
                 for a, md in zip(arrays, modes)]

    def body(*refs):
        srcs, dsts = refs[:n], refs[n:2 * n]
        send_sems, recv_sems, loc_sems = refs[2 * n:]
        x, y, c = lax.axis_index("x"), lax.axis_index("y"), lax.axis_index("c")

        def slot(i, px, py, pc):
            return 4 * px + 2 * py + pc if modes[i] == "gather" else 2 * px + py

        def src_of(i, px, py, pc):
            return srcs[i] if modes[i] == "gather" else srcs[i].at[slot(i, px, py, pc)]

        local = [pltpu.make_async_copy(src_of(i, x, y, c), dsts[i].at[slot(i, x, y, c)], loc_sems.at[i])
                 for i in range(n)]
        for cp in local:
            cp.start()
        sends, recvs = [], []
        for r in (1, 2, 4, 3, 5, 6, 7):
            px = 1 - x if r & 4 else x
            py = 1 - y if r & 2 else y
            pc = 1 - c if r & 1 else c
            for i in range(n):
                if modes[i] == "chips" and r & 1:
                    continue
                sems = dict(send_sem=send_sems.at[i, r - 1], recv_sem=recv_sems.at[i, r - 1],
                            device_id=(px, py, pc), device_id_type=pl.DeviceIdType.MESH)
                sends.append(pltpu.make_async_remote_copy(
                    src_ref=src_of(i, px, py, pc), dst_ref=dsts[i].at[slot(i, x, y, c)], **sems))
                recvs.append(pltpu.make_async_remote_copy(
                    src_ref=src_of(i, px, py, pc), dst_ref=dsts[i].at[slot(i, px, py, pc)], **sems))
        for cp in sends:
            cp.start()
        for cp in recvs:
            cp.wait_recv()
        for cp in sends:
            cp.wait_send()
        for cp in local:
            cp.wait()

    return pl.pallas_call(
        body, name=name, out_shape=out_shape,
        in_specs=[ANY_SPEC] * n, out_specs=[ANY_SPEC] * n,
        scratch_shapes=[pltpu.SemaphoreType.DMA((n, N_DEV - 1)), pltpu.SemaphoreType.DMA((n, N_DEV - 1)),
                        pltpu.SemaphoreType.DMA((n,))],
    )(*arrays)


def _gather_two_level(arrays, name):
    n = len(arrays)
    out_shape = [jax.ShapeDtypeStruct((N_DEV,) + a.shape, a.dtype) for a in arrays]

    def body(*refs):
        srcs, dsts = refs[:n], refs[n:2 * n]
        send_sems, recv_sems, loc_sems = refs[2 * n:]
        x, y, c = lax.axis_index("x"), lax.axis_index("y"), lax.axis_index("c")
        me, sibling = (x, y, c), (x, y, 1 - c)
        chips = [(1 - x, y), (x, 1 - y), (1 - x, 1 - y)]

        def rows(i, dev):
            return dsts[i].at[4 * dev[0] + 2 * dev[1] + dev[2]]

        def copy(i, k, block, to, src=None):
            return pltpu.make_async_remote_copy(
                src_ref=rows(i, block) if src is None else src, dst_ref=rows(i, block),
                send_sem=send_sems.at[i, k], recv_sem=recv_sems.at[i, k],
                device_id=to, device_id_type=pl.DeviceIdType.MESH)

        local = [pltpu.make_async_copy(srcs[i], rows(i, me), loc_sems.at[i]) for i in range(n)]
        for cp in local:
            cp.start()
        first = []
        for i in range(n):
            first.append(copy(i, 0, me, sibling, src=srcs[i]))
            first += [copy(i, 1 + j, me, (*chip, c), src=srcs[i]) for j, chip in enumerate(chips)]
        for cp in first:
            cp.start()
        passed = []
        for j, chip in enumerate(chips):
            for i in range(n):
                copy(i, 1 + j, (*chip, c), me).wait_recv()
                cp = copy(i, 4 + j, (*chip, c), sibling)
                cp.start()
                passed.append(cp)
        for i in range(n):
            copy(i, 0, sibling, me).wait_recv()
            for j, chip in enumerate(chips):
                copy(i, 4 + j, (*chip, 1 - c), me).wait_recv()
        for cp in first + passed:
            cp.wait_send()
        for cp in local:
            cp.wait()

    return pl.pallas_call(
        body, name=name, out_shape=out_shape,
        in_specs=[ANY_SPEC] * n, out_specs=[ANY_SPEC] * n,
        scratch_shapes=[pltpu.SemaphoreType.DMA((n, N_DEV - 1)), pltpu.SemaphoreType.DMA((n, N_DEV - 1)),
                        pltpu.SemaphoreType.DMA((n,))],
    )(*arrays)


def _swap_sibling(arrays, name):
    n = len(arrays)
    out_shape = [jax.ShapeDtypeStruct(a.shape[1:], a.dtype) for a in arrays]

    def body(*refs):
        srcs, dsts = refs[:n], refs[n:2 * n]
        send_sems, recv_sems = refs[2 * n:]
        x, y, c = lax.axis_index("x"), lax.axis_index("y"), lax.axis_index("c")
        copies = [pltpu.make_async_remote_copy(
            src_ref=srcs[i].at[1 - c], dst_ref=dsts[i], send_sem=send_sems.at[i], recv_sem=recv_sems.at[i],
            device_id=(x, y, 1 - c), device_id_type=pl.DeviceIdType.MESH) for i in range(n)]
        for cp in copies:
            cp.start()
        for cp in copies:
            cp.wait()

    return pl.pallas_call(
        body, name=name, out_shape=out_shape, in_specs=[ANY_SPEC] * n, out_specs=[ANY_SPEC] * n,
        scratch_shapes=[pltpu.SemaphoreType.DMA((n,)), pltpu.SemaphoreType.DMA((n,))],
    )(*arrays)


def _chip_partial(mine, theirs, tr, name):
    k, rws, cols = mine.shape

    def body(a_ref, b_ref, f_ref, h_ref):
        sm = a_ref[...] + b_ref[...]
        f_ref[...] = sm
        h_ref[...] = sm.astype(BF16)

    blk = pl.BlockSpec((1, tr, cols), lambda j, i: (j, i, 0))
    return pl.pallas_call(
        body, name=name, grid=(k, rws // tr),
        out_shape=[jax.ShapeDtypeStruct(mine.shape, F32), jax.ShapeDtypeStruct(mine.shape, BF16)],
        in_specs=[blk, blk], out_specs=[blk, blk], compiler_params=_params(2))(mine, theirs)


def _ada_fwd(c_all, w_shard, b_shard):
    def body(c_ref, w_ref, b_ref, o_ref):
        ch, cm, cl = [t.astype(BF16) for t in _split3(c_ref[...])]
        wh, wm, wl = [t.astype(BF16) for t in _split3(w_ref[...])]
        acc = _dot(ch, wh) + _dot(ch, wm) + _dot(cm, wh) + _dot(ch, wl) + _dot(cl, wh) + _dot(cm, wm)
        o_ref[...] = acc + b_ref[...]

    return pl.pallas_call(body, name="ada_fwd", out_shape=jax.ShapeDtypeStruct((N_DEV, ADA_SHARD), F32),
                          compiler_params=_params(0))(c_all, w_shard, b_shard)


def _rope_tables(pos_col, inv_freq, tm):
    s = pos_col.shape[0]

    def body(p_ref, f_ref, cos_ref, sa_ref, sb_ref):
        ang = p_ref[...] * f_ref[...]
        sin = jnp.sin(ang)
        first_half = (_lane(CHUNK) & (HEAD_DIM - 1)) < HEAD_DIM // 2
        cos_ref[...] = jnp.cos(ang)
        sa_ref[...] = jnp.where(first_half, -sin, 0.0)
        sb_ref[...] = jnp.where(first_half, 0.0, sin)

    tab = jax.ShapeDtypeStruct((s, CHUNK), F32)
    blk = pl.BlockSpec((tm, CHUNK), lambda i: (i, 0))
    return pl.pallas_call(
        body, name="rope_tables", grid=(s // tm,), out_shape=[tab, tab, tab],
        in_specs=[pl.BlockSpec((tm, 1), lambda i: (i, 0)), pl.BlockSpec((1, CHUNK), lambda i: (0, 0))],
        out_specs=[blk, blk, blk], compiler_params=_params(1))(pos_col, inv_freq)


def _fwd_proj(x, mod, w_all, cos, sa, sb, bf_row, emat, ind, tm):
    s = x.shape[0]

    def body(x_ref, mod_ref, w_ref, cos_ref, sa_ref, sb_ref, bf_ref, e_ref, ind_ref,
             h_ref, qa_ref, ka_ref, va_ref, za_ref, qb_ref, kb_ref, vb_ref, zb_ref, ga_ref, gb_ref, f_ref,
             bnd_ref, nrm_ref, carry_ref):
        @pl.when(pl.program_id(0) == 0)
        def _():
            carry_ref[...] = jnp.zeros_like(carry_ref)

        xv = x_ref[...]
        rstd = lax.rsqrt(jnp.mean(xv * xv, axis=-1, keepdims=True) + NORM_EPS)
        h = (xv * rstd * mod_ref[0:1, :]) * (1.0 + mod_ref[1:2, :]) + mod_ref[2:3, :]
        hb = h.astype(BF16)
        h_ref[...] = hb.T

        def seg(name):
            off, w = SEG[name]
            return _dot(hb, w_ref[:, off:off + w])

        lane = _lane(CHUNK)
        low = lane < HEAD_DIM
        q_ones = _ones_lanes(64, 67)
        k_ones = _ones_lanes(67, 70)
        cos_t, sa_t, sb_t = cos_ref[...], sa_ref[...], sb_ref[...]

        def write_heads(ref, nat, extra_of, rope, scale):
            for p in range(N_HEADS // 2):
                c = nat[:, CHUNK * p:CHUNK * (p + 1)]
                if rope:
                    c = _rope(c, cos_t, sa_t, sb_t)
                if scale != 1.0:
                    c = c * scale
                for hh in range(2):
                    hd = 2 * p + hh
                    src = c if hh == 0 else pltpu.roll(c, HEAD_DIM, 1)
                    ref[:, CHUNK * hd:CHUNK * (hd + 1)] = jnp.where(low, src, extra_of(hd)).astype(BF16)

        write_heads(qa_ref, seg("qa"), lambda hd: q_ones, True, Q_SCALE)
        ka = seg("ka")
        va = seg("va")
        for kv in range(KV_GROUPS):
            sl = slice(CHUNK * kv, CHUNK * (kv + 1))
            ka_ref[:, sl] = jnp.where(low, _rope(ka[:, sl], cos_t, sa_t, sb_t), k_ones).astype(BF16)
            va_ref[:, sl] = jnp.where(low, va[:, sl], q_ones).astype(BF16)
        za_ref[...] = seg("za")
        zb_ref[...] = seg("zb")
        ga_ref[...] = seg("ga")
        gb_ref[...] = seg("gb")

        xf = seg("f") + bf_ref[...]
        f_ref[...] = xf
        logf = jnp.minimum(xf, 0.0) - jnp.log1p(jnp.exp(-jnp.abs(xf)))
        row = lax.broadcasted_iota(jnp.int32, (tm, tm), 0)
        col = lax.broadcasted_iota(jnp.int32, (tm, tm), 1)
        tri = jnp.where(col <= row, 1.0, 0.0).astype(BF16)
        hi, mid, lo = _split3(logf)
        cum = _dot(tri, hi.astype(BF16)) + _dot(tri, mid.astype(BF16)) + _dot(tri, lo.astype(BF16))
        cum = cum + carry_ref[0:1, :]
        carry_ref[0:1, :] += jnp.sum(logf, axis=0, keepdims=True)
        lane_all = _lane(AUG_W) & (CHUNK - 1)
        bias = _spread3(-cum, e_ref) + jnp.where((lane_all >= 67) & (lane_all < 70), 1.0, 0.0)

        qb, kb = seg("qb"), seg("kb")
        write_heads(qb_ref, qb, lambda hd: q_ones, False, Q_SCALE)
        write_heads(kb_ref, kb, lambda hd: bias[:, CHUNK * hd:CHUNK * (hd + 1)], False, 1.0)
        write_heads(vb_ref, seg("vb"), lambda hd: q_ones, False, 1.0)

        rid = lax.broadcasted_iota(jnp.int32, (tm, CHUNK), 0)
        bnd_ref[...] = jnp.zeros_like(bnd_ref)
        bnd_ref[0, 0:1, :] = jnp.sum(jnp.where(rid == 0, cum, 0.0), axis=0, keepdims=True)
        bnd_ref[0, 1:2, :] = jnp.sum(jnp.where(rid == tm - 1, cum, 0.0), axis=0, keepdims=True)

        @pl.when(pl.program_id(0) == 0)
        def _():
            nrm_ref[...] = jnp.zeros_like(nrm_ref)

        for r_, nat in ((0, qb * Q_SCALE), (1, kb)):
            sq = _dot((nat * nat).astype(BF16), ind_ref[...])
            nrm_ref[r_:r_ + 1, :] = jnp.maximum(nrm_ref[r_:r_ + 1, :], jnp.max(sq, axis=0, keepdims=True))

    row_blk = lambda w: pl.BlockSpec((tm, w), lambda i: (i, 0))
    full = lambda a: pl.BlockSpec(a.shape, lambda i: (0,) * a.ndim)
    sds = lambda w, dt: jax.ShapeDtypeStruct((s, w), dt)
    out_shape = [sds(AUG_W, BF16), sds(KV_GROUPS * CHUNK, BF16), sds(KV_GROUPS * CHUNK, BF16),
                 sds(512, F32), sds(AUG_W, BF16), sds(AUG_W, BF16), sds(AUG_W, BF16), sds(512, F32),
                 sds(D_MODEL, F32), sds(D_MODEL, F32), sds(CHUNK, F32)]
    small = [jax.ShapeDtypeStruct((s // tm, 8, CHUNK), F32), jax.ShapeDtypeStruct((8, CHUNK), F32)]
    return pl.pallas_call(
        body, name="fwd_proj", grid=(s // tm,),
        out_shape=[jax.ShapeDtypeStruct((D_MODEL, s), BF16)] + out_shape + small,
        in_specs=[row_blk(D_MODEL), full(mod), full(w_all), row_blk(CHUNK), row_blk(CHUNK), row_blk(CHUNK),
                  full(bf_row), full(emat), full(ind)],
        out_specs=[pl.BlockSpec((D_MODEL, tm), lambda i: (0, i))] + [row_blk(o.shape[1]) for o in out_shape] + [
            pl.BlockSpec((1, 8, CHUNK), lambda i: (i, 0, 0)), pl.BlockSpec((8, CHUNK), lambda i: (0, 0))],
        scratch_shapes=[pltpu.VMEM((8, CHUNK), F32)],
        compiler_params=_params(1))(x, mod, w_all, cos, sa, sb, bf_row, emat, ind)


def _swa_fwd(q_aug, k_aug, v_aug, sink_rows, tq):
    s = q_aug.shape[0]
    r = tq // WINDOW
    gw = GROUP * CHUNK

    def body(q_ref, kc_ref, kp_ref, vc_ref, vp_ref, sink_ref, o_ref, qb_ref):
        i = pl.program_id(1)
        st = GROUP * WINDOW
        qloc = lax.broadcasted_iota(jnp.int32, (st, 2 * WINDOW), 0) & (WINDOW - 1)
        col = lax.broadcasted_iota(jnp.int32, (st, 2 * WINDOW), 1)
        band = (col > qloc) & (col <= qloc + WINDOW)
        head = jnp.right_shift(lax.broadcasted_iota(jnp.int32, (st, 1), 0), 7)
        sink = jnp.zeros((st, 1), F32)
        for g in range(GROUP):
            sink = jnp.where(head == g, jnp.max(sink_ref[0, g:g + 1, :], axis=1, keepdims=True), sink)
        lane = _lane(CHUNK)
        for sub in range(r):
            rows = slice(WINDOW * sub, WINDOW * (sub + 1))
            q = jnp.concatenate([q_ref[rows, CHUNK * g:CHUNK * (g + 1)] for g in range(GROUP)], axis=0)
            if sub == 0:
                k = jnp.concatenate([kp_ref[...], kc_ref[rows, :]], axis=0)
                v = jnp.concatenate([vp_ref[...], vc_ref[rows, :]], axis=0)
                valid = band & ((col >= WINDOW) | (i > 0))
            else:
                both = slice(WINDOW * (sub - 1), WINDOW * (sub + 1))
                k, v, valid = kc_ref[both, :], vc_ref[both, :], band
            sc = jnp.where(valid, _dot_nt(q, k), NEG)
            m = jnp.maximum(jnp.max(sc, axis=1, keepdims=True), sink)
            acc = _dot(jnp.exp(sc - m).astype(BF16), v)
            denom = _lane_sum(acc, 64, 65) + jnp.exp(sink - m)
            out = acc / denom
            aug = _place3(lane, 67, -(m + jnp.log(denom)), q.astype(F32)).astype(BF16)
            hrows = lambda a, g: a[WINDOW * g:WINDOW * (g + 1), :]
            for g in range(GROUP):
                qb_ref[rows, CHUNK * g:CHUNK * (g + 1)] = hrows(aug, g)
            for pp in range(GROUP // 2):
                o_ref[rows, CHUNK * pp:CHUNK * (pp + 1)] = _pair(hrows(out, 2 * pp), hrows(out, 2 * pp + 1))

    return pl.pallas_call(
        body, name="swa_fwd", grid=(KV_GROUPS, s // tq),
        out_shape=[jax.ShapeDtypeStruct((s, 512), F32), jax.ShapeDtypeStruct((s, AUG_W), BF16)],
        in_specs=[pl.BlockSpec((tq, gw), lambda kv, i: (i, kv)),
                  pl.BlockSpec((tq, CHUNK), lambda kv, i: (i, kv)),
                  pl.BlockSpec((WINDOW, CHUNK), lambda kv, i: (jnp.maximum(i * r - 1, 0), kv)),
                  pl.BlockSpec((tq, CHUNK), lambda kv, i: (i, kv)),
                  pl.BlockSpec((WINDOW, CHUNK), lambda kv, i: (jnp.maximum(i * r - 1, 0), kv)),
                  pl.BlockSpec((1, 8, CHUNK), lambda kv, i: (kv, 0, 0))],
        out_specs=[pl.BlockSpec((tq, GROUP * HEAD_DIM), lambda kv, i: (i, kv)),
                   pl.BlockSpec((tq, gw), lambda kv, i: (i, kv))],
        compiler_params=_params(2))(q_aug, k_aug, k_aug, v_aug, v_aug, sink_rows)


def _swa_bwd(k_aug, v_aug, q_bwd, do_aug, cos, sa, sb, sink_rows, tk):
    s = k_aug.shape[0]
    r = tk // WINDOW
    nt = s // tk
    nb = s // WINDOW
    gw = GROUP * CHUNK

    def body(k_ref, v_ref, q_ref, qn_ref, do_ref, don_ref, cos_ref, sa_ref, sb_ref, sink_ref,
             da_ref, dsink_ref, carry_ref, acc_ref):
        j = pl.program_id(0)

        @pl.when(j == 0)
        def _():
            carry_ref[...] = jnp.zeros_like(carry_ref)
            dsink_ref[...] = jnp.zeros_like(dsink_ref)

        acc_ref[...] = jnp.zeros_like(acc_ref)
        acc_ref[0:WINDOW, :] = carry_ref[...]
        dk_col, dv_col = SEG["ka"][0] - SEG["qa"][0], SEG["va"][0] - SEG["qa"][0]
        st = GROUP * WINDOW
        key = lax.broadcasted_iota(jnp.int32, (WINDOW, 2 * st), 0)
        col = lax.broadcasted_iota(jnp.int32, (WINDOW, 2 * st), 1)
        qloc = col & (WINDOW - 1)
        band = ((col < st) & (key <= qloc)) | ((col >= st) & (key > qloc))
        lane = _lane(CHUNK)
        low = lane < HEAD_DIM
        cos_t, sa_t, sb_t = cos_ref[...], sa_ref[...], sb_ref[...]
        for kv, sub in [(kv, sub) for kv in range(KV_GROUPS) for sub in range(r)]:
            heads = [slice(CHUNK * (GROUP * kv + g), CHUNK * (GROUP * kv + g + 1)) for g in range(GROUP)]
            kvs = slice(CHUNK * kv, CHUNK * (kv + 1))
            rows = slice(WINDOW * sub, WINDOW * (sub + 1))
            nxt = slice(WINDOW * (sub + 1), WINDOW * (sub + 2))
            k, v = k_ref[rows, kvs], v_ref[rows, kvs]
            q_cur, do_cur = [q_ref[rows, cs] for cs in heads], [do_ref[rows, cs] for cs in heads]
            if sub < r - 1:
                q_nxt, do_nxt, valid = [q_ref[nxt, cs] for cs in heads], [do_ref[nxt, cs] for cs in heads], band
            else:
                q_nxt, do_nxt = [qn_ref[:, cs] for cs in heads], [don_ref[:, cs] for cs in heads]
                valid = band & ((col < st) | (j < nt - 1))
            q = jnp.concatenate(q_cur + q_nxt, axis=0)
            do = jnp.concatenate(do_cur + do_nxt, axis=0)
            pt = jnp.exp(jnp.where(valid, _dot_nt(k, q), NEG))
            ds = (pt * _dot_nt(v, do)).astype(BF16)
            dv = _dot(pt.astype(BF16), do)
            dk = _dot(ds, q)
            dq = _dot_tn(ds, k)
            for g, cs in enumerate(heads):
                acc_ref[rows, cs] += dq[WINDOW * g:WINDOW * (g + 1), :]
                dqn = dq[st + WINDOW * g:st + WINDOW * (g + 1), :]
                if sub < r - 1:
                    acc_ref[nxt, cs] += dqn
                else:
                    carry_ref[:, cs] = dqn
                sink = jnp.max(sink_ref[kv, g:g + 1, :], axis=1, keepdims=True)
                p_sink = jnp.exp(sink + _lane_sum(q_cur[g].astype(F32), 67, 70))
                term = jnp.sum(p_sink * _lane_sum(do_cur[g].astype(F32), 64, 67), axis=0, keepdims=True)
                dsink_ref[kv, g:g + 1, :] += jnp.broadcast_to(term, (1, CHUNK))
            da_ref[rows, dk_col + CHUNK * kv:dk_col + CHUNK * (kv + 1)] = _rope_inv(
                jnp.where(low, dk, 0.0), cos_t[rows, :], sa_t[rows, :], sb_t[rows, :]).astype(BF16)
            da_ref[rows, dv_col + CHUNK * kv:dv_col + CHUNK * (kv + 1)] = jnp.where(low, dv, 0.0).astype(BF16)
        for pp in range(N_HEADS // 2):
            d = _pair(acc_ref[:, CHUNK * 2 * pp:CHUNK * (2 * pp + 1)], acc_ref[:, CHUNK * (2 * pp + 1):CHUNK * (2 * pp + 2)])
            da_ref[:, CHUNK * pp:CHUNK * (pp + 1)] = (_rope_inv(d, cos_t, sa_t, sb_t) * Q_SCALE).astype(BF16)

    cur = lambda w: pl.BlockSpec((tk, w), lambda j: (j, 0))
    nxt = pl.BlockSpec((WINDOW, AUG_W), lambda j: (jnp.minimum((j + 1) * r, nb - 1), 0))
    whole = pl.BlockSpec((KV_GROUPS, 8, CHUNK), lambda j: (0, 0, 0))
    return pl.pallas_call(
        body, name="swa_bwd", grid=(nt,),
        out_shape=[jax.ShapeDtypeStruct((s, 1024), BF16), jax.ShapeDtypeStruct((KV_GROUPS, 8, CHUNK), F32)],
        in_specs=[cur(KV_GROUPS * CHUNK), cur(KV_GROUPS * CHUNK), cur(AUG_W), nxt, cur(AUG_W), nxt,
                  cur(CHUNK), cur(CHUNK), cur(CHUNK), whole],
        out_specs=[cur(1024), whole],
        scratch_shapes=[pltpu.VMEM((WINDOW, AUG_W), F32), pltpu.VMEM((tk, AUG_W), F32)],
        compiler_params=_params(1))(k_aug, v_aug, q_bwd, q_bwd, do_aug, do_aug, cos, sa, sb, sink_rows)


def _fox_fwd(first_key, q_aug, k_aug, v_aug, t):
    s = q_aug.shape[0]
    pw = 2 * CHUNK

    def body(lo_ref, q_ref, k_ref, v_ref, o_ref, qb_ref):
        i = pl.program_id(1)
        first = lo_ref[pl.program_id(0), i]
        row = lax.broadcasted_iota(jnp.int32, (t, t), 0)
        col = lax.broadcasted_iota(jnp.int32, (t, t), 1)
        lane = _lane(CHUNK)
        heads = (slice(0, CHUNK), slice(CHUNK, pw))
        qs = [q_ref[:, hs] for hs in heads]

        def step(jb, carry, masked):
            rows = pl.ds(pl.multiple_of(jb * t, t), t)
            new = []
            for hh, hs in enumerate(heads):
                m, acc = carry[2 * hh], carry[2 * hh + 1]
                sc = _dot_nt(qs[hh], k_ref[rows, hs])
                if masked:
                    sc = jnp.where(col <= row, sc, NEG)
                m_new = jnp.maximum(m, jnp.max(sc, axis=1, keepdims=True))
                p = jnp.exp(sc - m_new).astype(BF16)
                new += [m_new, jnp.exp(m - m_new) * acc + _dot(p, v_ref[rows, hs])]
            return tuple(new)

        init = (jnp.full((t, 1), NEG, F32), jnp.zeros((t, CHUNK), F32)) * 2
        carry = step(i, lax.fori_loop(first, i, lambda jb, c: step(jb, c, False), init), True)
        outs = []
        for hh, hs in enumerate(heads):
            m, acc = carry[2 * hh], carry[2 * hh + 1]
            denom = _lane_sum(acc, 64, 65)
            outs.append(acc / denom)
            lse = m + jnp.log(denom)
            qb_ref[:, hs] = _place3(lane, 67, -lse, qs[hh].astype(F32)).astype(BF16)
        o_ref[...] = _pair(outs[0], outs[1])

    return pl.pallas_call(
        body, name="fox_fwd", grid=(N_HEADS // 2, s // t),
        out_shape=[jax.ShapeDtypeStruct((s, 512), F32), jax.ShapeDtypeStruct((s, AUG_W), BF16)],
        in_specs=[pl.BlockSpec(memory_space=pltpu.SMEM),
                  pl.BlockSpec((t, pw), lambda p, i: (i, p)),
                  pl.BlockSpec((s, pw), lambda p, i: (0, p)),
                  pl.BlockSpec((s, pw), lambda p, i: (0, p))],
        out_specs=[pl.BlockSpec((t, CHUNK), lambda p, i: (i, p)), pl.BlockSpec((t, pw), lambda p, i: (i, p))],
        compiler_params=_params(2))(first_key, q_aug, k_aug, v_aug)


def _fox_block_ranges(bounds, norms, r):
    cum_first = bounds[0::r, 0, :N_HEADS]
    cum_last = bounds[r - 1::r, 1, :N_HEADS]
    n = cum_first.shape[0]
    reach = 2.0 * 1.02 * jnp.sqrt(norms[0, :N_HEADS] * norms[1, :N_HEADS]) + 105.0
    blk = jnp.arange(n, dtype=jnp.int32)
    decay = cum_last[None, :, :] - cum_first[:, None, :]
    skip = (decay > reach[None, None, :]) & (blk[None, :, None] < blk[:, None, None])
    first = jnp.min(jnp.where(skip, n, blk[None, :, None]), axis=1)
    first = jnp.minimum(first[:, 0::2], first[:, 1::2]).T
    needed = (first[:, :, None] <= blk[None, None, :]) & (blk[None, :, None] >= blk[None, None, :])
    last = jnp.max(jnp.where(needed, blk[None, :, None], blk[None, None, :]), axis=1)
    return first.astype(jnp.int32), last.astype(jnp.int32)


def _fox_bwd(last_query, k_aug, v_aug, q_bwd, do_aug, t):
    s = k_aug.shape[0]
    n = s // t
    pw = 2 * CHUNK

    def body(hi_ref, k_ref, v_ref, q_ref, do_ref, dk_ref, dv_ref, dck_ref, dq_ref, dcq_ref, dq_scr):
        j = pl.program_id(1)
        last = hi_ref[pl.program_id(0), j]

        @pl.when(j == 0)
        def _():
            dq_scr[...] = jnp.zeros_like(dq_scr)

        row = lax.broadcasted_iota(jnp.int32, (t, t), 0)
        col = lax.broadcasted_iota(jnp.int32, (t, t), 1)
        lane = _lane(CHUNK)
        heads = (slice(0, CHUNK), slice(CHUNK, pw))
        ks = [k_ref[:, hs] for hs in heads]
        vs = [v_ref[:, hs] for hs in heads]

        def step(ib, carry, masked):
            rows = pl.ds(pl.multiple_of(ib * t, t), t)
            new = []
            for hh, hs in enumerate(heads):
                dv, dk = carry[2 * hh], carry[2 * hh + 1]
                q, do = q_ref[rows, hs], do_ref[rows, hs]
                st = _dot_nt(ks[hh], q)
                if masked:
                    st = jnp.where(row <= col, st, NEG)
                pt = jnp.exp(st)
                ds = (pt * _dot_nt(vs[hh], do)).astype(BF16)
                new += [dv + _dot(pt.astype(BF16), do), dk + _dot(ds, q)]
                dq_scr[rows, hs] += _dot_tn(ds, ks[hh])
            return tuple(new)

        zero = jnp.zeros((t, CHUNK), F32)
        carry = lax.fori_loop(j + 1, last + 1, lambda ib, c: step(ib, c, False), step(j, (zero,) * 4, True))
        dvs, dks = (carry[0], carry[2]), (carry[1], carry[3])
        dk_ref[...] = _pair(dks[0], dks[1]).astype(BF16)
        dv_ref[...] = _pair(dvs[0], dvs[1]).astype(BF16)
        dck_ref[0] = jnp.where(lane == 0, pltpu.roll(dks[0], 64, 1),
                               jnp.where(lane == 1, pltpu.roll(dks[1], 65, 1), 0.0))

        @pl.when(j == n - 1)
        def _():
            for ib in range(n):
                rows = slice(t * ib, t * (ib + 1))
                d0, d1 = dq_scr[rows, 0:CHUNK], dq_scr[rows, CHUNK:pw]
                dq_ref[rows, :] = (_pair(d0, d1) * Q_SCALE).astype(BF16)
                dcq_ref[0, rows, :] = jnp.where(lane == 0, pltpu.roll(d0, CHUNK - 67, 1),
                                                jnp.where(lane == 1, pltpu.roll(d1, CHUNK - 66, 1), 0.0))

    return pl.pallas_call(
        body, name="fox_bwd", grid=(N_HEADS // 2, n),
        out_shape=[jax.ShapeDtypeStruct((s, 512), BF16), jax.ShapeDtypeStruct((s, 512), BF16),
                   jax.ShapeDtypeStruct((N_HEADS // 2, s, CHUNK), F32), jax.ShapeDtypeStruct((s, 512), BF16),
                   jax.ShapeDtypeStruct((N_HEADS // 2, s, CHUNK), F32)],
        in_specs=[pl.BlockSpec(memory_space=pltpu.SMEM),
                  pl.BlockSpec((t, pw), lambda p, j: (j, p)), pl.BlockSpec((t, pw), lambda p, j: (j, p)),
                  pl.BlockSpec((s, pw), lambda p, j: (0, p)), pl.BlockSpec((s, pw), lambda p, j: (0, p))],
        out_specs=[pl.BlockSpec((t, CHUNK), lambda p, j: (j, p)), pl.BlockSpec((t, CHUNK), lambda p, j: (j, p)),
                   pl.BlockSpec((1, t, CHUNK), lambda p, j: (p, j, 0)), pl.BlockSpec((s, CHUNK), lambda p, j: (0, p)),
                   pl.BlockSpec((1, s, CHUNK), lambda p, j: (p, 0, 0))],
        scratch_shapes=[pltpu.VMEM((s, pw), F32)],
        compiler_params=_params(2))(last_query, k_aug, v_aug, q_bwd, do_aug)


def _fgate_bwd(dcq, dck, xf, tm):
    s = xf.shape[0]
    nt = s // tm

    def body(dq_ref, dc_ref, xf_ref, df_ref, dbf_ref, carry_ref):
        @pl.when(pl.program_id(0) == 0)
        def _():
            carry_ref[...] = jnp.zeros_like(carry_ref)
            dbf_ref[...] = jnp.zeros_like(dbf_ref)

        row = lax.broadcasted_iota(jnp.int32, (tm, tm), 0)
        col = lax.broadcasted_iota(jnp.int32, (tm, tm), 1)
        tri = jnp.where(col >= row, 1.0, 0.0).astype(BF16)
        dcum = dq_ref[...] - dc_ref[...]
        hi, mid, lo = _split3(dcum)
        dlogf = _dot(tri, hi.astype(BF16)) + _dot(tri, mid.astype(BF16)) + _dot(tri, lo.astype(BF16))
        dlogf = dlogf + carry_ref[0:1, :]
        carry_ref[0:1, :] += jnp.sum(dcum, axis=0, keepdims=True)
        df = dlogf * _sigmoid(-xf_ref[...])
        df_ref[...] = df.astype(BF16)
        dbf_ref[0:1, :] += jnp.sum(df, axis=0, keepdims=True)

    rev = pl.BlockSpec((tm, CHUNK), lambda i: (nt - 1 - i, 0))
    return pl.pallas_call(
        body, name="fgate_bwd", grid=(nt,),
        out_shape=[jax.ShapeDtypeStruct((s, CHUNK), BF16), jax.ShapeDtypeStruct((8, CHUNK), F32)],
        in_specs=[rev, rev, rev], out_specs=[rev, pl.BlockSpec((8, CHUNK), lambda i: (0, 0))],
        scratch_shapes=[pltpu.VMEM((8, CHUNK), F32)],
        compiler_params=_params(1))(dcq, dck, xf)


def _post(x, tgt, att_a, att_b, za, zb, ga, gb, woa_t, wob_t, wout, vec, ind, emat, tm):
    s = x.shape[0]

    def body(x_ref, t_ref, aa_ref, ab_ref, za_ref, zb_ref, ga_ref, gb_ref, woat_ref, wobt_ref, wout_ref,
             vec_ref, ind_ref, e_ref,
             dx1_ref, ua_ref, ub_ref, mg_ref, do_ref, dya_ref, dyb_ref, doa_ref, dob_ref, dzg_ref, acc_ref):
        @pl.when(pl.program_id(0) == 0)
        def _():
            acc_ref[...] = jnp.zeros_like(acc_ref)

        gate, gfin = vec_ref[0:1, :], vec_ref[1:2, :]
        inv_d = 1.0 / D_MODEL

        def branch_fwd(att_ref, z_ref, w_ref, u_ref):
            att, z = att_ref[...], z_ref[...]
            sz = _sigmoid(z)
            silu = z * sz
            u = (att * silu).astype(BF16)
            u_ref[...] = u.T
            return att, z, sz, silu, _dot_nt(u, w_ref[...])

        att_a, z_a, sz_a, silu_a, y_a = branch_fwd(aa_ref, za_ref, woat_ref, ua_ref)
        att_b, z_b, sz_b, silu_b, y_b = branch_fwd(ab_ref, zb_ref, wobt_ref, ub_ref)
        sg_a, sg_b = _sigmoid(ga_ref[...]), _sigmoid(gb_ref[...])
        merged = (sg_a * y_a + sg_b * y_b).astype(BF16)
        mg_ref[...] = merged.T
        o = _dot(merged, wout_ref[...])
        x1 = x_ref[...] + gate * o
        rstd = lax.rsqrt(jnp.mean(x1 * x1, axis=-1, keepdims=True) + NORM_EPS)
        xh = x1 * rstd
        diff = xh * gfin - t_ref[...]
        acc_ref[2:3, :] += (0.5 * inv_d) * jnp.sum(diff * diff, axis=0, keepdims=True)
        dy = diff * inv_d
        acc_ref[1:2, :] += jnp.sum(dy * xh, axis=0, keepdims=True)
        dyg = dy * gfin
        dx1 = rstd * (dyg - xh * jnp.mean(dyg * xh, axis=-1, keepdims=True))
        dx1_ref[...] = dx1
        acc_ref[0:1, :] += jnp.sum(dx1 * o, axis=0, keepdims=True)
        d_o = (dx1 * gate).astype(BF16)
        do_ref[...] = d_o
        dmg = _dot_nt(d_o, wout_ref[...])

        lane = _lane(CHUNK)
        low = lane < HEAD_DIM
        lane_all = _lane(AUG_W) & (CHUNK - 1)

        zg0 = SEG["za"][0]

        def branch_bwd(sg, y, att, z, sz, silu, wt_ref, dy_ref, g_name, z_name, doaug_ref):
            dyb = (dmg * sg).astype(BF16)
            dy_ref[...] = dyb
            g_off, z_off = SEG[g_name][0] - zg0, SEG[z_name][0] - zg0
            dzg_ref[:, g_off:g_off + D_MODEL] = (dmg * y * sg * (1.0 - sg)).astype(BF16)
            du = _dot(dyb, wt_ref[...])
            datt = du * silu
            dzg_ref[:, z_off:z_off + 512] = (du * att * (sz * (1.0 + z * (1.0 - sz)))).astype(BF16)
            extra = _spread3(-_dot_split(datt * att, ind_ref[...]), e_ref)
            for p in range(N_HEADS // 2):
                c = datt[:, CHUNK * p:CHUNK * (p + 1)]
                for hh in range(2):
                    hd = 2 * p + hh
                    src = c if hh == 0 else pltpu.roll(c, HEAD_DIM, 1)
                    doaug_ref[:, CHUNK * hd:CHUNK * (hd + 1)] = jnp.where(
                        low, src, extra[:, CHUNK * hd:CHUNK * (hd + 1)]).astype(BF16)

        branch_bwd(sg_a, y_a, att_a, z_a, sz_a, silu_a, woat_ref, dya_ref, "ga", "za", doa_ref)
        branch_bwd(sg_b, y_b, att_b, z_b, sz_b, silu_b, wobt_ref, dyb_ref, "gb", "zb", dob_ref)

    row_blk = lambda w: pl.BlockSpec((tm, w), lambda i: (i, 0))
    full = lambda a: pl.BlockSpec(a.shape, lambda i: (0,) * a.ndim)
    sds = lambda w, dt: jax.ShapeDtypeStruct((s, w), dt)
    tds = lambda w: jax.ShapeDtypeStruct((w, s), BF16)
    col_blk = lambda w: pl.BlockSpec((w, tm), lambda i: (0, i))
    out_shape = [sds(D_MODEL, F32), tds(512), tds(512), tds(D_MODEL), sds(D_MODEL, BF16),
                 sds(D_MODEL, BF16), sds(D_MODEL, BF16), sds(AUG_W, BF16), sds(AUG_W, BF16), sds(3072, BF16),
                 jax.ShapeDtypeStruct((8, D_MODEL), F32)]
    ins = [x, tgt, att_a, att_b, za, zb, ga, gb, woa_t, wob_t, wout, vec, ind, emat]
    in_specs = [row_blk(a.shape[1]) for a in ins[:8]] + [full(a) for a in ins[8:]]
    out_specs = ([row_blk(D_MODEL), col_blk(512), col_blk(512), col_blk(D_MODEL)]
                 + [row_blk(o.shape[1]) for o in out_shape[4:-1]] + [pl.BlockSpec((8, D_MODEL), lambda i: (0, 0))])
    return pl.pallas_call(body, name="post", grid=(s // tm,), out_shape=out_shape, in_specs=in_specs,
                          out_specs=out_specs, compiler_params=_params(1))(*ins)


def _bwd_pre(dproj, w_all, x, dx1, mod, tm):
    s = x.shape[0]
    ng = len(DPROJ_GROUPS)

    def body(*refs):
        dp_refs = refs[:ng]
        wt_ref, x_ref, dx1_ref, mod_ref, gx_ref, acc_ref = refs[ng:]

        @pl.when(pl.program_id(0) == 0)
        def _():
            acc_ref[...] = jnp.zeros_like(acc_ref)

        dh = None
        for dp_ref, (_, off, w) in zip(dp_refs, DPROJ_GROUPS):
            part = _dot_nt(dp_ref[...], wt_ref[:, off:off + w])
            dh = part if dh is None else dh + part
        xv = x_ref[...]
        rstd = lax.rsqrt(jnp.mean(xv * xv, axis=-1, keepdims=True) + NORM_EPS)
        xh = xv * rstd
        gn = mod_ref[0:1, :]
        acc_ref[0:1, :] += jnp.sum(dh, axis=0, keepdims=True)
        acc_ref[1:2, :] += jnp.sum(dh * (xh * gn), axis=0, keepdims=True)
        dn = dh * (1.0 + mod_ref[1:2, :])
        acc_ref[2:3, :] += jnp.sum(dn * xh, axis=0, keepdims=True)
        dxh = dn * gn
        gx_ref[...] = dx1_ref[...] + rstd * (dxh - xh * jnp.mean(dxh * xh, axis=-1, keepdims=True))

    row_blk = lambda w: pl.BlockSpec((tm, w), lambda i: (i, 0))
    full = lambda a: pl.BlockSpec(a.shape, lambda i: (0,) * a.ndim)
    return pl.pallas_call(
        body, name="bwd_pre", grid=(s // tm,),
        out_shape=[jax.ShapeDtypeStruct((s, D_MODEL), F32), jax.ShapeDtypeStruct((8, D_MODEL), F32)],
        in_specs=[row_blk(w) for _, _, w in DPROJ_GROUPS] + [full(w_all), row_blk(D_MODEL), row_blk(D_MODEL), full(mod)],
        out_specs=[row_blk(D_MODEL), pl.BlockSpec((8, D_MODEL), lambda i: (0, 0))],
        compiler_params=_params(1))(*dproj, w_all, x, dx1, mod)


def _token_matmul(a_t, b, tn, tk, name):
    m, s = a_t.shape
    n = b.shape[1]

    def body(a_ref, b_ref, o_ref):
        @pl.when(pl.program_id(1) == 0)
        def _():
            o_ref[...] = jnp.zeros_like(o_ref)

        o_ref[...] += _dot(a_ref[...], b_ref[...])

    return pl.pallas_call(
        body, name=name, grid=(n // tn, s // tk), out_shape=jax.ShapeDtypeStruct((m, n), F32),
        in_specs=[pl.BlockSpec((m, tk), lambda jn, k: (0, k)), pl.BlockSpec((tk, tn), lambda jn, k: (k, jn))],
        out_specs=pl.BlockSpec((m, tn), lambda jn, k: (0, jn)),
        compiler_params=_params(2))(a_t, b)


def _adam_math(g, w, m, v):
    m2 = ADAM_B1 * m + (1.0 - ADAM_B1) * g
    v2 = ADAM_B2 * v + (1.0 - ADAM_B2) * (g * g)
    delta = -ADAM_LR * ((m2 / ADAM_C1) / (jnp.sqrt(v2 / ADAM_C2) + ADAM_EPS) + ADAM_WD * w)
    return delta, m2, v2


def _adamw_parts(own, parts, others, w, m, v, tr, name):
    rws, cols = w.shape

    def body(oth_ref, own_ref, p_ref, w_ref, m_ref, v_ref, g_ref, d_ref, m2_ref, v2_ref):
        g = own_ref[...]
        for chip in range(N_CHIPS):
            g = g + oth_ref[chip] * p_ref[chip].astype(F32)
        g_ref[...] = g
        d_ref[...], m2_ref[...], v2_ref[...] = _adam_math(g, w_ref[...], m_ref[...], v_ref[...])

    blk = pl.BlockSpec((tr, cols), lambda i: (i, 0))
    o = jax.ShapeDtypeStruct((rws, cols), F32)
    return pl.pallas_call(
        body, name=name, grid=(rws // tr,), out_shape=[o, o, o, o],
        in_specs=[pl.BlockSpec(memory_space=pltpu.SMEM), blk,
                  pl.BlockSpec((N_CHIPS, tr, cols), lambda i: (0, i, 0)), blk, blk, blk],
        out_specs=[blk, blk, blk, blk], compiler_params=_params(1))(others, own, parts, w, m, v)


def _adamw_ada(c_lanes, d_rows, w, m, v, tr):
    rws, cols = w.shape

    def body(c_ref, d_ref, w_ref, m_ref, v_ref, g_ref, dl_ref, m2_ref, v2_ref):
        for k in range(cols // CHUNK):
            cs = slice(CHUNK * k, CHUNK * (k + 1))
            g = c_ref[0] * d_ref[0:1, cs]
            for b in range(1, N_DEV):
                g = g + c_ref[b] * d_ref[b:b + 1, cs]
            g_ref[:, cs] = g
            dl_ref[:, cs], m2_ref[:, cs], v2_ref[:, cs] = _adam_math(g, w_ref[:, cs], m_ref[:, cs], v_ref[:, cs])

    blk = pl.BlockSpec((tr, cols), lambda i: (i, 0))
    o = jax.ShapeDtypeStruct((rws, cols), F32)
    return pl.pallas_call(
        body, name="adamw_ada", grid=(rws // tr,), out_shape=[o, o, o, o],
        in_specs=[pl.BlockSpec((N_DEV, tr, CHUNK), lambda i: (0, i, 0)), pl.BlockSpec((N_DEV, cols), lambda i: (0, 0)),
                  blk, blk, blk],
        out_specs=[blk, blk, blk, blk], compiler_params=_params(1))(c_lanes, d_rows, w, m, v)


def _adamw_small(packs, w, m, v):
    def body(p_ref, w_ref, m_ref, v_ref, g_ref, d_ref, m2_ref, v2_ref, loss_ref):
        g = p_ref[0]
        for dev in range(1, N_DEV):
            g = g + p_ref[dev]
        g_ref[...] = g
        d_ref[...], m2_ref[...], v2_ref[...] = _adam_math(g, w_ref[...], m_ref[...], v_ref[...])
        loss_ref[...] = jnp.broadcast_to(jnp.sum(g_ref[2:3, 0:D_MODEL], axis=1, keepdims=True), loss_ref.shape)

    o = jax.ShapeDtypeStruct(w.shape, F32)
    return pl.pallas_call(body, name="adamw_small", out_shape=[o, o, o, o, jax.ShapeDtypeStruct((8, CHUNK), F32)],
                          compiler_params=_params(0))(packs, w, m, v)


def _tile(s, want):
    return min(s, want)


def _layout_pieces():
    orig = {"qa": 0, "ka": 512, "va": 640, "za": 768, "qb": 1280, "kb": 1792, "vb": 2304, "f": 2816, "zb": 2824,
            "ga": 3336, "gb": 4360}
    pieces = []
    for name, (off, w) in SEG.items():
        if name in ("ka", "va"):
            pieces += [(orig[name] + HEAD_DIM * kv, orig[name] + HEAD_DIM * (kv + 1), off + CHUNK * kv) for kv in range(KV_GROUPS)]
        else:
            pieces.append((orig[name], orig[name] + (N_HEADS if name == "f" else w), off))
    return pieces


def _assemble_w_all(win_g):
    cols, pos = [], 0
    for lo, hi, new in sorted(_layout_pieces(), key=lambda t: t[2]):
        if new > pos:
            cols.append(jnp.zeros((D_MODEL, new - pos), win_g.dtype))
        col = lo
        while col < hi:
            dev = col // IN_SHARD
            end = min(hi, (dev + 1) * IN_SHARD)
            cols.append(win_g[dev, :, col - dev * IN_SHARD:end - dev * IN_SHARD])
            col = end
        pos = new + hi - lo
    cols.append(jnp.zeros((D_MODEL, PROJ_W - pos), win_g.dtype))
    return jnp.concatenate(cols, axis=1)


def _grad_slot(dw_groups, dev):
    lo_d, hi_d = dev * IN_SHARD, (dev + 1) * IN_SHARD
    cols = []
    for lo, hi, new in sorted(_layout_pieces()):
        a, b = max(lo, lo_d), min(hi, hi_d)
        if a < b:
            arr, off = next((g, o) for g, (_, o, w) in zip(dw_groups, DPROJ_GROUPS) if o <= new < o + w)
            cols.append(arr[:, new - off + a - lo:new - off + b - lo])
    return jnp.concatenate(cols, axis=1)


def kernel(x, c, positions, w_ada, b_ada, g_norm, w_in, b_f, sinks, w_o_swa, w_o_fox, w_out, g_final, loss_target, m_w_ada, m_b_ada, m_g_norm, m_w_in, m_b_f, m_sinks, m_w_o_swa, m_w_o_fox, m_w_out, m_g_final, v_w_ada, v_b_ada, v_g_norm, v_w_in, v_b_f, v_sinks, v_w_o_swa, v_w_o_fox, v_w_out, v_g_final):
    s = x.shape[1]
    tm = _tile(s, 256)
    ta = _tile(s, 512)
    me = 4 * lax.axis_index("x") + 2 * lax.axis_index("y") + lax.axis_index("c")
    x2, tgt = x[0], loss_target[0]

    (c_all,) = _exchange([jnp.broadcast_to(c, (8, D_MODEL))], ["gather"], "gather_c")
    c_all = c_all[:, 0, :]
    b_shard = lax.dynamic_slice(b_ada, (0, me * ADA_SHARD), (1, ADA_SHARD))
    ada_part = _ada_fwd(c_all, w_ada[0], b_shard)
    ada_g, win_g, woa_g, wob_g, wout_g = _gather_two_level(
        [ada_part, w_in[0].astype(BF16), w_o_swa[0].T.astype(BF16), w_o_fox[0].T.astype(BF16), w_out[0].astype(BF16)],
        "gather_weights")
    ada = lax.dynamic_index_in_dim(ada_g, me, axis=1, keepdims=False).reshape(3 * D_MODEL)
    shift, scale, gate = ada[:D_MODEL], ada[D_MODEL:2 * D_MODEL], ada[2 * D_MODEL:]
    woa_t = woa_g.reshape(D_MODEL, 512)
    wob_t = wob_g.reshape(D_MODEL, 512)
    wout = wout_g.reshape(D_MODEL, D_MODEL)
    w_all = _assemble_w_all(win_g)

    zrow = jnp.zeros((1, D_MODEL), F32)
    mod = jnp.concatenate([g_norm, scale[None], shift[None], zrow, zrow, zrow, zrow, zrow], axis=0)
    bf_row = jnp.pad(b_f, ((0, 0), (0, CHUNK - N_HEADS)))
    emat_np = np.zeros((3, CHUNK, AUG_W), np.float32)
    ind_np = np.zeros((512, CHUNK), np.float32)
    for hd in range(N_HEADS):
        for part in range(3):
            emat_np[part, hd, CHUNK * hd + 64 + part] = 1.0
        ind_np[HEAD_DIM * hd:HEAD_DIM * (hd + 1), hd] = 1.0
    emat, ind = jnp.asarray(emat_np, BF16), jnp.asarray(ind_np, BF16)
    inv_freq = np.power(np.float32(ROPE_THETA), -np.arange(0, HEAD_DIM, 2, dtype=np.float32) / HEAD_DIM)
    inv_freq = jnp.asarray(np.tile(inv_freq, CHUNK // (HEAD_DIM // 2))[None, :], F32)
    cos, sa, sb = _rope_tables(positions.reshape(s, 1).astype(F32), inv_freq, tm)
    sink_rows = jnp.broadcast_to(jnp.pad(sinks.reshape(KV_GROUPS, GROUP), ((0, 0), (0, 8 - GROUP)))[:, :, None],
                                 (KV_GROUPS, 8, CHUNK))

    (h_t, qa, ka, va, za, qb, kb, vb, zb, ga, gb, xf, bounds, norms) = _fwd_proj(
        x2, mod, w_all, cos, sa, sb, bf_row, emat, ind, tm)
    first_key, last_query = _fox_block_ranges(bounds, norms, ta // tm)
    att_a, qa_bwd = _swa_fwd(qa, ka, va, sink_rows, ta)
    att_b, qb_bwd = _fox_fwd(first_key, qb, kb, vb, ta)
    vec = jnp.concatenate([gate[None], g_final[None], zrow, zrow, zrow, zrow, zrow, zrow], axis=0)
    (dx1, ua_t, ub_t, merged_t, d_o, dya, dyb, doa, dob, dzg, post_acc) = _post(
        x2, tgt, att_a, att_b, za, zb, ga, gb, woa_t, wob_t, wout, vec, ind, emat, tm)

    da, dsink = _swa_bwd(ka, va, qa_bwd, doa, cos, sa, sb, sink_rows, ta)
    dkb, dvb, dck, dqb, dcq = _fox_bwd(last_query, kb, vb, qb_bwd, dob, ta)
    by_head = lambda a: jnp.pad(a[:, :, :2].transpose(1, 0, 2).reshape(s, N_HEADS), ((0, 0), (0, CHUNK - N_HEADS)))
    dfb, dbf_acc = _fgate_bwd(by_head(dcq), by_head(dck), xf, tm)
    dproj = [da, dqb, dkb, dvb, dzg, dfb]
    grad_x, pre_acc = _bwd_pre(dproj, w_all, x2, dx1, mod, tm)
    dw_all = [_token_matmul(h_t, d, min(w, 1024), ta, "dw_in_" + nm) for d, (nm, _, w) in zip(dproj, DPROJ_GROUPS)]
    dwoa = _token_matmul(ua_t, dya, 1024, ta, "dw_o_swa")
    dwob = _token_matmul(ub_t, dyb, 1024, ta, "dw_o_fox")
    dwout = _token_matmul(merged_t, d_o, 1024, ta, "dw_out")


    misc = jnp.concatenate([dbf_acc[0, :N_HEADS], dsink[:, :GROUP, 0].reshape(N_HEADS), jnp.zeros((D_MODEL - 16,), F32)])
    zpad = jnp.zeros((2 * D_MODEL,), F32)
    pack = jnp.stack([jnp.concatenate([pre_acc[0], pre_acc[1], post_acc[0]]),
                      jnp.concatenate([pre_acc[2], post_acc[1], misc]),
                      jnp.concatenate([post_acc[2], zpad])] + [jnp.zeros((3 * D_MODEL,), F32)] * 5)
    core, chip = lax.axis_index("c"), 2 * lax.axis_index("x") + lax.axis_index("y")
    col_slots = lambda g, r, w: g.reshape(r, N_CHIPS, 2, w).transpose(2, 1, 0, 3)
    win_slots = jnp.stack([jnp.stack([_grad_slot(dw_all, 2 * ch + co) for ch in range(N_CHIPS)]) for co in range(2)])
    slots = [win_slots, col_slots(dwoa, 512, 128), col_slots(dwob, 512, 128),
             dwout.reshape(N_CHIPS, 2, 128, D_MODEL).transpose(1, 0, 2, 3)]
    theirs = _swap_sibling(slots, "grads_sibling")
    own, halves = [], []
    for sl, th, nm in zip(slots, theirs, ("w_in", "w_o_swa", "w_o_fox", "w_out")):
        full, half = _chip_partial(lax.dynamic_index_in_dim(sl, core, 0, keepdims=False), th, min(th.shape[1], 256),
                                   "chip_partial_" + nm)
        own.append(lax.dynamic_index_in_dim(full, chip, 0, keepdims=False))
        halves.append(half)
    p_win, p_woa, p_wob, p_wout, packs = _exchange(halves + [pack], ["chips"] * 4 + ["gather"], "exchange_grads")
    others = jnp.where(jnp.arange(N_CHIPS) == chip, 0.0, 1.0).astype(F32)

    g_win, d_win, m_win, v_win = _adamw_parts(own[0], p_win, others, w_in[0], m_w_in[0], v_w_in[0], 128, "adamw_w_in")
    g_woa, d_woa, m_woa, v_woa = _adamw_parts(own[1], p_woa, others, w_o_swa[0], m_w_o_swa[0], v_w_o_swa[0], 128, "adamw_w_o_swa")
    g_wob, d_wob, m_wob, v_wob = _adamw_parts(own[2], p_wob, others, w_o_fox[0], m_w_o_fox[0], v_w_o_fox[0], 128, "adamw_w_o_fox")
    g_wout, d_wout, m_wout, v_wout = _adamw_parts(own[3], p_wout, others, w_out[0], m_w_out[0], v_w_out[0], 128, "adamw_w_out")
    d_ada_rows = lax.dynamic_slice(packs[:, 0, :], (0, me * ADA_SHARD), (N_DEV, ADA_SHARD))
    c_lanes = jnp.broadcast_to(c_all[:, :, None], (N_DEV, D_MODEL, CHUNK))
    g_wada, d_wada, m_wada, v_wada = _adamw_ada(c_lanes, d_ada_rows, w_ada[0], m_w_ada[0], v_w_ada[0], 256)

    def small_pack(bada, gn, gf, bfv, sk):
        misc_w = jnp.concatenate([bfv[0], sk[0], jnp.zeros((D_MODEL - 16,), F32)])
        return jnp.stack([bada[0], jnp.concatenate([gn[0], gf, misc_w])] + [jnp.zeros((3 * D_MODEL,), F32)] * 6)

    sm = _adamw_small(packs, small_pack(b_ada, g_norm, g_final, b_f, sinks),
                      small_pack(m_b_ada, m_g_norm, m_g_final, m_b_f, m_sinks),
                      small_pack(v_b_ada, v_g_norm, v_g_final, v_b_f, v_sinks))
    loss = sm[4][0, 0]

    def unpack(p):
        return (p[0][None], p[1, :D_MODEL][None], p[1, 2 * D_MODEL:2 * D_MODEL + 8][None],
                p[1, 2 * D_MODEL + 8:2 * D_MODEL + 16][None], p[1, D_MODEL:2 * D_MODEL])

    outs = []
    for big, small in (((g_wada, g_win, g_woa, g_wob, g_wout), sm[0]), ((d_wada, d_win, d_woa, d_wob, d_wout), sm[1]),
                       ((m_wada, m_win, m_woa, m_wob, m_wout), sm[2]), ((v_wada, v_win, v_woa, v_wob, v_wout), sm[3])):
        wada_o, win_o, woa_o, wob_o, wout_o = big
        bada_o, gn_o, bf_o, sk_o, gf_o = unpack(small)
        outs += [wada_o[None], bada_o, gn_o, win_o[None], bf_o, sk_o, woa_o[None], wob_o[None], wout_o[None], gf_o]
    return (loss, grad_x[None], *outs)
```

```python
import numpy as np
import jax
import jax.numpy as jnp
from jax import lax
from jax.experimental import pallas as pl
from jax.experimental.pallas import tpu as pltpu

F32 = jnp.float32
BF16 = jnp.bfloat16

D_MODEL = 1024
HEAD_DIM = 64
N_HEADS = 8
KV_GROUPS = 2
GROUP = N_HEADS // KV_GROUPS
WINDOW = 128
CHUNK = 128
AUG_W = N_HEADS * CHUNK
N_DEV = 8
IN_SHARD = 673
ADA_SHARD = 384
NORM_EPS = 1e-6
ROPE_THETA = 10000.0
Q_SCALE = HEAD_DIM ** -0.5
NEG = -1e30

ADAM_LR = 0.001
ADAM_B1 = 0.9
ADAM_B2 = 0.999
ADAM_EPS = 1e-08
ADAM_WD = 0.01
ADAM_STEP = 10
ADAM_C1 = 1.0 - ADAM_B1 ** ADAM_STEP
ADAM_C2 = 1.0 - ADAM_B2 ** ADAM_STEP

SEG = {}
_off = 0
for _name, _w in (("qa", 512), ("ka", 256), ("va", 256), ("qb", 512), ("kb", 512), ("vb", 512),
                  ("za", 512), ("zb", 512), ("ga", 1024), ("gb", 1024), ("f", 128)):
    SEG[_name] = (_off, _w)
    _off += _w
PROJ_W = _off
DPROJ_GROUPS = (("a", SEG["qa"][0], 1024), ("qb", SEG["qb"][0], 512), ("kb", SEG["kb"][0], 512),
                ("vb", SEG["vb"][0], 512), ("zg", SEG["za"][0], 3072), ("f", SEG["f"][0], 128))

VMEM_LIMIT = 56 * 1024 * 1024


def _params(n_axes):
    return pltpu.CompilerParams(dimension_semantics=("arbitrary",) * n_axes, vmem_limit_bytes=VMEM_LIMIT)


def _dot(a, b):
    return jnp.dot(a, b, preferred_element_type=F32)


def _dot_nt(a, b):
    return lax.dot_general(a, b, (((1,), (1,)), ((), ())), preferred_element_type=F32)


def _dot_tn(a, b):
    return lax.dot_general(a, b, (((0,), (0,)), ((), ())), preferred_element_type=F32)


def _lane(n):
    return lax.broadcasted_iota(jnp.int32, (1, n), 1)


def _split3(x):
    hi = x.astype(BF16).astype(F32)
    r = x - hi
    mid = r.astype(BF16).astype(F32)
    lo = (r - mid).astype(BF16).astype(F32)
    return hi, mid, lo


def _dot_split(x, b):
    hi, mid, lo = _split3(x)
    return _dot(hi.astype(BF16), b) + _dot(mid.astype(BF16), b) + _dot(lo.astype(BF16), b)


def _spread3(x, e3_ref):
    hi, mid, lo = _split3(x)
    return _dot(hi.astype(BF16), e3_ref[0]) + _dot(mid.astype(BF16), e3_ref[1]) + _dot(lo.astype(BF16), e3_ref[2])


def _place3(lane, base, x, other):
    hi, mid, lo = _split3(x)
    return jnp.where(lane == base, hi, jnp.where(lane == base + 1, mid, jnp.where(lane == base + 2, lo, other)))


def _lane_sum(x, lo, hi):
    lane = _lane(x.shape[1])
    return jnp.sum(jnp.where((lane >= lo) & (lane < hi), x, 0.0), axis=1, keepdims=True)


def _ones_lanes(lo, hi):
    lane = _lane(CHUNK)
    return jnp.where((lane >= lo) & (lane < hi), 1.0, 0.0).astype(F32)


def _rope(c, cos, sa, sb):
    return c * cos + pltpu.roll(c, CHUNK - 32, 1) * sa + pltpu.roll(c, 32, 1) * sb


def _rope_inv(d, cos, sa, sb):
    return d * cos - (pltpu.roll(d, CHUNK - 32, 1) * sa + pltpu.roll(d, 32, 1) * sb)


def _pair(c0, c1):
    return jnp.where(_lane(CHUNK) < HEAD_DIM, c0, pltpu.roll(c1, HEAD_DIM, 1))


def _sigmoid(x):
    return 0.5 * jnp.tanh(0.5 * x) + 0.5


N_CHIPS = 4
ANY_SPEC = pl.BlockSpec(memory_space=pl.ANY)


def _exchange(arrays, modes, name):
    n = len(arrays)
    out_shape = [jax.ShapeDtypeStruct((N_DEV,) + a.shape if md == "gather" else a.shape, a.dtype)
                 for a, md in zip(arrays, modes)]

    def body(*refs):
        srcs, dsts = refs[:n], refs[n:2 * n]
        send_sems, recv_sems, loc_sems = refs[2 * n:]
        x, y, c = lax.axis_index("x"), lax.axis_index("y"), lax.axis_index("c")

        def slot(i, px, py, pc):
            return 4 * px + 2 * py + pc if modes[i] == "gather" else 2 * px + py

        def src_of(i, px, py, pc):
            return srcs[i] if modes[i] == "gather" else srcs[i].at[slot(i, px, py, pc)]

        local = [pltpu.make_async_copy(src_of(i, x, y, c), dsts[i].at[slot(i, x, y, c)], loc_sems.at[i])
                 for i in range(n)]
        for cp in local:
            cp.start()
        sends, recvs = [], []
        for r in (1, 2, 4, 3, 5, 6, 7):
            px = 1 - x if r & 4 else x
            py = 1 - y if r & 2 else y
            pc = 1 - c if r & 1 else c
            for i in range(n):
                if modes[i] == "chips" and r & 1:
                    continue
                sems = dict(send_sem=send_sems.at[i, r - 1], recv_sem=recv_sems.at[i, r - 1],
                            device_id=(px, py, pc), device_id_type=pl.DeviceIdType.MESH)
                sends.append(pltpu.make_async_remote_copy(
                    src_ref=src_of(i, px, py, pc), dst_ref=dsts[i].at[slot(i, x, y, c)], **sems))
                recvs.append(pltpu.make_async_remote_copy(
                    src_ref=src_of(i, px, py, pc), dst_ref=dsts[i].at[slot(i, px, py, pc)], **sems))
        for cp in sends:
            cp.start()
        for cp in recvs:
            cp.wait_recv()
        for cp in sends:
            cp.wait_send()
        for cp in local:
            cp.wait()

    return pl.pallas_call(
        body, name=name, out_shape=out_shape,
        in_specs=[ANY_SPEC] * n, out_specs=[ANY_SPEC] * n,
        scratch_shapes=[pltpu.SemaphoreType.DMA((n, N_DEV - 1)), pltpu.SemaphoreType.DMA((n, N_DEV - 1)),
                        pltpu.SemaphoreType.DMA((n,))],
    )(*arrays)


def _gather_two_level(arrays, name):
    n = len(arrays)
    out_shape = [jax.ShapeDtypeStruct((N_DEV,) + a.shape, a.dtype) for a in arrays]

    def body(*refs):
        srcs, dsts = refs[:n], refs[n:2 * n]
        send_sems, recv_sems, loc_sems = refs[2 * n:]
        x, y, c = lax.axis_index("x"), lax.axis_index("y"), lax.axis_index("c")
        me, sibling = (x, y, c), (x, y, 1 - c)
        chips = [(1 - x, y), (x, 1 - y), (1 - x, 1 - y)]

        def rows(i, dev):
            return dsts[i].at[4 * dev[0] + 2 * dev[1] + dev[2]]

        def copy(i, k, block, to, src=None):
            return pltpu.make_async_remote_copy(
                src_ref=rows(i, block) if src is None else src, dst_ref=rows(i, block),
                send_sem=send_sems.at[i, k], recv_sem=recv_sems.at[i, k],
                device_id=to, device_id_type=pl.DeviceIdType.MESH)

        local = [pltpu.make_async_copy(srcs[i], rows(i, me), loc_sems.at[i]) for i in range(n)]
        for cp in local:
            cp.start()
        first = []
        for i in range(n):
            first.append(copy(i, 0, me, sibling, src=srcs[i]))
            first += [copy(i, 1 + j, me, (*chip, c), src=srcs[i]) for j, chip in enumerate(chips)]
        for cp in first:
            cp.start()
        passed = []
        for j, chip in enumerate(chips):
            for i in range(n):
                copy(i, 1 + j, (*chip, c), me).wait_recv()
                cp = copy(i, 4 + j, (*chip, c), sibling)
                cp.start()
                passed.append(cp)
        for i in range(n):
            copy(i, 0, sibling, me).wait_recv()
            for j, chip in enumerate(chips):
                copy(i, 4 + j, (*chip, 1 - c), me).wait_recv()
        for cp in first + passed:
            cp.wait_send()
        for cp in local:
            cp.wait()

    return pl.pallas_call(
        body, name=name, out_shape=out_shape,
        in_specs=[ANY_SPEC] * n, out_specs=[ANY_SPEC] * n,
        scratch_shapes=[pltpu.SemaphoreType.DMA((n, N_DEV - 1)), pltpu.SemaphoreType.DMA((n, N_DEV - 1)),
                        pltpu.SemaphoreType.DMA((n,))],
    )(*arrays)


def _swap_sibling(arrays, name):
    n = len(arrays)
    out_shape = [jax.ShapeDtypeStruct(a.shape[1:], a.dtype) for a in arrays]

    def body(*refs):
        srcs, dsts = refs[:n], refs[n:2 * n]
        send_sems, recv_sems = refs[2 * n:]
        x, y, c = lax.axis_index("x"), lax.axis_index("y"), lax.axis_index("c")
        copies = [pltpu.make_async_remote_copy(
            src_ref=srcs[i].at[1 - c], dst_ref=dsts[i], send_sem=send_sems.at[i], recv_sem=recv_sems.at[i],
            device_id=(x, y, 1 - c), device_id_type=pl.DeviceIdType.MESH) for i in range(n)]
        for cp in copies:
            cp.start()
        for cp in copies:
            cp.wait()

    return pl.pallas_call(
        body, name=name, out_shape=out_shape, in_specs=[ANY_SPEC] * n, out_specs=[ANY_SPEC] * n,
        scratch_shapes=[pltpu.SemaphoreType.DMA((n,)), pltpu.SemaphoreType.DMA((n,))],
    )(*arrays)


def _chip_partial(core, slots, theirs, tr, name):
    _, k, rws, cols = slots.shape

    def body(core_ref, a_ref, b_ref, f_ref, h_ref):
        sm = a_ref[0] + b_ref[...]
        f_ref[...] = sm
        h_ref[...] = sm.astype(BF16)

    blk = pl.BlockSpec((1, tr, cols), lambda j, i, core_ref: (j, i, 0))
    mine = pl.BlockSpec((1, 1, tr, cols), lambda j, i, core_ref: (core_ref[0], j, i, 0))
    return pl.pallas_call(
        body, name=name,
        grid_spec=pltpu.PrefetchScalarGridSpec(num_scalar_prefetch=1, grid=(k, rws // tr),
                                               in_specs=[mine, blk], out_specs=[blk, blk]),
        out_shape=[jax.ShapeDtypeStruct(theirs.shape, F32), jax.ShapeDtypeStruct(theirs.shape, BF16)],
        compiler_params=_params(2))(core, slots, theirs)


def _ada_fwd(c_all, w_shard, b_shard):
    def body(c_ref, w_ref, b_ref, o_ref):
        ch, cm, cl = [t.astype(BF16) for t in _split3(c_ref[...])]
        wh, wm, wl = [t.astype(BF16) for t in _split3(w_ref[...])]
        acc = _dot(ch, wh) + _dot(ch, wm) + _dot(cm, wh) + _dot(ch, wl) + _dot(cl, wh) + _dot(cm, wm)
        o_ref[...] = acc + b_ref[...]

    return pl.pallas_call(body, name="ada_fwd", out_shape=jax.ShapeDtypeStruct((N_DEV, ADA_SHARD), F32),
                          compiler_params=_params(0))(c_all, w_shard, b_shard)


def _rope_tables(pos_col, inv_freq, tm):
    s = pos_col.shape[0]

    def body(p_ref, f_ref, cos_ref, sa_ref, sb_ref):
        ang = p_ref[...] * f_ref[...]
        sin = jnp.sin(ang)
        first_half = (_lane(CHUNK) & (HEAD_DIM - 1)) < HEAD_DIM // 2
        cos_ref[...] = jnp.cos(ang)
        sa_ref[...] = jnp.where(first_half, -sin, 0.0)
        sb_ref[...] = jnp.where(first_half, 0.0, sin)

    tab = jax.ShapeDtypeStruct((s, CHUNK), F32)
    blk = pl.BlockSpec((tm, CHUNK), lambda i: (i, 0))
    return pl.pallas_call(
        body, name="rope_tables", grid=(s // tm,), out_shape=[tab, tab, tab],
        in_specs=[pl.BlockSpec((tm, 1), lambda i: (i, 0)), pl.BlockSpec((1, CHUNK), lambda i: (0, 0))],
        out_specs=[blk, blk, blk], compiler_params=_params(1))(pos_col, inv_freq)


def _fwd_proj(x, mod, w_all, cos, sa, sb, bf_row, emat, ind, tm):
    s = x.shape[0]

    def body(x_ref, mod_ref, w_ref, cos_ref, sa_ref, sb_ref, bf_ref, e_ref, ind_ref,
             h_ref, qa_ref, ka_ref, va_ref, za_ref, qb_ref, kb_ref, vb_ref, zb_ref, ga_ref, gb_ref, f_ref,
             bnd_ref, nrm_ref, carry_ref):
        @pl.when(pl.program_id(0) == 0)
        def _():
            carry_ref[...] = jnp.zeros_like(carry_ref)

        xv = x_ref[...]
        rstd = lax.rsqrt(jnp.mean(xv * xv, axis=-1, keepdims=True) + NORM_EPS)
        h = (xv * rstd * mod_ref[0:1, :]) * (1.0 + mod_ref[1:2, :]) + mod_ref[2:3, :]
        hb = h.astype(BF16)
        h_ref[...] = hb.T

        def seg(name):
            off, w = SEG[name]
            return _dot(hb, w_ref[:, off:off + w])

        lane = _lane(CHUNK)
        low = lane < HEAD_DIM
        q_ones = _ones_lanes(64, 67)
        k_ones = _ones_lanes(67, 70)
        cos_t, sa_t, sb_t = cos_ref[...], sa_ref[...], sb_ref[...]

        def write_heads(ref, nat, extra_of, rope, scale):
            for p in range(N_HEADS // 2):
                c = nat[:, CHUNK * p:CHUNK * (p + 1)]
                if rope:
                    c = _rope(c, cos_t, sa_t, sb_t)
                if scale != 1.0:
                    c = c * scale
                for hh in range(2):
                    hd = 2 * p + hh
                    src = c if hh == 0 else pltpu.roll(c, HEAD_DIM, 1)
                    ref[:, CHUNK * hd:CHUNK * (hd + 1)] = jnp.where(low, src, extra_of(hd)).astype(BF16)

        write_heads(qa_ref, seg("qa"), lambda hd: q_ones, True, Q_SCALE)
        ka = seg("ka")
        va = seg("va")
        for kv in range(KV_GROUPS):
            sl = slice(CHUNK * kv, CHUNK * (kv + 1))
            ka_ref[:, sl] = jnp.where(low, _rope(ka[:, sl], cos_t, sa_t, sb_t), k_ones).astype(BF16)
            va_ref[:, sl] = jnp.where(low, va[:, sl], q_ones).astype(BF16)
        za_ref[...] = seg("za")
        zb_ref[...] = seg("zb")
        ga_ref[...] = seg("ga")
        gb_ref[...] = seg("gb")

        xf = seg("f") + bf_ref[...]
        f_ref[...] = xf
        logf = jnp.minimum(xf, 0.0) - jnp.log1p(jnp.exp(-jnp.abs(xf)))
        row = lax.broadcasted_iota(jnp.int32, (tm, tm), 0)
        col = lax.broadcasted_iota(jnp.int32, (tm, tm), 1)
        tri = jnp.where(col <= row, 1.0, 0.0).astype(BF16)
        hi, mid, lo = _split3(logf)
        cum = _dot(tri, hi.astype(BF16)) + _dot(tri, mid.astype(BF16)) + _dot(tri, lo.astype(BF16))
        cum = cum + carry_ref[0:1, :]
        carry_ref[0:1, :] += jnp.sum(logf, axis=0, keepdims=True)
        lane_all = _lane(AUG_W) & (CHUNK - 1)
        bias = _spread3(-cum, e_ref) + jnp.where((lane_all >= 67) & (lane_all < 70), 1.0, 0.0)

        qb, kb = seg("qb"), seg("kb")
        write_heads(qb_ref, qb, lambda hd: q_ones, False, Q_SCALE)
        write_heads(kb_ref, kb, lambda hd: bias[:, CHUNK * hd:CHUNK * (hd + 1)], False, 1.0)
        write_heads(vb_ref, seg("vb"), lambda hd: q_ones, False, 1.0)

        rid = lax.broadcasted_iota(jnp.int32, (tm, CHUNK), 0)
        bnd_ref[...] = jnp.zeros_like(bnd_ref)
        bnd_ref[0, 0:1, :] = jnp.sum(jnp.where(rid == 0, cum, 0.0), axis=0, keepdims=True)
        bnd_ref[0, 1:2, :] = jnp.sum(jnp.where(rid == tm - 1, cum, 0.0), axis=0, keepdims=True)

        @pl.when(pl.program_id(0) == 0)
        def _():
            nrm_ref[...] = jnp.zeros_like(nrm_ref)

        for r_, nat in ((0, qb * Q_SCALE), (1, kb)):
            sq = _dot((nat * nat).astype(BF16), ind_ref[...])
            nrm_ref[r_:r_ + 1, :] = jnp.maximum(nrm_ref[r_:r_ + 1, :], jnp.max(sq, axis=0, keepdims=True))

    row_blk = lambda w: pl.BlockSpec((tm, w), lambda i: (i, 0))
    full = lambda a: pl.BlockSpec(a.shape, lambda i: (0,) * a.ndim)
    sds = lambda w, dt: jax.ShapeDtypeStruct((s, w), dt)
    out_shape = [sds(AUG_W, BF16), sds(KV_GROUPS * CHUNK, BF16), sds(KV_GROUPS * CHUNK, BF16),
                 sds(512, F32), sds(AUG_W, BF16), sds(AUG_W, BF16), sds(AUG_W, BF16), sds(512, F32),
                 sds(D_MODEL, F32), sds(D_MODEL, F32), sds(CHUNK, F32)]
    small = [jax.ShapeDtypeStruct((s // tm, 8, CHUNK), F32), jax.ShapeDtypeStruct((8, CHUNK), F32)]
    return pl.pallas_call(
        body, name="fwd_proj", grid=(s // tm,),
        out_shape=[jax.ShapeDtypeStruct((D_MODEL, s), BF16)] + out_shape + small,
        in_specs=[row_blk(D_MODEL), full(mod), full(w_all), row_blk(CHUNK), row_blk(CHUNK), row_blk(CHUNK),
                  full(bf_row), full(emat), full(ind)],
        out_specs=[pl.BlockSpec((D_MODEL, tm), lambda i: (0, i))] + [row_blk(o.shape[1]) for o in out_shape] + [
            pl.BlockSpec((1, 8, CHUNK), lambda i: (i, 0, 0)), pl.BlockSpec((8, CHUNK), lambda i: (0, 0))],
        scratch_shapes=[pltpu.VMEM((8, CHUNK), F32)],
        compiler_params=_params(1))(x, mod, w_all, cos, sa, sb, bf_row, emat, ind)


def _swa_fwd(q_aug, k_aug, v_aug, sink_rows, tq):
    s = q_aug.shape[0]
    r = tq // WINDOW
    gw = GROUP * CHUNK

    def body(q_ref, kc_ref, kp_ref, vc_ref, vp_ref, sink_ref, o_ref, qb_ref):
        i = pl.program_id(1)
        st = GROUP * WINDOW
        qloc = lax.broadcasted_iota(jnp.int32, (st, 2 * WINDOW), 0) & (WINDOW - 1)
        col = lax.broadcasted_iota(jnp.int32, (st, 2 * WINDOW), 1)
        band = (col > qloc) & (col <= qloc + WINDOW)
        head = jnp.right_shift(lax.broadcasted_iota(jnp.int32, (st, 1), 0), 7)
        sink = jnp.zeros((st, 1), F32)
        for g in range(GROUP):
            sink = jnp.where(head == g, jnp.max(sink_ref[0, g:g + 1, :], axis=1, keepdims=True), sink)
        lane = _lane(CHUNK)
        for sub in range(r):
            rows = slice(WINDOW * sub, WINDOW * (sub + 1))
            q = jnp.concatenate([q_ref[rows, CHUNK * g:CHUNK * (g + 1)] for g in range(GROUP)], axis=0)
            if sub == 0:
                k = jnp.concatenate([kp_ref[...], kc_ref[rows, :]], axis=0)
                v = jnp.concatenate([vp_ref[...], vc_ref[rows, :]], axis=0)
                valid = band & ((col >= WINDOW) | (i > 0))
            else:
                both = slice(WINDOW * (sub - 1), WINDOW * (sub + 1))
                k, v, valid = kc_ref[both, :], vc_ref[both, :], band
            sc = jnp.where(valid, _dot_nt(q, k), NEG)
            m = jnp.maximum(jnp.max(sc, axis=1, keepdims=True), sink)
            acc = _dot(jnp.exp(sc - m).astype(BF16), v)
            denom = _lane_sum(acc, 64, 65) + jnp.exp(sink - m)
            out = acc / denom
            aug = _place3(lane, 67, -(m + jnp.log(denom)), q.astype(F32)).astype(BF16)
            hrows = lambda a, g: a[WINDOW * g:WINDOW * (g + 1), :]
            for g in range(GROUP):
                qb_ref[rows, CHUNK * g:CHUNK * (g + 1)] = hrows(aug, g)
            for pp in range(GROUP // 2):
                o_ref[rows, CHUNK * pp:CHUNK * (pp + 1)] = _pair(hrows(out, 2 * pp), hrows(out, 2 * pp + 1))

    return pl.pallas_call(
        body, name="swa_fwd", grid=(KV_GROUPS, s // tq),
        out_shape=[jax.ShapeDtypeStruct((s, 512), F32), jax.ShapeDtypeStruct((s, AUG_W), BF16)],
        in_specs=[pl.BlockSpec((tq, gw), lambda kv, i: (i, kv)),
                  pl.BlockSpec((tq, CHUNK), lambda kv, i: (i, kv)),
                  pl.BlockSpec((WINDOW, CHUNK), lambda kv, i: (jnp.maximum(i * r - 1, 0), kv)),
                  pl.BlockSpec((tq, CHUNK), lambda kv, i: (i, kv)),
                  pl.BlockSpec((WINDOW, CHUNK), lambda kv, i: (jnp.maximum(i * r - 1, 0), kv)),
                  pl.BlockSpec((1, 8, CHUNK), lambda kv, i: (kv, 0, 0))],
        out_specs=[pl.BlockSpec((tq, GROUP * HEAD_DIM), lambda kv, i: (i, kv)),
                   pl.BlockSpec((tq, gw), lambda kv, i: (i, kv))],
        compiler_params=_params(2))(q_aug, k_aug, k_aug, v_aug, v_aug, sink_rows)


def _swa_bwd(k_aug, v_aug, q_bwd, do_aug, cos, sa, sb, sink_rows, tk):
    s = k_aug.shape[0]
    r = tk // WINDOW
    nt = s // tk
    nb = s // WINDOW
    gw = GROUP * CHUNK

    def body(k_ref, v_ref, q_ref, qn_ref, do_ref, don_ref, cos_ref, sa_ref, sb_ref, sink_ref,
             da_ref, dsink_ref, carry_ref, acc_ref):
        j = pl.program_id(0)

        @pl.when(j == 0)
        def _():
            carry_ref[...] = jnp.zeros_like(carry_ref)
            dsink_ref[...] = jnp.zeros_like(dsink_ref)

        acc_ref[...] = jnp.zeros_like(acc_ref)
        acc_ref[0:WINDOW, :] = carry_ref[...]
        dk_col, dv_col = SEG["ka"][0] - SEG["qa"][0], SEG["va"][0] - SEG["qa"][0]
        st = GROUP * WINDOW
        key = lax.broadcasted_iota(jnp.int32, (WINDOW, 2 * st), 0)
        col = lax.broadcasted_iota(jnp.int32, (WINDOW, 2 * st), 1)
        qloc = col & (WINDOW - 1)
        band = ((col < st) & (key <= qloc)) | ((col >= st) & (key > qloc))
        lane = _lane(CHUNK)
        low = lane < HEAD_DIM
        cos_t, sa_t, sb_t = cos_ref[...], sa_ref[...], sb_ref[...]
        for kv, sub in [(kv, sub) for kv in range(KV_GROUPS) for sub in range(r)]:
            heads = [slice(CHUNK * (GROUP * kv + g), CHUNK * (GROUP * kv + g + 1)) for g in range(GROUP)]
            kvs = slice(CHUNK * kv, CHUNK * (kv + 1))
            rows = slice(WINDOW * sub, WINDOW * (sub + 1))
            nxt = slice(WINDOW * (sub + 1), WINDOW * (sub + 2))
            k, v = k_ref[rows, kvs], v_ref[rows, kvs]
            q_cur, do_cur = [q_ref[rows, cs] for cs in heads], [do_ref[rows, cs] for cs in heads]
            if sub < r - 1:
                q_nxt, do_nxt, valid = [q_ref[nxt, cs] for cs in heads], [do_ref[nxt, cs] for cs in heads], band
            else:
                q_nxt, do_nxt = [qn_ref[:, cs] for cs in heads], [don_ref[:, cs] for cs in heads]
                valid = band & ((col < st) | (j < nt - 1))
            q = jnp.concatenate(q_cur + q_nxt, axis=0)
            do = jnp.concatenate(do_cur + do_nxt, axis=0)
            pt = jnp.exp(jnp.where(valid, _dot_nt(k, q), NEG))
            ds = (pt * _dot_nt(v, do)).astype(BF16)
            dv = _dot(pt.astype(BF16), do)
            dk = _dot(ds, q)
            dq = _dot_tn(ds, k)
            for g, cs in enumerate(heads):
                acc_ref[rows, cs] += dq[WINDOW * g:WINDOW * (g + 1), :]
                dqn = dq[st + WINDOW * g:st + WINDOW * (g + 1), :]
                if sub < r - 1:
                    acc_ref[nxt, cs] += dqn
                else:
                    carry_ref[:, cs] = dqn
                sink = jnp.max(sink_ref[kv, g:g + 1, :], axis=1, keepdims=True)
                p_sink = jnp.exp(sink + _lane_sum(q_cur[g].astype(F32), 67, 70))
                term = jnp.sum(p_sink * _lane_sum(do_cur[g].astype(F32), 64, 67), axis=0, keepdims=True)
                dsink_ref[kv, g:g + 1, :] += jnp.broadcast_to(term, (1, CHUNK))
            da_ref[rows, dk_col + CHUNK * kv:dk_col + CHUNK * (kv + 1)] = _rope_inv(
                jnp.where(low, dk, 0.0), cos_t[rows, :], sa_t[rows, :], sb_t[rows, :]).astype(BF16)
            da_ref[rows, dv_col + CHUNK * kv:dv_col + CHUNK * (kv + 1)] = jnp.where(low, dv, 0.0).astype(BF16)
        for pp in range(N_HEADS // 2):
            d = _pair(acc_ref[:, CHUNK * 2 * pp:CHUNK * (2 * pp + 1)], acc_ref[:, CHUNK * (2 * pp + 1):CHUNK * (2 * pp + 2)])
            da_ref[:, CHUNK * pp:CHUNK * (pp + 1)] = (_rope_inv(d, cos_t, sa_t, sb_t) * Q_SCALE).astype(BF16)

    cur = lambda w: pl.BlockSpec((tk, w), lambda j: (j, 0))
    nxt = pl.BlockSpec((WINDOW, AUG_W), lambda j: (jnp.minimum((j + 1) * r, nb - 1), 0))
    whole = pl.BlockSpec((KV_GROUPS, 8, CHUNK), lambda j: (0, 0, 0))
    return pl.pallas_call(
        body, name="swa_bwd", grid=(nt,),
        out_shape=[jax.ShapeDtypeStruct((s, 1024), BF16), jax.ShapeDtypeStruct((KV_GROUPS, 8, CHUNK), F32)],
        in_specs=[cur(KV_GROUPS * CHUNK), cur(KV_GROUPS * CHUNK), cur(AUG_W), nxt, cur(AUG_W), nxt,
                  cur(CHUNK), cur(CHUNK), cur(CHUNK), whole],
        out_specs=[cur(1024), whole],
        scratch_shapes=[pltpu.VMEM((WINDOW, AUG_W), F32), pltpu.VMEM((tk, AUG_W), F32)],
        compiler_params=_params(1))(k_aug, v_aug, q_bwd, q_bwd, do_aug, do_aug, cos, sa, sb, sink_rows)


def _fox_fwd(first_key, q_aug, k_aug, v_aug, t):
    s = q_aug.shape[0]
    pw = 2 * CHUNK

    def body(lo_ref, q_ref, k_ref, v_ref, o_ref, qb_ref):
        i = pl.program_id(1)
        first = lo_ref[pl.program_id(0), i]
        row = lax.broadcasted_iota(jnp.int32, (t, t), 0)
        col = lax.broadcasted_iota(jnp.int32, (t, t), 1)
        lane = _lane(CHUNK)
        heads = (slice(0, CHUNK), slice(CHUNK, pw))
        qs = [q_ref[:, hs] for hs in heads]

        def step(jb, carry, masked):
            rows = pl.ds(pl.multiple_of(jb * t, t), t)
            new = []
            for hh, hs in enumerate(heads):
                m, acc = carry[2 * hh], carry[2 * hh + 1]
                sc = _dot_nt(qs[hh], k_ref[rows, hs])
                if masked:
                    sc = jnp.where(col <= row, sc, NEG)
                m_new = jnp.maximum(m, jnp.max(sc, axis=1, keepdims=True))
                p = jnp.exp(sc - m_new).astype(BF16)
                new += [m_new, jnp.exp(m - m_new) * acc + _dot(p, v_ref[rows, hs])]
            return tuple(new)

        init = (jnp.full((t, 1), NEG, F32), jnp.zeros((t, CHUNK), F32)) * 2
        carry = step(i, lax.fori_loop(first, i, lambda jb, c: step(jb, c, False), init), True)
        outs = []
        for hh, hs in enumerate(heads):
            m, acc = carry[2 * hh], carry[2 * hh + 1]
            denom = _lane_sum(acc, 64, 65)
            outs.append(acc / denom)
            lse = m + jnp.log(denom)
            qb_ref[:, hs] = _place3(lane, 67, -lse, qs[hh].astype(F32)).astype(BF16)
        o_ref[...] = _pair(outs[0], outs[1])

    return pl.pallas_call(
        body, name="fox_fwd", grid=(N_HEADS // 2, s // t),
        out_shape=[jax.ShapeDtypeStruct((s, 512), F32), jax.ShapeDtypeStruct((s, AUG_W), BF16)],
        in_specs=[pl.BlockSpec(memory_space=pltpu.SMEM),
                  pl.BlockSpec((t, pw), lambda p, i: (i, p)),
                  pl.BlockSpec((s, pw), lambda p, i: (0, p)),
                  pl.BlockSpec((s, pw), lambda p, i: (0, p))],
        out_specs=[pl.BlockSpec((t, CHUNK), lambda p, i: (i, p)), pl.BlockSpec((t, pw), lambda p, i: (i, p))],
        compiler_params=_params(2))(first_key, q_aug, k_aug, v_aug)


def _fox_block_ranges(bounds, norms, r):
    cum_first = bounds[0::r, 0, :N_HEADS]
    cum_last = bounds[r - 1::r, 1, :N_HEADS]
    n = cum_first.shape[0]
    reach = 2.0 * 1.02 * jnp.sqrt(norms[0, :N_HEADS] * norms[1, :N_HEADS]) + 105.0
    blk = jnp.arange(n, dtype=jnp.int32)
    decay = cum_last[None, :, :] - cum_first[:, None, :]
    skip = (decay > reach[None, None, :]) & (blk[None, :, None] < blk[:, None, None])
    first = jnp.min(jnp.where(skip, n, blk[None, :, None]), axis=1)
    first = jnp.minimum(first[:, 0::2], first[:, 1::2]).T
    needed = (first[:, :, None] <= blk[None, None, :]) & (blk[None, :, None] >= blk[None, None, :])
    last = jnp.max(jnp.where(needed, blk[None, :, None], blk[None, None, :]), axis=1)
    return first.astype(jnp.int32), last.astype(jnp.int32)


def _fox_bwd(last_query, k_aug, v_aug, q_bwd, do_aug, t):
    s = k_aug.shape[0]
    n = s // t
    pw = 2 * CHUNK

    def body(hi_ref, k_ref, v_ref, q_ref, do_ref, dk_ref, dv_ref, dck_ref, dq_ref, dcq_ref, dq_scr):
        j = pl.program_id(1)
        last = hi_ref[pl.program_id(0), j]

        @pl.when(j == 0)
        def _():
            dq_scr[...] = jnp.zeros_like(dq_scr)

        row = lax.broadcasted_iota(jnp.int32, (t, t), 0)
        col = lax.broadcasted_iota(jnp.int32, (t, t), 1)
        lane = _lane(CHUNK)
        heads = (slice(0, CHUNK), slice(CHUNK, pw))
        ks = [k_ref[:, hs] for hs in heads]
        vs = [v_ref[:, hs] for hs in heads]

        def step(ib, carry, masked):
            rows = pl.ds(pl.multiple_of(ib * t, t), t)
            new = []
            for hh, hs in enumerate(heads):
                dv, dk = carry[2 * hh], carry[2 * hh + 1]
                q, do = q_ref[rows, hs], do_ref[rows, hs]
                st = _dot_nt(ks[hh], q)
                if masked:
                    st = jnp.where(row <= col, st, NEG)
                pt = jnp.exp(st)
                ds = (pt * _dot_nt(vs[hh], do)).astype(BF16)
                new += [dv + _dot(pt.astype(BF16), do), dk + _dot(ds, q)]
                dq_scr[rows, hs] += _dot_tn(ds, ks[hh])
            return tuple(new)

        zero = jnp.zeros((t, CHUNK), F32)
        carry = lax.fori_loop(j + 1, last + 1, lambda ib, c: step(ib, c, False), step(j, (zero,) * 4, True))
        dvs, dks = (carry[0], carry[2]), (carry[1], carry[3])
        dk_ref[...] = _pair(dks[0], dks[1]).astype(BF16)
        dv_ref[...] = _pair(dvs[0], dvs[1]).astype(BF16)
        dck_ref[0] = jnp.where(lane == 0, pltpu.roll(dks[0], 64, 1),
                               jnp.where(lane == 1, pltpu.roll(dks[1], 65, 1), 0.0))

        @pl.when(j == n - 1)
        def _():
            for ib in range(n):
                rows = slice(t * ib, t * (ib + 1))
                d0, d1 = dq_scr[rows, 0:CHUNK], dq_scr[rows, CHUNK:pw]
                dq_ref[rows, :] = (_pair(d0, d1) * Q_SCALE).astype(BF16)
                dcq_ref[0, rows, :] = jnp.where(lane == 0, pltpu.roll(d0, CHUNK - 67, 1),
                                                jnp.where(lane == 1, pltpu.roll(d1, CHUNK - 66, 1), 0.0))

    return pl.pallas_call(
        body, name="fox_bwd", grid=(N_HEADS // 2, n),
        out_shape=[jax.ShapeDtypeStruct((s, 512), BF16), jax.ShapeDtypeStruct((s, 512), BF16),
                   jax.ShapeDtypeStruct((N_HEADS // 2, s, CHUNK), F32), jax.ShapeDtypeStruct((s, 512), BF16),
                   jax.ShapeDtypeStruct((N_HEADS // 2, s, CHUNK), F32)],
        in_specs=[pl.BlockSpec(memory_space=pltpu.SMEM),
                  pl.BlockSpec((t, pw), lambda p, j: (j, p)), pl.BlockSpec((t, pw), lambda p, j: (j, p)),
                  pl.BlockSpec((s, pw), lambda p, j: (0, p)), pl.BlockSpec((s, pw), lambda p, j: (0, p))],
        out_specs=[pl.BlockSpec((t, CHUNK), lambda p, j: (j, p)), pl.BlockSpec((t, CHUNK), lambda p, j: (j, p)),
                   pl.BlockSpec((1, t, CHUNK), lambda p, j: (p, j, 0)), pl.BlockSpec((s, CHUNK), lambda p, j: (0, p)),
                   pl.BlockSpec((1, s, CHUNK), lambda p, j: (p, 0, 0))],
        scratch_shapes=[pltpu.VMEM((s, pw), F32)],
        compiler_params=_params(2))(last_query, k_aug, v_aug, q_bwd, do_aug)


def _fgate_bwd(dcq, dck, xf, tm):
    s = xf.shape[0]
    nt = s // tm

    def body(dq_ref, dc_ref, xf_ref, df_ref, dbf_ref, carry_ref):
        @pl.when(pl.program_id(0) == 0)
        def _():
            carry_ref[...] = jnp.zeros_like(carry_ref)
            dbf_ref[...] = jnp.zeros_like(dbf_ref)

        row = lax.broadcasted_iota(jnp.int32, (tm, tm), 0)
        col = lax.broadcasted_iota(jnp.int32, (tm, tm), 1)
        tri = jnp.where(col >= row, 1.0, 0.0).astype(BF16)
        dcum = dq_ref[0] - dc_ref[0]
        for p in range(1, N_HEADS // 2):
            dcum = dcum + pltpu.roll(dq_ref[p] - dc_ref[p], 2 * p, 1)
        hi, mid, lo = _split3(dcum)
        dlogf = _dot(tri, hi.astype(BF16)) + _dot(tri, mid.astype(BF16)) + _dot(tri, lo.astype(BF16))
        dlogf = dlogf + carry_ref[0:1, :]
        carry_ref[0:1, :] += jnp.sum(dcum, axis=0, keepdims=True)
        df = dlogf * _sigmoid(-xf_ref[...])
        df_ref[...] = df.astype(BF16)
        dbf_ref[0:1, :] += jnp.sum(df, axis=0, keepdims=True)

    rev = pl.BlockSpec((tm, CHUNK), lambda i: (nt - 1 - i, 0))
    rev4 = pl.BlockSpec((N_HEADS // 2, tm, CHUNK), lambda i: (0, nt - 1 - i, 0))
    return pl.pallas_call(
        body, name="fgate_bwd", grid=(nt,),
        out_shape=[jax.ShapeDtypeStruct((s, CHUNK), BF16), jax.ShapeDtypeStruct((8, CHUNK), F32)],
        in_specs=[rev4, rev4, rev], out_specs=[rev, pl.BlockSpec((8, CHUNK), lambda i: (0, 0))],
        scratch_shapes=[pltpu.VMEM((8, CHUNK), F32)],
        compiler_params=_params(1))(dcq, dck, xf)


def _post(x, tgt, att_a, att_b, za, zb, ga, gb, woa_t, wob_t, wout, vec, ind, emat, tm):
    s = x.shape[0]

    def body(x_ref, t_ref, aa_ref, ab_ref, za_ref, zb_ref, ga_ref, gb_ref, woat_ref, wobt_ref, wout_ref,
             vec_ref, ind_ref, e_ref,
             dx1_ref, ua_ref, ub_ref, mg_ref, do_ref, dya_ref, dyb_ref, doa_ref, dob_ref, dzg_ref, acc_ref):
        @pl.when(pl.program_id(0) == 0)
        def _():
            acc_ref[...] = jnp.zeros_like(acc_ref)

        gate, gfin = vec_ref[0:1, :], vec_ref[1:2, :]
        inv_d = 1.0 / D_MODEL

        def branch_fwd(att_ref, z_ref, w_ref, u_ref):
            att, z = att_ref[...], z_ref[...]
            sz = _sigmoid(z)
            silu = z * sz
            u = (att * silu).astype(BF16)
            u_ref[...] = u.T
            return att, z, sz, silu, _dot_nt(u, w_ref[...])

        att_a, z_a, sz_a, silu_a, y_a = branch_fwd(aa_ref, za_ref, woat_ref, ua_ref)
        att_b, z_b, sz_b, silu_b, y_b = branch_fwd(ab_ref, zb_ref, wobt_ref, ub_ref)
        sg_a, sg_b = _sigmoid(ga_ref[...]), _sigmoid(gb_ref[...])
        merged = (sg_a * y_a + sg_b * y_b).astype(BF16)
        mg_ref[...] = merged.T
        o = _dot(merged, wout_ref[...])
        x1 = x_ref[...] + gate * o
        rstd = lax.rsqrt(jnp.mean(x1 * x1, axis=-1, keepdims=True) + NORM_EPS)
        xh = x1 * rstd
        diff = xh * gfin - t_ref[...]
        acc_ref[2:3, :] += (0.5 * inv_d) * jnp.sum(diff * diff, axis=0, keepdims=True)
        dy = diff * inv_d
        acc_ref[1:2, :] += jnp.sum(dy * xh, axis=0, keepdims=True)
        dyg = dy * gfin
        dx1 = rstd * (dyg - xh * jnp.mean(dyg * xh, axis=-1, keepdims=True))
        dx1_ref[...] = dx1
        acc_ref[0:1, :] += jnp.sum(dx1 * o, axis=0, keepdims=True)
        d_o = (dx1 * gate).astype(BF16)
        do_ref[...] = d_o
        dmg = _dot_nt(d_o, wout_ref[...])

        lane = _lane(CHUNK)
        low = lane < HEAD_DIM
        lane_all = _lane(AUG_W) & (CHUNK - 1)

        zg0 = SEG["za"][0]

        def branch_bwd(sg, y, att, z, sz, silu, wt_ref, dy_ref, g_name, z_name, doaug_ref):
            dyb = (dmg * sg).astype(BF16)
            dy_ref[...] = dyb
            g_off, z_off = SEG[g_name][0] - zg0, SEG[z_name][0] - zg0
            dzg_ref[:, g_off:g_off + D_MODEL] = (dmg * y * sg * (1.0 - sg)).astype(BF16)
            du = _dot(dyb, wt_ref[...])
            datt = du * silu
            dzg_ref[:, z_off:z_off + 512] = (du * att * (sz * (1.0 + z * (1.0 - sz)))).astype(BF16)
            extra = _spread3(-_dot_split(datt * att, ind_ref[...]), e_ref)
            for p in range(N_HEADS // 2):
                c = datt[:, CHUNK * p:CHUNK * (p + 1)]
                for hh in range(2):
                    hd = 2 * p + hh
                    src = c if hh == 0 else pltpu.roll(c, HEAD_DIM, 1)
                    doaug_ref[:, CHUNK * hd:CHUNK * (hd + 1)] = jnp.where(
                        low, src, extra[:, CHUNK * hd:CHUNK * (hd + 1)]).astype(BF16)

        branch_bwd(sg_a, y_a, att_a, z_a, sz_a, silu_a, woat_ref, dya_ref, "ga", "za", doa_ref)
        branch_bwd(sg_b, y_b, att_b, z_b, sz_b, silu_b, wobt_ref, dyb_ref, "gb", "zb", dob_ref)

    row_blk = lambda w: pl.BlockSpec((tm, w), lambda i: (i, 0))
    full = lambda a: pl.BlockSpec(a.shape, lambda i: (0,) * a.ndim)
    sds = lambda w, dt: jax.ShapeDtypeStruct((s, w), dt)
    tds = lambda w: jax.ShapeDtypeStruct((w, s), BF16)
    col_blk = lambda w: pl.BlockSpec((w, tm), lambda i: (0, i))
    out_shape = [sds(D_MODEL, F32), tds(512), tds(512), tds(D_MODEL), sds(D_MODEL, BF16),
                 sds(D_MODEL, BF16), sds(D_MODEL, BF16), sds(AUG_W, BF16), sds(AUG_W, BF16), sds(3072, BF16),
                 jax.ShapeDtypeStruct((8, D_MODEL), F32)]
    ins = [x, tgt, att_a, att_b, za, zb, ga, gb, woa_t, wob_t, wout, vec, ind, emat]
    in_specs = [row_blk(a.shape[1]) for a in ins[:8]] + [full(a) for a in ins[8:]]
    out_specs = ([row_blk(D_MODEL), col_blk(512), col_blk(512), col_blk(D_MODEL)]
                 + [row_blk(o.shape[1]) for o in out_shape[4:-1]] + [pl.BlockSpec((8, D_MODEL), lambda i: (0, 0))])
    return pl.pallas_call(body, name="post", grid=(s // tm,), out_shape=out_shape, in_specs=in_specs,
                          out_specs=out_specs, compiler_params=_params(1))(*ins)


def _bwd_pre(dproj, w_all, x, dx1, mod, tm):
    s = x.shape[0]
    ng = len(DPROJ_GROUPS)

    def body(*refs):
        dp_refs = refs[:ng]
        wt_ref, x_ref, dx1_ref, mod_ref, gx_ref, acc_ref = refs[ng:]

        @pl.when(pl.program_id(0) == 0)
        def _():
            acc_ref[...] = jnp.zeros_like(acc_ref)

        dh = None
        for dp_ref, (_, off, w) in zip(dp_refs, DPROJ_GROUPS):
            part = _dot_nt(dp_ref[...], wt_ref[:, off:off + w])
            dh = part if dh is None else dh + part
        xv = x_ref[...]
        rstd = lax.rsqrt(jnp.mean(xv * xv, axis=-1, keepdims=True) + NORM_EPS)
        xh = xv * rstd
        gn = mod_ref[0:1, :]
        acc_ref[0:1, :] += jnp.sum(dh, axis=0, keepdims=True)
        acc_ref[1:2, :] += jnp.sum(dh * (xh * gn), axis=0, keepdims=True)
        dn = dh * (1.0 + mod_ref[1:2, :])
        acc_ref[2:3, :] += jnp.sum(dn * xh, axis=0, keepdims=True)
        dxh = dn * gn
        gx_ref[...] = dx1_ref[...] + rstd * (dxh - xh * jnp.mean(dxh * xh, axis=-1, keepdims=True))

    row_blk = lambda w: pl.BlockSpec((tm, w), lambda i: (i, 0))
    full = lambda a: pl.BlockSpec(a.shape, lambda i: (0,) * a.ndim)
    return pl.pallas_call(
        body, name="bwd_pre", grid=(s // tm,),
        out_shape=[jax.ShapeDtypeStruct((s, D_MODEL), F32), jax.ShapeDtypeStruct((8, D_MODEL), F32)],
        in_specs=[row_blk(w) for _, _, w in DPROJ_GROUPS] + [full(w_all), row_blk(D_MODEL), row_blk(D_MODEL), full(mod)],
        out_specs=[row_blk(D_MODEL), pl.BlockSpec((8, D_MODEL), lambda i: (0, 0))],
        compiler_params=_params(1))(*dproj, w_all, x, dx1, mod)


def _token_matmul(a_t, b, tn, tk, name):
    m, s = a_t.shape
    n = b.shape[1]

    def body(a_ref, b_ref, o_ref):
        @pl.when(pl.program_id(1) == 0)
        def _():
            o_ref[...] = jnp.zeros_like(o_ref)

        o_ref[...] += _dot(a_ref[...], b_ref[...])

    return pl.pallas_call(
        body, name=name, grid=(n // tn, s // tk), out_shape=jax.ShapeDtypeStruct((m, n), F32),
        in_specs=[pl.BlockSpec((m, tk), lambda jn, k: (0, k)), pl.BlockSpec((tk, tn), lambda jn, k: (k, jn))],
        out_specs=pl.BlockSpec((m, tn), lambda jn, k: (0, jn)),
        compiler_params=_params(2))(a_t, b)


def _token_matmul_multi(a_t, bs, tk, name):
    m, s = a_t.shape
    nb = len(bs)

    def body(*refs):
        a_ref, b_refs, o_refs = refs[0], refs[1:1 + nb], refs[1 + nb:]
        a = a_ref[...]
        for b_ref, o_ref in zip(b_refs, o_refs):
            @pl.when(pl.program_id(0) == 0)
            def _():
                o_ref[...] = jnp.zeros_like(o_ref)

            o_ref[...] += _dot(a, b_ref[...])

    return pl.pallas_call(
        body, name=name, grid=(s // tk,),
        out_shape=[jax.ShapeDtypeStruct((m, b.shape[1]), F32) for b in bs],
        in_specs=[pl.BlockSpec((m, tk), lambda k: (0, k))] + [pl.BlockSpec((tk, b.shape[1]), lambda k: (k, 0)) for b in bs],
        out_specs=[pl.BlockSpec((m, b.shape[1]), lambda k: (0, 0)) for b in bs],
        compiler_params=_params(1))(a_t, *bs)


def _adam_math(g, w, m, v):
    m2 = ADAM_B1 * m + (1.0 - ADAM_B1) * g
    v2 = ADAM_B2 * v + (1.0 - ADAM_B2) * (g * g)
    delta = -ADAM_LR * ((m2 / ADAM_C1) / (jnp.sqrt(v2 / ADAM_C2) + ADAM_EPS) + ADAM_WD * w)
    return delta, m2, v2


def _adamw_parts(full, parts, others, w, m, v, tr, name):
    _, rws, cols = w.shape

    def body(oth_ref, f_ref, p_ref, w_ref, m_ref, v_ref, g_ref, d_ref, m2_ref, v2_ref):
        g = None
        for chip in range(N_CHIPS):
            part = (1.0 - oth_ref[chip]) * f_ref[chip] + oth_ref[chip] * p_ref[chip].astype(F32)
            g = part if g is None else g + part
        g_ref[0] = g
        d_ref[0], m2_ref[0], v2_ref[0] = _adam_math(g, w_ref[0], m_ref[0], v_ref[0])

    blk = pl.BlockSpec((1, tr, cols), lambda i: (0, i, 0))
    slots = pl.BlockSpec((N_CHIPS, tr, cols), lambda i: (0, i, 0))
    o = jax.ShapeDtypeStruct((1, rws, cols), F32)
    return pl.pallas_call(
        body, name=name, grid=(rws // tr,), out_shape=[o, o, o, o],
        in_specs=[pl.BlockSpec(memory_space=pltpu.SMEM), slots, slots, blk, blk, blk],
        out_specs=[blk, blk, blk, blk], compiler_params=_params(1))(others, full, parts, w, m, v)


def _adamw_ada(c_lanes, d_rows, w, m, v, tr):
    _, rws, cols = w.shape

    def body(c_ref, d_ref, w_ref, m_ref, v_ref, g_ref, dl_ref, m2_ref, v2_ref):
        for k in range(cols // CHUNK):
            cs = slice(CHUNK * k, CHUNK * (k + 1))
            g = c_ref[0] * d_ref[0:1, cs]
            for b in range(1, N_DEV):
                g = g + c_ref[b] * d_ref[b:b + 1, cs]
            g_ref[0, :, cs] = g
            dl_ref[0, :, cs], m2_ref[0, :, cs], v2_ref[0, :, cs] = _adam_math(
                g, w_ref[0, :, cs], m_ref[0, :, cs], v_ref[0, :, cs])

    blk = pl.BlockSpec((1, tr, cols), lambda i: (0, i, 0))
    o = jax.ShapeDtypeStruct((1, rws, cols), F32)
    return pl.pallas_call(
        body, name="adamw_ada", grid=(rws // tr,), out_shape=[o, o, o, o],
        in_specs=[pl.BlockSpec((N_DEV, tr, CHUNK), lambda i: (0, i, 0)), pl.BlockSpec((N_DEV, cols), lambda i: (0, 0)),
                  blk, blk, blk],
        out_specs=[blk, blk, blk, blk], compiler_params=_params(1))(c_lanes, d_rows, w, m, v)


def _adamw_small(packs, w, m, v):
    def body(p_ref, w_ref, m_ref, v_ref, g_ref, d_ref, m2_ref, v2_ref, loss_ref):
        g = p_ref[0]
        for dev in range(1, N_DEV):
            g = g + p_ref[dev]
        g_ref[...] = g
        d_ref[...], m2_ref[...], v2_ref[...] = _adam_math(g, w_ref[...], m_ref[...], v_ref[...])
        loss_ref[...] = jnp.broadcast_to(jnp.sum(g_ref[2:3, 0:D_MODEL], axis=1, keepdims=True), loss_ref.shape)

    o = jax.ShapeDtypeStruct(w.shape, F32)
    return pl.pallas_call(body, name="adamw_small", out_shape=[o, o, o, o, jax.ShapeDtypeStruct((8, CHUNK), F32)],
                          compiler_params=_params(0))(packs, w, m, v)


def _tile(s, want):
    return min(s, want)


def _layout_pieces():
    orig = {"qa": 0, "ka": 512, "va": 640, "za": 768, "qb": 1280, "kb": 1792, "vb": 2304, "f": 2816, "zb": 2824,
            "ga": 3336, "gb": 4360}
    pieces = []
    for name, (off, w) in SEG.items():
        if name in ("ka", "va"):
            pieces += [(orig[name] + HEAD_DIM * kv, orig[name] + HEAD_DIM * (kv + 1), off + CHUNK * kv) for kv in range(KV_GROUPS)]
        else:
            pieces.append((orig[name], orig[name] + (N_HEADS if name == "f" else w), off))
    return pieces


def _assemble_w_all(win_g):
    cols, pos = [], 0
    for lo, hi, new in sorted(_layout_pieces(), key=lambda t: t[2]):
        if new > pos:
            cols.append(jnp.zeros((D_MODEL, new - pos), win_g.dtype))
        col = lo
        while col < hi:
            dev = col // IN_SHARD
            end = min(hi, (dev + 1) * IN_SHARD)
            cols.append(win_g[dev, :, col - dev * IN_SHARD:end - dev * IN_SHARD])
            col = end
        pos = new + hi - lo
    cols.append(jnp.zeros((D_MODEL, PROJ_W - pos), win_g.dtype))
    return jnp.concatenate(cols, axis=1)


def _grad_slot(dw_groups, dev):
    lo_d, hi_d = dev * IN_SHARD, (dev + 1) * IN_SHARD
    cols = []
    for lo, hi, new in sorted(_layout_pieces()):
        a, b = max(lo, lo_d), min(hi, hi_d)
        if a < b:
            arr, off = next((g, o) for g, (_, o, w) in zip(dw_groups, DPROJ_GROUPS) if o <= new < o + w)
            cols.append(arr[:, new - off + a - lo:new - off + b - lo])
    return jnp.concatenate(cols, axis=1)


def kernel(x, c, positions, w_ada, b_ada, g_norm, w_in, b_f, sinks, w_o_swa, w_o_fox, w_out, g_final, loss_target, m_w_ada, m_b_ada, m_g_norm, m_w_in, m_b_f, m_sinks, m_w_o_swa, m_w_o_fox, m_w_out, m_g_final, v_w_ada, v_b_ada, v_g_norm, v_w_in, v_b_f, v_sinks, v_w_o_swa, v_w_o_fox, v_w_out, v_g_final):
    s = x.shape[1]
    tm = _tile(s, 256)
    ta = _tile(s, 512)
    me = 4 * lax.axis_index("x") + 2 * lax.axis_index("y") + lax.axis_index("c")
    x2, tgt = x[0], loss_target[0]

    (c_all,) = _exchange([jnp.broadcast_to(c, (8, D_MODEL))], ["gather"], "gather_c")
    c_all = c_all[:, 0, :]
    b_shard = lax.dynamic_slice(b_ada, (0, me * ADA_SHARD), (1, ADA_SHARD))
    ada_part = _ada_fwd(c_all, w_ada[0], b_shard)
    ada_g, win_g, woa_g, wob_g, wout_g = _gather_two_level(
        [ada_part, w_in[0].astype(BF16), w_o_swa[0].T.astype(BF16), w_o_fox[0].T.astype(BF16), w_out[0].astype(BF16)],
        "gather_weights")
    ada = lax.dynamic_index_in_dim(ada_g, me, axis=1, keepdims=False).reshape(3 * D_MODEL)
    shift, scale, gate = ada[:D_MODEL], ada[D_MODEL:2 * D_MODEL], ada[2 * D_MODEL:]
    woa_t = woa_g.reshape(D_MODEL, 512)
    wob_t = wob_g.reshape(D_MODEL, 512)
    wout = wout_g.reshape(D_MODEL, D_MODEL)
    w_all = _assemble_w_all(win_g)

    zrow = jnp.zeros((1, D_MODEL), F32)
    mod = jnp.concatenate([g_norm, scale[None], shift[None], zrow, zrow, zrow, zrow, zrow], axis=0)
    bf_row = jnp.pad(b_f, ((0, 0), (0, CHUNK - N_HEADS)))
    emat_np = np.zeros((3, CHUNK, AUG_W), np.float32)
    ind_np = np.zeros((512, CHUNK), np.float32)
    for hd in range(N_HEADS):
        for part in range(3):
            emat_np[part, hd, CHUNK * hd + 64 + part] = 1.0
        ind_np[HEAD_DIM * hd:HEAD_DIM * (hd + 1), hd] = 1.0
    emat, ind = jnp.asarray(emat_np, BF16), jnp.asarray(ind_np, BF16)
    inv_freq = np.power(np.float32(ROPE_THETA), -np.arange(0, HEAD_DIM, 2, dtype=np.float32) / HEAD_DIM)
    inv_freq = jnp.asarray(np.tile(inv_freq, CHUNK // (HEAD_DIM // 2))[None, :], F32)
    cos, sa, sb = _rope_tables(positions.reshape(s, 1).astype(F32), inv_freq, tm)
    sink_rows = jnp.broadcast_to(jnp.pad(sinks.reshape(KV_GROUPS, GROUP), ((0, 0), (0, 8 - GROUP)))[:, :, None],
                                 (KV_GROUPS, 8, CHUNK))

    (h_t, qa, ka, va, za, qb, kb, vb, zb, ga, gb, xf, bounds, norms) = _fwd_proj(
        x2, mod, w_all, cos, sa, sb, bf_row, emat, ind, tm)
    first_key, last_query = _fox_block_ranges(bounds, norms, ta // tm)
    att_a, qa_bwd = _swa_fwd(qa, ka, va, sink_rows, ta)
    att_b, qb_bwd = _fox_fwd(first_key, qb, kb, vb, ta)
    vec = jnp.concatenate([gate[None], g_final[None], zrow, zrow, zrow, zrow, zrow, zrow], axis=0)
    (dx1, ua_t, ub_t, merged_t, d_o, dya, dyb, doa, dob, dzg, post_acc) = _post(
        x2, tgt, att_a, att_b, za, zb, ga, gb, woa_t, wob_t, wout, vec, ind, emat, tm)

    da, dsink = _swa_bwd(ka, va, qa_bwd, doa, cos, sa, sb, sink_rows, ta)
    dkb, dvb, dck, dqb, dcq = _fox_bwd(last_query, kb, vb, qb_bwd, dob, ta)
    dfb, dbf_acc = _fgate_bwd(dcq, dck, xf, tm)
    dproj = [da, dqb, dkb, dvb, dzg, dfb]
    grad_x, pre_acc = _bwd_pre(dproj, w_all, x2, dx1, mod, tm)
    dw_zg, = _token_matmul_multi(h_t, [dzg], ta, "dw_in_zg")
    dw_a, dw_qb, dw_kb, dw_vb, dw_f = _token_matmul_multi(h_t, [da, dqb, dkb, dvb, dfb], ta, "dw_in_rest")
    dw_all = [dw_a, dw_qb, dw_kb, dw_vb, dw_zg, dw_f]
    dwoa = _token_matmul(ua_t, dya, 1024, ta, "dw_o_swa")
    dwob = _token_matmul(ub_t, dyb, 1024, ta, "dw_o_fox")
    dwout = _token_matmul(merged_t, d_o, 1024, ta, "dw_out")

    misc = jnp.concatenate([dbf_acc[0, :N_HEADS], dsink[:, :GROUP, 0].reshape(N_HEADS), jnp.zeros((D_MODEL - 16,), F32)])
    zpad = jnp.zeros((2 * D_MODEL,), F32)
    pack = jnp.stack([jnp.concatenate([pre_acc[0], pre_acc[1], post_acc[0]]),
                      jnp.concatenate([pre_acc[2], post_acc[1], misc]),
                      jnp.concatenate([post_acc[2], zpad])] + [jnp.zeros((3 * D_MODEL,), F32)] * 5)
    core, chip = lax.axis_index("c"), 2 * lax.axis_index("x") + lax.axis_index("y")
    col_slots = lambda g, r, w: g.reshape(r, N_CHIPS, 2, w).transpose(2, 1, 0, 3)
    win_slots = jnp.stack([jnp.stack([_grad_slot(dw_all, 2 * ch + co) for ch in range(N_CHIPS)]) for co in range(2)])
    slots = [win_slots, col_slots(dwoa, 512, 128), col_slots(dwob, 512, 128),
             dwout.reshape(N_CHIPS, 2, 128, D_MODEL).transpose(1, 0, 2, 3)]
    theirs = _swap_sibling(slots, "grads_sibling")
    fulls, halves = [], []
    core_arr = jnp.reshape(core, (1,)).astype(jnp.int32)
    for sl, th, nm in zip(slots, theirs, ("w_in", "w_o_swa", "w_o_fox", "w_out")):
        full, half = _chip_partial(core_arr, sl, th, min(th.shape[1], 256), "chip_partial_" + nm)
        fulls.append(full)
        halves.append(half)
    p_win, p_woa, p_wob, p_wout, packs = _exchange(halves + [pack], ["chips"] * 4 + ["gather"], "exchange_grads")
    others = jnp.where(jnp.arange(N_CHIPS) == chip, 0.0, 1.0).astype(F32)

    g_win, d_win, m_win, v_win = _adamw_parts(fulls[0], p_win, others, w_in, m_w_in, v_w_in, 128, "adamw_w_in")
    g_woa, d_woa, m_woa, v_woa = _adamw_parts(fulls[1], p_woa, others, w_o_swa, m_w_o_swa, v_w_o_swa, 128, "adamw_w_o_swa")
    g_wob, d_wob, m_wob, v_wob = _adamw_parts(fulls[2], p_wob, others, w_o_fox, m_w_o_fox, v_w_o_fox, 128, "adamw_w_o_fox")
    g_wout, d_wout, m_wout, v_wout = _adamw_parts(fulls[3], p_wout, others, w_out, m_w_out, v_w_out, 128, "adamw_w_out")
    d_ada_rows = lax.dynamic_slice(packs[:, 0, :], (0, me * ADA_SHARD), (N_DEV, ADA_SHARD))
    c_lanes = jnp.broadcast_to(c_all[:, :, None], (N_DEV, D_MODEL, CHUNK))
    g_wada, d_wada, m_wada, v_wada = _adamw_ada(c_lanes, d_ada_rows, w_ada, m_w_ada, v_w_ada, 256)

    def small_pack(bada, gn, gf, bfv, sk):
        misc_w = jnp.concatenate([bfv[0], sk[0], jnp.zeros((D_MODEL - 16,), F32)])
        return jnp.stack([bada[0], jnp.concatenate([gn[0], gf, misc_w])] + [jnp.zeros((3 * D_MODEL,), F32)] * 6)

    sm = _adamw_small(packs, small_pack(b_ada, g_norm, g_final, b_f, sinks),
                      small_pack(m_b_ada, m_g_norm, m_g_final, m_b_f, m_sinks),
                      small_pack(v_b_ada, v_g_norm, v_g_final, v_b_f, v_sinks))
    loss = sm[4][0, 0]

    def unpack(p):
        return (p[0][None], p[1, :D_MODEL][None], p[1, 2 * D_MODEL:2 * D_MODEL + 8][None],
                p[1, 2 * D_MODEL + 8:2 * D_MODEL + 16][None], p[1, D_MODEL:2 * D_MODEL])

    outs = []
    for big, small in (((g_wada, g_win, g_woa, g_wob, g_wout), sm[0]), ((d_wada, d_win, d_woa, d_wob, d_wout), sm[1]),
                       ((m_wada, m_win, m_woa, m_wob, m_wout), sm[2]), ((v_wada, v_win, v_woa, v_wob, v_wout), sm[3])):
        wada_o, win_o, woa_o, wob_o, wout_o = big
        bada_o, gn_o, bf_o, sk_o, gf_o = unpack(small)
        outs += [wada_o, bada_o, gn_o, win_o, bf_o, sk_o, woa_o, wob_o, wout_o, gf_o]
    return (loss, grad_x[None], *outs)
```

```python
import numpy as np
import jax
import jax.numpy as jnp
from jax import lax
from jax.experimental import pallas as pl
from jax.experimental.pallas import tpu as pltpu

F32 = jnp.float32
BF16 = jnp.bfloat16

D_MODEL = 1024
HEAD_DIM = 64
N_HEADS = 8
KV_GROUPS = 2
GROUP = N_HEADS // KV_GROUPS
WINDOW = 128
CHUNK = 128
AUG_W = N_HEADS * CHUNK
N_DEV = 8
IN_SHARD = 673
ADA_SHARD = 384
NORM_EPS = 1e-6
ROPE_THETA = 10000.0
Q_SCALE = HEAD_DIM ** -0.5
NEG = -1e30

ADAM_LR = 0.001
ADAM_B1 = 0.9
ADAM_B2 = 0.999
ADAM_EPS = 1e-08
ADAM_WD = 0.01
ADAM_STEP = 10
ADAM_C1 = 1.0 - ADAM_B1 ** ADAM_STEP
ADAM_C2 = 1.0 - ADAM_B2 ** ADAM_STEP

SEG = {}
_off = 0
for _name, _w in (("qa", 512), ("ka", 256), ("va", 256), ("qb", 512), ("kb", 512), ("vb", 512),
                  ("za", 512), ("zb", 512), ("ga", 1024), ("gb", 1024), ("f", 128)):
    SEG[_name] = (_off, _w)
    _off += _w
PROJ_W = _off
DPROJ_GROUPS = (("a", SEG["qa"][0], 1024), ("qb", SEG["qb"][0], 512), ("kb", SEG["kb"][0], 512),
                ("vb", SEG["vb"][0], 512), ("zg", SEG["za"][0], 3072), ("f", SEG["f"][0], 128))

VMEM_LIMIT = 56 * 1024 * 1024


def _params(n_axes):
    return pltpu.CompilerParams(dimension_semantics=("arbitrary",) * n_axes, vmem_limit_bytes=VMEM_LIMIT)


def _dot(a, b):
    return jnp.dot(a, b, preferred_element_type=F32)


def _dot_nt(a, b):
    return lax.dot_general(a, b, (((1,), (1,)), ((), ())), preferred_element_type=F32)


def _dot_tn(a, b):
    return lax.dot_general(a, b, (((0,), (0,)), ((), ())), preferred_element_type=F32)


def _lane(n):
    return lax.broadcasted_iota(jnp.int32, (1, n), 1)


def _split3(x):
    hi = x.astype(BF16).astype(F32)
    r = x - hi
    mid = r.astype(BF16).astype(F32)
    lo = (r - mid).astype(BF16).astype(F32)
    return hi, mid, lo


def _dot_split(x, b):
    hi, mid, lo = _split3(x)
    return _dot(hi.astype(BF16), b) + _dot(mid.astype(BF16), b) + _dot(lo.astype(BF16), b)


def _spread3(x, e3_ref):
    hi, mid, lo = _split3(x)
    return _dot(hi.astype(BF16), e3_ref[0]) + _dot(mid.astype(BF16), e3_ref[1]) + _dot(lo.astype(BF16), e3_ref[2])


def _place3(lane, base, x, other):
    hi, mid, lo = _split3(x)
    return jnp.where(lane == base, hi, jnp.where(lane == base + 1, mid, jnp.where(lane == base + 2, lo, other)))


def _lane_sum(x, lo, hi):
    lane = _lane(x.shape[1])
    return jnp.sum(jnp.where((lane >= lo) & (lane < hi), x, 0.0), axis=1, keepdims=True)


def _ones_lanes(lo, hi):
    lane = _lane(CHUNK)
    return jnp.where((lane >= lo) & (lane < hi), 1.0, 0.0).astype(F32)


def _rope(c, cos, sa, sb):
    return c * cos + pltpu.roll(c, CHUNK - 32, 1) * sa + pltpu.roll(c, 32, 1) * sb


def _rope_inv(d, cos, sa, sb):
    return d * cos - (pltpu.roll(d, CHUNK - 32, 1) * sa + pltpu.roll(d, 32, 1) * sb)


def _pair(c0, c1):
    return jnp.where(_lane(CHUNK) < HEAD_DIM, c0, pltpu.roll(c1, HEAD_DIM, 1))


def _sigmoid(x):
    return 0.5 * jnp.tanh(0.5 * x) + 0.5


N_CHIPS = 4
ANY_SPEC = pl.BlockSpec(memory_space=pl.ANY)


def _exchange(arrays, modes, name):
    n = len(arrays)

    def body(*refs):
        copies = _exchange_copies(refs[:n], refs[n:2 * n], modes, *refs[2 * n:])
        _exchange_start(copies)
        _exchange_wait(copies)

    return pl.pallas_call(
        body, name=name, out_shape=_exchange_out_shapes(arrays, modes),
        in_specs=[ANY_SPEC] * n, out_specs=[ANY_SPEC] * n, scratch_shapes=_exchange_sems(n),
    )(*arrays)


def _exchange_out_shapes(arrays, modes):
    return [jax.ShapeDtypeStruct((N_DEV,) + a.shape if md == "gather" else a.shape, a.dtype)
            for a, md in zip(arrays, modes)]


def _exchange_sems(n):
    return [pltpu.SemaphoreType.DMA((n, N_DEV - 1)), pltpu.SemaphoreType.DMA((n, N_DEV - 1)),
            pltpu.SemaphoreType.DMA((n,))]


def _exchange_copies(srcs, dsts, modes, send_sems, recv_sems, loc_sems):
    n = len(srcs)
    x, y, c = lax.axis_index("x"), lax.axis_index("y"), lax.axis_index("c")

    def slot(i, px, py, pc):
        return 4 * px + 2 * py + pc if modes[i] == "gather" else 2 * px + py

    def src_of(i, px, py, pc):
        return srcs[i] if modes[i] == "gather" else srcs[i].at[slot(i, px, py, pc)]

    local = [pltpu.make_async_copy(src_of(i, x, y, c), dsts[i].at[slot(i, x, y, c)], loc_sems.at[i])
             for i in range(n)]
    sends, recvs = [], []
    for r in (1, 2, 4, 3, 5, 6, 7):
        px = 1 - x if r & 4 else x
        py = 1 - y if r & 2 else y
        pc = 1 - c if r & 1 else c
        for i in range(n):
            if modes[i] == "chips" and r & 1:
                continue
            sems = dict(send_sem=send_sems.at[i, r - 1], recv_sem=recv_sems.at[i, r - 1],
                        device_id=(px, py, pc), device_id_type=pl.DeviceIdType.MESH)
            sends.append(pltpu.make_async_remote_copy(
                src_ref=src_of(i, px, py, pc), dst_ref=dsts[i].at[slot(i, x, y, c)], **sems))
            recvs.append(pltpu.make_async_remote_copy(
                src_ref=src_of(i, px, py, pc), dst_ref=dsts[i].at[slot(i, px, py, pc)], **sems))
    return local, sends, recvs


def _exchange_start(copies):
    local, sends, _ = copies
    for cp in local + sends:
        cp.start()


def _exchange_wait(copies):
    local, sends, recvs = copies
    for cp in recvs:
        cp.wait_recv()
    for cp in sends:
        cp.wait_send()
    for cp in local:
        cp.wait()


def _gather_two_level(arrays, name):
    n = len(arrays)

    def body(*refs):
        _gather2_start(refs[:n], refs[n:2 * n], *refs[2 * n:])
        _gather2_finish(refs[:n], refs[n:2 * n], *refs[2 * n:])

    return pl.pallas_call(
        body, name=name, out_shape=[jax.ShapeDtypeStruct((N_DEV,) + a.shape, a.dtype) for a in arrays],
        in_specs=[ANY_SPEC] * n, out_specs=[ANY_SPEC] * n, scratch_shapes=_exchange_sems(n),
    )(*arrays)


def _gather2_parts(srcs, dsts, send_sems, recv_sems, loc_sems):
    n = len(srcs)
    x, y, c = lax.axis_index("x"), lax.axis_index("y"), lax.axis_index("c")
    me, sibling = (x, y, c), (x, y, 1 - c)
    chips = [(1 - x, y), (x, 1 - y), (1 - x, 1 - y)]

    def rows(i, dev):
        return dsts[i].at[4 * dev[0] + 2 * dev[1] + dev[2]]

    def copy(i, k, block, to, src=None):
        return pltpu.make_async_remote_copy(
            src_ref=rows(i, block) if src is None else src, dst_ref=rows(i, block),
            send_sem=send_sems.at[i, k], recv_sem=recv_sems.at[i, k],
            device_id=to, device_id_type=pl.DeviceIdType.MESH)

    local = [pltpu.make_async_copy(srcs[i], rows(i, me), loc_sems.at[i]) for i in range(n)]
    first = []
    for i in range(n):
        first.append(copy(i, 0, me, sibling, src=srcs[i]))
        first += [copy(i, 1 + j, me, (*chip, c), src=srcs[i]) for j, chip in enumerate(chips)]
    return n, c, me, sibling, chips, copy, local, first


def _gather2_start(srcs, dsts, send_sems, recv_sems, loc_sems):
    *_, local, first = _gather2_parts(srcs, dsts, send_sems, recv_sems, loc_sems)
    for cp in local + first:
        cp.start()


def _gather2_finish(srcs, dsts, send_sems, recv_sems, loc_sems):
    n, c, me, sibling, chips, copy, local, first = _gather2_parts(srcs, dsts, send_sems, recv_sems, loc_sems)
    passed = []
    for j, chip in enumerate(chips):
        for i in range(n):
            copy(i, 1 + j, (*chip, c), me).wait_recv()
            cp = copy(i, 4 + j, (*chip, c), sibling)
            cp.start()
            passed.append(cp)
    for i in range(n):
        copy(i, 0, sibling, me).wait_recv()
        for j, chip in enumerate(chips):
            copy(i, 4 + j, (*chip, 1 - c), me).wait_recv()
    for cp in first + passed:
        cp.wait_send()
    for cp in local:
        cp.wait()


def _swap_sibling(arrays, name):
    n = len(arrays)
    out_shape = [jax.ShapeDtypeStruct(a.shape[1:], a.dtype) for a in arrays]

    def body(*refs):
        srcs, dsts = refs[:n], refs[n:2 * n]
        send_sems, recv_sems = refs[2 * n:]
        x, y, c = lax.axis_index("x"), lax.axis_index("y"), lax.axis_index("c")
        copies = [pltpu.make_async_remote_copy(
            src_ref=srcs[i].at[1 - c], dst_ref=dsts[i], send_sem=send_sems.at[i], recv_sem=recv_sems.at[i],
            device_id=(x, y, 1 - c), device_id_type=pl.DeviceIdType.MESH) for i in range(n)]
        for cp in copies:
            cp.start()
        for cp in copies:
            cp.wait()

    return pl.pallas_call(
        body, name=name, out_shape=out_shape, in_specs=[ANY_SPEC] * n, out_specs=[ANY_SPEC] * n,
        scratch_shapes=[pltpu.SemaphoreType.DMA((n,)), pltpu.SemaphoreType.DMA((n,))],
    )(*arrays)


def _chip_partial(core, slots, theirs, tr, name):
    _, k, rws, cols = slots.shape

    def body(core_ref, a_ref, b_ref, f_ref, h_ref):
        sm = a_ref[0] + b_ref[...]
        f_ref[...] = sm
        h_ref[...] = sm.astype(BF16)

    blk = pl.BlockSpec((1, tr, cols), lambda j, i, core_ref: (j, i, 0))
    mine = pl.BlockSpec((1, 1, tr, cols), lambda j, i, core_ref: (core_ref[0], j, i, 0))
    return pl.pallas_call(
        body, name=name,
        grid_spec=pltpu.PrefetchScalarGridSpec(num_scalar_prefetch=1, grid=(k, rws // tr),
                                               in_specs=[mine, blk], out_specs=[blk, blk]),
        out_shape=[jax.ShapeDtypeStruct(theirs.shape, F32), jax.ShapeDtypeStruct(theirs.shape, BF16)],
        compiler_params=_params(2))(core, slots, theirs)


def _ada_fwd(c_all, w_shard, b_shard):
    def body(c_ref, w_ref, b_ref, o_ref):
        ch, cm, cl = [t.astype(BF16) for t in _split3(c_ref[...])]
        wh, wm, wl = [t.astype(BF16) for t in _split3(w_ref[...])]
        acc = _dot(ch, wh) + _dot(ch, wm) + _dot(cm, wh) + _dot(ch, wl) + _dot(cl, wh) + _dot(cm, wm)
        o_ref[...] = acc + b_ref[...]

    return pl.pallas_call(body, name="ada_fwd", out_shape=jax.ShapeDtypeStruct((N_DEV, ADA_SHARD), F32),
                          compiler_params=_params(0))(c_all, w_shard, b_shard)


def _rope_tables_and_gather(pos_col, inv_freq, tm, weights, c_rows):
    s = pos_col.shape[0]
    nw = len(weights)
    last = s // tm - 1

    def body(*refs):
        p_ref, f_ref = refs[:2]
        w_src, c_src = refs[2:2 + nw], refs[2 + nw:3 + nw]
        cos_ref, sa_ref, sb_ref = refs[3 + nw:6 + nw]
        w_dst, c_dst = refs[6 + nw:6 + 2 * nw], refs[6 + 2 * nw:7 + 2 * nw]
        w_sems, c_sems = refs[7 + 2 * nw:10 + 2 * nw], refs[10 + 2 * nw:]

        @pl.when(pl.program_id(0) == 0)
        def _():
            _gather2_start(w_src, w_dst, *w_sems)
            _exchange_start(_exchange_copies(c_src, c_dst, ["gather"], *c_sems))

        ang = p_ref[...] * f_ref[...]
        sin = jnp.sin(ang)
        first_half = (_lane(CHUNK) & (HEAD_DIM - 1)) < HEAD_DIM // 2
        cos_ref[...] = jnp.cos(ang)
        sa_ref[...] = jnp.where(first_half, -sin, 0.0)
        sb_ref[...] = jnp.where(first_half, 0.0, sin)

        @pl.when(pl.program_id(0) == last)
        def _():
            _exchange_wait(_exchange_copies(c_src, c_dst, ["gather"], *c_sems))
            _gather2_finish(w_src, w_dst, *w_sems)

    tab = jax.ShapeDtypeStruct((s, CHUNK), F32)
    blk = pl.BlockSpec((tm, CHUNK), lambda i: (i, 0))
    gathered = [jax.ShapeDtypeStruct((N_DEV,) + a.shape, a.dtype) for a in list(weights) + [c_rows]]
    outs = pl.pallas_call(
        body, name="rope_tables_gather", grid=(s // tm,), out_shape=[tab, tab, tab] + gathered,
        in_specs=[pl.BlockSpec((tm, 1), lambda i: (i, 0)), pl.BlockSpec((1, CHUNK), lambda i: (0, 0))]
        + [ANY_SPEC] * (nw + 1),
        out_specs=[blk, blk, blk] + [ANY_SPEC] * (nw + 1),
        scratch_shapes=_exchange_sems(nw) + _exchange_sems(1),
        compiler_params=_params(1))(pos_col, inv_freq, *weights, c_rows)
    return outs[:3], outs[3:3 + nw], outs[3 + nw]


def _fwd_proj(x, mod, w_all, cos, sa, sb, bf_row, emat, ind, tm):
    s = x.shape[0]

    def body(x_ref, mod_ref, w_ref, cos_ref, sa_ref, sb_ref, bf_ref, e_ref, ind_ref,
             h_ref, qa_ref, ka_ref, va_ref, za_ref, qb_ref, kb_ref, vb_ref, zb_ref, ga_ref, gb_ref, f_ref,
             bnd_ref, nrm_ref, carry_ref):
        @pl.when(pl.program_id(0) == 0)
        def _():
            carry_ref[...] = jnp.zeros_like(carry_ref)

        xv = x_ref[...]
        rstd = lax.rsqrt(jnp.mean(xv * xv, axis=-1, keepdims=True) + NORM_EPS)
        h = (xv * rstd * mod_ref[0:1, :]) * (1.0 + mod_ref[1:2, :]) + mod_ref[2:3, :]
        hb = h.astype(BF16)
        h_ref[...] = hb.T

        def seg(name):
            off, w = SEG[name]
            return _dot(hb, w_ref[:, off:off + w])

        lane = _lane(CHUNK)
        low = lane < HEAD_DIM
        q_ones = _ones_lanes(64, 67)
        k_ones = _ones_lanes(67, 70)
        cos_t, sa_t, sb_t = cos_ref[...], sa_ref[...], sb_ref[...]

        def write_heads(ref, nat, extra_of, rope, scale):
            for p in range(N_HEADS // 2):
                c = nat[:, CHUNK * p:CHUNK * (p + 1)]
                if rope:
                    c = _rope(c, cos_t, sa_t, sb_t)
                if scale != 1.0:
                    c = c * scale
                for hh in range(2):
                    hd = 2 * p + hh
                    src = c if hh == 0 else pltpu.roll(c, HEAD_DIM, 1)
                    ref[:, CHUNK * hd:CHUNK * (hd + 1)] = jnp.where(low, src, extra_of(hd)).astype(BF16)

        write_heads(qa_ref, seg("qa"), lambda hd: q_ones, True, Q_SCALE)
        ka = seg("ka")
        va = seg("va")
        for kv in range(KV_GROUPS):
            sl = slice(CHUNK * kv, CHUNK * (kv + 1))
            ka_ref[:, sl] = jnp.where(low, _rope(ka[:, sl], cos_t, sa_t, sb_t), k_ones).astype(BF16)
            va_ref[:, sl] = jnp.where(low, va[:, sl], q_ones).astype(BF16)
        za_ref[...] = seg("za")
        zb_ref[...] = seg("zb")
        ga_ref[...] = seg("ga")
        gb_ref[...] = seg("gb")

        xf = seg("f") + bf_ref[...]
        f_ref[...] = xf
        logf = jnp.minimum(xf, 0.0) - jnp.log1p(jnp.exp(-jnp.abs(xf)))
        row = lax.broadcasted_iota(jnp.int32, (tm, tm), 0)
        col = lax.broadcasted_iota(jnp.int32, (tm, tm), 1)
        tri = jnp.where(col <= row, 1.0, 0.0).astype(BF16)
        hi, mid, lo = _split3(logf)
        cum = _dot(tri, hi.astype(BF16)) + _dot(tri, mid.astype(BF16)) + _dot(tri, lo.astype(BF16))
        cum = cum + carry_ref[0:1, :]
        carry_ref[0:1, :] += jnp.sum(logf, axis=0, keepdims=True)
        lane_all = _lane(AUG_W) & (CHUNK - 1)
        bias = _spread3(-cum, e_ref) + jnp.where((lane_all >= 67) & (lane_all < 70), 1.0, 0.0)

        qb, kb = seg("qb"), seg("kb")
        write_heads(qb_ref, qb, lambda hd: q_ones, False, Q_SCALE)
        write_heads(kb_ref, kb, lambda hd: bias[:, CHUNK * hd:CHUNK * (hd + 1)], False, 1.0)
        write_heads(vb_ref, seg("vb"), lambda hd: q_ones, False, 1.0)

        rid = lax.broadcasted_iota(jnp.int32, (tm, CHUNK), 0)
        bnd_ref[...] = jnp.zeros_like(bnd_ref)
        bnd_ref[0, 0:1, :] = jnp.sum(jnp.where(rid == 0, cum, 0.0), axis=0, keepdims=True)
        bnd_ref[0, 1:2, :] = jnp.sum(jnp.where(rid == tm - 1, cum, 0.0), axis=0, keepdims=True)

        @pl.when(pl.program_id(0) == 0)
        def _():
            nrm_ref[...] = jnp.zeros_like(nrm_ref)

        for r_, nat in ((0, qb * Q_SCALE), (1, kb)):
            sq = _dot((nat * nat).astype(BF16), ind_ref[...])
            nrm_ref[r_:r_ + 1, :] = jnp.maximum(nrm_ref[r_:r_ + 1, :], jnp.max(sq, axis=0, keepdims=True))

    row_blk = lambda w: pl.BlockSpec((tm, w), lambda i: (i, 0))
    full = lambda a: pl.BlockSpec(a.shape, lambda i: (0,) * a.ndim)
    sds = lambda w, dt: jax.ShapeDtypeStruct((s, w), dt)
    out_shape = [sds(AUG_W, BF16), sds(KV_GROUPS * CHUNK, BF16), sds(KV_GROUPS * CHUNK, BF16),
                 sds(512, F32), sds(AUG_W, BF16), sds(AUG_W, BF16), sds(AUG_W, BF16), sds(512, F32),
                 sds(D_MODEL, F32), sds(D_MODEL, F32), sds(CHUNK, F32)]
    small = [jax.ShapeDtypeStruct((s // tm, 8, CHUNK), F32), jax.ShapeDtypeStruct((8, CHUNK), F32)]
    return pl.pallas_call(
        body, name="fwd_proj", grid=(s // tm,),
        out_shape=[jax.ShapeDtypeStruct((D_MODEL, s), BF16)] + out_shape + small,
        in_specs=[row_blk(D_MODEL), full(mod), full(w_all), row_blk(CHUNK), row_blk(CHUNK), row_blk(CHUNK),
                  full(bf_row), full(emat), full(ind)],
        out_specs=[pl.BlockSpec((D_MODEL, tm), lambda i: (0, i))] + [row_blk(o.shape[1]) for o in out_shape] + [
            pl.BlockSpec((1, 8, CHUNK), lambda i: (i, 0, 0)), pl.BlockSpec((8, CHUNK), lambda i: (0, 0))],
        scratch_shapes=[pltpu.VMEM((8, CHUNK), F32)],
        compiler_params=_params(1))(x, mod, w_all, cos, sa, sb, bf_row, emat, ind)


def _swa_fwd(q_aug, k_aug, v_aug, sink_rows, tq):
    s = q_aug.shape[0]
    r = tq // WINDOW
    gw = GROUP * CHUNK

    def body(q_ref, kc_ref, kp_ref, vc_ref, vp_ref, sink_ref, o_ref, qb_ref):
        i = pl.program_id(1)
        st = GROUP * WINDOW
        qloc = lax.broadcasted_iota(jnp.int32, (st, 2 * WINDOW), 0) & (WINDOW - 1)
        col = lax.broadcasted_iota(jnp.int32, (st, 2 * WINDOW), 1)
        band = (col > qloc) & (col <= qloc + WINDOW)
        head = jnp.right_shift(lax.broadcasted_iota(jnp.int32, (st, 1), 0), 7)
        sink = jnp.zeros((st, 1), F32)
        for g in range(GROUP):
            sink = jnp.where(head == g, jnp.max(sink_ref[0, g:g + 1, :], axis=1, keepdims=True), sink)
        lane = _lane(CHUNK)
        for sub in range(r):
            rows = slice(WINDOW * sub, WINDOW * (sub + 1))
            q = jnp.concatenate([q_ref[rows, CHUNK * g:CHUNK * (g + 1)] for g in range(GROUP)], axis=0)
            if sub == 0:
                k = jnp.concatenate([kp_ref[...], kc_ref[rows, :]], axis=0)
                v = jnp.concatenate([vp_ref[...], vc_ref[rows, :]], axis=0)
                valid = band & ((col >= WINDOW) | (i > 0))
            else:
                both = slice(WINDOW * (sub - 1), WINDOW * (sub + 1))
                k, v, valid = kc_ref[both, :], vc_ref[both, :], band
            sc = jnp.where(valid, _dot_nt(q, k), NEG)
            m = jnp.maximum(jnp.max(sc, axis=1, keepdims=True), sink)
            acc = _dot(jnp.exp(sc - m).astype(BF16), v)
            denom = _lane_sum(acc, 64, 65) + jnp.exp(sink - m)
            out = acc / denom
            aug = _place3(lane, 67, -(m + jnp.log(denom)), q.astype(F32)).astype(BF16)
            hrows = lambda a, g: a[WINDOW * g:WINDOW * (g + 1), :]
            for g in range(GROUP):
                qb_ref[rows, CHUNK * g:CHUNK * (g + 1)] = hrows(aug, g)
            for pp in range(GROUP // 2):
                o_ref[rows, CHUNK * pp:CHUNK * (pp + 1)] = _pair(hrows(out, 2 * pp), hrows(out, 2 * pp + 1))

    return pl.pallas_call(
        body, name="swa_fwd", grid=(KV_GROUPS, s // tq),
        out_shape=[jax.ShapeDtypeStruct((s, 512), F32), jax.ShapeDtypeStruct((s, AUG_W), BF16)],
        in_specs=[pl.BlockSpec((tq, gw), lambda kv, i: (i, kv)),
                  pl.BlockSpec((tq, CHUNK), lambda kv, i: (i, kv)),
                  pl.BlockSpec((WINDOW, CHUNK), lambda kv, i: (jnp.maximum(i * r - 1, 0), kv)),
                  pl.BlockSpec((tq, CHUNK), lambda kv, i: (i, kv)),
                  pl.BlockSpec((WINDOW, CHUNK), lambda kv, i: (jnp.maximum(i * r - 1, 0), kv)),
                  pl.BlockSpec((1, 8, CHUNK), lambda kv, i: (kv, 0, 0))],
        out_specs=[pl.BlockSpec((tq, GROUP * HEAD_DIM), lambda kv, i: (i, kv)),
                   pl.BlockSpec((tq, gw), lambda kv, i: (i, kv))],
        compiler_params=_params(2))(q_aug, k_aug, k_aug, v_aug, v_aug, sink_rows)


def _swa_bwd(k_aug, v_aug, q_bwd, do_aug, cos, sa, sb, sink_rows, tk):
    s = k_aug.shape[0]
    r = tk // WINDOW
    nt = s // tk
    nb = s // WINDOW
    gw = GROUP * CHUNK

    def body(k_ref, v_ref, q_ref, qn_ref, do_ref, don_ref, cos_ref, sa_ref, sb_ref, sink_ref,
             da_ref, dsink_ref, carry_ref, acc_ref):
        j = pl.program_id(0)

        @pl.when(j == 0)
        def _():
            carry_ref[...] = jnp.zeros_like(carry_ref)
            dsink_ref[...] = jnp.zeros_like(dsink_ref)

        acc_ref[...] = jnp.zeros_like(acc_ref)
        acc_ref[0:WINDOW, :] = carry_ref[...]
        dk_col, dv_col = SEG["ka"][0] - SEG["qa"][0], SEG["va"][0] - SEG["qa"][0]
        st = GROUP * WINDOW
        key = lax.broadcasted_iota(jnp.int32, (WINDOW, 2 * st), 0)
        col = lax.broadcasted_iota(jnp.int32, (WINDOW, 2 * st), 1)
        qloc = col & (WINDOW - 1)
        band = ((col < st) & (key <= qloc)) | ((col >= st) & (key > qloc))
        lane = _lane(CHUNK)
        low = lane < HEAD_DIM
        cos_t, sa_t, sb_t = cos_ref[...], sa_ref[...], sb_ref[...]
        for kv, sub in [(kv, sub) for kv in range(KV_GROUPS) for sub in range(r)]:
            heads = [slice(CHUNK * (GROUP * kv + g), CHUNK * (GROUP * kv + g + 1)) for g in range(GROUP)]
            kvs = slice(CHUNK * kv, CHUNK * (kv + 1))
            rows = slice(WINDOW * sub, WINDOW * (sub + 1))
            nxt = slice(WINDOW * (sub + 1), WINDOW * (sub + 2))
            k, v = k_ref[rows, kvs], v_ref[rows, kvs]
            q_cur, do_cur = [q_ref[rows, cs] for cs in heads], [do_ref[rows, cs] for cs in heads]
            if sub < r - 1:
                q_nxt, do_nxt, valid = [q_ref[nxt, cs] for cs in heads], [do_ref[nxt, cs] for cs in heads], band
            else:
                q_nxt, do_nxt = [qn_ref[:, cs] for cs in heads], [don_ref[:, cs] for cs in heads]
                valid = band & ((col < st) | (j < nt - 1))
            q = jnp.concatenate(q_cur + q_nxt, axis=0)
            do = jnp.concatenate(do_cur + do_nxt, axis=0)
            pt = jnp.exp(jnp.where(valid, _dot_nt(k, q), NEG))
            ds = (pt * _dot_nt(v, do)).astype(BF16)
            dv = _dot(pt.astype(BF16), do)
            dk = _dot(ds, q)
            dq = _dot_tn(ds, k)
            for g, cs in enumerate(heads):
                acc_ref[rows, cs] += dq[WINDOW * g:WINDOW * (g + 1), :]
                dqn = dq[st + WINDOW * g:st + WINDOW * (g + 1), :]
                if sub < r - 1:
                    acc_ref[nxt, cs] += dqn
                else:
                    carry_ref[:, cs] = dqn
                sink = jnp.max(sink_ref[kv, g:g + 1, :], axis=1, keepdims=True)
                p_sink = jnp.exp(sink + _lane_sum(q_cur[g].astype(F32), 67, 70))
                term = jnp.sum(p_sink * _lane_sum(do_cur[g].astype(F32), 64, 67), axis=0, keepdims=True)
                dsink_ref[kv, g:g + 1, :] += jnp.broadcast_to(term, (1, CHUNK))
            da_ref[rows, dk_col + CHUNK * kv:dk_col + CHUNK * (kv + 1)] = _rope_inv(
                jnp.where(low, dk, 0.0), cos_t[rows, :], sa_t[rows, :], sb_t[rows, :]).astype(BF16)
            da_ref[rows, dv_col + CHUNK * kv:dv_col + CHUNK * (kv + 1)] = jnp.where(low, dv, 0.0).astype(BF16)
        for pp in range(N_HEADS // 2):
            d = _pair(acc_ref[:, CHUNK * 2 * pp:CHUNK * (2 * pp + 1)], acc_ref[:, CHUNK * (2 * pp + 1):CHUNK * (2 * pp + 2)])
            da_ref[:, CHUNK * pp:CHUNK * (pp + 1)] = (_rope_inv(d, cos_t, sa_t, sb_t) * Q_SCALE).astype(BF16)

    cur = lambda w: pl.BlockSpec((tk, w), lambda j: (j, 0))
    nxt = pl.BlockSpec((WINDOW, AUG_W), lambda j: (jnp.minimum((j + 1) * r, nb - 1), 0))
    whole = pl.BlockSpec((KV_GROUPS, 8, CHUNK), lambda j: (0, 0, 0))
    return pl.pallas_call(
        body, name="swa_bwd", grid=(nt,),
        out_shape=[jax.ShapeDtypeStruct((s, 1024), BF16), jax.ShapeDtypeStruct((KV_GROUPS, 8, CHUNK), F32)],
        in_specs=[cur(KV_GROUPS * CHUNK), cur(KV_GROUPS * CHUNK), cur(AUG_W), nxt, cur(AUG_W), nxt,
                  cur(CHUNK), cur(CHUNK), cur(CHUNK), whole],
        out_specs=[cur(1024), whole],
        scratch_shapes=[pltpu.VMEM((WINDOW, AUG_W), F32), pltpu.VMEM((tk, AUG_W), F32)],
        compiler_params=_params(1))(k_aug, v_aug, q_bwd, q_bwd, do_aug, do_aug, cos, sa, sb, sink_rows)


def _fox_fwd(first_key, q_aug, k_aug, v_aug, t):
    s = q_aug.shape[0]
    pw = 2 * CHUNK

    def body(lo_ref, q_ref, k_ref, v_ref, o_ref, qb_ref):
        i = pl.program_id(1)
        first = lo_ref[pl.program_id(0), i]
        row = lax.broadcasted_iota(jnp.int32, (t, t), 0)
        col = lax.broadcasted_iota(jnp.int32, (t, t), 1)
        lane = _lane(CHUNK)
        heads = (slice(0, CHUNK), slice(CHUNK, pw))
        qs = [q_ref[:, hs] for hs in heads]

        def step(jb, carry, masked):
            rows = pl.ds(pl.multiple_of(jb * t, t), t)
            new = []
            for hh, hs in enumerate(heads):
                m, acc = carry[2 * hh], carry[2 * hh + 1]
                sc = _dot_nt(qs[hh], k_ref[rows, hs])
                if masked:
                    sc = jnp.where(col <= row, sc, NEG)
                m_new = jnp.maximum(m, jnp.max(sc, axis=1, keepdims=True))
                p = jnp.exp(sc - m_new).astype(BF16)
                new += [m_new, jnp.exp(m - m_new) * acc + _dot(p, v_ref[rows, hs])]
            return tuple(new)

        init = (jnp.full((t, 1), NEG, F32), jnp.zeros((t, CHUNK), F32)) * 2
        carry = step(i, lax.fori_loop(first, i, lambda jb, c: step(jb, c, False), init), True)
        outs = []
        for hh, hs in enumerate(heads):
            m, acc = carry[2 * hh], carry[2 * hh + 1]
            denom = _lane_sum(acc, 64, 65)
            outs.append(acc / denom)
            lse = m + jnp.log(denom)
            qb_ref[:, hs] = _place3(lane, 67, -lse, qs[hh].astype(F32)).astype(BF16)
        o_ref[...] = _pair(outs[0], outs[1])

    return pl.pallas_call(
        body, name="fox_fwd", grid=(N_HEADS // 2, s // t),
        out_shape=[jax.ShapeDtypeStruct((s, 512), F32), jax.ShapeDtypeStruct((s, AUG_W), BF16)],
        in_specs=[pl.BlockSpec(memory_space=pltpu.SMEM),
                  pl.BlockSpec((t, pw), lambda p, i: (i, p)),
                  pl.BlockSpec((s, pw), lambda p, i: (0, p)),
                  pl.BlockSpec((s, pw), lambda p, i: (0, p))],
        out_specs=[pl.BlockSpec((t, CHUNK), lambda p, i: (i, p)), pl.BlockSpec((t, pw), lambda p, i: (i, p))],
        compiler_params=_params(2))(first_key, q_aug, k_aug, v_aug)


def _fox_block_ranges(bounds, norms, r):
    cum_first = bounds[0::r, 0, :N_HEADS]
    cum_last = bounds[r - 1::r, 1, :N_HEADS]
    n = cum_first.shape[0]
    reach = 2.0 * 1.02 * jnp.sqrt(norms[0, :N_HEADS] * norms[1, :N_HEADS]) + 105.0
    blk = jnp.arange(n, dtype=jnp.int32)
    decay = cum_last[None, :, :] - cum_first[:, None, :]
    skip = (decay > reach[None, None, :]) & (blk[None, :, None] < blk[:, None, None])
    first = jnp.min(jnp.where(skip, n, blk[None, :, None]), axis=1)
    first = jnp.minimum(first[:, 0::2], first[:, 1::2]).T
    needed = (first[:, :, None] <= blk[None, None, :]) & (blk[None, :, None] >= blk[None, None, :])
    last = jnp.max(jnp.where(needed, blk[None, :, None], blk[None, None, :]), axis=1)
    return first.astype(jnp.int32), last.astype(jnp.int32)


def _fox_bwd(last_query, k_aug, v_aug, q_bwd, do_aug, t):
    s = k_aug.shape[0]
    n = s // t
    pw = 2 * CHUNK

    def body(hi_ref, k_ref, v_ref, q_ref, do_ref, dk_ref, dv_ref, dck_ref, dq_ref, dcq_ref, dq_scr):
        j = pl.program_id(1)
        last = hi_ref[pl.program_id(0), j]

        @pl.when(j == 0)
        def _():
            dq_scr[...] = jnp.zeros_like(dq_scr)

        row = lax.broadcasted_iota(jnp.int32, (t, t), 0)
        col = lax.broadcasted_iota(jnp.int32, (t, t), 1)
        lane = _lane(CHUNK)
        heads = (slice(0, CHUNK), slice(CHUNK, pw))
        ks = [k_ref[:, hs] for hs in heads]
        vs = [v_ref[:, hs] for hs in heads]

        def step(ib, carry, masked):
            rows = pl.ds(pl.multiple_of(ib * t, t), t)
            new = []
            for hh, hs in enumerate(heads):
                dv, dk = carry[2 * hh], carry[2 * hh + 1]
                q, do = q_ref[rows, hs], do_ref[rows, hs]
                st = _dot_nt(ks[hh], q)
                if masked:
                    st = jnp.where(row <= col, st, NEG)
                pt = jnp.exp(st)
                ds = (pt * _dot_nt(vs[hh], do)).astype(BF16)
                new += [dv + _dot(pt.astype(BF16), do), dk + _dot(ds, q)]
                dq_scr[rows, hs] += _dot_tn(ds, ks[hh])
            return tuple(new)

        zero = jnp.zeros((t, CHUNK), F32)
        carry = lax.fori_loop(j + 1, last + 1, lambda ib, c: step(ib, c, False), step(j, (zero,) * 4, True))
        dvs, dks = (carry[0], carry[2]), (carry[1], carry[3])
        dk_ref[...] = _pair(dks[0], dks[1]).astype(BF16)
        dv_ref[...] = _pair(dvs[0], dvs[1]).astype(BF16)
        dck_ref[0] = jnp.where(lane == 0, pltpu.roll(dks[0], 64, 1),
                               jnp.where(lane == 1, pltpu.roll(dks[1], 65, 1), 0.0))

        @pl.when(j == n - 1)
        def _():
            for ib in range(n):
                rows = slice(t * ib, t * (ib + 1))
                d0, d1 = dq_scr[rows, 0:CHUNK], dq_scr[rows, CHUNK:pw]
                dq_ref[rows, :] = (_pair(d0, d1) * Q_SCALE).astype(BF16)
                dcq_ref[0, rows, :] = jnp.where(lane == 0, pltpu.roll(d0, CHUNK - 67, 1),
                                                jnp.where(lane == 1, pltpu.roll(d1, CHUNK - 66, 1), 0.0))

    return pl.pallas_call(
        body, name="fox_bwd", grid=(N_HEADS // 2, n),
        out_shape=[jax.ShapeDtypeStruct((s, 512), BF16), jax.ShapeDtypeStruct((s, 512), BF16),
                   jax.ShapeDtypeStruct((N_HEADS // 2, s, CHUNK), F32), jax.ShapeDtypeStruct((s, 512), BF16),
                   jax.ShapeDtypeStruct((N_HEADS // 2, s, CHUNK), F32)],
        in_specs=[pl.BlockSpec(memory_space=pltpu.SMEM),
                  pl.BlockSpec((t, pw), lambda p, j: (j, p)), pl.BlockSpec((t, pw), lambda p, j: (j, p)),
                  pl.BlockSpec((s, pw), lambda p, j: (0, p)), pl.BlockSpec((s, pw), lambda p, j: (0, p))],
        out_specs=[pl.BlockSpec((t, CHUNK), lambda p, j: (j, p)), pl.BlockSpec((t, CHUNK), lambda p, j: (j, p)),
                   pl.BlockSpec((1, t, CHUNK), lambda p, j: (p, j, 0)), pl.BlockSpec((s, CHUNK), lambda p, j: (0, p)),
                   pl.BlockSpec((1, s, CHUNK), lambda p, j: (p, 0, 0))],
        scratch_shapes=[pltpu.VMEM((s, pw), F32)],
        compiler_params=_params(2))(last_query, k_aug, v_aug, q_bwd, do_aug)


def _fgate_bwd(dcq, dck, xf, tm):
    s = xf.shape[0]
    nt = s // tm

    def body(dq_ref, dc_ref, xf_ref, df_ref, dbf_ref, carry_ref):
        @pl.when(pl.program_id(0) == 0)
        def _():
            carry_ref[...] = jnp.zeros_like(carry_ref)
            dbf_ref[...] = jnp.zeros_like(dbf_ref)

        row = lax.broadcasted_iota(jnp.int32, (tm, tm), 0)
        col = lax.broadcasted_iota(jnp.int32, (tm, tm), 1)
        tri = jnp.where(col >= row, 1.0, 0.0).astype(BF16)
        dcum = dq_ref[0] - dc_ref[0]
        for p in range(1, N_HEADS // 2):
            dcum = dcum + pltpu.roll(dq_ref[p] - dc_ref[p], 2 * p, 1)
        hi, mid, lo = _split3(dcum)
        dlogf = _dot(tri, hi.astype(BF16)) + _dot(tri, mid.astype(BF16)) + _dot(tri, lo.astype(BF16))
        dlogf = dlogf + carry_ref[0:1, :]
        carry_ref[0:1, :] += jnp.sum(dcum, axis=0, keepdims=True)
        df = dlogf * _sigmoid(-xf_ref[...])
        df_ref[...] = df.astype(BF16)
        dbf_ref[0:1, :] += jnp.sum(df, axis=0, keepdims=True)

    rev = pl.BlockSpec((tm, CHUNK), lambda i: (nt - 1 - i, 0))
    rev4 = pl.BlockSpec((N_HEADS // 2, tm, CHUNK), lambda i: (0, nt - 1 - i, 0))
    return pl.pallas_call(
        body, name="fgate_bwd", grid=(nt,),
        out_shape=[jax.ShapeDtypeStruct((s, CHUNK), BF16), jax.ShapeDtypeStruct((8, CHUNK), F32)],
        in_specs=[rev4, rev4, rev], out_specs=[rev, pl.BlockSpec((8, CHUNK), lambda i: (0, 0))],
        scratch_shapes=[pltpu.VMEM((8, CHUNK), F32)],
        compiler_params=_params(1))(dcq, dck, xf)


def _post(x, tgt, att_a, att_b, za, zb, ga, gb, woa_t, wob_t, wout, vec, ind, emat, tm):
    s = x.shape[0]

    def body(x_ref, t_ref, aa_ref, ab_ref, za_ref, zb_ref, ga_ref, gb_ref, woat_ref, wobt_ref, wout_ref,
             vec_ref, ind_ref, e_ref,
             dx1_ref, ua_ref, ub_ref, mg_ref, do_ref, dya_ref, dyb_ref, doa_ref, dob_ref, dzg_ref, acc_ref):
        @pl.when(pl.program_id(0) == 0)
        def _():
            acc_ref[...] = jnp.zeros_like(acc_ref)

        gate, gfin = vec_ref[0:1, :], vec_ref[1:2, :]
        inv_d = 1.0 / D_MODEL

        def branch_fwd(att_ref, z_ref, w_ref, u_ref):
            att, z = att_ref[...], z_ref[...]
            sz = _sigmoid(z)
            silu = z * sz
            u = (att * silu).astype(BF16)
            u_ref[...] = u.T
            return att, z, sz, silu, _dot_nt(u, w_ref[...])

        att_a, z_a, sz_a, silu_a, y_a = branch_fwd(aa_ref, za_ref, woat_ref, ua_ref)
        att_b, z_b, sz_b, silu_b, y_b = branch_fwd(ab_ref, zb_ref, wobt_ref, ub_ref)
        sg_a, sg_b = _sigmoid(ga_ref[...]), _sigmoid(gb_ref[...])
        merged = (sg_a * y_a + sg_b * y_b).astype(BF16)
        mg_ref[...] = merged.T
        o = _dot(merged, wout_ref[...])
        x1 = x_ref[...] + gate * o
        rstd = lax.rsqrt(jnp.mean(x1 * x1, axis=-1, keepdims=True) + NORM_EPS)
        xh = x1 * rstd
        diff = xh * gfin - t_ref[...]
        acc_ref[2:3, :] += (0.5 * inv_d) * jnp.sum(diff * diff, axis=0, keepdims=True)
        dy = diff * inv_d
        acc_ref[1:2, :] += jnp.sum(dy * xh, axis=0, keepdims=True)
        dyg = dy * gfin
        dx1 = rstd * (dyg - xh * jnp.mean(dyg * xh, axis=-1, keepdims=True))
        dx1_ref[...] = dx1
        acc_ref[0:1, :] += jnp.sum(dx1 * o, axis=0, keepdims=True)
        d_o = (dx1 * gate).astype(BF16)
        do_ref[...] = d_o
        dmg = _dot_nt(d_o, wout_ref[...])

        lane = _lane(CHUNK)
        low = lane < HEAD_DIM
        lane_all = _lane(AUG_W) & (CHUNK - 1)

        zg0 = SEG["za"][0]

        def branch_bwd(sg, y, att, z, sz, silu, wt_ref, dy_ref, g_name, z_name, doaug_ref):
            dyb = (dmg * sg).astype(BF16)
            dy_ref[...] = dyb
            g_off, z_off = SEG[g_name][0] - zg0, SEG[z_name][0] - zg0
            dzg_ref[:, g_off:g_off + D_MODEL] = (dmg * y * sg * (1.0 - sg)).astype(BF16)
            du = _dot(dyb, wt_ref[...])
            datt = du * silu
            dzg_ref[:, z_off:z_off + 512] = (du * att * (sz * (1.0 + z * (1.0 - sz)))).astype(BF16)
            extra = _spread3(-_dot_split(datt * att, ind_ref[...]), e_ref)
            for p in range(N_HEADS // 2):
                c = datt[:, CHUNK * p:CHUNK * (p + 1)]
                for hh in range(2):
                    hd = 2 * p + hh
                    src = c if hh == 0 else pltpu.roll(c, HEAD_DIM, 1)
                    doaug_ref[:, CHUNK * hd:CHUNK * (hd + 1)] = jnp.where(
                        low, src, extra[:, CHUNK * hd:CHUNK * (hd + 1)]).astype(BF16)

        branch_bwd(sg_a, y_a, att_a, z_a, sz_a, silu_a, woat_ref, dya_ref, "ga", "za", doa_ref)
        branch_bwd(sg_b, y_b, att_b, z_b, sz_b, silu_b, wobt_ref, dyb_ref, "gb", "zb", dob_ref)

    row_blk = lambda w: pl.BlockSpec((tm, w), lambda i: (i, 0))
    full = lambda a: pl.BlockSpec(a.shape, lambda i: (0,) * a.ndim)
    sds = lambda w, dt: jax.ShapeDtypeStruct((s, w), dt)
    tds = lambda w: jax.ShapeDtypeStruct((w, s), BF16)
    col_blk = lambda w: pl.BlockSpec((w, tm), lambda i: (0, i))
    out_shape = [sds(D_MODEL, F32), tds(512), tds(512), tds(D_MODEL), sds(D_MODEL, BF16),
                 sds(D_MODEL, BF16), sds(D_MODEL, BF16), sds(AUG_W, BF16), sds(AUG_W, BF16), sds(3072, BF16),
                 jax.ShapeDtypeStruct((8, D_MODEL), F32)]
    ins = [x, tgt, att_a, att_b, za, zb, ga, gb, woa_t, wob_t, wout, vec, ind, emat]
    in_specs = [row_blk(a.shape[1]) for a in ins[:8]] + [full(a) for a in ins[8:]]
    out_specs = ([row_blk(D_MODEL), col_blk(512), col_blk(512), col_blk(D_MODEL)]
                 + [row_blk(o.shape[1]) for o in out_shape[4:-1]] + [pl.BlockSpec((8, D_MODEL), lambda i: (0, 0))])
    return pl.pallas_call(body, name="post", grid=(s // tm,), out_shape=out_shape, in_specs=in_specs,
                          out_specs=out_specs, compiler_params=_params(1))(*ins)


def _bwd_pre(dproj, w_all, x, dx1, mod, tm, chip_halves):
    s = x.shape[0]
    ng, nh = len(DPROJ_GROUPS), len(chip_halves)
    last = s // tm - 1
    modes = ["chips"] * nh

    def body(*refs):
        dp_refs = refs[:ng]
        wt_ref, x_ref, dx1_ref, mod_ref = refs[ng:ng + 4]
        send_refs = refs[ng + 4:ng + 4 + nh]
        gx_ref, acc_ref = refs[ng + 4 + nh:ng + 6 + nh]
        recv_refs = refs[ng + 6 + nh:ng + 6 + 2 * nh]
        sems = refs[ng + 6 + 2 * nh:]

        @pl.when(pl.program_id(0) == 0)
        def _():
            acc_ref[...] = jnp.zeros_like(acc_ref)
            _exchange_start(_exchange_copies(send_refs, recv_refs, modes, *sems))

        dh = None
        for dp_ref, (_, off, w) in zip(dp_refs, DPROJ_GROUPS):
            part = _dot_nt(dp_ref[...], wt_ref[:, off:off + w])
            dh = part if dh is None else dh + part
        xv = x_ref[...]
        rstd = lax.rsqrt(jnp.mean(xv * xv, axis=-1, keepdims=True) + NORM_EPS)
        xh = xv * rstd
        gn = mod_ref[0:1, :]
        acc_ref[0:1, :] += jnp.sum(dh, axis=0, keepdims=True)
        acc_ref[1:2, :] += jnp.sum(dh * (xh * gn), axis=0, keepdims=True)
        dn = dh * (1.0 + mod_ref[1:2, :])
        acc_ref[2:3, :] += jnp.sum(dn * xh, axis=0, keepdims=True)
        dxh = dn * gn
        gx_ref[...] = dx1_ref[...] + rstd * (dxh - xh * jnp.mean(dxh * xh, axis=-1, keepdims=True))

        @pl.when(pl.program_id(0) == last)
        def _():
            _exchange_wait(_exchange_copies(send_refs, recv_refs, modes, *sems))

    row_blk = lambda w: pl.BlockSpec((tm, w), lambda i: (i, 0))
    full = lambda a: pl.BlockSpec(a.shape, lambda i: (0,) * a.ndim)
    outs = pl.pallas_call(
        body, name="bwd_pre", grid=(s // tm,),
        out_shape=[jax.ShapeDtypeStruct((s, D_MODEL), F32), jax.ShapeDtypeStruct((8, D_MODEL), F32)]
        + _exchange_out_shapes(chip_halves, modes),
        in_specs=[row_blk(w) for _, _, w in DPROJ_GROUPS] + [full(w_all), row_blk(D_MODEL), row_blk(D_MODEL), full(mod)]
        + [ANY_SPEC] * nh,
        out_specs=[row_blk(D_MODEL), pl.BlockSpec((8, D_MODEL), lambda i: (0, 0))] + [ANY_SPEC] * nh,
        scratch_shapes=_exchange_sems(nh),
        compiler_params=_params(1))(*dproj, w_all, x, dx1, mod, *chip_halves)
    return outs[0], outs[1], outs[2:]


def _token_matmul(a_t, b, tn, tk, name):
    m, s = a_t.shape
    n = b.shape[1]

    def body(a_ref, b_ref, o_ref):
        @pl.when(pl.program_id(1) == 0)
        def _():
            o_ref[...] = jnp.zeros_like(o_ref)

        o_ref[...] += _dot(a_ref[...], b_ref[...])

    return pl.pallas_call(
        body, name=name, grid=(n // tn, s // tk), out_shape=jax.ShapeDtypeStruct((m, n), F32),
        in_specs=[pl.BlockSpec((m, tk), lambda jn, k: (0, k)), pl.BlockSpec((tk, tn), lambda jn, k: (k, jn))],
        out_specs=pl.BlockSpec((m, tn), lambda jn, k: (0, jn)),
        compiler_params=_params(2))(a_t, b)


def _token_matmul_multi(a_t, bs, tk, name):
    m, s = a_t.shape
    nb = len(bs)

    def body(*refs):
        a_ref, b_refs, o_refs = refs[0], refs[1:1 + nb], refs[1 + nb:]
        a = a_ref[...]
        for b_ref, o_ref in zip(b_refs, o_refs):
            @pl.when(pl.program_id(0) == 0)
            def _():
                o_ref[...] = jnp.zeros_like(o_ref)

            o_ref[...] += _dot(a, b_ref[...])

    return pl.pallas_call(
        body, name=name, grid=(s // tk,),
        out_shape=[jax.ShapeDtypeStruct((m, b.shape[1]), F32) for b in bs],
        in_specs=[pl.BlockSpec((m, tk), lambda k: (0, k))] + [pl.BlockSpec((tk, b.shape[1]), lambda k: (k, 0)) for b in bs],
        out_specs=[pl.BlockSpec((m, b.shape[1]), lambda k: (0, 0)) for b in bs],
        compiler_params=_params(1))(a_t, *bs)


def _adam_math(g, w, m, v):
    m2 = ADAM_B1 * m + (1.0 - ADAM_B1) * g
    v2 = ADAM_B2 * v + (1.0 - ADAM_B2) * (g * g)
    delta = -ADAM_LR * ((m2 / ADAM_C1) / (jnp.sqrt(v2 / ADAM_C2) + ADAM_EPS) + ADAM_WD * w)
    return delta, m2, v2


def _adamw_parts(full, parts, others, w, m, v, tr, name):
    _, rws, cols = w.shape

    def body(oth_ref, f_ref, p_ref, w_ref, m_ref, v_ref, g_ref, d_ref, m2_ref, v2_ref):
        g = None
        for chip in range(N_CHIPS):
            part = (1.0 - oth_ref[chip]) * f_ref[chip] + oth_ref[chip] * p_ref[chip].astype(F32)
            g = part if g is None else g + part
        g_ref[0] = g
        d_ref[0], m2_ref[0], v2_ref[0] = _adam_math(g, w_ref[0], m_ref[0], v_ref[0])

    blk = pl.BlockSpec((1, tr, cols), lambda i: (0, i, 0))
    slots = pl.BlockSpec((N_CHIPS, tr, cols), lambda i: (0, i, 0))
    o = jax.ShapeDtypeStruct((1, rws, cols), F32)
    return pl.pallas_call(
        body, name=name, grid=(rws // tr,), out_shape=[o, o, o, o],
        in_specs=[pl.BlockSpec(memory_space=pltpu.SMEM), slots, slots, blk, blk, blk],
        out_specs=[blk, blk, blk, blk], compiler_params=_params(1))(others, full, parts, w, m, v)


def _adamw_ada(c_lanes, d_rows, w, m, v, tr):
    _, rws, cols = w.shape

    def body(c_ref, d_ref, w_ref, m_ref, v_ref, g_ref, dl_ref, m2_ref, v2_ref):
        for k in range(cols // CHUNK):
            cs = slice(CHUNK * k, CHUNK * (k + 1))
            g = c_ref[0] * d_ref[0:1, cs]
            for b in range(1, N_DEV):
                g = g + c_ref[b] * d_ref[b:b + 1, cs]
            g_ref[0, :, cs] = g
            dl_ref[0, :, cs], m2_ref[0, :, cs], v2_ref[0, :, cs] = _adam_math(
                g, w_ref[0, :, cs], m_ref[0, :, cs], v_ref[0, :, cs])

    blk = pl.BlockSpec((1, tr, cols), lambda i: (0, i, 0))
    o = jax.ShapeDtypeStruct((1, rws, cols), F32)
    return pl.pallas_call(
        body, name="adamw_ada", grid=(rws // tr,), out_shape=[o, o, o, o],
        in_specs=[pl.BlockSpec((N_DEV, tr, CHUNK), lambda i: (0, i, 0)), pl.BlockSpec((N_DEV, cols), lambda i: (0, 0)),
                  blk, blk, blk],
        out_specs=[blk, blk, blk, blk], compiler_params=_params(1))(c_lanes, d_rows, w, m, v)


def _adamw_small(packs, w, m, v):
    def body(p_ref, w_ref, m_ref, v_ref, g_ref, d_ref, m2_ref, v2_ref, loss_ref):
        g = p_ref[0]
        for dev in range(1, N_DEV):
            g = g + p_ref[dev]
        g_ref[...] = g
        d_ref[...], m2_ref[...], v2_ref[...] = _adam_math(g, w_ref[...], m_ref[...], v_ref[...])
        loss_ref[...] = jnp.broadcast_to(jnp.sum(g_ref[2:3, 0:D_MODEL], axis=1, keepdims=True), loss_ref.shape)

    o = jax.ShapeDtypeStruct(w.shape, F32)
    return pl.pallas_call(body, name="adamw_small", out_shape=[o, o, o, o, jax.ShapeDtypeStruct((8, CHUNK), F32)],
                          compiler_params=_params(0))(packs, w, m, v)


def _tile(s, want):
    return min(s, want)


def _layout_pieces():
    orig = {"qa": 0, "ka": 512, "va": 640, "za": 768, "qb": 1280, "kb": 1792, "vb": 2304, "f": 2816, "zb": 2824,
            "ga": 3336, "gb": 4360}
    pieces = []
    for name, (off, w) in SEG.items():
        if name in ("ka", "va"):
            pieces += [(orig[name] + HEAD_DIM * kv, orig[name] + HEAD_DIM * (kv + 1), off + CHUNK * kv) for kv in range(KV_GROUPS)]
        else:
            pieces.append((orig[name], orig[name] + (N_HEADS if name == "f" else w), off))
    return pieces


def _assemble_w_all(win_g):
    cols, pos = [], 0
    for lo, hi, new in sorted(_layout_pieces(), key=lambda t: t[2]):
        if new > pos:
            cols.append(jnp.zeros((D_MODEL, new - pos), win_g.dtype))
        col = lo
        while col < hi:
            dev = col // IN_SHARD
            end = min(hi, (dev + 1) * IN_SHARD)
            cols.append(win_g[dev, :, col - dev * IN_SHARD:end - dev * IN_SHARD])
            col = end
        pos = new + hi - lo
    cols.append(jnp.zeros((D_MODEL, PROJ_W - pos), win_g.dtype))
    return jnp.concatenate(cols, axis=1)


def _grad_slot(dw_groups, dev):
    lo_d, hi_d = dev * IN_SHARD, (dev + 1) * IN_SHARD
    cols = []
    for lo, hi, new in sorted(_layout_pieces()):
        a, b = max(lo, lo_d), min(hi, hi_d)
        if a < b:
            arr, off = next((g, o) for g, (_, o, w) in zip(dw_groups, DPROJ_GROUPS) if o <= new < o + w)
            cols.append(arr[:, new - off + a - lo:new - off + b - lo])
    return jnp.concatenate(cols, axis=1)


def kernel(x, c, positions, w_ada, b_ada, g_norm, w_in, b_f, sinks, w_o_swa, w_o_fox, w_out, g_final, loss_target, m_w_ada, m_b_ada, m_g_norm, m_w_in, m_b_f, m_sinks, m_w_o_swa, m_w_o_fox, m_w_out, m_g_final, v_w_ada, v_b_ada, v_g_norm, v_w_in, v_b_f, v_sinks, v_w_o_swa, v_w_o_fox, v_w_out, v_g_final):
    s = x.shape[1]
    tm = _tile(s, 256)
    ta = _tile(s, 512)
    me = 4 * lax.axis_index("x") + 2 * lax.axis_index("y") + lax.axis_index("c")
    x2, tgt = x[0], loss_target[0]

    inv_freq = np.power(np.float32(ROPE_THETA), -np.arange(0, HEAD_DIM, 2, dtype=np.float32) / HEAD_DIM)
    inv_freq = jnp.asarray(np.tile(inv_freq, CHUNK // (HEAD_DIM // 2))[None, :], F32)
    (cos, sa, sb), (win_g, woa_g, wob_g, wout_g), c_all = _rope_tables_and_gather(
        positions.reshape(s, 1).astype(F32), inv_freq, tm,
        [w_in[0].astype(BF16), w_o_swa[0].T.astype(BF16), w_o_fox[0].T.astype(BF16), w_out[0].astype(BF16)],
        jnp.broadcast_to(c, (8, D_MODEL)))
    c_all = c_all[:, 0, :]
    b_shard = lax.dynamic_slice(b_ada, (0, me * ADA_SHARD), (1, ADA_SHARD))
    ada_part = _ada_fwd(c_all, w_ada[0], b_shard)
    (ada_g,) = _exchange([ada_part], ["gather"], "gather_ada")
    ada = lax.dynamic_index_in_dim(ada_g, me, axis=1, keepdims=False).reshape(3 * D_MODEL)
    shift, scale, gate = ada[:D_MODEL], ada[D_MODEL:2 * D_MODEL], ada[2 * D_MODEL:]
    woa_t = woa_g.reshape(D_MODEL, 512)
    wob_t = wob_g.reshape(D_MODEL, 512)
    wout = wout_g.reshape(D_MODEL, D_MODEL)
    w_all = _assemble_w_all(win_g)

    zrow = jnp.zeros((1, D_MODEL), F32)
    mod = jnp.concatenate([g_norm, scale[None], shift[None], zrow, zrow, zrow, zrow, zrow], axis=0)
    bf_row = jnp.pad(b_f, ((0, 0), (0, CHUNK - N_HEADS)))
    emat_np = np.zeros((3, CHUNK, AUG_W), np.float32)
    ind_np = np.zeros((512, CHUNK), np.float32)
    for hd in range(N_HEADS):
        for part in range(3):
            emat_np[part, hd, CHUNK * hd + 64 + part] = 1.0
        ind_np[HEAD_DIM * hd:HEAD_DIM * (hd + 1), hd] = 1.0
    emat, ind = jnp.asarray(emat_np, BF16), jnp.asarray(ind_np, BF16)
    sink_rows = jnp.broadcast_to(jnp.pad(sinks.reshape(KV_GROUPS, GROUP), ((0, 0), (0, 8 - GROUP)))[:, :, None],
                                 (KV_GROUPS, 8, CHUNK))

    (h_t, qa, ka, va, za, qb, kb, vb, zb, ga, gb, xf, bounds, norms) = _fwd_proj(
        x2, mod, w_all, cos, sa, sb, bf_row, emat, ind, tm)
    first_key, last_query = _fox_block_ranges(bounds, norms, ta // tm)
    att_a, qa_bwd = _swa_fwd(qa, ka, va, sink_rows, ta)
    att_b, qb_bwd = _fox_fwd(first_key, qb, kb, vb, ta)
    vec = jnp.concatenate([gate[None], g_final[None], zrow, zrow, zrow, zrow, zrow, zrow], axis=0)
    (dx1, ua_t, ub_t, merged_t, d_o, dya, dyb, doa, dob, dzg, post_acc) = _post(
        x2, tgt, att_a, att_b, za, zb, ga, gb, woa_t, wob_t, wout, vec, ind, emat, tm)

    da, dsink = _swa_bwd(ka, va, qa_bwd, doa, cos, sa, sb, sink_rows, ta)
    dkb, dvb, dck, dqb, dcq = _fox_bwd(last_query, kb, vb, qb_bwd, dob, ta)
    dfb, dbf_acc = _fgate_bwd(dcq, dck, xf, tm)
    dproj = [da, dqb, dkb, dvb, dzg, dfb]
    dw_zg, = _token_matmul_multi(h_t, [dzg], ta, "dw_in_zg")
    dw_a, dw_qb, dw_kb, dw_vb, dw_f = _token_matmul_multi(h_t, [da, dqb, dkb, dvb, dfb], ta, "dw_in_rest")
    dw_all = [dw_a, dw_qb, dw_kb, dw_vb, dw_zg, dw_f]
    dwoa = _token_matmul(ua_t, dya, 1024, ta, "dw_o_swa")
    dwob = _token_matmul(ub_t, dyb, 1024, ta, "dw_o_fox")
    dwout = _token_matmul(merged_t, d_o, 1024, ta, "dw_out")

    core, chip = lax.axis_index("c"), 2 * lax.axis_index("x") + lax.axis_index("y")
    col_slots = lambda g, r, w: g.reshape(r, N_CHIPS, 2, w).transpose(2, 1, 0, 3)
    win_slots = jnp.stack([jnp.stack([_grad_slot(dw_all, 2 * ch + co) for ch in range(N_CHIPS)]) for co in range(2)])
    slots = [win_slots, col_slots(dwoa, 512, 128), col_slots(dwob, 512, 128),
             dwout.reshape(N_CHIPS, 2, 128, D_MODEL).transpose(1, 0, 2, 3)]
    theirs = _swap_sibling(slots, "grads_sibling")
    fulls, halves = [], []
    core_arr = jnp.reshape(core, (1,)).astype(jnp.int32)
    for sl, th, nm in zip(slots, theirs, ("w_in", "w_o_swa", "w_o_fox", "w_out")):
        full, half = _chip_partial(core_arr, sl, th, min(th.shape[1], 256), "chip_partial_" + nm)
        fulls.append(full)
        halves.append(half)
    grad_x, pre_acc, (p_win, p_woa, p_wob, p_wout) = _bwd_pre(dproj, w_all, x2, dx1, mod, tm, halves)
    misc = jnp.concatenate([dbf_acc[0, :N_HEADS], dsink[:, :GROUP, 0].reshape(N_HEADS), jnp.zeros((D_MODEL - 16,), F32)])
    zpad = jnp.zeros((2 * D_MODEL,), F32)
    pack = jnp.stack([jnp.concatenate([pre_acc[0], pre_acc[1], post_acc[0]]),
                      jnp.concatenate([pre_acc[2], post_acc[1], misc]),
                      jnp.concatenate([post_acc[2], zpad])] + [jnp.zeros((3 * D_MODEL,), F32)] * 5)
    (packs,) = _exchange([pack], ["gather"], "gather_small_grads")
    others = jnp.where(jnp.arange(N_CHIPS) == chip, 0.0, 1.0).astype(F32)

    g_win, d_win, m_win, v_win = _adamw_parts(fulls[0], p_win, others, w_in, m_w_in, v_w_in, 128, "adamw_w_in")
    g_woa, d_woa, m_woa, v_woa = _adamw_parts(fulls[1], p_woa, others, w_o_swa, m_w_o_swa, v_w_o_swa, 128, "adamw_w_o_swa")
    g_wob, d_wob, m_wob, v_wob = _adamw_parts(fulls[2], p_wob, others, w_o_fox, m_w_o_fox, v_w_o_fox, 128, "adamw_w_o_fox")
    g_wout, d_wout, m_wout, v_wout = _adamw_parts(fulls[3], p_wout, others, w_out, m_w_out, v_w_out, 128, "adamw_w_out")
    d_ada_rows = lax.dynamic_slice(packs[:, 0, :], (0, me * ADA_SHARD), (N_DEV, ADA_SHARD))
    c_lanes = jnp.broadcast_to(c_all[:, :, None], (N_DEV, D_MODEL, CHUNK))
    g_wada, d_wada, m_wada, v_wada = _adamw_ada(c_lanes, d_ada_rows, w_ada, m_w_ada, v_w_ada, 256)

    def small_pack(bada, gn, gf, bfv, sk):
        misc_w = jnp.concatenate([bfv[0], sk[0], jnp.zeros((D_MODEL - 16,), F32)])
        return jnp.stack([bada[0], jnp.concatenate([gn[0], gf, misc_w])] + [jnp.zeros((3 * D_MODEL,), F32)] * 6)

    sm = _adamw_small(packs, small_pack(b_ada, g_norm, g_final, b_f, sinks),
                      small_pack(m_b_ada, m_g_norm, m_g_final, m_b_f, m_sinks),
                      small_pack(v_b_ada, v_g_norm, v_g_final, v_b_f, v_sinks))
    loss = sm[4][0, 0]

    def unpack(p):
        return (p[0][None], p[1, :D_MODEL][None], p[1, 2 * D_MODEL:2 * D_MODEL + 8][None],
                p[1, 2 * D_MODEL + 8:2 * D_MODEL + 16][None], p[1, D_MODEL:2 * D_MODEL])

    outs = []
    for big, small in (((g_wada, g_win, g_woa, g_wob, g_wout), sm[0]), ((d_wada, d_win, d_woa, d_wob, d_wout), sm[1]),
                       ((m_wada, m_win, m_woa, m_wob, m_wout), sm[2]), ((v_wada, v_win, v_woa, v_wob, v_wout), sm[3])):
        wada_o, win_o, woa_o, wob_o, wout_o = big
        bada_o, gn_o, bf_o, sk_o, gf_o = unpack(small)
        outs += [wada_o, bada_o, gn_o, win_o, bf_o, sk_o, woa_o, wob_o, wout_o, gf_o]
    return (loss, grad_x[None], *outs)
```

```python
import numpy as np
import jax
import jax.numpy as jnp
from jax import lax
from jax.experimental import pallas as pl
from jax.experimental.pallas import tpu as pltpu

F32 = jnp.float32
BF16 = jnp.bfloat16

D_MODEL = 1024
HEAD_DIM = 64
N_HEADS = 8
KV_GROUPS = 2
GROUP = N_HEADS // KV_GROUPS
WINDOW = 128
CHUNK = 128
AUG_W = N_HEADS * CHUNK
N_DEV = 8
IN_SHARD = 673
ADA_SHARD = 384
NORM_EPS = 1e-6
ROPE_THETA = 10000.0
Q_SCALE = HEAD_DIM ** -0.5
NEG = -1e30

ADAM_LR = 0.001
ADAM_B1 = 0.9
ADAM_B2 = 0.999
ADAM_EPS = 1e-08
ADAM_WD = 0.01
ADAM_STEP = 10
ADAM_C1 = 1.0 - ADAM_B1 ** ADAM_STEP
ADAM_C2 = 1.0 - ADAM_B2 ** ADAM_STEP

SEG = {}
_off = 0
for _name, _w in (("qa", 512), ("ka", 256), ("va", 256), ("qb", 512), ("kb", 512), ("vb", 512),
                  ("za", 512), ("zb", 512), ("ga", 1024), ("gb", 1024), ("f", 128)):
    SEG[_name] = (_off, _w)
    _off += _w
PROJ_W = _off
DPROJ_GROUPS = (("a", SEG["qa"][0], 1024), ("qb", SEG["qb"][0], 512), ("kb", SEG["kb"][0], 512),
                ("vb", SEG["vb"][0], 512), ("zg", SEG["za"][0], 3072), ("f", SEG["f"][0], 128))

VMEM_LIMIT = 56 * 1024 * 1024


def _params(n_axes):
    return pltpu.CompilerParams(dimension_semantics=("arbitrary",) * n_axes, vmem_limit_bytes=VMEM_LIMIT)


def _dot(a, b):
    return jnp.dot(a, b, preferred_element_type=F32)


def _dot_nt(a, b):
    return lax.dot_general(a, b, (((1,), (1,)), ((), ())), preferred_element_type=F32)


def _dot_tn(a, b):
    return lax.dot_general(a, b, (((0,), (0,)), ((), ())), preferred_element_type=F32)


def _lane(n):
    return lax.broadcasted_iota(jnp.int32, (1, n), 1)


def _split3(x):
    hi = x.astype(BF16).astype(F32)
    r = x - hi
    mid = r.astype(BF16).astype(F32)
    lo = (r - mid).astype(BF16).astype(F32)
    return hi, mid, lo


def _dot_split(x, b):
    hi, mid, lo = _split3(x)
    return _dot(hi.astype(BF16), b) + _dot(mid.astype(BF16), b) + _dot(lo.astype(BF16), b)


def _spread3(x, e3_ref):
    hi, mid, lo = _split3(x)
    return _dot(hi.astype(BF16), e3_ref[0]) + _dot(mid.astype(BF16), e3_ref[1]) + _dot(lo.astype(BF16), e3_ref[2])


def _place3(lane, base, x, other):
    hi, mid, lo = _split3(x)
    return jnp.where(lane == base, hi, jnp.where(lane == base + 1, mid, jnp.where(lane == base + 2, lo, other)))


def _lane_sum(x, lo, hi):
    lane = _lane(x.shape[1])
    return jnp.sum(jnp.where((lane >= lo) & (lane < hi), x, 0.0), axis=1, keepdims=True)


def _ones_lanes(lo, hi):
    lane = _lane(CHUNK)
    return jnp.where((lane >= lo) & (lane < hi), 1.0, 0.0).astype(F32)


def _rope(c, cos, sa, sb):
    return c * cos + pltpu.roll(c, CHUNK - 32, 1) * sa + pltpu.roll(c, 32, 1) * sb


def _rope_inv(d, cos, sa, sb):
    return d * cos - (pltpu.roll(d, CHUNK - 32, 1) * sa + pltpu.roll(d, 32, 1) * sb)


def _pair(c0, c1):
    return jnp.where(_lane(CHUNK) < HEAD_DIM, c0, pltpu.roll(c1, HEAD_DIM, 1))


def _sigmoid(x):
    return 0.5 * jnp.tanh(0.5 * x) + 0.5


N_CHIPS = 4
ANY_SPEC = pl.BlockSpec(memory_space=pl.ANY)


def _exchange(arrays, modes, name):
    n = len(arrays)

    def body(*refs):
        copies = _exchange_copies(refs[:n], refs[n:2 * n], modes, *refs[2 * n:])
        _exchange_start(copies)
        _exchange_wait(copies)

    return pl.pallas_call(
        body, name=name, out_shape=_exchange_out_shapes(arrays, modes),
        in_specs=[ANY_SPEC] * n, out_specs=[ANY_SPEC] * n, scratch_shapes=_exchange_sems(n),
    )(*arrays)


def _exchange_out_shapes(arrays, modes):
    return [jax.ShapeDtypeStruct((N_DEV,) + a.shape if md == "gather" else a.shape, a.dtype)
            for a, md in zip(arrays, modes)]


def _exchange_sems(n):
    return [pltpu.SemaphoreType.DMA((n, N_DEV - 1)), pltpu.SemaphoreType.DMA((n, N_DEV - 1)),
            pltpu.SemaphoreType.DMA((n,))]


def _exchange_copies(srcs, dsts, modes, send_sems, recv_sems, loc_sems):
    n = len(srcs)
    x, y, c = lax.axis_index("x"), lax.axis_index("y"), lax.axis_index("c")

    def slot(i, px, py, pc):
        return 4 * px + 2 * py + pc if modes[i] == "gather" else 2 * px + py

    def src_of(i, px, py, pc):
        return srcs[i] if modes[i] == "gather" else srcs[i].at[slot(i, px, py, pc)]

    local = [pltpu.make_async_copy(src_of(i, x, y, c), dsts[i].at[slot(i, x, y, c)], loc_sems.at[i])
             for i in range(n)]
    sends, recvs = [], []
    for r in (1, 2, 4, 3, 5, 6, 7):
        px = 1 - x if r & 4 else x
        py = 1 - y if r & 2 else y
        pc = 1 - c if r & 1 else c
        for i in range(n):
            if modes[i] == "chips" and r & 1:
                continue
            sems = dict(send_sem=send_sems.at[i, r - 1], recv_sem=recv_sems.at[i, r - 1],
                        device_id=(px, py, pc), device_id_type=pl.DeviceIdType.MESH)
            sends.append(pltpu.make_async_remote_copy(
                src_ref=src_of(i, px, py, pc), dst_ref=dsts[i].at[slot(i, x, y, c)], **sems))
            recvs.append(pltpu.make_async_remote_copy(
                src_ref=src_of(i, px, py, pc), dst_ref=dsts[i].at[slot(i, px, py, pc)], **sems))
    return local, sends, recvs


def _exchange_start(copies):
    local, sends, _ = copies
    for cp in local + sends:
        cp.start()


def _exchange_wait(copies):
    local, sends, recvs = copies
    for cp in recvs:
        cp.wait_recv()
    for cp in sends:
        cp.wait_send()
    for cp in local:
        cp.wait()


def _gather_two_level(arrays, name):
    n = len(arrays)

    def body(*refs):
        _gather2_start(refs[:n], refs[n:2 * n], *refs[2 * n:])
        _gather2_finish(refs[:n], refs[n:2 * n], *refs[2 * n:])

    return pl.pallas_call(
        body, name=name, out_shape=[jax.ShapeDtypeStruct((N_DEV,) + a.shape, a.dtype) for a in arrays],
        in_specs=[ANY_SPEC] * n, out_specs=[ANY_SPEC] * n, scratch_shapes=_exchange_sems(n),
    )(*arrays)


def _gather2_parts(srcs, dsts, send_sems, recv_sems, loc_sems):
    n = len(srcs)
    x, y, c = lax.axis_index("x"), lax.axis_index("y"), lax.axis_index("c")
    me, sibling = (x, y, c), (x, y, 1 - c)
    chips = [(1 - x, y), (x, 1 - y), (1 - x, 1 - y)]

    def rows(i, dev):
        return dsts[i].at[4 * dev[0] + 2 * dev[1] + dev[2]]

    def copy(i, k, block, to, src=None):
        return pltpu.make_async_remote_copy(
            src_ref=rows(i, block) if src is None else src, dst_ref=rows(i, block),
            send_sem=send_sems.at[i, k], recv_sem=recv_sems.at[i, k],
            device_id=to, device_id_type=pl.DeviceIdType.MESH)

    local = [pltpu.make_async_copy(srcs[i], rows(i, me), loc_sems.at[i]) for i in range(n)]
    first = []
    for i in range(n):
        first.append(copy(i, 0, me, sibling, src=srcs[i]))
        first += [copy(i, 1 + j, me, (*chip, c), src=srcs[i]) for j, chip in enumerate(chips)]
    return n, c, me, sibling, chips, copy, local, first


def _gather2_start(srcs, dsts, send_sems, recv_sems, loc_sems):
    *_, local, first = _gather2_parts(srcs, dsts, send_sems, recv_sems, loc_sems)
    for cp in local + first:
        cp.start()


def _gather2_finish(srcs, dsts, send_sems, recv_sems, loc_sems):
    n, c, me, sibling, chips, copy, local, first = _gather2_parts(srcs, dsts, send_sems, recv_sems, loc_sems)
    passed = []
    for j, chip in enumerate(chips):
        for i in range(n):
            copy(i, 1 + j, (*chip, c), me).wait_recv()
            cp = copy(i, 4 + j, (*chip, c), sibling)
            cp.start()
            passed.append(cp)
    for i in range(n):
        copy(i, 0, sibling, me).wait_recv()
        for j, chip in enumerate(chips):
            copy(i, 4 + j, (*chip, 1 - c), me).wait_recv()
    for cp in first + passed:
        cp.wait_send()
    for cp in local:
        cp.wait()


def _swap_sibling(arrays, name):
    n = len(arrays)
    out_shape = [jax.ShapeDtypeStruct(a.shape[1:], a.dtype) for a in arrays]

    def body(*refs):
        copies = _sibling_copies(refs[:n], refs[n:2 * n], *refs[2 * n:])
        for cp in copies:
            cp.start()
        for cp in copies:
            cp.wait()

    return pl.pallas_call(
        body, name=name, out_shape=out_shape, in_specs=[ANY_SPEC] * n, out_specs=[ANY_SPEC] * n,
        scratch_shapes=[pltpu.SemaphoreType.DMA((n,)), pltpu.SemaphoreType.DMA((n,))],
    )(*arrays)


def _sibling_copies(srcs, dsts, send_sems, recv_sems):
    x, y, c = lax.axis_index("x"), lax.axis_index("y"), lax.axis_index("c")
    return [pltpu.make_async_remote_copy(
        src_ref=srcs[i].at[1 - c], dst_ref=dsts[i], send_sem=send_sems.at[i], recv_sem=recv_sems.at[i],
        device_id=(x, y, 1 - c), device_id_type=pl.DeviceIdType.MESH) for i in range(len(srcs))]


def _token_matmuls_and_swap(pairs, tk, swap):
    npair, nsw = len(pairs), len(swap)
    s = pairs[0][0].shape[1]
    last = s // tk - 1

    def body(*refs):
        a_refs, b_refs = refs[0:2 * npair:2], refs[1:2 * npair:2]
        src = refs[2 * npair:2 * npair + nsw]
        o_refs = refs[2 * npair + nsw:3 * npair + nsw]
        dst = refs[3 * npair + nsw:3 * npair + 2 * nsw]
        sems = refs[3 * npair + 2 * nsw:]

        @pl.when(pl.program_id(0) == 0)
        def _():
            for cp in _sibling_copies(src, dst, *sems):
                cp.start()

        for a_ref, b_ref, o_ref in zip(a_refs, b_refs, o_refs):
            @pl.when(pl.program_id(0) == 0)
            def _():
                o_ref[...] = jnp.zeros_like(o_ref)

            o_ref[...] += _dot(a_ref[...], b_ref[...])

        @pl.when(pl.program_id(0) == last)
        def _():
            for cp in _sibling_copies(src, dst, *sems):
                cp.wait()

    in_specs, args = [], []
    for a_t, b in pairs:
        in_specs += [pl.BlockSpec((a_t.shape[0], tk), lambda k: (0, k)), pl.BlockSpec((tk, b.shape[1]), lambda k: (k, 0))]
        args += [a_t, b]
    outs = pl.pallas_call(
        body, name="dw_out_projs_swap", grid=(s // tk,),
        out_shape=[jax.ShapeDtypeStruct((a_t.shape[0], b.shape[1]), F32) for a_t, b in pairs]
        + [jax.ShapeDtypeStruct(a.shape[1:], a.dtype) for a in swap],
        in_specs=in_specs + [ANY_SPEC] * nsw,
        out_specs=[pl.BlockSpec((a_t.shape[0], b.shape[1]), lambda k: (0, 0)) for a_t, b in pairs] + [ANY_SPEC] * nsw,
        scratch_shapes=[pltpu.SemaphoreType.DMA((nsw,)), pltpu.SemaphoreType.DMA((nsw,))],
        compiler_params=_params(1))(*args, *swap)
    return outs[:npair], outs[npair:]


def _chip_partial(core, slots, theirs, tr, name):
    _, k, rws, cols = slots.shape

    def body(core_ref, a_ref, b_ref, f_ref, h_ref):
        sm = a_ref[0] + b_ref[...]
        f_ref[...] = sm
        h_ref[...] = sm.astype(BF16)

    blk = pl.BlockSpec((1, tr, cols), lambda j, i, core_ref: (j, i, 0))
    mine = pl.BlockSpec((1, 1, tr, cols), lambda j, i, core_ref: (core_ref[0], j, i, 0))
    return pl.pallas_call(
        body, name=name,
        grid_spec=pltpu.PrefetchScalarGridSpec(num_scalar_prefetch=1, grid=(k, rws // tr),
                                               in_specs=[mine, blk], out_specs=[blk, blk]),
        out_shape=[jax.ShapeDtypeStruct(theirs.shape, F32), jax.ShapeDtypeStruct(theirs.shape, BF16)],
        compiler_params=_params(2))(core, slots, theirs)


def _ada_fwd(c_all, w_shard, b_shard):
    def body(c_ref, w_ref, b_ref, o_ref):
        ch, cm, cl = [t.astype(BF16) for t in _split3(c_ref[...])]
        wh, wm, wl = [t.astype(BF16) for t in _split3(w_ref[...])]
        acc = _dot(ch, wh) + _dot(ch, wm) + _dot(cm, wh) + _dot(ch, wl) + _dot(cl, wh) + _dot(cm, wm)
        o_ref[...] = acc + b_ref[...]

    return pl.pallas_call(body, name="ada_fwd", out_shape=jax.ShapeDtypeStruct((N_DEV, ADA_SHARD), F32),
                          compiler_params=_params(0))(c_all, w_shard, b_shard)


def _rope_tables_and_gather(pos_col, inv_freq, tm, weights, c_rows):
    s = pos_col.shape[0]
    nw = len(weights)
    last = s // tm - 1

    def body(*refs):
        p_ref, f_ref = refs[:2]
        w_src, c_src = refs[2:2 + nw], refs[2 + nw:3 + nw]
        cos_ref, sa_ref, sb_ref = refs[3 + nw:6 + nw]
        w_dst, c_dst = refs[6 + nw:6 + 2 * nw], refs[6 + 2 * nw:7 + 2 * nw]
        w_sems, c_sems = refs[7 + 2 * nw:10 + 2 * nw], refs[10 + 2 * nw:]

        @pl.when(pl.program_id(0) == 0)
        def _():
            _gather2_start(w_src, w_dst, *w_sems)
            _exchange_start(_exchange_copies(c_src, c_dst, ["gather"], *c_sems))

        ang = p_ref[...] * f_ref[...]
        sin = jnp.sin(ang)
        first_half = (_lane(CHUNK) & (HEAD_DIM - 1)) < HEAD_DIM // 2
        cos_ref[...] = jnp.cos(ang)
        sa_ref[...] = jnp.where(first_half, -sin, 0.0)
        sb_ref[...] = jnp.where(first_half, 0.0, sin)

        @pl.when(pl.program_id(0) == last)
        def _():
            _exchange_wait(_exchange_copies(c_src, c_dst, ["gather"], *c_sems))
            _gather2_finish(w_src, w_dst, *w_sems)

    tab = jax.ShapeDtypeStruct((s, CHUNK), F32)
    blk = pl.BlockSpec((tm, CHUNK), lambda i: (i, 0))
    gathered = [jax.ShapeDtypeStruct((N_DEV,) + a.shape, a.dtype) for a in list(weights) + [c_rows]]
    outs = pl.pallas_call(
        body, name="rope_tables_gather", grid=(s // tm,), out_shape=[tab, tab, tab] + gathered,
        in_specs=[pl.BlockSpec((tm, 1), lambda i: (i, 0)), pl.BlockSpec((1, CHUNK), lambda i: (0, 0))]
        + [ANY_SPEC] * (nw + 1),
        out_specs=[blk, blk, blk] + [ANY_SPEC] * (nw + 1),
        scratch_shapes=_exchange_sems(nw) + _exchange_sems(1),
        compiler_params=_params(1))(pos_col, inv_freq, *weights, c_rows)
    return outs[:3], outs[3:3 + nw], outs[3 + nw]


def _fwd_proj(x, mod, w_all, cos, sa, sb, bf_row, emat, ind, tm):
    s = x.shape[0]

    def body(x_ref, mod_ref, w_ref, cos_ref, sa_ref, sb_ref, bf_ref, e_ref, ind_ref,
             h_ref, qa_ref, ka_ref, va_ref, za_ref, qb_ref, kb_ref, vb_ref, zb_ref, ga_ref, gb_ref, f_ref,
             bnd_ref, nrm_ref, carry_ref):
        @pl.when(pl.program_id(0) == 0)
        def _():
            carry_ref[...] = jnp.zeros_like(carry_ref)

        xv = x_ref[...]
        rstd = lax.rsqrt(jnp.mean(xv * xv, axis=-1, keepdims=True) + NORM_EPS)
        h = (xv * rstd * mod_ref[0:1, :]) * (1.0 + mod_ref[1:2, :]) + mod_ref[2:3, :]
        hb = h.astype(BF16)
        h_ref[...] = hb.T

        def seg(name):
            off, w = SEG[name]
            return _dot(hb, w_ref[:, off:off + w])

        lane = _lane(CHUNK)
        low = lane < HEAD_DIM
        q_ones = _ones_lanes(64, 67)
        k_ones = _ones_lanes(67, 70)
        cos_t, sa_t, sb_t = cos_ref[...], sa_ref[...], sb_ref[...]

        def write_heads(ref, nat, extra_of, rope, scale):
            for p in range(N_HEADS // 2):
                c = nat[:, CHUNK * p:CHUNK * (p + 1)]
                if rope:
                    c = _rope(c, cos_t, sa_t, sb_t)
                if scale != 1.0:
                    c = c * scale
                for hh in range(2):
                    hd = 2 * p + hh
                    src = c if hh == 0 else pltpu.roll(c, HEAD_DIM, 1)
                    ref[:, CHUNK * hd:CHUNK * (hd + 1)] = jnp.where(low, src, extra_of(hd)).astype(BF16)

        write_heads(qa_ref, seg("qa"), lambda hd: q_ones, True, Q_SCALE)
        ka = seg("ka")
        va = seg("va")
        for kv in range(KV_GROUPS):
            sl = slice(CHUNK * kv, CHUNK * (kv + 1))
            ka_ref[:, sl] = jnp.where(low, _rope(ka[:, sl], cos_t, sa_t, sb_t), k_ones).astype(BF16)
            va_ref[:, sl] = jnp.where(low, va[:, sl], q_ones).astype(BF16)
        za_ref[...] = seg("za")
        zb_ref[...] = seg("zb")
        ga_ref[...] = seg("ga")
        gb_ref[...] = seg("gb")

        xf = seg("f") + bf_ref[...]
        f_ref[...] = xf
        logf = jnp.minimum(xf, 0.0) - jnp.log1p(jnp.exp(-jnp.abs(xf)))
        row = lax.broadcasted_iota(jnp.int32, (tm, tm), 0)
        col = lax.broadcasted_iota(jnp.int32, (tm, tm), 1)
        tri = jnp.where(col <= row, 1.0, 0.0).astype(BF16)
        hi, mid, lo = _split3(logf)
        cum = _dot(tri, hi.astype(BF16)) + _dot(tri, mid.astype(BF16)) + _dot(tri, lo.astype(BF16))
        cum = cum + carry_ref[0:1, :]
        carry_ref[0:1, :] += jnp.sum(logf, axis=0, keepdims=True)
        lane_all = _lane(AUG_W) & (CHUNK - 1)
        bias = _spread3(-cum, e_ref) + jnp.where((lane_all >= 67) & (lane_all < 70), 1.0, 0.0)

        qb, kb = seg("qb"), seg("kb")
        write_heads(qb_ref, qb, lambda hd: q_ones, False, Q_SCALE)
        write_heads(kb_ref, kb, lambda hd: bias[:, CHUNK * hd:CHUNK * (hd + 1)], False, 1.0)
        write_heads(vb_ref, seg("vb"), lambda hd: q_ones, False, 1.0)

        rid = lax.broadcasted_iota(jnp.int32, (tm, CHUNK), 0)
        bnd_ref[...] = jnp.zeros_like(bnd_ref)
        bnd_ref[0, 0:1, :] = jnp.sum(jnp.where(rid == 0, cum, 0.0), axis=0, keepdims=True)
        bnd_ref[0, 1:2, :] = jnp.sum(jnp.where(rid == tm - 1, cum, 0.0), axis=0, keepdims=True)

        @pl.when(pl.program_id(0) == 0)
        def _():
            nrm_ref[...] = jnp.zeros_like(nrm_ref)

        for r_, nat in ((0, qb * Q_SCALE), (1, kb)):
            sq = _dot((nat * nat).astype(BF16), ind_ref[...])
            nrm_ref[r_:r_ + 1, :] = jnp.maximum(nrm_ref[r_:r_ + 1, :], jnp.max(sq, axis=0, keepdims=True))

    row_blk = lambda w: pl.BlockSpec((tm, w), lambda i: (i, 0))
    full = lambda a: pl.BlockSpec(a.shape, lambda i: (0,) * a.ndim)
    sds = lambda w, dt: jax.ShapeDtypeStruct((s, w), dt)
    out_shape = [sds(AUG_W, BF16), sds(KV_GROUPS * CHUNK, BF16), sds(KV_GROUPS * CHUNK, BF16),
                 sds(512, F32), sds(AUG_W, BF16), sds(AUG_W, BF16), sds(AUG_W, BF16), sds(512, F32),
                 sds(D_MODEL, F32), sds(D_MODEL, F32), sds(CHUNK, F32)]
    small = [jax.ShapeDtypeStruct((s // tm, 8, CHUNK), F32), jax.ShapeDtypeStruct((8, CHUNK), F32)]
    return pl.pallas_call(
        body, name="fwd_proj", grid=(s // tm,),
        out_shape=[jax.ShapeDtypeStruct((D_MODEL, s), BF16)] + out_shape + small,
        in_specs=[row_blk(D_MODEL), full(mod), full(w_all), row_blk(CHUNK), row_blk(CHUNK), row_blk(CHUNK),
                  full(bf_row), full(emat), full(ind)],
        out_specs=[pl.BlockSpec((D_MODEL, tm), lambda i: (0, i))] + [row_blk(o.shape[1]) for o in out_shape] + [
            pl.BlockSpec((1, 8, CHUNK), lambda i: (i, 0, 0)), pl.BlockSpec((8, CHUNK), lambda i: (0, 0))],
        scratch_shapes=[pltpu.VMEM((8, CHUNK), F32)],
        compiler_params=_params(1))(x, mod, w_all, cos, sa, sb, bf_row, emat, ind)


def _swa_fwd(q_aug, k_aug, v_aug, sink_rows, tq):
    s = q_aug.shape[0]
    r = tq // WINDOW
    gw = GROUP * CHUNK

    def body(q_ref, kc_ref, kp_ref, vc_ref, vp_ref, sink_ref, o_ref, qb_ref):
        i = pl.program_id(1)
        st = GROUP * WINDOW
        qloc = lax.broadcasted_iota(jnp.int32, (st, 2 * WINDOW), 0) & (WINDOW - 1)
        col = lax.broadcasted_iota(jnp.int32, (st, 2 * WINDOW), 1)
        band = (col > qloc) & (col <= qloc + WINDOW)
        head = jnp.right_shift(lax.broadcasted_iota(jnp.int32, (st, 1), 0), 7)
        sink = jnp.zeros((st, 1), F32)
        for g in range(GROUP):
            sink = jnp.where(head == g, jnp.max(sink_ref[0, g:g + 1, :], axis=1, keepdims=True), sink)
        lane = _lane(CHUNK)
        for sub in range(r):
            rows = slice(WINDOW * sub, WINDOW * (sub + 1))
            q = jnp.concatenate([q_ref[rows, CHUNK * g:CHUNK * (g + 1)] for g in range(GROUP)], axis=0)
            if sub == 0:
                k = jnp.concatenate([kp_ref[...], kc_ref[rows, :]], axis=0)
                v = jnp.concatenate([vp_ref[...], vc_ref[rows, :]], axis=0)
                valid = band & ((col >= WINDOW) | (i > 0))
            else:
                both = slice(WINDOW * (sub - 1), WINDOW * (sub + 1))
                k, v, valid = kc_ref[both, :], vc_ref[both, :], band
            sc = jnp.where(valid, _dot_nt(q, k), NEG)
            m = jnp.maximum(jnp.max(sc, axis=1, keepdims=True), sink)
            acc = _dot(jnp.exp(sc - m).astype(BF16), v)
            denom = _lane_sum(acc, 64, 65) + jnp.exp(sink - m)
            out = acc / denom
            aug = _place3(lane, 67, -(m + jnp.log(denom)), q.astype(F32)).astype(BF16)
            hrows = lambda a, g: a[WINDOW * g:WINDOW * (g + 1), :]
            for g in range(GROUP):
                qb_ref[rows, CHUNK * g:CHUNK * (g + 1)] = hrows(aug, g)
            for pp in range(GROUP // 2):
                o_ref[rows, CHUNK * pp:CHUNK * (pp + 1)] = _pair(hrows(out, 2 * pp), hrows(out, 2 * pp + 1))

    return pl.pallas_call(
        body, name="swa_fwd", grid=(KV_GROUPS, s // tq),
        out_shape=[jax.ShapeDtypeStruct((s, 512), F32), jax.ShapeDtypeStruct((s, AUG_W), BF16)],
        in_specs=[pl.BlockSpec((tq, gw), lambda kv, i: (i, kv)),
                  pl.BlockSpec((tq, CHUNK), lambda kv, i: (i, kv)),
                  pl.BlockSpec((WINDOW, CHUNK), lambda kv, i: (jnp.maximum(i * r - 1, 0), kv)),
                  pl.BlockSpec((tq, CHUNK), lambda kv, i: (i, kv)),
                  pl.BlockSpec((WINDOW, CHUNK), lambda kv, i: (jnp.maximum(i * r - 1, 0), kv)),
                  pl.BlockSpec((1, 8, CHUNK), lambda kv, i: (kv, 0, 0))],
        out_specs=[pl.BlockSpec((tq, GROUP * HEAD_DIM), lambda kv, i: (i, kv)),
                   pl.BlockSpec((tq, gw), lambda kv, i: (i, kv))],
        compiler_params=_params(2))(q_aug, k_aug, k_aug, v_aug, v_aug, sink_rows)


def _swa_bwd(k_aug, v_aug, q_bwd, do_aug, cos, sa, sb, sink_rows, tk):
    s = k_aug.shape[0]
    r = tk // WINDOW
    nt = s // tk
    nb = s // WINDOW
    gw = GROUP * CHUNK

    def body(k_ref, v_ref, q_ref, qn_ref, do_ref, don_ref, cos_ref, sa_ref, sb_ref, sink_ref,
             da_ref, dsink_ref, carry_ref, acc_ref):
        j = pl.program_id(0)

        @pl.when(j == 0)
        def _():
            carry_ref[...] = jnp.zeros_like(carry_ref)
            dsink_ref[...] = jnp.zeros_like(dsink_ref)

        acc_ref[...] = jnp.zeros_like(acc_ref)
        acc_ref[0:WINDOW, :] = carry_ref[...]
        dk_col, dv_col = SEG["ka"][0] - SEG["qa"][0], SEG["va"][0] - SEG["qa"][0]
        st = GROUP * WINDOW
        key = lax.broadcasted_iota(jnp.int32, (WINDOW, 2 * st), 0)
        col = lax.broadcasted_iota(jnp.int32, (WINDOW, 2 * st), 1)
        qloc = col & (WINDOW - 1)
        band = ((col < st) & (key <= qloc)) | ((col >= st) & (key > qloc))
        lane = _lane(CHUNK)
        low = lane < HEAD_DIM
        cos_t, sa_t, sb_t = cos_ref[...], sa_ref[...], sb_ref[...]
        for kv, sub in [(kv, sub) for kv in range(KV_GROUPS) for sub in range(r)]:
            heads = [slice(CHUNK * (GROUP * kv + g), CHUNK * (GROUP * kv + g + 1)) for g in range(GROUP)]
            kvs = slice(CHUNK * kv, CHUNK * (kv + 1))
            rows = slice(WINDOW * sub, WINDOW * (sub + 1))
            nxt = slice(WINDOW * (sub + 1), WINDOW * (sub + 2))
            k, v = k_ref[rows, kvs], v_ref[rows, kvs]
            q_cur, do_cur = [q_ref[rows, cs] for cs in heads], [do_ref[rows, cs] for cs in heads]
            if sub < r - 1:
                q_nxt, do_nxt, valid = [q_ref[nxt, cs] for cs in heads], [do_ref[nxt, cs] for cs in heads], band
            else:
                q_nxt, do_nxt = [qn_ref[:, cs] for cs in heads], [don_ref[:, cs] for cs in heads]
                valid = band & ((col < st) | (j < nt - 1))
            q = jnp.concatenate(q_cur + q_nxt, axis=0)
            do = jnp.concatenate(do_cur + do_nxt, axis=0)
            pt = jnp.exp(jnp.where(valid, _dot_nt(k, q), NEG))
            ds = (pt * _dot_nt(v, do)).astype(BF16)
            dv = _dot(pt.astype(BF16), do)
            dk = _dot(ds, q)
            dq = _dot_tn(ds, k)
            for g, cs in enumerate(heads):
                acc_ref[rows, cs] += dq[WINDOW * g:WINDOW * (g + 1), :]
                dqn = dq[st + WINDOW * g:st + WINDOW * (g + 1), :]
                if sub < r - 1:
                    acc_ref[nxt, cs] += dqn
                else:
                    carry_ref[:, cs] = dqn
                sink = jnp.max(sink_ref[kv, g:g + 1, :], axis=1, keepdims=True)
                p_sink = jnp.exp(sink + _lane_sum(q_cur[g].astype(F32), 67, 70))
                term = jnp.sum(p_sink * _lane_sum(do_cur[g].astype(F32), 64, 67), axis=0, keepdims=True)
                dsink_ref[kv, g:g + 1, :] += jnp.broadcast_to(term, (1, CHUNK))
            da_ref[rows, dk_col + CHUNK * kv:dk_col + CHUNK * (kv + 1)] = _rope_inv(
                jnp.where(low, dk, 0.0), cos_t[rows, :], sa_t[rows, :], sb_t[rows, :]).astype(BF16)
            da_ref[rows, dv_col + CHUNK * kv:dv_col + CHUNK * (kv + 1)] = jnp.where(low, dv, 0.0).astype(BF16)
        for pp in range(N_HEADS // 2):
            d = _pair(acc_ref[:, CHUNK * 2 * pp:CHUNK * (2 * pp + 1)], acc_ref[:, CHUNK * (2 * pp + 1):CHUNK * (2 * pp + 2)])
            da_ref[:, CHUNK * pp:CHUNK * (pp + 1)] = (_rope_inv(d, cos_t, sa_t, sb_t) * Q_SCALE).astype(BF16)

    cur = lambda w: pl.BlockSpec((tk, w), lambda j: (j, 0))
    nxt = pl.BlockSpec((WINDOW, AUG_W), lambda j: (jnp.minimum((j + 1) * r, nb - 1), 0))
    whole = pl.BlockSpec((KV_GROUPS, 8, CHUNK), lambda j: (0, 0, 0))
    return pl.pallas_call(
        body, name="swa_bwd", grid=(nt,),
        out_shape=[jax.ShapeDtypeStruct((s, 1024), BF16), jax.ShapeDtypeStruct((KV_GROUPS, 8, CHUNK), F32)],
        in_specs=[cur(KV_GROUPS * CHUNK), cur(KV_GROUPS * CHUNK), cur(AUG_W), nxt, cur(AUG_W), nxt,
                  cur(CHUNK), cur(CHUNK), cur(CHUNK), whole],
        out_specs=[cur(1024), whole],
        scratch_shapes=[pltpu.VMEM((WINDOW, AUG_W), F32), pltpu.VMEM((tk, AUG_W), F32)],
        compiler_params=_params(1))(k_aug, v_aug, q_bwd, q_bwd, do_aug, do_aug, cos, sa, sb, sink_rows)


def _fox_fwd(first_key, q_aug, k_aug, v_aug, t):
    s = q_aug.shape[0]
    pw = 2 * CHUNK

    def body(lo_ref, q_ref, k_ref, v_ref, o_ref, qb_ref):
        i = pl.program_id(1)
        first = lo_ref[pl.program_id(0), i]
        row = lax.broadcasted_iota(jnp.int32, (t, t), 0)
        col = lax.broadcasted_iota(jnp.int32, (t, t), 1)
        lane = _lane(CHUNK)
        heads = (slice(0, CHUNK), slice(CHUNK, pw))
        qs = [q_ref[:, hs] for hs in heads]

        def step(jb, carry, masked):
            rows = pl.ds(pl.multiple_of(jb * t, t), t)
            new = []
            for hh, hs in enumerate(heads):
                m, acc = carry[2 * hh], carry[2 * hh + 1]
                sc = _dot_nt(qs[hh], k_ref[rows, hs])
                if masked:
                    sc = jnp.where(col <= row, sc, NEG)
                m_new = jnp.maximum(m, jnp.max(sc, axis=1, keepdims=True))
                p = jnp.exp(sc - m_new).astype(BF16)
                new += [m_new, jnp.exp(m - m_new) * acc + _dot(p, v_ref[rows, hs])]
            return tuple(new)

        init = (jnp.full((t, 1), NEG, F32), jnp.zeros((t, CHUNK), F32)) * 2
        carry = step(i, lax.fori_loop(first, i, lambda jb, c: step(jb, c, False), init), True)
        outs = []
        for hh, hs in enumerate(heads):
            m, acc = carry[2 * hh], carry[2 * hh + 1]
            denom = _lane_sum(acc, 64, 65)
            outs.append(acc / denom)
            lse = m + jnp.log(denom)
            qb_ref[:, hs] = _place3(lane, 67, -lse, qs[hh].astype(F32)).astype(BF16)
        o_ref[...] = _pair(outs[0], outs[1])

    return pl.pallas_call(
        body, name="fox_fwd", grid=(N_HEADS // 2, s // t),
        out_shape=[jax.ShapeDtypeStruct((s, 512), F32), jax.ShapeDtypeStruct((s, AUG_W), BF16)],
        in_specs=[pl.BlockSpec(memory_space=pltpu.SMEM),
                  pl.BlockSpec((t, pw), lambda p, i: (i, p)),
                  pl.BlockSpec((s, pw), lambda p, i: (0, p)),
                  pl.BlockSpec((s, pw), lambda p, i: (0, p))],
        out_specs=[pl.BlockSpec((t, CHUNK), lambda p, i: (i, p)), pl.BlockSpec((t, pw), lambda p, i: (i, p))],
        compiler_params=_params(2))(first_key, q_aug, k_aug, v_aug)


def _fox_block_ranges(bounds, norms, r):
    cum_first = bounds[0::r, 0, :N_HEADS]
    cum_last = bounds[r - 1::r, 1, :N_HEADS]
    n = cum_first.shape[0]
    reach = 2.0 * 1.02 * jnp.sqrt(norms[0, :N_HEADS] * norms[1, :N_HEADS]) + 105.0
    blk = jnp.arange(n, dtype=jnp.int32)
    decay = cum_last[None, :, :] - cum_first[:, None, :]
    skip = (decay > reach[None, None, :]) & (blk[None, :, None] < blk[:, None, None])
    first = jnp.min(jnp.where(skip, n, blk[None, :, None]), axis=1)
    first = jnp.minimum(first[:, 0::2], first[:, 1::2]).T
    needed = (first[:, :, None] <= blk[None, None, :]) & (blk[None, :, None] >= blk[None, None, :])
    last = jnp.max(jnp.where(needed, blk[None, :, None], blk[None, None, :]), axis=1)
    return first.astype(jnp.int32), last.astype(jnp.int32)


def _fox_bwd(last_query, k_aug, v_aug, q_bwd, do_aug, t):
    s = k_aug.shape[0]
    n = s // t
    pw = 2 * CHUNK

    def body(hi_ref, k_ref, v_ref, q_ref, do_ref, dk_ref, dv_ref, dck_ref, dq_ref, dcq_ref, dq_scr):
        j = pl.program_id(1)
        last = hi_ref[pl.program_id(0), j]

        @pl.when(j == 0)
        def _():
            dq_scr[...] = jnp.zeros_like(dq_scr)

        row = lax.broadcasted_iota(jnp.int32, (t, t), 0)
        col = lax.broadcasted_iota(jnp.int32, (t, t), 1)
        lane = _lane(CHUNK)
        heads = (slice(0, CHUNK), slice(CHUNK, pw))
        ks = [k_ref[:, hs] for hs in heads]
        vs = [v_ref[:, hs] for hs in heads]

        def step(ib, carry, masked):
            rows = pl.ds(pl.multiple_of(ib * t, t), t)
            new = []
            for hh, hs in enumerate(heads):
                dv, dk = carry[2 * hh], carry[2 * hh + 1]
                q, do = q_ref[rows, hs], do_ref[rows, hs]
                st = _dot_nt(ks[hh], q)
                if masked:
                    st = jnp.where(row <= col, st, NEG)
                pt = jnp.exp(st)
                ds = (pt * _dot_nt(vs[hh], do)).astype(BF16)
                new += [dv + _dot(pt.astype(BF16), do), dk + _dot(ds, q)]
                dq_scr[rows, hs] += _dot_tn(ds, ks[hh])
            return tuple(new)

        zero = jnp.zeros((t, CHUNK), F32)
        carry = lax.fori_loop(j + 1, last + 1, lambda ib, c: step(ib, c, False), step(j, (zero,) * 4, True))
        dvs, dks = (carry[0], carry[2]), (carry[1], carry[3])
        dk_ref[...] = _pair(dks[0], dks[1]).astype(BF16)
        dv_ref[...] = _pair(dvs[0], dvs[1]).astype(BF16)
        dck_ref[0] = jnp.where(lane == 0, pltpu.roll(dks[0], 64, 1),
                               jnp.where(lane == 1, pltpu.roll(dks[1], 65, 1), 0.0))

        @pl.when(j == n - 1)
        def _():
            for ib in range(n):
                rows = slice(t * ib, t * (ib + 1))
                d0, d1 = dq_scr[rows, 0:CHUNK], dq_scr[rows, CHUNK:pw]
                dq_ref[rows, :] = (_pair(d0, d1) * Q_SCALE).astype(BF16)
                dcq_ref[0, rows, :] = jnp.where(lane == 0, pltpu.roll(d0, CHUNK - 67, 1),
                                                jnp.where(lane == 1, pltpu.roll(d1, CHUNK - 66, 1), 0.0))

    return pl.pallas_call(
        body, name="fox_bwd", grid=(N_HEADS // 2, n),
        out_shape=[jax.ShapeDtypeStruct((s, 512), BF16), jax.ShapeDtypeStruct((s, 512), BF16),
                   jax.ShapeDtypeStruct((N_HEADS // 2, s, CHUNK), F32), jax.ShapeDtypeStruct((s, 512), BF16),
                   jax.ShapeDtypeStruct((N_HEADS // 2, s, CHUNK), F32)],
        in_specs=[pl.BlockSpec(memory_space=pltpu.SMEM),
                  pl.BlockSpec((t, pw), lambda p, j: (j, p)), pl.BlockSpec((t, pw), lambda p, j: (j, p)),
                  pl.BlockSpec((s, pw), lambda p, j: (0, p)), pl.BlockSpec((s, pw), lambda p, j: (0, p))],
        out_specs=[pl.BlockSpec((t, CHUNK), lambda p, j: (j, p)), pl.BlockSpec((t, CHUNK), lambda p, j: (j, p)),
                   pl.BlockSpec((1, t, CHUNK), lambda p, j: (p, j, 0)), pl.BlockSpec((s, CHUNK), lambda p, j: (0, p)),
                   pl.BlockSpec((1, s, CHUNK), lambda p, j: (p, 0, 0))],
        scratch_shapes=[pltpu.VMEM((s, pw), F32)],
        compiler_params=_params(2))(last_query, k_aug, v_aug, q_bwd, do_aug)


def _fgate_bwd(dcq, dck, xf, tm):
    s = xf.shape[0]
    nt = s // tm

    def body(dq_ref, dc_ref, xf_ref, df_ref, dbf_ref, carry_ref):
        @pl.when(pl.program_id(0) == 0)
        def _():
            carry_ref[...] = jnp.zeros_like(carry_ref)
            dbf_ref[...] = jnp.zeros_like(dbf_ref)

        row = lax.broadcasted_iota(jnp.int32, (tm, tm), 0)
        col = lax.broadcasted_iota(jnp.int32, (tm, tm), 1)
        tri = jnp.where(col >= row, 1.0, 0.0).astype(BF16)
        dcum = dq_ref[0] - dc_ref[0]
        for p in range(1, N_HEADS // 2):
            dcum = dcum + pltpu.roll(dq_ref[p] - dc_ref[p], 2 * p, 1)
        hi, mid, lo = _split3(dcum)
        dlogf = _dot(tri, hi.astype(BF16)) + _dot(tri, mid.astype(BF16)) + _dot(tri, lo.astype(BF16))
        dlogf = dlogf + carry_ref[0:1, :]
        carry_ref[0:1, :] += jnp.sum(dcum, axis=0, keepdims=True)
        df = dlogf * _sigmoid(-xf_ref[...])
        df_ref[...] = df.astype(BF16)
        dbf_ref[0:1, :] += jnp.sum(df, axis=0, keepdims=True)

    rev = pl.BlockSpec((tm, CHUNK), lambda i: (nt - 1 - i, 0))
    rev4 = pl.BlockSpec((N_HEADS // 2, tm, CHUNK), lambda i: (0, nt - 1 - i, 0))
    return pl.pallas_call(
        body, name="fgate_bwd", grid=(nt,),
        out_shape=[jax.ShapeDtypeStruct((s, CHUNK), BF16), jax.ShapeDtypeStruct((8, CHUNK), F32)],
        in_specs=[rev4, rev4, rev], out_specs=[rev, pl.BlockSpec((8, CHUNK), lambda i: (0, 0))],
        scratch_shapes=[pltpu.VMEM((8, CHUNK), F32)],
        compiler_params=_params(1))(dcq, dck, xf)


def _post(x, tgt, att_a, att_b, za, zb, ga, gb, woa_t, wob_t, wout, vec, ind, emat, tm):
    s = x.shape[0]

    def body(x_ref, t_ref, aa_ref, ab_ref, za_ref, zb_ref, ga_ref, gb_ref, woat_ref, wobt_ref, wout_ref,
             vec_ref, ind_ref, e_ref,
             dx1_ref, ua_ref, ub_ref, mg_ref, do_ref, dya_ref, dyb_ref, doa_ref, dob_ref, dzg_ref, acc_ref):
        @pl.when(pl.program_id(0) == 0)
        def _():
            acc_ref[...] = jnp.zeros_like(acc_ref)

        gate, gfin = vec_ref[0:1, :], vec_ref[1:2, :]
        inv_d = 1.0 / D_MODEL

        def branch_fwd(att_ref, z_ref, w_ref, u_ref):
            att, z = att_ref[...], z_ref[...]
            sz = _sigmoid(z)
            silu = z * sz
            u = (att * silu).astype(BF16)
            u_ref[...] = u.T
            return att, z, sz, silu, _dot_nt(u, w_ref[...])

        att_a, z_a, sz_a, silu_a, y_a = branch_fwd(aa_ref, za_ref, woat_ref, ua_ref)
        att_b, z_b, sz_b, silu_b, y_b = branch_fwd(ab_ref, zb_ref, wobt_ref, ub_ref)
        sg_a, sg_b = _sigmoid(ga_ref[...]), _sigmoid(gb_ref[...])
        merged = (sg_a * y_a + sg_b * y_b).astype(BF16)
        mg_ref[...] = merged.T
        o = _dot(merged, wout_ref[...])
        x1 = x_ref[...] + gate * o
        rstd = lax.rsqrt(jnp.mean(x1 * x1, axis=-1, keepdims=True) + NORM_EPS)
        xh = x1 * rstd
        diff = xh * gfin - t_ref[...]
        acc_ref[2:3, :] += (0.5 * inv_d) * jnp.sum(diff * diff, axis=0, keepdims=True)
        dy = diff * inv_d
        acc_ref[1:2, :] += jnp.sum(dy * xh, axis=0, keepdims=True)
        dyg = dy * gfin
        dx1 = rstd * (dyg - xh * jnp.mean(dyg * xh, axis=-1, keepdims=True))
        dx1_ref[...] = dx1
        acc_ref[0:1, :] += jnp.sum(dx1 * o, axis=0, keepdims=True)
        d_o = (dx1 * gate).astype(BF16)
        do_ref[...] = d_o
        dmg = _dot_nt(d_o, wout_ref[...])

        lane = _lane(CHUNK)
        low = lane < HEAD_DIM
        lane_all = _lane(AUG_W) & (CHUNK - 1)

        zg0 = SEG["za"][0]

        def branch_bwd(sg, y, att, z, sz, silu, wt_ref, dy_ref, g_name, z_name, doaug_ref):
            dyb = (dmg * sg).astype(BF16)
            dy_ref[...] = dyb
            g_off, z_off = SEG[g_name][0] - zg0, SEG[z_name][0] - zg0
            dzg_ref[:, g_off:g_off + D_MODEL] = (dmg * y * sg * (1.0 - sg)).astype(BF16)
            du = _dot(dyb, wt_ref[...])
            datt = du * silu
            dzg_ref[:, z_off:z_off + 512] = (du * att * (sz * (1.0 + z * (1.0 - sz)))).astype(BF16)
            extra = _spread3(-_dot_split(datt * att, ind_ref[...]), e_ref)
            for p in range(N_HEADS // 2):
                c = datt[:, CHUNK * p:CHUNK * (p + 1)]
                for hh in range(2):
                    hd = 2 * p + hh
                    src = c if hh == 0 else pltpu.roll(c, HEAD_DIM, 1)
                    doaug_ref[:, CHUNK * hd:CHUNK * (hd + 1)] = jnp.where(
                        low, src, extra[:, CHUNK * hd:CHUNK * (hd + 1)]).astype(BF16)

        branch_bwd(sg_a, y_a, att_a, z_a, sz_a, silu_a, woat_ref, dya_ref, "ga", "za", doa_ref)
        branch_bwd(sg_b, y_b, att_b, z_b, sz_b, silu_b, wobt_ref, dyb_ref, "gb", "zb", dob_ref)

    row_blk = lambda w: pl.BlockSpec((tm, w), lambda i: (i, 0))
    full = lambda a: pl.BlockSpec(a.shape, lambda i: (0,) * a.ndim)
    sds = lambda w, dt: jax.ShapeDtypeStruct((s, w), dt)
    tds = lambda w: jax.ShapeDtypeStruct((w, s), BF16)
    col_blk = lambda w: pl.BlockSpec((w, tm), lambda i: (0, i))
    out_shape = [sds(D_MODEL, F32), tds(512), tds(512), tds(D_MODEL), sds(D_MODEL, BF16),
                 sds(D_MODEL, BF16), sds(D_MODEL, BF16), sds(AUG_W, BF16), sds(AUG_W, BF16), sds(3072, BF16),
                 jax.ShapeDtypeStruct((8, D_MODEL), F32)]
    ins = [x, tgt, att_a, att_b, za, zb, ga, gb, woa_t, wob_t, wout, vec, ind, emat]
    in_specs = [row_blk(a.shape[1]) for a in ins[:8]] + [full(a) for a in ins[8:]]
    out_specs = ([row_blk(D_MODEL), col_blk(512), col_blk(512), col_blk(D_MODEL)]
                 + [row_blk(o.shape[1]) for o in out_shape[4:-1]] + [pl.BlockSpec((8, D_MODEL), lambda i: (0, 0))])
    return pl.pallas_call(body, name="post", grid=(s // tm,), out_shape=out_shape, in_specs=in_specs,
                          out_specs=out_specs, compiler_params=_params(1))(*ins)


def _bwd_pre(dproj, w_all, x, dx1, mod, tm, chip_halves):
    s = x.shape[0]
    ng, nh = len(DPROJ_GROUPS), len(chip_halves)
    last = s // tm - 1
    modes = ["chips"] * nh

    def body(*refs):
        dp_refs = refs[:ng]
        wt_ref, x_ref, dx1_ref, mod_ref = refs[ng:ng + 4]
        send_refs = refs[ng + 4:ng + 4 + nh]
        gx_ref, acc_ref = refs[ng + 4 + nh:ng + 6 + nh]
        recv_refs = refs[ng + 6 + nh:ng + 6 + 2 * nh]
        sems = refs[ng + 6 + 2 * nh:]

        @pl.when(pl.program_id(0) == 0)
        def _():
            acc_ref[...] = jnp.zeros_like(acc_ref)
            _exchange_start(_exchange_copies(send_refs, recv_refs, modes, *sems))

        dh = None
        for dp_ref, (_, off, w) in zip(dp_refs, DPROJ_GROUPS):
            part = _dot_nt(dp_ref[...], wt_ref[:, off:off + w])
            dh = part if dh is None else dh + part
        xv = x_ref[...]
        rstd = lax.rsqrt(jnp.mean(xv * xv, axis=-1, keepdims=True) + NORM_EPS)
        xh = xv * rstd
        gn = mod_ref[0:1, :]
        acc_ref[0:1, :] += jnp.sum(dh, axis=0, keepdims=True)
        acc_ref[1:2, :] += jnp.sum(dh * (xh * gn), axis=0, keepdims=True)
        dn = dh * (1.0 + mod_ref[1:2, :])
        acc_ref[2:3, :] += jnp.sum(dn * xh, axis=0, keepdims=True)
        dxh = dn * gn
        gx_ref[...] = dx1_ref[...] + rstd * (dxh - xh * jnp.mean(dxh * xh, axis=-1, keepdims=True))

        @pl.when(pl.program_id(0) == last)
        def _():
            _exchange_wait(_exchange_copies(send_refs, recv_refs, modes, *sems))

    row_blk = lambda w: pl.BlockSpec((tm, w), lambda i: (i, 0))
    full = lambda a: pl.BlockSpec(a.shape, lambda i: (0,) * a.ndim)
    outs = pl.pallas_call(
        body, name="bwd_pre", grid=(s // tm,),
        out_shape=[jax.ShapeDtypeStruct((s, D_MODEL), F32), jax.ShapeDtypeStruct((8, D_MODEL), F32)]
        + _exchange_out_shapes(chip_halves, modes),
        in_specs=[row_blk(w) for _, _, w in DPROJ_GROUPS] + [full(w_all), row_blk(D_MODEL), row_blk(D_MODEL), full(mod)]
        + [ANY_SPEC] * nh,
        out_specs=[row_blk(D_MODEL), pl.BlockSpec((8, D_MODEL), lambda i: (0, 0))] + [ANY_SPEC] * nh,
        scratch_shapes=_exchange_sems(nh),
        compiler_params=_params(1))(*dproj, w_all, x, dx1, mod, *chip_halves)
    return outs[0], outs[1], outs[2:]


def _token_matmul(a_t, b, tn, tk, name):
    m, s = a_t.shape
    n = b.shape[1]

    def body(a_ref, b_ref, o_ref):
        @pl.when(pl.program_id(1) == 0)
        def _():
            o_ref[...] = jnp.zeros_like(o_ref)

        o_ref[...] += _dot(a_ref[...], b_ref[...])

    return pl.pallas_call(
        body, name=name, grid=(n // tn, s // tk), out_shape=jax.ShapeDtypeStruct((m, n), F32),
        in_specs=[pl.BlockSpec((m, tk), lambda jn, k: (0, k)), pl.BlockSpec((tk, tn), lambda jn, k: (k, jn))],
        out_specs=pl.BlockSpec((m, tn), lambda jn, k: (0, jn)),
        compiler_params=_params(2))(a_t, b)


def _token_matmul_multi(a_t, bs, tk, name):
    m, s = a_t.shape
    nb = len(bs)

    def body(*refs):
        a_ref, b_refs, o_refs = refs[0], refs[1:1 + nb], refs[1 + nb:]
        a = a_ref[...]
        for b_ref, o_ref in zip(b_refs, o_refs):
            @pl.when(pl.program_id(0) == 0)
            def _():
                o_ref[...] = jnp.zeros_like(o_ref)

            o_ref[...] += _dot(a, b_ref[...])

    return pl.pallas_call(
        body, name=name, grid=(s // tk,),
        out_shape=[jax.ShapeDtypeStruct((m, b.shape[1]), F32) for b in bs],
        in_specs=[pl.BlockSpec((m, tk), lambda k: (0, k))] + [pl.BlockSpec((tk, b.shape[1]), lambda k: (k, 0)) for b in bs],
        out_specs=[pl.BlockSpec((m, b.shape[1]), lambda k: (0, 0)) for b in bs],
        compiler_params=_params(1))(a_t, *bs)


def _adam_math(g, w, m, v):
    m2 = ADAM_B1 * m + (1.0 - ADAM_B1) * g
    v2 = ADAM_B2 * v + (1.0 - ADAM_B2) * (g * g)
    delta = -ADAM_LR * ((m2 / ADAM_C1) / (jnp.sqrt(v2 / ADAM_C2) + ADAM_EPS) + ADAM_WD * w)
    return delta, m2, v2


def _adamw_parts(full, parts, others, w, m, v, tr, name):
    _, rws, cols = w.shape

    def body(oth_ref, f_ref, p_ref, w_ref, m_ref, v_ref, g_ref, d_ref, m2_ref, v2_ref):
        g = None
        for chip in range(N_CHIPS):
            part = (1.0 - oth_ref[chip]) * f_ref[chip] + oth_ref[chip] * p_ref[chip].astype(F32)
            g = part if g is None else g + part
        g_ref[0] = g
        d_ref[0], m2_ref[0], v2_ref[0] = _adam_math(g, w_ref[0], m_ref[0], v_ref[0])

    blk = pl.BlockSpec((1, tr, cols), lambda i: (0, i, 0))
    slots = pl.BlockSpec((N_CHIPS, tr, cols), lambda i: (0, i, 0))
    o = jax.ShapeDtypeStruct((1, rws, cols), F32)
    return pl.pallas_call(
        body, name=name, grid=(rws // tr,), out_shape=[o, o, o, o],
        in_specs=[pl.BlockSpec(memory_space=pltpu.SMEM), slots, slots, blk, blk, blk],
        out_specs=[blk, blk, blk, blk], compiler_params=_params(1))(others, full, parts, w, m, v)


def _adamw_ada(c_lanes, d_rows, w, m, v, tr):
    _, rws, cols = w.shape

    def body(c_ref, d_ref, w_ref, m_ref, v_ref, g_ref, dl_ref, m2_ref, v2_ref):
        for k in range(cols // CHUNK):
            cs = slice(CHUNK * k, CHUNK * (k + 1))
            g = c_ref[0] * d_ref[0:1, cs]
            for b in range(1, N_DEV):
                g = g + c_ref[b] * d_ref[b:b + 1, cs]
            g_ref[0, :, cs] = g
            dl_ref[0, :, cs], m2_ref[0, :, cs], v2_ref[0, :, cs] = _adam_math(
                g, w_ref[0, :, cs], m_ref[0, :, cs], v_ref[0, :, cs])

    blk = pl.BlockSpec((1, tr, cols), lambda i: (0, i, 0))
    o = jax.ShapeDtypeStruct((1, rws, cols), F32)
    return pl.pallas_call(
        body, name="adamw_ada", grid=(rws // tr,), out_shape=[o, o, o, o],
        in_specs=[pl.BlockSpec((N_DEV, tr, CHUNK), lambda i: (0, i, 0)), pl.BlockSpec((N_DEV, cols), lambda i: (0, 0)),
                  blk, blk, blk],
        out_specs=[blk, blk, blk, blk], compiler_params=_params(1))(c_lanes, d_rows, w, m, v)


def _adamw_small(packs, w, m, v):
    def body(p_ref, w_ref, m_ref, v_ref, g_ref, d_ref, m2_ref, v2_ref, loss_ref):
        g = p_ref[0]
        for dev in range(1, N_DEV):
            g = g + p_ref[dev]
        g_ref[...] = g
        d_ref[...], m2_ref[...], v2_ref[...] = _adam_math(g, w_ref[...], m_ref[...], v_ref[...])
        loss_ref[...] = jnp.broadcast_to(jnp.sum(g_ref[2:3, 0:D_MODEL], axis=1, keepdims=True), loss_ref.shape)

    o = jax.ShapeDtypeStruct(w.shape, F32)
    return pl.pallas_call(body, name="adamw_small", out_shape=[o, o, o, o, jax.ShapeDtypeStruct((8, CHUNK), F32)],
                          compiler_params=_params(0))(packs, w, m, v)


def _tile(s, want):
    return min(s, want)


def _layout_pieces():
    orig = {"qa": 0, "ka": 512, "va": 640, "za": 768, "qb": 1280, "kb": 1792, "vb": 2304, "f": 2816, "zb": 2824,
            "ga": 3336, "gb": 4360}
    pieces = []
    for name, (off, w) in SEG.items():
        if name in ("ka", "va"):
            pieces += [(orig[name] + HEAD_DIM * kv, orig[name] + HEAD_DIM * (kv + 1), off + CHUNK * kv) for kv in range(KV_GROUPS)]
        else:
            pieces.append((orig[name], orig[name] + (N_HEADS if name == "f" else w), off))
    return pieces


def _assemble_w_all(win_g):
    cols, pos = [], 0
    for lo, hi, new in sorted(_layout_pieces(), key=lambda t: t[2]):
        if new > pos:
            cols.append(jnp.zeros((D_MODEL, new - pos), win_g.dtype))
        col = lo
        while col < hi:
            dev = col // IN_SHARD
            end = min(hi, (dev + 1) * IN_SHARD)
            cols.append(win_g[dev, :, col - dev * IN_SHARD:end - dev * IN_SHARD])
            col = end
        pos = new + hi - lo
    cols.append(jnp.zeros((D_MODEL, PROJ_W - pos), win_g.dtype))
    return jnp.concatenate(cols, axis=1)


def _grad_slot(dw_groups, dev):
    lo_d, hi_d = dev * IN_SHARD, (dev + 1) * IN_SHARD
    cols = []
    for lo, hi, new in sorted(_layout_pieces()):
        a, b = max(lo, lo_d), min(hi, hi_d)
        if a < b:
            arr, off = next((g, o) for g, (_, o, w) in zip(dw_groups, DPROJ_GROUPS) if o <= new < o + w)
            cols.append(arr[:, new - off + a - lo:new - off + b - lo])
    return jnp.concatenate(cols, axis=1)


def kernel(x, c, positions, w_ada, b_ada, g_norm, w_in, b_f, sinks, w_o_swa, w_o_fox, w_out, g_final, loss_target, m_w_ada, m_b_ada, m_g_norm, m_w_in, m_b_f, m_sinks, m_w_o_swa, m_w_o_fox, m_w_out, m_g_final, v_w_ada, v_b_ada, v_g_norm, v_w_in, v_b_f, v_sinks, v_w_o_swa, v_w_o_fox, v_w_out, v_g_final):
    s = x.shape[1]
    tm = _tile(s, 256)
    ta = _tile(s, 512)
    me = 4 * lax.axis_index("x") + 2 * lax.axis_index("y") + lax.axis_index("c")
    x2, tgt = x[0], loss_target[0]

    inv_freq = np.power(np.float32(ROPE_THETA), -np.arange(0, HEAD_DIM, 2, dtype=np.float32) / HEAD_DIM)
    inv_freq = jnp.asarray(np.tile(inv_freq, CHUNK // (HEAD_DIM // 2))[None, :], F32)
    (cos, sa, sb), (win_g, woa_g, wob_g, wout_g), c_all = _rope_tables_and_gather(
        positions.reshape(s, 1).astype(F32), inv_freq, tm,
        [w_in[0].astype(BF16), w_o_swa[0].T.astype(BF16), w_o_fox[0].T.astype(BF16), w_out[0].astype(BF16)],
        jnp.broadcast_to(c, (8, D_MODEL)))
    c_all = c_all[:, 0, :]
    b_shard = lax.dynamic_slice(b_ada, (0, me * ADA_SHARD), (1, ADA_SHARD))
    ada_part = _ada_fwd(c_all, w_ada[0], b_shard)
    (ada_g,) = _exchange([ada_part], ["gather"], "gather_ada")
    ada = lax.dynamic_index_in_dim(ada_g, me, axis=1, keepdims=False).reshape(3 * D_MODEL)
    shift, scale, gate = ada[:D_MODEL], ada[D_MODEL:2 * D_MODEL], ada[2 * D_MODEL:]
    woa_t = woa_g.reshape(D_MODEL, 512)
    wob_t = wob_g.reshape(D_MODEL, 512)
    wout = wout_g.reshape(D_MODEL, D_MODEL)
    w_all = _assemble_w_all(win_g)

    zrow = jnp.zeros((1, D_MODEL), F32)
    mod = jnp.concatenate([g_norm, scale[None], shift[None], zrow, zrow, zrow, zrow, zrow], axis=0)
    bf_row = jnp.pad(b_f, ((0, 0), (0, CHUNK - N_HEADS)))
    emat_np = np.zeros((3, CHUNK, AUG_W), np.float32)
    ind_np = np.zeros((512, CHUNK), np.float32)
    for hd in range(N_HEADS):
        for part in range(3):
            emat_np[part, hd, CHUNK * hd + 64 + part] = 1.0
        ind_np[HEAD_DIM * hd:HEAD_DIM * (hd + 1), hd] = 1.0
    emat, ind = jnp.asarray(emat_np, BF16), jnp.asarray(ind_np, BF16)
    sink_rows = jnp.broadcast_to(jnp.pad(sinks.reshape(KV_GROUPS, GROUP), ((0, 0), (0, 8 - GROUP)))[:, :, None],
                                 (KV_GROUPS, 8, CHUNK))

    (h_t, qa, ka, va, za, qb, kb, vb, zb, ga, gb, xf, bounds, norms) = _fwd_proj(
        x2, mod, w_all, cos, sa, sb, bf_row, emat, ind, tm)
    first_key, last_query = _fox_block_ranges(bounds, norms, ta // tm)
    att_a, qa_bwd = _swa_fwd(qa, ka, va, sink_rows, ta)
    att_b, qb_bwd = _fox_fwd(first_key, qb, kb, vb, ta)
    vec = jnp.concatenate([gate[None], g_final[None], zrow, zrow, zrow, zrow, zrow, zrow], axis=0)
    (dx1, ua_t, ub_t, merged_t, d_o, dya, dyb, doa, dob, dzg, post_acc) = _post(
        x2, tgt, att_a, att_b, za, zb, ga, gb, woa_t, wob_t, wout, vec, ind, emat, tm)

    da, dsink = _swa_bwd(ka, va, qa_bwd, doa, cos, sa, sb, sink_rows, ta)
    dkb, dvb, dck, dqb, dcq = _fox_bwd(last_query, kb, vb, qb_bwd, dob, ta)
    dfb, dbf_acc = _fgate_bwd(dcq, dck, xf, ta)
    dproj = [da, dqb, dkb, dvb, dzg, dfb]
    dw_zg, = _token_matmul_multi(h_t, [dzg], ta, "dw_in_zg")
    dw_a, dw_qb, dw_kb, dw_vb, dw_f = _token_matmul_multi(h_t, [da, dqb, dkb, dvb, dfb], ta, "dw_in_rest")
    dw_all = [dw_a, dw_qb, dw_kb, dw_vb, dw_zg, dw_f]

    core, chip = lax.axis_index("c"), 2 * lax.axis_index("x") + lax.axis_index("y")
    col_slots = lambda g, r, w: g.reshape(r, N_CHIPS, 2, w).transpose(2, 1, 0, 3)
    win_slots = jnp.stack([jnp.stack([_grad_slot(dw_all, 2 * ch + co) for ch in range(N_CHIPS)]) for co in range(2)])
    (dwoa, dwob, dwout), (win_theirs,) = _token_matmuls_and_swap(
        [(ua_t, dya), (ub_t, dyb), (merged_t, d_o)], ta, [win_slots])
    slots = [win_slots, col_slots(dwoa, 512, 128), col_slots(dwob, 512, 128),
             dwout.reshape(N_CHIPS, 2, 128, D_MODEL).transpose(1, 0, 2, 3)]
    theirs = [win_theirs] + list(_swap_sibling(slots[1:], "grads_sibling"))
    fulls, halves = [], []
    core_arr = jnp.reshape(core, (1,)).astype(jnp.int32)
    for sl, th, nm in zip(slots, theirs, ("w_in", "w_o_swa", "w_o_fox", "w_out")):
        full, half = _chip_partial(core_arr, sl, th, min(th.shape[1], 256), "chip_partial_" + nm)
        fulls.append(full)
        halves.append(half)
    grad_x, pre_acc, (p_win, p_woa, p_wob, p_wout) = _bwd_pre(dproj, w_all, x2, dx1, mod, tm, halves)
    misc = jnp.concatenate([dbf_acc[0, :N_HEADS], dsink[:, :GROUP, 0].reshape(N_HEADS), jnp.zeros((D_MODEL - 16,), F32)])
    zpad = jnp.zeros((2 * D_MODEL,), F32)
    pack = jnp.stack([jnp.concatenate([pre_acc[0], pre_acc[1], post_acc[0]]),
                      jnp.concatenate([pre_acc[2], post_acc[1], misc]),
                      jnp.concatenate([post_acc[2], zpad])] + [jnp.zeros((3 * D_MODEL,), F32)] * 5)
    (packs,) = _exchange([pack], ["gather"], "gather_small_grads")
    others = jnp.where(jnp.arange(N_CHIPS) == chip, 0.0, 1.0).astype(F32)

    g_win, d_win, m_win, v_win = _adamw_parts(fulls[0], p_win, others, w_in, m_w_in, v_w_in, 128, "adamw_w_in")
    g_woa, d_woa, m_woa, v_woa = _adamw_parts(fulls[1], p_woa, others, w_o_swa, m_w_o_swa, v_w_o_swa, 128, "adamw_w_o_swa")
    g_wob, d_wob, m_wob, v_wob = _adamw_parts(fulls[2], p_wob, others, w_o_fox, m_w_o_fox, v_w_o_fox, 128, "adamw_w_o_fox")
    g_wout, d_wout, m_wout, v_wout = _adamw_parts(fulls[3], p_wout, others, w_out, m_w_out, v_w_out, 128, "adamw_w_out")
    d_ada_rows = lax.dynamic_slice(packs[:, 0, :], (0, me * ADA_SHARD), (N_DEV, ADA_SHARD))
    c_lanes = jnp.broadcast_to(c_all[:, :, None], (N_DEV, D_MODEL, CHUNK))
    g_wada, d_wada, m_wada, v_wada = _adamw_ada(c_lanes, d_ada_rows, w_ada, m_w_ada, v_w_ada, 256)

    def small_pack(bada, gn, gf, bfv, sk):
        misc_w = jnp.concatenate([bfv[0], sk[0], jnp.zeros((D_MODEL - 16,), F32)])
        return jnp.stack([bada[0], jnp.concatenate([gn[0], gf, misc_w])] + [jnp.zeros((3 * D_MODEL,), F32)] * 6)

    sm = _adamw_small(packs, small_pack(b_ada, g_norm, g_final, b_f, sinks),
                      small_pack(m_b_ada, m_g_norm, m_g_final, m_b_f, m_sinks),
                      small_pack(v_b_ada, v_g_norm, v_g_final, v_b_f, v_sinks))
    loss = sm[4][0, 0]

    def unpack(p):
        return (p[0][None], p[1, :D_MODEL][None], p[1, 2 * D_MODEL:2 * D_MODEL + 8][None],
                p[1, 2 * D_MODEL + 8:2 * D_MODEL + 16][None], p[1, D_MODEL:2 * D_MODEL])

    outs = []
    for big, small in (((g_wada, g_win, g_woa, g_wob, g_wout), sm[0]), ((d_wada, d_win, d_woa, d_wob, d_wout), sm[1]),
                       ((m_wada, m_win, m_woa, m_wob, m_wout), sm[2]), ((v_wada, v_win, v_woa, v_wob, v_wout), sm[3])):
        wada_o, win_o, woa_o, wob_o, wout_o = big
        bada_o, gn_o, bf_o, sk_o, gf_o = unpack(small)
        outs += [wada_o, bada_o, gn_o, win_o, bf_o, sk_o, woa_o, wob_o, wout_o, gf_o]
    return (loss, grad_x[None], *outs)
```

```python
import numpy as np
import jax
import jax.numpy as jnp
from jax import lax
from jax.experimental import pallas as pl
from jax.experimental.pallas import tpu as pltpu

F32 = jnp.float32
BF16 = jnp.bfloat16

D_MODEL = 1024
HEAD_DIM = 64
N_HEADS = 8
KV_GROUPS = 2
GROUP = N_HEADS // KV_GROUPS
WINDOW = 128
CHUNK = 128
AUG_W = N_HEADS * CHUNK
N_DEV = 8
IN_SHARD = 673
ADA_SHARD = 384
NORM_EPS = 1e-6
ROPE_THETA = 10000.0
Q_SCALE = HEAD_DIM ** -0.5
NEG = -1e30

ADAM_LR = 0.001
ADAM_B1 = 0.9
ADAM_B2 = 0.999
ADAM_EPS = 1e-08
ADAM_WD = 0.01
ADAM_STEP = 10
ADAM_C1 = 1.0 - ADAM_B1 ** ADAM_STEP
ADAM_C2 = 1.0 - ADAM_B2 ** ADAM_STEP

SEG = {}
_off = 0
for _name, _w in (("qa", 512), ("ka", 256), ("va", 256), ("qb", 512), ("kb", 512), ("vb", 512),
                  ("za", 512), ("zb", 512), ("ga", 1024), ("gb", 1024), ("f", 128)):
    SEG[_name] = (_off, _w)
    _off += _w
PROJ_W = _off
DPROJ_GROUPS = (("a", SEG["qa"][0], 1024), ("qb", SEG["qb"][0], 512), ("kb", SEG["kb"][0], 512),
                ("vb", SEG["vb"][0], 512), ("zg", SEG["za"][0], 3072), ("f", SEG["f"][0], 128))

VMEM_LIMIT = 56 * 1024 * 1024


def _params(n_axes):
    return pltpu.CompilerParams(dimension_semantics=("arbitrary",) * n_axes, vmem_limit_bytes=VMEM_LIMIT)


def _dot(a, b):
    return jnp.dot(a, b, preferred_element_type=F32)


def _dot_nt(a, b):
    return lax.dot_general(a, b, (((1,), (1,)), ((), ())), preferred_element_type=F32)


def _dot_tn(a, b):
    return lax.dot_general(a, b, (((0,), (0,)), ((), ())), preferred_element_type=F32)


def _lane(n):
    return lax.broadcasted_iota(jnp.int32, (1, n), 1)


def _split3(x):
    hi = x.astype(BF16).astype(F32)
    r = x - hi
    mid = r.astype(BF16).astype(F32)
    lo = (r - mid).astype(BF16).astype(F32)
    return hi, mid, lo


def _dot_split(x, b):
    hi, mid, lo = _split3(x)
    return _dot(hi.astype(BF16), b) + _dot(mid.astype(BF16), b) + _dot(lo.astype(BF16), b)


def _spread3(x, e3_ref):
    hi, mid, lo = _split3(x)
    return _dot(hi.astype(BF16), e3_ref[0]) + _dot(mid.astype(BF16), e3_ref[1]) + _dot(lo.astype(BF16), e3_ref[2])


def _place3(lane, base, x, other):
    hi, mid, lo = _split3(x)
    return jnp.where(lane == base, hi, jnp.where(lane == base + 1, mid, jnp.where(lane == base + 2, lo, other)))


def _lane_sum(x, lo, hi):
    lane = _lane(x.shape[1])
    return jnp.sum(jnp.where((lane >= lo) & (lane < hi), x, 0.0), axis=1, keepdims=True)


def _ones_lanes(lo, hi):
    lane = _lane(CHUNK)
    return jnp.where((lane >= lo) & (lane < hi), 1.0, 0.0).astype(F32)


def _rope(c, cos, sa, sb):
    return c * cos + pltpu.roll(c, CHUNK - 32, 1) * sa + pltpu.roll(c, 32, 1) * sb


def _rope_inv(d, cos, sa, sb):
    return d * cos - (pltpu.roll(d, CHUNK - 32, 1) * sa + pltpu.roll(d, 32, 1) * sb)


def _pair(c0, c1):
    return jnp.where(_lane(CHUNK) < HEAD_DIM, c0, pltpu.roll(c1, HEAD_DIM, 1))


def _sigmoid(x):
    return 0.5 * jnp.tanh(0.5 * x) + 0.5


N_CHIPS = 4
ANY_SPEC = pl.BlockSpec(memory_space=pl.ANY)


def _exchange(arrays, modes, name):
    n = len(arrays)

    def body(*refs):
        copies = _exchange_copies(refs[:n], refs[n:2 * n], modes, *refs[2 * n:])
        _exchange_start(copies)
        _exchange_wait(copies)

    return pl.pallas_call(
        body, name=name, out_shape=_exchange_out_shapes(arrays, modes),
        in_specs=[ANY_SPEC] * n, out_specs=[ANY_SPEC] * n, scratch_shapes=_exchange_sems(n),
    )(*arrays)


def _exchange_out_shapes(arrays, modes):
    return [jax.ShapeDtypeStruct((N_DEV,) + a.shape if md == "gather" else a.shape, a.dtype)
            for a, md in zip(arrays, modes)]


def _exchange_sems(n):
    return [pltpu.SemaphoreType.DMA((n, N_DEV - 1)), pltpu.SemaphoreType.DMA((n, N_DEV - 1)),
            pltpu.SemaphoreType.DMA((n,))]


def _exchange_copies(srcs, dsts, modes, send_sems, recv_sems, loc_sems):
    n = len(srcs)
    x, y, c = lax.axis_index("x"), lax.axis_index("y"), lax.axis_index("c")

    def slot(i, px, py, pc):
        return 4 * px + 2 * py + pc if modes[i] == "gather" else 2 * px + py

    def src_of(i, px, py, pc):
        return srcs[i] if modes[i] == "gather" else srcs[i].at[slot(i, px, py, pc)]

    local = [pltpu.make_async_copy(src_of(i, x, y, c), dsts[i].at[slot(i, x, y, c)], loc_sems.at[i])
             for i in range(n)]
    sends, recvs = [], []
    for r in (1, 2, 4, 3, 5, 6, 7):
        px = 1 - x if r & 4 else x
        py = 1 - y if r & 2 else y
        pc = 1 - c if r & 1 else c
        for i in range(n):
            if modes[i] == "chips" and r & 1:
                continue
            sems = dict(send_sem=send_sems.at[i, r - 1], recv_sem=recv_sems.at[i, r - 1],
                        device_id=(px, py, pc), device_id_type=pl.DeviceIdType.MESH)
            sends.append(pltpu.make_async_remote_copy(
                src_ref=src_of(i, px, py, pc), dst_ref=dsts[i].at[slot(i, x, y, c)], **sems))
            recvs.append(pltpu.make_async_remote_copy(
                src_ref=src_of(i, px, py, pc), dst_ref=dsts[i].at[slot(i, px, py, pc)], **sems))
    return local, sends, recvs


def _exchange_start(copies):
    local, sends, _ = copies
    for cp in local + sends:
        cp.start()


def _exchange_wait(copies):
    local, sends, recvs = copies
    for cp in recvs:
        cp.wait_recv()
    for cp in sends:
        cp.wait_send()
    for cp in local:
        cp.wait()


def _gather_two_level(arrays, name):
    n = len(arrays)

    def body(*refs):
        _gather2_start(refs[:n], refs[n:2 * n], *refs[2 * n:])
        _gather2_finish(refs[:n], refs[n:2 * n], *refs[2 * n:])

    return pl.pallas_call(
        body, name=name, out_shape=[jax.ShapeDtypeStruct((N_DEV,) + a.shape, a.dtype) for a in arrays],
        in_specs=[ANY_SPEC] * n, out_specs=[ANY_SPEC] * n, scratch_shapes=_exchange_sems(n),
    )(*arrays)


def _gather2_parts(srcs, dsts, send_sems, recv_sems, loc_sems):
    n = len(srcs)
    x, y, c = lax.axis_index("x"), lax.axis_index("y"), lax.axis_index("c")
    me, sibling = (x, y, c), (x, y, 1 - c)
    chips = [(1 - x, y), (x, 1 - y), (1 - x, 1 - y)]

    def rows(i, dev):
        return dsts[i].at[4 * dev[0] + 2 * dev[1] + dev[2]]

    def copy(i, k, block, to, src=None):
        return pltpu.make_async_remote_copy(
            src_ref=rows(i, block) if src is None else src, dst_ref=rows(i, block),
            send_sem=send_sems.at[i, k], recv_sem=recv_sems.at[i, k],
            device_id=to, device_id_type=pl.DeviceIdType.MESH)

    local = [pltpu.make_async_copy(srcs[i], rows(i, me), loc_sems.at[i]) for i in range(n)]
    first = []
    for i in range(n):
        first.append(copy(i, 0, me, sibling, src=srcs[i]))
        first += [copy(i, 1 + j, me, (*chip, c), src=srcs[i]) for j, chip in enumerate(chips)]
    return n, c, me, sibling, chips, copy, local, first


def _gather2_start(srcs, dsts, send_sems, recv_sems, loc_sems):
    *_, local, first = _gather2_parts(srcs, dsts, send_sems, recv_sems, loc_sems)
    for cp in local + first:
        cp.start()


def _gather2_finish(srcs, dsts, send_sems, recv_sems, loc_sems):
    n, c, me, sibling, chips, copy, local, first = _gather2_parts(srcs, dsts, send_sems, recv_sems, loc_sems)
    passed = []
    for j, chip in enumerate(chips):
        for i in range(n):
            copy(i, 1 + j, (*chip, c), me).wait_recv()
            cp = copy(i, 4 + j, (*chip, c), sibling)
            cp.start()
            passed.append(cp)
    for i in range(n):
        copy(i, 0, sibling, me).wait_recv()
        for j, chip in enumerate(chips):
            copy(i, 4 + j, (*chip, 1 - c), me).wait_recv()
    for cp in first + passed:
        cp.wait_send()
    for cp in local:
        cp.wait()


def _swap_sibling(arrays, name):
    n = len(arrays)
    out_shape = [jax.ShapeDtypeStruct(a.shape[1:], a.dtype) for a in arrays]

    def body(*refs):
        copies = _sibling_copies(refs[:n], refs[n:2 * n], *refs[2 * n:])
        for cp in copies:
            cp.start()
        for cp in copies:
            cp.wait()

    return pl.pallas_call(
        body, name=name, out_shape=out_shape, in_specs=[ANY_SPEC] * n, out_specs=[ANY_SPEC] * n,
        scratch_shapes=[pltpu.SemaphoreType.DMA((n, N_CHIPS)), pltpu.SemaphoreType.DMA((n, N_CHIPS))],
    )(*arrays)


def _sibling_copies(srcs, dsts, send_sems, recv_sems):
    x, y, c = lax.axis_index("x"), lax.axis_index("y"), lax.axis_index("c")
    return [pltpu.make_async_remote_copy(
        src_ref=srcs[i].at[1 - c, k], dst_ref=dsts[i].at[k], send_sem=send_sems.at[i, k], recv_sem=recv_sems.at[i, k],
        device_id=(x, y, 1 - c), device_id_type=pl.DeviceIdType.MESH)
        for i in range(len(srcs)) for k in range(N_CHIPS)]


def _token_matmuls_and_swap(pairs, tk, swap):
    npair, nsw = len(pairs), len(swap)
    s = pairs[0][0].shape[1]
    last = s // tk - 1

    def body(*refs):
        a_refs, b_refs = refs[0:2 * npair:2], refs[1:2 * npair:2]
        src = refs[2 * npair:2 * npair + nsw]
        o_refs = refs[2 * npair + nsw:3 * npair + nsw]
        dst = refs[3 * npair + nsw:3 * npair + 2 * nsw]
        sems = refs[3 * npair + 2 * nsw:]

        @pl.when(pl.program_id(0) == 0)
        def _():
            for cp in _sibling_copies(src, dst, *sems):
                cp.start()

        for a_ref, b_ref, o_ref in zip(a_refs, b_refs, o_refs):
            @pl.when(pl.program_id(0) == 0)
            def _():
                o_ref[...] = jnp.zeros_like(o_ref)

            o_ref[...] += _dot(a_ref[...], b_ref[...])

        @pl.when(pl.program_id(0) == last)
        def _():
            for cp in _sibling_copies(src, dst, *sems):
                cp.wait()

    in_specs, args = [], []
    for a_t, b in pairs:
        in_specs += [pl.BlockSpec((a_t.shape[0], tk), lambda k: (0, k)), pl.BlockSpec((tk, b.shape[1]), lambda k: (k, 0))]
        args += [a_t, b]
    outs = pl.pallas_call(
        body, name="dw_out_projs_swap", grid=(s // tk,),
        out_shape=[jax.ShapeDtypeStruct((a_t.shape[0], b.shape[1]), F32) for a_t, b in pairs]
        + [jax.ShapeDtypeStruct(a.shape[1:], a.dtype) for a in swap],
        in_specs=in_specs + [ANY_SPEC] * nsw,
        out_specs=[pl.BlockSpec((a_t.shape[0], b.shape[1]), lambda k: (0, 0)) for a_t, b in pairs] + [ANY_SPEC] * nsw,
        scratch_shapes=[pltpu.SemaphoreType.DMA((nsw, N_CHIPS)), pltpu.SemaphoreType.DMA((nsw, N_CHIPS))],
        compiler_params=_params(1))(*args, *swap)
    return outs[:npair], outs[npair:]


def _chip_partial(core, slots, theirs, tr, name):
    _, k, rws, cols = slots.shape

    def body(core_ref, a_ref, b_ref, f_ref, h_ref):
        sm = a_ref[0] + b_ref[...]
        f_ref[...] = sm
        h_ref[...] = sm.astype(BF16)

    blk = pl.BlockSpec((1, tr, cols), lambda j, i, core_ref: (j, i, 0))
    mine = pl.BlockSpec((1, 1, tr, cols), lambda j, i, core_ref: (core_ref[0], j, i, 0))
    return pl.pallas_call(
        body, name=name,
        grid_spec=pltpu.PrefetchScalarGridSpec(num_scalar_prefetch=1, grid=(k, rws // tr),
                                               in_specs=[mine, blk], out_specs=[blk, blk]),
        out_shape=[jax.ShapeDtypeStruct(theirs.shape, F32), jax.ShapeDtypeStruct(theirs.shape, BF16)],
        compiler_params=_params(2))(core, slots, theirs)


def _ada_fwd(c_all, w_shard, b_shard):
    def body(c_ref, w_ref, b_ref, o_ref):
        ch, cm, cl = [t.astype(BF16) for t in _split3(c_ref[...])]
        wh, wm, wl = [t.astype(BF16) for t in _split3(w_ref[...])]
        acc = _dot(ch, wh) + _dot(ch, wm) + _dot(cm, wh) + _dot(ch, wl) + _dot(cl, wh) + _dot(cm, wm)
        o_ref[...] = acc + b_ref[...]

    return pl.pallas_call(body, name="ada_fwd", out_shape=jax.ShapeDtypeStruct((N_DEV, ADA_SHARD), F32),
                          compiler_params=_params(0))(c_all, w_shard, b_shard)


def _rope_tables_and_gather(pos_col, inv_freq, tm, weights, c_rows):
    s = pos_col.shape[0]
    nw = len(weights)
    last = s // tm - 1

    def body(*refs):
        p_ref, f_ref = refs[:2]
        w_src, c_src = refs[2:2 + nw], refs[2 + nw:3 + nw]
        cos_ref, sa_ref, sb_ref = refs[3 + nw:6 + nw]
        w_dst, c_dst = refs[6 + nw:6 + 2 * nw], refs[6 + 2 * nw:7 + 2 * nw]
        w_sems, c_sems = refs[7 + 2 * nw:10 + 2 * nw], refs[10 + 2 * nw:]

        @pl.when(pl.program_id(0) == 0)
        def _():
            _gather2_start(w_src, w_dst, *w_sems)
            _exchange_start(_exchange_copies(c_src, c_dst, ["gather"], *c_sems))

        ang = p_ref[...] * f_ref[...]
        sin = jnp.sin(ang)
        first_half = (_lane(CHUNK) & (HEAD_DIM - 1)) < HEAD_DIM // 2
        cos_ref[...] = jnp.cos(ang)
        sa_ref[...] = jnp.where(first_half, -sin, 0.0)
        sb_ref[...] = jnp.where(first_half, 0.0, sin)

        @pl.when(pl.program_id(0) == last)
        def _():
            _exchange_wait(_exchange_copies(c_src, c_dst, ["gather"], *c_sems))
            _gather2_finish(w_src, w_dst, *w_sems)

    tab = jax.ShapeDtypeStruct((s, CHUNK), F32)
    blk = pl.BlockSpec((tm, CHUNK), lambda i: (i, 0))
    gathered = [jax.ShapeDtypeStruct((N_DEV,) + a.shape, a.dtype) for a in list(weights) + [c_rows]]
    outs = pl.pallas_call(
        body, name="rope_tables_gather", grid=(s // tm,), out_shape=[tab, tab, tab] + gathered,
        in_specs=[pl.BlockSpec((tm, 1), lambda i: (i, 0)), pl.BlockSpec((1, CHUNK), lambda i: (0, 0))]
        + [ANY_SPEC] * (nw + 1),
        out_specs=[blk, blk, blk] + [ANY_SPEC] * (nw + 1),
        scratch_shapes=_exchange_sems(nw) + _exchange_sems(1),
        compiler_params=_params(1))(pos_col, inv_freq, *weights, c_rows)
    return outs[:3], outs[3:3 + nw], outs[3 + nw]


def _fwd_proj(x, mod, w_all, cos, sa, sb, bf_row, emat, ind, tm):
    s = x.shape[0]

    def body(x_ref, mod_ref, w_ref, cos_ref, sa_ref, sb_ref, bf_ref, e_ref, ind_ref,
             h_ref, qa_ref, ka_ref, va_ref, za_ref, qb_ref, kb_ref, vb_ref, zb_ref, ga_ref, gb_ref, f_ref,
             bnd_ref, nrm_ref, carry_ref):
        @pl.when(pl.program_id(0) == 0)
        def _():
            carry_ref[...] = jnp.zeros_like(carry_ref)

        xv = x_ref[...]
        rstd = lax.rsqrt(jnp.mean(xv * xv, axis=-1, keepdims=True) + NORM_EPS)
        h = (xv * rstd * mod_ref[0:1, :]) * (1.0 + mod_ref[1:2, :]) + mod_ref[2:3, :]
        hb = h.astype(BF16)
        h_ref[...] = hb.T

        def seg(name):
            off, w = SEG[name]
            return _dot(hb, w_ref[:, off:off + w])

        lane = _lane(CHUNK)
        low = lane < HEAD_DIM
        q_ones = _ones_lanes(64, 67)
        k_ones = _ones_lanes(67, 70)
        cos_t, sa_t, sb_t = cos_ref[...], sa_ref[...], sb_ref[...]

        def write_heads(ref, nat, extra_of, rope, scale):
            for p in range(N_HEADS // 2):
                c = nat[:, CHUNK * p:CHUNK * (p + 1)]
                if rope:
                    c = _rope(c, cos_t, sa_t, sb_t)
                if scale != 1.0:
                    c = c * scale
                for hh in range(2):
                    hd = 2 * p + hh
                    src = c if hh == 0 else pltpu.roll(c, HEAD_DIM, 1)
                    ref[:, CHUNK * hd:CHUNK * (hd + 1)] = jnp.where(low, src, extra_of(hd)).astype(BF16)

        write_heads(qa_ref, seg("qa"), lambda hd: q_ones, True, Q_SCALE)
        ka = seg("ka")
        va = seg("va")
        for kv in range(KV_GROUPS):
            sl = slice(CHUNK * kv, CHUNK * (kv + 1))
            ka_ref[:, sl] = jnp.where(low, _rope(ka[:, sl], cos_t, sa_t, sb_t), k_ones).astype(BF16)
            va_ref[:, sl] = jnp.where(low, va[:, sl], q_ones).astype(BF16)
        za_ref[...] = seg("za")
        zb_ref[...] = seg("zb")
        ga_ref[...] = seg("ga")
        gb_ref[...] = seg("gb")

        xf = seg("f") + bf_ref[...]
        f_ref[...] = xf
        logf = jnp.minimum(xf, 0.0) - jnp.log1p(jnp.exp(-jnp.abs(xf)))
        row = lax.broadcasted_iota(jnp.int32, (tm, tm), 0)
        col = lax.broadcasted_iota(jnp.int32, (tm, tm), 1)
        tri = jnp.where(col <= row, 1.0, 0.0).astype(BF16)
        hi, mid, lo = _split3(logf)
        cum = _dot(tri, hi.astype(BF16)) + _dot(tri, mid.astype(BF16)) + _dot(tri, lo.astype(BF16))
        cum = cum + carry_ref[0:1, :]
        carry_ref[0:1, :] += jnp.sum(logf, axis=0, keepdims=True)
        lane_all = _lane(AUG_W) & (CHUNK - 1)
        bias = _spread3(-cum, e_ref) + jnp.where((lane_all >= 67) & (lane_all < 70), 1.0, 0.0)

        qb, kb = seg("qb"), seg("kb")
        write_heads(qb_ref, qb, lambda hd: q_ones, False, Q_SCALE)
        write_heads(kb_ref, kb, lambda hd: bias[:, CHUNK * hd:CHUNK * (hd + 1)], False, 1.0)
        write_heads(vb_ref, seg("vb"), lambda hd: q_ones, False, 1.0)

        rid = lax.broadcasted_iota(jnp.int32, (tm, CHUNK), 0)
        bnd_ref[...] = jnp.zeros_like(bnd_ref)
        bnd_ref[0, 0:1, :] = jnp.sum(jnp.where(rid == 0, cum, 0.0), axis=0, keepdims=True)
        bnd_ref[0, 1:2, :] = jnp.sum(jnp.where(rid == tm - 1, cum, 0.0), axis=0, keepdims=True)

        @pl.when(pl.program_id(0) == 0)
        def _():
            nrm_ref[...] = jnp.zeros_like(nrm_ref)

        for r_, nat in ((0, qb * Q_SCALE), (1, kb)):
            sq = _dot((nat * nat).astype(BF16), ind_ref[...])
            nrm_ref[r_:r_ + 1, :] = jnp.maximum(nrm_ref[r_:r_ + 1, :], jnp.max(sq, axis=0, keepdims=True))

    row_blk = lambda w: pl.BlockSpec((tm, w), lambda i: (i, 0))
    full = lambda a: pl.BlockSpec(a.shape, lambda i: (0,) * a.ndim)
    sds = lambda w, dt: jax.ShapeDtypeStruct((s, w), dt)
    out_shape = [sds(AUG_W, BF16), sds(KV_GROUPS * CHUNK, BF16), sds(KV_GROUPS * CHUNK, BF16),
                 sds(512, F32), sds(AUG_W, BF16), sds(AUG_W, BF16), sds(AUG_W, BF16), sds(512, F32),
                 sds(D_MODEL, F32), sds(D_MODEL, F32), sds(CHUNK, F32)]
    small = [jax.ShapeDtypeStruct((s // tm, 8, CHUNK), F32), jax.ShapeDtypeStruct((8, CHUNK), F32)]
    return pl.pallas_call(
        body, name="fwd_proj", grid=(s // tm,),
        out_shape=[jax.ShapeDtypeStruct((D_MODEL, s), BF16)] + out_shape + small,
        in_specs=[row_blk(D_MODEL), full(mod), full(w_all), row_blk(CHUNK), row_blk(CHUNK), row_blk(CHUNK),
                  full(bf_row), full(emat), full(ind)],
        out_specs=[pl.BlockSpec((D_MODEL, tm), lambda i: (0, i))] + [row_blk(o.shape[1]) for o in out_shape] + [
            pl.BlockSpec((1, 8, CHUNK), lambda i: (i, 0, 0)), pl.BlockSpec((8, CHUNK), lambda i: (0, 0))],
        scratch_shapes=[pltpu.VMEM((8, CHUNK), F32)],
        compiler_params=_params(1))(x, mod, w_all, cos, sa, sb, bf_row, emat, ind)


def _swa_fwd(q_aug, k_aug, v_aug, sink_rows, tq):
    s = q_aug.shape[0]
    r = tq // WINDOW
    gw = GROUP * CHUNK

    def body(q_ref, kc_ref, kp_ref, vc_ref, vp_ref, sink_ref, o_ref, qb_ref):
        i = pl.program_id(1)
        st = GROUP * WINDOW
        qloc = lax.broadcasted_iota(jnp.int32, (st, 2 * WINDOW), 0) & (WINDOW - 1)
        col = lax.broadcasted_iota(jnp.int32, (st, 2 * WINDOW), 1)
        band = (col > qloc) & (col <= qloc + WINDOW)
        head = jnp.right_shift(lax.broadcasted_iota(jnp.int32, (st, 1), 0), 7)
        sink = jnp.zeros((st, 1), F32)
        for g in range(GROUP):
            sink = jnp.where(head == g, jnp.max(sink_ref[0, g:g + 1, :], axis=1, keepdims=True), sink)
        lane = _lane(CHUNK)
        for sub in range(r):
            rows = slice(WINDOW * sub, WINDOW * (sub + 1))
            q = jnp.concatenate([q_ref[rows, CHUNK * g:CHUNK * (g + 1)] for g in range(GROUP)], axis=0)
            if sub == 0:
                k = jnp.concatenate([kp_ref[...], kc_ref[rows, :]], axis=0)
                v = jnp.concatenate([vp_ref[...], vc_ref[rows, :]], axis=0)
                valid = band & ((col >= WINDOW) | (i > 0))
            else:
                both = slice(WINDOW * (sub - 1), WINDOW * (sub + 1))
                k, v, valid = kc_ref[both, :], vc_ref[both, :], band
            sc = jnp.where(valid, _dot_nt(q, k), NEG)
            m = jnp.maximum(jnp.max(sc, axis=1, keepdims=True), sink)
            acc = _dot(jnp.exp(sc - m).astype(BF16), v)
            denom = _lane_sum(acc, 64, 65) + jnp.exp(sink - m)
            out = acc / denom
            aug = _place3(lane, 67, -(m + jnp.log(denom)), q.astype(F32)).astype(BF16)
            hrows = lambda a, g: a[WINDOW * g:WINDOW * (g + 1), :]
            for g in range(GROUP):
                qb_ref[rows, CHUNK * g:CHUNK * (g + 1)] = hrows(aug, g)
            for pp in range(GROUP // 2):
                o_ref[rows, CHUNK * pp:CHUNK * (pp + 1)] = _pair(hrows(out, 2 * pp), hrows(out, 2 * pp + 1))

    return pl.pallas_call(
        body, name="swa_fwd", grid=(KV_GROUPS, s // tq),
        out_shape=[jax.ShapeDtypeStruct((s, 512), F32), jax.ShapeDtypeStruct((s, AUG_W), BF16)],
        in_specs=[pl.BlockSpec((tq, gw), lambda kv, i: (i, kv)),
                  pl.BlockSpec((tq, CHUNK), lambda kv, i: (i, kv)),
                  pl.BlockSpec((WINDOW, CHUNK), lambda kv, i: (jnp.maximum(i * r - 1, 0), kv)),
                  pl.BlockSpec((tq, CHUNK), lambda kv, i: (i, kv)),
                  pl.BlockSpec((WINDOW, CHUNK), lambda kv, i: (jnp.maximum(i * r - 1, 0), kv)),
                  pl.BlockSpec((1, 8, CHUNK), lambda kv, i: (kv, 0, 0))],
        out_specs=[pl.BlockSpec((tq, GROUP * HEAD_DIM), lambda kv, i: (i, kv)),
                   pl.BlockSpec((tq, gw), lambda kv, i: (i, kv))],
        compiler_params=_params(2))(q_aug, k_aug, k_aug, v_aug, v_aug, sink_rows)


def _swa_bwd(k_aug, v_aug, q_bwd, do_aug, cos, sa, sb, sink_rows, tk):
    s = k_aug.shape[0]
    r = tk // WINDOW
    nt = s // tk
    nb = s // WINDOW
    gw = GROUP * CHUNK

    def body(k_ref, v_ref, q_ref, qn_ref, do_ref, don_ref, cos_ref, sa_ref, sb_ref, sink_ref,
             da_ref, dsink_ref, carry_ref, acc_ref):
        j = pl.program_id(0)

        @pl.when(j == 0)
        def _():
            carry_ref[...] = jnp.zeros_like(carry_ref)
            dsink_ref[...] = jnp.zeros_like(dsink_ref)

        acc_ref[...] = jnp.zeros_like(acc_ref)
        acc_ref[0:WINDOW, :] = carry_ref[...]
        dk_col, dv_col = SEG["ka"][0] - SEG["qa"][0], SEG["va"][0] - SEG["qa"][0]
        st = GROUP * WINDOW
        key = lax.broadcasted_iota(jnp.int32, (WINDOW, 2 * st), 0)
        col = lax.broadcasted_iota(jnp.int32, (WINDOW, 2 * st), 1)
        qloc = col & (WINDOW - 1)
        band = ((col < st) & (key <= qloc)) | ((col >= st) & (key > qloc))
        lane = _lane(CHUNK)
        low = lane < HEAD_DIM
        cos_t, sa_t, sb_t = cos_ref[...], sa_ref[...], sb_ref[...]
        for kv, sub in [(kv, sub) for kv in range(KV_GROUPS) for sub in range(r)]:
            heads = [slice(CHUNK * (GROUP * kv + g), CHUNK * (GROUP * kv + g + 1)) for g in range(GROUP)]
            kvs = slice(CHUNK * kv, CHUNK * (kv + 1))
            rows = slice(WINDOW * sub, WINDOW * (sub + 1))
            nxt = slice(WINDOW * (sub + 1), WINDOW * (sub + 2))
            k, v = k_ref[rows, kvs], v_ref[rows, kvs]
            q_cur, do_cur = [q_ref[rows, cs] for cs in heads], [do_ref[rows, cs] for cs in heads]
            if sub < r - 1:
                q_nxt, do_nxt, valid = [q_ref[nxt, cs] for cs in heads], [do_ref[nxt, cs] for cs in heads], band
            else:
                q_nxt, do_nxt = [qn_ref[:, cs] for cs in heads], [don_ref[:, cs] for cs in heads]
                valid = band & ((col < st) | (j < nt - 1))
            q = jnp.concatenate(q_cur + q_nxt, axis=0)
            do = jnp.concatenate(do_cur + do_nxt, axis=0)
            pt = jnp.exp(jnp.where(valid, _dot_nt(k, q), NEG))
            ds = (pt * _dot_nt(v, do)).astype(BF16)
            dv = _dot(pt.astype(BF16), do)
            dk = _dot(ds, q)
            dq = _dot_tn(ds, k)
            for g, cs in enumerate(heads):
                acc_ref[rows, cs] += dq[WINDOW * g:WINDOW * (g + 1), :]
                dqn = dq[st + WINDOW * g:st + WINDOW * (g + 1), :]
                if sub < r - 1:
                    acc_ref[nxt, cs] += dqn
                else:
                    carry_ref[:, cs] = dqn
                sink = jnp.max(sink_ref[kv, g:g + 1, :], axis=1, keepdims=True)
                p_sink = jnp.exp(sink + _lane_sum(q_cur[g].astype(F32), 67, 70))
                term = jnp.sum(p_sink * _lane_sum(do_cur[g].astype(F32), 64, 67), axis=0, keepdims=True)
                dsink_ref[kv, g:g + 1, :] += jnp.broadcast_to(term, (1, CHUNK))
            da_ref[rows, dk_col + CHUNK * kv:dk_col + CHUNK * (kv + 1)] = _rope_inv(
                jnp.where(low, dk, 0.0), cos_t[rows, :], sa_t[rows, :], sb_t[rows, :]).astype(BF16)
            da_ref[rows, dv_col + CHUNK * kv:dv_col + CHUNK * (kv + 1)] = jnp.where(low, dv, 0.0).astype(BF16)
        for pp in range(N_HEADS // 2):
            d = _pair(acc_ref[:, CHUNK * 2 * pp:CHUNK * (2 * pp + 1)], acc_ref[:, CHUNK * (2 * pp + 1):CHUNK * (2 * pp + 2)])
            da_ref[:, CHUNK * pp:CHUNK * (pp + 1)] = (_rope_inv(d, cos_t, sa_t, sb_t) * Q_SCALE).astype(BF16)

    cur = lambda w: pl.BlockSpec((tk, w), lambda j: (j, 0))
    nxt = pl.BlockSpec((WINDOW, AUG_W), lambda j: (jnp.minimum((j + 1) * r, nb - 1), 0))
    whole = pl.BlockSpec((KV_GROUPS, 8, CHUNK), lambda j: (0, 0, 0))
    return pl.pallas_call(
        body, name="swa_bwd", grid=(nt,),
        out_shape=[jax.ShapeDtypeStruct((s, 1024), BF16), jax.ShapeDtypeStruct((KV_GROUPS, 8, CHUNK), F32)],
        in_specs=[cur(KV_GROUPS * CHUNK), cur(KV_GROUPS * CHUNK), cur(AUG_W), nxt, cur(AUG_W), nxt,
                  cur(CHUNK), cur(CHUNK), cur(CHUNK), whole],
        out_specs=[cur(1024), whole],
        scratch_shapes=[pltpu.VMEM((WINDOW, AUG_W), F32), pltpu.VMEM((tk, AUG_W), F32)],
        compiler_params=_params(1))(k_aug, v_aug, q_bwd, q_bwd, do_aug, do_aug, cos, sa, sb, sink_rows)


def _fox_fwd(first_key, q_aug, k_aug, v_aug, t):
    s = q_aug.shape[0]
    pw = 2 * CHUNK

    def body(lo_ref, q_ref, k_ref, v_ref, o_ref, qb_ref):
        i = pl.program_id(1)
        first = lo_ref[pl.program_id(0), i]
        row = lax.broadcasted_iota(jnp.int32, (t, t), 0)
        col = lax.broadcasted_iota(jnp.int32, (t, t), 1)
        lane = _lane(CHUNK)
        heads = (slice(0, CHUNK), slice(CHUNK, pw))
        qs = [q_ref[:, hs] for hs in heads]

        def step(jb, carry, masked):
            rows = pl.ds(pl.multiple_of(jb * t, t), t)
            new = []
            for hh, hs in enumerate(heads):
                m, acc = carry[2 * hh], carry[2 * hh + 1]
                sc = _dot_nt(qs[hh], k_ref[rows, hs])
                if masked:
                    sc = jnp.where(col <= row, sc, NEG)
                m_new = jnp.maximum(m, jnp.max(sc, axis=1, keepdims=True))
                p = jnp.exp(sc - m_new).astype(BF16)
                new += [m_new, jnp.exp(m - m_new) * acc + _dot(p, v_ref[rows, hs])]
            return tuple(new)

        init = (jnp.full((t, 1), NEG, F32), jnp.zeros((t, CHUNK), F32)) * 2
        carry = step(i, lax.fori_loop(first, i, lambda jb, c: step(jb, c, False), init), True)
        outs = []
        for hh, hs in enumerate(heads):
            m, acc = carry[2 * hh], carry[2 * hh + 1]
            denom = _lane_sum(acc, 64, 65)
            outs.append(acc / denom)
            lse = m + jnp.log(denom)
            qb_ref[:, hs] = _place3(lane, 67, -lse, qs[hh].astype(F32)).astype(BF16)
        o_ref[...] = _pair(outs[0], outs[1])

    return pl.pallas_call(
        body, name="fox_fwd", grid=(N_HEADS // 2, s // t),
        out_shape=[jax.ShapeDtypeStruct((s, 512), F32), jax.ShapeDtypeStruct((s, AUG_W), BF16)],
        in_specs=[pl.BlockSpec(memory_space=pltpu.SMEM),
                  pl.BlockSpec((t, pw), lambda p, i: (i, p)),
                  pl.BlockSpec((s, pw), lambda p, i: (0, p)),
                  pl.BlockSpec((s, pw), lambda p, i: (0, p))],
        out_specs=[pl.BlockSpec((t, CHUNK), lambda p, i: (i, p)), pl.BlockSpec((t, pw), lambda p, i: (i, p))],
        compiler_params=_params(2))(first_key, q_aug, k_aug, v_aug)


def _fox_block_ranges(bounds, norms, r):
    cum_first = bounds[0::r, 0, :N_HEADS]
    cum_last = bounds[r - 1::r, 1, :N_HEADS]
    n = cum_first.shape[0]
    reach = 2.0 * 1.02 * jnp.sqrt(norms[0, :N_HEADS] * norms[1, :N_HEADS]) + 105.0
    blk = jnp.arange(n, dtype=jnp.int32)
    decay = cum_last[None, :, :] - cum_first[:, None, :]
    skip = (decay > reach[None, None, :]) & (blk[None, :, None] < blk[:, None, None])
    first = jnp.min(jnp.where(skip, n, blk[None, :, None]), axis=1)
    first = jnp.minimum(first[:, 0::2], first[:, 1::2]).T
    needed = (first[:, :, None] <= blk[None, None, :]) & (blk[None, :, None] >= blk[None, None, :])
    last = jnp.max(jnp.where(needed, blk[None, :, None], blk[None, None, :]), axis=1)
    return first.astype(jnp.int32), last.astype(jnp.int32)


def _fox_bwd(last_query, k_aug, v_aug, q_bwd, do_aug, t):
    s = k_aug.shape[0]
    n = s // t
    pw = 2 * CHUNK

    def body(hi_ref, k_ref, v_ref, q_ref, do_ref, dk_ref, dv_ref, dck_ref, dq_ref, dcq_ref, dq_scr):
        j = pl.program_id(1)
        last = hi_ref[pl.program_id(0), j]

        @pl.when(j == 0)
        def _():
            dq_scr[...] = jnp.zeros_like(dq_scr)

        row = lax.broadcasted_iota(jnp.int32, (t, t), 0)
        col = lax.broadcasted_iota(jnp.int32, (t, t), 1)
        lane = _lane(CHUNK)
        heads = (slice(0, CHUNK), slice(CHUNK, pw))
        ks = [k_ref[:, hs] for hs in heads]
        vs = [v_ref[:, hs] for hs in heads]

        def step(ib, carry, masked):
            rows = pl.ds(pl.multiple_of(ib * t, t), t)
            new = []
            for hh, hs in enumerate(heads):
                dv, dk = carry[2 * hh], carry[2 * hh + 1]
                q, do = q_ref[rows, hs], do_ref[rows, hs]
                st = _dot_nt(ks[hh], q)
                if masked:
                    st = jnp.where(row <= col, st, NEG)
                pt = jnp.exp(st)
                ds = (pt * _dot_nt(vs[hh], do)).astype(BF16)
                new += [dv + _dot(pt.astype(BF16), do), dk + _dot(ds, q)]
                dq_scr[rows, hs] += _dot_tn(ds, ks[hh])
            return tuple(new)

        zero = jnp.zeros((t, CHUNK), F32)
        carry = lax.fori_loop(j + 1, last + 1, lambda ib, c: step(ib, c, False), step(j, (zero,) * 4, True))
        dvs, dks = (carry[0], carry[2]), (carry[1], carry[3])
        dk_ref[...] = _pair(dks[0], dks[1]).astype(BF16)
        dv_ref[...] = _pair(dvs[0], dvs[1]).astype(BF16)
        dck_ref[0] = jnp.where(lane == 0, pltpu.roll(dks[0], 64, 1),
                               jnp.where(lane == 1, pltpu.roll(dks[1], 65, 1), 0.0))

        @pl.when(j == n - 1)
        def _():
            for ib in range(n):
                rows = slice(t * ib, t * (ib + 1))
                d0, d1 = dq_scr[rows, 0:CHUNK], dq_scr[rows, CHUNK:pw]
                dq_ref[rows, :] = (_pair(d0, d1) * Q_SCALE).astype(BF16)
                dcq_ref[0, rows, :] = jnp.where(lane == 0, pltpu.roll(d0, CHUNK - 67, 1),
                                                jnp.where(lane == 1, pltpu.roll(d1, CHUNK - 66, 1), 0.0))

    return pl.pallas_call(
        body, name="fox_bwd", grid=(N_HEADS // 2, n),
        out_shape=[jax.ShapeDtypeStruct((s, 512), BF16), jax.ShapeDtypeStruct((s, 512), BF16),
                   jax.ShapeDtypeStruct((N_HEADS // 2, s, CHUNK), F32), jax.ShapeDtypeStruct((s, 512), BF16),
                   jax.ShapeDtypeStruct((N_HEADS // 2, s, CHUNK), F32)],
        in_specs=[pl.BlockSpec(memory_space=pltpu.SMEM),
                  pl.BlockSpec((t, pw), lambda p, j: (j, p)), pl.BlockSpec((t, pw), lambda p, j: (j, p)),
                  pl.BlockSpec((s, pw), lambda p, j: (0, p)), pl.BlockSpec((s, pw), lambda p, j: (0, p))],
        out_specs=[pl.BlockSpec((t, CHUNK), lambda p, j: (j, p)), pl.BlockSpec((t, CHUNK), lambda p, j: (j, p)),
                   pl.BlockSpec((1, t, CHUNK), lambda p, j: (p, j, 0)), pl.BlockSpec((s, CHUNK), lambda p, j: (0, p)),
                   pl.BlockSpec((1, s, CHUNK), lambda p, j: (p, 0, 0))],
        scratch_shapes=[pltpu.VMEM((s, pw), F32)],
        compiler_params=_params(2))(last_query, k_aug, v_aug, q_bwd, do_aug)


def _fgate_bwd(dcq, dck, xf, tm):
    s = xf.shape[0]
    nt = s // tm

    def body(dq_ref, dc_ref, xf_ref, df_ref, dbf_ref, carry_ref):
        @pl.when(pl.program_id(0) == 0)
        def _():
            carry_ref[...] = jnp.zeros_like(carry_ref)
            dbf_ref[...] = jnp.zeros_like(dbf_ref)

        row = lax.broadcasted_iota(jnp.int32, (tm, tm), 0)
        col = lax.broadcasted_iota(jnp.int32, (tm, tm), 1)
        tri = jnp.where(col >= row, 1.0, 0.0).astype(BF16)
        dcum = dq_ref[0] - dc_ref[0]
        for p in range(1, N_HEADS // 2):
            dcum = dcum + pltpu.roll(dq_ref[p] - dc_ref[p], 2 * p, 1)
        hi, mid, lo = _split3(dcum)
        dlogf = _dot(tri, hi.astype(BF16)) + _dot(tri, mid.astype(BF16)) + _dot(tri, lo.astype(BF16))
        dlogf = dlogf + carry_ref[0:1, :]
        carry_ref[0:1, :] += jnp.sum(dcum, axis=0, keepdims=True)
        df = dlogf * _sigmoid(-xf_ref[...])
        df_ref[...] = df.astype(BF16)
        dbf_ref[0:1, :] += jnp.sum(df, axis=0, keepdims=True)

    rev = pl.BlockSpec((tm, CHUNK), lambda i: (nt - 1 - i, 0))
    rev4 = pl.BlockSpec((N_HEADS // 2, tm, CHUNK), lambda i: (0, nt - 1 - i, 0))
    return pl.pallas_call(
        body, name="fgate_bwd", grid=(nt,),
        out_shape=[jax.ShapeDtypeStruct((s, CHUNK), BF16), jax.ShapeDtypeStruct((8, CHUNK), F32)],
        in_specs=[rev4, rev4, rev], out_specs=[rev, pl.BlockSpec((8, CHUNK), lambda i: (0, 0))],
        scratch_shapes=[pltpu.VMEM((8, CHUNK), F32)],
        compiler_params=_params(1))(dcq, dck, xf)


def _post(x, tgt, att_a, att_b, za, zb, ga, gb, woa_t, wob_t, wout, vec, ind, emat, tm):
    s = x.shape[0]

    def body(x_ref, t_ref, aa_ref, ab_ref, za_ref, zb_ref, ga_ref, gb_ref, woat_ref, wobt_ref, wout_ref,
             vec_ref, ind_ref, e_ref,
             dx1_ref, ua_ref, ub_ref, mg_ref, do_ref, dya_ref, dyb_ref, doa_ref, dob_ref, dzg_ref, acc_ref):
        @pl.when(pl.program_id(0) == 0)
        def _():
            acc_ref[...] = jnp.zeros_like(acc_ref)

        gate, gfin = vec_ref[0:1, :], vec_ref[1:2, :]
        inv_d = 1.0 / D_MODEL

        def branch_fwd(att_ref, z_ref, w_ref, u_ref):
            att, z = att_ref[...], z_ref[...]
            sz = _sigmoid(z)
            silu = z * sz
            u = (att * silu).astype(BF16)
            u_ref[...] = u.T
            return att, z, sz, silu, _dot_nt(u, w_ref[...])

        att_a, z_a, sz_a, silu_a, y_a = branch_fwd(aa_ref, za_ref, woat_ref, ua_ref)
        att_b, z_b, sz_b, silu_b, y_b = branch_fwd(ab_ref, zb_ref, wobt_ref, ub_ref)
        sg_a, sg_b = _sigmoid(ga_ref[...]), _sigmoid(gb_ref[...])
        merged = (sg_a * y_a + sg_b * y_b).astype(BF16)
        mg_ref[...] = merged.T
        o = _dot(merged, wout_ref[...])
        x1 = x_ref[...] + gate * o
        rstd = lax.rsqrt(jnp.mean(x1 * x1, axis=-1, keepdims=True) + NORM_EPS)
        xh = x1 * rstd
        diff = xh * gfin - t_ref[...]
        acc_ref[2:3, :] += (0.5 * inv_d) * jnp.sum(diff * diff, axis=0, keepdims=True)
        dy = diff * inv_d
        acc_ref[1:2, :] += jnp.sum(dy * xh, axis=0, keepdims=True)
        dyg = dy * gfin
        dx1 = rstd * (dyg - xh * jnp.mean(dyg * xh, axis=-1, keepdims=True))
        dx1_ref[...] = dx1
        acc_ref[0:1, :] += jnp.sum(dx1 * o, axis=0, keepdims=True)
        d_o = (dx1 * gate).astype(BF16)
        do_ref[...] = d_o
        dmg = _dot_nt(d_o, wout_ref[...])

        lane = _lane(CHUNK)
        low = lane < HEAD_DIM
        lane_all = _lane(AUG_W) & (CHUNK - 1)

        zg0 = SEG["za"][0]

        def branch_bwd(sg, y, att, z, sz, silu, wt_ref, dy_ref, g_name, z_name, doaug_ref):
            dyb = (dmg * sg).astype(BF16)
            dy_ref[...] = dyb
            g_off, z_off = SEG[g_name][0] - zg0, SEG[z_name][0] - zg0
            dzg_ref[:, g_off:g_off + D_MODEL] = (dmg * y * sg * (1.0 - sg)).astype(BF16)
            du = _dot(dyb, wt_ref[...])
            datt = du * silu
            dzg_ref[:, z_off:z_off + 512] = (du * att * (sz * (1.0 + z * (1.0 - sz)))).astype(BF16)
            extra = _spread3(-_dot_split(datt * att, ind_ref[...]), e_ref)
            for p in range(N_HEADS // 2):
                c = datt[:, CHUNK * p:CHUNK * (p + 1)]
                for hh in range(2):
                    hd = 2 * p + hh
                    src = c if hh == 0 else pltpu.roll(c, HEAD_DIM, 1)
                    doaug_ref[:, CHUNK * hd:CHUNK * (hd + 1)] = jnp.where(
                        low, src, extra[:, CHUNK * hd:CHUNK * (hd + 1)]).astype(BF16)

        branch_bwd(sg_a, y_a, att_a, z_a, sz_a, silu_a, woat_ref, dya_ref, "ga", "za", doa_ref)
        branch_bwd(sg_b, y_b, att_b, z_b, sz_b, silu_b, wobt_ref, dyb_ref, "gb", "zb", dob_ref)

    row_blk = lambda w: pl.BlockSpec((tm, w), lambda i: (i, 0))
    full = lambda a: pl.BlockSpec(a.shape, lambda i: (0,) * a.ndim)
    sds = lambda w, dt: jax.ShapeDtypeStruct((s, w), dt)
    tds = lambda w: jax.ShapeDtypeStruct((w, s), BF16)
    col_blk = lambda w: pl.BlockSpec((w, tm), lambda i: (0, i))
    out_shape = [sds(D_MODEL, F32), tds(512), tds(512), tds(D_MODEL), sds(D_MODEL, BF16),
                 sds(D_MODEL, BF16), sds(D_MODEL, BF16), sds(AUG_W, BF16), sds(AUG_W, BF16), sds(3072, BF16),
                 jax.ShapeDtypeStruct((8, D_MODEL), F32)]
    ins = [x, tgt, att_a, att_b, za, zb, ga, gb, woa_t, wob_t, wout, vec, ind, emat]
    in_specs = [row_blk(a.shape[1]) for a in ins[:8]] + [full(a) for a in ins[8:]]
    out_specs = ([row_blk(D_MODEL), col_blk(512), col_blk(512), col_blk(D_MODEL)]
                 + [row_blk(o.shape[1]) for o in out_shape[4:-1]] + [pl.BlockSpec((8, D_MODEL), lambda i: (0, 0))])
    return pl.pallas_call(body, name="post", grid=(s // tm,), out_shape=out_shape, in_specs=in_specs,
                          out_specs=out_specs, compiler_params=_params(1))(*ins)


def _bwd_pre(dproj, w_all, x, dx1, mod, tm, chip_halves):
    s = x.shape[0]
    ng, nh = len(DPROJ_GROUPS), len(chip_halves)
    last = s // tm - 1
    modes = ["chips"] * nh

    def body(*refs):
        dp_refs = refs[:ng]
        wt_ref, x_ref, dx1_ref, mod_ref = refs[ng:ng + 4]
        send_refs = refs[ng + 4:ng + 4 + nh]
        gx_ref, acc_ref = refs[ng + 4 + nh:ng + 6 + nh]
        recv_refs = refs[ng + 6 + nh:ng + 6 + 2 * nh]
        sems = refs[ng + 6 + 2 * nh:]

        @pl.when(pl.program_id(0) == 0)
        def _():
            acc_ref[...] = jnp.zeros_like(acc_ref)
            _exchange_start(_exchange_copies(send_refs, recv_refs, modes, *sems))

        dh = None
        for dp_ref, (_, off, w) in zip(dp_refs, DPROJ_GROUPS):
            part = _dot_nt(dp_ref[...], wt_ref[:, off:off + w])
            dh = part if dh is None else dh + part
        xv = x_ref[...]
        rstd = lax.rsqrt(jnp.mean(xv * xv, axis=-1, keepdims=True) + NORM_EPS)
        xh = xv * rstd
        gn = mod_ref[0:1, :]
        acc_ref[0:1, :] += jnp.sum(dh, axis=0, keepdims=True)
        acc_ref[1:2, :] += jnp.sum(dh * (xh * gn), axis=0, keepdims=True)
        dn = dh * (1.0 + mod_ref[1:2, :])
        acc_ref[2:3, :] += jnp.sum(dn * xh, axis=0, keepdims=True)
        dxh = dn * gn
        gx_ref[...] = dx1_ref[...] + rstd * (dxh - xh * jnp.mean(dxh * xh, axis=-1, keepdims=True))

        @pl.when(pl.program_id(0) == last)
        def _():
            _exchange_wait(_exchange_copies(send_refs, recv_refs, modes, *sems))

    row_blk = lambda w: pl.BlockSpec((tm, w), lambda i: (i, 0))
    full = lambda a: pl.BlockSpec(a.shape, lambda i: (0,) * a.ndim)
    outs = pl.pallas_call(
        body, name="bwd_pre", grid=(s // tm,),
        out_shape=[jax.ShapeDtypeStruct((s, D_MODEL), F32), jax.ShapeDtypeStruct((8, D_MODEL), F32)]
        + _exchange_out_shapes(chip_halves, modes),
        in_specs=[row_blk(w) for _, _, w in DPROJ_GROUPS] + [full(w_all), row_blk(D_MODEL), row_blk(D_MODEL), full(mod)]
        + [ANY_SPEC] * nh,
        out_specs=[row_blk(D_MODEL), pl.BlockSpec((8, D_MODEL), lambda i: (0, 0))] + [ANY_SPEC] * nh,
        scratch_shapes=_exchange_sems(nh),
        compiler_params=_params(1))(*dproj, w_all, x, dx1, mod, *chip_halves)
    return outs[0], outs[1], outs[2:]


def _token_matmul(a_t, b, tn, tk, name):
    m, s = a_t.shape
    n = b.shape[1]

    def body(a_ref, b_ref, o_ref):
        @pl.when(pl.program_id(1) == 0)
        def _():
            o_ref[...] = jnp.zeros_like(o_ref)

        o_ref[...] += _dot(a_ref[...], b_ref[...])

    return pl.pallas_call(
        body, name=name, grid=(n // tn, s // tk), out_shape=jax.ShapeDtypeStruct((m, n), F32),
        in_specs=[pl.BlockSpec((m, tk), lambda jn, k: (0, k)), pl.BlockSpec((tk, tn), lambda jn, k: (k, jn))],
        out_specs=pl.BlockSpec((m, tn), lambda jn, k: (0, jn)),
        compiler_params=_params(2))(a_t, b)


def _token_matmul_multi(a_t, bs, tk, name):
    m, s = a_t.shape
    nb = len(bs)

    def body(*refs):
        a_ref, b_refs, o_refs = refs[0], refs[1:1 + nb], refs[1 + nb:]
        a = a_ref[...]
        for b_ref, o_ref in zip(b_refs, o_refs):
            @pl.when(pl.program_id(0) == 0)
            def _():
                o_ref[...] = jnp.zeros_like(o_ref)

            o_ref[...] += _dot(a, b_ref[...])

    return pl.pallas_call(
        body, name=name, grid=(s // tk,),
        out_shape=[jax.ShapeDtypeStruct((m, b.shape[1]), F32) for b in bs],
        in_specs=[pl.BlockSpec((m, tk), lambda k: (0, k))] + [pl.BlockSpec((tk, b.shape[1]), lambda k: (k, 0)) for b in bs],
        out_specs=[pl.BlockSpec((m, b.shape[1]), lambda k: (0, 0)) for b in bs],
        compiler_params=_params(1))(a_t, *bs)


def _adam_math(g, w, m, v):
    m2 = ADAM_B1 * m + (1.0 - ADAM_B1) * g
    v2 = ADAM_B2 * v + (1.0 - ADAM_B2) * (g * g)
    delta = -ADAM_LR * ((m2 / ADAM_C1) / (jnp.sqrt(v2 / ADAM_C2) + ADAM_EPS) + ADAM_WD * w)
    return delta, m2, v2


def _adamw_parts(full, parts, others, w, m, v, tr, name):
    _, rws, cols = w.shape

    def body(oth_ref, f_ref, p_ref, w_ref, m_ref, v_ref, g_ref, d_ref, m2_ref, v2_ref):
        g = None
        for chip in range(N_CHIPS):
            part = (1.0 - oth_ref[chip]) * f_ref[chip] + oth_ref[chip] * p_ref[chip].astype(F32)
            g = part if g is None else g + part
        g_ref[0] = g
        d_ref[0], m2_ref[0], v2_ref[0] = _adam_math(g, w_ref[0], m_ref[0], v_ref[0])

    blk = pl.BlockSpec((1, tr, cols), lambda i: (0, i, 0))
    slots = pl.BlockSpec((N_CHIPS, tr, cols), lambda i: (0, i, 0))
    o = jax.ShapeDtypeStruct((1, rws, cols), F32)
    return pl.pallas_call(
        body, name=name, grid=(rws // tr,), out_shape=[o, o, o, o],
        in_specs=[pl.BlockSpec(memory_space=pltpu.SMEM), slots, slots, blk, blk, blk],
        out_specs=[blk, blk, blk, blk], compiler_params=_params(1))(others, full, parts, w, m, v)


def _adamw_ada(c_lanes, d_rows, w, m, v, tr):
    _, rws, cols = w.shape

    def body(c_ref, d_ref, w_ref, m_ref, v_ref, g_ref, dl_ref, m2_ref, v2_ref):
        for k in range(cols // CHUNK):
            cs = slice(CHUNK * k, CHUNK * (k + 1))
            g = c_ref[0] * d_ref[0:1, cs]
            for b in range(1, N_DEV):
                g = g + c_ref[b] * d_ref[b:b + 1, cs]
            g_ref[0, :, cs] = g
            dl_ref[0, :, cs], m2_ref[0, :, cs], v2_ref[0, :, cs] = _adam_math(
                g, w_ref[0, :, cs], m_ref[0, :, cs], v_ref[0, :, cs])

    blk = pl.BlockSpec((1, tr, cols), lambda i: (0, i, 0))
    o = jax.ShapeDtypeStruct((1, rws, cols), F32)
    return pl.pallas_call(
        body, name="adamw_ada", grid=(rws // tr,), out_shape=[o, o, o, o],
        in_specs=[pl.BlockSpec((N_DEV, tr, CHUNK), lambda i: (0, i, 0)), pl.BlockSpec((N_DEV, cols), lambda i: (0, 0)),
                  blk, blk, blk],
        out_specs=[blk, blk, blk, blk], compiler_params=_params(1))(c_lanes, d_rows, w, m, v)


def _adamw_small(packs, w, m, v):
    def body(p_ref, w_ref, m_ref, v_ref, g_ref, d_ref, m2_ref, v2_ref, loss_ref):
        g = p_ref[0]
        for dev in range(1, N_DEV):
            g = g + p_ref[dev]
        g_ref[...] = g
        d_ref[...], m2_ref[...], v2_ref[...] = _adam_math(g, w_ref[...], m_ref[...], v_ref[...])
        loss_ref[...] = jnp.broadcast_to(jnp.sum(g_ref[2:3, 0:D_MODEL], axis=1, keepdims=True), loss_ref.shape)

    o = jax.ShapeDtypeStruct(w.shape, F32)
    return pl.pallas_call(body, name="adamw_small", out_shape=[o, o, o, o, jax.ShapeDtypeStruct((8, CHUNK), F32)],
                          compiler_params=_params(0))(packs, w, m, v)


def _tile(s, want):
    return min(s, want)


def _layout_pieces():
    orig = {"qa": 0, "ka": 512, "va": 640, "za": 768, "qb": 1280, "kb": 1792, "vb": 2304, "f": 2816, "zb": 2824,
            "ga": 3336, "gb": 4360}
    pieces = []
    for name, (off, w) in SEG.items():
        if name in ("ka", "va"):
            pieces += [(orig[name] + HEAD_DIM * kv, orig[name] + HEAD_DIM * (kv + 1), off + CHUNK * kv) for kv in range(KV_GROUPS)]
        else:
            pieces.append((orig[name], orig[name] + (N_HEADS if name == "f" else w), off))
    return pieces


def _assemble_w_all(win_g):
    cols, pos = [], 0
    for lo, hi, new in sorted(_layout_pieces(), key=lambda t: t[2]):
        if new > pos:
            cols.append(jnp.zeros((D_MODEL, new - pos), win_g.dtype))
        col = lo
        while col < hi:
            dev = col // IN_SHARD
            end = min(hi, (dev + 1) * IN_SHARD)
            cols.append(win_g[dev, :, col - dev * IN_SHARD:end - dev * IN_SHARD])
            col = end
        pos = new + hi - lo
    cols.append(jnp.zeros((D_MODEL, PROJ_W - pos), win_g.dtype))
    return jnp.concatenate(cols, axis=1)


def _grad_slot(dw_groups, dev):
    lo_d, hi_d = dev * IN_SHARD, (dev + 1) * IN_SHARD
    cols = []
    for lo, hi, new in sorted(_layout_pieces()):
        a, b = max(lo, lo_d), min(hi, hi_d)
        if a < b:
            arr, off = next((g, o) for g, (_, o, w) in zip(dw_groups, DPROJ_GROUPS) if o <= new < o + w)
            cols.append(arr[:, new - off + a - lo:new - off + b - lo])
    return jnp.concatenate(cols, axis=1)


def kernel(x, c, positions, w_ada, b_ada, g_norm, w_in, b_f, sinks, w_o_swa, w_o_fox, w_out, g_final, loss_target, m_w_ada, m_b_ada, m_g_norm, m_w_in, m_b_f, m_sinks, m_w_o_swa, m_w_o_fox, m_w_out, m_g_final, v_w_ada, v_b_ada, v_g_norm, v_w_in, v_b_f, v_sinks, v_w_o_swa, v_w_o_fox, v_w_out, v_g_final):
    s = x.shape[1]
    tm = _tile(s, 256)
    ta = _tile(s, 512)
    me = 4 * lax.axis_index("x") + 2 * lax.axis_index("y") + lax.axis_index("c")
    x2, tgt = x[0], loss_target[0]

    inv_freq = np.power(np.float32(ROPE_THETA), -np.arange(0, HEAD_DIM, 2, dtype=np.float32) / HEAD_DIM)
    inv_freq = jnp.asarray(np.tile(inv_freq, CHUNK // (HEAD_DIM // 2))[None, :], F32)
    (cos, sa, sb), (win_g, woa_g, wob_g, wout_g), c_all = _rope_tables_and_gather(
        positions.reshape(s, 1).astype(F32), inv_freq, tm,
        [w_in[0].astype(BF16), w_o_swa[0].T.astype(BF16), w_o_fox[0].T.astype(BF16), w_out[0].astype(BF16)],
        jnp.broadcast_to(c, (8, D_MODEL)))
    c_all = c_all[:, 0, :]
    b_shard = lax.dynamic_slice(b_ada, (0, me * ADA_SHARD), (1, ADA_SHARD))
    ada_part = _ada_fwd(c_all, w_ada[0], b_shard)
    (ada_g,) = _exchange([ada_part], ["gather"], "gather_ada")
    ada = lax.dynamic_index_in_dim(ada_g, me, axis=1, keepdims=False).reshape(3 * D_MODEL)
    shift, scale, gate = ada[:D_MODEL], ada[D_MODEL:2 * D_MODEL], ada[2 * D_MODEL:]
    woa_t = woa_g.reshape(D_MODEL, 512)
    wob_t = wob_g.reshape(D_MODEL, 512)
    wout = wout_g.reshape(D_MODEL, D_MODEL)
    w_all = _assemble_w_all(win_g)

    zrow = jnp.zeros((1, D_MODEL), F32)
    mod = jnp.concatenate([g_norm, scale[None], shift[None], zrow, zrow, zrow, zrow, zrow], axis=0)
    bf_row = jnp.pad(b_f, ((0, 0), (0, CHUNK - N_HEADS)))
    emat_np = np.zeros((3, CHUNK, AUG_W), np.float32)
    ind_np = np.zeros((512, CHUNK), np.float32)
    for hd in range(N_HEADS):
        for part in range(3):
            emat_np[part, hd, CHUNK * hd + 64 + part] = 1.0
        ind_np[HEAD_DIM * hd:HEAD_DIM * (hd + 1), hd] = 1.0
    emat, ind = jnp.asarray(emat_np, BF16), jnp.asarray(ind_np, BF16)
    sink_rows = jnp.broadcast_to(jnp.pad(sinks.reshape(KV_GROUPS, GROUP), ((0, 0), (0, 8 - GROUP)))[:, :, None],
                                 (KV_GROUPS, 8, CHUNK))

    (h_t, qa, ka, va, za, qb, kb, vb, zb, ga, gb, xf, bounds, norms) = _fwd_proj(
        x2, mod, w_all, cos, sa, sb, bf_row, emat, ind, tm)
    first_key, last_query = _fox_block_ranges(bounds, norms, ta // tm)
    att_a, qa_bwd = _swa_fwd(qa, ka, va, sink_rows, ta)
    att_b, qb_bwd = _fox_fwd(first_key, qb, kb, vb, ta)
    vec = jnp.concatenate([gate[None], g_final[None], zrow, zrow, zrow, zrow, zrow, zrow], axis=0)
    (dx1, ua_t, ub_t, merged_t, d_o, dya, dyb, doa, dob, dzg, post_acc) = _post(
        x2, tgt, att_a, att_b, za, zb, ga, gb, woa_t, wob_t, wout, vec, ind, emat, tm)

    da, dsink = _swa_bwd(ka, va, qa_bwd, doa, cos, sa, sb, sink_rows, ta)
    dkb, dvb, dck, dqb, dcq = _fox_bwd(last_query, kb, vb, qb_bwd, dob, ta)
    dfb, dbf_acc = _fgate_bwd(dcq, dck, xf, ta)
    dproj = [da, dqb, dkb, dvb, dzg, dfb]
    dw_zg, = _token_matmul_multi(h_t, [dzg], ta, "dw_in_zg")
    dw_a, dw_qb, dw_kb, dw_vb, dw_f = _token_matmul_multi(h_t, [da, dqb, dkb, dvb, dfb], ta, "dw_in_rest")
    dw_all = [dw_a, dw_qb, dw_kb, dw_vb, dw_zg, dw_f]

    core, chip = lax.axis_index("c"), 2 * lax.axis_index("x") + lax.axis_index("y")
    col_slots = lambda g, r, w: g.reshape(r, N_CHIPS, 2, w).transpose(2, 1, 0, 3)
    win_slots = jnp.stack([jnp.stack([_grad_slot(dw_all, 2 * ch + co) for ch in range(N_CHIPS)]) for co in range(2)])
    (dwoa, dwob, dwout), (win_theirs,) = _token_matmuls_and_swap(
        [(ua_t, dya), (ub_t, dyb), (merged_t, d_o)], ta, [win_slots])
    slots = [win_slots, col_slots(dwoa, 512, 128), col_slots(dwob, 512, 128),
             dwout.reshape(N_CHIPS, 2, 128, D_MODEL).transpose(1, 0, 2, 3)]
    theirs = [win_theirs] + list(_swap_sibling(slots[1:], "grads_sibling"))
    fulls, halves = [], []
    core_arr = jnp.reshape(core, (1,)).astype(jnp.int32)
    for sl, th, nm in zip(slots, theirs, ("w_in", "w_o_swa", "w_o_fox", "w_out")):
        full, half = _chip_partial(core_arr, sl, th, min(th.shape[1], 256), "chip_partial_" + nm)
        fulls.append(full)
        halves.append(half)
    grad_x, pre_acc, (p_win, p_woa, p_wob, p_wout) = _bwd_pre(dproj, w_all, x2, dx1, mod, tm, halves)
    misc = jnp.concatenate([dbf_acc[0, :N_HEADS], dsink[:, :GROUP, 0].reshape(N_HEADS), jnp.zeros((D_MODEL - 16,), F32)])
    zpad = jnp.zeros((2 * D_MODEL,), F32)
    pack = jnp.stack([jnp.concatenate([pre_acc[0], pre_acc[1], post_acc[0]]),
                      jnp.concatenate([pre_acc[2], post_acc[1], misc]),
                      jnp.concatenate([post_acc[2], zpad])] + [jnp.zeros((3 * D_MODEL,), F32)] * 5)
    (packs,) = _exchange([pack], ["gather"], "gather_small_grads")
    others = jnp.where(jnp.arange(N_CHIPS) == chip, 0.0, 1.0).astype(F32)

    g_win, d_win, m_win, v_win = _adamw_parts(fulls[0], p_win, others, w_in, m_w_in, v_w_in, 128, "adamw_w_in")
    g_woa, d_woa, m_woa, v_woa = _adamw_parts(fulls[1], p_woa, others, w_o_swa, m_w_o_swa, v_w_o_swa, 128, "adamw_w_o_swa")
    g_wob, d_wob, m_wob, v_wob = _adamw_parts(fulls[2], p_wob, others, w_o_fox, m_w_o_fox, v_w_o_fox, 128, "adamw_w_o_fox")
    g_wout, d_wout, m_wout, v_wout = _adamw_parts(fulls[3], p_wout, others, w_out, m_w_out, v_w_out, 128, "adamw_w_out")
    d_ada_rows = lax.dynamic_slice(packs[:, 0, :], (0, me * ADA_SHARD), (N_DEV, ADA_SHARD))
    c_lanes = jnp.broadcast_to(c_all[:, :, None], (N_DEV, D_MODEL, CHUNK))
    g_wada, d_wada, m_wada, v_wada = _adamw_ada(c_lanes, d_ada_rows, w_ada, m_w_ada, v_w_ada, 256)

    def small_pack(bada, gn, gf, bfv, sk):
        misc_w = jnp.concatenate([bfv[0], sk[0], jnp.zeros((D_MODEL - 16,), F32)])
        return jnp.stack([bada[0], jnp.concatenate([gn[0], gf, misc_w])] + [jnp.zeros((3 * D_MODEL,), F32)] * 6)

    sm = _adamw_small(packs, small_pack(b_ada, g_norm, g_final, b_f, sinks),
                      small_pack(m_b_ada, m_g_norm, m_g_final, m_b_f, m_sinks),
                      small_pack(v_b_ada, v_g_norm, v_g_final, v_b_f, v_sinks))
    loss = sm[4][0, 0]

    def unpack(p):
        return (p[0][None], p[1, :D_MODEL][None], p[1, 2 * D_MODEL:2 * D_MODEL + 8][None],
                p[1, 2 * D_MODEL + 8:2 * D_MODEL + 16][None], p[1, D_MODEL:2 * D_MODEL])

    outs = []
    for big, small in (((g_wada, g_win, g_woa, g_wob, g_wout), sm[0]), ((d_wada, d_win, d_woa, d_wob, d_wout), sm[1]),
                       ((m_wada, m_win, m_woa, m_wob, m_wout), sm[2]), ((v_wada, v_win, v_woa, v_wob, v_wout), sm[3])):
        wada_o, win_o, woa_o, wob_o, wout_o = big
        bada_o, gn_o, bf_o, sk_o, gf_o = unpack(small)
        outs += [wada_o, bada_o, gn_o, win_o, bf_o, sk_o, woa_o, wob_o, wout_o, gf_o]
    return (loss, grad_x[None], *outs)
```

```python
import numpy as np
import jax
import jax.numpy as jnp
from jax import lax
from jax.experimental import pallas as pl
from jax.experimental.pallas import tpu as pltpu

F32 = jnp.float32
BF16 = jnp.bfloat16

D_MODEL = 1024
HEAD_DIM = 64
N_HEADS = 8
KV_GROUPS = 2
GROUP = N_HEADS // KV_GROUPS
WINDOW = 128
CHUNK = 128
AUG_W = N_HEADS * CHUNK
N_DEV = 8
IN_SHARD = 673
ADA_SHARD = 384
NORM_EPS = 1e-6
ROPE_THETA = 10000.0
Q_SCALE = HEAD_DIM ** -0.5
NEG = -1e30

ADAM_LR = 0.001
ADAM_B1 = 0.9
ADAM_B2 = 0.999
ADAM_EPS = 1e-08
ADAM_WD = 0.01
ADAM_STEP = 10
ADAM_C1 = 1.0 - ADAM_B1 ** ADAM_STEP
ADAM_C2 = 1.0 - ADAM_B2 ** ADAM_STEP

SEG = {}
_off = 0
for _name, _w in (("qa", 512), ("ka", 256), ("va", 256), ("qb", 512), ("kb", 512), ("vb", 512),
                  ("za", 512), ("zb", 512), ("ga", 1024), ("gb", 1024), ("f", 128)):
    SEG[_name] = (_off, _w)
    _off += _w
PROJ_W = _off
DPROJ_GROUPS = (("a", SEG["qa"][0], 1024), ("qb", SEG["qb"][0], 512), ("kb", SEG["kb"][0], 512),
                ("vb", SEG["vb"][0], 512), ("zg", SEG["za"][0], 3072), ("f", SEG["f"][0], 128))

VMEM_LIMIT = 56 * 1024 * 1024


def _params(n_axes):
    return pltpu.CompilerParams(dimension_semantics=("arbitrary",) * n_axes, vmem_limit_bytes=VMEM_LIMIT)


def _dot(a, b):
    return jnp.dot(a, b, preferred_element_type=F32)


def _dot_nt(a, b):
    return lax.dot_general(a, b, (((1,), (1,)), ((), ())), preferred_element_type=F32)


def _dot_tn(a, b):
    return lax.dot_general(a, b, (((0,), (0,)), ((), ())), preferred_element_type=F32)


def _lane(n):
    return lax.broadcasted_iota(jnp.int32, (1, n), 1)


def _split3(x):
    hi = x.astype(BF16).astype(F32)
    r = x - hi
    mid = r.astype(BF16).astype(F32)
    lo = (r - mid).astype(BF16).astype(F32)
    return hi, mid, lo


def _dot_split(x, b):
    hi, mid, lo = _split3(x)
    return _dot(hi.astype(BF16), b) + _dot(mid.astype(BF16), b) + _dot(lo.astype(BF16), b)


def _spread3(x, e3_ref):
    return _dot(jnp.concatenate(_split3(x), axis=1).astype(BF16), e3_ref[...])


def _place3(lane, base, x, other):
    hi, mid, lo = _split3(x)
    return jnp.where(lane == base, hi, jnp.where(lane == base + 1, mid, jnp.where(lane == base + 2, lo, other)))


def _lane_sum(x, lo, hi):
    lane = _lane(x.shape[1])
    return jnp.sum(jnp.where((lane >= lo) & (lane < hi), x, 0.0), axis=1, keepdims=True)


def _ones_lanes(lo, hi):
    lane = _lane(CHUNK)
    return jnp.where((lane >= lo) & (lane < hi), 1.0, 0.0).astype(F32)


def _rope(c, cos, sa, sb):
    return c * cos + pltpu.roll(c, CHUNK - 32, 1) * sa + pltpu.roll(c, 32, 1) * sb


def _rope_inv(d, cos, sa, sb):
    return d * cos - (pltpu.roll(d, CHUNK - 32, 1) * sa + pltpu.roll(d, 32, 1) * sb)


def _pair(c0, c1):
    return jnp.where(_lane(CHUNK) < HEAD_DIM, c0, pltpu.roll(c1, HEAD_DIM, 1))


def _sigmoid(x):
    return 0.5 * jnp.tanh(0.5 * x) + 0.5


N_CHIPS = 4
ANY_SPEC = pl.BlockSpec(memory_space=pl.ANY)


def _exchange(arrays, modes, name):
    n = len(arrays)

    def body(*refs):
        copies = _exchange_copies(refs[:n], refs[n:2 * n], modes, *refs[2 * n:])
        _exchange_start(copies)
        _exchange_wait(copies)

    return pl.pallas_call(
        body, name=name, out_shape=_exchange_out_shapes(arrays, modes),
        in_specs=[ANY_SPEC] * n, out_specs=[ANY_SPEC] * n, scratch_shapes=_exchange_sems(n),
    )(*arrays)


def _exchange_out_shapes(arrays, modes):
    return [jax.ShapeDtypeStruct((N_DEV,) + a.shape if md == "gather" else a.shape, a.dtype)
            for a, md in zip(arrays, modes)]


def _exchange_sems(n):
    return [pltpu.SemaphoreType.DMA((n, N_DEV - 1)), pltpu.SemaphoreType.DMA((n, N_DEV - 1)),
            pltpu.SemaphoreType.DMA((n,))]


def _exchange_copies(srcs, dsts, modes, send_sems, recv_sems, loc_sems):
    n = len(srcs)
    x, y, c = lax.axis_index("x"), lax.axis_index("y"), lax.axis_index("c")

    def slot(i, px, py, pc):
        return 2 * px + py if modes[i] == "chips" else 4 * px + 2 * py + pc

    def src_of(i, px, py, pc):
        return srcs[i] if modes[i] == "gather" else srcs[i].at[slot(i, px, py, pc)]

    local = [pltpu.make_async_copy(src_of(i, x, y, c), dsts[i].at[slot(i, x, y, c)], loc_sems.at[i])
             for i in range(n)]
    sends, recvs = [], []
    for r in (1, 2, 4, 3, 5, 6, 7):
        px = 1 - x if r & 4 else x
        py = 1 - y if r & 2 else y
        pc = 1 - c if r & 1 else c
        for i in range(n):
            if modes[i] == "chips" and r & 1:
                continue
            sems = dict(send_sem=send_sems.at[i, r - 1], recv_sem=recv_sems.at[i, r - 1],
                        device_id=(px, py, pc), device_id_type=pl.DeviceIdType.MESH)
            sends.append(pltpu.make_async_remote_copy(
                src_ref=src_of(i, px, py, pc), dst_ref=dsts[i].at[slot(i, x, y, c)], **sems))
            recvs.append(pltpu.make_async_remote_copy(
                src_ref=src_of(i, px, py, pc), dst_ref=dsts[i].at[slot(i, px, py, pc)], **sems))
    return local, sends, recvs


def _exchange_start(copies):
    local, sends, _ = copies
    for cp in local + sends:
        cp.start()


def _exchange_wait(copies):
    local, sends, recvs = copies
    for cp in recvs:
        cp.wait_recv()
    for cp in sends:
        cp.wait_send()
    for cp in local:
        cp.wait()


def _gather2_parts(srcs, dsts, send_sems, recv_sems, loc_sems):
    n = len(srcs)
    x, y, c = lax.axis_index("x"), lax.axis_index("y"), lax.axis_index("c")
    me, sibling = (x, y, c), (x, y, 1 - c)
    chips = [(1 - x, y), (x, 1 - y), (1 - x, 1 - y)]

    def rows(i, dev):
        return dsts[i].at[4 * dev[0] + 2 * dev[1] + dev[2]]

    def copy(i, k, block, to, src=None):
        return pltpu.make_async_remote_copy(
            src_ref=rows(i, block) if src is None else src, dst_ref=rows(i, block),
            send_sem=send_sems.at[i, k], recv_sem=recv_sems.at[i, k],
            device_id=to, device_id_type=pl.DeviceIdType.MESH)

    local = [pltpu.make_async_copy(srcs[i], rows(i, me), loc_sems.at[i]) for i in range(n)]
    first = []
    for i in range(n):
        first.append(copy(i, 0, me, sibling, src=srcs[i]))
        first += [copy(i, 1 + j, me, (*chip, c), src=srcs[i]) for j, chip in enumerate(chips)]
    return n, c, me, sibling, chips, copy, local, first


def _gather2_start(srcs, dsts, send_sems, recv_sems, loc_sems):
    *_, local, first = _gather2_parts(srcs, dsts, send_sems, recv_sems, loc_sems)
    for cp in local + first:
        cp.start()


def _gather2_finish(srcs, dsts, send_sems, recv_sems, loc_sems):
    n, c, me, sibling, chips, copy, local, first = _gather2_parts(srcs, dsts, send_sems, recv_sems, loc_sems)
    passed = []
    for j, chip in enumerate(chips):
        for i in range(n):
            copy(i, 1 + j, (*chip, c), me).wait_recv()
            cp = copy(i, 4 + j, (*chip, c), sibling)
            cp.start()
            passed.append(cp)
    for i in range(n):
        copy(i, 0, sibling, me).wait_recv()
        for j, chip in enumerate(chips):
            copy(i, 4 + j, (*chip, 1 - c), me).wait_recv()
    for cp in first + passed:
        cp.wait_send()
    for cp in local:
        cp.wait()


def _sibling_copies(srcs, dsts, send_sems, recv_sems):
    x, y, c = lax.axis_index("x"), lax.axis_index("y"), lax.axis_index("c")
    return [pltpu.make_async_remote_copy(
        src_ref=srcs[i].at[1 - c, k], dst_ref=dsts[i].at[k], send_sem=send_sems.at[i, k], recv_sem=recv_sems.at[i, k],
        device_id=(x, y, 1 - c), device_id_type=pl.DeviceIdType.MESH)
        for i in range(len(srcs)) for k in range(N_CHIPS)]


def _token_matmuls_and_swap(pairs, tk, swap):
    npair, nsw = len(pairs), len(swap)
    s = pairs[0][0].shape[1]
    last = s // tk - 1

    def body(*refs):
        a_refs, b_refs = refs[0:2 * npair:2], refs[1:2 * npair:2]
        src = refs[2 * npair:2 * npair + nsw]
        o_refs = refs[2 * npair + nsw:3 * npair + nsw]
        dst = refs[3 * npair + nsw:3 * npair + 2 * nsw]
        sems = refs[3 * npair + 2 * nsw:]

        @pl.when(pl.program_id(0) == 0)
        def _():
            for cp in _sibling_copies(src, dst, *sems):
                cp.start()

        for a_ref, b_ref, o_ref in zip(a_refs, b_refs, o_refs):
            @pl.when(pl.program_id(0) == 0)
            def _():
                o_ref[...] = jnp.zeros_like(o_ref)

            o_ref[...] += _dot(a_ref[...], b_ref[...])

        @pl.when(pl.program_id(0) == last)
        def _():
            for cp in _sibling_copies(src, dst, *sems):
                cp.wait()

    in_specs, args = [], []
    for a_t, b in pairs:
        in_specs += [pl.BlockSpec((a_t.shape[0], tk), lambda k: (0, k)), pl.BlockSpec((tk, b.shape[1]), lambda k: (k, 0))]
        args += [a_t, b]
    outs = pl.pallas_call(
        body, name="dw_out_projs_swap", grid=(s // tk,),
        out_shape=[jax.ShapeDtypeStruct((a_t.shape[0], b.shape[1]), F32) for a_t, b in pairs]
        + [jax.ShapeDtypeStruct(a.shape[1:], a.dtype) for a in swap],
        in_specs=in_specs + [ANY_SPEC] * nsw,
        out_specs=[pl.BlockSpec((a_t.shape[0], b.shape[1]), lambda k: (0, 0)) for a_t, b in pairs] + [ANY_SPEC] * nsw,
        scratch_shapes=[pltpu.SemaphoreType.DMA((nsw, N_CHIPS)), pltpu.SemaphoreType.DMA((nsw, N_CHIPS))],
        compiler_params=_params(1))(*args, *swap)
    return outs[:npair], outs[npair:]


def _chip_partial(core, slots, theirs, tr, name):
    _, k, rws, cols = slots.shape

    def body(core_ref, a_ref, b_ref, f_ref, h_ref):
        sm = a_ref[0] + b_ref[...]
        f_ref[...] = sm
        h_ref[...] = sm.astype(BF16)

    blk = pl.BlockSpec((1, tr, cols), lambda j, i, core_ref: (j, i, 0))
    mine = pl.BlockSpec((1, 1, tr, cols), lambda j, i, core_ref: (core_ref[0], j, i, 0))
    return pl.pallas_call(
        body, name=name,
        grid_spec=pltpu.PrefetchScalarGridSpec(num_scalar_prefetch=1, grid=(k, rws // tr),
                                               in_specs=[mine, blk], out_specs=[blk, blk]),
        out_shape=[jax.ShapeDtypeStruct(theirs.shape, F32), jax.ShapeDtypeStruct(theirs.shape, BF16)],
        compiler_params=_params(2))(core, slots, theirs)


def _ada_fwd(c_all, w_shard, b_shard):
    def body(c_ref, w_ref, b_ref, o_ref):
        ch, cm, cl = [t.astype(BF16) for t in _split3(c_ref[...])]
        wh, wm, wl = [t.astype(BF16) for t in _split3(w_ref[...])]
        acc = _dot(ch, wh) + _dot(ch, wm) + _dot(cm, wh) + _dot(ch, wl) + _dot(cl, wh) + _dot(cm, wm)
        o_ref[...] = acc + b_ref[...]

    return pl.pallas_call(body, name="ada_fwd", out_shape=jax.ShapeDtypeStruct((N_DEV, ADA_SHARD), F32),
                          compiler_params=_params(0))(c_all, w_shard, b_shard)


def _rope_tables_and_gather(pos_col, inv_freq, tm, weights, c_rows):
    s = pos_col.shape[0]
    nw = len(weights)
    last = s // tm - 1

    def body(*refs):
        p_ref, f_ref = refs[:2]
        w_src, c_src = refs[2:2 + nw], refs[2 + nw:3 + nw]
        cos_ref, sa_ref, sb_ref = refs[3 + nw:6 + nw]
        w_dst, c_dst = refs[6 + nw:6 + 2 * nw], refs[6 + 2 * nw:7 + 2 * nw]
        w_sems, c_sems = refs[7 + 2 * nw:10 + 2 * nw], refs[10 + 2 * nw:]

        @pl.when(pl.program_id(0) == 0)
        def _():
            _gather2_start(w_src, w_dst, *w_sems)
            _exchange_start(_exchange_copies(c_src, c_dst, ["gather"], *c_sems))

        ang = p_ref[...] * f_ref[...]
        sin = jnp.sin(ang)
        first_half = (_lane(CHUNK) & (HEAD_DIM - 1)) < HEAD_DIM // 2
        cos_ref[...] = jnp.cos(ang)
        sa_ref[...] = jnp.where(first_half, -sin, 0.0)
        sb_ref[...] = jnp.where(first_half, 0.0, sin)

        @pl.when(pl.program_id(0) == last)
        def _():
            _exchange_wait(_exchange_copies(c_src, c_dst, ["gather"], *c_sems))
            _gather2_finish(w_src, w_dst, *w_sems)

    tab = jax.ShapeDtypeStruct((s, CHUNK), F32)
    blk = pl.BlockSpec((tm, CHUNK), lambda i: (i, 0))
    gathered = [jax.ShapeDtypeStruct((N_DEV,) + a.shape, a.dtype) for a in list(weights) + [c_rows]]
    outs = pl.pallas_call(
        body, name="rope_tables_gather", grid=(s // tm,), out_shape=[tab, tab, tab] + gathered,
        in_specs=[pl.BlockSpec((tm, 1), lambda i: (i, 0)), pl.BlockSpec((1, CHUNK), lambda i: (0, 0))]
        + [ANY_SPEC] * (nw + 1),
        out_specs=[blk, blk, blk] + [ANY_SPEC] * (nw + 1),
        scratch_shapes=_exchange_sems(nw) + _exchange_sems(1),
        compiler_params=_params(1))(pos_col, inv_freq, *weights, c_rows)
    return outs[:3], outs[3:3 + nw], outs[3 + nw]


def _fwd_proj(x, mod, w_all, cos, sa, sb, bf_row, emat, ind, tm):
    s = x.shape[0]

    def body(x_ref, mod_ref, w_ref, cos_ref, sa_ref, sb_ref, bf_ref, e_ref, ind_ref,
             h_ref, qa_ref, ka_ref, va_ref, za_ref, qb_ref, kb_ref, vb_ref, zb_ref, ga_ref, gb_ref, f_ref,
             bnd_ref, nrm_ref, carry_ref):
        @pl.when(pl.program_id(0) == 0)
        def _():
            carry_ref[...] = jnp.zeros_like(carry_ref)

        xv = x_ref[...]
        rstd = lax.rsqrt(jnp.mean(xv * xv, axis=-1, keepdims=True) + NORM_EPS)
        h = (xv * rstd * mod_ref[0:1, :]) * (1.0 + mod_ref[1:2, :]) + mod_ref[2:3, :]
        hb = h.astype(BF16)
        h_ref[...] = hb.T

        def seg(name):
            off, w = SEG[name]
            return _dot(hb, w_ref[:, off:off + w])

        lane = _lane(CHUNK)
        low = lane < HEAD_DIM
        q_ones = _ones_lanes(64, 67)
        k_ones = _ones_lanes(67, 70)
        cos_t, sa_t, sb_t = cos_ref[...], sa_ref[...], sb_ref[...]

        def write_heads(ref, nat, extra_of, rope, scale):
            for p in range(N_HEADS // 2):
                c = nat[:, CHUNK * p:CHUNK * (p + 1)]
                if rope:
                    c = _rope(c, cos_t, sa_t, sb_t)
                if scale != 1.0:
                    c = c * scale
                for hh in range(2):
                    hd = 2 * p + hh
                    src = c if hh == 0 else pltpu.roll(c, HEAD_DIM, 1)
                    ref[:, CHUNK * hd:CHUNK * (hd + 1)] = jnp.where(low, src, extra_of(hd)).astype(BF16)

        write_heads(qa_ref, seg("qa"), lambda hd: q_ones, True, Q_SCALE)
        ka = seg("ka")
        va = seg("va")
        for kv in range(KV_GROUPS):
            sl = slice(CHUNK * kv, CHUNK * (kv + 1))
            ka_ref[:, sl] = jnp.where(low, _rope(ka[:, sl], cos_t, sa_t, sb_t), k_ones).astype(BF16)
            va_ref[:, sl] = jnp.where(low, va[:, sl], q_ones).astype(BF16)
        za_ref[...] = seg("za")
        zb_ref[...] = seg("zb")
        ga_ref[...] = seg("ga")
        gb_ref[...] = seg("gb")

        xf = seg("f") + bf_ref[...]
        f_ref[...] = xf
        logf = jnp.minimum(xf, 0.0) - jnp.log1p(jnp.exp(-jnp.abs(xf)))
        row = lax.broadcasted_iota(jnp.int32, (tm, tm), 0)
        col = lax.broadcasted_iota(jnp.int32, (tm, tm), 1)
        tri = jnp.where(col <= row, 1.0, 0.0).astype(BF16)
        hi, mid, lo = _split3(logf)
        cum = _dot(tri, hi.astype(BF16)) + _dot(tri, mid.astype(BF16)) + _dot(tri, lo.astype(BF16))
        cum = cum + carry_ref[0:1, :]
        carry_ref[0:1, :] += jnp.sum(logf, axis=0, keepdims=True)
        lane_all = _lane(AUG_W) & (CHUNK - 1)
        bias = _spread3(-cum, e_ref) + jnp.where((lane_all >= 67) & (lane_all < 70), 1.0, 0.0)

        qb, kb = seg("qb"), seg("kb")
        write_heads(qb_ref, qb, lambda hd: q_ones, False, Q_SCALE)
        write_heads(kb_ref, kb, lambda hd: bias[:, CHUNK * hd:CHUNK * (hd + 1)], False, 1.0)
        write_heads(vb_ref, seg("vb"), lambda hd: q_ones, False, 1.0)

        rid = lax.broadcasted_iota(jnp.int32, (tm, CHUNK), 0)
        bnd_ref[...] = jnp.zeros_like(bnd_ref)
        bnd_ref[0, 0:1, :] = jnp.sum(jnp.where(rid == 0, cum, 0.0), axis=0, keepdims=True)
        bnd_ref[0, 1:2, :] = jnp.sum(jnp.where(rid == tm - 1, cum, 0.0), axis=0, keepdims=True)

        @pl.when(pl.program_id(0) == 0)
        def _():
            nrm_ref[...] = jnp.zeros_like(nrm_ref)

        for r_, nat in ((0, qb * Q_SCALE), (1, kb)):
            sq = _dot((nat * nat).astype(BF16), ind_ref[...])
            nrm_ref[r_:r_ + 1, :] = jnp.maximum(nrm_ref[r_:r_ + 1, :], jnp.max(sq, axis=0, keepdims=True))

    row_blk = lambda w: pl.BlockSpec((tm, w), lambda i: (i, 0))
    full = lambda a: pl.BlockSpec(a.shape, lambda i: (0,) * a.ndim)
    sds = lambda w, dt: jax.ShapeDtypeStruct((s, w), dt)
    out_shape = [sds(AUG_W, BF16), sds(KV_GROUPS * CHUNK, BF16), sds(KV_GROUPS * CHUNK, BF16),
                 sds(512, F32), sds(AUG_W, BF16), sds(AUG_W, BF16), sds(AUG_W, BF16), sds(512, F32),
                 sds(D_MODEL, F32), sds(D_MODEL, F32), sds(CHUNK, F32)]
    small = [jax.ShapeDtypeStruct((s // tm, 8, CHUNK), F32), jax.ShapeDtypeStruct((8, CHUNK), F32)]
    return pl.pallas_call(
        body, name="fwd_proj", grid=(s // tm,),
        out_shape=[jax.ShapeDtypeStruct((D_MODEL, s), BF16)] + out_shape + small,
        in_specs=[row_blk(D_MODEL), full(mod), full(w_all), row_blk(CHUNK), row_blk(CHUNK), row_blk(CHUNK),
                  full(bf_row), full(emat), full(ind)],
        out_specs=[pl.BlockSpec((D_MODEL, tm), lambda i: (0, i))] + [row_blk(o.shape[1]) for o in out_shape] + [
            pl.BlockSpec((1, 8, CHUNK), lambda i: (i, 0, 0)), pl.BlockSpec((8, CHUNK), lambda i: (0, 0))],
        scratch_shapes=[pltpu.VMEM((8, CHUNK), F32)],
        compiler_params=_params(1))(x, mod, w_all, cos, sa, sb, bf_row, emat, ind)


def _swa_fwd(q_aug, k_aug, v_aug, sink_rows, tq):
    s = q_aug.shape[0]
    r = tq // WINDOW
    gw = GROUP * CHUNK

    def body(q_ref, kc_ref, kp_ref, vc_ref, vp_ref, sink_ref, o_ref, qb_ref):
        i = pl.program_id(1)
        st = GROUP * WINDOW
        qloc = lax.broadcasted_iota(jnp.int32, (st, 2 * WINDOW), 0) & (WINDOW - 1)
        col = lax.broadcasted_iota(jnp.int32, (st, 2 * WINDOW), 1)
        band = (col > qloc) & (col <= qloc + WINDOW)
        head = jnp.right_shift(lax.broadcasted_iota(jnp.int32, (st, 1), 0), 7)
        sink = jnp.zeros((st, 1), F32)
        for g in range(GROUP):
            sink = jnp.where(head == g, jnp.max(sink_ref[0, g:g + 1, :], axis=1, keepdims=True), sink)
        lane = _lane(CHUNK)
        for sub in range(r):
            rows = slice(WINDOW * sub, WINDOW * (sub + 1))
            q = jnp.concatenate([q_ref[rows, CHUNK * g:CHUNK * (g + 1)] for g in range(GROUP)], axis=0)
            if sub == 0:
                k = jnp.concatenate([kp_ref[...], kc_ref[rows, :]], axis=0)
                v = jnp.concatenate([vp_ref[...], vc_ref[rows, :]], axis=0)
                valid = band & ((col >= WINDOW) | (i > 0))
            else:
                both = slice(WINDOW * (sub - 1), WINDOW * (sub + 1))
                k, v, valid = kc_ref[both, :], vc_ref[both, :], band
            sc = jnp.where(valid, _dot_nt(q, k), NEG)
            m = jnp.maximum(jnp.max(sc, axis=1, keepdims=True), sink)
            acc = _dot(jnp.exp(sc - m).astype(BF16), v)
            denom = _lane_sum(acc, 64, 65) + jnp.exp(sink - m)
            out = acc / denom
            aug = _place3(lane, 67, -(m + jnp.log(denom)), q.astype(F32)).astype(BF16)
            hrows = lambda a, g: a[WINDOW * g:WINDOW * (g + 1), :]
            for g in range(GROUP):
                qb_ref[rows, CHUNK * g:CHUNK * (g + 1)] = hrows(aug, g)
            for pp in range(GROUP // 2):
                o_ref[rows, CHUNK * pp:CHUNK * (pp + 1)] = _pair(hrows(out, 2 * pp), hrows(out, 2 * pp + 1))

    return pl.pallas_call(
        body, name="swa_fwd", grid=(KV_GROUPS, s // tq),
        out_shape=[jax.ShapeDtypeStruct((s, 512), F32), jax.ShapeDtypeStruct((s, AUG_W), BF16)],
        in_specs=[pl.BlockSpec((tq, gw), lambda kv, i: (i, kv)),
                  pl.BlockSpec((tq, CHUNK), lambda kv, i: (i, kv)),
                  pl.BlockSpec((WINDOW, CHUNK), lambda kv, i: (jnp.maximum(i * r - 1, 0), kv)),
                  pl.BlockSpec((tq, CHUNK), lambda kv, i: (i, kv)),
                  pl.BlockSpec((WINDOW, CHUNK), lambda kv, i: (jnp.maximum(i * r - 1, 0), kv)),
                  pl.BlockSpec((1, 8, CHUNK), lambda kv, i: (kv, 0, 0))],
        out_specs=[pl.BlockSpec((tq, GROUP * HEAD_DIM), lambda kv, i: (i, kv)),
                   pl.BlockSpec((tq, gw), lambda kv, i: (i, kv))],
        compiler_params=_params(2))(q_aug, k_aug, k_aug, v_aug, v_aug, sink_rows)


def _swa_bwd(k_aug, v_aug, q_bwd, do_aug, cos, sa, sb, sink_rows, tk):
    s = k_aug.shape[0]
    r = tk // WINDOW
    nt = s // tk
    nb = s // WINDOW
    gw = GROUP * CHUNK

    def body(k_ref, v_ref, q_ref, qn_ref, do_ref, don_ref, cos_ref, sa_ref, sb_ref, sink_ref,
             da_ref, dsink_ref, carry_ref, acc_ref):
        j = pl.program_id(0)

        @pl.when(j == 0)
        def _():
            carry_ref[...] = jnp.zeros_like(carry_ref)
            dsink_ref[...] = jnp.zeros_like(dsink_ref)

        acc_ref[...] = jnp.zeros_like(acc_ref)
        acc_ref[0:WINDOW, :] = carry_ref[...]
        dk_col, dv_col = SEG["ka"][0] - SEG["qa"][0], SEG["va"][0] - SEG["qa"][0]
        st = GROUP * WINDOW
        key = lax.broadcasted_iota(jnp.int32, (WINDOW, 2 * st), 0)
        col = lax.broadcasted_iota(jnp.int32, (WINDOW, 2 * st), 1)
        qloc = col & (WINDOW - 1)
        band = ((col < st) & (key <= qloc)) | ((col >= st) & (key > qloc))
        lane = _lane(CHUNK)
        low = lane < HEAD_DIM
        cos_t, sa_t, sb_t = cos_ref[...], sa_ref[...], sb_ref[...]
        for kv, sub in [(kv, sub) for kv in range(KV_GROUPS) for sub in range(r)]:
            heads = [slice(CHUNK * (GROUP * kv + g), CHUNK * (GROUP * kv + g + 1)) for g in range(GROUP)]
            kvs = slice(CHUNK * kv, CHUNK * (kv + 1))
            rows = slice(WINDOW * sub, WINDOW * (sub + 1))
            nxt = slice(WINDOW * (sub + 1), WINDOW * (sub + 2))
            k, v = k_ref[rows, kvs], v_ref[rows, kvs]
            q_cur, do_cur = [q_ref[rows, cs] for cs in heads], [do_ref[rows, cs] for cs in heads]
            if sub < r - 1:
                q_nxt, do_nxt, valid = [q_ref[nxt, cs] for cs in heads], [do_ref[nxt, cs] for cs in heads], band
            else:
                q_nxt, do_nxt = [qn_ref[:, cs] for cs in heads], [don_ref[:, cs] for cs in heads]
                valid = band & ((col < st) | (j < nt - 1))
            q = jnp.concatenate(q_cur + q_nxt, axis=0)
            do = jnp.concatenate(do_cur + do_nxt, axis=0)
            pt = jnp.exp(jnp.where(valid, _dot_nt(k, q), NEG))
            ds = (pt * _dot_nt(v, do)).astype(BF16)
            dv = _dot(pt.astype(BF16), do)
            dk = _dot(ds, q)
            dq = _dot_tn(ds, k)
            for g, cs in enumerate(heads):
                acc_ref[rows, cs] += dq[WINDOW * g:WINDOW * (g + 1), :]
                dqn = dq[st + WINDOW * g:st + WINDOW * (g + 1), :]
                if sub < r - 1:
                    acc_ref[nxt, cs] += dqn
                else:
                    carry_ref[:, cs] = dqn
                sink = jnp.max(sink_ref[kv, g:g + 1, :], axis=1, keepdims=True)
                p_sink = jnp.exp(sink + _lane_sum(q_cur[g].astype(F32), 67, 70))
                term = jnp.sum(p_sink * _lane_sum(do_cur[g].astype(F32), 64, 67), axis=0, keepdims=True)
                dsink_ref[kv, g:g + 1, :] += jnp.broadcast_to(term, (1, CHUNK))
            da_ref[rows, dk_col + CHUNK * kv:dk_col + CHUNK * (kv + 1)] = _rope_inv(
                jnp.where(low, dk, 0.0), cos_t[rows, :], sa_t[rows, :], sb_t[rows, :]).astype(BF16)
            da_ref[rows, dv_col + CHUNK * kv:dv_col + CHUNK * (kv + 1)] = jnp.where(low, dv, 0.0).astype(BF16)
        for pp in range(N_HEADS // 2):
            d = _pair(acc_ref[:, CHUNK * 2 * pp:CHUNK * (2 * pp + 1)], acc_ref[:, CHUNK * (2 * pp + 1):CHUNK * (2 * pp + 2)])
            da_ref[:, CHUNK * pp:CHUNK * (pp + 1)] = (_rope_inv(d, cos_t, sa_t, sb_t) * Q_SCALE).astype(BF16)

    cur = lambda w: pl.BlockSpec((tk, w), lambda j: (j, 0))
    nxt = pl.BlockSpec((WINDOW, AUG_W), lambda j: (jnp.minimum((j + 1) * r, nb - 1), 0))
    whole = pl.BlockSpec((KV_GROUPS, 8, CHUNK), lambda j: (0, 0, 0))
    return pl.pallas_call(
        body, name="swa_bwd", grid=(nt,),
        out_shape=[jax.ShapeDtypeStruct((s, 1024), BF16), jax.ShapeDtypeStruct((KV_GROUPS, 8, CHUNK), F32)],
        in_specs=[cur(KV_GROUPS * CHUNK), cur(KV_GROUPS * CHUNK), cur(AUG_W), nxt, cur(AUG_W), nxt,
                  cur(CHUNK), cur(CHUNK), cur(CHUNK), whole],
        out_specs=[cur(1024), whole],
        scratch_shapes=[pltpu.VMEM((WINDOW, AUG_W), F32), pltpu.VMEM((tk, AUG_W), F32)],
        compiler_params=_params(1))(k_aug, v_aug, q_bwd, q_bwd, do_aug, do_aug, cos, sa, sb, sink_rows)


def _fox_fwd(first_key, q_aug, k_aug, v_aug, t):
    s = q_aug.shape[0]
    pw = 2 * CHUNK

    def body(lo_ref, q_ref, k_ref, v_ref, o_ref, qb_ref):
        i = pl.program_id(1)
        first = lo_ref[pl.program_id(0), i]
        row = lax.broadcasted_iota(jnp.int32, (t, t), 0)
        col = lax.broadcasted_iota(jnp.int32, (t, t), 1)
        lane = _lane(CHUNK)
        heads = (slice(0, CHUNK), slice(CHUNK, pw))
        qs = [q_ref[:, hs] for hs in heads]

        def step(jb, carry, masked):
            rows = pl.ds(pl.multiple_of(jb * t, t), t)
            new = []
            for hh, hs in enumerate(heads):
                m, acc = carry[2 * hh], carry[2 * hh + 1]
                sc = _dot_nt(qs[hh], k_ref[rows, hs])
                if masked:
                    sc = jnp.where(col <= row, sc, NEG)
                m_new = jnp.maximum(m, jnp.max(sc, axis=1, keepdims=True))
                p = jnp.exp(sc - m_new).astype(BF16)
                new += [m_new, jnp.exp(m - m_new) * acc + _dot(p, v_ref[rows, hs])]
            return tuple(new)

        init = (jnp.full((t, 1), NEG, F32), jnp.zeros((t, CHUNK), F32)) * 2
        carry = step(i, lax.fori_loop(first, i, lambda jb, c: step(jb, c, False), init), True)
        outs = []
        for hh, hs in enumerate(heads):
            m, acc = carry[2 * hh], carry[2 * hh + 1]
            denom = _lane_sum(acc, 64, 65)
            outs.append(acc / denom)
            lse = m + jnp.log(denom)
            qb_ref[:, hs] = _place3(lane, 67, -lse, qs[hh].astype(F32)).astype(BF16)
        o_ref[...] = _pair(outs[0], outs[1])

    return pl.pallas_call(
        body, name="fox_fwd", grid=(N_HEADS // 2, s // t),
        out_shape=[jax.ShapeDtypeStruct((s, 512), F32), jax.ShapeDtypeStruct((s, AUG_W), BF16)],
        in_specs=[pl.BlockSpec(memory_space=pltpu.SMEM),
                  pl.BlockSpec((t, pw), lambda p, i: (i, p)),
                  pl.BlockSpec((s, pw), lambda p, i: (0, p)),
                  pl.BlockSpec((s, pw), lambda p, i: (0, p))],
        out_specs=[pl.BlockSpec((t, CHUNK), lambda p, i: (i, p)), pl.BlockSpec((t, pw), lambda p, i: (i, p))],
        compiler_params=_params(2))(first_key, q_aug, k_aug, v_aug)


def _fox_block_ranges(bounds, norms, r):
    cum_first = bounds[0::r, 0, :N_HEADS]
    cum_last = bounds[r - 1::r, 1, :N_HEADS]
    n = cum_first.shape[0]
    reach = 2.0 * 1.02 * jnp.sqrt(norms[0, :N_HEADS] * norms[1, :N_HEADS]) + 105.0
    blk = jnp.arange(n, dtype=jnp.int32)
    decay = cum_last[None, :, :] - cum_first[:, None, :]
    skip = (decay > reach[None, None, :]) & (blk[None, :, None] < blk[:, None, None])
    first = jnp.min(jnp.where(skip, n, blk[None, :, None]), axis=1)
    first = jnp.minimum(first[:, 0::2], first[:, 1::2]).T
    needed = (first[:, :, None] <= blk[None, None, :]) & (blk[None, :, None] >= blk[None, None, :])
    last = jnp.max(jnp.where(needed, blk[None, :, None], blk[None, None, :]), axis=1)
    return first.astype(jnp.int32), last.astype(jnp.int32)


def _fox_bwd(last_query, k_aug, v_aug, q_bwd, do_aug, t):
    s = k_aug.shape[0]
    n = s // t
    pw = 2 * CHUNK

    def body(hi_ref, k_ref, v_ref, q_ref, do_ref, dk_ref, dv_ref, dck_ref, dq_ref, dcq_ref, dq_scr):
        j = pl.program_id(1)
        last = hi_ref[pl.program_id(0), j]

        @pl.when(j == 0)
        def _():
            dq_scr[...] = jnp.zeros_like(dq_scr)

        row = lax.broadcasted_iota(jnp.int32, (t, t), 0)
        col = lax.broadcasted_iota(jnp.int32, (t, t), 1)
        lane = _lane(CHUNK)
        heads = (slice(0, CHUNK), slice(CHUNK, pw))
        ks = [k_ref[:, hs] for hs in heads]
        vs = [v_ref[:, hs] for hs in heads]

        def step(ib, carry, masked):
            rows = pl.ds(pl.multiple_of(ib * t, t), t)
            new = []
            for hh, hs in enumerate(heads):
                dv, dk = carry[2 * hh], carry[2 * hh + 1]
                q, do = q_ref[rows, hs], do_ref[rows, hs]
                st = _dot_nt(ks[hh], q)
                if masked:
                    st = jnp.where(row <= col, st, NEG)
                pt = jnp.exp(st)
                ds = (pt * _dot_nt(vs[hh], do)).astype(BF16)
                new += [dv + _dot(pt.astype(BF16), do), dk + _dot(ds, q)]
                dq_scr[rows, hs] += _dot_tn(ds, ks[hh])
            return tuple(new)

        zero = jnp.zeros((t, CHUNK), F32)
        carry = lax.fori_loop(j + 1, last + 1, lambda ib, c: step(ib, c, False), step(j, (zero,) * 4, True))
        dvs, dks = (carry[0], carry[2]), (carry[1], carry[3])
        dk_ref[...] = _pair(dks[0], dks[1]).astype(BF16)
        dv_ref[...] = _pair(dvs[0], dvs[1]).astype(BF16)
        dck_ref[0] = jnp.where(lane == 0, pltpu.roll(dks[0], 64, 1),
                               jnp.where(lane == 1, pltpu.roll(dks[1], 65, 1), 0.0))

        @pl.when(j == n - 1)
        def _():
            for ib in range(n):
                rows = slice(t * ib, t * (ib + 1))
                d0, d1 = dq_scr[rows, 0:CHUNK], dq_scr[rows, CHUNK:pw]
                dq_ref[rows, :] = (_pair(d0, d1) * Q_SCALE).astype(BF16)
                dcq_ref[0, rows, :] = jnp.where(lane == 0, pltpu.roll(d0, CHUNK - 67, 1),
                                                jnp.where(lane == 1, pltpu.roll(d1, CHUNK - 66, 1), 0.0))

    return pl.pallas_call(
        body, name="fox_bwd", grid=(N_HEADS // 2, n),
        out_shape=[jax.ShapeDtypeStruct((s, 512), BF16), jax.ShapeDtypeStruct((s, 512), BF16),
                   jax.ShapeDtypeStruct((N_HEADS // 2, s, CHUNK), F32), jax.ShapeDtypeStruct((s, 512), BF16),
                   jax.ShapeDtypeStruct((N_HEADS // 2, s, CHUNK), F32)],
        in_specs=[pl.BlockSpec(memory_space=pltpu.SMEM),
                  pl.BlockSpec((t, pw), lambda p, j: (j, p)), pl.BlockSpec((t, pw), lambda p, j: (j, p)),
                  pl.BlockSpec((s, pw), lambda p, j: (0, p)), pl.BlockSpec((s, pw), lambda p, j: (0, p))],
        out_specs=[pl.BlockSpec((t, CHUNK), lambda p, j: (j, p)), pl.BlockSpec((t, CHUNK), lambda p, j: (j, p)),
                   pl.BlockSpec((1, t, CHUNK), lambda p, j: (p, j, 0)), pl.BlockSpec((s, CHUNK), lambda p, j: (0, p)),
                   pl.BlockSpec((1, s, CHUNK), lambda p, j: (p, 0, 0))],
        scratch_shapes=[pltpu.VMEM((s, pw), F32)],
        compiler_params=_params(2))(last_query, k_aug, v_aug, q_bwd, do_aug)


def _fgate_bwd(dcq, dck, xf, tm):
    s = xf.shape[0]
    nt = s // tm

    def body(dq_ref, dc_ref, xf_ref, df_ref, dbf_ref, carry_ref):
        @pl.when(pl.program_id(0) == 0)
        def _():
            carry_ref[...] = jnp.zeros_like(carry_ref)
            dbf_ref[...] = jnp.zeros_like(dbf_ref)

        row = lax.broadcasted_iota(jnp.int32, (tm, tm), 0)
        col = lax.broadcasted_iota(jnp.int32, (tm, tm), 1)
        tri = jnp.where(col >= row, 1.0, 0.0).astype(BF16)
        dcum = dq_ref[0] - dc_ref[0]
        for p in range(1, N_HEADS // 2):
            dcum = dcum + pltpu.roll(dq_ref[p] - dc_ref[p], 2 * p, 1)
        hi, mid, lo = _split3(dcum)
        dlogf = _dot(tri, hi.astype(BF16)) + _dot(tri, mid.astype(BF16)) + _dot(tri, lo.astype(BF16))
        dlogf = dlogf + carry_ref[0:1, :]
        carry_ref[0:1, :] += jnp.sum(dcum, axis=0, keepdims=True)
        df = dlogf * _sigmoid(-xf_ref[...])
        df_ref[...] = df.astype(BF16)
        dbf_ref[0:1, :] += jnp.sum(df, axis=0, keepdims=True)

    rev = pl.BlockSpec((tm, CHUNK), lambda i: (nt - 1 - i, 0))
    rev4 = pl.BlockSpec((N_HEADS // 2, tm, CHUNK), lambda i: (0, nt - 1 - i, 0))
    return pl.pallas_call(
        body, name="fgate_bwd", grid=(nt,),
        out_shape=[jax.ShapeDtypeStruct((s, CHUNK), BF16), jax.ShapeDtypeStruct((8, CHUNK), F32)],
        in_specs=[rev4, rev4, rev], out_specs=[rev, pl.BlockSpec((8, CHUNK), lambda i: (0, 0))],
        scratch_shapes=[pltpu.VMEM((8, CHUNK), F32)],
        compiler_params=_params(1))(dcq, dck, xf)


def _post(x, tgt, att_a, att_b, za, zb, ga, gb, woa_t, wob_t, wout, vec, ind, emat, tm):
    s = x.shape[0]

    def body(x_ref, t_ref, aa_ref, ab_ref, za_ref, zb_ref, ga_ref, gb_ref, woat_ref, wobt_ref, wout_ref,
             vec_ref, ind_ref, e_ref,
             dx1_ref, ua_ref, ub_ref, mg_ref, do_ref, dya_ref, dyb_ref, doa_ref, dob_ref, dzg_ref, acc_ref):
        @pl.when(pl.program_id(0) == 0)
        def _():
            acc_ref[...] = jnp.zeros_like(acc_ref)

        gate, gfin = vec_ref[0:1, :], vec_ref[1:2, :]
        inv_d = 1.0 / D_MODEL

        def branch_fwd(att_ref, z_ref, w_ref, u_ref):
            att, z = att_ref[...], z_ref[...]
            sz = _sigmoid(z)
            silu = z * sz
            u = (att * silu).astype(BF16)
            u_ref[...] = u.T
            return att, z, sz, silu, _dot_nt(u, w_ref[...])

        att_a, z_a, sz_a, silu_a, y_a = branch_fwd(aa_ref, za_ref, woat_ref, ua_ref)
        att_b, z_b, sz_b, silu_b, y_b = branch_fwd(ab_ref, zb_ref, wobt_ref, ub_ref)
        sg_a, sg_b = _sigmoid(ga_ref[...]), _sigmoid(gb_ref[...])
        merged = (sg_a * y_a + sg_b * y_b).astype(BF16)
        mg_ref[...] = merged.T
        o = _dot(merged, wout_ref[...])
        x1 = x_ref[...] + gate * o
        rstd = lax.rsqrt(jnp.mean(x1 * x1, axis=-1, keepdims=True) + NORM_EPS)
        xh = x1 * rstd
        diff = xh * gfin - t_ref[...]
        acc_ref[2:3, :] += (0.5 * inv_d) * jnp.sum(diff * diff, axis=0, keepdims=True)
        dy = diff * inv_d
        acc_ref[1:2, :] += jnp.sum(dy * xh, axis=0, keepdims=True)
        dyg = dy * gfin
        dx1 = rstd * (dyg - xh * jnp.mean(dyg * xh, axis=-1, keepdims=True))
        dx1_ref[...] = dx1
        acc_ref[0:1, :] += jnp.sum(dx1 * o, axis=0, keepdims=True)
        d_o = (dx1 * gate).astype(BF16)
        do_ref[...] = d_o
        dmg = _dot_nt(d_o, wout_ref[...])

        lane = _lane(CHUNK)
        low = lane < HEAD_DIM
        lane_all = _lane(AUG_W) & (CHUNK - 1)

        zg0 = SEG["za"][0]

        def branch_bwd(sg, y, att, z, sz, silu, wt_ref, dy_ref, g_name, z_name, doaug_ref):
            dyb = (dmg * sg).astype(BF16)
            dy_ref[...] = dyb
            g_off, z_off = SEG[g_name][0] - zg0, SEG[z_name][0] - zg0
            dzg_ref[:, g_off:g_off + D_MODEL] = (dmg * y * sg * (1.0 - sg)).astype(BF16)
            du = _dot(dyb, wt_ref[...])
            datt = du * silu
            dzg_ref[:, z_off:z_off + 512] = (du * att * (sz * (1.0 + z * (1.0 - sz)))).astype(BF16)
            extra = _spread3(-_dot_split(datt * att, ind_ref[...]), e_ref)
            for p in range(N_HEADS // 2):
                c = datt[:, CHUNK * p:CHUNK * (p + 1)]
                for hh in range(2):
                    hd = 2 * p + hh
                    src = c if hh == 0 else pltpu.roll(c, HEAD_DIM, 1)
                    doaug_ref[:, CHUNK * hd:CHUNK * (hd + 1)] = jnp.where(
                        low, src, extra[:, CHUNK * hd:CHUNK * (hd + 1)]).astype(BF16)

        branch_bwd(sg_a, y_a, att_a, z_a, sz_a, silu_a, woat_ref, dya_ref, "ga", "za", doa_ref)
        branch_bwd(sg_b, y_b, att_b, z_b, sz_b, silu_b, wobt_ref, dyb_ref, "gb", "zb", dob_ref)

    row_blk = lambda w: pl.BlockSpec((tm, w), lambda i: (i, 0))
    full = lambda a: pl.BlockSpec(a.shape, lambda i: (0,) * a.ndim)
    sds = lambda w, dt: jax.ShapeDtypeStruct((s, w), dt)
    tds = lambda w: jax.ShapeDtypeStruct((w, s), BF16)
    col_blk = lambda w: pl.BlockSpec((w, tm), lambda i: (0, i))
    out_shape = [sds(D_MODEL, F32), tds(512), tds(512), tds(D_MODEL), sds(D_MODEL, BF16),
                 sds(D_MODEL, BF16), sds(D_MODEL, BF16), sds(AUG_W, BF16), sds(AUG_W, BF16), sds(3072, BF16),
                 jax.ShapeDtypeStruct((8, D_MODEL), F32)]
    ins = [x, tgt, att_a, att_b, za, zb, ga, gb, woa_t, wob_t, wout, vec, ind, emat]
    in_specs = [row_blk(a.shape[1]) for a in ins[:8]] + [full(a) for a in ins[8:]]
    out_specs = ([row_blk(D_MODEL), col_blk(512), col_blk(512), col_blk(D_MODEL)]
                 + [row_blk(o.shape[1]) for o in out_shape[4:-1]] + [pl.BlockSpec((8, D_MODEL), lambda i: (0, 0))])
    return pl.pallas_call(body, name="post", grid=(s // tm,), out_shape=out_shape, in_specs=in_specs,
                          out_specs=out_specs, compiler_params=_params(1))(*ins)


def _bwd_pre(dproj, w_all, x, dx1, mod, tm, chip_halves, modes):
    s = x.shape[0]
    ng, nh = len(DPROJ_GROUPS), len(chip_halves)
    last = s // tm - 1

    def body(*refs):
        dp_refs = refs[:ng]
        wt_ref, x_ref, dx1_ref, mod_ref = refs[ng:ng + 4]
        send_refs = refs[ng + 4:ng + 4 + nh]
        gx_ref, acc_ref = refs[ng + 4 + nh:ng + 6 + nh]
        recv_refs = refs[ng + 6 + nh:ng + 6 + 2 * nh]
        sems = refs[ng + 6 + 2 * nh:]

        @pl.when(pl.program_id(0) == 0)
        def _():
            acc_ref[...] = jnp.zeros_like(acc_ref)
            _exchange_start(_exchange_copies(send_refs, recv_refs, modes, *sems))

        dh = None
        for dp_ref, (_, off, w) in zip(dp_refs, DPROJ_GROUPS):
            part = _dot_nt(dp_ref[...], wt_ref[:, off:off + w])
            dh = part if dh is None else dh + part
        xv = x_ref[...]
        rstd = lax.rsqrt(jnp.mean(xv * xv, axis=-1, keepdims=True) + NORM_EPS)
        xh = xv * rstd
        gn = mod_ref[0:1, :]
        acc_ref[0:1, :] += jnp.sum(dh, axis=0, keepdims=True)
        acc_ref[1:2, :] += jnp.sum(dh * (xh * gn), axis=0, keepdims=True)
        dn = dh * (1.0 + mod_ref[1:2, :])
        acc_ref[2:3, :] += jnp.sum(dn * xh, axis=0, keepdims=True)
        dxh = dn * gn
        gx_ref[...] = dx1_ref[...] + rstd * (dxh - xh * jnp.mean(dxh * xh, axis=-1, keepdims=True))

        @pl.when(pl.program_id(0) == last)
        def _():
            _exchange_wait(_exchange_copies(send_refs, recv_refs, modes, *sems))

    row_blk = lambda w: pl.BlockSpec((tm, w), lambda i: (i, 0))
    full = lambda a: pl.BlockSpec(a.shape, lambda i: (0,) * a.ndim)
    outs = pl.pallas_call(
        body, name="bwd_pre", grid=(s // tm,),
        out_shape=[jax.ShapeDtypeStruct((s, D_MODEL), F32), jax.ShapeDtypeStruct((8, D_MODEL), F32)]
        + _exchange_out_shapes(chip_halves, modes),
        in_specs=[row_blk(w) for _, _, w in DPROJ_GROUPS] + [full(w_all), row_blk(D_MODEL), row_blk(D_MODEL), full(mod)]
        + [ANY_SPEC] * nh,
        out_specs=[row_blk(D_MODEL), pl.BlockSpec((8, D_MODEL), lambda i: (0, 0))] + [ANY_SPEC] * nh,
        scratch_shapes=_exchange_sems(nh),
        compiler_params=_params(1))(*dproj, w_all, x, dx1, mod, *chip_halves)
    return outs[0], outs[1], outs[2:]


def _token_matmul_multi(a_t, bs, tk, name):
    m, s = a_t.shape
    nb = len(bs)

    def body(*refs):
        a_ref, b_refs, o_refs = refs[0], refs[1:1 + nb], refs[1 + nb:]
        a = a_ref[...]
        for b_ref, o_ref in zip(b_refs, o_refs):
            @pl.when(pl.program_id(0) == 0)
            def _():
                o_ref[...] = jnp.zeros_like(o_ref)

            o_ref[...] += _dot(a, b_ref[...])

    return pl.pallas_call(
        body, name=name, grid=(s // tk,),
        out_shape=[jax.ShapeDtypeStruct((m, b.shape[1]), F32) for b in bs],
        in_specs=[pl.BlockSpec((m, tk), lambda k: (0, k))] + [pl.BlockSpec((tk, b.shape[1]), lambda k: (k, 0)) for b in bs],
        out_specs=[pl.BlockSpec((m, b.shape[1]), lambda k: (0, 0)) for b in bs],
        compiler_params=_params(1))(a_t, *bs)


def _adam_math(g, w, m, v):
    m2 = ADAM_B1 * m + (1.0 - ADAM_B1) * g
    v2 = ADAM_B2 * v + (1.0 - ADAM_B2) * (g * g)
    delta = -ADAM_LR * ((m2 / ADAM_C1) / (jnp.sqrt(v2 / ADAM_C2) + ADAM_EPS) + ADAM_WD * w)
    return delta, m2, v2


def _adamw_parts(full, parts, others, w, m, v, tr, name):
    _, rws, cols = w.shape

    def body(oth_ref, f_ref, p_ref, w_ref, m_ref, v_ref, g_ref, d_ref, m2_ref, v2_ref):
        g = None
        for chip in range(N_CHIPS):
            part = (1.0 - oth_ref[chip]) * f_ref[chip] + oth_ref[chip] * p_ref[chip].astype(F32)
            g = part if g is None else g + part
        g_ref[0] = g
        d_ref[0], m2_ref[0], v2_ref[0] = _adam_math(g, w_ref[0], m_ref[0], v_ref[0])

    blk = pl.BlockSpec((1, tr, cols), lambda i: (0, i, 0))
    slots = pl.BlockSpec((N_CHIPS, tr, cols), lambda i: (0, i, 0))
    o = jax.ShapeDtypeStruct((1, rws, cols), F32)
    return pl.pallas_call(
        body, name=name, grid=(rws // tr,), out_shape=[o, o, o, o],
        in_specs=[pl.BlockSpec(memory_space=pltpu.SMEM), slots, slots, blk, blk, blk],
        out_specs=[blk, blk, blk, blk], compiler_params=_params(1))(others, full, parts, w, m, v)


def _adamw_devices(parts, w, m, v, tr, name):
    _, rws, cols = w.shape

    def body(p_ref, w_ref, m_ref, v_ref, g_ref, d_ref, m2_ref, v2_ref):
        g = p_ref[0].astype(F32)
        for dev in range(1, N_DEV):
            g = g + p_ref[dev].astype(F32)
        g_ref[0] = g
        d_ref[0], m2_ref[0], v2_ref[0] = _adam_math(g, w_ref[0], m_ref[0], v_ref[0])

    blk = pl.BlockSpec((1, tr, cols), lambda i: (0, i, 0))
    o = jax.ShapeDtypeStruct((1, rws, cols), F32)
    return pl.pallas_call(
        body, name=name, grid=(rws // tr,), out_shape=[o, o, o, o],
        in_specs=[pl.BlockSpec((N_DEV, tr, cols), lambda i: (0, i, 0)), blk, blk, blk],
        out_specs=[blk, blk, blk, blk], compiler_params=_params(1))(parts, w, m, v)


def _adamw_ada(c_lanes, d_rows, w, m, v, tr):
    _, rws, cols = w.shape

    def body(c_ref, d_ref, w_ref, m_ref, v_ref, g_ref, dl_ref, m2_ref, v2_ref):
        for k in range(cols // CHUNK):
            cs = slice(CHUNK * k, CHUNK * (k + 1))
            g = c_ref[0] * d_ref[0:1, cs]
            for b in range(1, N_DEV):
                g = g + c_ref[b] * d_ref[b:b + 1, cs]
            g_ref[0, :, cs] = g
            dl_ref[0, :, cs], m2_ref[0, :, cs], v2_ref[0, :, cs] = _adam_math(
                g, w_ref[0, :, cs], m_ref[0, :, cs], v_ref[0, :, cs])

    blk = pl.BlockSpec((1, tr, cols), lambda i: (0, i, 0))
    o = jax.ShapeDtypeStruct((1, rws, cols), F32)
    return pl.pallas_call(
        body, name="adamw_ada", grid=(rws // tr,), out_shape=[o, o, o, o],
        in_specs=[pl.BlockSpec((N_DEV, tr, CHUNK), lambda i: (0, i, 0)), pl.BlockSpec((N_DEV, cols), lambda i: (0, 0)),
                  blk, blk, blk],
        out_specs=[blk, blk, blk, blk], compiler_params=_params(1))(c_lanes, d_rows, w, m, v)


def _adamw_small(packs, w, m, v):
    def body(p_ref, w_ref, m_ref, v_ref, g_ref, d_ref, m2_ref, v2_ref, loss_ref):
        g = p_ref[0]
        for dev in range(1, N_DEV):
            g = g + p_ref[dev]
        g_ref[...] = g
        d_ref[...], m2_ref[...], v2_ref[...] = _adam_math(g, w_ref[...], m_ref[...], v_ref[...])
        loss_ref[...] = jnp.broadcast_to(jnp.sum(g_ref[2:3, 0:D_MODEL], axis=1, keepdims=True), loss_ref.shape)

    o = jax.ShapeDtypeStruct(w.shape, F32)
    return pl.pallas_call(body, name="adamw_small", out_shape=[o, o, o, o, jax.ShapeDtypeStruct((8, CHUNK), F32)],
                          compiler_params=_params(0))(packs, w, m, v)


def _tile(s, want):
    return min(s, want)


def _layout_pieces():
    orig = {"qa": 0, "ka": 512, "va": 640, "za": 768, "qb": 1280, "kb": 1792, "vb": 2304, "f": 2816, "zb": 2824,
            "ga": 3336, "gb": 4360}
    pieces = []
    for name, (off, w) in SEG.items():
        if name in ("ka", "va"):
            pieces += [(orig[name] + HEAD_DIM * kv, orig[name] + HEAD_DIM * (kv + 1), off + CHUNK * kv) for kv in range(KV_GROUPS)]
        else:
            pieces.append((orig[name], orig[name] + (N_HEADS if name == "f" else w), off))
    return pieces


def _assemble_w_all(win_g):
    cols, pos = [], 0
    for lo, hi, new in sorted(_layout_pieces(), key=lambda t: t[2]):
        if new > pos:
            cols.append(jnp.zeros((D_MODEL, new - pos), win_g.dtype))
        col = lo
        while col < hi:
            dev = col // IN_SHARD
            end = min(hi, (dev + 1) * IN_SHARD)
            cols.append(win_g[dev, :, col - dev * IN_SHARD:end - dev * IN_SHARD])
            col = end
        pos = new + hi - lo
    cols.append(jnp.zeros((D_MODEL, PROJ_W - pos), win_g.dtype))
    return jnp.concatenate(cols, axis=1)


def _grad_slot(dw_groups, dev):
    lo_d, hi_d = dev * IN_SHARD, (dev + 1) * IN_SHARD
    cols = []
    for lo, hi, new in sorted(_layout_pieces()):
        a, b = max(lo, lo_d), min(hi, hi_d)
        if a < b:
            arr, off = next((g, o) for g, (_, o, w) in zip(dw_groups, DPROJ_GROUPS) if o <= new < o + w)
            cols.append(arr[:, new - off + a - lo:new - off + b - lo])
    return jnp.concatenate(cols, axis=1)


def kernel(x, c, positions, w_ada, b_ada, g_norm, w_in, b_f, sinks, w_o_swa, w_o_fox, w_out, g_final, loss_target, m_w_ada, m_b_ada, m_g_norm, m_w_in, m_b_f, m_sinks, m_w_o_swa, m_w_o_fox, m_w_out, m_g_final, v_w_ada, v_b_ada, v_g_norm, v_w_in, v_b_f, v_sinks, v_w_o_swa, v_w_o_fox, v_w_out, v_g_final):
    s = x.shape[1]
    tm = _tile(s, 256)
    ta = _tile(s, 512)
    me = 4 * lax.axis_index("x") + 2 * lax.axis_index("y") + lax.axis_index("c")
    x2, tgt = x[0], loss_target[0]

    inv_freq = np.power(np.float32(ROPE_THETA), -np.arange(0, HEAD_DIM, 2, dtype=np.float32) / HEAD_DIM)
    inv_freq = jnp.asarray(np.tile(inv_freq, CHUNK // (HEAD_DIM // 2))[None, :], F32)
    (cos, sa, sb), (win_g, woa_g, wob_g, wout_g), c_all = _rope_tables_and_gather(
        positions.reshape(s, 1).astype(F32), inv_freq, tm,
        [w_in[0].astype(BF16), w_o_swa[0].T.astype(BF16), w_o_fox[0].T.astype(BF16), w_out[0].astype(BF16)],
        jnp.broadcast_to(c, (8, D_MODEL)))
    c_all = c_all[:, 0, :]
    b_shard = lax.dynamic_slice(b_ada, (0, me * ADA_SHARD), (1, ADA_SHARD))
    ada_part = _ada_fwd(c_all, w_ada[0], b_shard)
    (ada_g,) = _exchange([ada_part], ["gather"], "gather_ada")
    ada = lax.dynamic_index_in_dim(ada_g, me, axis=1, keepdims=False).reshape(3 * D_MODEL)
    shift, scale, gate = ada[:D_MODEL], ada[D_MODEL:2 * D_MODEL], ada[2 * D_MODEL:]
    woa_t = woa_g.reshape(D_MODEL, 512)
    wob_t = wob_g.reshape(D_MODEL, 512)
    wout = wout_g.reshape(D_MODEL, D_MODEL)
    w_all = _assemble_w_all(win_g)

    zrow = jnp.zeros((1, D_MODEL), F32)
    mod = jnp.concatenate([g_norm, scale[None], shift[None], zrow, zrow, zrow, zrow, zrow], axis=0)
    bf_row = jnp.pad(b_f, ((0, 0), (0, CHUNK - N_HEADS)))
    emat_np = np.zeros((3 * CHUNK, AUG_W), np.float32)
    ind_np = np.zeros((512, CHUNK), np.float32)
    for hd in range(N_HEADS):
        for part in range(3):
            emat_np[CHUNK * part + hd, CHUNK * hd + 64 + part] = 1.0
        ind_np[HEAD_DIM * hd:HEAD_DIM * (hd + 1), hd] = 1.0
    emat, ind = jnp.asarray(emat_np, BF16), jnp.asarray(ind_np, BF16)
    sink_rows = jnp.broadcast_to(jnp.pad(sinks.reshape(KV_GROUPS, GROUP), ((0, 0), (0, 8 - GROUP)))[:, :, None],
                                 (KV_GROUPS, 8, CHUNK))

    (h_t, qa, ka, va, za, qb, kb, vb, zb, ga, gb, xf, bounds, norms) = _fwd_proj(
        x2, mod, w_all, cos, sa, sb, bf_row, emat, ind, tm)
    first_key, last_query = _fox_block_ranges(bounds, norms, ta // tm)
    att_a, qa_bwd = _swa_fwd(qa, ka, va, sink_rows, ta)
    att_b, qb_bwd = _fox_fwd(first_key, qb, kb, vb, ta)
    vec = jnp.concatenate([gate[None], g_final[None], zrow, zrow, zrow, zrow, zrow, zrow], axis=0)
    (dx1, ua_t, ub_t, merged_t, d_o, dya, dyb, doa, dob, dzg, post_acc) = _post(
        x2, tgt, att_a, att_b, za, zb, ga, gb, woa_t, wob_t, wout, vec, ind, emat, tm)

    da, dsink = _swa_bwd(ka, va, qa_bwd, doa, cos, sa, sb, sink_rows, ta)
    dkb, dvb, dck, dqb, dcq = _fox_bwd(last_query, kb, vb, qb_bwd, dob, ta)
    dfb, dbf_acc = _fgate_bwd(dcq, dck, xf, ta)
    dproj = [da, dqb, dkb, dvb, dzg, dfb]
    dw_zg, = _token_matmul_multi(h_t, [dzg], ta, "dw_in_zg")
    dw_a, dw_qb, dw_kb, dw_vb, dw_f = _token_matmul_multi(h_t, [da, dqb, dkb, dvb, dfb], ta, "dw_in_rest")
    dw_all = [dw_a, dw_qb, dw_kb, dw_vb, dw_zg, dw_f]

    core, chip = lax.axis_index("c"), 2 * lax.axis_index("x") + lax.axis_index("y")
    win_slots = jnp.stack([jnp.stack([_grad_slot(dw_all, 2 * ch + co) for ch in range(N_CHIPS)]) for co in range(2)])
    (dwoa, dwob, dwout), (win_theirs,) = _token_matmuls_and_swap(
        [(ua_t, dya), (ub_t, dyb), (merged_t, d_o)], ta, [win_slots])
    win_full, win_half = _chip_partial(jnp.reshape(core, (1,)).astype(jnp.int32), win_slots, win_theirs, 256,
                                       "chip_partial_w_in")
    col_slots = lambda g: g.reshape(512, N_DEV, 128).transpose(1, 0, 2).astype(BF16)
    grad_x, pre_acc, (p_win, p_woa, p_wob, p_wout) = _bwd_pre(
        dproj, w_all, x2, dx1, mod, tm,
        [win_half, col_slots(dwoa), col_slots(dwob), dwout.reshape(N_DEV, 128, D_MODEL).astype(BF16)],
        ["chips", "devices", "devices", "devices"])
    misc = jnp.concatenate([dbf_acc[0, :N_HEADS], dsink[:, :GROUP, 0].reshape(N_HEADS), jnp.zeros((D_MODEL - 16,), F32)])
    zpad = jnp.zeros((2 * D_MODEL,), F32)
    pack = jnp.stack([jnp.concatenate([pre_acc[0], pre_acc[1], post_acc[0]]),
                      jnp.concatenate([pre_acc[2], post_acc[1], misc]),
                      jnp.concatenate([post_acc[2], zpad])] + [jnp.zeros((3 * D_MODEL,), F32)] * 5)
    (packs,) = _exchange([pack], ["gather"], "gather_small_grads")
    others = jnp.where(jnp.arange(N_CHIPS) == chip, 0.0, 1.0).astype(F32)

    g_win, d_win, m_win, v_win = _adamw_parts(win_full, p_win, others, w_in, m_w_in, v_w_in, 128, "adamw_w_in")
    g_woa, d_woa, m_woa, v_woa = _adamw_devices(p_woa, w_o_swa, m_w_o_swa, v_w_o_swa, 128, "adamw_w_o_swa")
    g_wob, d_wob, m_wob, v_wob = _adamw_devices(p_wob, w_o_fox, m_w_o_fox, v_w_o_fox, 128, "adamw_w_o_fox")
    g_wout, d_wout, m_wout, v_wout = _adamw_devices(p_wout, w_out, m_w_out, v_w_out, 128, "adamw_w_out")
    d_ada_rows = lax.dynamic_slice(packs[:, 0, :], (0, me * ADA_SHARD), (N_DEV, ADA_SHARD))
    c_lanes = jnp.broadcast_to(c_all[:, :, None], (N_DEV, D_MODEL, CHUNK))
    g_wada, d_wada, m_wada, v_wada = _adamw_ada(c_lanes, d_ada_rows, w_ada, m_w_ada, v_w_ada, 256)

    def small_pack(bada, gn, gf, bfv, sk):
        misc_w = jnp.concatenate([bfv[0], sk[0], jnp.zeros((D_MODEL - 16,), F32)])
        return jnp.stack([bada[0], jnp.concatenate([gn[0], gf, misc_w])] + [jnp.zeros((3 * D_MODEL,), F32)] * 6)

    sm = _adamw_small(packs, small_pack(b_ada, g_norm, g_final, b_f, sinks),
                      small_pack(m_b_ada, m_g_norm, m_g_final, m_b_f, m_sinks),
                      small_pack(v_b_ada, v_g_norm, v_g_final, v_b_f, v_sinks))
    loss = sm[4][0, 0]

    def unpack(p):
        return (p[0][None], p[1, :D_MODEL][None], p[1, 2 * D_MODEL:2 * D_MODEL + 8][None],
                p[1, 2 * D_MODEL + 8:2 * D_MODEL + 16][None], p[1, D_MODEL:2 * D_MODEL])

    outs = []
    for big, small in (((g_wada, g_win, g_woa, g_wob, g_wout), sm[0]), ((d_wada, d_win, d_woa, d_wob, d_wout), sm[1]),
                       ((m_wada, m_win, m_woa, m_wob, m_wout), sm[2]), ((v_wada, v_win, v_woa, v_wob, v_wout), sm[3])):
        wada_o, win_o, woa_o, wob_o, wout_o = big
        bada_o, gn_o, bf_o, sk_o, gf_o = unpack(small)
        outs += [wada_o, bada_o, gn_o, win_o, bf_o, sk_o, woa_o, wob_o, wout_o, gf_o]
    return (loss, grad_x[None], *outs)
```

```python
import numpy as np
import jax
import jax.numpy as jnp
from jax import lax
from jax.experimental import pallas as pl
from jax.experimental.pallas import tpu as pltpu

F32 = jnp.float32
BF16 = jnp.bfloat16

D_MODEL = 1024
HEAD_DIM = 64
N_HEADS = 8
KV_GROUPS = 2
GROUP = N_HEADS // KV_GROUPS
WINDOW = 128
CHUNK = 128
AUG_W = N_HEADS * CHUNK
N_DEV = 8
IN_SHARD = 673
ADA_SHARD = 384
NORM_EPS = 1e-6
ROPE_THETA = 10000.0
Q_SCALE = HEAD_DIM ** -0.5
NEG = -1e30

ADAM_LR = 0.001
ADAM_B1 = 0.9
ADAM_B2 = 0.999
ADAM_EPS = 1e-08
ADAM_WD = 0.01
ADAM_STEP = 10
ADAM_C1 = 1.0 - ADAM_B1 ** ADAM_STEP
ADAM_C2 = 1.0 - ADAM_B2 ** ADAM_STEP

SEG = {}
_off = 0
for _name, _w in (("qa", 512), ("ka", 256), ("va", 256), ("qb", 512), ("kb", 512), ("vb", 512),
                  ("za", 512), ("zb", 512), ("ga", 1024), ("gb", 1024), ("f", 128)):
    SEG[_name] = (_off, _w)
    _off += _w
PROJ_W = _off
DPROJ_GROUPS = (("a", SEG["qa"][0], 1024), ("qb", SEG["qb"][0], 512), ("kb", SEG["kb"][0], 512),
                ("vb", SEG["vb"][0], 512), ("zg", SEG["za"][0], 3072), ("f", SEG["f"][0], 128))

VMEM_LIMIT = 56 * 1024 * 1024


def _params(n_axes):
    return pltpu.CompilerParams(dimension_semantics=("arbitrary",) * n_axes, vmem_limit_bytes=VMEM_LIMIT)


def _dot(a, b):
    return jnp.dot(a, b, preferred_element_type=F32)


def _dot_nt(a, b):
    return lax.dot_general(a, b, (((1,), (1,)), ((), ())), preferred_element_type=F32)


def _dot_tn(a, b):
    return lax.dot_general(a, b, (((0,), (0,)), ((), ())), preferred_element_type=F32)


def _lane(n):
    return lax.broadcasted_iota(jnp.int32, (1, n), 1)


def _split3(x):
    hi = x.astype(BF16).astype(F32)
    r = x - hi
    mid = r.astype(BF16).astype(F32)
    lo = (r - mid).astype(BF16).astype(F32)
    return hi, mid, lo


def _dot_split(x, b):
    hi, mid, lo = _split3(x)
    return _dot(hi.astype(BF16), b) + _dot(mid.astype(BF16), b) + _dot(lo.astype(BF16), b)


def _spread3(x, e3_ref):
    return _dot(jnp.concatenate(_split3(x), axis=1).astype(BF16), e3_ref[...])


def _place3(lane, base, x, other):
    hi, mid, lo = _split3(x)
    return jnp.where(lane == base, hi, jnp.where(lane == base + 1, mid, jnp.where(lane == base + 2, lo, other)))


def _lane_sum(x, lo, hi):
    lane = _lane(x.shape[1])
    return jnp.sum(jnp.where((lane >= lo) & (lane < hi), x, 0.0), axis=1, keepdims=True)


def _ones_lanes(lo, hi):
    lane = _lane(CHUNK)
    return jnp.where((lane >= lo) & (lane < hi), 1.0, 0.0).astype(F32)


def _rope(c, cos, sa, sb):
    return c * cos + pltpu.roll(c, CHUNK - 32, 1) * sa + pltpu.roll(c, 32, 1) * sb


def _rope_inv(d, cos, sa, sb):
    return d * cos - (pltpu.roll(d, CHUNK - 32, 1) * sa + pltpu.roll(d, 32, 1) * sb)


def _pair(c0, c1):
    return jnp.where(_lane(CHUNK) < HEAD_DIM, c0, pltpu.roll(c1, HEAD_DIM, 1))


def _sigmoid(x):
    return 0.5 * jnp.tanh(0.5 * x) + 0.5


N_CHIPS = 4
ANY_SPEC = pl.BlockSpec(memory_space=pl.ANY)


def _exchange(arrays, modes, name):
    n = len(arrays)

    def body(*refs):
        copies = _exchange_copies(refs[:n], refs[n:2 * n], modes, *refs[2 * n:])
        _exchange_start(copies)
        _exchange_wait(copies)

    return pl.pallas_call(
        body, name=name, out_shape=_exchange_out_shapes(arrays, modes),
        in_specs=[ANY_SPEC] * n, out_specs=[ANY_SPEC] * n, scratch_shapes=_exchange_sems(n),
    )(*arrays)


def _exchange_out_shapes(arrays, modes):
    return [jax.ShapeDtypeStruct((N_DEV,) + a.shape if md == "gather" else a.shape, a.dtype)
            for a, md in zip(arrays, modes)]


def _exchange_sems(n):
    return [pltpu.SemaphoreType.DMA((n, N_DEV - 1)), pltpu.SemaphoreType.DMA((n, N_DEV - 1)),
            pltpu.SemaphoreType.DMA((n,))]


def _exchange_copies(srcs, dsts, modes, send_sems, recv_sems, loc_sems):
    n = len(srcs)
    x, y, c = lax.axis_index("x"), lax.axis_index("y"), lax.axis_index("c")

    def slot(i, px, py, pc):
        return 2 * px + py if modes[i] == "chips" else 4 * px + 2 * py + pc

    def src_of(i, px, py, pc):
        return srcs[i] if modes[i] == "gather" else srcs[i].at[slot(i, px, py, pc)]

    local = [pltpu.make_async_copy(src_of(i, x, y, c), dsts[i].at[slot(i, x, y, c)], loc_sems.at[i])
             for i in range(n)]
    sends, recvs = [], []
    for r in (1, 2, 4, 3, 5, 6, 7):
        px = 1 - x if r & 4 else x
        py = 1 - y if r & 2 else y
        pc = 1 - c if r & 1 else c
        for i in range(n):
            if modes[i] == "chips" and r & 1:
                continue
            sems = dict(send_sem=send_sems.at[i, r - 1], recv_sem=recv_sems.at[i, r - 1],
                        device_id=(px, py, pc), device_id_type=pl.DeviceIdType.MESH)
            sends.append(pltpu.make_async_remote_copy(
                src_ref=src_of(i, px, py, pc), dst_ref=dsts[i].at[slot(i, x, y, c)], **sems))
            recvs.append(pltpu.make_async_remote_copy(
                src_ref=src_of(i, px, py, pc), dst_ref=dsts[i].at[slot(i, px, py, pc)], **sems))
    return local, sends, recvs


def _exchange_start(copies):
    local, sends, _ = copies
    for cp in local + sends:
        cp.start()


def _exchange_wait(copies):
    local, sends, recvs = copies
    for cp in recvs:
        cp.wait_recv()
    for cp in sends:
        cp.wait_send()
    for cp in local:
        cp.wait()


def _gather2_parts(srcs, dsts, send_sems, recv_sems, loc_sems):
    n = len(srcs)
    x, y, c = lax.axis_index("x"), lax.axis_index("y"), lax.axis_index("c")
    me, sibling = (x, y, c), (x, y, 1 - c)
    chips = [(1 - x, y), (x, 1 - y), (1 - x, 1 - y)]

    def rows(i, dev):
        return dsts[i].at[4 * dev[0] + 2 * dev[1] + dev[2]]

    def copy(i, k, block, to, src=None):
        return pltpu.make_async_remote_copy(
            src_ref=rows(i, block) if src is None else src, dst_ref=rows(i, block),
            send_sem=send_sems.at[i, k], recv_sem=recv_sems.at[i, k],
            device_id=to, device_id_type=pl.DeviceIdType.MESH)

    local = [pltpu.make_async_copy(srcs[i], rows(i, me), loc_sems.at[i]) for i in range(n)]
    first = []
    for i in range(n):
        first.append(copy(i, 0, me, sibling, src=srcs[i]))
        first += [copy(i, 1 + j, me, (*chip, c), src=srcs[i]) for j, chip in enumerate(chips)]
    return n, c, me, sibling, chips, copy, local, first


def _gather2_start(srcs, dsts, send_sems, recv_sems, loc_sems):
    *_, local, first = _gather2_parts(srcs, dsts, send_sems, recv_sems, loc_sems)
    for cp in local + first:
        cp.start()


def _gather2_finish(srcs, dsts, send_sems, recv_sems, loc_sems):
    n, c, me, sibling, chips, copy, local, first = _gather2_parts(srcs, dsts, send_sems, recv_sems, loc_sems)
    passed = []
    for j, chip in enumerate(chips):
        for i in range(n):
            copy(i, 1 + j, (*chip, c), me).wait_recv()
            cp = copy(i, 4 + j, (*chip, c), sibling)
            cp.start()
            passed.append(cp)
    for i in range(n):
        copy(i, 0, sibling, me).wait_recv()
        for j, chip in enumerate(chips):
            copy(i, 4 + j, (*chip, 1 - c), me).wait_recv()
    for cp in first + passed:
        cp.wait_send()
    for cp in local:
        cp.wait()


def _sibling_copies(srcs, dsts, send_sems, recv_sems):
    x, y, c = lax.axis_index("x"), lax.axis_index("y"), lax.axis_index("c")
    return [pltpu.make_async_remote_copy(
        src_ref=srcs[i].at[k], dst_ref=dsts[i].at[k], send_sem=send_sems.at[i, k], recv_sem=recv_sems.at[i, k],
        device_id=(x, y, 1 - c), device_id_type=pl.DeviceIdType.MESH)
        for i in range(len(srcs)) for k in range(N_CHIPS)]


def _token_matmuls_and_swap(pairs, tk, swap):
    npair, nsw = len(pairs), len(swap)
    s = pairs[0][0].shape[1]
    last = s // tk - 1

    def body(*refs):
        a_refs, b_refs = refs[0:2 * npair:2], refs[1:2 * npair:2]
        src = refs[2 * npair:2 * npair + nsw]
        o_refs = refs[2 * npair + nsw:3 * npair + nsw]
        dst = refs[3 * npair + nsw:3 * npair + 2 * nsw]
        sems = refs[3 * npair + 2 * nsw:]

        @pl.when(pl.program_id(0) == 0)
        def _():
            for cp in _sibling_copies(src, dst, *sems):
                cp.start()

        for a_ref, b_ref, o_ref in zip(a_refs, b_refs, o_refs):
            @pl.when(pl.program_id(0) == 0)
            def _():
                o_ref[...] = jnp.zeros_like(o_ref)

            o_ref[...] += _dot(a_ref[...], b_ref[...])

        @pl.when(pl.program_id(0) == last)
        def _():
            for cp in _sibling_copies(src, dst, *sems):
                cp.wait()

    in_specs, args = [], []
    for a_t, b in pairs:
        in_specs += [pl.BlockSpec((a_t.shape[0], tk), lambda k: (0, k)), pl.BlockSpec((tk, b.shape[1]), lambda k: (k, 0))]
        args += [a_t, b]
    outs = pl.pallas_call(
        body, name="dw_out_projs_swap", grid=(s // tk,),
        out_shape=[jax.ShapeDtypeStruct((a_t.shape[0], b.shape[1]), F32) for a_t, b in pairs]
        + [jax.ShapeDtypeStruct(a.shape, a.dtype) for a in swap],
        in_specs=in_specs + [ANY_SPEC] * nsw,
        out_specs=[pl.BlockSpec((a_t.shape[0], b.shape[1]), lambda k: (0, 0)) for a_t, b in pairs] + [ANY_SPEC] * nsw,
        scratch_shapes=[pltpu.SemaphoreType.DMA((nsw, N_CHIPS)), pltpu.SemaphoreType.DMA((nsw, N_CHIPS))],
        compiler_params=_params(1))(*args, *swap)
    return outs[:npair], outs[npair:]


def _chip_partial(core, slots, theirs, tr, name):
    _, k, rws, cols = slots.shape

    def body(core_ref, a_ref, b_ref, f_ref, h_ref):
        sm = a_ref[0] + b_ref[...].astype(F32)
        f_ref[...] = sm
        h_ref[...] = sm.astype(BF16)

    blk = pl.BlockSpec((1, tr, cols), lambda j, i, core_ref: (j, i, 0))
    mine = pl.BlockSpec((1, 1, tr, cols), lambda j, i, core_ref: (core_ref[0], j, i, 0))
    return pl.pallas_call(
        body, name=name,
        grid_spec=pltpu.PrefetchScalarGridSpec(num_scalar_prefetch=1, grid=(k, rws // tr),
                                               in_specs=[mine, blk], out_specs=[blk, blk]),
        out_shape=[jax.ShapeDtypeStruct(theirs.shape, F32), jax.ShapeDtypeStruct(theirs.shape, BF16)],
        compiler_params=_params(2))(core, slots, theirs)


def _ada_fwd(c_all, w_shard, b_shard):
    def body(c_ref, w_ref, b_ref, o_ref):
        ch, cm, cl = [t.astype(BF16) for t in _split3(c_ref[...])]
        wh, wm, wl = [t.astype(BF16) for t in _split3(w_ref[...])]
        acc = _dot(ch, wh) + _dot(ch, wm) + _dot(cm, wh) + _dot(ch, wl) + _dot(cl, wh) + _dot(cm, wm)
        o_ref[...] = acc + b_ref[...]

    return pl.pallas_call(body, name="ada_fwd", out_shape=jax.ShapeDtypeStruct((N_DEV, ADA_SHARD), F32),
                          compiler_params=_params(0))(c_all, w_shard, b_shard)


def _rope_tables_and_gather(pos_col, inv_freq, tm, weights, c_rows):
    s = pos_col.shape[0]
    nw = len(weights)
    last = s // tm - 1

    def body(*refs):
        p_ref, f_ref = refs[:2]
        w_src, c_src = refs[2:2 + nw], refs[2 + nw:3 + nw]
        cos_ref, sa_ref, sb_ref = refs[3 + nw:6 + nw]
        w_dst, c_dst = refs[6 + nw:6 + 2 * nw], refs[6 + 2 * nw:7 + 2 * nw]
        w_sems, c_sems = refs[7 + 2 * nw:10 + 2 * nw], refs[10 + 2 * nw:]

        @pl.when(pl.program_id(0) == 0)
        def _():
            _gather2_start(w_src, w_dst, *w_sems)
            _exchange_start(_exchange_copies(c_src, c_dst, ["gather"], *c_sems))

        ang = p_ref[...] * f_ref[...]
        sin = jnp.sin(ang)
        first_half = (_lane(CHUNK) & (HEAD_DIM - 1)) < HEAD_DIM // 2
        cos_ref[...] = jnp.cos(ang)
        sa_ref[...] = jnp.where(first_half, -sin, 0.0)
        sb_ref[...] = jnp.where(first_half, 0.0, sin)

        @pl.when(pl.program_id(0) == last)
        def _():
            _exchange_wait(_exchange_copies(c_src, c_dst, ["gather"], *c_sems))
            _gather2_finish(w_src, w_dst, *w_sems)

    tab = jax.ShapeDtypeStruct((s, CHUNK), F32)
    blk = pl.BlockSpec((tm, CHUNK), lambda i: (i, 0))
    gathered = [jax.ShapeDtypeStruct((N_DEV,) + a.shape, a.dtype) for a in list(weights) + [c_rows]]
    outs = pl.pallas_call(
        body, name="rope_tables_gather", grid=(s // tm,), out_shape=[tab, tab, tab] + gathered,
        in_specs=[pl.BlockSpec((tm, 1), lambda i: (i, 0)), pl.BlockSpec((1, CHUNK), lambda i: (0, 0))]
        + [ANY_SPEC] * (nw + 1),
        out_specs=[blk, blk, blk] + [ANY_SPEC] * (nw + 1),
        scratch_shapes=_exchange_sems(nw) + _exchange_sems(1),
        compiler_params=_params(1))(pos_col, inv_freq, *weights, c_rows)
    return outs[:3], outs[3:3 + nw], outs[3 + nw]


def _fwd_proj(x, mod, w_all, cos, sa, sb, bf_row, emat, ind, tm):
    s = x.shape[0]

    def body(x_ref, mod_ref, w_ref, cos_ref, sa_ref, sb_ref, bf_ref, e_ref, ind_ref,
             h_ref, qa_ref, ka_ref, va_ref, za_ref, qb_ref, kb_ref, vb_ref, zb_ref, ga_ref, gb_ref, f_ref,
             bnd_ref, nrm_ref, carry_ref):
        @pl.when(pl.program_id(0) == 0)
        def _():
            carry_ref[...] = jnp.zeros_like(carry_ref)

        xv = x_ref[...]
        rstd = lax.rsqrt(jnp.mean(xv * xv, axis=-1, keepdims=True) + NORM_EPS)
        h = (xv * rstd * mod_ref[0:1, :]) * (1.0 + mod_ref[1:2, :]) + mod_ref[2:3, :]
        hb = h.astype(BF16)
        h_ref[...] = hb.T

        def seg(name):
            off, w = SEG[name]
            return _dot(hb, w_ref[:, off:off + w])

        lane = _lane(CHUNK)
        low = lane < HEAD_DIM
        q_ones = _ones_lanes(64, 67)
        k_ones = _ones_lanes(67, 70)
        cos_t, sa_t, sb_t = cos_ref[...], sa_ref[...], sb_ref[...]

        def write_heads(ref, nat, extra_of, rope, scale):
            for p in range(N_HEADS // 2):
                c = nat[:, CHUNK * p:CHUNK * (p + 1)]
                if rope:
                    c = _rope(c, cos_t, sa_t, sb_t)
                if scale != 1.0:
                    c = c * scale
                for hh in range(2):
                    hd = 2 * p + hh
                    src = c if hh == 0 else pltpu.roll(c, HEAD_DIM, 1)
                    ref[:, CHUNK * hd:CHUNK * (hd + 1)] = jnp.where(low, src, extra_of(hd)).astype(BF16)

        write_heads(qa_ref, seg("qa"), lambda hd: q_ones, True, Q_SCALE)
        ka = seg("ka")
        va = seg("va")
        for kv in range(KV_GROUPS):
            sl = slice(CHUNK * kv, CHUNK * (kv + 1))
            ka_ref[:, sl] = jnp.where(low, _rope(ka[:, sl], cos_t, sa_t, sb_t), k_ones).astype(BF16)
            va_ref[:, sl] = jnp.where(low, va[:, sl], q_ones).astype(BF16)
        za_ref[...] = seg("za")
        zb_ref[...] = seg("zb")
        ga_ref[...] = seg("ga")
        gb_ref[...] = seg("gb")

        xf = seg("f") + bf_ref[...]
        f_ref[...] = xf
        logf = jnp.minimum(xf, 0.0) - jnp.log1p(jnp.exp(-jnp.abs(xf)))
        row = lax.broadcasted_iota(jnp.int32, (tm, tm), 0)
        col = lax.broadcasted_iota(jnp.int32, (tm, tm), 1)
        tri = jnp.where(col <= row, 1.0, 0.0).astype(BF16)
        hi, mid, lo = _split3(logf)
        cum = _dot(tri, hi.astype(BF16)) + _dot(tri, mid.astype(BF16)) + _dot(tri, lo.astype(BF16))
        cum = cum + carry_ref[0:1, :]
        carry_ref[0:1, :] += jnp.sum(logf, axis=0, keepdims=True)
        lane_all = _lane(AUG_W) & (CHUNK - 1)
        bias = _spread3(-cum, e_ref) + jnp.where((lane_all >= 67) & (lane_all < 70), 1.0, 0.0)

        qb, kb = seg("qb"), seg("kb")
        write_heads(qb_ref, qb, lambda hd: q_ones, False, Q_SCALE)
        write_heads(kb_ref, kb, lambda hd: bias[:, CHUNK * hd:CHUNK * (hd + 1)], False, 1.0)
        write_heads(vb_ref, seg("vb"), lambda hd: q_ones, False, 1.0)

        rid = lax.broadcasted_iota(jnp.int32, (tm, CHUNK), 0)
        bnd_ref[...] = jnp.zeros_like(bnd_ref)
        bnd_ref[0, 0:1, :] = jnp.sum(jnp.where(rid == 0, cum, 0.0), axis=0, keepdims=True)
        bnd_ref[0, 1:2, :] = jnp.sum(jnp.where(rid == tm - 1, cum, 0.0), axis=0, keepdims=True)

        @pl.when(pl.program_id(0) == 0)
        def _():
            nrm_ref[...] = jnp.zeros_like(nrm_ref)

        for r_, nat in ((0, qb * Q_SCALE), (1, kb)):
            sq = _dot((nat * nat).astype(BF16), ind_ref[...])
            nrm_ref[r_:r_ + 1, :] = jnp.maximum(nrm_ref[r_:r_ + 1, :], jnp.max(sq, axis=0, keepdims=True))

    row_blk = lambda w: pl.BlockSpec((tm, w), lambda i: (i, 0))
    full = lambda a: pl.BlockSpec(a.shape, lambda i: (0,) * a.ndim)
    sds = lambda w, dt: jax.ShapeDtypeStruct((s, w), dt)
    out_shape = [sds(AUG_W, BF16), sds(KV_GROUPS * CHUNK, BF16), sds(KV_GROUPS * CHUNK, BF16),
                 sds(512, F32), sds(AUG_W, BF16), sds(AUG_W, BF16), sds(AUG_W, BF16), sds(512, F32),
                 sds(D_MODEL, F32), sds(D_MODEL, F32), sds(CHUNK, F32)]
    small = [jax.ShapeDtypeStruct((s // tm, 8, CHUNK), F32), jax.ShapeDtypeStruct((8, CHUNK), F32)]
    return pl.pallas_call(
        body, name="fwd_proj", grid=(s // tm,),
        out_shape=[jax.ShapeDtypeStruct((D_MODEL, s), BF16)] + out_shape + small,
        in_specs=[row_blk(D_MODEL), full(mod), full(w_all), row_blk(CHUNK), row_blk(CHUNK), row_blk(CHUNK),
                  full(bf_row), full(emat), full(ind)],
        out_specs=[pl.BlockSpec((D_MODEL, tm), lambda i: (0, i))] + [row_blk(o.shape[1]) for o in out_shape] + [
            pl.BlockSpec((1, 8, CHUNK), lambda i: (i, 0, 0)), pl.BlockSpec((8, CHUNK), lambda i: (0, 0))],
        scratch_shapes=[pltpu.VMEM((8, CHUNK), F32)],
        compiler_params=_params(1))(x, mod, w_all, cos, sa, sb, bf_row, emat, ind)


def _swa_fwd(q_aug, k_aug, v_aug, sink_rows, tq):
    s = q_aug.shape[0]
    r = tq // WINDOW
    gw = GROUP * CHUNK

    def body(q_ref, kc_ref, kp_ref, vc_ref, vp_ref, sink_ref, o_ref, qb_ref):
        i = pl.program_id(1)
        st = GROUP * WINDOW
        qloc = lax.broadcasted_iota(jnp.int32, (st, 2 * WINDOW), 0) & (WINDOW - 1)
        col = lax.broadcasted_iota(jnp.int32, (st, 2 * WINDOW), 1)
        band = (col > qloc) & (col <= qloc + WINDOW)
        head = jnp.right_shift(lax.broadcasted_iota(jnp.int32, (st, 1), 0), 7)
        sink = jnp.zeros((st, 1), F32)
        for g in range(GROUP):
            sink = jnp.where(head == g, jnp.max(sink_ref[0, g:g + 1, :], axis=1, keepdims=True), sink)
        lane = _lane(CHUNK)
        for sub in range(r):
            rows = slice(WINDOW * sub, WINDOW * (sub + 1))
            q = jnp.concatenate([q_ref[rows, CHUNK * g:CHUNK * (g + 1)] for g in range(GROUP)], axis=0)
            if sub == 0:
                k = jnp.concatenate([kp_ref[...], kc_ref[rows, :]], axis=0)
                v = jnp.concatenate([vp_ref[...], vc_ref[rows, :]], axis=0)
                valid = band & ((col >= WINDOW) | (i > 0))
            else:
                both = slice(WINDOW * (sub - 1), WINDOW * (sub + 1))
                k, v, valid = kc_ref[both, :], vc_ref[both, :], band
            sc = jnp.where(valid, _dot_nt(q, k), NEG)
            m = jnp.maximum(jnp.max(sc, axis=1, keepdims=True), sink)
            acc = _dot(jnp.exp(sc - m).astype(BF16), v)
            denom = _lane_sum(acc, 64, 65) + jnp.exp(sink - m)
            out = acc / denom
            aug = _place3(lane, 67, -(m + jnp.log(denom)), q.astype(F32)).astype(BF16)
            hrows = lambda a, g: a[WINDOW * g:WINDOW * (g + 1), :]
            for g in range(GROUP):
                qb_ref[rows, CHUNK * g:CHUNK * (g + 1)] = hrows(aug, g)
            for pp in range(GROUP // 2):
                o_ref[rows, CHUNK * pp:CHUNK * (pp + 1)] = _pair(hrows(out, 2 * pp), hrows(out, 2 * pp + 1))

    return pl.pallas_call(
        body, name="swa_fwd", grid=(KV_GROUPS, s // tq),
        out_shape=[jax.ShapeDtypeStruct((s, 512), F32), jax.ShapeDtypeStruct((s, AUG_W), BF16)],
        in_specs=[pl.BlockSpec((tq, gw), lambda kv, i: (i, kv)),
                  pl.BlockSpec((tq, CHUNK), lambda kv, i: (i, kv)),
                  pl.BlockSpec((WINDOW, CHUNK), lambda kv, i: (jnp.maximum(i * r - 1, 0), kv)),
                  pl.BlockSpec((tq, CHUNK), lambda kv, i: (i, kv)),
                  pl.BlockSpec((WINDOW, CHUNK), lambda kv, i: (jnp.maximum(i * r - 1, 0), kv)),
                  pl.BlockSpec((1, 8, CHUNK), lambda kv, i: (kv, 0, 0))],
        out_specs=[pl.BlockSpec((tq, GROUP * HEAD_DIM), lambda kv, i: (i, kv)),
                   pl.BlockSpec((tq, gw), lambda kv, i: (i, kv))],
        compiler_params=_params(2))(q_aug, k_aug, k_aug, v_aug, v_aug, sink_rows)


def _swa_bwd(k_aug, v_aug, q_bwd, do_aug, cos, sa, sb, sink_rows, tk):
    s = k_aug.shape[0]
    r = tk // WINDOW
    nt = s // tk
    nb = s // WINDOW
    gw = GROUP * CHUNK

    def body(k_ref, v_ref, q_ref, qn_ref, do_ref, don_ref, cos_ref, sa_ref, sb_ref, sink_ref,
             da_ref, dsink_ref, carry_ref, acc_ref):
        j = pl.program_id(0)

        @pl.when(j == 0)
        def _():
            carry_ref[...] = jnp.zeros_like(carry_ref)
            dsink_ref[...] = jnp.zeros_like(dsink_ref)

        acc_ref[...] = jnp.zeros_like(acc_ref)
        acc_ref[0:WINDOW, :] = carry_ref[...]
        dk_col, dv_col = SEG["ka"][0] - SEG["qa"][0], SEG["va"][0] - SEG["qa"][0]
        st = GROUP * WINDOW
        key = lax.broadcasted_iota(jnp.int32, (WINDOW, 2 * st), 0)
        col = lax.broadcasted_iota(jnp.int32, (WINDOW, 2 * st), 1)
        qloc = col & (WINDOW - 1)
        band = ((col < st) & (key <= qloc)) | ((col >= st) & (key > qloc))
        lane = _lane(CHUNK)
        low = lane < HEAD_DIM
        cos_t, sa_t, sb_t = cos_ref[...], sa_ref[...], sb_ref[...]
        for kv, sub in [(kv, sub) for kv in range(KV_GROUPS) for sub in range(r)]:
            heads = [slice(CHUNK * (GROUP * kv + g), CHUNK * (GROUP * kv + g + 1)) for g in range(GROUP)]
            kvs = slice(CHUNK * kv, CHUNK * (kv + 1))
            rows = slice(WINDOW * sub, WINDOW * (sub + 1))
            nxt = slice(WINDOW * (sub + 1), WINDOW * (sub + 2))
            k, v = k_ref[rows, kvs], v_ref[rows, kvs]
            q_cur, do_cur = [q_ref[rows, cs] for cs in heads], [do_ref[rows, cs] for cs in heads]
            if sub < r - 1:
                q_nxt, do_nxt, valid = [q_ref[nxt, cs] for cs in heads], [do_ref[nxt, cs] for cs in heads], band
            else:
                q_nxt, do_nxt = [qn_ref[:, cs] for cs in heads], [don_ref[:, cs] for cs in heads]
                valid = band & ((col < st) | (j < nt - 1))
            q = jnp.concatenate(q_cur + q_nxt, axis=0)
            do = jnp.concatenate(do_cur + do_nxt, axis=0)
            pt = jnp.exp(jnp.where(valid, _dot_nt(k, q), NEG))
            ds = (pt * _dot_nt(v, do)).astype(BF16)
            dv = _dot(pt.astype(BF16), do)
            dk = _dot(ds, q)
            dq = _dot_tn(ds, k)
            for g, cs in enumerate(heads):
                acc_ref[rows, cs] += dq[WINDOW * g:WINDOW * (g + 1), :]
                dqn = dq[st + WINDOW * g:st + WINDOW * (g + 1), :]
                if sub < r - 1:
                    acc_ref[nxt, cs] += dqn
                else:
                    carry_ref[:, cs] = dqn
                sink = jnp.max(sink_ref[kv, g:g + 1, :], axis=1, keepdims=True)
                p_sink = jnp.exp(sink + _lane_sum(q_cur[g].astype(F32), 67, 70))
                term = jnp.sum(p_sink * _lane_sum(do_cur[g].astype(F32), 64, 67), axis=0, keepdims=True)
                dsink_ref[kv, g:g + 1, :] += jnp.broadcast_to(term, (1, CHUNK))
            da_ref[rows, dk_col + CHUNK * kv:dk_col + CHUNK * (kv + 1)] = _rope_inv(
                jnp.where(low, dk, 0.0), cos_t[rows, :], sa_t[rows, :], sb_t[rows, :]).astype(BF16)
            da_ref[rows, dv_col + CHUNK * kv:dv_col + CHUNK * (kv + 1)] = jnp.where(low, dv, 0.0).astype(BF16)
        for pp in range(N_HEADS // 2):
            d = _pair(acc_ref[:, CHUNK * 2 * pp:CHUNK * (2 * pp + 1)], acc_ref[:, CHUNK * (2 * pp + 1):CHUNK * (2 * pp + 2)])
            da_ref[:, CHUNK * pp:CHUNK * (pp + 1)] = (_rope_inv(d, cos_t, sa_t, sb_t) * Q_SCALE).astype(BF16)

    cur = lambda w: pl.BlockSpec((tk, w), lambda j: (j, 0))
    nxt = pl.BlockSpec((WINDOW, AUG_W), lambda j: (jnp.minimum((j + 1) * r, nb - 1), 0))
    whole = pl.BlockSpec((KV_GROUPS, 8, CHUNK), lambda j: (0, 0, 0))
    return pl.pallas_call(
        body, name="swa_bwd", grid=(nt,),
        out_shape=[jax.ShapeDtypeStruct((s, 1024), BF16), jax.ShapeDtypeStruct((KV_GROUPS, 8, CHUNK), F32)],
        in_specs=[cur(KV_GROUPS * CHUNK), cur(KV_GROUPS * CHUNK), cur(AUG_W), nxt, cur(AUG_W), nxt,
                  cur(CHUNK), cur(CHUNK), cur(CHUNK), whole],
        out_specs=[cur(1024), whole],
        scratch_shapes=[pltpu.VMEM((WINDOW, AUG_W), F32), pltpu.VMEM((tk, AUG_W), F32)],
        compiler_params=_params(1))(k_aug, v_aug, q_bwd, q_bwd, do_aug, do_aug, cos, sa, sb, sink_rows)


def _fox_fwd(first_key, q_aug, k_aug, v_aug, t):
    s = q_aug.shape[0]
    pw = 2 * CHUNK

    def body(lo_ref, q_ref, k_ref, v_ref, o_ref, qb_ref):
        i = pl.program_id(1)
        first = lo_ref[pl.program_id(0), i]
        row = lax.broadcasted_iota(jnp.int32, (t, t), 0)
        col = lax.broadcasted_iota(jnp.int32, (t, t), 1)
        lane = _lane(CHUNK)
        heads = (slice(0, CHUNK), slice(CHUNK, pw))
        qs = [q_ref[:, hs] for hs in heads]

        def step(jb, carry, masked):
            rows = pl.ds(pl.multiple_of(jb * t, t), t)
            scs = [_dot_nt(qs[hh], k_ref[rows, hs]) for hh, hs in enumerate(heads)]
            if masked:
                scs = [jnp.where(col <= row, sc, NEG) for sc in scs]
            m_news = [jnp.maximum(carry[2 * hh], jnp.max(scs[hh], axis=1, keepdims=True)) for hh in range(2)]
            ps = [jnp.exp(scs[hh] - m_news[hh]).astype(BF16) for hh in range(2)]
            new = []
            for hh, hs in enumerate(heads):
                m, acc = carry[2 * hh], carry[2 * hh + 1]
                new += [m_news[hh], jnp.exp(m - m_news[hh]) * acc + _dot(ps[hh], v_ref[rows, hs])]
            return tuple(new)

        init = (jnp.full((t, 1), NEG, F32), jnp.zeros((t, CHUNK), F32)) * 2
        carry = step(i, lax.fori_loop(first, i, lambda jb, c: step(jb, c, False), init), True)
        outs = []
        for hh, hs in enumerate(heads):
            m, acc = carry[2 * hh], carry[2 * hh + 1]
            denom = _lane_sum(acc, 64, 65)
            outs.append(acc / denom)
            lse = m + jnp.log(denom)
            qb_ref[:, hs] = _place3(lane, 67, -lse, qs[hh].astype(F32)).astype(BF16)
        o_ref[...] = _pair(outs[0], outs[1])

    return pl.pallas_call(
        body, name="fox_fwd", grid=(N_HEADS // 2, s // t),
        out_shape=[jax.ShapeDtypeStruct((s, 512), F32), jax.ShapeDtypeStruct((s, AUG_W), BF16)],
        in_specs=[pl.BlockSpec(memory_space=pltpu.SMEM),
                  pl.BlockSpec((t, pw), lambda p, i: (i, p)),
                  pl.BlockSpec((s, pw), lambda p, i: (0, p)),
                  pl.BlockSpec((s, pw), lambda p, i: (0, p))],
        out_specs=[pl.BlockSpec((t, CHUNK), lambda p, i: (i, p)), pl.BlockSpec((t, pw), lambda p, i: (i, p))],
        compiler_params=_params(2))(first_key, q_aug, k_aug, v_aug)


def _fox_block_ranges(bounds, norms, r):
    cum_first = bounds[0::r, 0, :N_HEADS]
    cum_last = bounds[r - 1::r, 1, :N_HEADS]
    n = cum_first.shape[0]
    reach = 2.0 * 1.02 * jnp.sqrt(norms[0, :N_HEADS] * norms[1, :N_HEADS]) + 105.0
    blk = jnp.arange(n, dtype=jnp.int32)
    decay = cum_last[None, :, :] - cum_first[:, None, :]
    skip = (decay > reach[None, None, :]) & (blk[None, :, None] < blk[:, None, None])
    first = jnp.min(jnp.where(skip, n, blk[None, :, None]), axis=1)
    first = jnp.minimum(first[:, 0::2], first[:, 1::2]).T
    needed = (first[:, :, None] <= blk[None, None, :]) & (blk[None, :, None] >= blk[None, None, :])
    last = jnp.max(jnp.where(needed, blk[None, :, None], blk[None, None, :]), axis=1)
    return first.astype(jnp.int32), last.astype(jnp.int32)


def _fox_bwd(last_query, k_aug, v_aug, q_bwd, do_aug, t):
    s = k_aug.shape[0]
    n = s // t
    pw = 2 * CHUNK

    def body(hi_ref, k_ref, v_ref, q_ref, do_ref, dk_ref, dv_ref, dck_ref, dq_ref, dcq_ref, dq_scr):
        j = pl.program_id(1)
        last = hi_ref[pl.program_id(0), j]

        @pl.when(j == 0)
        def _():
            dq_scr[...] = jnp.zeros_like(dq_scr)

        row = lax.broadcasted_iota(jnp.int32, (t, t), 0)
        col = lax.broadcasted_iota(jnp.int32, (t, t), 1)
        lane = _lane(CHUNK)
        heads = (slice(0, CHUNK), slice(CHUNK, pw))
        ks = [k_ref[:, hs] for hs in heads]
        vs = [v_ref[:, hs] for hs in heads]

        def step(ib, carry, masked):
            rows = pl.ds(pl.multiple_of(ib * t, t), t)
            new = []
            for hh, hs in enumerate(heads):
                dv, dk = carry[2 * hh], carry[2 * hh + 1]
                q, do = q_ref[rows, hs], do_ref[rows, hs]
                st = _dot_nt(ks[hh], q)
                if masked:
                    st = jnp.where(row <= col, st, NEG)
                pt = jnp.exp(st)
                ds = (pt * _dot_nt(vs[hh], do)).astype(BF16)
                new += [dv + _dot(pt.astype(BF16), do), dk + _dot(ds, q)]
                dq_scr[rows, hs] += _dot_tn(ds, ks[hh])
            return tuple(new)

        zero = jnp.zeros((t, CHUNK), F32)
        carry = lax.fori_loop(j + 1, last + 1, lambda ib, c: step(ib, c, False), step(j, (zero,) * 4, True))
        dvs, dks = (carry[0], carry[2]), (carry[1], carry[3])
        dk_ref[...] = _pair(dks[0], dks[1]).astype(BF16)
        dv_ref[...] = _pair(dvs[0], dvs[1]).astype(BF16)
        dck_ref[0] = jnp.where(lane == 0, pltpu.roll(dks[0], 64, 1),
                               jnp.where(lane == 1, pltpu.roll(dks[1], 65, 1), 0.0))

        @pl.when(j == n - 1)
        def _():
            for ib in range(n):
                rows = slice(t * ib, t * (ib + 1))
                d0, d1 = dq_scr[rows, 0:CHUNK], dq_scr[rows, CHUNK:pw]
                dq_ref[rows, :] = (_pair(d0, d1) * Q_SCALE).astype(BF16)
                dcq_ref[0, rows, :] = jnp.where(lane == 0, pltpu.roll(d0, CHUNK - 67, 1),
                                                jnp.where(lane == 1, pltpu.roll(d1, CHUNK - 66, 1), 0.0))

    return pl.pallas_call(
        body, name="fox_bwd", grid=(N_HEADS // 2, n),
        out_shape=[jax.ShapeDtypeStruct((s, 512), BF16), jax.ShapeDtypeStruct((s, 512), BF16),
                   jax.ShapeDtypeStruct((N_HEADS // 2, s, CHUNK), F32), jax.ShapeDtypeStruct((s, 512), BF16),
                   jax.ShapeDtypeStruct((N_HEADS // 2, s, CHUNK), F32)],
        in_specs=[pl.BlockSpec(memory_space=pltpu.SMEM),
                  pl.BlockSpec((t, pw), lambda p, j: (j, p)), pl.BlockSpec((t, pw), lambda p, j: (j, p)),
                  pl.BlockSpec((s, pw), lambda p, j: (0, p)), pl.BlockSpec((s, pw), lambda p, j: (0, p))],
        out_specs=[pl.BlockSpec((t, CHUNK), lambda p, j: (j, p)), pl.BlockSpec((t, CHUNK), lambda p, j: (j, p)),
                   pl.BlockSpec((1, t, CHUNK), lambda p, j: (p, j, 0)), pl.BlockSpec((s, CHUNK), lambda p, j: (0, p)),
                   pl.BlockSpec((1, s, CHUNK), lambda p, j: (p, 0, 0))],
        scratch_shapes=[pltpu.VMEM((s, pw), F32)],
        compiler_params=_params(2))(last_query, k_aug, v_aug, q_bwd, do_aug)


def _fgate_bwd(dcq, dck, xf, tm):
    s = xf.shape[0]
    nt = s // tm

    def body(dq_ref, dc_ref, xf_ref, df_ref, dbf_ref, carry_ref):
        @pl.when(pl.program_id(0) == 0)
        def _():
            carry_ref[...] = jnp.zeros_like(carry_ref)
            dbf_ref[...] = jnp.zeros_like(dbf_ref)

        row = lax.broadcasted_iota(jnp.int32, (tm, tm), 0)
        col = lax.broadcasted_iota(jnp.int32, (tm, tm), 1)
        tri = jnp.where(col >= row, 1.0, 0.0).astype(BF16)
        dcum = dq_ref[0] - dc_ref[0]
        for p in range(1, N_HEADS // 2):
            dcum = dcum + pltpu.roll(dq_ref[p] - dc_ref[p], 2 * p, 1)
        hi, mid, lo = _split3(dcum)
        dlogf = _dot(tri, hi.astype(BF16)) + _dot(tri, mid.astype(BF16)) + _dot(tri, lo.astype(BF16))
        dlogf = dlogf + carry_ref[0:1, :]
        carry_ref[0:1, :] += jnp.sum(dcum, axis=0, keepdims=True)
        df = dlogf * _sigmoid(-xf_ref[...])
        df_ref[...] = df.astype(BF16)
        dbf_ref[0:1, :] += jnp.sum(df, axis=0, keepdims=True)

    rev = pl.BlockSpec((tm, CHUNK), lambda i: (nt - 1 - i, 0))
    rev4 = pl.BlockSpec((N_HEADS // 2, tm, CHUNK), lambda i: (0, nt - 1 - i, 0))
    return pl.pallas_call(
        body, name="fgate_bwd", grid=(nt,),
        out_shape=[jax.ShapeDtypeStruct((s, CHUNK), BF16), jax.ShapeDtypeStruct((8, CHUNK), F32)],
        in_specs=[rev4, rev4, rev], out_specs=[rev, pl.BlockSpec((8, CHUNK), lambda i: (0, 0))],
        scratch_shapes=[pltpu.VMEM((8, CHUNK), F32)],
        compiler_params=_params(1))(dcq, dck, xf)


def _post(x, tgt, att_a, att_b, za, zb, ga, gb, woa_t, wob_t, wout, vec, ind, emat, tm):
    s = x.shape[0]

    def body(x_ref, t_ref, aa_ref, ab_ref, za_ref, zb_ref, ga_ref, gb_ref, woat_ref, wobt_ref, wout_ref,
             vec_ref, ind_ref, e_ref,
             dx1_ref, ua_ref, ub_ref, mg_ref, do_ref, dya_ref, dyb_ref, doa_ref, dob_ref, dzg_ref, acc_ref):
        @pl.when(pl.program_id(0) == 0)
        def _():
            acc_ref[...] = jnp.zeros_like(acc_ref)

        gate, gfin = vec_ref[0:1, :], vec_ref[1:2, :]
        inv_d = 1.0 / D_MODEL

        def branch_fwd(att_ref, z_ref, w_ref, u_ref):
            att, z = att_ref[...], z_ref[...]
            sz = _sigmoid(z)
            silu = z * sz
            u = (att * silu).astype(BF16)
            u_ref[...] = u.T
            return att, z, sz, silu, _dot_nt(u, w_ref[...])

        att_a, z_a, sz_a, silu_a, y_a = branch_fwd(aa_ref, za_ref, woat_ref, ua_ref)
        att_b, z_b, sz_b, silu_b, y_b = branch_fwd(ab_ref, zb_ref, wobt_ref, ub_ref)
        sg_a, sg_b = _sigmoid(ga_ref[...]), _sigmoid(gb_ref[...])
        merged = (sg_a * y_a + sg_b * y_b).astype(BF16)
        mg_ref[...] = merged.T
        o = _dot(merged, wout_ref[...])
        x1 = x_ref[...] + gate * o
        rstd = lax.rsqrt(jnp.mean(x1 * x1, axis=-1, keepdims=True) + NORM_EPS)
        xh = x1 * rstd
        diff = xh * gfin - t_ref[...]
        acc_ref[2:3, :] += (0.5 * inv_d) * jnp.sum(diff * diff, axis=0, keepdims=True)
        dy = diff * inv_d
        acc_ref[1:2, :] += jnp.sum(dy * xh, axis=0, keepdims=True)
        dyg = dy * gfin
        dx1 = rstd * (dyg - xh * jnp.mean(dyg * xh, axis=-1, keepdims=True))
        dx1_ref[...] = dx1
        acc_ref[0:1, :] += jnp.sum(dx1 * o, axis=0, keepdims=True)
        d_o = (dx1 * gate).astype(BF16)
        do_ref[...] = d_o
        dmg = _dot_nt(d_o, wout_ref[...])

        lane = _lane(CHUNK)
        low = lane < HEAD_DIM
        lane_all = _lane(AUG_W) & (CHUNK - 1)

        zg0 = SEG["za"][0]

        def branch_bwd(sg, y, att, z, sz, silu, wt_ref, dy_ref, g_name, z_name, doaug_ref):
            dyb = (dmg * sg).astype(BF16)
            dy_ref[...] = dyb
            g_off, z_off = SEG[g_name][0] - zg0, SEG[z_name][0] - zg0
            dzg_ref[:, g_off:g_off + D_MODEL] = (dmg * y * sg * (1.0 - sg)).astype(BF16)
            du = _dot(dyb, wt_ref[...])
            datt = du * silu
            dzg_ref[:, z_off:z_off + 512] = (du * att * (sz * (1.0 + z * (1.0 - sz)))).astype(BF16)
            extra = _spread3(-_dot_split(datt * att, ind_ref[...]), e_ref)
            for p in range(N_HEADS // 2):
                c = datt[:, CHUNK * p:CHUNK * (p + 1)]
                for hh in range(2):
                    hd = 2 * p + hh
                    src = c if hh == 0 else pltpu.roll(c, HEAD_DIM, 1)
                    doaug_ref[:, CHUNK * hd:CHUNK * (hd + 1)] = jnp.where(
                        low, src, extra[:, CHUNK * hd:CHUNK * (hd + 1)]).astype(BF16)

        branch_bwd(sg_a, y_a, att_a, z_a, sz_a, silu_a, woat_ref, dya_ref, "ga", "za", doa_ref)
        branch_bwd(sg_b, y_b, att_b, z_b, sz_b, silu_b, wobt_ref, dyb_ref, "gb", "zb", dob_ref)

    row_blk = lambda w: pl.BlockSpec((tm, w), lambda i: (i, 0))
    full = lambda a: pl.BlockSpec(a.shape, lambda i: (0,) * a.ndim)
    sds = lambda w, dt: jax.ShapeDtypeStruct((s, w), dt)
    tds = lambda w: jax.ShapeDtypeStruct((w, s), BF16)
    col_blk = lambda w: pl.BlockSpec((w, tm), lambda i: (0, i))
    out_shape = [sds(D_MODEL, F32), tds(512), tds(512), tds(D_MODEL), sds(D_MODEL, BF16),
                 sds(D_MODEL, BF16), sds(D_MODEL, BF16), sds(AUG_W, BF16), sds(AUG_W, BF16), sds(3072, BF16),
                 jax.ShapeDtypeStruct((8, D_MODEL), F32)]
    ins = [x, tgt, att_a, att_b, za, zb, ga, gb, woa_t, wob_t, wout, vec, ind, emat]
    in_specs = [row_blk(a.shape[1]) for a in ins[:8]] + [full(a) for a in ins[8:]]
    out_specs = ([row_blk(D_MODEL), col_blk(512), col_blk(512), col_blk(D_MODEL)]
                 + [row_blk(o.shape[1]) for o in out_shape[4:-1]] + [pl.BlockSpec((8, D_MODEL), lambda i: (0, 0))])
    return pl.pallas_call(body, name="post", grid=(s // tm,), out_shape=out_shape, in_specs=in_specs,
                          out_specs=out_specs, compiler_params=_params(1))(*ins)


def _bwd_pre(dproj, w_all, x, dx1, mod, tm, chip_halves, modes):
    s = x.shape[0]
    ng, nh = len(DPROJ_GROUPS), len(chip_halves)
    last = s // tm - 1

    def body(*refs):
        dp_refs = refs[:ng]
        wt_ref, x_ref, dx1_ref, mod_ref = refs[ng:ng + 4]
        send_refs = refs[ng + 4:ng + 4 + nh]
        gx_ref, acc_ref = refs[ng + 4 + nh:ng + 6 + nh]
        recv_refs = refs[ng + 6 + nh:ng + 6 + 2 * nh]
        sems = refs[ng + 6 + 2 * nh:]

        @pl.when(pl.program_id(0) == 0)
        def _():
            acc_ref[...] = jnp.zeros_like(acc_ref)
            _exchange_start(_exchange_copies(send_refs, recv_refs, modes, *sems))

        dh = None
        for dp_ref, (_, off, w) in zip(dp_refs, DPROJ_GROUPS):
            part = _dot_nt(dp_ref[...], wt_ref[:, off:off + w])
            dh = part if dh is None else dh + part
        xv = x_ref[...]
        rstd = lax.rsqrt(jnp.mean(xv * xv, axis=-1, keepdims=True) + NORM_EPS)
        xh = xv * rstd
        gn = mod_ref[0:1, :]
        acc_ref[0:1, :] += jnp.sum(dh, axis=0, keepdims=True)
        acc_ref[1:2, :] += jnp.sum(dh * (xh * gn), axis=0, keepdims=True)
        dn = dh * (1.0 + mod_ref[1:2, :])
        acc_ref[2:3, :] += jnp.sum(dn * xh, axis=0, keepdims=True)
        dxh = dn * gn
        gx_ref[...] = dx1_ref[...] + rstd * (dxh - xh * jnp.mean(dxh * xh, axis=-1, keepdims=True))

        @pl.when(pl.program_id(0) == last)
        def _():
            _exchange_wait(_exchange_copies(send_refs, recv_refs, modes, *sems))

    row_blk = lambda w: pl.BlockSpec((tm, w), lambda i: (i, 0))
    full = lambda a: pl.BlockSpec(a.shape, lambda i: (0,) * a.ndim)
    outs = pl.pallas_call(
        body, name="bwd_pre", grid=(s // tm,),
        out_shape=[jax.ShapeDtypeStruct((s, D_MODEL), F32), jax.ShapeDtypeStruct((8, D_MODEL), F32)]
        + _exchange_out_shapes(chip_halves, modes),
        in_specs=[row_blk(w) for _, _, w in DPROJ_GROUPS] + [full(w_all), row_blk(D_MODEL), row_blk(D_MODEL), full(mod)]
        + [ANY_SPEC] * nh,
        out_specs=[row_blk(D_MODEL), pl.BlockSpec((8, D_MODEL), lambda i: (0, 0))] + [ANY_SPEC] * nh,
        scratch_shapes=_exchange_sems(nh),
        compiler_params=_params(1))(*dproj, w_all, x, dx1, mod, *chip_halves)
    return outs[0], outs[1], outs[2:]


def _token_matmul_multi(a_t, bs, tk, name):
    m, s = a_t.shape
    nb = len(bs)

    def body(*refs):
        a_ref, b_refs, o_refs = refs[0], refs[1:1 + nb], refs[1 + nb:]
        a = a_ref[...]
        for b_ref, o_ref in zip(b_refs, o_refs):
            @pl.when(pl.program_id(0) == 0)
            def _():
                o_ref[...] = jnp.zeros_like(o_ref)

            o_ref[...] += _dot(a, b_ref[...])

    return pl.pallas_call(
        body, name=name, grid=(s // tk,),
        out_shape=[jax.ShapeDtypeStruct((m, b.shape[1]), F32) for b in bs],
        in_specs=[pl.BlockSpec((m, tk), lambda k: (0, k))] + [pl.BlockSpec((tk, b.shape[1]), lambda k: (k, 0)) for b in bs],
        out_specs=[pl.BlockSpec((m, b.shape[1]), lambda k: (0, 0)) for b in bs],
        compiler_params=_params(1))(a_t, *bs)


def _adam_math(g, w, m, v):
    m2 = ADAM_B1 * m + (1.0 - ADAM_B1) * g
    v2 = ADAM_B2 * v + (1.0 - ADAM_B2) * (g * g)
    delta = -ADAM_LR * ((m2 / ADAM_C1) / (jnp.sqrt(v2 / ADAM_C2) + ADAM_EPS) + ADAM_WD * w)
    return delta, m2, v2


def _adamw_parts(full, parts, others, w, m, v, tr, name):
    _, rws, cols = w.shape

    def body(oth_ref, f_ref, p_ref, w_ref, m_ref, v_ref, g_ref, d_ref, m2_ref, v2_ref):
        g = None
        for chip in range(N_CHIPS):
            part = (1.0 - oth_ref[chip]) * f_ref[chip] + oth_ref[chip] * p_ref[chip].astype(F32)
            g = part if g is None else g + part
        g_ref[0] = g
        d_ref[0], m2_ref[0], v2_ref[0] = _adam_math(g, w_ref[0], m_ref[0], v_ref[0])

    blk = pl.BlockSpec((1, tr, cols), lambda i: (0, i, 0))
    slots = pl.BlockSpec((N_CHIPS, tr, cols), lambda i: (0, i, 0))
    o = jax.ShapeDtypeStruct((1, rws, cols), F32)
    return pl.pallas_call(
        body, name=name, grid=(rws // tr,), out_shape=[o, o, o, o],
        in_specs=[pl.BlockSpec(memory_space=pltpu.SMEM), slots, slots, blk, blk, blk],
        out_specs=[blk, blk, blk, blk], compiler_params=_params(1))(others, full, parts, w, m, v)


def _adamw_devices(parts, w, m, v, tr, name):
    _, rws, cols = w.shape

    def body(p_ref, w_ref, m_ref, v_ref, g_ref, d_ref, m2_ref, v2_ref):
        g = p_ref[0].astype(F32)
        for dev in range(1, N_DEV):
            g = g + p_ref[dev].astype(F32)
        g_ref[0] = g
        d_ref[0], m2_ref[0], v2_ref[0] = _adam_math(g, w_ref[0], m_ref[0], v_ref[0])

    blk = pl.BlockSpec((1, tr, cols), lambda i: (0, i, 0))
    o = jax.ShapeDtypeStruct((1, rws, cols), F32)
    return pl.pallas_call(
        body, name=name, grid=(rws // tr,), out_shape=[o, o, o, o],
        in_specs=[pl.BlockSpec((N_DEV, tr, cols), lambda i: (0, i, 0)), blk, blk, blk],
        out_specs=[blk, blk, blk, blk], compiler_params=_params(1))(parts, w, m, v)


def _adamw_ada(c_lanes, d_rows, w, m, v, tr):
    _, rws, cols = w.shape

    def body(c_ref, d_ref, w_ref, m_ref, v_ref, g_ref, dl_ref, m2_ref, v2_ref):
        for k in range(cols // CHUNK):
            cs = slice(CHUNK * k, CHUNK * (k + 1))
            g = c_ref[0] * d_ref[0:1, cs]
            for b in range(1, N_DEV):
                g = g + c_ref[b] * d_ref[b:b + 1, cs]
            g_ref[0, :, cs] = g
            dl_ref[0, :, cs], m2_ref[0, :, cs], v2_ref[0, :, cs] = _adam_math(
                g, w_ref[0, :, cs], m_ref[0, :, cs], v_ref[0, :, cs])

    blk = pl.BlockSpec((1, tr, cols), lambda i: (0, i, 0))
    o = jax.ShapeDtypeStruct((1, rws, cols), F32)
    return pl.pallas_call(
        body, name="adamw_ada", grid=(rws // tr,), out_shape=[o, o, o, o],
        in_specs=[pl.BlockSpec((N_DEV, tr, CHUNK), lambda i: (0, i, 0)), pl.BlockSpec((N_DEV, cols), lambda i: (0, 0)),
                  blk, blk, blk],
        out_specs=[blk, blk, blk, blk], compiler_params=_params(1))(c_lanes, d_rows, w, m, v)


def _adamw_small(packs, w, m, v):
    def body(p_ref, w_ref, m_ref, v_ref, g_ref, d_ref, m2_ref, v2_ref, loss_ref):
        g = p_ref[0]
        for dev in range(1, N_DEV):
            g = g + p_ref[dev]
        g_ref[...] = g
        d_ref[...], m2_ref[...], v2_ref[...] = _adam_math(g, w_ref[...], m_ref[...], v_ref[...])
        loss_ref[...] = jnp.broadcast_to(jnp.sum(g_ref[2:3, 0:D_MODEL], axis=1, keepdims=True), loss_ref.shape)

    o = jax.ShapeDtypeStruct(w.shape, F32)
    return pl.pallas_call(body, name="adamw_small", out_shape=[o, o, o, o, jax.ShapeDtypeStruct((8, CHUNK), F32)],
                          compiler_params=_params(0))(packs, w, m, v)


def _tile(s, want):
    return min(s, want)


def _layout_pieces():
    orig = {"qa": 0, "ka": 512, "va": 640, "za": 768, "qb": 1280, "kb": 1792, "vb": 2304, "f": 2816, "zb": 2824,
            "ga": 3336, "gb": 4360}
    pieces = []
    for name, (off, w) in SEG.items():
        if name in ("ka", "va"):
            pieces += [(orig[name] + HEAD_DIM * kv, orig[name] + HEAD_DIM * (kv + 1), off + CHUNK * kv) for kv in range(KV_GROUPS)]
        else:
            pieces.append((orig[name], orig[name] + (N_HEADS if name == "f" else w), off))
    return pieces


def _assemble_w_all(win_g):
    cols, pos = [], 0
    for lo, hi, new in sorted(_layout_pieces(), key=lambda t: t[2]):
        if new > pos:
            cols.append(jnp.zeros((D_MODEL, new - pos), win_g.dtype))
        col = lo
        while col < hi:
            dev = col // IN_SHARD
            end = min(hi, (dev + 1) * IN_SHARD)
            cols.append(win_g[dev, :, col - dev * IN_SHARD:end - dev * IN_SHARD])
            col = end
        pos = new + hi - lo
    cols.append(jnp.zeros((D_MODEL, PROJ_W - pos), win_g.dtype))
    return jnp.concatenate(cols, axis=1)


def _grad_slot(dw_groups, dev):
    lo_d, hi_d = dev * IN_SHARD, (dev + 1) * IN_SHARD
    cols = []
    for lo, hi, new in sorted(_layout_pieces()):
        a, b = max(lo, lo_d), min(hi, hi_d)
        if a < b:
            arr, off = next((g, o) for g, (_, o, w) in zip(dw_groups, DPROJ_GROUPS) if o <= new < o + w)
            cols.append(arr[:, new - off + a - lo:new - off + b - lo])
    return jnp.concatenate(cols, axis=1)


def kernel(x, c, positions, w_ada, b_ada, g_norm, w_in, b_f, sinks, w_o_swa, w_o_fox, w_out, g_final, loss_target, m_w_ada, m_b_ada, m_g_norm, m_w_in, m_b_f, m_sinks, m_w_o_swa, m_w_o_fox, m_w_out, m_g_final, v_w_ada, v_b_ada, v_g_norm, v_w_in, v_b_f, v_sinks, v_w_o_swa, v_w_o_fox, v_w_out, v_g_final):
    s = x.shape[1]
    tm = _tile(s, 256)
    ta = _tile(s, 512)
    me = 4 * lax.axis_index("x") + 2 * lax.axis_index("y") + lax.axis_index("c")
    x2, tgt = x[0], loss_target[0]

    inv_freq = np.power(np.float32(ROPE_THETA), -np.arange(0, HEAD_DIM, 2, dtype=np.float32) / HEAD_DIM)
    inv_freq = jnp.asarray(np.tile(inv_freq, CHUNK // (HEAD_DIM // 2))[None, :], F32)
    (cos, sa, sb), (win_g, woa_g, wob_g, wout_g), c_all = _rope_tables_and_gather(
        positions.reshape(s, 1).astype(F32), inv_freq, tm,
        [w_in[0].astype(BF16), w_o_swa[0].T.astype(BF16), w_o_fox[0].T.astype(BF16), w_out[0].astype(BF16)],
        jnp.broadcast_to(c, (8, D_MODEL)))
    c_all = c_all[:, 0, :]
    b_shard = lax.dynamic_slice(b_ada, (0, me * ADA_SHARD), (1, ADA_SHARD))
    ada_part = _ada_fwd(c_all, w_ada[0], b_shard)
    (ada_g,) = _exchange([ada_part], ["gather"], "gather_ada")
    ada = lax.dynamic_index_in_dim(ada_g, me, axis=1, keepdims=False).reshape(3 * D_MODEL)
    shift, scale, gate = ada[:D_MODEL], ada[D_MODEL:2 * D_MODEL], ada[2 * D_MODEL:]
    woa_t = woa_g.reshape(D_MODEL, 512)
    wob_t = wob_g.reshape(D_MODEL, 512)
    wout = wout_g.reshape(D_MODEL, D_MODEL)
    w_all = _assemble_w_all(win_g)

    zrow = jnp.zeros((1, D_MODEL), F32)
    mod = jnp.concatenate([g_norm, scale[None], shift[None], zrow, zrow, zrow, zrow, zrow], axis=0)
    bf_row = jnp.pad(b_f, ((0, 0), (0, CHUNK - N_HEADS)))
    emat_np = np.zeros((3 * CHUNK, AUG_W), np.float32)
    ind_np = np.zeros((512, CHUNK), np.float32)
    for hd in range(N_HEADS):
        for part in range(3):
            emat_np[CHUNK * part + hd, CHUNK * hd + 64 + part] = 1.0
        ind_np[HEAD_DIM * hd:HEAD_DIM * (hd + 1), hd] = 1.0
    emat, ind = jnp.asarray(emat_np, BF16), jnp.asarray(ind_np, BF16)
    sink_rows = jnp.broadcast_to(jnp.pad(sinks.reshape(KV_GROUPS, GROUP), ((0, 0), (0, 8 - GROUP)))[:, :, None],
                                 (KV_GROUPS, 8, CHUNK))

    (h_t, qa, ka, va, za, qb, kb, vb, zb, ga, gb, xf, bounds, norms) = _fwd_proj(
        x2, mod, w_all, cos, sa, sb, bf_row, emat, ind, tm)
    first_key, last_query = _fox_block_ranges(bounds, norms, ta // tm)
    att_a, qa_bwd = _swa_fwd(qa, ka, va, sink_rows, ta)
    att_b, qb_bwd = _fox_fwd(first_key, qb, kb, vb, ta)
    vec = jnp.concatenate([gate[None], g_final[None], zrow, zrow, zrow, zrow, zrow, zrow], axis=0)
    (dx1, ua_t, ub_t, merged_t, d_o, dya, dyb, doa, dob, dzg, post_acc) = _post(
        x2, tgt, att_a, att_b, za, zb, ga, gb, woa_t, wob_t, wout, vec, ind, emat, tm)

    da, dsink = _swa_bwd(ka, va, qa_bwd, doa, cos, sa, sb, sink_rows, ta)
    dkb, dvb, dck, dqb, dcq = _fox_bwd(last_query, kb, vb, qb_bwd, dob, ta)
    dfb, dbf_acc = _fgate_bwd(dcq, dck, xf, ta)
    dproj = [da, dqb, dkb, dvb, dzg, dfb]
    dw_zg, = _token_matmul_multi(h_t, [dzg], ta, "dw_in_zg")
    dw_a, dw_qb, dw_kb, dw_vb, dw_f = _token_matmul_multi(h_t, [da, dqb, dkb, dvb, dfb], ta, "dw_in_rest")
    dw_all = [dw_a, dw_qb, dw_kb, dw_vb, dw_zg, dw_f]

    core, chip = lax.axis_index("c"), 2 * lax.axis_index("x") + lax.axis_index("y")
    win_slots = jnp.stack([jnp.stack([_grad_slot(dw_all, 2 * ch + co) for ch in range(N_CHIPS)]) for co in range(2)])
    to_sibling = lax.dynamic_index_in_dim(win_slots, 1 - core, 0, keepdims=False).astype(BF16)
    (dwoa, dwob, dwout), (win_theirs,) = _token_matmuls_and_swap(
        [(ua_t, dya), (ub_t, dyb), (merged_t, d_o)], ta, [to_sibling])
    win_full, win_half = _chip_partial(jnp.reshape(core, (1,)).astype(jnp.int32), win_slots, win_theirs, 256,
                                       "chip_partial_w_in")
    col_slots = lambda g: g.reshape(512, N_DEV, 128).transpose(1, 0, 2).astype(BF16)
    grad_x, pre_acc, (p_win, p_woa, p_wob, p_wout) = _bwd_pre(
        dproj, w_all, x2, dx1, mod, tm,
        [win_half, col_slots(dwoa), col_slots(dwob), dwout.reshape(N_DEV, 128, D_MODEL).astype(BF16)],
        ["chips", "devices", "devices", "devices"])
    misc = jnp.concatenate([dbf_acc[0, :N_HEADS], dsink[:, :GROUP, 0].reshape(N_HEADS), jnp.zeros((D_MODEL - 16,), F32)])
    zpad = jnp.zeros((2 * D_MODEL,), F32)
    pack = jnp.stack([jnp.concatenate([pre_acc[0], pre_acc[1], post_acc[0]]),
                      jnp.concatenate([pre_acc[2], post_acc[1], misc]),
                      jnp.concatenate([post_acc[2], zpad])] + [jnp.zeros((3 * D_MODEL,), F32)] * 5)
    (packs,) = _exchange([pack], ["gather"], "gather_small_grads")
    others = jnp.where(jnp.arange(N_CHIPS) == chip, 0.0, 1.0).astype(F32)

    g_win, d_win, m_win, v_win = _adamw_parts(win_full, p_win, others, w_in, m_w_in, v_w_in, 128, "adamw_w_in")
    g_woa, d_woa, m_woa, v_woa = _adamw_devices(p_woa, w_o_swa, m_w_o_swa, v_w_o_swa, 128, "adamw_w_o_swa")
    g_wob, d_wob, m_wob, v_wob = _adamw_devices(p_wob, w_o_fox, m_w_o_fox, v_w_o_fox, 128, "adamw_w_o_fox")
    g_wout, d_wout, m_wout, v_wout = _adamw_devices(p_wout, w_out, m_w_out, v_w_out, 128, "adamw_w_out")
    d_ada_rows = lax.dynamic_slice(packs[:, 0, :], (0, me * ADA_SHARD), (N_DEV, ADA_SHARD))
    c_lanes = jnp.broadcast_to(c_all[:, :, None], (N_DEV, D_MODEL, CHUNK))
    g_wada, d_wada, m_wada, v_wada = _adamw_ada(c_lanes, d_ada_rows, w_ada, m_w_ada, v_w_ada, 256)

    def small_pack(bada, gn, gf, bfv, sk):
        misc_w = jnp.concatenate([bfv[0], sk[0], jnp.zeros((D_MODEL - 16,), F32)])
        return jnp.stack([bada[0], jnp.concatenate([gn[0], gf, misc_w])] + [jnp.zeros((3 * D_MODEL,), F32)] * 6)

    sm = _adamw_small(packs, small_pack(b_ada, g_norm, g_final, b_f, sinks),
                      small_pack(m_b_ada, m_g_norm, m_g_final, m_b_f, m_sinks),
                      small_pack(v_b_ada, v_g_norm, v_g_final, v_b_f, v_sinks))
    loss = sm[4][0, 0]

    def unpack(p):
        return (p[0][None], p[1, :D_MODEL][None], p[1, 2 * D_MODEL:2 * D_MODEL + 8][None],
                p[1, 2 * D_MODEL + 8:2 * D_MODEL + 16][None], p[1, D_MODEL:2 * D_MODEL])

    outs = []
    for big, small in (((g_wada, g_win, g_woa, g_wob, g_wout), sm[0]), ((d_wada, d_win, d_woa, d_wob, d_wout), sm[1]),
                       ((m_wada, m_win, m_woa, m_wob, m_wout), sm[2]), ((v_wada, v_win, v_woa, v_wob, v_wout), sm[3])):
        wada_o, win_o, woa_o, wob_o, wout_o = big
        bada_o, gn_o, bf_o, sk_o, gf_o = unpack(small)
        outs += [wada_o, bada_o, gn_o, win_o, bf_o, sk_o, woa_o, wob_o, wout_o, gf_o]
    return (loss, grad_x[None], *outs)
```

```python
import numpy as np
import jax
import jax.numpy as jnp
from jax import lax
from jax.experimental import pallas as pl
from jax.experimental.pallas import tpu as pltpu

F32 = jnp.float32
BF16 = jnp.bfloat16

D_MODEL = 1024
HEAD_DIM = 64
N_HEADS = 8
KV_GROUPS = 2
GROUP = N_HEADS // KV_GROUPS
WINDOW = 128
CHUNK = 128
AUG_W = N_HEADS * CHUNK
N_DEV = 8
IN_SHARD = 673
ADA_SHARD = 384
NORM_EPS = 1e-6
ROPE_THETA = 10000.0
Q_SCALE = HEAD_DIM ** -0.5
NEG = -1e30
UNDERFLOW = 89.0

ADAM_LR = 0.001
ADAM_B1 = 0.9
ADAM_B2 = 0.999
ADAM_EPS = 1e-08
ADAM_WD = 0.01
ADAM_STEP = 10
ADAM_C1 = 1.0 - ADAM_B1 ** ADAM_STEP
ADAM_C2 = 1.0 - ADAM_B2 ** ADAM_STEP

SEG = {}
_off = 0
for _name, _w in (("qa", 512), ("ka", 256), ("va", 256), ("qb", 512), ("kb", 512), ("vb", 512),
                  ("za", 512), ("zb", 512), ("ga", 1024), ("gb", 1024), ("f", 128)):
    SEG[_name] = (_off, _w)
    _off += _w
PROJ_W = _off
DPROJ_GROUPS = (("a", SEG["qa"][0], 1024), ("qb", SEG["qb"][0], 512), ("kb", SEG["kb"][0], 512),
                ("vb", SEG["vb"][0], 512), ("zg", SEG["za"][0], 3072), ("f", SEG["f"][0], 128))

VMEM_LIMIT = 56 * 1024 * 1024


def _params(n_axes):
    return pltpu.CompilerParams(dimension_semantics=("arbitrary",) * n_axes, vmem_limit_bytes=VMEM_LIMIT)


def _dot(a, b):
    return jnp.dot(a, b, preferred_element_type=F32)


def _dot_nt(a, b):
    return lax.dot_general(a, b, (((1,), (1,)), ((), ())), preferred_element_type=F32)


def _dot_tn(a, b):
    return lax.dot_general(a, b, (((0,), (0,)), ((), ())), preferred_element_type=F32)


def _lane(n):
    return lax.broadcasted_iota(jnp.int32, (1, n), 1)


def _split3(x):
    hi = x.astype(BF16).astype(F32)
    r = x - hi
    mid = r.astype(BF16).astype(F32)
    lo = (r - mid).astype(BF16).astype(F32)
    return hi, mid, lo


def _dot_split(x, b):
    hi, mid, lo = _split3(x)
    return _dot(hi.astype(BF16), b) + _dot(mid.astype(BF16), b) + _dot(lo.astype(BF16), b)


def _spread3(x, e3_ref):
    return _dot(jnp.concatenate(_split3(x), axis=1).astype(BF16), e3_ref[...])


def _place3(lane, base, x, other):
    hi, mid, lo = _split3(x)
    return jnp.where(lane == base, hi, jnp.where(lane == base + 1, mid, jnp.where(lane == base + 2, lo, other)))


def _lane_sum(x, lo, hi):
    lane = _lane(x.shape[1])
    return jnp.sum(jnp.where((lane >= lo) & (lane < hi), x, 0.0), axis=1, keepdims=True)


def _ones_lanes(lo, hi):
    lane = _lane(CHUNK)
    return jnp.where((lane >= lo) & (lane < hi), 1.0, 0.0).astype(F32)


def _rope(c, cos, sa, sb):
    return c * cos + pltpu.roll(c, CHUNK - 32, 1) * sa + pltpu.roll(c, 32, 1) * sb


def _rope_inv(d, cos, sa, sb):
    return d * cos - (pltpu.roll(d, CHUNK - 32, 1) * sa + pltpu.roll(d, 32, 1) * sb)


def _pair(c0, c1):
    return jnp.where(_lane(CHUNK) < HEAD_DIM, c0, pltpu.roll(c1, HEAD_DIM, 1))


def _sigmoid(x):
    return 0.5 * jnp.tanh(0.5 * x) + 0.5


N_CHIPS = 4
ANY_SPEC = pl.BlockSpec(memory_space=pl.ANY)


def _exchange(arrays, modes, name):
    n = len(arrays)

    def body(*refs):
        copies = _exchange_copies(refs[:n], refs[n:2 * n], modes, *refs[2 * n:])
        _exchange_start(copies)
        _exchange_wait(copies)

    return pl.pallas_call(
        body, name=name, out_shape=_exchange_out_shapes(arrays, modes),
        in_specs=[ANY_SPEC] * n, out_specs=[ANY_SPEC] * n, scratch_shapes=_exchange_sems(n),
    )(*arrays)


def _exchange_out_shapes(arrays, modes):
    return [jax.ShapeDtypeStruct((N_DEV,) + a.shape if md == "gather" else a.shape, a.dtype)
            for a, md in zip(arrays, modes)]


def _exchange_sems(n):
    return [pltpu.SemaphoreType.DMA((n, N_DEV - 1)), pltpu.SemaphoreType.DMA((n, N_DEV - 1)),
            pltpu.SemaphoreType.DMA((n,))]


def _exchange_copies(srcs, dsts, modes, send_sems, recv_sems, loc_sems):
    n = len(srcs)
    x, y, c = lax.axis_index("x"), lax.axis_index("y"), lax.axis_index("c")

    def slot(i, px, py, pc):
        return 2 * px + py if modes[i] == "chips" else 4 * px + 2 * py + pc

    def src_of(i, px, py, pc):
        return srcs[i] if modes[i] == "gather" else srcs[i].at[slot(i, px, py, pc)]

    local = [pltpu.make_async_copy(src_of(i, x, y, c), dsts[i].at[slot(i, x, y, c)], loc_sems.at[i])
             for i in range(n)]
    sends, recvs = [], []
    for r in (1, 2, 4, 3, 5, 6, 7):
        px = 1 - x if r & 4 else x
        py = 1 - y if r & 2 else y
        pc = 1 - c if r & 1 else c
        for i in range(n):
            if modes[i] == "chips" and r & 1:
                continue
            sems = dict(send_sem=send_sems.at[i, r - 1], recv_sem=recv_sems.at[i, r - 1],
                        device_id=(px, py, pc), device_id_type=pl.DeviceIdType.MESH)
            sends.append(pltpu.make_async_remote_copy(
                src_ref=src_of(i, px, py, pc), dst_ref=dsts[i].at[slot(i, x, y, c)], **sems))
            recvs.append(pltpu.make_async_remote_copy(
                src_ref=src_of(i, px, py, pc), dst_ref=dsts[i].at[slot(i, px, py, pc)], **sems))
    return local, sends, recvs


def _exchange_start(copies):
    local, sends, _ = copies
    for cp in local + sends:
        cp.start()


def _exchange_wait(copies):
    local, sends, recvs = copies
    for cp in recvs:
        cp.wait_recv()
    for cp in sends:
        cp.wait_send()
    for cp in local:
        cp.wait()


def _gather2_parts(srcs, dsts, send_sems, recv_sems, loc_sems):
    n = len(srcs)
    x, y, c = lax.axis_index("x"), lax.axis_index("y"), lax.axis_index("c")
    me, sibling = (x, y, c), (x, y, 1 - c)
    chips = [(1 - x, y), (x, 1 - y), (1 - x, 1 - y)]

    def rows(i, dev):
        return dsts[i].at[4 * dev[0] + 2 * dev[1] + dev[2]]

    def copy(i, k, block, to, src=None):
        return pltpu.make_async_remote_copy(
            src_ref=rows(i, block) if src is None else src, dst_ref=rows(i, block),
            send_sem=send_sems.at[i, k], recv_sem=recv_sems.at[i, k],
            device_id=to, device_id_type=pl.DeviceIdType.MESH)

    local = [pltpu.make_async_copy(srcs[i], rows(i, me), loc_sems.at[i]) for i in range(n)]
    first = []
    for i in range(n):
        first.append(copy(i, 0, me, sibling, src=srcs[i]))
        first += [copy(i, 1 + j, me, (*chip, c), src=srcs[i]) for j, chip in enumerate(chips)]
    return n, c, me, sibling, chips, copy, local, first


def _gather2_start(srcs, dsts, send_sems, recv_sems, loc_sems):
    *_, local, first = _gather2_parts(srcs, dsts, send_sems, recv_sems, loc_sems)
    for cp in local + first:
        cp.start()


def _gather2_finish(srcs, dsts, send_sems, recv_sems, loc_sems):
    n, c, me, sibling, chips, copy, local, first = _gather2_parts(srcs, dsts, send_sems, recv_sems, loc_sems)
    passed = []
    for j, chip in enumerate(chips):
        for i in range(n):
            copy(i, 1 + j, (*chip, c), me).wait_recv()
            cp = copy(i, 4 + j, (*chip, c), sibling)
            cp.start()
            passed.append(cp)
    for i in range(n):
        copy(i, 0, sibling, me).wait_recv()
        for j, chip in enumerate(chips):
            copy(i, 4 + j, (*chip, 1 - c), me).wait_recv()
    for cp in first + passed:
        cp.wait_send()
    for cp in local:
        cp.wait()


def _sibling_copies(srcs, dsts, send_sems, recv_sems):
    x, y, c = lax.axis_index("x"), lax.axis_index("y"), lax.axis_index("c")
    return [pltpu.make_async_remote_copy(
        src_ref=srcs[i].at[k], dst_ref=dsts[i].at[k], send_sem=send_sems.at[i, k], recv_sem=recv_sems.at[i, k],
        device_id=(x, y, 1 - c), device_id_type=pl.DeviceIdType.MESH)
        for i in range(len(srcs)) for k in range(N_CHIPS)]


def _token_matmuls_and_swap(pairs, tk, swap):
    npair, nsw = len(pairs), len(swap)
    s = pairs[0][0].shape[1]
    last = s // tk - 1

    def body(*refs):
        a_refs, b_refs = refs[0:2 * npair:2], refs[1:2 * npair:2]
        src = refs[2 * npair:2 * npair + nsw]
        o_refs = refs[2 * npair + nsw:3 * npair + nsw]
        dst = refs[3 * npair + nsw:3 * npair + 2 * nsw]
        sems = refs[3 * npair + 2 * nsw:]

        @pl.when(pl.program_id(0) == 0)
        def _():
            for cp in _sibling_copies(src, dst, *sems):
                cp.start()

        for a_ref, b_ref, o_ref in zip(a_refs, b_refs, o_refs):
            @pl.when(pl.program_id(0) == 0)
            def _():
                o_ref[...] = jnp.zeros_like(o_ref)

            o_ref[...] += _dot(a_ref[...], b_ref[...])

        @pl.when(pl.program_id(0) == last)
        def _():
            for cp in _sibling_copies(src, dst, *sems):
                cp.wait()

    in_specs, args = [], []
    for a_t, b in pairs:
        in_specs += [pl.BlockSpec((a_t.shape[0], tk), lambda k: (0, k)), pl.BlockSpec((tk, b.shape[1]), lambda k: (k, 0))]
        args += [a_t, b]
    outs = pl.pallas_call(
        body, name="dw_out_projs_swap", grid=(s // tk,),
        out_shape=[jax.ShapeDtypeStruct((a_t.shape[0], b.shape[1]), F32) for a_t, b in pairs]
        + [jax.ShapeDtypeStruct(a.shape, a.dtype) for a in swap],
        in_specs=in_specs + [ANY_SPEC] * nsw,
        out_specs=[pl.BlockSpec((a_t.shape[0], b.shape[1]), lambda k: (0, 0)) for a_t, b in pairs] + [ANY_SPEC] * nsw,
        scratch_shapes=[pltpu.SemaphoreType.DMA((nsw, N_CHIPS)), pltpu.SemaphoreType.DMA((nsw, N_CHIPS))],
        compiler_params=_params(1))(*args, *swap)
    return outs[:npair], outs[npair:]


def _chip_partial(core, slots, theirs, tr, name):
    _, k, rws, cols = slots.shape

    def body(core_ref, a_ref, b_ref, f_ref, h_ref):
        sm = a_ref[0] + b_ref[...].astype(F32)
        f_ref[...] = sm
        h_ref[...] = sm.astype(BF16)

    blk = pl.BlockSpec((1, tr, cols), lambda j, i, core_ref: (j, i, 0))
    mine = pl.BlockSpec((1, 1, tr, cols), lambda j, i, core_ref: (core_ref[0], j, i, 0))
    return pl.pallas_call(
        body, name=name,
        grid_spec=pltpu.PrefetchScalarGridSpec(num_scalar_prefetch=1, grid=(k, rws // tr),
                                               in_specs=[mine, blk], out_specs=[blk, blk]),
        out_shape=[jax.ShapeDtypeStruct(theirs.shape, F32), jax.ShapeDtypeStruct(theirs.shape, BF16)],
        compiler_params=_params(2))(core, slots, theirs)


def _ada_fwd(c_all, w_shard, b_shard):
    def body(c_ref, w_ref, b_ref, o_ref):
        ch, cm, cl = [t.astype(BF16) for t in _split3(c_ref[...])]
        wh, wm, wl = [t.astype(BF16) for t in _split3(w_ref[...])]
        acc = _dot(ch, wh) + _dot(ch, wm) + _dot(cm, wh) + _dot(ch, wl) + _dot(cl, wh) + _dot(cm, wm)
        o_ref[...] = acc + b_ref[...]

    return pl.pallas_call(body, name="ada_fwd", out_shape=jax.ShapeDtypeStruct((N_DEV, ADA_SHARD), F32),
                          compiler_params=_params(0))(c_all, w_shard, b_shard)


def _rope_tables_and_gather(pos_col, inv_freq, tm, weights, c_rows):
    s = pos_col.shape[0]
    nw = len(weights)
    last = s // tm - 1

    def body(*refs):
        p_ref, f_ref = refs[:2]
        w_src, c_src = refs[2:2 + nw], refs[2 + nw:3 + nw]
        cos_ref, sa_ref, sb_ref = refs[3 + nw:6 + nw]
        w_dst, c_dst = refs[6 + nw:6 + 2 * nw], refs[6 + 2 * nw:7 + 2 * nw]
        w_sems, c_sems = refs[7 + 2 * nw:10 + 2 * nw], refs[10 + 2 * nw:]

        @pl.when(pl.program_id(0) == 0)
        def _():
            _gather2_start(w_src, w_dst, *w_sems)
            _exchange_start(_exchange_copies(c_src, c_dst, ["gather"], *c_sems))

        ang = p_ref[...] * f_ref[...]
        sin = jnp.sin(ang)
        first_half = (_lane(CHUNK) & (HEAD_DIM - 1)) < HEAD_DIM // 2
        cos_ref[...] = jnp.cos(ang)
        sa_ref[...] = jnp.where(first_half, -sin, 0.0)
        sb_ref[...] = jnp.where(first_half, 0.0, sin)

        @pl.when(pl.program_id(0) == last)
        def _():
            _exchange_wait(_exchange_copies(c_src, c_dst, ["gather"], *c_sems))
            _gather2_finish(w_src, w_dst, *w_sems)

    tab = jax.ShapeDtypeStruct((s, CHUNK), F32)
    blk = pl.BlockSpec((tm, CHUNK), lambda i: (i, 0))
    gathered = [jax.ShapeDtypeStruct((N_DEV,) + a.shape, a.dtype) for a in list(weights) + [c_rows]]
    outs = pl.pallas_call(
        body, name="rope_tables_gather", grid=(s // tm,), out_shape=[tab, tab, tab] + gathered,
        in_specs=[pl.BlockSpec((tm, 1), lambda i: (i, 0)), pl.BlockSpec((1, CHUNK), lambda i: (0, 0))]
        + [ANY_SPEC] * (nw + 1),
        out_specs=[blk, blk, blk] + [ANY_SPEC] * (nw + 1),
        scratch_shapes=_exchange_sems(nw) + _exchange_sems(1),
        compiler_params=_params(1))(pos_col, inv_freq, *weights, c_rows)
    return outs[:3], outs[3:3 + nw], outs[3 + nw]


def _fwd_proj(x, mod, w_all, cos, sa, sb, bf_row, emat, ind, tm):
    s = x.shape[0]

    def body(x_ref, mod_ref, w_ref, cos_ref, sa_ref, sb_ref, bf_ref, e_ref, ind_ref,
             h_ref, qa_ref, ka_ref, va_ref, za_ref, qb_ref, kb_ref, vb_ref, zb_ref, ga_ref, gb_ref, f_ref,
             bnd_ref, nrm_ref, carry_ref):
        @pl.when(pl.program_id(0) == 0)
        def _():
            carry_ref[...] = jnp.zeros_like(carry_ref)

        xv = x_ref[...]
        rstd = lax.rsqrt(jnp.mean(xv * xv, axis=-1, keepdims=True) + NORM_EPS)
        h = (xv * rstd * mod_ref[0:1, :]) * (1.0 + mod_ref[1:2, :]) + mod_ref[2:3, :]
        hb = h.astype(BF16)
        h_ref[...] = hb.T

        def seg(name):
            off, w = SEG[name]
            return _dot(hb, w_ref[:, off:off + w])

        lane = _lane(CHUNK)
        low = lane < HEAD_DIM
        q_ones = _ones_lanes(64, 67)
        k_ones = _ones_lanes(67, 70)
        cos_t, sa_t, sb_t = cos_ref[...], sa_ref[...], sb_ref[...]

        def write_heads(ref, nat, extra_of, rope, scale):
            for p in range(N_HEADS // 2):
                c = nat[:, CHUNK * p:CHUNK * (p + 1)]
                if rope:
                    c = _rope(c, cos_t, sa_t, sb_t)
                if scale != 1.0:
                    c = c * scale
                for hh in range(2):
                    hd = 2 * p + hh
                    src = c if hh == 0 else pltpu.roll(c, HEAD_DIM, 1)
                    ref[:, CHUNK * hd:CHUNK * (hd + 1)] = jnp.where(low, src, extra_of(hd)).astype(BF16)

        write_heads(qa_ref, seg("qa"), lambda hd: q_ones, True, Q_SCALE)
        ka = seg("ka")
        va = seg("va")
        for kv in range(KV_GROUPS):
            sl = slice(CHUNK * kv, CHUNK * (kv + 1))
            ka_ref[:, sl] = jnp.where(low, _rope(ka[:, sl], cos_t, sa_t, sb_t), k_ones).astype(BF16)
            va_ref[:, sl] = jnp.where(low, va[:, sl], q_ones).astype(BF16)
        za_ref[...] = seg("za")
        zb_ref[...] = seg("zb")
        ga_ref[...] = seg("ga")
        gb_ref[...] = seg("gb")

        xf = seg("f") + bf_ref[...]
        f_ref[...] = xf
        logf = jnp.minimum(xf, 0.0) - jnp.log1p(jnp.exp(-jnp.abs(xf)))
        row = lax.broadcasted_iota(jnp.int32, (tm, tm), 0)
        col = lax.broadcasted_iota(jnp.int32, (tm, tm), 1)
        tri = jnp.where(col <= row, 1.0, 0.0).astype(BF16)
        hi, mid, lo = _split3(logf)
        cum = _dot(tri, hi.astype(BF16)) + _dot(tri, mid.astype(BF16)) + _dot(tri, lo.astype(BF16))
        cum = cum + carry_ref[0:1, :]
        carry_ref[0:1, :] += jnp.sum(logf, axis=0, keepdims=True)
        lane_all = _lane(AUG_W) & (CHUNK - 1)
        bias = _spread3(-cum, e_ref) + jnp.where((lane_all >= 67) & (lane_all < 70), 1.0, 0.0)

        qb, kb = seg("qb"), seg("kb")
        write_heads(qb_ref, qb, lambda hd: q_ones, False, Q_SCALE)
        write_heads(kb_ref, kb, lambda hd: bias[:, CHUNK * hd:CHUNK * (hd + 1)], False, 1.0)
        write_heads(vb_ref, seg("vb"), lambda hd: q_ones, False, 1.0)

        rid = lax.broadcasted_iota(jnp.int32, (tm, CHUNK), 0)
        bnd_ref[...] = jnp.zeros_like(bnd_ref)
        bnd_ref[0, 0:1, :] = jnp.sum(jnp.where(rid == 0, cum, 0.0), axis=0, keepdims=True)
        bnd_ref[0, 1:2, :] = jnp.sum(jnp.where(rid == tm - 1, cum, 0.0), axis=0, keepdims=True)

        @pl.when(pl.program_id(0) == 0)
        def _():
            nrm_ref[...] = jnp.zeros_like(nrm_ref)

        for r_, nat in ((0, qb * Q_SCALE), (1, kb)):
            sq = _dot((nat * nat).astype(BF16), ind_ref[...])
            nrm_ref[r_:r_ + 1, :] = jnp.maximum(nrm_ref[r_:r_ + 1, :], jnp.max(sq, axis=0, keepdims=True))

    row_blk = lambda w: pl.BlockSpec((tm, w), lambda i: (i, 0))
    full = lambda a: pl.BlockSpec(a.shape, lambda i: (0,) * a.ndim)
    sds = lambda w, dt: jax.ShapeDtypeStruct((s, w), dt)
    out_shape = [sds(AUG_W, BF16), sds(KV_GROUPS * CHUNK, BF16), sds(KV_GROUPS * CHUNK, BF16),
                 sds(512, F32), sds(AUG_W, BF16), sds(AUG_W, BF16), sds(AUG_W, BF16), sds(512, F32),
                 sds(D_MODEL, F32), sds(D_MODEL, F32), sds(CHUNK, F32)]
    small = [jax.ShapeDtypeStruct((s // tm, 8, CHUNK), F32), jax.ShapeDtypeStruct((8, CHUNK), F32)]
    return pl.pallas_call(
        body, name="fwd_proj", grid=(s // tm,),
        out_shape=[jax.ShapeDtypeStruct((D_MODEL, s), BF16)] + out_shape + small,
        in_specs=[row_blk(D_MODEL), full(mod), full(w_all), row_blk(CHUNK), row_blk(CHUNK), row_blk(CHUNK),
                  full(bf_row), full(emat), full(ind)],
        out_specs=[pl.BlockSpec((D_MODEL, tm), lambda i: (0, i))] + [row_blk(o.shape[1]) for o in out_shape] + [
            pl.BlockSpec((1, 8, CHUNK), lambda i: (i, 0, 0)), pl.BlockSpec((8, CHUNK), lambda i: (0, 0))],
        scratch_shapes=[pltpu.VMEM((8, CHUNK), F32)],
        compiler_params=_params(1))(x, mod, w_all, cos, sa, sb, bf_row, emat, ind)


def _swa_fwd(q_aug, k_aug, v_aug, sink_rows, tq):
    s = q_aug.shape[0]
    r = tq // WINDOW
    gw = GROUP * CHUNK

    def body(q_ref, kc_ref, kp_ref, vc_ref, vp_ref, sink_ref, o_ref, qb_ref):
        i = pl.program_id(1)
        st = GROUP * WINDOW
        qloc = lax.broadcasted_iota(jnp.int32, (st, 2 * WINDOW), 0) & (WINDOW - 1)
        col = lax.broadcasted_iota(jnp.int32, (st, 2 * WINDOW), 1)
        band = (col > qloc) & (col <= qloc + WINDOW)
        head = jnp.right_shift(lax.broadcasted_iota(jnp.int32, (st, 1), 0), 7)
        sink = jnp.zeros((st, 1), F32)
        for g in range(GROUP):
            sink = jnp.where(head == g, jnp.max(sink_ref[0, g:g + 1, :], axis=1, keepdims=True), sink)
        lane = _lane(CHUNK)
        for sub in range(r):
            rows = slice(WINDOW * sub, WINDOW * (sub + 1))
            q = jnp.concatenate([q_ref[rows, CHUNK * g:CHUNK * (g + 1)] for g in range(GROUP)], axis=0)
            if sub == 0:
                k = jnp.concatenate([kp_ref[...], kc_ref[rows, :]], axis=0)
                v = jnp.concatenate([vp_ref[...], vc_ref[rows, :]], axis=0)
                valid = band & ((col >= WINDOW) | (i > 0))
            else:
                both = slice(WINDOW * (sub - 1), WINDOW * (sub + 1))
                k, v, valid = kc_ref[both, :], vc_ref[both, :], band
            sc = jnp.where(valid, _dot_nt(q, k), NEG)
            m = jnp.maximum(jnp.max(sc, axis=1, keepdims=True), sink)
            acc = _dot(jnp.exp(sc - m).astype(BF16), v)
            denom = _lane_sum(acc, 64, 65) + jnp.exp(sink - m)
            out = acc / denom
            aug = _place3(lane, 67, -(m + jnp.log(denom)), q.astype(F32)).astype(BF16)
            hrows = lambda a, g: a[WINDOW * g:WINDOW * (g + 1), :]
            for g in range(GROUP):
                qb_ref[rows, CHUNK * g:CHUNK * (g + 1)] = hrows(aug, g)
            for pp in range(GROUP // 2):
                o_ref[rows, CHUNK * pp:CHUNK * (pp + 1)] = _pair(hrows(out, 2 * pp), hrows(out, 2 * pp + 1))

    return pl.pallas_call(
        body, name="swa_fwd", grid=(KV_GROUPS, s // tq),
        out_shape=[jax.ShapeDtypeStruct((s, 512), F32), jax.ShapeDtypeStruct((s, AUG_W), BF16)],
        in_specs=[pl.BlockSpec((tq, gw), lambda kv, i: (i, kv)),
                  pl.BlockSpec((tq, CHUNK), lambda kv, i: (i, kv)),
                  pl.BlockSpec((WINDOW, CHUNK), lambda kv, i: (jnp.maximum(i * r - 1, 0), kv)),
                  pl.BlockSpec((tq, CHUNK), lambda kv, i: (i, kv)),
                  pl.BlockSpec((WINDOW, CHUNK), lambda kv, i: (jnp.maximum(i * r - 1, 0), kv)),
                  pl.BlockSpec((1, 8, CHUNK), lambda kv, i: (kv, 0, 0))],
        out_specs=[pl.BlockSpec((tq, GROUP * HEAD_DIM), lambda kv, i: (i, kv)),
                   pl.BlockSpec((tq, gw), lambda kv, i: (i, kv))],
        compiler_params=_params(2))(q_aug, k_aug, k_aug, v_aug, v_aug, sink_rows)


def _swa_bwd(k_aug, v_aug, q_bwd, do_aug, cos, sa, sb, sink_rows, tk):
    s = k_aug.shape[0]
    r = tk // WINDOW
    nt = s // tk
    nb = s // WINDOW
    gw = GROUP * CHUNK

    def body(k_ref, v_ref, q_ref, qn_ref, do_ref, don_ref, cos_ref, sa_ref, sb_ref, sink_ref,
             da_ref, dsink_ref, carry_ref, acc_ref):
        j = pl.program_id(0)

        @pl.when(j == 0)
        def _():
            carry_ref[...] = jnp.zeros_like(carry_ref)
            dsink_ref[...] = jnp.zeros_like(dsink_ref)

        acc_ref[...] = jnp.zeros_like(acc_ref)
        acc_ref[0:WINDOW, :] = carry_ref[...]
        dk_col, dv_col = SEG["ka"][0] - SEG["qa"][0], SEG["va"][0] - SEG["qa"][0]
        st = GROUP * WINDOW
        key = lax.broadcasted_iota(jnp.int32, (WINDOW, 2 * st), 0)
        col = lax.broadcasted_iota(jnp.int32, (WINDOW, 2 * st), 1)
        qloc = col & (WINDOW - 1)
        band = ((col < st) & (key <= qloc)) | ((col >= st) & (key > qloc))
        lane = _lane(CHUNK)
        low = lane < HEAD_DIM
        cos_t, sa_t, sb_t = cos_ref[...], sa_ref[...], sb_ref[...]
        for kv, sub in [(kv, sub) for kv in range(KV_GROUPS) for sub in range(r)]:
            heads = [slice(CHUNK * (GROUP * kv + g), CHUNK * (GROUP * kv + g + 1)) for g in range(GROUP)]
            kvs = slice(CHUNK * kv, CHUNK * (kv + 1))
            rows = slice(WINDOW * sub, WINDOW * (sub + 1))
            nxt = slice(WINDOW * (sub + 1), WINDOW * (sub + 2))
            k, v = k_ref[rows, kvs], v_ref[rows, kvs]
            q_cur, do_cur = [q_ref[rows, cs] for cs in heads], [do_ref[rows, cs] for cs in heads]
            if sub < r - 1:
                q_nxt, do_nxt, valid = [q_ref[nxt, cs] for cs in heads], [do_ref[nxt, cs] for cs in heads], band
            else:
                q_nxt, do_nxt = [qn_ref[:, cs] for cs in heads], [don_ref[:, cs] for cs in heads]
                valid = band & ((col < st) | (j < nt - 1))
            q = jnp.concatenate(q_cur + q_nxt, axis=0)
            do = jnp.concatenate(do_cur + do_nxt, axis=0)
            pt = jnp.exp(jnp.where(valid, _dot_nt(k, q), NEG))
            ds = (pt * _dot_nt(v, do)).astype(BF16)
            dv = _dot(pt.astype(BF16), do)
            dk = _dot(ds, q)
            dq = _dot_tn(ds, k)
            for g, cs in enumerate(heads):
                acc_ref[rows, cs] += dq[WINDOW * g:WINDOW * (g + 1), :]
                dqn = dq[st + WINDOW * g:st + WINDOW * (g + 1), :]
                if sub < r - 1:
                    acc_ref[nxt, cs] += dqn
                else:
                    carry_ref[:, cs] = dqn
                sink = jnp.max(sink_ref[kv, g:g + 1, :], axis=1, keepdims=True)
                p_sink = jnp.exp(sink + _lane_sum(q_cur[g].astype(F32), 67, 70))
                term = jnp.sum(p_sink * _lane_sum(do_cur[g].astype(F32), 64, 67), axis=0, keepdims=True)
                dsink_ref[kv, g:g + 1, :] += jnp.broadcast_to(term, (1, CHUNK))
            da_ref[rows, dk_col + CHUNK * kv:dk_col + CHUNK * (kv + 1)] = _rope_inv(
                jnp.where(low, dk, 0.0), cos_t[rows, :], sa_t[rows, :], sb_t[rows, :]).astype(BF16)
            da_ref[rows, dv_col + CHUNK * kv:dv_col + CHUNK * (kv + 1)] = jnp.where(low, dv, 0.0).astype(BF16)
        for pp in range(N_HEADS // 2):
            d = _pair(acc_ref[:, CHUNK * 2 * pp:CHUNK * (2 * pp + 1)], acc_ref[:, CHUNK * (2 * pp + 1):CHUNK * (2 * pp + 2)])
            da_ref[:, CHUNK * pp:CHUNK * (pp + 1)] = (_rope_inv(d, cos_t, sa_t, sb_t) * Q_SCALE).astype(BF16)

    cur = lambda w: pl.BlockSpec((tk, w), lambda j: (j, 0))
    nxt = pl.BlockSpec((WINDOW, AUG_W), lambda j: (jnp.minimum((j + 1) * r, nb - 1), 0))
    whole = pl.BlockSpec((KV_GROUPS, 8, CHUNK), lambda j: (0, 0, 0))
    return pl.pallas_call(
        body, name="swa_bwd", grid=(nt,),
        out_shape=[jax.ShapeDtypeStruct((s, 1024), BF16), jax.ShapeDtypeStruct((KV_GROUPS, 8, CHUNK), F32)],
        in_specs=[cur(KV_GROUPS * CHUNK), cur(KV_GROUPS * CHUNK), cur(AUG_W), nxt, cur(AUG_W), nxt,
                  cur(CHUNK), cur(CHUNK), cur(CHUNK), whole],
        out_specs=[cur(1024), whole],
        scratch_shapes=[pltpu.VMEM((WINDOW, AUG_W), F32), pltpu.VMEM((tk, AUG_W), F32)],
        compiler_params=_params(1))(k_aug, v_aug, q_bwd, q_bwd, do_aug, do_aug, cos, sa, sb, sink_rows)


def _fox_fwd(first_key, q_aug, k_aug, v_aug, t):
    s = q_aug.shape[0]
    pw = 2 * CHUNK

    def body(lo_ref, q_ref, k_ref, v_ref, o_ref, qb_ref):
        i = pl.program_id(1)
        first = lo_ref[pl.program_id(0), i]
        row = lax.broadcasted_iota(jnp.int32, (t, t), 0)
        col = lax.broadcasted_iota(jnp.int32, (t, t), 1)
        lane = _lane(CHUNK)
        heads = (slice(0, CHUNK), slice(CHUNK, pw))
        qs = [q_ref[:, hs] for hs in heads]

        def step(jb, carry, masked):
            rows = pl.ds(pl.multiple_of(jb * t, t), t)
            scs = [_dot_nt(qs[hh], k_ref[rows, hs]) for hh, hs in enumerate(heads)]
            if masked:
                scs = [jnp.where(col <= row, sc, NEG) for sc in scs]
            m_news = [jnp.maximum(carry[2 * hh], jnp.max(scs[hh], axis=1, keepdims=True)) for hh in range(2)]
            ps = [jnp.exp(scs[hh] - m_news[hh]).astype(BF16) for hh in range(2)]
            new = []
            for hh, hs in enumerate(heads):
                m, acc = carry[2 * hh], carry[2 * hh + 1]
                new += [m_news[hh], jnp.exp(m - m_news[hh]) * acc + _dot(ps[hh], v_ref[rows, hs])]
            return tuple(new)

        init = (jnp.full((t, 1), NEG, F32), jnp.zeros((t, CHUNK), F32)) * 2
        carry = step(i, lax.fori_loop(first, i, lambda jb, c: step(jb, c, False), init), True)
        outs = []
        for hh, hs in enumerate(heads):
            m, acc = carry[2 * hh], carry[2 * hh + 1]
            denom = _lane_sum(acc, 64, 65)
            outs.append(acc / denom)
            lse = m + jnp.log(denom)
            qb_ref[:, hs] = _place3(lane, 67, -lse, qs[hh].astype(F32)).astype(BF16)
        o_ref[...] = _pair(outs[0], outs[1])

    return pl.pallas_call(
        body, name="fox_fwd", grid=(N_HEADS // 2, s // t),
        out_shape=[jax.ShapeDtypeStruct((s, 512), F32), jax.ShapeDtypeStruct((s, AUG_W), BF16)],
        in_specs=[pl.BlockSpec(memory_space=pltpu.SMEM),
                  pl.BlockSpec((t, pw), lambda p, i: (i, p)),
                  pl.BlockSpec((s, pw), lambda p, i: (0, p)),
                  pl.BlockSpec((s, pw), lambda p, i: (0, p))],
        out_specs=[pl.BlockSpec((t, CHUNK), lambda p, i: (i, p)), pl.BlockSpec((t, pw), lambda p, i: (i, p))],
        compiler_params=_params(2))(first_key, q_aug, k_aug, v_aug)


def _fox_block_ranges(bounds, norms, r):
    cum_first = bounds[0::r, 0, :N_HEADS]
    cum_last = bounds[r - 1::r, 1, :N_HEADS]
    n = cum_first.shape[0]
    reach = 2.0 * 1.02 * jnp.sqrt(norms[0, :N_HEADS] * norms[1, :N_HEADS]) + UNDERFLOW
    blk = jnp.arange(n, dtype=jnp.int32)
    decay = cum_last[None, :, :] - cum_first[:, None, :]
    skip = (decay > reach[None, None, :]) & (blk[None, :, None] < blk[:, None, None])
    first = jnp.min(jnp.where(skip, n, blk[None, :, None]), axis=1)
    first = jnp.minimum(first[:, 0::2], first[:, 1::2]).T
    needed = (first[:, :, None] <= blk[None, None, :]) & (blk[None, :, None] >= blk[None, None, :])
    last = jnp.max(jnp.where(needed, blk[None, :, None], blk[None, None, :]), axis=1)
    return first.astype(jnp.int32), last.astype(jnp.int32)


def _fox_bwd(last_query, k_aug, v_aug, q_bwd, do_aug, t):
    s = k_aug.shape[0]
    n = s // t
    pw = 2 * CHUNK

    def body(hi_ref, k_ref, v_ref, q_ref, do_ref, dk_ref, dv_ref, dck_ref, dq_ref, dcq_ref, dq_scr):
        j = pl.program_id(1)
        last = hi_ref[pl.program_id(0), j]

        @pl.when(j == 0)
        def _():
            dq_scr[...] = jnp.zeros_like(dq_scr)

        row = lax.broadcasted_iota(jnp.int32, (t, t), 0)
        col = lax.broadcasted_iota(jnp.int32, (t, t), 1)
        lane = _lane(CHUNK)
        heads = (slice(0, CHUNK), slice(CHUNK, pw))
        ks = [k_ref[:, hs] for hs in heads]
        vs = [v_ref[:, hs] for hs in heads]

        def step(ib, carry, masked):
            rows = pl.ds(pl.multiple_of(ib * t, t), t)
            new = []
            for hh, hs in enumerate(heads):
                dv, dk = carry[2 * hh], carry[2 * hh + 1]
                q, do = q_ref[rows, hs], do_ref[rows, hs]
                st = _dot_nt(ks[hh], q)
                if masked:
                    st = jnp.where(row <= col, st, NEG)
                pt = jnp.exp(st)
                ds = (pt * _dot_nt(vs[hh], do)).astype(BF16)
                new += [dv + _dot(pt.astype(BF16), do), dk + _dot(ds, q)]
                dq_scr[rows, hs] += _dot_tn(ds, ks[hh])
            return tuple(new)

        zero = jnp.zeros((t, CHUNK), F32)
        carry = lax.fori_loop(j + 1, last + 1, lambda ib, c: step(ib, c, False), step(j, (zero,) * 4, True))
        dvs, dks = (carry[0], carry[2]), (carry[1], carry[3])
        dk_ref[...] = _pair(dks[0], dks[1]).astype(BF16)
        dv_ref[...] = _pair(dvs[0], dvs[1]).astype(BF16)
        dck_ref[0] = jnp.where(lane == 0, pltpu.roll(dks[0], 64, 1),
                               jnp.where(lane == 1, pltpu.roll(dks[1], 65, 1), 0.0))

        @pl.when(j == n - 1)
        def _():
            for ib in range(n):
                rows = slice(t * ib, t * (ib + 1))
                d0, d1 = dq_scr[rows, 0:CHUNK], dq_scr[rows, CHUNK:pw]
                dq_ref[rows, :] = (_pair(d0, d1) * Q_SCALE).astype(BF16)
                dcq_ref[0, rows, :] = jnp.where(lane == 0, pltpu.roll(d0, CHUNK - 67, 1),
                                                jnp.where(lane == 1, pltpu.roll(d1, CHUNK - 66, 1), 0.0))

    return pl.pallas_call(
        body, name="fox_bwd", grid=(N_HEADS // 2, n),
        out_shape=[jax.ShapeDtypeStruct((s, 512), BF16), jax.ShapeDtypeStruct((s, 512), BF16),
                   jax.ShapeDtypeStruct((N_HEADS // 2, s, CHUNK), F32), jax.ShapeDtypeStruct((s, 512), BF16),
                   jax.ShapeDtypeStruct((N_HEADS // 2, s, CHUNK), F32)],
        in_specs=[pl.BlockSpec(memory_space=pltpu.SMEM),
                  pl.BlockSpec((t, pw), lambda p, j: (j, p)), pl.BlockSpec((t, pw), lambda p, j: (j, p)),
                  pl.BlockSpec((s, pw), lambda p, j: (0, p)), pl.BlockSpec((s, pw), lambda p, j: (0, p))],
        out_specs=[pl.BlockSpec((t, CHUNK), lambda p, j: (j, p)), pl.BlockSpec((t, CHUNK), lambda p, j: (j, p)),
                   pl.BlockSpec((1, t, CHUNK), lambda p, j: (p, j, 0)), pl.BlockSpec((s, CHUNK), lambda p, j: (0, p)),
                   pl.BlockSpec((1, s, CHUNK), lambda p, j: (p, 0, 0))],
        scratch_shapes=[pltpu.VMEM((s, pw), F32)],
        compiler_params=_params(2))(last_query, k_aug, v_aug, q_bwd, do_aug)


def _fgate_bwd(dcq, dck, xf, tm):
    s = xf.shape[0]
    nt = s // tm

    def body(dq_ref, dc_ref, xf_ref, df_ref, dbf_ref, carry_ref):
        @pl.when(pl.program_id(0) == 0)
        def _():
            carry_ref[...] = jnp.zeros_like(carry_ref)
            dbf_ref[...] = jnp.zeros_like(dbf_ref)

        row = lax.broadcasted_iota(jnp.int32, (tm, tm), 0)
        col = lax.broadcasted_iota(jnp.int32, (tm, tm), 1)
        tri = jnp.where(col >= row, 1.0, 0.0).astype(BF16)
        dcum = dq_ref[0] - dc_ref[0]
        for p in range(1, N_HEADS // 2):
            dcum = dcum + pltpu.roll(dq_ref[p] - dc_ref[p], 2 * p, 1)
        hi, mid, lo = _split3(dcum)
        dlogf = _dot(tri, hi.astype(BF16)) + _dot(tri, mid.astype(BF16)) + _dot(tri, lo.astype(BF16))
        dlogf = dlogf + carry_ref[0:1, :]
        carry_ref[0:1, :] += jnp.sum(dcum, axis=0, keepdims=True)
        df = dlogf * _sigmoid(-xf_ref[...])
        df_ref[...] = df.astype(BF16)
        dbf_ref[0:1, :] += jnp.sum(df, axis=0, keepdims=True)

    rev = pl.BlockSpec((tm, CHUNK), lambda i: (nt - 1 - i, 0))
    rev4 = pl.BlockSpec((N_HEADS // 2, tm, CHUNK), lambda i: (0, nt - 1 - i, 0))
    return pl.pallas_call(
        body, name="fgate_bwd", grid=(nt,),
        out_shape=[jax.ShapeDtypeStruct((s, CHUNK), BF16), jax.ShapeDtypeStruct((8, CHUNK), F32)],
        in_specs=[rev4, rev4, rev], out_specs=[rev, pl.BlockSpec((8, CHUNK), lambda i: (0, 0))],
        scratch_shapes=[pltpu.VMEM((8, CHUNK), F32)],
        compiler_params=_params(1))(dcq, dck, xf)


def _post(x, tgt, att_a, att_b, za, zb, ga, gb, woa_t, wob_t, wout, vec, ind, emat, tm):
    s = x.shape[0]

    def body(x_ref, t_ref, aa_ref, ab_ref, za_ref, zb_ref, ga_ref, gb_ref, woat_ref, wobt_ref, wout_ref,
             vec_ref, ind_ref, e_ref,
             dx1_ref, ua_ref, ub_ref, mg_ref, do_ref, dya_ref, dyb_ref, doa_ref, dob_ref, dzg_ref, acc_ref):
        @pl.when(pl.program_id(0) == 0)
        def _():
            acc_ref[...] = jnp.zeros_like(acc_ref)

        gate, gfin = vec_ref[0:1, :], vec_ref[1:2, :]
        inv_d = 1.0 / D_MODEL

        def branch_fwd(att_ref, z_ref, w_ref, u_ref):
            att, z = att_ref[...], z_ref[...]
            sz = _sigmoid(z)
            silu = z * sz
            u = (att * silu).astype(BF16)
            u_ref[...] = u.T
            return att, z, sz, silu, _dot_nt(u, w_ref[...])

        att_a, z_a, sz_a, silu_a, y_a = branch_fwd(aa_ref, za_ref, woat_ref, ua_ref)
        att_b, z_b, sz_b, silu_b, y_b = branch_fwd(ab_ref, zb_ref, wobt_ref, ub_ref)
        sg_a, sg_b = _sigmoid(ga_ref[...]), _sigmoid(gb_ref[...])
        merged = (sg_a * y_a + sg_b * y_b).astype(BF16)
        mg_ref[...] = merged.T
        o = _dot(merged, wout_ref[...])
        x1 = x_ref[...] + gate * o
        rstd = lax.rsqrt(jnp.mean(x1 * x1, axis=-1, keepdims=True) + NORM_EPS)
        xh = x1 * rstd
        diff = xh * gfin - t_ref[...]
        acc_ref[2:3, :] += (0.5 * inv_d) * jnp.sum(diff * diff, axis=0, keepdims=True)
        dy = diff * inv_d
        acc_ref[1:2, :] += jnp.sum(dy * xh, axis=0, keepdims=True)
        dyg = dy * gfin
        dx1 = rstd * (dyg - xh * jnp.mean(dyg * xh, axis=-1, keepdims=True))
        dx1_ref[...] = dx1
        acc_ref[0:1, :] += jnp.sum(dx1 * o, axis=0, keepdims=True)
        d_o = (dx1 * gate).astype(BF16)
        do_ref[...] = d_o
        dmg = _dot_nt(d_o, wout_ref[...])

        lane = _lane(CHUNK)
        low = lane < HEAD_DIM
        lane_all = _lane(AUG_W) & (CHUNK - 1)

        zg0 = SEG["za"][0]

        def branch_bwd(sg, y, att, z, sz, silu, wt_ref, dy_ref, g_name, z_name, doaug_ref):
            dyb = (dmg * sg).astype(BF16)
            dy_ref[...] = dyb
            g_off, z_off = SEG[g_name][0] - zg0, SEG[z_name][0] - zg0
            dzg_ref[:, g_off:g_off + D_MODEL] = (dmg * y * sg * (1.0 - sg)).astype(BF16)
            du = _dot(dyb, wt_ref[...])
            datt = du * silu
            dzg_ref[:, z_off:z_off + 512] = (du * att * (sz * (1.0 + z * (1.0 - sz)))).astype(BF16)
            extra = _spread3(-_dot_split(datt * att, ind_ref[...]), e_ref)
            for p in range(N_HEADS // 2):
                c = datt[:, CHUNK * p:CHUNK * (p + 1)]
                for hh in range(2):
                    hd = 2 * p + hh
                    src = c if hh == 0 else pltpu.roll(c, HEAD_DIM, 1)
                    doaug_ref[:, CHUNK * hd:CHUNK * (hd + 1)] = jnp.where(
                        low, src, extra[:, CHUNK * hd:CHUNK * (hd + 1)]).astype(BF16)

        branch_bwd(sg_a, y_a, att_a, z_a, sz_a, silu_a, woat_ref, dya_ref, "ga", "za", doa_ref)
        branch_bwd(sg_b, y_b, att_b, z_b, sz_b, silu_b, wobt_ref, dyb_ref, "gb", "zb", dob_ref)

    row_blk = lambda w: pl.BlockSpec((tm, w), lambda i: (i, 0))
    full = lambda a: pl.BlockSpec(a.shape, lambda i: (0,) * a.ndim)
    sds = lambda w, dt: jax.ShapeDtypeStruct((s, w), dt)
    tds = lambda w: jax.ShapeDtypeStruct((w, s), BF16)
    col_blk = lambda w: pl.BlockSpec((w, tm), lambda i: (0, i))
    out_shape = [sds(D_MODEL, F32), tds(512), tds(512), tds(D_MODEL), sds(D_MODEL, BF16),
                 sds(D_MODEL, BF16), sds(D_MODEL, BF16), sds(AUG_W, BF16), sds(AUG_W, BF16), sds(3072, BF16),
                 jax.ShapeDtypeStruct((8, D_MODEL), F32)]
    ins = [x, tgt, att_a, att_b, za, zb, ga, gb, woa_t, wob_t, wout, vec, ind, emat]
    in_specs = [row_blk(a.shape[1]) for a in ins[:8]] + [full(a) for a in ins[8:]]
    out_specs = ([row_blk(D_MODEL), col_blk(512), col_blk(512), col_blk(D_MODEL)]
                 + [row_blk(o.shape[1]) for o in out_shape[4:-1]] + [pl.BlockSpec((8, D_MODEL), lambda i: (0, 0))])
    return pl.pallas_call(body, name="post", grid=(s // tm,), out_shape=out_shape, in_specs=in_specs,
                          out_specs=out_specs, compiler_params=_params(1))(*ins)


def _bwd_pre(dproj, w_all, x, dx1, mod, tm, chip_halves, modes):
    s = x.shape[0]
    ng, nh = len(DPROJ_GROUPS), len(chip_halves)
    last = s // tm - 1

    def body(*refs):
        dp_refs = refs[:ng]
        wt_ref, x_ref, dx1_ref, mod_ref = refs[ng:ng + 4]
        send_refs = refs[ng + 4:ng + 4 + nh]
        gx_ref, acc_ref = refs[ng + 4 + nh:ng + 6 + nh]
        recv_refs = refs[ng + 6 + nh:ng + 6 + 2 * nh]
        sems = refs[ng + 6 + 2 * nh:]

        @pl.when(pl.program_id(0) == 0)
        def _():
            acc_ref[...] = jnp.zeros_like(acc_ref)
            _exchange_start(_exchange_copies(send_refs, recv_refs, modes, *sems))

        dh = None
        for dp_ref, (_, off, w) in zip(dp_refs, DPROJ_GROUPS):
            part = _dot_nt(dp_ref[...], wt_ref[:, off:off + w])
            dh = part if dh is None else dh + part
        xv = x_ref[...]
        rstd = lax.rsqrt(jnp.mean(xv * xv, axis=-1, keepdims=True) + NORM_EPS)
        xh = xv * rstd
        gn = mod_ref[0:1, :]
        acc_ref[0:1, :] += jnp.sum(dh, axis=0, keepdims=True)
        acc_ref[1:2, :] += jnp.sum(dh * (xh * gn), axis=0, keepdims=True)
        dn = dh * (1.0 + mod_ref[1:2, :])
        acc_ref[2:3, :] += jnp.sum(dn * xh, axis=0, keepdims=True)
        dxh = dn * gn
        gx_ref[...] = dx1_ref[...] + rstd * (dxh - xh * jnp.mean(dxh * xh, axis=-1, keepdims=True))

        @pl.when(pl.program_id(0) == last)
        def _():
            _exchange_wait(_exchange_copies(send_refs, recv_refs, modes, *sems))

    row_blk = lambda w: pl.BlockSpec((tm, w), lambda i: (i, 0))
    full = lambda a: pl.BlockSpec(a.shape, lambda i: (0,) * a.ndim)
    outs = pl.pallas_call(
        body, name="bwd_pre", grid=(s // tm,),
        out_shape=[jax.ShapeDtypeStruct((s, D_MODEL), F32), jax.ShapeDtypeStruct((8, D_MODEL), F32)]
        + _exchange_out_shapes(chip_halves, modes),
        in_specs=[row_blk(w) for _, _, w in DPROJ_GROUPS] + [full(w_all), row_blk(D_MODEL), row_blk(D_MODEL), full(mod)]
        + [ANY_SPEC] * nh,
        out_specs=[row_blk(D_MODEL), pl.BlockSpec((8, D_MODEL), lambda i: (0, 0))] + [ANY_SPEC] * nh,
        scratch_shapes=_exchange_sems(nh),
        compiler_params=_params(1))(*dproj, w_all, x, dx1, mod, *chip_halves)
    return outs[0], outs[1], outs[2:]


def _token_matmul_multi(a_t, bs, tk, name):
    m, s = a_t.shape
    nb = len(bs)

    def body(*refs):
        a_ref, b_refs, o_refs = refs[0], refs[1:1 + nb], refs[1 + nb:]
        a = a_ref[...]
        for b_ref, o_ref in zip(b_refs, o_refs):
            @pl.when(pl.program_id(0) == 0)
            def _():
                o_ref[...] = jnp.zeros_like(o_ref)

            o_ref[...] += _dot(a, b_ref[...])

    return pl.pallas_call(
        body, name=name, grid=(s // tk,),
        out_shape=[jax.ShapeDtypeStruct((m, b.shape[1]), F32) for b in bs],
        in_specs=[pl.BlockSpec((m, tk), lambda k: (0, k))] + [pl.BlockSpec((tk, b.shape[1]), lambda k: (k, 0)) for b in bs],
        out_specs=[pl.BlockSpec((m, b.shape[1]), lambda k: (0, 0)) for b in bs],
        compiler_params=_params(1))(a_t, *bs)


def _adam_math(g, w, m, v):
    m2 = ADAM_B1 * m + (1.0 - ADAM_B1) * g
    v2 = ADAM_B2 * v + (1.0 - ADAM_B2) * (g * g)
    delta = -ADAM_LR * ((m2 / ADAM_C1) / (jnp.sqrt(v2 / ADAM_C2) + ADAM_EPS) + ADAM_WD * w)
    return delta, m2, v2


def _adamw_parts(full, parts, others, w, m, v, tr, name):
    _, rws, cols = w.shape

    def body(oth_ref, f_ref, p_ref, w_ref, m_ref, v_ref, g_ref, d_ref, m2_ref, v2_ref):
        g = None
        for chip in range(N_CHIPS):
            part = (1.0 - oth_ref[chip]) * f_ref[chip] + oth_ref[chip] * p_ref[chip].astype(F32)
            g = part if g is None else g + part
        g_ref[0] = g
        d_ref[0], m2_ref[0], v2_ref[0] = _adam_math(g, w_ref[0], m_ref[0], v_ref[0])

    blk = pl.BlockSpec((1, tr, cols), lambda i: (0, i, 0))
    slots = pl.BlockSpec((N_CHIPS, tr, cols), lambda i: (0, i, 0))
    o = jax.ShapeDtypeStruct((1, rws, cols), F32)
    return pl.pallas_call(
        body, name=name, grid=(rws // tr,), out_shape=[o, o, o, o],
        in_specs=[pl.BlockSpec(memory_space=pltpu.SMEM), slots, slots, blk, blk, blk],
        out_specs=[blk, blk, blk, blk], compiler_params=_params(1))(others, full, parts, w, m, v)


def _adamw_devices(parts, w, m, v, tr, name):
    _, rws, cols = w.shape

    def body(p_ref, w_ref, m_ref, v_ref, g_ref, d_ref, m2_ref, v2_ref):
        g = p_ref[0].astype(F32)
        for dev in range(1, N_DEV):
            g = g + p_ref[dev].astype(F32)
        g_ref[0] = g
        d_ref[0], m2_ref[0], v2_ref[0] = _adam_math(g, w_ref[0], m_ref[0], v_ref[0])

    blk = pl.BlockSpec((1, tr, cols), lambda i: (0, i, 0))
    o = jax.ShapeDtypeStruct((1, rws, cols), F32)
    return pl.pallas_call(
        body, name=name, grid=(rws // tr,), out_shape=[o, o, o, o],
        in_specs=[pl.BlockSpec((N_DEV, tr, cols), lambda i: (0, i, 0)), blk, blk, blk],
        out_specs=[blk, blk, blk, blk], compiler_params=_params(1))(parts, w, m, v)


def _adamw_ada(c_lanes, d_rows, w, m, v, tr):
    _, rws, cols = w.shape

    def body(c_ref, d_ref, w_ref, m_ref, v_ref, g_ref, dl_ref, m2_ref, v2_ref):
        for k in range(cols // CHUNK):
            cs = slice(CHUNK * k, CHUNK * (k + 1))
            g = c_ref[0] * d_ref[0:1, cs]
            for b in range(1, N_DEV):
                g = g + c_ref[b] * d_ref[b:b + 1, cs]
            g_ref[0, :, cs] = g
            dl_ref[0, :, cs], m2_ref[0, :, cs], v2_ref[0, :, cs] = _adam_math(
                g, w_ref[0, :, cs], m_ref[0, :, cs], v_ref[0, :, cs])

    blk = pl.BlockSpec((1, tr, cols), lambda i: (0, i, 0))
    o = jax.ShapeDtypeStruct((1, rws, cols), F32)
    return pl.pallas_call(
        body, name="adamw_ada", grid=(rws // tr,), out_shape=[o, o, o, o],
        in_specs=[pl.BlockSpec((N_DEV, tr, CHUNK), lambda i: (0, i, 0)), pl.BlockSpec((N_DEV, cols), lambda i: (0, 0)),
                  blk, blk, blk],
        out_specs=[blk, blk, blk, blk], compiler_params=_params(1))(c_lanes, d_rows, w, m, v)


def _adamw_small(packs, w, m, v):
    def body(p_ref, w_ref, m_ref, v_ref, g_ref, d_ref, m2_ref, v2_ref, loss_ref):
        g = p_ref[0]
        for dev in range(1, N_DEV):
            g = g + p_ref[dev]
        g_ref[...] = g
        d_ref[...], m2_ref[...], v2_ref[...] = _adam_math(g, w_ref[...], m_ref[...], v_ref[...])
        loss_ref[...] = jnp.broadcast_to(jnp.sum(g_ref[2:3, 0:D_MODEL], axis=1, keepdims=True), loss_ref.shape)

    o = jax.ShapeDtypeStruct(w.shape, F32)
    return pl.pallas_call(body, name="adamw_small", out_shape=[o, o, o, o, jax.ShapeDtypeStruct((8, CHUNK), F32)],
                          compiler_params=_params(0))(packs, w, m, v)


def _tile(s, want):
    return min(s, want)


def _layout_pieces():
    orig = {"qa": 0, "ka": 512, "va": 640, "za": 768, "qb": 1280, "kb": 1792, "vb": 2304, "f": 2816, "zb": 2824,
            "ga": 3336, "gb": 4360}
    pieces = []
    for name, (off, w) in SEG.items():
        if name in ("ka", "va"):
            pieces += [(orig[name] + HEAD_DIM * kv, orig[name] + HEAD_DIM * (kv + 1), off + CHUNK * kv) for kv in range(KV_GROUPS)]
        else:
            pieces.append((orig[name], orig[name] + (N_HEADS if name == "f" else w), off))
    return pieces


def _assemble_w_all(win_g):
    cols, pos = [], 0
    for lo, hi, new in sorted(_layout_pieces(), key=lambda t: t[2]):
        if new > pos:
            cols.append(jnp.zeros((D_MODEL, new - pos), win_g.dtype))
        col = lo
        while col < hi:
            dev = col // IN_SHARD
            end = min(hi, (dev + 1) * IN_SHARD)
            cols.append(win_g[dev, :, col - dev * IN_SHARD:end - dev * IN_SHARD])
            col = end
        pos = new + hi - lo
    cols.append(jnp.zeros((D_MODEL, PROJ_W - pos), win_g.dtype))
    return jnp.concatenate(cols, axis=1)


def _grad_slot(dw_groups, dev):
    lo_d, hi_d = dev * IN_SHARD, (dev + 1) * IN_SHARD
    cols = []
    for lo, hi, new in sorted(_layout_pieces()):
        a, b = max(lo, lo_d), min(hi, hi_d)
        if a < b:
            arr, off = next((g, o) for g, (_, o, w) in zip(dw_groups, DPROJ_GROUPS) if o <= new < o + w)
            cols.append(arr[:, new - off + a - lo:new - off + b - lo])
    return jnp.concatenate(cols, axis=1)


def kernel(x, c, positions, w_ada, b_ada, g_norm, w_in, b_f, sinks, w_o_swa, w_o_fox, w_out, g_final, loss_target, m_w_ada, m_b_ada, m_g_norm, m_w_in, m_b_f, m_sinks, m_w_o_swa, m_w_o_fox, m_w_out, m_g_final, v_w_ada, v_b_ada, v_g_norm, v_w_in, v_b_f, v_sinks, v_w_o_swa, v_w_o_fox, v_w_out, v_g_final):
    s = x.shape[1]
    tm = _tile(s, 256)
    ta = _tile(s, 512)
    me = 4 * lax.axis_index("x") + 2 * lax.axis_index("y") + lax.axis_index("c")
    x2, tgt = x[0], loss_target[0]

    inv_freq = np.power(np.float32(ROPE_THETA), -np.arange(0, HEAD_DIM, 2, dtype=np.float32) / HEAD_DIM)
    inv_freq = jnp.asarray(np.tile(inv_freq, CHUNK // (HEAD_DIM // 2))[None, :], F32)
    (cos, sa, sb), (win_g, woa_g, wob_g, wout_g), c_all = _rope_tables_and_gather(
        positions.reshape(s, 1).astype(F32), inv_freq, tm,
        [w_in[0].astype(BF16), w_o_swa[0].T.astype(BF16), w_o_fox[0].T.astype(BF16), w_out[0].astype(BF16)],
        jnp.broadcast_to(c, (8, D_MODEL)))
    c_all = c_all[:, 0, :]
    b_shard = lax.dynamic_slice(b_ada, (0, me * ADA_SHARD), (1, ADA_SHARD))
    ada_part = _ada_fwd(c_all, w_ada[0], b_shard)
    (ada_g,) = _exchange([ada_part], ["gather"], "gather_ada")
    ada = lax.dynamic_index_in_dim(ada_g, me, axis=1, keepdims=False).reshape(3 * D_MODEL)
    shift, scale, gate = ada[:D_MODEL], ada[D_MODEL:2 * D_MODEL], ada[2 * D_MODEL:]
    woa_t = woa_g.reshape(D_MODEL, 512)
    wob_t = wob_g.reshape(D_MODEL, 512)
    wout = wout_g.reshape(D_MODEL, D_MODEL)
    w_all = _assemble_w_all(win_g)

    zrow = jnp.zeros((1, D_MODEL), F32)
    mod = jnp.concatenate([g_norm, scale[None], shift[None], zrow, zrow, zrow, zrow, zrow], axis=0)
    bf_row = jnp.pad(b_f, ((0, 0), (0, CHUNK - N_HEADS)))
    emat_np = np.zeros((3 * CHUNK, AUG_W), np.float32)
    ind_np = np.zeros((512, CHUNK), np.float32)
    for hd in range(N_HEADS):
        for part in range(3):
            emat_np[CHUNK * part + hd, CHUNK * hd + 64 + part] = 1.0
        ind_np[HEAD_DIM * hd:HEAD_DIM * (hd + 1), hd] = 1.0
    emat, ind = jnp.asarray(emat_np, BF16), jnp.asarray(ind_np, BF16)
    sink_rows = jnp.broadcast_to(jnp.pad(sinks.reshape(KV_GROUPS, GROUP), ((0, 0), (0, 8 - GROUP)))[:, :, None],
                                 (KV_GROUPS, 8, CHUNK))

    (h_t, qa, ka, va, za, qb, kb, vb, zb, ga, gb, xf, bounds, norms) = _fwd_proj(
        x2, mod, w_all, cos, sa, sb, bf_row, emat, ind, tm)
    first_key, last_query = _fox_block_ranges(bounds, norms, ta // tm)
    att_a, qa_bwd = _swa_fwd(qa, ka, va, sink_rows, ta)
    att_b, qb_bwd = _fox_fwd(first_key, qb, kb, vb, ta)
    vec = jnp.concatenate([gate[None], g_final[None], zrow, zrow, zrow, zrow, zrow, zrow], axis=0)
    (dx1, ua_t, ub_t, merged_t, d_o, dya, dyb, doa, dob, dzg, post_acc) = _post(
        x2, tgt, att_a, att_b, za, zb, ga, gb, woa_t, wob_t, wout, vec, ind, emat, tm)

    da, dsink = _swa_bwd(ka, va, qa_bwd, doa, cos, sa, sb, sink_rows, ta)
    dkb, dvb, dck, dqb, dcq = _fox_bwd(last_query, kb, vb, qb_bwd, dob, ta)
    dfb, dbf_acc = _fgate_bwd(dcq, dck, xf, ta)
    dproj = [da, dqb, dkb, dvb, dzg, dfb]
    dw_zg, = _token_matmul_multi(h_t, [dzg], ta, "dw_in_zg")
    dw_a, dw_qb, dw_kb, dw_vb, dw_f = _token_matmul_multi(h_t, [da, dqb, dkb, dvb, dfb], ta, "dw_in_rest")
    dw_all = [dw_a, dw_qb, dw_kb, dw_vb, dw_zg, dw_f]

    core, chip = lax.axis_index("c"), 2 * lax.axis_index("x") + lax.axis_index("y")
    win_slots = jnp.stack([jnp.stack([_grad_slot(dw_all, 2 * ch + co) for ch in range(N_CHIPS)]) for co in range(2)])
    to_sibling = lax.dynamic_index_in_dim(win_slots, 1 - core, 0, keepdims=False).astype(BF16)
    (dwoa, dwob, dwout), (win_theirs,) = _token_matmuls_and_swap(
        [(ua_t, dya), (ub_t, dyb), (merged_t, d_o)], ta, [to_sibling])
    win_full, win_half = _chip_partial(jnp.reshape(core, (1,)).astype(jnp.int32), win_slots, win_theirs, 256,
                                       "chip_partial_w_in")
    col_slots = lambda g: g.reshape(512, N_DEV, 128).transpose(1, 0, 2).astype(BF16)
    grad_x, pre_acc, (p_win, p_woa, p_wob, p_wout) = _bwd_pre(
        dproj, w_all, x2, dx1, mod, tm,
        [win_half, col_slots(dwoa), col_slots(dwob), dwout.reshape(N_DEV, 128, D_MODEL).astype(BF16)],
        ["chips", "devices", "devices", "devices"])
    misc = jnp.concatenate([dbf_acc[0, :N_HEADS], dsink[:, :GROUP, 0].reshape(N_HEADS), jnp.zeros((D_MODEL - 16,), F32)])
    zpad = jnp.zeros((2 * D_MODEL,), F32)
    pack = jnp.stack([jnp.concatenate([pre_acc[0], pre_acc[1], post_acc[0]]),
                      jnp.concatenate([pre_acc[2], post_acc[1], misc]),
                      jnp.concatenate([post_acc[2], zpad])] + [jnp.zeros((3 * D_MODEL,), F32)] * 5)
    (packs,) = _exchange([pack], ["gather"], "gather_small_grads")
    others = jnp.where(jnp.arange(N_CHIPS) == chip, 0.0, 1.0).astype(F32)

    g_win, d_win, m_win, v_win = _adamw_parts(win_full, p_win, others, w_in, m_w_in, v_w_in, 128, "adamw_w_in")
    g_woa, d_woa, m_woa, v_woa = _adamw_devices(p_woa, w_o_swa, m_w_o_swa, v_w_o_swa, 128, "adamw_w_o_swa")
    g_wob, d_wob, m_wob, v_wob = _adamw_devices(p_wob, w_o_fox, m_w_o_fox, v_w_o_fox, 128, "adamw_w_o_fox")
    g_wout, d_wout, m_wout, v_wout = _adamw_devices(p_wout, w_out, m_w_out, v_w_out, 128, "adamw_w_out")
    d_ada_rows = lax.dynamic_slice(packs[:, 0, :], (0, me * ADA_SHARD), (N_DEV, ADA_SHARD))
    c_lanes = jnp.broadcast_to(c_all[:, :, None], (N_DEV, D_MODEL, CHUNK))
    g_wada, d_wada, m_wada, v_wada = _adamw_ada(c_lanes, d_ada_rows, w_ada, m_w_ada, v_w_ada, 256)

    def small_pack(bada, gn, gf, bfv, sk):
        misc_w = jnp.concatenate([bfv[0], sk[0], jnp.zeros((D_MODEL - 16,), F32)])
        return jnp.stack([bada[0], jnp.concatenate([gn[0], gf, misc_w])] + [jnp.zeros((3 * D_MODEL,), F32)] * 6)

    sm = _adamw_small(packs, small_pack(b_ada, g_norm, g_final, b_f, sinks),
                      small_pack(m_b_ada, m_g_norm, m_g_final, m_b_f, m_sinks),
                      small_pack(v_b_ada, v_g_norm, v_g_final, v_b_f, v_sinks))
    loss = sm[4][0, 0]

    def unpack(p):
        return (p[0][None], p[1, :D_MODEL][None], p[1, 2 * D_MODEL:2 * D_MODEL + 8][None],
                p[1, 2 * D_MODEL + 8:2 * D_MODEL + 16][None], p[1, D_MODEL:2 * D_MODEL])

    outs = []
    for big, small in (((g_wada, g_win, g_woa, g_wob, g_wout), sm[0]), ((d_wada, d_win, d_woa, d_wob, d_wout), sm[1]),
                       ((m_wada, m_win, m_woa, m_wob, m_wout), sm[2]), ((v_wada, v_win, v_woa, v_wob, v_wout), sm[3])):
        wada_o, win_o, woa_o, wob_o, wout_o = big
        bada_o, gn_o, bf_o, sk_o, gf_o = unpack(small)
        outs += [wada_o, bada_o, gn_o, win_o, bf_o, sk_o, woa_o, wob_o, wout_o, gf_o]
    return (loss, grad_x[None], *outs)
```

```python
import numpy as np
import jax
import jax.numpy as jnp
from jax import lax
from jax.experimental import pallas as pl
from jax.experimental.pallas import tpu as pltpu

F32 = jnp.float32
BF16 = jnp.bfloat16

D_MODEL = 1024
HEAD_DIM = 64
N_HEADS = 8
KV_GROUPS = 2
GROUP = N_HEADS // KV_GROUPS
WINDOW = 128
CHUNK = 128
AUG_W = N_HEADS * CHUNK
N_DEV = 8
IN_SHARD = 673
ADA_SHARD = 384
NORM_EPS = 1e-6
ROPE_THETA = 10000.0
Q_SCALE = HEAD_DIM ** -0.5
NEG = -1e30
UNDERFLOW = 105.0

ADAM_LR = 0.001
ADAM_B1 = 0.9
ADAM_B2 = 0.999
ADAM_EPS = 1e-08
ADAM_WD = 0.01
ADAM_STEP = 10
ADAM_C1 = 1.0 - ADAM_B1 ** ADAM_STEP
ADAM_C2 = 1.0 - ADAM_B2 ** ADAM_STEP

SEG = {}
_off = 0
for _name, _w in (("qa", 512), ("ka", 256), ("va", 256), ("qb", 512), ("kb", 512), ("vb", 512),
                  ("za", 512), ("zb", 512), ("ga", 1024), ("gb", 1024), ("f", 128)):
    SEG[_name] = (_off, _w)
    _off += _w
PROJ_W = _off
DPROJ_GROUPS = (("a", SEG["qa"][0], 1024), ("qb", SEG["qb"][0], 512), ("kb", SEG["kb"][0], 512),
                ("vb", SEG["vb"][0], 512), ("zg", SEG["za"][0], 3072), ("f", SEG["f"][0], 128))

VMEM_LIMIT = 56 * 1024 * 1024


def _params(n_axes):
    return pltpu.CompilerParams(dimension_semantics=("arbitrary",) * n_axes, vmem_limit_bytes=VMEM_LIMIT)


def _dot(a, b):
    return jnp.dot(a, b, preferred_element_type=F32)


def _dot_nt(a, b):
    return lax.dot_general(a, b, (((1,), (1,)), ((), ())), preferred_element_type=F32)


def _dot_tn(a, b):
    return lax.dot_general(a, b, (((0,), (0,)), ((), ())), preferred_element_type=F32)


def _lane(n):
    return lax.broadcasted_iota(jnp.int32, (1, n), 1)


def _split3(x):
    hi = x.astype(BF16).astype(F32)
    r = x - hi
    mid = r.astype(BF16).astype(F32)
    lo = (r - mid).astype(BF16).astype(F32)
    return hi, mid, lo


def _dot_split(x, b):
    hi, mid, lo = _split3(x)
    return _dot(hi.astype(BF16), b) + _dot(mid.astype(BF16), b) + _dot(lo.astype(BF16), b)


def _spread3(x, e3_ref):
    return _dot(jnp.concatenate(_split3(x), axis=1).astype(BF16), e3_ref[...])


def _place3(lane, base, x, other):
    hi, mid, lo = _split3(x)
    return jnp.where(lane == base, hi, jnp.where(lane == base + 1, mid, jnp.where(lane == base + 2, lo, other)))


def _lane_sum(x, lo, hi):
    lane = _lane(x.shape[1])
    return jnp.sum(jnp.where((lane >= lo) & (lane < hi), x, 0.0), axis=1, keepdims=True)


def _ones_lanes(lo, hi):
    lane = _lane(CHUNK)
    return jnp.where((lane >= lo) & (lane < hi), 1.0, 0.0).astype(F32)


def _rope(c, cos, sa, sb):
    return c * cos + pltpu.roll(c, CHUNK - 32, 1) * sa + pltpu.roll(c, 32, 1) * sb


def _rope_inv(d, cos, sa, sb):
    return d * cos - (pltpu.roll(d, CHUNK - 32, 1) * sa + pltpu.roll(d, 32, 1) * sb)


def _pair(c0, c1):
    return jnp.where(_lane(CHUNK) < HEAD_DIM, c0, pltpu.roll(c1, HEAD_DIM, 1))


def _sigmoid(x):
    return 0.5 * jnp.tanh(0.5 * x) + 0.5


N_CHIPS = 4
ANY_SPEC = pl.BlockSpec(memory_space=pl.ANY)


def _exchange(arrays, modes, name):
    n = len(arrays)

    def body(*refs):
        copies = _exchange_copies(refs[:n], refs[n:2 * n], modes, *refs[2 * n:])
        _exchange_start(copies)
        _exchange_wait(copies)

    return pl.pallas_call(
        body, name=name, out_shape=_exchange_out_shapes(arrays, modes),
        in_specs=[ANY_SPEC] * n, out_specs=[ANY_SPEC] * n, scratch_shapes=_exchange_sems(n),
    )(*arrays)


def _exchange_out_shapes(arrays, modes):
    return [jax.ShapeDtypeStruct((N_DEV,) + a.shape if md == "gather" else a.shape, a.dtype)
            for a, md in zip(arrays, modes)]


def _exchange_sems(n):
    return [pltpu.SemaphoreType.DMA((n, N_DEV - 1)), pltpu.SemaphoreType.DMA((n, N_DEV - 1)),
            pltpu.SemaphoreType.DMA((n,))]


def _exchange_copies(srcs, dsts, modes, send_sems, recv_sems, loc_sems):
    n = len(srcs)
    x, y, c = lax.axis_index("x"), lax.axis_index("y"), lax.axis_index("c")

    def slot(i, px, py, pc):
        return 2 * px + py if modes[i] == "chips" else 4 * px + 2 * py + pc

    def src_of(i, px, py, pc):
        return srcs[i] if modes[i] == "gather" else srcs[i].at[slot(i, px, py, pc)]

    local = [pltpu.make_async_copy(src_of(i, x, y, c), dsts[i].at[slot(i, x, y, c)], loc_sems.at[i])
             for i in range(n)]
    sends, recvs = [], []
    for r in (1, 2, 4, 3, 5, 6, 7):
        px = 1 - x if r & 4 else x
        py = 1 - y if r & 2 else y
        pc = 1 - c if r & 1 else c
        for i in range(n):
            if modes[i] == "chips" and r & 1:
                continue
            sems = dict(send_sem=send_sems.at[i, r - 1], recv_sem=recv_sems.at[i, r - 1],
                        device_id=(px, py, pc), device_id_type=pl.DeviceIdType.MESH)
            sends.append(pltpu.make_async_remote_copy(
                src_ref=src_of(i, px, py, pc), dst_ref=dsts[i].at[slot(i, x, y, c)], **sems))
            recvs.append(pltpu.make_async_remote_copy(
                src_ref=src_of(i, px, py, pc), dst_ref=dsts[i].at[slot(i, px, py, pc)], **sems))
    return local, sends, recvs


def _exchange_start(copies):
    local, sends, _ = copies
    for cp in local + sends:
        cp.start()


def _exchange_wait(copies):
    local, sends, recvs = copies
    for cp in recvs:
        cp.wait_recv()
    for cp in sends:
        cp.wait_send()
    for cp in local:
        cp.wait()


def _gather2_parts(srcs, dsts, send_sems, recv_sems, loc_sems):
    n = len(srcs)
    x, y, c = lax.axis_index("x"), lax.axis_index("y"), lax.axis_index("c")
    me, sibling = (x, y, c), (x, y, 1 - c)
    chips = [(1 - x, y), (x, 1 - y), (1 - x, 1 - y)]

    def rows(i, dev):
        return dsts[i].at[4 * dev[0] + 2 * dev[1] + dev[2]]

    def copy(i, k, block, to, src=None):
        return pltpu.make_async_remote_copy(
            src_ref=rows(i, block) if src is None else src, dst_ref=rows(i, block),
            send_sem=send_sems.at[i, k], recv_sem=recv_sems.at[i, k],
            device_id=to, device_id_type=pl.DeviceIdType.MESH)

    local = [pltpu.make_async_copy(srcs[i], rows(i, me), loc_sems.at[i]) for i in range(n)]
    first = []
    for i in range(n):
        first.append(copy(i, 0, me, sibling, src=srcs[i]))
        first += [copy(i, 1 + j, me, (*chip, c), src=srcs[i]) for j, chip in enumerate(chips)]
    return n, c, me, sibling, chips, copy, local, first


def _gather2_start(srcs, dsts, send_sems, recv_sems, loc_sems):
    *_, local, first = _gather2_parts(srcs, dsts, send_sems, recv_sems, loc_sems)
    for cp in local + first:
        cp.start()


def _gather2_finish(srcs, dsts, send_sems, recv_sems, loc_sems):
    n, c, me, sibling, chips, copy, local, first = _gather2_parts(srcs, dsts, send_sems, recv_sems, loc_sems)
    passed = []
    for j, chip in enumerate(chips):
        for i in range(n):
            copy(i, 1 + j, (*chip, c), me).wait_recv()
            cp = copy(i, 4 + j, (*chip, c), sibling)
            cp.start()
            passed.append(cp)
    for i in range(n):
        copy(i, 0, sibling, me).wait_recv()
        for j, chip in enumerate(chips):
            copy(i, 4 + j, (*chip, 1 - c), me).wait_recv()
    for cp in first + passed:
        cp.wait_send()
    for cp in local:
        cp.wait()


def _sibling_copies(srcs, dsts, send_sems, recv_sems):
    x, y, c = lax.axis_index("x"), lax.axis_index("y"), lax.axis_index("c")
    return [pltpu.make_async_remote_copy(
        src_ref=srcs[i].at[1 - c, k], dst_ref=dsts[i].at[k], send_sem=send_sems.at[i, k], recv_sem=recv_sems.at[i, k],
        device_id=(x, y, 1 - c), device_id_type=pl.DeviceIdType.MESH)
        for i in range(len(srcs)) for k in range(N_CHIPS)]


def _token_matmuls_and_swap(pairs, tk, swap):
    npair, nsw = len(pairs), len(swap)
    s = pairs[0][0].shape[1]
    last = s // tk - 1

    def body(*refs):
        a_refs, b_refs = refs[0:2 * npair:2], refs[1:2 * npair:2]
        src = refs[2 * npair:2 * npair + nsw]
        o_refs = refs[2 * npair + nsw:3 * npair + nsw]
        dst = refs[3 * npair + nsw:3 * npair + 2 * nsw]
        sems = refs[3 * npair + 2 * nsw:3 * npair + 2 * nsw + 2]
        acc_refs = refs[3 * npair + 2 * nsw + 2:]

        @pl.when(pl.program_id(0) == 0)
        def _():
            for cp in _sibling_copies(src, dst, *sems):
                cp.start()

        for a_ref, b_ref, o_ref, acc_ref in zip(a_refs, b_refs, o_refs, acc_refs):
            _accumulate_tokens(a_ref[...], b_ref, acc_ref, o_ref, last)

        @pl.when(pl.program_id(0) == last)
        def _():
            for cp in _sibling_copies(src, dst, *sems):
                cp.wait()

    in_specs, args = [], []
    for a_t, b in pairs:
        in_specs += [pl.BlockSpec((a_t.shape[0], tk), lambda k: (0, k)), pl.BlockSpec((tk, b.shape[1]), lambda k: (k, 0))]
        args += [a_t, b]
    outs = pl.pallas_call(
        body, name="dw_out_projs_swap", grid=(s // tk,),
        out_shape=[jax.ShapeDtypeStruct((a_t.shape[0], b.shape[1]), BF16) for a_t, b in pairs]
        + [jax.ShapeDtypeStruct(a.shape[1:], a.dtype) for a in swap],
        in_specs=in_specs + [ANY_SPEC] * nsw,
        out_specs=[pl.BlockSpec((a_t.shape[0], b.shape[1]), lambda k: (0, 0)) for a_t, b in pairs] + [ANY_SPEC] * nsw,
        scratch_shapes=[pltpu.SemaphoreType.DMA((nsw, N_CHIPS)), pltpu.SemaphoreType.DMA((nsw, N_CHIPS))]
        + [pltpu.VMEM((a_t.shape[0], b.shape[1]), F32) for a_t, b in pairs],
        compiler_params=_params(1))(*args, *swap)
    return outs[:npair], outs[npair:]


def _chip_partial(core, slots, theirs, tr, name):
    _, k, rws, cols = slots.shape

    def body(core_ref, a_ref, b_ref, f_ref, h_ref):
        sm = a_ref[0].astype(F32) + b_ref[...].astype(F32)
        f_ref[...] = sm
        h_ref[...] = sm.astype(BF16)

    blk = pl.BlockSpec((1, tr, cols), lambda j, i, core_ref: (j, i, 0))
    mine = pl.BlockSpec((1, 1, tr, cols), lambda j, i, core_ref: (core_ref[0], j, i, 0))
    return pl.pallas_call(
        body, name=name,
        grid_spec=pltpu.PrefetchScalarGridSpec(num_scalar_prefetch=1, grid=(k, rws // tr),
                                               in_specs=[mine, blk], out_specs=[blk, blk]),
        out_shape=[jax.ShapeDtypeStruct(theirs.shape, F32), jax.ShapeDtypeStruct(theirs.shape, BF16)],
        compiler_params=_params(2))(core, slots, theirs)


def _ada_fwd(c_all, w_shard, b_shard):
    def body(c_ref, w_ref, b_ref, o_ref):
        ch, cm, cl = [t.astype(BF16) for t in _split3(c_ref[...])]
        wh, wm, wl = [t.astype(BF16) for t in _split3(w_ref[...])]
        acc = _dot(ch, wh) + _dot(ch, wm) + _dot(cm, wh) + _dot(ch, wl) + _dot(cl, wh) + _dot(cm, wm)
        o_ref[...] = acc + b_ref[...]

    return pl.pallas_call(body, name="ada_fwd", out_shape=jax.ShapeDtypeStruct((N_DEV, ADA_SHARD), F32),
                          compiler_params=_params(0))(c_all, w_shard, b_shard)


def _rope_tables_and_gather(pos_col, inv_freq, tm, weights, c_rows):
    s = pos_col.shape[0]
    nw = len(weights)
    last = s // tm - 1

    def body(*refs):
        p_ref, f_ref = refs[:2]
        w_src, c_src = refs[2:2 + nw], refs[2 + nw:3 + nw]
        cos_ref, sa_ref, sb_ref = refs[3 + nw:6 + nw]
        w_dst, c_dst = refs[6 + nw:6 + 2 * nw], refs[6 + 2 * nw:7 + 2 * nw]
        w_sems, c_sems = refs[7 + 2 * nw:10 + 2 * nw], refs[10 + 2 * nw:]

        @pl.when(pl.program_id(0) == 0)
        def _():
            _gather2_start(w_src, w_dst, *w_sems)
            _exchange_start(_exchange_copies(c_src, c_dst, ["gather"], *c_sems))

        ang = p_ref[...] * f_ref[...]
        sin = jnp.sin(ang)
        first_half = (_lane(CHUNK) & (HEAD_DIM - 1)) < HEAD_DIM // 2
        cos_ref[...] = jnp.cos(ang)
        sa_ref[...] = jnp.where(first_half, -sin, 0.0)
        sb_ref[...] = jnp.where(first_half, 0.0, sin)

        @pl.when(pl.program_id(0) == last)
        def _():
            _exchange_wait(_exchange_copies(c_src, c_dst, ["gather"], *c_sems))
            _gather2_finish(w_src, w_dst, *w_sems)

    tab = jax.ShapeDtypeStruct((s, CHUNK), F32)
    blk = pl.BlockSpec((tm, CHUNK), lambda i: (i, 0))
    gathered = [jax.ShapeDtypeStruct((N_DEV,) + a.shape, a.dtype) for a in list(weights) + [c_rows]]
    outs = pl.pallas_call(
        body, name="rope_tables_gather", grid=(s // tm,), out_shape=[tab, tab, tab] + gathered,
        in_specs=[pl.BlockSpec((tm, 1), lambda i: (i, 0)), pl.BlockSpec((1, CHUNK), lambda i: (0, 0))]
        + [ANY_SPEC] * (nw + 1),
        out_specs=[blk, blk, blk] + [ANY_SPEC] * (nw + 1),
        scratch_shapes=_exchange_sems(nw) + _exchange_sems(1),
        compiler_params=_params(1))(pos_col, inv_freq, *weights, c_rows)
    return outs[:3], outs[3:3 + nw], outs[3 + nw]


def _fwd_proj(x, mod, w_all, cos, sa, sb, bf_row, emat, ind, tm):
    s = x.shape[0]

    def body(x_ref, mod_ref, w_ref, cos_ref, sa_ref, sb_ref, bf_ref, e_ref, ind_ref,
             h_ref, qa_ref, ka_ref, va_ref, za_ref, qb_ref, kb_ref, vb_ref, zb_ref, ga_ref, gb_ref, f_ref,
             bnd_ref, nrm_ref, carry_ref):
        @pl.when(pl.program_id(0) == 0)
        def _():
            carry_ref[...] = jnp.zeros_like(carry_ref)

        xv = x_ref[...]
        rstd = lax.rsqrt(jnp.mean(xv * xv, axis=-1, keepdims=True) + NORM_EPS)
        h = (xv * rstd * mod_ref[0:1, :]) * (1.0 + mod_ref[1:2, :]) + mod_ref[2:3, :]
        hb = h.astype(BF16)
        h_ref[...] = hb.T

        def seg(name):
            off, w = SEG[name]
            return _dot(hb, w_ref[:, off:off + w])

        lane = _lane(CHUNK)
        low = lane < HEAD_DIM
        q_ones = _ones_lanes(64, 67)
        k_ones = _ones_lanes(67, 70)
        cos_t, sa_t, sb_t = cos_ref[...], sa_ref[...], sb_ref[...]

        def write_heads(ref, nat, extra_of, rope, scale):
            for p in range(N_HEADS // 2):
                c = nat[:, CHUNK * p:CHUNK * (p + 1)]
                if rope:
                    c = _rope(c, cos_t, sa_t, sb_t)
                if scale != 1.0:
                    c = c * scale
                for hh in range(2):
                    hd = 2 * p + hh
                    src = c if hh == 0 else pltpu.roll(c, HEAD_DIM, 1)
                    ref[:, CHUNK * hd:CHUNK * (hd + 1)] = jnp.where(low, src, extra_of(hd)).astype(BF16)

        write_heads(qa_ref, seg("qa"), lambda hd: q_ones, True, Q_SCALE)
        ka = seg("ka")
        va = seg("va")
        for kv in range(KV_GROUPS):
            sl = slice(CHUNK * kv, CHUNK * (kv + 1))
            ka_ref[:, sl] = jnp.where(low, _rope(ka[:, sl], cos_t, sa_t, sb_t), k_ones).astype(BF16)
            va_ref[:, sl] = jnp.where(low, va[:, sl], q_ones).astype(BF16)
        za_ref[...] = seg("za")
        zb_ref[...] = seg("zb")
        ga_ref[...] = seg("ga")
        gb_ref[...] = seg("gb")

        xf = seg("f") + bf_ref[...]
        f_ref[...] = xf
        logf = jnp.minimum(xf, 0.0) - jnp.log1p(jnp.exp(-jnp.abs(xf)))
        row = lax.broadcasted_iota(jnp.int32, (tm, tm), 0)
        col = lax.broadcasted_iota(jnp.int32, (tm, tm), 1)
        tri = jnp.where(col <= row, 1.0, 0.0).astype(BF16)
        hi, mid, lo = _split3(logf)
        cum = _dot(tri, hi.astype(BF16)) + _dot(tri, mid.astype(BF16)) + _dot(tri, lo.astype(BF16))
        cum = cum + carry_ref[0:1, :]
        carry_ref[0:1, :] += jnp.sum(logf, axis=0, keepdims=True)
        lane_all = _lane(AUG_W) & (CHUNK - 1)
        bias = _spread3(-cum, e_ref) + jnp.where((lane_all >= 67) & (lane_all < 70), 1.0, 0.0)

        qb, kb = seg("qb"), seg("kb")
        write_heads(qb_ref, qb, lambda hd: q_ones, False, Q_SCALE)
        write_heads(kb_ref, kb, lambda hd: bias[:, CHUNK * hd:CHUNK * (hd + 1)], False, 1.0)
        write_heads(vb_ref, seg("vb"), lambda hd: q_ones, False, 1.0)

        rid = lax.broadcasted_iota(jnp.int32, (tm, CHUNK), 0)
        bnd_ref[...] = jnp.zeros_like(bnd_ref)
        bnd_ref[0, 0:1, :] = jnp.sum(jnp.where(rid == 0, cum, 0.0), axis=0, keepdims=True)
        bnd_ref[0, 1:2, :] = jnp.sum(jnp.where(rid == tm - 1, cum, 0.0), axis=0, keepdims=True)

        @pl.when(pl.program_id(0) == 0)
        def _():
            nrm_ref[...] = jnp.zeros_like(nrm_ref)

        for r_, nat in ((0, qb * Q_SCALE), (1, kb)):
            sq = _dot((nat * nat).astype(BF16), ind_ref[...])
            nrm_ref[r_:r_ + 1, :] = jnp.maximum(nrm_ref[r_:r_ + 1, :], jnp.max(sq, axis=0, keepdims=True))

    row_blk = lambda w: pl.BlockSpec((tm, w), lambda i: (i, 0))
    full = lambda a: pl.BlockSpec(a.shape, lambda i: (0,) * a.ndim)
    sds = lambda w, dt: jax.ShapeDtypeStruct((s, w), dt)
    out_shape = [sds(AUG_W, BF16), sds(KV_GROUPS * CHUNK, BF16), sds(KV_GROUPS * CHUNK, BF16),
                 sds(512, F32), sds(AUG_W, BF16), sds(AUG_W, BF16), sds(AUG_W, BF16), sds(512, F32),
                 sds(D_MODEL, F32), sds(D_MODEL, F32), sds(CHUNK, F32)]
    small = [jax.ShapeDtypeStruct((s // tm, 8, CHUNK), F32), jax.ShapeDtypeStruct((8, CHUNK), F32)]
    return pl.pallas_call(
        body, name="fwd_proj", grid=(s // tm,),
        out_shape=[jax.ShapeDtypeStruct((D_MODEL, s), BF16)] + out_shape + small,
        in_specs=[row_blk(D_MODEL), full(mod), full(w_all), row_blk(CHUNK), row_blk(CHUNK), row_blk(CHUNK),
                  full(bf_row), full(emat), full(ind)],
        out_specs=[pl.BlockSpec((D_MODEL, tm), lambda i: (0, i))] + [row_blk(o.shape[1]) for o in out_shape] + [
            pl.BlockSpec((1, 8, CHUNK), lambda i: (i, 0, 0)), pl.BlockSpec((8, CHUNK), lambda i: (0, 0))],
        scratch_shapes=[pltpu.VMEM((8, CHUNK), F32)],
        compiler_params=_params(1))(x, mod, w_all, cos, sa, sb, bf_row, emat, ind)


def _swa_fwd(q_aug, k_aug, v_aug, sink_rows, tq):
    s = q_aug.shape[0]
    r = tq // WINDOW
    gw = GROUP * CHUNK

    def body(q_ref, kc_ref, kp_ref, vc_ref, vp_ref, sink_ref, o_ref, qb_ref):
        i = pl.program_id(1)
        st = GROUP * WINDOW
        qloc = lax.broadcasted_iota(jnp.int32, (st, 2 * WINDOW), 0) & (WINDOW - 1)
        col = lax.broadcasted_iota(jnp.int32, (st, 2 * WINDOW), 1)
        band = (col > qloc) & (col <= qloc + WINDOW)
        head = jnp.right_shift(lax.broadcasted_iota(jnp.int32, (st, 1), 0), 7)
        sink = jnp.zeros((st, 1), F32)
        for g in range(GROUP):
            sink = jnp.where(head == g, jnp.max(sink_ref[0, g:g + 1, :], axis=1, keepdims=True), sink)
        lane = _lane(CHUNK)
        for sub in range(r):
            rows = slice(WINDOW * sub, WINDOW * (sub + 1))
            q = jnp.concatenate([q_ref[rows, CHUNK * g:CHUNK * (g + 1)] for g in range(GROUP)], axis=0)
            if sub == 0:
                k = jnp.concatenate([kp_ref[...], kc_ref[rows, :]], axis=0)
                v = jnp.concatenate([vp_ref[...], vc_ref[rows, :]], axis=0)
                valid = band & ((col >= WINDOW) | (i > 0))
            else:
                both = slice(WINDOW * (sub - 1), WINDOW * (sub + 1))
                k, v, valid = kc_ref[both, :], vc_ref[both, :], band
            sc = jnp.where(valid, _dot_nt(q, k), NEG)
            m = jnp.maximum(jnp.max(sc, axis=1, keepdims=True), sink)
            acc = _dot(jnp.exp(sc - m).astype(BF16), v)
            denom = _lane_sum(acc, 64, 65) + jnp.exp(sink - m)
            out = acc / denom
            aug = _place3(lane, 67, -(m + jnp.log(denom)), q.astype(F32)).astype(BF16)
            hrows = lambda a, g: a[WINDOW * g:WINDOW * (g + 1), :]
            for g in range(GROUP):
                qb_ref[rows, CHUNK * g:CHUNK * (g + 1)] = hrows(aug, g)
            for pp in range(GROUP // 2):
                o_ref[rows, CHUNK * pp:CHUNK * (pp + 1)] = _pair(hrows(out, 2 * pp), hrows(out, 2 * pp + 1))

    return pl.pallas_call(
        body, name="swa_fwd", grid=(KV_GROUPS, s // tq),
        out_shape=[jax.ShapeDtypeStruct((s, 512), F32), jax.ShapeDtypeStruct((s, AUG_W), BF16)],
        in_specs=[pl.BlockSpec((tq, gw), lambda kv, i: (i, kv)),
                  pl.BlockSpec((tq, CHUNK), lambda kv, i: (i, kv)),
                  pl.BlockSpec((WINDOW, CHUNK), lambda kv, i: (jnp.maximum(i * r - 1, 0), kv)),
                  pl.BlockSpec((tq, CHUNK), lambda kv, i: (i, kv)),
                  pl.BlockSpec((WINDOW, CHUNK), lambda kv, i: (jnp.maximum(i * r - 1, 0), kv)),
                  pl.BlockSpec((1, 8, CHUNK), lambda kv, i: (kv, 0, 0))],
        out_specs=[pl.BlockSpec((tq, GROUP * HEAD_DIM), lambda kv, i: (i, kv)),
                   pl.BlockSpec((tq, gw), lambda kv, i: (i, kv))],
        compiler_params=_params(2))(q_aug, k_aug, k_aug, v_aug, v_aug, sink_rows)


def _swa_bwd(k_aug, v_aug, q_bwd, do_aug, cos, sa, sb, sink_rows, tk):
    s = k_aug.shape[0]
    r = tk // WINDOW
    nt = s // tk
    nb = s // WINDOW
    gw = GROUP * CHUNK

    def body(k_ref, v_ref, q_ref, qn_ref, do_ref, don_ref, cos_ref, sa_ref, sb_ref, sink_ref,
             da_ref, dsink_ref, carry_ref, acc_ref):
        j = pl.program_id(0)

        @pl.when(j == 0)
        def _():
            carry_ref[...] = jnp.zeros_like(carry_ref)
            dsink_ref[...] = jnp.zeros_like(dsink_ref)

        acc_ref[...] = jnp.zeros_like(acc_ref)
        acc_ref[0:WINDOW, :] = carry_ref[...]
        dk_col, dv_col = SEG["ka"][0] - SEG["qa"][0], SEG["va"][0] - SEG["qa"][0]
        st = GROUP * WINDOW
        key = lax.broadcasted_iota(jnp.int32, (WINDOW, 2 * st), 0)
        col = lax.broadcasted_iota(jnp.int32, (WINDOW, 2 * st), 1)
        qloc = col & (WINDOW - 1)
        band = ((col < st) & (key <= qloc)) | ((col >= st) & (key > qloc))
        lane = _lane(CHUNK)
        low = lane < HEAD_DIM
        cos_t, sa_t, sb_t = cos_ref[...], sa_ref[...], sb_ref[...]
        for kv, sub in [(kv, sub) for kv in range(KV_GROUPS) for sub in range(r)]:
            heads = [slice(CHUNK * (GROUP * kv + g), CHUNK * (GROUP * kv + g + 1)) for g in range(GROUP)]
            kvs = slice(CHUNK * kv, CHUNK * (kv + 1))
            rows = slice(WINDOW * sub, WINDOW * (sub + 1))
            nxt = slice(WINDOW * (sub + 1), WINDOW * (sub + 2))
            k, v = k_ref[rows, kvs], v_ref[rows, kvs]
            q_cur, do_cur = [q_ref[rows, cs] for cs in heads], [do_ref[rows, cs] for cs in heads]
            if sub < r - 1:
                q_nxt, do_nxt, valid = [q_ref[nxt, cs] for cs in heads], [do_ref[nxt, cs] for cs in heads], band
            else:
                q_nxt, do_nxt = [qn_ref[:, cs] for cs in heads], [don_ref[:, cs] for cs in heads]
                valid = band & ((col < st) | (j < nt - 1))
            q = jnp.concatenate(q_cur + q_nxt, axis=0)
            do = jnp.concatenate(do_cur + do_nxt, axis=0)
            pt = jnp.exp(jnp.where(valid, _dot_nt(k, q), NEG))
            ds = (pt * _dot_nt(v, do)).astype(BF16)
            dv = _dot(pt.astype(BF16), do)
            dk = _dot(ds, q)
            dq = _dot_tn(ds, k)
            for g, cs in enumerate(heads):
                acc_ref[rows, cs] += dq[WINDOW * g:WINDOW * (g + 1), :]
                dqn = dq[st + WINDOW * g:st + WINDOW * (g + 1), :]
                if sub < r - 1:
                    acc_ref[nxt, cs] += dqn
                else:
                    carry_ref[:, cs] = dqn
                sink = jnp.max(sink_ref[kv, g:g + 1, :], axis=1, keepdims=True)
                p_sink = jnp.exp(sink + _lane_sum(q_cur[g].astype(F32), 67, 70))
                term = jnp.sum(p_sink * _lane_sum(do_cur[g].astype(F32), 64, 67), axis=0, keepdims=True)
                dsink_ref[kv, g:g + 1, :] += jnp.broadcast_to(term, (1, CHUNK))
            da_ref[rows, dk_col + CHUNK * kv:dk_col + CHUNK * (kv + 1)] = _rope_inv(
                jnp.where(low, dk, 0.0), cos_t[rows, :], sa_t[rows, :], sb_t[rows, :]).astype(BF16)
            da_ref[rows, dv_col + CHUNK * kv:dv_col + CHUNK * (kv + 1)] = jnp.where(low, dv, 0.0).astype(BF16)
        for pp in range(N_HEADS // 2):
            d = _pair(acc_ref[:, CHUNK * 2 * pp:CHUNK * (2 * pp + 1)], acc_ref[:, CHUNK * (2 * pp + 1):CHUNK * (2 * pp + 2)])
            da_ref[:, CHUNK * pp:CHUNK * (pp + 1)] = (_rope_inv(d, cos_t, sa_t, sb_t) * Q_SCALE).astype(BF16)

    cur = lambda w: pl.BlockSpec((tk, w), lambda j: (j, 0))
    nxt = pl.BlockSpec((WINDOW, AUG_W), lambda j: (jnp.minimum((j + 1) * r, nb - 1), 0))
    whole = pl.BlockSpec((KV_GROUPS, 8, CHUNK), lambda j: (0, 0, 0))
    return pl.pallas_call(
        body, name="swa_bwd", grid=(nt,),
        out_shape=[jax.ShapeDtypeStruct((s, 1024), BF16), jax.ShapeDtypeStruct((KV_GROUPS, 8, CHUNK), F32)],
        in_specs=[cur(KV_GROUPS * CHUNK), cur(KV_GROUPS * CHUNK), cur(AUG_W), nxt, cur(AUG_W), nxt,
                  cur(CHUNK), cur(CHUNK), cur(CHUNK), whole],
        out_specs=[cur(1024), whole],
        scratch_shapes=[pltpu.VMEM((WINDOW, AUG_W), F32), pltpu.VMEM((tk, AUG_W), F32)],
        compiler_params=_params(1))(k_aug, v_aug, q_bwd, q_bwd, do_aug, do_aug, cos, sa, sb, sink_rows)


def _fox_fwd(first_key, q_aug, k_aug, v_aug, t):
    s = q_aug.shape[0]
    pw = 2 * CHUNK

    def body(lo_ref, q_ref, k_ref, v_ref, o_ref, qb_ref):
        i = pl.program_id(1)
        first = lo_ref[pl.program_id(0), i]
        row = lax.broadcasted_iota(jnp.int32, (t, t), 0)
        col = lax.broadcasted_iota(jnp.int32, (t, t), 1)
        lane = _lane(CHUNK)
        heads = (slice(0, CHUNK), slice(CHUNK, pw))
        qs = [q_ref[:, hs] for hs in heads]

        def step(jb, carry, masked):
            rows = pl.ds(pl.multiple_of(jb * t, t), t)
            scs = [_dot_nt(qs[hh], k_ref[rows, hs]) for hh, hs in enumerate(heads)]
            if masked:
                scs = [jnp.where(col <= row, sc, NEG) for sc in scs]
            m_news = [jnp.maximum(carry[2 * hh], jnp.max(scs[hh], axis=1, keepdims=True)) for hh in range(2)]
            ps = [jnp.exp(scs[hh] - m_news[hh]).astype(BF16) for hh in range(2)]
            new = []
            for hh, hs in enumerate(heads):
                m, acc = carry[2 * hh], carry[2 * hh + 1]
                new += [m_news[hh], jnp.exp(m - m_news[hh]) * acc + _dot(ps[hh], v_ref[rows, hs])]
            return tuple(new)

        init = (jnp.full((t, 1), NEG, F32), jnp.zeros((t, CHUNK), F32)) * 2
        carry = step(i, lax.fori_loop(first, i, lambda jb, c: step(jb, c, False), init), True)
        outs = []
        for hh, hs in enumerate(heads):
            m, acc = carry[2 * hh], carry[2 * hh + 1]
            denom = _lane_sum(acc, 64, 65)
            outs.append(acc / denom)
            lse = m + jnp.log(denom)
            qb_ref[:, hs] = _place3(lane, 67, -lse, qs[hh].astype(F32)).astype(BF16)
        o_ref[...] = _pair(outs[0], outs[1])

    return pl.pallas_call(
        body, name="fox_fwd", grid=(N_HEADS // 2, s // t),
        out_shape=[jax.ShapeDtypeStruct((s, 512), F32), jax.ShapeDtypeStruct((s, AUG_W), BF16)],
        in_specs=[pl.BlockSpec(memory_space=pltpu.SMEM),
                  pl.BlockSpec((t, pw), lambda p, i: (i, p)),
                  pl.BlockSpec((s, pw), lambda p, i: (0, p)),
                  pl.BlockSpec((s, pw), lambda p, i: (0, p))],
        out_specs=[pl.BlockSpec((t, CHUNK), lambda p, i: (i, p)), pl.BlockSpec((t, pw), lambda p, i: (i, p))],
        compiler_params=_params(2))(first_key, q_aug, k_aug, v_aug)


def _fox_block_ranges(bounds, norms, r):
    cum_first = bounds[0::r, 0, :N_HEADS]
    cum_last = bounds[r - 1::r, 1, :N_HEADS]
    n = cum_first.shape[0]
    reach = 2.0 * 1.02 * jnp.sqrt(norms[0, :N_HEADS] * norms[1, :N_HEADS]) + UNDERFLOW
    blk = jnp.arange(n, dtype=jnp.int32)
    decay = cum_last[None, :, :] - cum_first[:, None, :]
    skip = (decay > reach[None, None, :]) & (blk[None, :, None] < blk[:, None, None])
    first = jnp.min(jnp.where(skip, n, blk[None, :, None]), axis=1)
    first = jnp.minimum(first[:, 0::2], first[:, 1::2]).T
    needed = (first[:, :, None] <= blk[None, None, :]) & (blk[None, :, None] >= blk[None, None, :])
    last = jnp.max(jnp.where(needed, blk[None, :, None], blk[None, None, :]), axis=1)
    return first.astype(jnp.int32), last.astype(jnp.int32)


def _fox_bwd(last_query, k_aug, v_aug, q_bwd, do_aug, t):
    s = k_aug.shape[0]
    n = s // t
    pw = 2 * CHUNK

    def body(hi_ref, k_ref, v_ref, q_ref, do_ref, dk_ref, dv_ref, dck_ref, dq_ref, dcq_ref, dq_scr):
        j = pl.program_id(1)
        last = hi_ref[pl.program_id(0), j]

        @pl.when(j == 0)
        def _():
            dq_scr[...] = jnp.zeros_like(dq_scr)

        row = lax.broadcasted_iota(jnp.int32, (t, t), 0)
        col = lax.broadcasted_iota(jnp.int32, (t, t), 1)
        lane = _lane(CHUNK)
        heads = (slice(0, CHUNK), slice(CHUNK, pw))
        ks = [k_ref[:, hs] for hs in heads]
        vs = [v_ref[:, hs] for hs in heads]

        def step(ib, carry, masked):
            rows = pl.ds(pl.multiple_of(ib * t, t), t)
            new = []
            for hh, hs in enumerate(heads):
                dv, dk = carry[2 * hh], carry[2 * hh + 1]
                q, do = q_ref[rows, hs], do_ref[rows, hs]
                st = _dot_nt(ks[hh], q)
                if masked:
                    st = jnp.where(row <= col, st, NEG)
                pt = jnp.exp(st)
                ds = (pt * _dot_nt(vs[hh], do)).astype(BF16)
                new += [dv + _dot(pt.astype(BF16), do), dk + _dot(ds, q)]
                dq_scr[rows, hs] += _dot_tn(ds, ks[hh])
            return tuple(new)

        zero = jnp.zeros((t, CHUNK), F32)
        carry = lax.fori_loop(j + 1, last + 1, lambda ib, c: step(ib, c, False), step(j, (zero,) * 4, True))
        dvs, dks = (carry[0], carry[2]), (carry[1], carry[3])
        dk_ref[...] = _pair(dks[0], dks[1]).astype(BF16)
        dv_ref[...] = _pair(dvs[0], dvs[1]).astype(BF16)
        dck_ref[0] = jnp.where(lane == 0, pltpu.roll(dks[0], 64, 1),
                               jnp.where(lane == 1, pltpu.roll(dks[1], 65, 1), 0.0))

        @pl.when(j == n - 1)
        def _():
            for ib in range(n):
                rows = slice(t * ib, t * (ib + 1))
                d0, d1 = dq_scr[rows, 0:CHUNK], dq_scr[rows, CHUNK:pw]
                dq_ref[rows, :] = (_pair(d0, d1) * Q_SCALE).astype(BF16)
                dcq_ref[0, rows, :] = jnp.where(lane == 0, pltpu.roll(d0, CHUNK - 67, 1),
                                                jnp.where(lane == 1, pltpu.roll(d1, CHUNK - 66, 1), 0.0))

    return pl.pallas_call(
        body, name="fox_bwd", grid=(N_HEADS // 2, n),
        out_shape=[jax.ShapeDtypeStruct((s, 512), BF16), jax.ShapeDtypeStruct((s, 512), BF16),
                   jax.ShapeDtypeStruct((N_HEADS // 2, s, CHUNK), F32), jax.ShapeDtypeStruct((s, 512), BF16),
                   jax.ShapeDtypeStruct((N_HEADS // 2, s, CHUNK), F32)],
        in_specs=[pl.BlockSpec(memory_space=pltpu.SMEM),
                  pl.BlockSpec((t, pw), lambda p, j: (j, p)), pl.BlockSpec((t, pw), lambda p, j: (j, p)),
                  pl.BlockSpec((s, pw), lambda p, j: (0, p)), pl.BlockSpec((s, pw), lambda p, j: (0, p))],
        out_specs=[pl.BlockSpec((t, CHUNK), lambda p, j: (j, p)), pl.BlockSpec((t, CHUNK), lambda p, j: (j, p)),
                   pl.BlockSpec((1, t, CHUNK), lambda p, j: (p, j, 0)), pl.BlockSpec((s, CHUNK), lambda p, j: (0, p)),
                   pl.BlockSpec((1, s, CHUNK), lambda p, j: (p, 0, 0))],
        scratch_shapes=[pltpu.VMEM((s, pw), F32)],
        compiler_params=_params(2))(last_query, k_aug, v_aug, q_bwd, do_aug)


def _fgate_bwd(dcq, dck, xf, tm):
    s = xf.shape[0]
    nt = s // tm

    def body(dq_ref, dc_ref, xf_ref, df_ref, dbf_ref, carry_ref):
        @pl.when(pl.program_id(0) == 0)
        def _():
            carry_ref[...] = jnp.zeros_like(carry_ref)
            dbf_ref[...] = jnp.zeros_like(dbf_ref)

        row = lax.broadcasted_iota(jnp.int32, (tm, tm), 0)
        col = lax.broadcasted_iota(jnp.int32, (tm, tm), 1)
        tri = jnp.where(col >= row, 1.0, 0.0).astype(BF16)
        dcum = dq_ref[0] - dc_ref[0]
        for p in range(1, N_HEADS // 2):
            dcum = dcum + pltpu.roll(dq_ref[p] - dc_ref[p], 2 * p, 1)
        hi, mid, lo = _split3(dcum)
        dlogf = _dot(tri, hi.astype(BF16)) + _dot(tri, mid.astype(BF16)) + _dot(tri, lo.astype(BF16))
        dlogf = dlogf + carry_ref[0:1, :]
        carry_ref[0:1, :] += jnp.sum(dcum, axis=0, keepdims=True)
        df = dlogf * _sigmoid(-xf_ref[...])
        df_ref[...] = df.astype(BF16)
        dbf_ref[0:1, :] += jnp.sum(df, axis=0, keepdims=True)

    rev = pl.BlockSpec((tm, CHUNK), lambda i: (nt - 1 - i, 0))
    rev4 = pl.BlockSpec((N_HEADS // 2, tm, CHUNK), lambda i: (0, nt - 1 - i, 0))
    return pl.pallas_call(
        body, name="fgate_bwd", grid=(nt,),
        out_shape=[jax.ShapeDtypeStruct((s, CHUNK), BF16), jax.ShapeDtypeStruct((8, CHUNK), F32)],
        in_specs=[rev4, rev4, rev], out_specs=[rev, pl.BlockSpec((8, CHUNK), lambda i: (0, 0))],
        scratch_shapes=[pltpu.VMEM((8, CHUNK), F32)],
        compiler_params=_params(1))(dcq, dck, xf)


def _post(x, tgt, att_a, att_b, za, zb, ga, gb, woa_t, wob_t, wout, vec, ind, emat, tm):
    s = x.shape[0]

    def body(x_ref, t_ref, aa_ref, ab_ref, za_ref, zb_ref, ga_ref, gb_ref, woat_ref, wobt_ref, wout_ref,
             vec_ref, ind_ref, e_ref,
             dx1_ref, ua_ref, ub_ref, mg_ref, do_ref, dya_ref, dyb_ref, doa_ref, dob_ref, dzg_ref, acc_ref):
        @pl.when(pl.program_id(0) == 0)
        def _():
            acc_ref[...] = jnp.zeros_like(acc_ref)

        gate, gfin = vec_ref[0:1, :], vec_ref[1:2, :]
        inv_d = 1.0 / D_MODEL

        def branch_fwd(att_ref, z_ref, w_ref, u_ref):
            att, z = att_ref[...], z_ref[...]
            sz = _sigmoid(z)
            silu = z * sz
            u = (att * silu).astype(BF16)
            u_ref[...] = u.T
            return att, z, sz, silu, _dot_nt(u, w_ref[...])

        att_a, z_a, sz_a, silu_a, y_a = branch_fwd(aa_ref, za_ref, woat_ref, ua_ref)
        att_b, z_b, sz_b, silu_b, y_b = branch_fwd(ab_ref, zb_ref, wobt_ref, ub_ref)
        sg_a, sg_b = _sigmoid(ga_ref[...]), _sigmoid(gb_ref[...])
        merged = (sg_a * y_a + sg_b * y_b).astype(BF16)
        mg_ref[...] = merged.T
        o = _dot(merged, wout_ref[...])
        x1 = x_ref[...] + gate * o
        rstd = lax.rsqrt(jnp.mean(x1 * x1, axis=-1, keepdims=True) + NORM_EPS)
        xh = x1 * rstd
        diff = xh * gfin - t_ref[...]
        acc_ref[2:3, :] += (0.5 * inv_d) * jnp.sum(diff * diff, axis=0, keepdims=True)
        dy = diff * inv_d
        acc_ref[1:2, :] += jnp.sum(dy * xh, axis=0, keepdims=True)
        dyg = dy * gfin
        dx1 = rstd * (dyg - xh * jnp.mean(dyg * xh, axis=-1, keepdims=True))
        dx1_ref[...] = dx1
        acc_ref[0:1, :] += jnp.sum(dx1 * o, axis=0, keepdims=True)
        d_o = (dx1 * gate).astype(BF16)
        do_ref[...] = d_o
        dmg = _dot_nt(d_o, wout_ref[...])

        lane = _lane(CHUNK)
        low = lane < HEAD_DIM
        lane_all = _lane(AUG_W) & (CHUNK - 1)

        zg0 = SEG["za"][0]

        def branch_bwd(sg, y, att, z, sz, silu, wt_ref, dy_ref, g_name, z_name, doaug_ref):
            dyb = (dmg * sg).astype(BF16)
            dy_ref[...] = dyb
            g_off, z_off = SEG[g_name][0] - zg0, SEG[z_name][0] - zg0
            dzg_ref[:, g_off:g_off + D_MODEL] = (dmg * y * sg * (1.0 - sg)).astype(BF16)
            du = _dot(dyb, wt_ref[...])
            datt = du * silu
            dzg_ref[:, z_off:z_off + 512] = (du * att * (sz * (1.0 + z * (1.0 - sz)))).astype(BF16)
            extra = _spread3(-_dot_split(datt * att, ind_ref[...]), e_ref)
            for p in range(N_HEADS // 2):
                c = datt[:, CHUNK * p:CHUNK * (p + 1)]
                for hh in range(2):
                    hd = 2 * p + hh
                    src = c if hh == 0 else pltpu.roll(c, HEAD_DIM, 1)
                    doaug_ref[:, CHUNK * hd:CHUNK * (hd + 1)] = jnp.where(
                        low, src, extra[:, CHUNK * hd:CHUNK * (hd + 1)]).astype(BF16)

        branch_bwd(sg_a, y_a, att_a, z_a, sz_a, silu_a, woat_ref, dya_ref, "ga", "za", doa_ref)
        branch_bwd(sg_b, y_b, att_b, z_b, sz_b, silu_b, wobt_ref, dyb_ref, "gb", "zb", dob_ref)

    row_blk = lambda w: pl.BlockSpec((tm, w), lambda i: (i, 0))
    full = lambda a: pl.BlockSpec(a.shape, lambda i: (0,) * a.ndim)
    sds = lambda w, dt: jax.ShapeDtypeStruct((s, w), dt)
    tds = lambda w: jax.ShapeDtypeStruct((w, s), BF16)
    col_blk = lambda w: pl.BlockSpec((w, tm), lambda i: (0, i))
    out_shape = [sds(D_MODEL, F32), tds(512), tds(512), tds(D_MODEL), sds(D_MODEL, BF16),
                 sds(D_MODEL, BF16), sds(D_MODEL, BF16), sds(AUG_W, BF16), sds(AUG_W, BF16), sds(3072, BF16),
                 jax.ShapeDtypeStruct((8, D_MODEL), F32)]
    ins = [x, tgt, att_a, att_b, za, zb, ga, gb, woa_t, wob_t, wout, vec, ind, emat]
    in_specs = [row_blk(a.shape[1]) for a in ins[:8]] + [full(a) for a in ins[8:]]
    out_specs = ([row_blk(D_MODEL), col_blk(512), col_blk(512), col_blk(D_MODEL)]
                 + [row_blk(o.shape[1]) for o in out_shape[4:-1]] + [pl.BlockSpec((8, D_MODEL), lambda i: (0, 0))])
    return pl.pallas_call(body, name="post", grid=(s // tm,), out_shape=out_shape, in_specs=in_specs,
                          out_specs=out_specs, compiler_params=_params(1))(*ins)


def _bwd_pre(dproj, w_all, x, dx1, mod, tm, chip_halves, modes):
    s = x.shape[0]
    ng, nh = len(DPROJ_GROUPS), len(chip_halves)
    last = s // tm - 1

    def body(*refs):
        dp_refs = refs[:ng]
        wt_ref, x_ref, dx1_ref, mod_ref = refs[ng:ng + 4]
        send_refs = refs[ng + 4:ng + 4 + nh]
        gx_ref, acc_ref = refs[ng + 4 + nh:ng + 6 + nh]
        recv_refs = refs[ng + 6 + nh:ng + 6 + 2 * nh]
        sems = refs[ng + 6 + 2 * nh:]

        @pl.when(pl.program_id(0) == 0)
        def _():
            acc_ref[...] = jnp.zeros_like(acc_ref)
            _exchange_start(_exchange_copies(send_refs, recv_refs, modes, *sems))

        dh = None
        for dp_ref, (_, off, w) in zip(dp_refs, DPROJ_GROUPS):
            part = _dot_nt(dp_ref[...], wt_ref[:, off:off + w])
            dh = part if dh is None else dh + part
        xv = x_ref[...]
        rstd = lax.rsqrt(jnp.mean(xv * xv, axis=-1, keepdims=True) + NORM_EPS)
        xh = xv * rstd
        gn = mod_ref[0:1, :]
        acc_ref[0:1, :] += jnp.sum(dh, axis=0, keepdims=True)
        acc_ref[1:2, :] += jnp.sum(dh * (xh * gn), axis=0, keepdims=True)
        dn = dh * (1.0 + mod_ref[1:2, :])
        acc_ref[2:3, :] += jnp.sum(dn * xh, axis=0, keepdims=True)
        dxh = dn * gn
        gx_ref[...] = dx1_ref[...] + rstd * (dxh - xh * jnp.mean(dxh * xh, axis=-1, keepdims=True))

        @pl.when(pl.program_id(0) == last)
        def _():
            _exchange_wait(_exchange_copies(send_refs, recv_refs, modes, *sems))

    row_blk = lambda w: pl.BlockSpec((tm, w), lambda i: (i, 0))
    full = lambda a: pl.BlockSpec(a.shape, lambda i: (0,) * a.ndim)
    outs = pl.pallas_call(
        body, name="bwd_pre", grid=(s // tm,),
        out_shape=[jax.ShapeDtypeStruct((s, D_MODEL), F32), jax.ShapeDtypeStruct((8, D_MODEL), F32)]
        + _exchange_out_shapes(chip_halves, modes),
        in_specs=[row_blk(w) for _, _, w in DPROJ_GROUPS] + [full(w_all), row_blk(D_MODEL), row_blk(D_MODEL), full(mod)]
        + [ANY_SPEC] * nh,
        out_specs=[row_blk(D_MODEL), pl.BlockSpec((8, D_MODEL), lambda i: (0, 0))] + [ANY_SPEC] * nh,
        scratch_shapes=_exchange_sems(nh),
        compiler_params=_params(1))(*dproj, w_all, x, dx1, mod, *chip_halves)
    return outs[0], outs[1], outs[2:]


def _accumulate_tokens(a, b_ref, acc_ref, o_ref, last):
    @pl.when(pl.program_id(0) == 0)
    def _():
        acc_ref[...] = jnp.zeros_like(acc_ref)

    acc_ref[...] += _dot(a, b_ref[...])

    @pl.when(pl.program_id(0) == last)
    def _():
        o_ref[...] = acc_ref[...].astype(BF16)


def _token_matmul_multi(a_t, bs, tk, name):
    m, s = a_t.shape
    nb = len(bs)
    last = s // tk - 1

    def body(*refs):
        a_ref, b_refs, o_refs, acc_refs = refs[0], refs[1:1 + nb], refs[1 + nb:1 + 2 * nb], refs[1 + 2 * nb:]
        a = a_ref[...]
        for b_ref, o_ref, acc_ref in zip(b_refs, o_refs, acc_refs):
            _accumulate_tokens(a, b_ref, acc_ref, o_ref, last)

    return pl.pallas_call(
        body, name=name, grid=(s // tk,),
        out_shape=[jax.ShapeDtypeStruct((m, b.shape[1]), BF16) for b in bs],
        in_specs=[pl.BlockSpec((m, tk), lambda k: (0, k))] + [pl.BlockSpec((tk, b.shape[1]), lambda k: (k, 0)) for b in bs],
        out_specs=[pl.BlockSpec((m, b.shape[1]), lambda k: (0, 0)) for b in bs],
        scratch_shapes=[pltpu.VMEM((m, b.shape[1]), F32) for b in bs],
        compiler_params=_params(1))(a_t, *bs)


def _adam_math(g, w, m, v):
    m2 = ADAM_B1 * m + (1.0 - ADAM_B1) * g
    v2 = ADAM_B2 * v + (1.0 - ADAM_B2) * (g * g)
    delta = -ADAM_LR * ((m2 / ADAM_C1) / (jnp.sqrt(v2 / ADAM_C2) + ADAM_EPS) + ADAM_WD * w)
    return delta, m2, v2


def _adamw_parts(full, parts, others, w, m, v, tr, name):
    _, rws, cols = w.shape

    def body(oth_ref, f_ref, p_ref, w_ref, m_ref, v_ref, g_ref, d_ref, m2_ref, v2_ref):
        g = None
        for chip in range(N_CHIPS):
            part = (1.0 - oth_ref[chip]) * f_ref[chip] + oth_ref[chip] * p_ref[chip].astype(F32)
            g = part if g is None else g + part
        g_ref[0] = g
        d_ref[0], m2_ref[0], v2_ref[0] = _adam_math(g, w_ref[0], m_ref[0], v_ref[0])

    blk = pl.BlockSpec((1, tr, cols), lambda i: (0, i, 0))
    slots = pl.BlockSpec((N_CHIPS, tr, cols), lambda i: (0, i, 0))
    o = jax.ShapeDtypeStruct((1, rws, cols), F32)
    return pl.pallas_call(
        body, name=name, grid=(rws // tr,), out_shape=[o, o, o, o],
        in_specs=[pl.BlockSpec(memory_space=pltpu.SMEM), slots, slots, blk, blk, blk],
        out_specs=[blk, blk, blk, blk], compiler_params=_params(1))(others, full, parts, w, m, v)


def _adamw_devices(parts, w, m, v, tr, name):
    _, rws, cols = w.shape

    def body(p_ref, w_ref, m_ref, v_ref, g_ref, d_ref, m2_ref, v2_ref):
        g = p_ref[0].astype(F32)
        for dev in range(1, N_DEV):
            g = g + p_ref[dev].astype(F32)
        g_ref[0] = g
        d_ref[0], m2_ref[0], v2_ref[0] = _adam_math(g, w_ref[0], m_ref[0], v_ref[0])

    blk = pl.BlockSpec((1, tr, cols), lambda i: (0, i, 0))
    o = jax.ShapeDtypeStruct((1, rws, cols), F32)
    return pl.pallas_call(
        body, name=name, grid=(rws // tr,), out_shape=[o, o, o, o],
        in_specs=[pl.BlockSpec((N_DEV, tr, cols), lambda i: (0, i, 0)), blk, blk, blk],
        out_specs=[blk, blk, blk, blk], compiler_params=_params(1))(parts, w, m, v)


def _adamw_ada(c_lanes, d_rows, w, m, v, tr):
    _, rws, cols = w.shape

    def body(c_ref, d_ref, w_ref, m_ref, v_ref, g_ref, dl_ref, m2_ref, v2_ref):
        for k in range(cols // CHUNK):
            cs = slice(CHUNK * k, CHUNK * (k + 1))
            g = c_ref[0] * d_ref[0:1, cs]
            for b in range(1, N_DEV):
                g = g + c_ref[b] * d_ref[b:b + 1, cs]
            g_ref[0, :, cs] = g
            dl_ref[0, :, cs], m2_ref[0, :, cs], v2_ref[0, :, cs] = _adam_math(
                g, w_ref[0, :, cs], m_ref[0, :, cs], v_ref[0, :, cs])

    blk = pl.BlockSpec((1, tr, cols), lambda i: (0, i, 0))
    o = jax.ShapeDtypeStruct((1, rws, cols), F32)
    return pl.pallas_call(
        body, name="adamw_ada", grid=(rws // tr,), out_shape=[o, o, o, o],
        in_specs=[pl.BlockSpec((N_DEV, tr, CHUNK), lambda i: (0, i, 0)), pl.BlockSpec((N_DEV, cols), lambda i: (0, 0)),
                  blk, blk, blk],
        out_specs=[blk, blk, blk, blk], compiler_params=_params(1))(c_lanes, d_rows, w, m, v)


def _adamw_small(packs, w, m, v):
    def body(p_ref, w_ref, m_ref, v_ref, g_ref, d_ref, m2_ref, v2_ref, loss_ref):
        g = p_ref[0]
        for dev in range(1, N_DEV):
            g = g + p_ref[dev]
        g_ref[...] = g
        d_ref[...], m2_ref[...], v2_ref[...] = _adam_math(g, w_ref[...], m_ref[...], v_ref[...])
        loss_ref[...] = jnp.broadcast_to(jnp.sum(g_ref[2:3, 0:D_MODEL], axis=1, keepdims=True), loss_ref.shape)

    o = jax.ShapeDtypeStruct(w.shape, F32)
    return pl.pallas_call(body, name="adamw_small", out_shape=[o, o, o, o, jax.ShapeDtypeStruct((8, CHUNK), F32)],
                          compiler_params=_params(0))(packs, w, m, v)


def _tile(s, want):
    return min(s, want)


def _layout_pieces():
    orig = {"qa": 0, "ka": 512, "va": 640, "za": 768, "qb": 1280, "kb": 1792, "vb": 2304, "f": 2816, "zb": 2824,
            "ga": 3336, "gb": 4360}
    pieces = []
    for name, (off, w) in SEG.items():
        if name in ("ka", "va"):
            pieces += [(orig[name] + HEAD_DIM * kv, orig[name] + HEAD_DIM * (kv + 1), off + CHUNK * kv) for kv in range(KV_GROUPS)]
        else:
            pieces.append((orig[name], orig[name] + (N_HEADS if name == "f" else w), off))
    return pieces


def _assemble_w_all(win_g):
    cols, pos = [], 0
    for lo, hi, new in sorted(_layout_pieces(), key=lambda t: t[2]):
        if new > pos:
            cols.append(jnp.zeros((D_MODEL, new - pos), win_g.dtype))
        col = lo
        while col < hi:
            dev = col // IN_SHARD
            end = min(hi, (dev + 1) * IN_SHARD)
            cols.append(win_g[dev, :, col - dev * IN_SHARD:end - dev * IN_SHARD])
            col = end
        pos = new + hi - lo
    cols.append(jnp.zeros((D_MODEL, PROJ_W - pos), win_g.dtype))
    return jnp.concatenate(cols, axis=1)


def _grad_slot(dw_groups, dev):
    lo_d, hi_d = dev * IN_SHARD, (dev + 1) * IN_SHARD
    cols = []
    for lo, hi, new in sorted(_layout_pieces()):
        a, b = max(lo, lo_d), min(hi, hi_d)
        if a < b:
            arr, off = next((g, o) for g, (_, o, w) in zip(dw_groups, DPROJ_GROUPS) if o <= new < o + w)
            cols.append(arr[:, new - off + a - lo:new - off + b - lo])
    return jnp.concatenate(cols, axis=1)


def kernel(x, c, positions, w_ada, b_ada, g_norm, w_in, b_f, sinks, w_o_swa, w_o_fox, w_out, g_final, loss_target, m_w_ada, m_b_ada, m_g_norm, m_w_in, m_b_f, m_sinks, m_w_o_swa, m_w_o_fox, m_w_out, m_g_final, v_w_ada, v_b_ada, v_g_norm, v_w_in, v_b_f, v_sinks, v_w_o_swa, v_w_o_fox, v_w_out, v_g_final):
    s = x.shape[1]
    tm = _tile(s, 256)
    ta = _tile(s, 512)
    me = 4 * lax.axis_index("x") + 2 * lax.axis_index("y") + lax.axis_index("c")
    x2, tgt = x[0], loss_target[0]

    inv_freq = np.power(np.float32(ROPE_THETA), -np.arange(0, HEAD_DIM, 2, dtype=np.float32) / HEAD_DIM)
    inv_freq = jnp.asarray(np.tile(inv_freq, CHUNK // (HEAD_DIM // 2))[None, :], F32)
    (cos, sa, sb), (win_g, woa_g, wob_g, wout_g), c_all = _rope_tables_and_gather(
        positions.reshape(s, 1).astype(F32), inv_freq, tm,
        [w_in[0].astype(BF16), w_o_swa[0].T.astype(BF16), w_o_fox[0].T.astype(BF16), w_out[0].astype(BF16)],
        jnp.broadcast_to(c, (8, D_MODEL)))
    c_all = c_all[:, 0, :]
    b_shard = lax.dynamic_slice(b_ada, (0, me * ADA_SHARD), (1, ADA_SHARD))
    ada_part = _ada_fwd(c_all, w_ada[0], b_shard)
    (ada_g,) = _exchange([ada_part], ["gather"], "gather_ada")
    ada = lax.dynamic_index_in_dim(ada_g, me, axis=1, keepdims=False).reshape(3 * D_MODEL)
    shift, scale, gate = ada[:D_MODEL], ada[D_MODEL:2 * D_MODEL], ada[2 * D_MODEL:]
    woa_t = woa_g.reshape(D_MODEL, 512)
    wob_t = wob_g.reshape(D_MODEL, 512)
    wout = wout_g.reshape(D_MODEL, D_MODEL)
    w_all = _assemble_w_all(win_g)

    zrow = jnp.zeros((1, D_MODEL), F32)
    mod = jnp.concatenate([g_norm, scale[None], shift[None], zrow, zrow, zrow, zrow, zrow], axis=0)
    bf_row = jnp.pad(b_f, ((0, 0), (0, CHUNK - N_HEADS)))
    emat_np = np.zeros((3 * CHUNK, AUG_W), np.float32)
    ind_np = np.zeros((512, CHUNK), np.float32)
    for hd in range(N_HEADS):
        for part in range(3):
            emat_np[CHUNK * part + hd, CHUNK * hd + 64 + part] = 1.0
        ind_np[HEAD_DIM * hd:HEAD_DIM * (hd + 1), hd] = 1.0
    emat, ind = jnp.asarray(emat_np, BF16), jnp.asarray(ind_np, BF16)
    sink_rows = jnp.broadcast_to(jnp.pad(sinks.reshape(KV_GROUPS, GROUP), ((0, 0), (0, 8 - GROUP)))[:, :, None],
                                 (KV_GROUPS, 8, CHUNK))

    (h_t, qa, ka, va, za, qb, kb, vb, zb, ga, gb, xf, bounds, norms) = _fwd_proj(
        x2, mod, w_all, cos, sa, sb, bf_row, emat, ind, tm)
    first_key, last_query = _fox_block_ranges(bounds, norms, ta // tm)
    att_a, qa_bwd = _swa_fwd(qa, ka, va, sink_rows, ta)
    att_b, qb_bwd = _fox_fwd(first_key, qb, kb, vb, ta)
    vec = jnp.concatenate([gate[None], g_final[None], zrow, zrow, zrow, zrow, zrow, zrow], axis=0)
    (dx1, ua_t, ub_t, merged_t, d_o, dya, dyb, doa, dob, dzg, post_acc) = _post(
        x2, tgt, att_a, att_b, za, zb, ga, gb, woa_t, wob_t, wout, vec, ind, emat, tm)

    da, dsink = _swa_bwd(ka, va, qa_bwd, doa, cos, sa, sb, sink_rows, ta)
    dkb, dvb, dck, dqb, dcq = _fox_bwd(last_query, kb, vb, qb_bwd, dob, ta)
    dfb, dbf_acc = _fgate_bwd(dcq, dck, xf, ta)
    dproj = [da, dqb, dkb, dvb, dzg, dfb]
    dw_zg, = _token_matmul_multi(h_t, [dzg], ta, "dw_in_zg")
    dw_a, dw_qb, dw_kb, dw_vb, dw_f = _token_matmul_multi(h_t, [da, dqb, dkb, dvb, dfb], ta, "dw_in_rest")
    dw_all = [dw_a, dw_qb, dw_kb, dw_vb, dw_zg, dw_f]

    core, chip = lax.axis_index("c"), 2 * lax.axis_index("x") + lax.axis_index("y")
    win_slots = jnp.stack([jnp.stack([_grad_slot(dw_all, 2 * ch + co) for ch in range(N_CHIPS)]) for co in range(2)])
    (dwoa, dwob, dwout), (win_theirs,) = _token_matmuls_and_swap(
        [(ua_t, dya), (ub_t, dyb), (merged_t, d_o)], ta, [win_slots])
    win_full, win_half = _chip_partial(jnp.reshape(core, (1,)).astype(jnp.int32), win_slots, win_theirs, 256,
                                       "chip_partial_w_in")
    col_slots = lambda g: g.reshape(512, N_DEV, 128).transpose(1, 0, 2)
    grad_x, pre_acc, (p_win, p_woa, p_wob, p_wout) = _bwd_pre(
        dproj, w_all, x2, dx1, mod, tm,
        [win_half, col_slots(dwoa), col_slots(dwob), dwout.reshape(N_DEV, 128, D_MODEL)],
        ["chips", "devices", "devices", "devices"])
    misc = jnp.concatenate([dbf_acc[0, :N_HEADS], dsink[:, :GROUP, 0].reshape(N_HEADS), jnp.zeros((D_MODEL - 16,), F32)])
    zpad = jnp.zeros((2 * D_MODEL,), F32)
    pack = jnp.stack([jnp.concatenate([pre_acc[0], pre_acc[1], post_acc[0]]),
                      jnp.concatenate([pre_acc[2], post_acc[1], misc]),
                      jnp.concatenate([post_acc[2], zpad])] + [jnp.zeros((3 * D_MODEL,), F32)] * 5)
    (packs,) = _exchange([pack], ["gather"], "gather_small_grads")
    others = jnp.where(jnp.arange(N_CHIPS) == chip, 0.0, 1.0).astype(F32)

    g_win, d_win, m_win, v_win = _adamw_parts(win_full, p_win, others, w_in, m_w_in, v_w_in, 128, "adamw_w_in")
    g_woa, d_woa, m_woa, v_woa = _adamw_devices(p_woa, w_o_swa, m_w_o_swa, v_w_o_swa, 128, "adamw_w_o_swa")
    g_wob, d_wob, m_wob, v_wob = _adamw_devices(p_wob, w_o_fox, m_w_o_fox, v_w_o_fox, 128, "adamw_w_o_fox")
    g_wout, d_wout, m_wout, v_wout = _adamw_devices(p_wout, w_out, m_w_out, v_w_out, 128, "adamw_w_out")
    d_ada_rows = lax.dynamic_slice(packs[:, 0, :], (0, me * ADA_SHARD), (N_DEV, ADA_SHARD))
    c_lanes = jnp.broadcast_to(c_all[:, :, None], (N_DEV, D_MODEL, CHUNK))
    g_wada, d_wada, m_wada, v_wada = _adamw_ada(c_lanes, d_ada_rows, w_ada, m_w_ada, v_w_ada, 256)

    def small_pack(bada, gn, gf, bfv, sk):
        misc_w = jnp.concatenate([bfv[0], sk[0], jnp.zeros((D_MODEL - 16,), F32)])
        return jnp.stack([bada[0], jnp.concatenate([gn[0], gf, misc_w])] + [jnp.zeros((3 * D_MODEL,), F32)] * 6)

    sm = _adamw_small(packs, small_pack(b_ada, g_norm, g_final, b_f, sinks),
                      small_pack(m_b_ada, m_g_norm, m_g_final, m_b_f, m_sinks),
                      small_pack(v_b_ada, v_g_norm, v_g_final, v_b_f, v_sinks))
    loss = sm[4][0, 0]

    def unpack(p):
        return (p[0][None], p[1, :D_MODEL][None], p[1, 2 * D_MODEL:2 * D_MODEL + 8][None],
                p[1, 2 * D_MODEL + 8:2 * D_MODEL + 16][None], p[1, D_MODEL:2 * D_MODEL])

    outs = []
    for big, small in (((g_wada, g_win, g_woa, g_wob, g_wout), sm[0]), ((d_wada, d_win, d_woa, d_wob, d_wout), sm[1]),
                       ((m_wada, m_win, m_woa, m_wob, m_wout), sm[2]), ((v_wada, v_win, v_woa, v_wob, v_wout), sm[3])):
        wada_o, win_o, woa_o, wob_o, wout_o = big
        bada_o, gn_o, bf_o, sk_o, gf_o = unpack(small)
        outs += [wada_o, bada_o, gn_o, win_o, bf_o, sk_o, woa_o, wob_o, wout_o, gf_o]
    return (loss, grad_x[None], *outs)
```

```python
import numpy as np
import jax
import jax.numpy as jnp
from jax import lax
from jax.experimental import pallas as pl
from jax.experimental.pallas import tpu as pltpu

F32 = jnp.float32
BF16 = jnp.bfloat16

D_MODEL = 1024
HEAD_DIM = 64
N_HEADS = 8
KV_GROUPS = 2
GROUP = N_HEADS // KV_GROUPS
WINDOW = 128
CHUNK = 128
AUG_W = N_HEADS * CHUNK
N_DEV = 8
IN_SHARD = 673
ADA_SHARD = 384
NORM_EPS = 1e-6
ROPE_THETA = 10000.0
Q_SCALE = HEAD_DIM ** -0.5
NEG = -1e30
UNDERFLOW = 105.0

ADAM_LR = 0.001
ADAM_B1 = 0.9
ADAM_B2 = 0.999
ADAM_EPS = 1e-08
ADAM_WD = 0.01
ADAM_STEP = 10
ADAM_C1 = 1.0 - ADAM_B1 ** ADAM_STEP
ADAM_C2 = 1.0 - ADAM_B2 ** ADAM_STEP

SEG = {}
_off = 0
for _name, _w in (("qa", 512), ("ka", 256), ("va", 256), ("qb", 512), ("kb", 512), ("vb", 512),
                  ("za", 512), ("zb", 512), ("ga", 1024), ("gb", 1024), ("f", 128)):
    SEG[_name] = (_off, _w)
    _off += _w
PROJ_W = _off
DPROJ_GROUPS = (("a", SEG["qa"][0], 1024), ("qb", SEG["qb"][0], 512), ("kb", SEG["kb"][0], 512),
                ("vb", SEG["vb"][0], 512), ("zg", SEG["za"][0], 3072), ("f", SEG["f"][0], 128))

VMEM_LIMIT = 56 * 1024 * 1024


def _params(n_axes):
    return pltpu.CompilerParams(dimension_semantics=("arbitrary",) * n_axes, vmem_limit_bytes=VMEM_LIMIT)


def _dot(a, b):
    return jnp.dot(a, b, preferred_element_type=F32)


def _dot_nt(a, b):
    return lax.dot_general(a, b, (((1,), (1,)), ((), ())), preferred_element_type=F32)


def _dot_tn(a, b):
    return lax.dot_general(a, b, (((0,), (0,)), ((), ())), preferred_element_type=F32)


def _lane(n):
    return lax.broadcasted_iota(jnp.int32, (1, n), 1)


def _split3(x):
    hi = x.astype(BF16).astype(F32)
    r = x - hi
    mid = r.astype(BF16).astype(F32)
    lo = (r - mid).astype(BF16).astype(F32)
    return hi, mid, lo


def _dot_split(x, b):
    hi, mid, lo = _split3(x)
    return _dot(hi.astype(BF16), b) + _dot(mid.astype(BF16), b) + _dot(lo.astype(BF16), b)


def _spread3(x, e3_ref):
    return _dot(jnp.concatenate(_split3(x), axis=1).astype(BF16), e3_ref[...])


def _place3(lane, base, x, other):
    hi, mid, lo = _split3(x)
    return jnp.where(lane == base, hi, jnp.where(lane == base + 1, mid, jnp.where(lane == base + 2, lo, other)))


def _lane_sum(x, lo, hi):
    lane = _lane(x.shape[1])
    return jnp.sum(jnp.where((lane >= lo) & (lane < hi), x, 0.0), axis=1, keepdims=True)


def _ones_lanes(lo, hi):
    lane = _lane(CHUNK)
    return jnp.where((lane >= lo) & (lane < hi), 1.0, 0.0).astype(F32)


def _rope(c, cos, sa, sb):
    return c * cos + pltpu.roll(c, CHUNK - 32, 1) * sa + pltpu.roll(c, 32, 1) * sb


def _rope_inv(d, cos, sa, sb):
    return d * cos - (pltpu.roll(d, CHUNK - 32, 1) * sa + pltpu.roll(d, 32, 1) * sb)


def _pair(c0, c1):
    return jnp.where(_lane(CHUNK) < HEAD_DIM, c0, pltpu.roll(c1, HEAD_DIM, 1))


def _sigmoid(x):
    return 0.5 * jnp.tanh(0.5 * x) + 0.5


N_CHIPS = 4
ANY_SPEC = pl.BlockSpec(memory_space=pl.ANY)


def _exchange(arrays, modes, name):
    n = len(arrays)

    def body(*refs):
        copies = _exchange_copies(refs[:n], refs[n:2 * n], modes, *refs[2 * n:])
        _exchange_start(copies)
        _exchange_wait(copies)

    return pl.pallas_call(
        body, name=name, out_shape=_exchange_out_shapes(arrays, modes),
        in_specs=[ANY_SPEC] * n, out_specs=[ANY_SPEC] * n, scratch_shapes=_exchange_sems(n),
    )(*arrays)


def _exchange_out_shapes(arrays, modes):
    return [jax.ShapeDtypeStruct((N_DEV,) + a.shape if md == "gather" else a.shape, a.dtype)
            for a, md in zip(arrays, modes)]


def _exchange_sems(n):
    return [pltpu.SemaphoreType.DMA((n, N_DEV - 1)), pltpu.SemaphoreType.DMA((n, N_DEV - 1)),
            pltpu.SemaphoreType.DMA((n,))]


def _exchange_copies(srcs, dsts, modes, send_sems, recv_sems, loc_sems):
    n = len(srcs)
    x, y, c = lax.axis_index("x"), lax.axis_index("y"), lax.axis_index("c")

    def slot(i, px, py, pc):
        return 2 * px + py if modes[i] == "chips" else 4 * px + 2 * py + pc

    def src_of(i, px, py, pc):
        return srcs[i] if modes[i] == "gather" else srcs[i].at[slot(i, px, py, pc)]

    local = [pltpu.make_async_copy(src_of(i, x, y, c), dsts[i].at[slot(i, x, y, c)], loc_sems.at[i])
             for i in range(n)]
    sends, recvs = [], []
    for r in (1, 2, 4, 3, 5, 6, 7):
        px = 1 - x if r & 4 else x
        py = 1 - y if r & 2 else y
        pc = 1 - c if r & 1 else c
        for i in range(n):
            if modes[i] == "chips" and r & 1:
                continue
            sems = dict(send_sem=send_sems.at[i, r - 1], recv_sem=recv_sems.at[i, r - 1],
                        device_id=(px, py, pc), device_id_type=pl.DeviceIdType.MESH)
            sends.append(pltpu.make_async_remote_copy(
                src_ref=src_of(i, px, py, pc), dst_ref=dsts[i].at[slot(i, x, y, c)], **sems))
            recvs.append(pltpu.make_async_remote_copy(
                src_ref=src_of(i, px, py, pc), dst_ref=dsts[i].at[slot(i, px, py, pc)], **sems))
    return local, sends, recvs


def _exchange_start(copies):
    local, sends, _ = copies
    for cp in local + sends:
        cp.start()


def _exchange_wait(copies):
    local, sends, recvs = copies
    for cp in recvs:
        cp.wait_recv()
    for cp in sends:
        cp.wait_send()
    for cp in local:
        cp.wait()


def _gather2_parts(srcs, dsts, send_sems, recv_sems, loc_sems):
    n = len(srcs)
    x, y, c = lax.axis_index("x"), lax.axis_index("y"), lax.axis_index("c")
    me, sibling = (x, y, c), (x, y, 1 - c)
    chips = [(1 - x, y), (x, 1 - y), (1 - x, 1 - y)]

    def rows(i, dev):
        return dsts[i].at[4 * dev[0] + 2 * dev[1] + dev[2]]

    def copy(i, k, block, to, src=None):
        return pltpu.make_async_remote_copy(
            src_ref=rows(i, block) if src is None else src, dst_ref=rows(i, block),
            send_sem=send_sems.at[i, k], recv_sem=recv_sems.at[i, k],
            device_id=to, device_id_type=pl.DeviceIdType.MESH)

    local = [pltpu.make_async_copy(srcs[i], rows(i, me), loc_sems.at[i]) for i in range(n)]
    first = []
    for i in range(n):
        first.append(copy(i, 0, me, sibling, src=srcs[i]))
        first += [copy(i, 1 + j, me, (*chip, c), src=srcs[i]) for j, chip in enumerate(chips)]
    return n, c, me, sibling, chips, copy, local, first


def _gather2_start(srcs, dsts, send_sems, recv_sems, loc_sems):
    *_, local, first = _gather2_parts(srcs, dsts, send_sems, recv_sems, loc_sems)
    for cp in local + first:
        cp.start()


def _gather2_finish(srcs, dsts, send_sems, recv_sems, loc_sems):
    n, c, me, sibling, chips, copy, local, first = _gather2_parts(srcs, dsts, send_sems, recv_sems, loc_sems)
    passed = []
    for j, chip in enumerate(chips):
        for i in range(n):
            copy(i, 1 + j, (*chip, c), me).wait_recv()
            cp = copy(i, 4 + j, (*chip, c), sibling)
            cp.start()
            passed.append(cp)
    for i in range(n):
        copy(i, 0, sibling, me).wait_recv()
        for j, chip in enumerate(chips):
            copy(i, 4 + j, (*chip, 1 - c), me).wait_recv()
    for cp in first + passed:
        cp.wait_send()
    for cp in local:
        cp.wait()


def _sibling_copies(srcs, dsts, send_sems, recv_sems):
    x, y, c = lax.axis_index("x"), lax.axis_index("y"), lax.axis_index("c")
    return [pltpu.make_async_remote_copy(
        src_ref=srcs[i].at[1 - c, k], dst_ref=dsts[i].at[k], send_sem=send_sems.at[i, k], recv_sem=recv_sems.at[i, k],
        device_id=(x, y, 1 - c), device_id_type=pl.DeviceIdType.MESH)
        for i in range(len(srcs)) for k in range(N_CHIPS)]


def _token_matmuls_and_swap(pairs, tk, swap):
    npair, nsw = len(pairs), len(swap)
    s = pairs[0][0].shape[1]
    last = s // tk - 1

    def body(*refs):
        a_refs, b_refs = refs[0:2 * npair:2], refs[1:2 * npair:2]
        src = refs[2 * npair:2 * npair + nsw]
        o_refs = refs[2 * npair + nsw:3 * npair + nsw]
        dst = refs[3 * npair + nsw:3 * npair + 2 * nsw]
        sems = refs[3 * npair + 2 * nsw:3 * npair + 2 * nsw + 2]
        acc_refs = refs[3 * npair + 2 * nsw + 2:]

        @pl.when(pl.program_id(0) == 0)
        def _():
            for cp in _sibling_copies(src, dst, *sems):
                cp.start()

        for a_ref, b_ref, o_ref, acc_ref in zip(a_refs, b_refs, o_refs, acc_refs):
            _accumulate_tokens(a_ref[...], b_ref, acc_ref, o_ref, last)

        @pl.when(pl.program_id(0) == last)
        def _():
            for cp in _sibling_copies(src, dst, *sems):
                cp.wait()

    in_specs, args = [], []
    for a_t, b in pairs:
        in_specs += [pl.BlockSpec((a_t.shape[0], tk), lambda k: (0, k)), pl.BlockSpec((tk, b.shape[1]), lambda k: (k, 0))]
        args += [a_t, b]
    outs = pl.pallas_call(
        body, name="dw_out_projs_swap", grid=(s // tk,),
        out_shape=[jax.ShapeDtypeStruct((a_t.shape[0], b.shape[1]), BF16) for a_t, b in pairs]
        + [jax.ShapeDtypeStruct(a.shape[1:], a.dtype) for a in swap],
        in_specs=in_specs + [ANY_SPEC] * nsw,
        out_specs=[pl.BlockSpec((a_t.shape[0], b.shape[1]), lambda k: (0, 0)) for a_t, b in pairs] + [ANY_SPEC] * nsw,
        scratch_shapes=[pltpu.SemaphoreType.DMA((nsw, N_CHIPS)), pltpu.SemaphoreType.DMA((nsw, N_CHIPS))]
        + [pltpu.VMEM((a_t.shape[0], b.shape[1]), F32) for a_t, b in pairs],
        compiler_params=_params(1))(*args, *swap)
    return outs[:npair], outs[npair:]


def _chip_partial(core, slots, theirs, tr, name):
    _, k, rws, cols = slots.shape

    def body(core_ref, a_ref, b_ref, f_ref, h_ref):
        sm = a_ref[0].astype(F32) + b_ref[...].astype(F32)
        f_ref[...] = sm
        h_ref[...] = sm.astype(BF16)

    blk = pl.BlockSpec((1, tr, cols), lambda j, i, core_ref: (j, i, 0))
    mine = pl.BlockSpec((1, 1, tr, cols), lambda j, i, core_ref: (core_ref[0], j, i, 0))
    return pl.pallas_call(
        body, name=name,
        grid_spec=pltpu.PrefetchScalarGridSpec(num_scalar_prefetch=1, grid=(k, rws // tr),
                                               in_specs=[mine, blk], out_specs=[blk, blk]),
        out_shape=[jax.ShapeDtypeStruct(theirs.shape, F32), jax.ShapeDtypeStruct(theirs.shape, BF16)],
        compiler_params=_params(2))(core, slots, theirs)


def _ada_fwd(c_all, w_shard, b_shard):
    def body(c_ref, w_ref, b_ref, o_ref):
        ch, cm, cl = [t.astype(BF16) for t in _split3(c_ref[...])]
        wh, wm, wl = [t.astype(BF16) for t in _split3(w_ref[...])]
        acc = _dot(ch, wh) + _dot(ch, wm) + _dot(cm, wh) + _dot(ch, wl) + _dot(cl, wh) + _dot(cm, wm)
        o_ref[...] = acc + b_ref[...]

    return pl.pallas_call(body, name="ada_fwd", out_shape=jax.ShapeDtypeStruct((N_DEV, ADA_SHARD), F32),
                          compiler_params=_params(0))(c_all, w_shard, b_shard)


def _rope_tables_and_gather(pos_col, inv_freq, tm, weights, c_rows):
    s = pos_col.shape[0]
    nw = len(weights)
    last = s // tm - 1

    def body(*refs):
        p_ref, f_ref = refs[:2]
        w_src, c_src = refs[2:2 + nw], refs[2 + nw:3 + nw]
        cos_ref, sa_ref, sb_ref = refs[3 + nw:6 + nw]
        w_dst, c_dst = refs[6 + nw:6 + 2 * nw], refs[6 + 2 * nw:7 + 2 * nw]
        w_sems, c_sems = refs[7 + 2 * nw:10 + 2 * nw], refs[10 + 2 * nw:]

        @pl.when(pl.program_id(0) == 0)
        def _():
            _gather2_start(w_src, w_dst, *w_sems)
            _exchange_start(_exchange_copies(c_src, c_dst, ["gather"], *c_sems))

        ang = p_ref[...] * f_ref[...]
        sin = jnp.sin(ang)
        first_half = (_lane(CHUNK) & (HEAD_DIM - 1)) < HEAD_DIM // 2
        cos_ref[...] = jnp.cos(ang)
        sa_ref[...] = jnp.where(first_half, -sin, 0.0)
        sb_ref[...] = jnp.where(first_half, 0.0, sin)

        @pl.when(pl.program_id(0) == last)
        def _():
            _exchange_wait(_exchange_copies(c_src, c_dst, ["gather"], *c_sems))
            _gather2_finish(w_src, w_dst, *w_sems)

    tab = jax.ShapeDtypeStruct((s, CHUNK), F32)
    blk = pl.BlockSpec((tm, CHUNK), lambda i: (i, 0))
    gathered = [jax.ShapeDtypeStruct((N_DEV,) + a.shape, a.dtype) for a in list(weights) + [c_rows]]
    outs = pl.pallas_call(
        body, name="rope_tables_gather", grid=(s // tm,), out_shape=[tab, tab, tab] + gathered,
        in_specs=[pl.BlockSpec((tm, 1), lambda i: (i, 0)), pl.BlockSpec((1, CHUNK), lambda i: (0, 0))]
        + [ANY_SPEC] * (nw + 1),
        out_specs=[blk, blk, blk] + [ANY_SPEC] * (nw + 1),
        scratch_shapes=_exchange_sems(nw) + _exchange_sems(1),
        compiler_params=_params(1))(pos_col, inv_freq, *weights, c_rows)
    return outs[:3], outs[3:3 + nw], outs[3 + nw]


def _fwd_proj(x, mod, w_all, cos, sa, sb, bf_row, emat, ind, tm):
    s = x.shape[0]

    def body(x_ref, mod_ref, w_ref, cos_ref, sa_ref, sb_ref, bf_ref, e_ref, ind_ref,
             h_ref, qa_ref, ka_ref, va_ref, za_ref, qb_ref, kb_ref, vb_ref, zb_ref, ga_ref, gb_ref, f_ref,
             bnd_ref, nrm_ref, carry_ref):
        @pl.when(pl.program_id(0) == 0)
        def _():
            carry_ref[...] = jnp.zeros_like(carry_ref)

        xv = x_ref[...]
        rstd = lax.rsqrt(jnp.mean(xv * xv, axis=-1, keepdims=True) + NORM_EPS)
        h = (xv * rstd * mod_ref[0:1, :]) * (1.0 + mod_ref[1:2, :]) + mod_ref[2:3, :]
        hb = h.astype(BF16)
        h_ref[...] = hb.T

        def seg(name):
            off, w = SEG[name]
            return _dot(hb, w_ref[:, off:off + w])

        lane = _lane(CHUNK)
        low = lane < HEAD_DIM
        q_ones = _ones_lanes(64, 67)
        k_ones = _ones_lanes(67, 70)
        cos_t, sa_t, sb_t = cos_ref[...], sa_ref[...], sb_ref[...]

        def write_heads(ref, nat, extra_of, rope, scale):
            for p in range(N_HEADS // 2):
                c = nat[:, CHUNK * p:CHUNK * (p + 1)]
                if rope:
                    c = _rope(c, cos_t, sa_t, sb_t)
                if scale != 1.0:
                    c = c * scale
                for hh in range(2):
                    hd = 2 * p + hh
                    src = c if hh == 0 else pltpu.roll(c, HEAD_DIM, 1)
                    ref[:, CHUNK * hd:CHUNK * (hd + 1)] = jnp.where(low, src, extra_of(hd)).astype(BF16)

        write_heads(qa_ref, seg("qa"), lambda hd: q_ones, True, Q_SCALE)
        ka = seg("ka")
        va = seg("va")
        for kv in range(KV_GROUPS):
            sl = slice(CHUNK * kv, CHUNK * (kv + 1))
            ka_ref[:, sl] = jnp.where(low, _rope(ka[:, sl], cos_t, sa_t, sb_t), k_ones).astype(BF16)
            va_ref[:, sl] = jnp.where(low, va[:, sl], q_ones).astype(BF16)
        za_ref[...] = seg("za")
        zb_ref[...] = seg("zb")
        ga_ref[...] = seg("ga")
        gb_ref[...] = seg("gb")

        xf = seg("f") + bf_ref[...]
        f_ref[...] = xf
        logf = jnp.minimum(xf, 0.0) - jnp.log1p(jnp.exp(-jnp.abs(xf)))
        row = lax.broadcasted_iota(jnp.int32, (tm, tm), 0)
        col = lax.broadcasted_iota(jnp.int32, (tm, tm), 1)
        tri = jnp.where(col <= row, 1.0, 0.0).astype(BF16)
        hi, mid, lo = _split3(logf)
        cum = _dot(tri, hi.astype(BF16)) + _dot(tri, mid.astype(BF16)) + _dot(tri, lo.astype(BF16))
        cum = cum + carry_ref[0:1, :]
        carry_ref[0:1, :] += jnp.sum(logf, axis=0, keepdims=True)
        lane_all = _lane(AUG_W) & (CHUNK - 1)
        bias = _spread3(-cum, e_ref) + jnp.where((lane_all >= 67) & (lane_all < 70), 1.0, 0.0)

        qb, kb = seg("qb"), seg("kb")
        write_heads(qb_ref, qb, lambda hd: q_ones, False, Q_SCALE)
        write_heads(kb_ref, kb, lambda hd: bias[:, CHUNK * hd:CHUNK * (hd + 1)], False, 1.0)
        write_heads(vb_ref, seg("vb"), lambda hd: q_ones, False, 1.0)

        rid = lax.broadcasted_iota(jnp.int32, (tm, CHUNK), 0)
        bnd_ref[...] = jnp.zeros_like(bnd_ref)
        bnd_ref[0, 0:1, :] = jnp.sum(jnp.where(rid == 0, cum, 0.0), axis=0, keepdims=True)
        bnd_ref[0, 1:2, :] = jnp.sum(jnp.where(rid == tm - 1, cum, 0.0), axis=0, keepdims=True)

        @pl.when(pl.program_id(0) == 0)
        def _():
            nrm_ref[...] = jnp.zeros_like(nrm_ref)

        for r_, nat in ((0, qb * Q_SCALE), (1, kb)):
            sq = _dot((nat * nat).astype(BF16), ind_ref[...])
            nrm_ref[r_:r_ + 1, :] = jnp.maximum(nrm_ref[r_:r_ + 1, :], jnp.max(sq, axis=0, keepdims=True))

    row_blk = lambda w: pl.BlockSpec((tm, w), lambda i: (i, 0))
    full = lambda a: pl.BlockSpec(a.shape, lambda i: (0,) * a.ndim)
    sds = lambda w, dt: jax.ShapeDtypeStruct((s, w), dt)
    out_shape = [sds(AUG_W, BF16), sds(KV_GROUPS * CHUNK, BF16), sds(KV_GROUPS * CHUNK, BF16),
                 sds(512, F32), sds(AUG_W, BF16), sds(AUG_W, BF16), sds(AUG_W, BF16), sds(512, F32),
                 sds(D_MODEL, F32), sds(D_MODEL, F32), sds(CHUNK, F32)]
    small = [jax.ShapeDtypeStruct((s // tm, 8, CHUNK), F32), jax.ShapeDtypeStruct((8, CHUNK), F32)]
    return pl.pallas_call(
        body, name="fwd_proj", grid=(s // tm,),
        out_shape=[jax.ShapeDtypeStruct((D_MODEL, s), BF16)] + out_shape + small,
        in_specs=[row_blk(D_MODEL), full(mod), full(w_all), row_blk(CHUNK), row_blk(CHUNK), row_blk(CHUNK),
                  full(bf_row), full(emat), full(ind)],
        out_specs=[pl.BlockSpec((D_MODEL, tm), lambda i: (0, i))] + [row_blk(o.shape[1]) for o in out_shape] + [
            pl.BlockSpec((1, 8, CHUNK), lambda i: (i, 0, 0)), pl.BlockSpec((8, CHUNK), lambda i: (0, 0))],
        scratch_shapes=[pltpu.VMEM((8, CHUNK), F32)],
        compiler_params=_params(1))(x, mod, w_all, cos, sa, sb, bf_row, emat, ind)


def _swa_fwd(q_aug, k_aug, v_aug, sink_rows, tq):
    s = q_aug.shape[0]
    r = tq // WINDOW
    gw = GROUP * CHUNK

    def body(q_ref, kc_ref, kp_ref, vc_ref, vp_ref, sink_ref, o_ref, qb_ref):
        i = pl.program_id(1)
        st = GROUP * WINDOW
        qloc = lax.broadcasted_iota(jnp.int32, (st, 2 * WINDOW), 0) & (WINDOW - 1)
        col = lax.broadcasted_iota(jnp.int32, (st, 2 * WINDOW), 1)
        band = (col > qloc) & (col <= qloc + WINDOW)
        head = jnp.right_shift(lax.broadcasted_iota(jnp.int32, (st, 1), 0), 7)
        sink = jnp.zeros((st, 1), F32)
        for g in range(GROUP):
            sink = jnp.where(head == g, jnp.max(sink_ref[0, g:g + 1, :], axis=1, keepdims=True), sink)
        lane = _lane(CHUNK)
        for sub in range(r):
            rows = slice(WINDOW * sub, WINDOW * (sub + 1))
            q = jnp.concatenate([q_ref[rows, CHUNK * g:CHUNK * (g + 1)] for g in range(GROUP)], axis=0)
            if sub == 0:
                k = jnp.concatenate([kp_ref[...], kc_ref[rows, :]], axis=0)
                v = jnp.concatenate([vp_ref[...], vc_ref[rows, :]], axis=0)
                valid = band & ((col >= WINDOW) | (i > 0))
            else:
                both = slice(WINDOW * (sub - 1), WINDOW * (sub + 1))
                k, v, valid = kc_ref[both, :], vc_ref[both, :], band
            sc = jnp.where(valid, _dot_nt(q, k), NEG)
            m = jnp.maximum(jnp.max(sc, axis=1, keepdims=True), sink)
            acc = _dot(jnp.exp(sc - m).astype(BF16), v)
            denom = _lane_sum(acc, 64, 65) + jnp.exp(sink - m)
            out = acc / denom
            aug = _place3(lane, 67, -(m + jnp.log(denom)), q.astype(F32)).astype(BF16)
            hrows = lambda a, g: a[WINDOW * g:WINDOW * (g + 1), :]
            for g in range(GROUP):
                qb_ref[rows, CHUNK * g:CHUNK * (g + 1)] = hrows(aug, g)
            for pp in range(GROUP // 2):
                o_ref[rows, CHUNK * pp:CHUNK * (pp + 1)] = _pair(hrows(out, 2 * pp), hrows(out, 2 * pp + 1))

    return pl.pallas_call(
        body, name="swa_fwd", grid=(KV_GROUPS, s // tq),
        out_shape=[jax.ShapeDtypeStruct((s, 512), F32), jax.ShapeDtypeStruct((s, AUG_W), BF16)],
        in_specs=[pl.BlockSpec((tq, gw), lambda kv, i: (i, kv)),
                  pl.BlockSpec((tq, CHUNK), lambda kv, i: (i, kv)),
                  pl.BlockSpec((WINDOW, CHUNK), lambda kv, i: (jnp.maximum(i * r - 1, 0), kv)),
                  pl.BlockSpec((tq, CHUNK), lambda kv, i: (i, kv)),
                  pl.BlockSpec((WINDOW, CHUNK), lambda kv, i: (jnp.maximum(i * r - 1, 0), kv)),
                  pl.BlockSpec((1, 8, CHUNK), lambda kv, i: (kv, 0, 0))],
        out_specs=[pl.BlockSpec((tq, GROUP * HEAD_DIM), lambda kv, i: (i, kv)),
                   pl.BlockSpec((tq, gw), lambda kv, i: (i, kv))],
        compiler_params=_params(2))(q_aug, k_aug, k_aug, v_aug, v_aug, sink_rows)


def _swa_bwd(k_aug, v_aug, q_bwd, do_aug, cos, sa, sb, sink_rows, tk):
    s = k_aug.shape[0]
    r = tk // WINDOW
    nt = s // tk
    nb = s // WINDOW
    gw = GROUP * CHUNK

    def body(k_ref, v_ref, q_ref, qn_ref, do_ref, don_ref, cos_ref, sa_ref, sb_ref, sink_ref,
             da_ref, dsink_ref, carry_ref, acc_ref):
        j = pl.program_id(0)

        @pl.when(j == 0)
        def _():
            carry_ref[...] = jnp.zeros_like(carry_ref)
            dsink_ref[...] = jnp.zeros_like(dsink_ref)

        acc_ref[...] = jnp.zeros_like(acc_ref)
        acc_ref[0:WINDOW, :] = carry_ref[...]
        dk_col, dv_col = SEG["ka"][0] - SEG["qa"][0], SEG["va"][0] - SEG["qa"][0]
        st = GROUP * WINDOW
        key = lax.broadcasted_iota(jnp.int32, (WINDOW, 2 * st), 0)
        col = lax.broadcasted_iota(jnp.int32, (WINDOW, 2 * st), 1)
        qloc = col & (WINDOW - 1)
        band = ((col < st) & (key <= qloc)) | ((col >= st) & (key > qloc))
        lane = _lane(CHUNK)
        low = lane < HEAD_DIM
        cos_t, sa_t, sb_t = cos_ref[...], sa_ref[...], sb_ref[...]
        for kv, sub in [(kv, sub) for kv in range(KV_GROUPS) for sub in range(r)]:
            heads = [slice(CHUNK * (GROUP * kv + g), CHUNK * (GROUP * kv + g + 1)) for g in range(GROUP)]
            kvs = slice(CHUNK * kv, CHUNK * (kv + 1))
            rows = slice(WINDOW * sub, WINDOW * (sub + 1))
            nxt = slice(WINDOW * (sub + 1), WINDOW * (sub + 2))
            k, v = k_ref[rows, kvs], v_ref[rows, kvs]
            q_cur, do_cur = [q_ref[rows, cs] for cs in heads], [do_ref[rows, cs] for cs in heads]
            if sub < r - 1:
                q_nxt, do_nxt, valid = [q_ref[nxt, cs] for cs in heads], [do_ref[nxt, cs] for cs in heads], band
            else:
                q_nxt, do_nxt = [qn_ref[:, cs] for cs in heads], [don_ref[:, cs] for cs in heads]
                valid = band & ((col < st) | (j < nt - 1))
            q = jnp.concatenate(q_cur + q_nxt, axis=0)
            do = jnp.concatenate(do_cur + do_nxt, axis=0)
            pt = jnp.exp(jnp.where(valid, _dot_nt(k, q), NEG))
            ds = (pt * _dot_nt(v, do)).astype(BF16)
            dv = _dot(pt.astype(BF16), do)
            dk = _dot(ds, q)
            dq = _dot_tn(ds, k)
            for g, cs in enumerate(heads):
                acc_ref[rows, cs] += dq[WINDOW * g:WINDOW * (g + 1), :]
                dqn = dq[st + WINDOW * g:st + WINDOW * (g + 1), :]
                if sub < r - 1:
                    acc_ref[nxt, cs] += dqn
                else:
                    carry_ref[:, cs] = dqn
                sink = jnp.max(sink_ref[kv, g:g + 1, :], axis=1, keepdims=True)
                p_sink = jnp.exp(sink + _lane_sum(q_cur[g].astype(F32), 67, 70))
                term = jnp.sum(p_sink * _lane_sum(do_cur[g].astype(F32), 64, 67), axis=0, keepdims=True)
                dsink_ref[kv, g:g + 1, :] += jnp.broadcast_to(term, (1, CHUNK))
            da_ref[rows, dk_col + CHUNK * kv:dk_col + CHUNK * (kv + 1)] = _rope_inv(
                jnp.where(low, dk, 0.0), cos_t[rows, :], sa_t[rows, :], sb_t[rows, :]).astype(BF16)
            da_ref[rows, dv_col + CHUNK * kv:dv_col + CHUNK * (kv + 1)] = jnp.where(low, dv, 0.0).astype(BF16)
        for pp in range(N_HEADS // 2):
            d = _pair(acc_ref[:, CHUNK * 2 * pp:CHUNK * (2 * pp + 1)], acc_ref[:, CHUNK * (2 * pp + 1):CHUNK * (2 * pp + 2)])
            da_ref[:, CHUNK * pp:CHUNK * (pp + 1)] = (_rope_inv(d, cos_t, sa_t, sb_t) * Q_SCALE).astype(BF16)

    cur = lambda w: pl.BlockSpec((tk, w), lambda j: (j, 0))
    nxt = pl.BlockSpec((WINDOW, AUG_W), lambda j: (jnp.minimum((j + 1) * r, nb - 1), 0))
    whole = pl.BlockSpec((KV_GROUPS, 8, CHUNK), lambda j: (0, 0, 0))
    return pl.pallas_call(
        body, name="swa_bwd", grid=(nt,),
        out_shape=[jax.ShapeDtypeStruct((s, 1024), BF16), jax.ShapeDtypeStruct((KV_GROUPS, 8, CHUNK), F32)],
        in_specs=[cur(KV_GROUPS * CHUNK), cur(KV_GROUPS * CHUNK), cur(AUG_W), nxt, cur(AUG_W), nxt,
                  cur(CHUNK), cur(CHUNK), cur(CHUNK), whole],
        out_specs=[cur(1024), whole],
        scratch_shapes=[pltpu.VMEM((WINDOW, AUG_W), F32), pltpu.VMEM((tk, AUG_W), F32)],
        compiler_params=_params(1))(k_aug, v_aug, q_bwd, q_bwd, do_aug, do_aug, cos, sa, sb, sink_rows)


def _fox_fwd(first_key, q_aug, k_aug, v_aug, t):
    s = q_aug.shape[0]
    pw = 2 * CHUNK

    def body(lo_ref, q_ref, k_ref, v_ref, o_ref, qb_ref):
        i = pl.program_id(1)
        first = lo_ref[pl.program_id(0), i]
        row = lax.broadcasted_iota(jnp.int32, (t, t), 0)
        col = lax.broadcasted_iota(jnp.int32, (t, t), 1)
        lane = _lane(CHUNK)
        heads = (slice(0, CHUNK), slice(CHUNK, pw))
        qs = [q_ref[:, hs] for hs in heads]

        def step(jb, carry, masked):
            rows = pl.ds(pl.multiple_of(jb * t, t), t)
            scs = [_dot_nt(qs[hh], k_ref[rows, hs]) for hh, hs in enumerate(heads)]
            if masked:
                scs = [jnp.where(col <= row, sc, NEG) for sc in scs]
            m_news = [jnp.maximum(carry[2 * hh], jnp.max(scs[hh], axis=1, keepdims=True)) for hh in range(2)]
            ps = [jnp.exp(scs[hh] - m_news[hh]).astype(BF16) for hh in range(2)]
            new = []
            for hh, hs in enumerate(heads):
                m, acc = carry[2 * hh], carry[2 * hh + 1]
                new += [m_news[hh], jnp.exp(m - m_news[hh]) * acc + _dot(ps[hh], v_ref[rows, hs])]
            return tuple(new)

        init = (jnp.full((t, 1), NEG, F32), jnp.zeros((t, CHUNK), F32)) * 2
        carry = step(i, lax.fori_loop(first, i, lambda jb, c: step(jb, c, False), init), True)
        outs = []
        for hh, hs in enumerate(heads):
            m, acc = carry[2 * hh], carry[2 * hh + 1]
            denom = _lane_sum(acc, 64, 65)
            outs.append(acc / denom)
            lse = m + jnp.log(denom)
            qb_ref[:, hs] = _place3(lane, 67, -lse, qs[hh].astype(F32)).astype(BF16)
        o_ref[...] = _pair(outs[0], outs[1])

    return pl.pallas_call(
        body, name="fox_fwd", grid=(N_HEADS // 2, s // t),
        out_shape=[jax.ShapeDtypeStruct((s, 512), F32), jax.ShapeDtypeStruct((s, AUG_W), BF16)],
        in_specs=[pl.BlockSpec(memory_space=pltpu.SMEM),
                  pl.BlockSpec((t, pw), lambda p, i: (i, p)),
                  pl.BlockSpec((s, pw), lambda p, i: (0, p)),
                  pl.BlockSpec((s, pw), lambda p, i: (0, p))],
        out_specs=[pl.BlockSpec((t, CHUNK), lambda p, i: (i, p)), pl.BlockSpec((t, pw), lambda p, i: (i, p))],
        compiler_params=_params(2))(first_key, q_aug, k_aug, v_aug)


def _fox_block_ranges(bounds, norms, r):
    cum_first = bounds[0::r, 0, :N_HEADS]
    cum_last = bounds[r - 1::r, 1, :N_HEADS]
    n = cum_first.shape[0]
    reach = 2.0 * 1.02 * jnp.sqrt(norms[0, :N_HEADS] * norms[1, :N_HEADS]) + UNDERFLOW
    blk = jnp.arange(n, dtype=jnp.int32)
    decay = cum_last[None, :, :] - cum_first[:, None, :]
    skip = (decay > reach[None, None, :]) & (blk[None, :, None] < blk[:, None, None])
    first = jnp.min(jnp.where(skip, n, blk[None, :, None]), axis=1)
    first = jnp.minimum(first[:, 0::2], first[:, 1::2]).T
    needed = (first[:, :, None] <= blk[None, None, :]) & (blk[None, :, None] >= blk[None, None, :])
    last = jnp.max(jnp.where(needed, blk[None, :, None], blk[None, None, :]), axis=1)
    return first.astype(jnp.int32), last.astype(jnp.int32)


def _fox_bwd(last_query, k_aug, v_aug, q_bwd, do_aug, t):
    s = k_aug.shape[0]
    n = s // t
    pw = 2 * CHUNK

    def body(hi_ref, k_ref, v_ref, q_ref, do_ref, dk_ref, dv_ref, dck_ref, dq_ref, dcq_ref, dq_scr):
        j = pl.program_id(1)
        last = hi_ref[pl.program_id(0), j]

        @pl.when(j == 0)
        def _():
            dq_scr[...] = jnp.zeros_like(dq_scr)

        row = lax.broadcasted_iota(jnp.int32, (t, t), 0)
        col = lax.broadcasted_iota(jnp.int32, (t, t), 1)
        lane = _lane(CHUNK)
        heads = (slice(0, CHUNK), slice(CHUNK, pw))
        ks = [k_ref[:, hs] for hs in heads]
        vs = [v_ref[:, hs] for hs in heads]

        def step(ib, carry, masked):
            rows = pl.ds(pl.multiple_of(ib * t, t), t)
            new = []
            for hh, hs in enumerate(heads):
                dv, dk = carry[2 * hh], carry[2 * hh + 1]
                q, do = q_ref[rows, hs], do_ref[rows, hs]
                st = _dot_nt(ks[hh], q)
                if masked:
                    st = jnp.where(row <= col, st, NEG)
                pt = jnp.exp(st)
                ds = (pt * _dot_nt(vs[hh], do)).astype(BF16)
                new += [dv + _dot(pt.astype(BF16), do), dk + _dot(ds, q)]
                dq_scr[rows, hs] += _dot_tn(ds, ks[hh])
            return tuple(new)

        zero = jnp.zeros((t, CHUNK), F32)
        carry = lax.fori_loop(j + 1, last + 1, lambda ib, c: step(ib, c, False), step(j, (zero,) * 4, True))
        dvs, dks = (carry[0], carry[2]), (carry[1], carry[3])
        dk_ref[...] = _pair(dks[0], dks[1]).astype(BF16)
        dv_ref[...] = _pair(dvs[0], dvs[1]).astype(BF16)
        dck_ref[0] = jnp.where(lane == 0, pltpu.roll(dks[0], 64, 1),
                               jnp.where(lane == 1, pltpu.roll(dks[1], 65, 1), 0.0))

        @pl.when(j == n - 1)
        def _():
            for ib in range(n):
                rows = slice(t * ib, t * (ib + 1))
                d0, d1 = dq_scr[rows, 0:CHUNK], dq_scr[rows, CHUNK:pw]
                dq_ref[rows, :] = (_pair(d0, d1) * Q_SCALE).astype(BF16)
                dcq_ref[0, rows, :] = jnp.where(lane == 0, pltpu.roll(d0, CHUNK - 67, 1),
                                                jnp.where(lane == 1, pltpu.roll(d1, CHUNK - 66, 1), 0.0))

    return pl.pallas_call(
        body, name="fox_bwd", grid=(N_HEADS // 2, n),
        out_shape=[jax.ShapeDtypeStruct((s, 512), BF16), jax.ShapeDtypeStruct((s, 512), BF16),
                   jax.ShapeDtypeStruct((N_HEADS // 2, s, CHUNK), F32), jax.ShapeDtypeStruct((s, 512), BF16),
                   jax.ShapeDtypeStruct((N_HEADS // 2, s, CHUNK), F32)],
        in_specs=[pl.BlockSpec(memory_space=pltpu.SMEM),
                  pl.BlockSpec((t, pw), lambda p, j: (j, p)), pl.BlockSpec((t, pw), lambda p, j: (j, p)),
                  pl.BlockSpec((s, pw), lambda p, j: (0, p)), pl.BlockSpec((s, pw), lambda p, j: (0, p))],
        out_specs=[pl.BlockSpec((t, CHUNK), lambda p, j: (j, p)), pl.BlockSpec((t, CHUNK), lambda p, j: (j, p)),
                   pl.BlockSpec((1, t, CHUNK), lambda p, j: (p, j, 0)), pl.BlockSpec((s, CHUNK), lambda p, j: (0, p)),
                   pl.BlockSpec((1, s, CHUNK), lambda p, j: (p, 0, 0))],
        scratch_shapes=[pltpu.VMEM((s, pw), F32)],
        compiler_params=_params(2))(last_query, k_aug, v_aug, q_bwd, do_aug)


def _fgate_bwd(dcq, dck, xf, tm):
    s = xf.shape[0]
    nt = s // tm

    def body(dq_ref, dc_ref, xf_ref, df_ref, dbf_ref, carry_ref):
        @pl.when(pl.program_id(0) == 0)
        def _():
            carry_ref[...] = jnp.zeros_like(carry_ref)
            dbf_ref[...] = jnp.zeros_like(dbf_ref)

        row = lax.broadcasted_iota(jnp.int32, (tm, tm), 0)
        col = lax.broadcasted_iota(jnp.int32, (tm, tm), 1)
        tri = jnp.where(col >= row, 1.0, 0.0).astype(BF16)
        dcum = dq_ref[0] - dc_ref[0]
        for p in range(1, N_HEADS // 2):
            dcum = dcum + pltpu.roll(dq_ref[p] - dc_ref[p], 2 * p, 1)
        hi, mid, lo = _split3(dcum)
        dlogf = _dot(tri, hi.astype(BF16)) + _dot(tri, mid.astype(BF16)) + _dot(tri, lo.astype(BF16))
        dlogf = dlogf + carry_ref[0:1, :]
        carry_ref[0:1, :] += jnp.sum(dcum, axis=0, keepdims=True)
        df = dlogf * _sigmoid(-xf_ref[...])
        df_ref[...] = df.astype(BF16)
        dbf_ref[0:1, :] += jnp.sum(df, axis=0, keepdims=True)

    rev = pl.BlockSpec((tm, CHUNK), lambda i: (nt - 1 - i, 0))
    rev4 = pl.BlockSpec((N_HEADS // 2, tm, CHUNK), lambda i: (0, nt - 1 - i, 0))
    return pl.pallas_call(
        body, name="fgate_bwd", grid=(nt,),
        out_shape=[jax.ShapeDtypeStruct((s, CHUNK), BF16), jax.ShapeDtypeStruct((8, CHUNK), F32)],
        in_specs=[rev4, rev4, rev], out_specs=[rev, pl.BlockSpec((8, CHUNK), lambda i: (0, 0))],
        scratch_shapes=[pltpu.VMEM((8, CHUNK), F32)],
        compiler_params=_params(1))(dcq, dck, xf)


def _post(x, tgt, att_a, att_b, za, zb, ga, gb, woa_t, wob_t, wout, vec, ind, emat, tm):
    s = x.shape[0]

    def body(x_ref, t_ref, aa_ref, ab_ref, za_ref, zb_ref, ga_ref, gb_ref, woat_ref, wobt_ref, wout_ref,
             vec_ref, ind_ref, e_ref,
             dx1_ref, ua_ref, ub_ref, mg_ref, do_ref, dya_ref, dyb_ref, doa_ref, dob_ref, dzg_ref, acc_ref):
        @pl.when(pl.program_id(0) == 0)
        def _():
            acc_ref[...] = jnp.zeros_like(acc_ref)

        gate, gfin = vec_ref[0:1, :], vec_ref[1:2, :]
        inv_d = 1.0 / D_MODEL

        def branch_fwd(att_ref, z_ref, w_ref, u_ref):
            att, z = att_ref[...], z_ref[...]
            sz = _sigmoid(z)
            silu = z * sz
            u = (att * silu).astype(BF16)
            u_ref[...] = u.T
            return att, z, sz, silu, _dot_nt(u, w_ref[...])

        att_a, z_a, sz_a, silu_a, y_a = branch_fwd(aa_ref, za_ref, woat_ref, ua_ref)
        att_b, z_b, sz_b, silu_b, y_b = branch_fwd(ab_ref, zb_ref, wobt_ref, ub_ref)
        sg_a, sg_b = _sigmoid(ga_ref[...]), _sigmoid(gb_ref[...])
        merged = (sg_a * y_a + sg_b * y_b).astype(BF16)
        mg_ref[...] = merged.T
        o = _dot(merged, wout_ref[...])
        x1 = x_ref[...] + gate * o
        rstd = lax.rsqrt(jnp.mean(x1 * x1, axis=-1, keepdims=True) + NORM_EPS)
        xh = x1 * rstd
        diff = xh * gfin - t_ref[...]
        acc_ref[2:3, :] += (0.5 * inv_d) * jnp.sum(diff * diff, axis=0, keepdims=True)
        dy = diff * inv_d
        acc_ref[1:2, :] += jnp.sum(dy * xh, axis=0, keepdims=True)
        dyg = dy * gfin
        dx1 = rstd * (dyg - xh * jnp.mean(dyg * xh, axis=-1, keepdims=True))
        dx1_ref[...] = dx1
        acc_ref[0:1, :] += jnp.sum(dx1 * o, axis=0, keepdims=True)
        d_o = (dx1 * gate).astype(BF16)
        do_ref[...] = d_o
        dmg = _dot_nt(d_o, wout_ref[...])

        lane = _lane(CHUNK)
        low = lane < HEAD_DIM
        lane_all = _lane(AUG_W) & (CHUNK - 1)

        zg0 = SEG["za"][0]

        def branch_bwd(sg, y, att, z, sz, silu, wt_ref, dy_ref, g_name, z_name, doaug_ref):
            dyb = (dmg * sg).astype(BF16)
            dy_ref[...] = dyb
            g_off, z_off = SEG[g_name][0] - zg0, SEG[z_name][0] - zg0
            dzg_ref[:, g_off:g_off + D_MODEL] = (dmg * y * sg * (1.0 - sg)).astype(BF16)
            du = _dot(dyb, wt_ref[...])
            datt = du * silu
            dzg_ref[:, z_off:z_off + 512] = (du * att * (sz * (1.0 + z * (1.0 - sz)))).astype(BF16)
            extra = _spread3(-_dot_split(datt * att, ind_ref[...]), e_ref)
            for p in range(N_HEADS // 2):
                c = datt[:, CHUNK * p:CHUNK * (p + 1)]
                for hh in range(2):
                    hd = 2 * p + hh
                    src = c if hh == 0 else pltpu.roll(c, HEAD_DIM, 1)
                    doaug_ref[:, CHUNK * hd:CHUNK * (hd + 1)] = jnp.where(
                        low, src, extra[:, CHUNK * hd:CHUNK * (hd + 1)]).astype(BF16)

        branch_bwd(sg_a, y_a, att_a, z_a, sz_a, silu_a, woat_ref, dya_ref, "ga", "za", doa_ref)
        branch_bwd(sg_b, y_b, att_b, z_b, sz_b, silu_b, wobt_ref, dyb_ref, "gb", "zb", dob_ref)

    row_blk = lambda w: pl.BlockSpec((tm, w), lambda i: (i, 0))
    full = lambda a: pl.BlockSpec(a.shape, lambda i: (0,) * a.ndim)
    sds = lambda w, dt: jax.ShapeDtypeStruct((s, w), dt)
    tds = lambda w: jax.ShapeDtypeStruct((w, s), BF16)
    col_blk = lambda w: pl.BlockSpec((w, tm), lambda i: (0, i))
    out_shape = [sds(D_MODEL, F32), tds(512), tds(512), tds(D_MODEL), sds(D_MODEL, BF16),
                 sds(D_MODEL, BF16), sds(D_MODEL, BF16), sds(AUG_W, BF16), sds(AUG_W, BF16), sds(3072, BF16),
                 jax.ShapeDtypeStruct((8, D_MODEL), F32)]
    ins = [x, tgt, att_a, att_b, za, zb, ga, gb, woa_t, wob_t, wout, vec, ind, emat]
    in_specs = [row_blk(a.shape[1]) for a in ins[:8]] + [full(a) for a in ins[8:]]
    out_specs = ([row_blk(D_MODEL), col_blk(512), col_blk(512), col_blk(D_MODEL)]
                 + [row_blk(o.shape[1]) for o in out_shape[4:-1]] + [pl.BlockSpec((8, D_MODEL), lambda i: (0, 0))])
    return pl.pallas_call(body, name="post", grid=(s // tm,), out_shape=out_shape, in_specs=in_specs,
                          out_specs=out_specs, compiler_params=_params(1))(*ins)


def _bwd_pre(dproj, w_all, x, dx1, mod, tm, chip_halves, modes):
    s = x.shape[0]
    ng, nh = len(DPROJ_GROUPS), len(chip_halves)
    last = s // tm - 1

    def body(*refs):
        dp_refs = refs[:ng]
        wt_ref, x_ref, dx1_ref, mod_ref = refs[ng:ng + 4]
        send_refs = refs[ng + 4:ng + 4 + nh]
        gx_ref, acc_ref = refs[ng + 4 + nh:ng + 6 + nh]
        recv_refs = refs[ng + 6 + nh:ng + 6 + 2 * nh]
        sems = refs[ng + 6 + 2 * nh:]

        @pl.when(pl.program_id(0) == 0)
        def _():
            acc_ref[...] = jnp.zeros_like(acc_ref)
            _exchange_start(_exchange_copies(send_refs, recv_refs, modes, *sems))

        dh = None
        for dp_ref, (_, off, w) in zip(dp_refs, DPROJ_GROUPS):
            part = _dot_nt(dp_ref[...], wt_ref[:, off:off + w])
            dh = part if dh is None else dh + part
        xv = x_ref[...]
        rstd = lax.rsqrt(jnp.mean(xv * xv, axis=-1, keepdims=True) + NORM_EPS)
        xh = xv * rstd
        gn = mod_ref[0:1, :]
        acc_ref[0:1, :] += jnp.sum(dh, axis=0, keepdims=True)
        acc_ref[1:2, :] += jnp.sum(dh * (xh * gn), axis=0, keepdims=True)
        dn = dh * (1.0 + mod_ref[1:2, :])
        acc_ref[2:3, :] += jnp.sum(dn * xh, axis=0, keepdims=True)
        dxh = dn * gn
        gx_ref[...] = dx1_ref[...] + rstd * (dxh - xh * jnp.mean(dxh * xh, axis=-1, keepdims=True))

        @pl.when(pl.program_id(0) == last)
        def _():
            _exchange_wait(_exchange_copies(send_refs, recv_refs, modes, *sems))

    row_blk = lambda w: pl.BlockSpec((tm, w), lambda i: (i, 0))
    full = lambda a: pl.BlockSpec(a.shape, lambda i: (0,) * a.ndim)
    outs = pl.pallas_call(
        body, name="bwd_pre", grid=(s // tm,),
        out_shape=[jax.ShapeDtypeStruct((s, D_MODEL), F32), jax.ShapeDtypeStruct((8, D_MODEL), F32)]
        + _exchange_out_shapes(chip_halves, modes),
        in_specs=[row_blk(w) for _, _, w in DPROJ_GROUPS] + [full(w_all), row_blk(D_MODEL), row_blk(D_MODEL), full(mod)]
        + [ANY_SPEC] * nh,
        out_specs=[row_blk(D_MODEL), pl.BlockSpec((8, D_MODEL), lambda i: (0, 0))] + [ANY_SPEC] * nh,
        scratch_shapes=_exchange_sems(nh),
        compiler_params=_params(1))(*dproj, w_all, x, dx1, mod, *chip_halves)
    return outs[0], outs[1], outs[2:]


def _accumulate_tokens(a, b_ref, acc_ref, o_ref, last):
    @pl.when(pl.program_id(0) == 0)
    def _():
        acc_ref[...] = jnp.zeros_like(acc_ref)

    acc_ref[...] += _dot(a, b_ref[...])

    @pl.when(pl.program_id(0) == last)
    def _():
        o_ref[...] = acc_ref[...].astype(BF16)


def _token_matmul_multi(a_t, bs, tk, name):
    m, s = a_t.shape
    nb = len(bs)
    last = s // tk - 1

    def body(*refs):
        a_ref, b_refs, o_refs, acc_refs = refs[0], refs[1:1 + nb], refs[1 + nb:1 + 2 * nb], refs[1 + 2 * nb:]
        a = a_ref[...]
        for b_ref, o_ref, acc_ref in zip(b_refs, o_refs, acc_refs):
            _accumulate_tokens(a, b_ref, acc_ref, o_ref, last)

    return pl.pallas_call(
        body, name=name, grid=(s // tk,),
        out_shape=[jax.ShapeDtypeStruct((m, b.shape[1]), BF16) for b in bs],
        in_specs=[pl.BlockSpec((m, tk), lambda k: (0, k))] + [pl.BlockSpec((tk, b.shape[1]), lambda k: (k, 0)) for b in bs],
        out_specs=[pl.BlockSpec((m, b.shape[1]), lambda k: (0, 0)) for b in bs],
        scratch_shapes=[pltpu.VMEM((m, b.shape[1]), F32) for b in bs],
        compiler_params=_params(1))(a_t, *bs)


def _adam_math(g, w, m, v):
    m2 = ADAM_B1 * m + (1.0 - ADAM_B1) * g
    v2 = ADAM_B2 * v + (1.0 - ADAM_B2) * (g * g)
    delta = -ADAM_LR * ((m2 / ADAM_C1) / (jnp.sqrt(v2 / ADAM_C2) + ADAM_EPS) + ADAM_WD * w)
    return delta, m2, v2


def _adamw_parts(full, parts, others, w, m, v, tr, name):
    _, rws, cols = w.shape

    def body(oth_ref, f_ref, p_ref, w_ref, m_ref, v_ref, g_ref, d_ref, m2_ref, v2_ref):
        g = None
        for chip in range(N_CHIPS):
            part = (1.0 - oth_ref[chip]) * f_ref[chip] + oth_ref[chip] * p_ref[chip].astype(F32)
            g = part if g is None else g + part
        g_ref[0] = g
        d_ref[0], m2_ref[0], v2_ref[0] = _adam_math(g, w_ref[0], m_ref[0], v_ref[0])

    blk = pl.BlockSpec((1, tr, cols), lambda i: (0, i, 0))
    slots = pl.BlockSpec((N_CHIPS, tr, cols), lambda i: (0, i, 0))
    o = jax.ShapeDtypeStruct((1, rws, cols), F32)
    return pl.pallas_call(
        body, name=name, grid=(rws // tr,), out_shape=[o, o, o, o],
        in_specs=[pl.BlockSpec(memory_space=pltpu.SMEM), slots, slots, blk, blk, blk],
        out_specs=[blk, blk, blk, blk], compiler_params=_params(1))(others, full, parts, w, m, v)


def _adamw_devices(parts, w, m, v, tr, name):
    _, rws, cols = w.shape

    def body(p_ref, w_ref, m_ref, v_ref, g_ref, d_ref, m2_ref, v2_ref):
        g = p_ref[0].astype(F32)
        for dev in range(1, N_DEV):
            g = g + p_ref[dev].astype(F32)
        g_ref[0] = g
        d_ref[0], m2_ref[0], v2_ref[0] = _adam_math(g, w_ref[0], m_ref[0], v_ref[0])

    blk = pl.BlockSpec((1, tr, cols), lambda i: (0, i, 0))
    o = jax.ShapeDtypeStruct((1, rws, cols), F32)
    return pl.pallas_call(
        body, name=name, grid=(rws // tr,), out_shape=[o, o, o, o],
        in_specs=[pl.BlockSpec((N_DEV, tr, cols), lambda i: (0, i, 0)), blk, blk, blk],
        out_specs=[blk, blk, blk, blk], compiler_params=_params(1))(parts, w, m, v)


def _adamw_ada(c_lanes, d_rows, w, m, v, tr):
    _, rws, cols = w.shape

    def body(c_ref, d_ref, w_ref, m_ref, v_ref, g_ref, dl_ref, m2_ref, v2_ref):
        for k in range(cols // CHUNK):
            cs = slice(CHUNK * k, CHUNK * (k + 1))
            g = c_ref[0] * d_ref[0:1, cs]
            for b in range(1, N_DEV):
                g = g + c_ref[b] * d_ref[b:b + 1, cs]
            g_ref[0, :, cs] = g
            dl_ref[0, :, cs], m2_ref[0, :, cs], v2_ref[0, :, cs] = _adam_math(
                g, w_ref[0, :, cs], m_ref[0, :, cs], v_ref[0, :, cs])

    blk = pl.BlockSpec((1, tr, cols), lambda i: (0, i, 0))
    o = jax.ShapeDtypeStruct((1, rws, cols), F32)
    return pl.pallas_call(
        body, name="adamw_ada", grid=(rws // tr,), out_shape=[o, o, o, o],
        in_specs=[pl.BlockSpec((N_DEV, tr, CHUNK), lambda i: (0, i, 0)), pl.BlockSpec((N_DEV, cols), lambda i: (0, 0)),
                  blk, blk, blk],
        out_specs=[blk, blk, blk, blk], compiler_params=_params(1))(c_lanes, d_rows, w, m, v)


SMALL = (("b_ada", 0, 0, 3 * D_MODEL), ("g_norm", 1, 0, D_MODEL), ("b_f", 3, 0, N_HEADS), ("sinks", 4, 0, N_HEADS),
         ("g_final", 1, D_MODEL, D_MODEL))
LOSS_ROW = 2


def _pack_small(pre_acc, post_acc, dbf_acc, dsink):
    def body(pre_ref, post_ref, dbf_ref, dsink_ref, o_ref):
        o_ref[...] = jnp.zeros_like(o_ref)
        d = D_MODEL
        o_ref[0:1, 0:d], o_ref[0:1, d:2 * d], o_ref[0:1, 2 * d:3 * d] = pre_ref[0:1, :], pre_ref[1:2, :], post_ref[0:1, :]
        o_ref[1:2, 0:d], o_ref[1:2, d:2 * d] = pre_ref[2:3, :], post_ref[1:2, :]
        o_ref[LOSS_ROW:LOSS_ROW + 1, 0:d] = post_ref[2:3, :]
        o_ref[3:4, 0:CHUNK] = dbf_ref[0:1, :]
        lane = _lane(CHUNK)
        sinks = jnp.zeros((1, CHUNK), F32)
        for kv in range(KV_GROUPS):
            for g in range(GROUP):
                sinks = jnp.where(lane == GROUP * kv + g, dsink_ref[kv, g:g + 1, :], sinks)
        o_ref[4:5, 0:CHUNK] = sinks

    return pl.pallas_call(body, name="pack_small", out_shape=jax.ShapeDtypeStruct((8, 3 * D_MODEL), F32),
                          compiler_params=_params(0))(pre_acc, post_acc, dbf_acc, dsink)


def _adamw_small(packs, params):
    n = len(SMALL)

    def body(*refs):
        p_ref, wmv = refs[0], refs[1:1 + 3 * n]
        outs, loss_ref, tot_ref = refs[1 + 3 * n:1 + 7 * n], refs[1 + 7 * n], refs[2 + 7 * n]
        tot = p_ref[0]
        for dev in range(1, N_DEV):
            tot = tot + p_ref[dev]
        tot_ref[...] = tot
        for i, (_, row, lo, w) in enumerate(SMALL):
            g = tot_ref[row:row + 1, lo:lo + w]
            o = outs[4 * i:4 * i + 4]
            o[0][...] = g
            o[1][...], o[2][...], o[3][...] = _adam_math(g, wmv[3 * i][...], wmv[3 * i + 1][...], wmv[3 * i + 2][...])
        loss_ref[...] = jnp.broadcast_to(jnp.sum(tot_ref[LOSS_ROW:LOSS_ROW + 1, 0:D_MODEL], axis=1, keepdims=True),
                                         loss_ref.shape)

    flat = [a for wmv in params for a in wmv]
    out_shape = [jax.ShapeDtypeStruct(wmv[0].shape, F32) for wmv in params for _ in range(4)]
    res = pl.pallas_call(body, name="adamw_small", out_shape=out_shape + [jax.ShapeDtypeStruct((8, CHUNK), F32)],
                         scratch_shapes=[pltpu.VMEM((8, 3 * D_MODEL), F32)],
                         compiler_params=_params(0))(packs, *flat)
    return [tuple(res[4 * i:4 * i + 4]) for i in range(n)], res[4 * n]


def _tile(s, want):
    return min(s, want)


def _layout_pieces():
    orig = {"qa": 0, "ka": 512, "va": 640, "za": 768, "qb": 1280, "kb": 1792, "vb": 2304, "f": 2816, "zb": 2824,
            "ga": 3336, "gb": 4360}
    pieces = []
    for name, (off, w) in SEG.items():
        if name in ("ka", "va"):
            pieces += [(orig[name] + HEAD_DIM * kv, orig[name] + HEAD_DIM * (kv + 1), off + CHUNK * kv) for kv in range(KV_GROUPS)]
        else:
            pieces.append((orig[name], orig[name] + (N_HEADS if name == "f" else w), off))
    return pieces


def _assemble_w_all(win_g):
    cols, pos = [], 0
    for lo, hi, new in sorted(_layout_pieces(), key=lambda t: t[2]):
        if new > pos:
            cols.append(jnp.zeros((D_MODEL, new - pos), win_g.dtype))
        col = lo
        while col < hi:
            dev = col // IN_SHARD
            end = min(hi, (dev + 1) * IN_SHARD)
            cols.append(win_g[dev, :, col - dev * IN_SHARD:end - dev * IN_SHARD])
            col = end
        pos = new + hi - lo
    cols.append(jnp.zeros((D_MODEL, PROJ_W - pos), win_g.dtype))
    return jnp.concatenate(cols, axis=1)


def _grad_slot(dw_groups, dev):
    lo_d, hi_d = dev * IN_SHARD, (dev + 1) * IN_SHARD
    cols = []
    for lo, hi, new in sorted(_layout_pieces()):
        a, b = max(lo, lo_d), min(hi, hi_d)
        if a < b:
            arr, off = next((g, o) for g, (_, o, w) in zip(dw_groups, DPROJ_GROUPS) if o <= new < o + w)
            cols.append(arr[:, new - off + a - lo:new - off + b - lo])
    return jnp.concatenate(cols, axis=1)


def kernel(x, c, positions, w_ada, b_ada, g_norm, w_in, b_f, sinks, w_o_swa, w_o_fox, w_out, g_final, loss_target, m_w_ada, m_b_ada, m_g_norm, m_w_in, m_b_f, m_sinks, m_w_o_swa, m_w_o_fox, m_w_out, m_g_final, v_w_ada, v_b_ada, v_g_norm, v_w_in, v_b_f, v_sinks, v_w_o_swa, v_w_o_fox, v_w_out, v_g_final):
    s = x.shape[1]
    tm = _tile(s, 256)
    ta = _tile(s, 512)
    me = 4 * lax.axis_index("x") + 2 * lax.axis_index("y") + lax.axis_index("c")
    x2, tgt = x[0], loss_target[0]

    inv_freq = np.power(np.float32(ROPE_THETA), -np.arange(0, HEAD_DIM, 2, dtype=np.float32) / HEAD_DIM)
    inv_freq = jnp.asarray(np.tile(inv_freq, CHUNK // (HEAD_DIM // 2))[None, :], F32)
    (cos, sa, sb), (win_g, woa_g, wob_g, wout_g), c_all = _rope_tables_and_gather(
        positions.reshape(s, 1).astype(F32), inv_freq, tm,
        [w_in[0].astype(BF16), w_o_swa[0].T.astype(BF16), w_o_fox[0].T.astype(BF16), w_out[0].astype(BF16)],
        jnp.broadcast_to(c, (8, D_MODEL)))
    c_all = c_all[:, 0, :]
    b_shard = lax.dynamic_slice(b_ada, (0, me * ADA_SHARD), (1, ADA_SHARD))
    ada_part = _ada_fwd(c_all, w_ada[0], b_shard)
    (ada_g,) = _exchange([ada_part], ["gather"], "gather_ada")
    ada = lax.dynamic_index_in_dim(ada_g, me, axis=1, keepdims=False).reshape(3 * D_MODEL)
    shift, scale, gate = ada[:D_MODEL], ada[D_MODEL:2 * D_MODEL], ada[2 * D_MODEL:]
    woa_t = woa_g.reshape(D_MODEL, 512)
    wob_t = wob_g.reshape(D_MODEL, 512)
    wout = wout_g.reshape(D_MODEL, D_MODEL)
    w_all = _assemble_w_all(win_g)

    zrow = jnp.zeros((1, D_MODEL), F32)
    mod = jnp.concatenate([g_norm, scale[None], shift[None], zrow, zrow, zrow, zrow, zrow], axis=0)
    bf_row = jnp.pad(b_f, ((0, 0), (0, CHUNK - N_HEADS)))
    emat_np = np.zeros((3 * CHUNK, AUG_W), np.float32)
    ind_np = np.zeros((512, CHUNK), np.float32)
    for hd in range(N_HEADS):
        for part in range(3):
            emat_np[CHUNK * part + hd, CHUNK * hd + 64 + part] = 1.0
        ind_np[HEAD_DIM * hd:HEAD_DIM * (hd + 1), hd] = 1.0
    emat, ind = jnp.asarray(emat_np, BF16), jnp.asarray(ind_np, BF16)
    sink_rows = jnp.broadcast_to(jnp.pad(sinks.reshape(KV_GROUPS, GROUP), ((0, 0), (0, 8 - GROUP)))[:, :, None],
                                 (KV_GROUPS, 8, CHUNK))

    (h_t, qa, ka, va, za, qb, kb, vb, zb, ga, gb, xf, bounds, norms) = _fwd_proj(
        x2, mod, w_all, cos, sa, sb, bf_row, emat, ind, tm)
    first_key, last_query = _fox_block_ranges(bounds, norms, ta // tm)
    att_a, qa_bwd = _swa_fwd(qa, ka, va, sink_rows, ta)
    att_b, qb_bwd = _fox_fwd(first_key, qb, kb, vb, ta)
    vec = jnp.concatenate([gate[None], g_final[None], zrow, zrow, zrow, zrow, zrow, zrow], axis=0)
    (dx1, ua_t, ub_t, merged_t, d_o, dya, dyb, doa, dob, dzg, post_acc) = _post(
        x2, tgt, att_a, att_b, za, zb, ga, gb, woa_t, wob_t, wout, vec, ind, emat, tm)

    da, dsink = _swa_bwd(ka, va, qa_bwd, doa, cos, sa, sb, sink_rows, ta)
    dkb, dvb, dck, dqb, dcq = _fox_bwd(last_query, kb, vb, qb_bwd, dob, ta)
    dfb, dbf_acc = _fgate_bwd(dcq, dck, xf, ta)
    dproj = [da, dqb, dkb, dvb, dzg, dfb]
    dw_zg, = _token_matmul_multi(h_t, [dzg], ta, "dw_in_zg")
    dw_a, dw_qb, dw_kb, dw_vb, dw_f = _token_matmul_multi(h_t, [da, dqb, dkb, dvb, dfb], ta, "dw_in_rest")
    dw_all = [dw_a, dw_qb, dw_kb, dw_vb, dw_zg, dw_f]

    core, chip = lax.axis_index("c"), 2 * lax.axis_index("x") + lax.axis_index("y")
    win_slots = jnp.stack([jnp.stack([_grad_slot(dw_all, 2 * ch + co) for ch in range(N_CHIPS)]) for co in range(2)])
    (dwoa, dwob, dwout), (win_theirs,) = _token_matmuls_and_swap(
        [(ua_t, dya), (ub_t, dyb), (merged_t, d_o)], ta, [win_slots])
    win_full, win_half = _chip_partial(jnp.reshape(core, (1,)).astype(jnp.int32), win_slots, win_theirs, 256,
                                       "chip_partial_w_in")
    col_slots = lambda g: g.reshape(512, N_DEV, 128).transpose(1, 0, 2)
    grad_x, pre_acc, (p_win, p_woa, p_wob, p_wout) = _bwd_pre(
        dproj, w_all, x2, dx1, mod, tm,
        [win_half, col_slots(dwoa), col_slots(dwob), dwout.reshape(N_DEV, 128, D_MODEL)],
        ["chips", "devices", "devices", "devices"])
    (packs,) = _exchange([_pack_small(pre_acc, post_acc, dbf_acc, dsink)], ["gather"], "gather_small_grads")
    others = jnp.where(jnp.arange(N_CHIPS) == chip, 0.0, 1.0).astype(F32)

    g_win, d_win, m_win, v_win = _adamw_parts(win_full, p_win, others, w_in, m_w_in, v_w_in, 128, "adamw_w_in")
    g_woa, d_woa, m_woa, v_woa = _adamw_devices(p_woa, w_o_swa, m_w_o_swa, v_w_o_swa, 128, "adamw_w_o_swa")
    g_wob, d_wob, m_wob, v_wob = _adamw_devices(p_wob, w_o_fox, m_w_o_fox, v_w_o_fox, 128, "adamw_w_o_fox")
    g_wout, d_wout, m_wout, v_wout = _adamw_devices(p_wout, w_out, m_w_out, v_w_out, 128, "adamw_w_out")
    d_ada_rows = lax.dynamic_slice(packs[:, 0, :], (0, me * ADA_SHARD), (N_DEV, ADA_SHARD))
    c_lanes = jnp.broadcast_to(c_all[:, :, None], (N_DEV, D_MODEL, CHUNK))
    g_wada, d_wada, m_wada, v_wada = _adamw_ada(c_lanes, d_ada_rows, w_ada, m_w_ada, v_w_ada, 256)

    row = lambda a: a.reshape(1, D_MODEL)
    small, loss_rows = _adamw_small(packs, [(b_ada, m_b_ada, v_b_ada), (g_norm, m_g_norm, v_g_norm), (b_f, m_b_f, v_b_f),
                                            (sinks, m_sinks, v_sinks), (row(g_final), row(m_g_final), row(v_g_final))])
    bada_o, gn_o, bf_o, sk_o, gf_o = small

    outs = []
    for k, big in enumerate(((g_wada, g_win, g_woa, g_wob, g_wout), (d_wada, d_win, d_woa, d_wob, d_wout),
                             (m_wada, m_win, m_woa, m_wob, m_wout), (v_wada, v_win, v_woa, v_wob, v_wout))):
        wada_o, win_o, woa_o, wob_o, wout_o = big
        outs += [wada_o, bada_o[k], gn_o[k], win_o, bf_o[k], sk_o[k], woa_o, wob_o, wout_o, gf_o[k].reshape(D_MODEL)]
    return (loss_rows[0, 0], grad_x[None], *outs)
```

```python
import numpy as np
import jax
import jax.numpy as jnp
from jax import lax
from jax.experimental import pallas as pl
from jax.experimental.pallas import tpu as pltpu

F32 = jnp.float32
BF16 = jnp.bfloat16

D_MODEL = 1024
HEAD_DIM = 64
N_HEADS = 8
KV_GROUPS = 2
GROUP = N_HEADS // KV_GROUPS
WINDOW = 128
CHUNK = 128
AUG_W = N_HEADS * CHUNK
N_DEV = 8
IN_SHARD = 673
ADA_SHARD = 384
NORM_EPS = 1e-6
ROPE_THETA = 10000.0
Q_SCALE = HEAD_DIM ** -0.5
NEG = -1e30
UNDERFLOW = 105.0

ADAM_LR = 0.001
ADAM_B1 = 0.9
ADAM_B2 = 0.999
ADAM_EPS = 1e-08
ADAM_WD = 0.01
ADAM_STEP = 10
ADAM_C1 = 1.0 - ADAM_B1 ** ADAM_STEP
ADAM_C2 = 1.0 - ADAM_B2 ** ADAM_STEP

SEG = {}
_off = 0
for _name, _w in (("qa", 512), ("ka", 256), ("va", 256), ("qb", 512), ("kb", 512), ("vb", 512),
                  ("za", 512), ("zb", 512), ("ga", 1024), ("gb", 1024), ("f", 128)):
    SEG[_name] = (_off, _w)
    _off += _w
PROJ_W = _off
DPROJ_GROUPS = (("a", SEG["qa"][0], 1024), ("qb", SEG["qb"][0], 512), ("kb", SEG["kb"][0], 512),
                ("vb", SEG["vb"][0], 512), ("zg", SEG["za"][0], 3072), ("f", SEG["f"][0], 128))

VMEM_LIMIT = 56 * 1024 * 1024


def _params(n_axes):
    return pltpu.CompilerParams(dimension_semantics=("arbitrary",) * n_axes, vmem_limit_bytes=VMEM_LIMIT)


def _resident(a):
    return pl.BlockSpec(a.shape, lambda i: (0, 0), pipeline_mode=pl.Buffered(1))


def _dot(a, b):
    return jnp.dot(a, b, preferred_element_type=F32)


def _dot_nt(a, b):
    return lax.dot_general(a, b, (((1,), (1,)), ((), ())), preferred_element_type=F32)


def _dot_tn(a, b):
    return lax.dot_general(a, b, (((0,), (0,)), ((), ())), preferred_element_type=F32)


def _lane(n):
    return lax.broadcasted_iota(jnp.int32, (1, n), 1)


def _split3(x):
    hi = x.astype(BF16).astype(F32)
    r = x - hi
    mid = r.astype(BF16).astype(F32)
    lo = (r - mid).astype(BF16).astype(F32)
    return hi, mid, lo


def _dot_split(x, b):
    hi, mid, lo = _split3(x)
    return _dot(hi.astype(BF16), b) + _dot(mid.astype(BF16), b) + _dot(lo.astype(BF16), b)


def _spread3(x, e3_ref):
    return _dot(jnp.concatenate(_split3(x), axis=1).astype(BF16), e3_ref[...])


def _place3(lane, base, x, other):
    hi, mid, lo = _split3(x)
    return jnp.where(lane == base, hi, jnp.where(lane == base + 1, mid, jnp.where(lane == base + 2, lo, other)))


def _lane_sum(x, lo, hi):
    lane = _lane(x.shape[1])
    return jnp.sum(jnp.where((lane >= lo) & (lane < hi), x, 0.0), axis=1, keepdims=True)


def _ones_lanes(lo, hi):
    lane = _lane(CHUNK)
    return jnp.where((lane >= lo) & (lane < hi), 1.0, 0.0).astype(F32)


def _rope(c, cos, sa, sb):
    return c * cos + pltpu.roll(c, CHUNK - 32, 1) * sa + pltpu.roll(c, 32, 1) * sb


def _rope_inv(d, cos, sa, sb):
    return d * cos - (pltpu.roll(d, CHUNK - 32, 1) * sa + pltpu.roll(d, 32, 1) * sb)


def _pair(c0, c1):
    return jnp.where(_lane(CHUNK) < HEAD_DIM, c0, pltpu.roll(c1, HEAD_DIM, 1))


def _sigmoid(x):
    return 0.5 * jnp.tanh(0.5 * x) + 0.5


N_CHIPS = 4
ANY_SPEC = pl.BlockSpec(memory_space=pl.ANY)


def _exchange(arrays, modes, name):
    n = len(arrays)

    def body(*refs):
        copies = _exchange_copies(refs[:n], refs[n:2 * n], modes, *refs[2 * n:])
        _exchange_start(copies)
        _exchange_wait(copies)

    return pl.pallas_call(
        body, name=name, out_shape=_exchange_out_shapes(arrays, modes),
        in_specs=[ANY_SPEC] * n, out_specs=[ANY_SPEC] * n, scratch_shapes=_exchange_sems(n),
    )(*arrays)


def _exchange_out_shapes(arrays, modes):
    return [jax.ShapeDtypeStruct((N_DEV,) + a.shape if md == "gather" else a.shape, a.dtype)
            for a, md in zip(arrays, modes)]


def _exchange_sems(n):
    return [pltpu.SemaphoreType.DMA((n, N_DEV - 1)), pltpu.SemaphoreType.DMA((n, N_DEV - 1)),
            pltpu.SemaphoreType.DMA((n,))]


def _exchange_copies(srcs, dsts, modes, send_sems, recv_sems, loc_sems):
    n = len(srcs)
    x, y, c = lax.axis_index("x"), lax.axis_index("y"), lax.axis_index("c")

    def slot(i, px, py, pc):
        return 2 * px + py if modes[i] == "chips" else 4 * px + 2 * py + pc

    def src_of(i, px, py, pc):
        return srcs[i] if modes[i] == "gather" else srcs[i].at[slot(i, px, py, pc)]

    local = [pltpu.make_async_copy(src_of(i, x, y, c), dsts[i].at[slot(i, x, y, c)], loc_sems.at[i])
             for i in range(n)]
    sends, recvs = [], []
    for r in (1, 2, 4, 3, 5, 6, 7):
        px = 1 - x if r & 4 else x
        py = 1 - y if r & 2 else y
        pc = 1 - c if r & 1 else c
        for i in range(n):
            if modes[i] == "chips" and r & 1:
                continue
            sems = dict(send_sem=send_sems.at[i, r - 1], recv_sem=recv_sems.at[i, r - 1],
                        device_id=(px, py, pc), device_id_type=pl.DeviceIdType.MESH)
            sends.append(pltpu.make_async_remote_copy(
                src_ref=src_of(i, px, py, pc), dst_ref=dsts[i].at[slot(i, x, y, c)], **sems))
            recvs.append(pltpu.make_async_remote_copy(
                src_ref=src_of(i, px, py, pc), dst_ref=dsts[i].at[slot(i, px, py, pc)], **sems))
    return local, sends, recvs


def _exchange_start(copies):
    local, sends, _ = copies
    for cp in local + sends:
        cp.start()


def _exchange_wait(copies):
    local, sends, recvs = copies
    for cp in recvs:
        cp.wait_recv()
    for cp in sends:
        cp.wait_send()
    for cp in local:
        cp.wait()


def _gather2_parts(srcs, dsts, send_sems, recv_sems, loc_sems):
    n = len(srcs)
    x, y, c = lax.axis_index("x"), lax.axis_index("y"), lax.axis_index("c")
    me, sibling = (x, y, c), (x, y, 1 - c)
    chips = [(1 - x, y), (x, 1 - y), (1 - x, 1 - y)]

    def rows(i, dev):
        return dsts[i].at[4 * dev[0] + 2 * dev[1] + dev[2]]

    def copy(i, k, block, to, src=None):
        return pltpu.make_async_remote_copy(
            src_ref=rows(i, block) if src is None else src, dst_ref=rows(i, block),
            send_sem=send_sems.at[i, k], recv_sem=recv_sems.at[i, k],
            device_id=to, device_id_type=pl.DeviceIdType.MESH)

    local = [pltpu.make_async_copy(srcs[i], rows(i, me), loc_sems.at[i]) for i in range(n)]
    first = []
    for i in range(n):
        first.append(copy(i, 0, me, sibling, src=srcs[i]))
        first += [copy(i, 1 + j, me, (*chip, c), src=srcs[i]) for j, chip in enumerate(chips)]
    return n, c, me, sibling, chips, copy, local, first


def _gather2_start(srcs, dsts, send_sems, recv_sems, loc_sems):
    *_, local, first = _gather2_parts(srcs, dsts, send_sems, recv_sems, loc_sems)
    for cp in local + first:
        cp.start()


def _gather2_finish(srcs, dsts, send_sems, recv_sems, loc_sems):
    n, c, me, sibling, chips, copy, local, first = _gather2_parts(srcs, dsts, send_sems, recv_sems, loc_sems)
    passed = []
    for j, chip in enumerate(chips):
        for i in range(n):
            copy(i, 1 + j, (*chip, c), me).wait_recv()
            cp = copy(i, 4 + j, (*chip, c), sibling)
            cp.start()
            passed.append(cp)
    for i in range(n):
        copy(i, 0, sibling, me).wait_recv()
        for j, chip in enumerate(chips):
            copy(i, 4 + j, (*chip, 1 - c), me).wait_recv()
    for cp in first + passed:
        cp.wait_send()
    for cp in local:
        cp.wait()


def _sibling_copies(srcs, dsts, send_sems, recv_sems):
    x, y, c = lax.axis_index("x"), lax.axis_index("y"), lax.axis_index("c")
    return [pltpu.make_async_remote_copy(
        src_ref=srcs[i].at[1 - c, k], dst_ref=dsts[i].at[k], send_sem=send_sems.at[i, k], recv_sem=recv_sems.at[i, k],
        device_id=(x, y, 1 - c), device_id_type=pl.DeviceIdType.MESH)
        for i in range(len(srcs)) for k in range(N_CHIPS)]


def _token_matmuls_and_swap(pairs, tk, swap):
    npair, nsw = len(pairs), len(swap)
    s = pairs[0][0].shape[1]
    last = s // tk - 1

    def body(*refs):
        a_refs, b_refs = refs[0:2 * npair:2], refs[1:2 * npair:2]
        src = refs[2 * npair:2 * npair + nsw]
        o_refs = refs[2 * npair + nsw:3 * npair + nsw]
        dst = refs[3 * npair + nsw:3 * npair + 2 * nsw]
        sems = refs[3 * npair + 2 * nsw:3 * npair + 2 * nsw + 2]
        acc_refs = refs[3 * npair + 2 * nsw + 2:]

        @pl.when(pl.program_id(0) == 0)
        def _():
            for cp in _sibling_copies(src, dst, *sems):
                cp.start()

        for a_ref, b_ref, o_ref, acc_ref in zip(a_refs, b_refs, o_refs, acc_refs):
            _accumulate_tokens(a_ref[...], b_ref, acc_ref, o_ref, last)

        @pl.when(pl.program_id(0) == last)
        def _():
            for cp in _sibling_copies(src, dst, *sems):
                cp.wait()

    in_specs, args = [], []
    for a_t, b in pairs:
        in_specs += [pl.BlockSpec((a_t.shape[0], tk), lambda k: (0, k)), pl.BlockSpec((tk, b.shape[1]), lambda k: (k, 0))]
        args += [a_t, b]
    outs = pl.pallas_call(
        body, name="dw_out_projs_swap", grid=(s // tk,),
        out_shape=[jax.ShapeDtypeStruct((a_t.shape[0], b.shape[1]), BF16) for a_t, b in pairs]
        + [jax.ShapeDtypeStruct(a.shape[1:], a.dtype) for a in swap],
        in_specs=in_specs + [ANY_SPEC] * nsw,
        out_specs=[pl.BlockSpec((a_t.shape[0], b.shape[1]), lambda k: (0, 0)) for a_t, b in pairs] + [ANY_SPEC] * nsw,
        scratch_shapes=[pltpu.SemaphoreType.DMA((nsw, N_CHIPS)), pltpu.SemaphoreType.DMA((nsw, N_CHIPS))]
        + [pltpu.VMEM((a_t.shape[0], b.shape[1]), F32) for a_t, b in pairs],
        compiler_params=_params(1))(*args, *swap)
    return outs[:npair], outs[npair:]


def _chip_partial(core, slots, theirs, tr, name):
    _, k, rws, cols = slots.shape

    def body(core_ref, a_ref, b_ref, f_ref, h_ref):
        sm = a_ref[0].astype(F32) + b_ref[...].astype(F32)
        f_ref[...] = sm
        h_ref[...] = sm.astype(BF16)

    blk = pl.BlockSpec((1, tr, cols), lambda j, i, core_ref: (j, i, 0))
    mine = pl.BlockSpec((1, 1, tr, cols), lambda j, i, core_ref: (core_ref[0], j, i, 0))
    return pl.pallas_call(
        body, name=name,
        grid_spec=pltpu.PrefetchScalarGridSpec(num_scalar_prefetch=1, grid=(k, rws // tr),
                                               in_specs=[mine, blk], out_specs=[blk, blk]),
        out_shape=[jax.ShapeDtypeStruct(theirs.shape, F32), jax.ShapeDtypeStruct(theirs.shape, BF16)],
        compiler_params=_params(2))(core, slots, theirs)


def _ada_fwd(c_all, w_shard, b_shard):
    def body(c_ref, w_ref, b_ref, o_ref):
        ch, cm, cl = [t.astype(BF16) for t in _split3(c_ref[...])]
        wh, wm, wl = [t.astype(BF16) for t in _split3(w_ref[...])]
        acc = _dot(ch, wh) + _dot(ch, wm) + _dot(cm, wh) + _dot(ch, wl) + _dot(cl, wh) + _dot(cm, wm)
        o_ref[...] = acc + b_ref[...]

    return pl.pallas_call(body, name="ada_fwd", out_shape=jax.ShapeDtypeStruct((N_DEV, ADA_SHARD), F32),
                          compiler_params=_params(0))(c_all, w_shard, b_shard)


def _rope_tables_and_gather(pos_col, inv_freq, tm, weights, c_rows):
    s = pos_col.shape[0]
    nw = len(weights)
    last = s // tm - 1

    def body(*refs):
        p_ref, f_ref = refs[:2]
        w_src, c_src = refs[2:2 + nw], refs[2 + nw:3 + nw]
        cos_ref, sa_ref, sb_ref = refs[3 + nw:6 + nw]
        w_dst, c_dst = refs[6 + nw:6 + 2 * nw], refs[6 + 2 * nw:7 + 2 * nw]
        w_sems, c_sems = refs[7 + 2 * nw:10 + 2 * nw], refs[10 + 2 * nw:]

        @pl.when(pl.program_id(0) == 0)
        def _():
            _gather2_start(w_src, w_dst, *w_sems)
            _exchange_start(_exchange_copies(c_src, c_dst, ["gather"], *c_sems))

        ang = p_ref[...] * f_ref[...]
        sin = jnp.sin(ang)
        first_half = (_lane(CHUNK) & (HEAD_DIM - 1)) < HEAD_DIM // 2
        cos_ref[...] = jnp.cos(ang)
        sa_ref[...] = jnp.where(first_half, -sin, 0.0)
        sb_ref[...] = jnp.where(first_half, 0.0, sin)

        @pl.when(pl.program_id(0) == last)
        def _():
            _exchange_wait(_exchange_copies(c_src, c_dst, ["gather"], *c_sems))
            _gather2_finish(w_src, w_dst, *w_sems)

    tab = jax.ShapeDtypeStruct((s, CHUNK), F32)
    blk = pl.BlockSpec((tm, CHUNK), lambda i: (i, 0))
    gathered = [jax.ShapeDtypeStruct((N_DEV,) + a.shape, a.dtype) for a in list(weights) + [c_rows]]
    outs = pl.pallas_call(
        body, name="rope_tables_gather", grid=(s // tm,), out_shape=[tab, tab, tab] + gathered,
        in_specs=[pl.BlockSpec((tm, 1), lambda i: (i, 0)), pl.BlockSpec((1, CHUNK), lambda i: (0, 0))]
        + [ANY_SPEC] * (nw + 1),
        out_specs=[blk, blk, blk] + [ANY_SPEC] * (nw + 1),
        scratch_shapes=_exchange_sems(nw) + _exchange_sems(1),
        compiler_params=_params(1))(pos_col, inv_freq, *weights, c_rows)
    return outs[:3], outs[3:3 + nw], outs[3 + nw]


def _fwd_proj(x, mod, w_all, cos, sa, sb, bf_row, emat, ind, tm):
    s = x.shape[0]

    def body(x_ref, mod_ref, w_ref, cos_ref, sa_ref, sb_ref, bf_ref, e_ref, ind_ref,
             h_ref, qa_ref, ka_ref, va_ref, za_ref, qb_ref, kb_ref, vb_ref, zb_ref, ga_ref, gb_ref, f_ref,
             bnd_ref, nrm_ref, carry_ref):
        @pl.when(pl.program_id(0) == 0)
        def _():
            carry_ref[...] = jnp.zeros_like(carry_ref)

        xv = x_ref[...]
        rstd = lax.rsqrt(jnp.mean(xv * xv, axis=-1, keepdims=True) + NORM_EPS)
        h = (xv * rstd * mod_ref[0:1, :]) * (1.0 + mod_ref[1:2, :]) + mod_ref[2:3, :]
        hb = h.astype(BF16)
        h_ref[...] = hb.T

        def seg(name):
            off, w = SEG[name]
            return _dot(hb, w_ref[:, off:off + w])

        lane = _lane(CHUNK)
        low = lane < HEAD_DIM
        q_ones = _ones_lanes(64, 67)
        k_ones = _ones_lanes(67, 70)
        cos_t, sa_t, sb_t = cos_ref[...], sa_ref[...], sb_ref[...]

        def write_heads(ref, nat, extra_of, rope, scale):
            for p in range(N_HEADS // 2):
                c = nat[:, CHUNK * p:CHUNK * (p + 1)]
                if rope:
                    c = _rope(c, cos_t, sa_t, sb_t)
                if scale != 1.0:
                    c = c * scale
                for hh in range(2):
                    hd = 2 * p + hh
                    src = c if hh == 0 else pltpu.roll(c, HEAD_DIM, 1)
                    ref[:, CHUNK * hd:CHUNK * (hd + 1)] = jnp.where(low, src, extra_of(hd)).astype(BF16)

        write_heads(qa_ref, seg("qa"), lambda hd: q_ones, True, Q_SCALE)
        ka = seg("ka")
        va = seg("va")
        for kv in range(KV_GROUPS):
            sl = slice(CHUNK * kv, CHUNK * (kv + 1))
            ka_ref[:, sl] = jnp.where(low, _rope(ka[:, sl], cos_t, sa_t, sb_t), k_ones).astype(BF16)
            va_ref[:, sl] = jnp.where(low, va[:, sl], q_ones).astype(BF16)
        za_ref[...] = seg("za")
        zb_ref[...] = seg("zb")
        ga_ref[...] = seg("ga")
        gb_ref[...] = seg("gb")

        xf = seg("f") + bf_ref[...]
        f_ref[...] = xf
        logf = jnp.minimum(xf, 0.0) - jnp.log1p(jnp.exp(-jnp.abs(xf)))
        row = lax.broadcasted_iota(jnp.int32, (tm, tm), 0)
        col = lax.broadcasted_iota(jnp.int32, (tm, tm), 1)
        tri = jnp.where(col <= row, 1.0, 0.0).astype(BF16)
        hi, mid, lo = _split3(logf)
        cum = _dot(tri, hi.astype(BF16)) + _dot(tri, mid.astype(BF16)) + _dot(tri, lo.astype(BF16))
        cum = cum + carry_ref[0:1, :]
        carry_ref[0:1, :] += jnp.sum(logf, axis=0, keepdims=True)
        lane_all = _lane(AUG_W) & (CHUNK - 1)
        bias = _spread3(-cum, e_ref) + jnp.where((lane_all >= 67) & (lane_all < 70), 1.0, 0.0)

        qb, kb = seg("qb"), seg("kb")
        write_heads(qb_ref, qb, lambda hd: q_ones, False, Q_SCALE)
        write_heads(kb_ref, kb, lambda hd: bias[:, CHUNK * hd:CHUNK * (hd + 1)], False, 1.0)
        write_heads(vb_ref, seg("vb"), lambda hd: q_ones, False, 1.0)

        rid = lax.broadcasted_iota(jnp.int32, (tm, CHUNK), 0)
        bnd_ref[...] = jnp.zeros_like(bnd_ref)
        bnd_ref[0, 0:1, :] = jnp.sum(jnp.where(rid == 0, cum, 0.0), axis=0, keepdims=True)
        bnd_ref[0, 1:2, :] = jnp.sum(jnp.where(rid == tm - 1, cum, 0.0), axis=0, keepdims=True)

        @pl.when(pl.program_id(0) == 0)
        def _():
            nrm_ref[...] = jnp.zeros_like(nrm_ref)

        for r_, nat in ((0, qb * Q_SCALE), (1, kb)):
            sq = _dot((nat * nat).astype(BF16), ind_ref[...])
            nrm_ref[r_:r_ + 1, :] = jnp.maximum(nrm_ref[r_:r_ + 1, :], jnp.max(sq, axis=0, keepdims=True))

    row_blk = lambda w: pl.BlockSpec((tm, w), lambda i: (i, 0))
    full = lambda a: pl.BlockSpec(a.shape, lambda i: (0,) * a.ndim)
    sds = lambda w, dt: jax.ShapeDtypeStruct((s, w), dt)
    out_shape = [sds(AUG_W, BF16), sds(KV_GROUPS * CHUNK, BF16), sds(KV_GROUPS * CHUNK, BF16),
                 sds(512, F32), sds(AUG_W, BF16), sds(AUG_W, BF16), sds(AUG_W, BF16), sds(512, F32),
                 sds(D_MODEL, F32), sds(D_MODEL, F32), sds(CHUNK, F32)]
    small = [jax.ShapeDtypeStruct((s // tm, 8, CHUNK), F32), jax.ShapeDtypeStruct((8, CHUNK), F32)]
    return pl.pallas_call(
        body, name="fwd_proj", grid=(s // tm,),
        out_shape=[jax.ShapeDtypeStruct((D_MODEL, s), BF16)] + out_shape + small,
        in_specs=[row_blk(D_MODEL), full(mod), _resident(w_all), row_blk(CHUNK), row_blk(CHUNK), row_blk(CHUNK),
                  full(bf_row), full(emat), full(ind)],
        out_specs=[pl.BlockSpec((D_MODEL, tm), lambda i: (0, i))] + [row_blk(o.shape[1]) for o in out_shape] + [
            pl.BlockSpec((1, 8, CHUNK), lambda i: (i, 0, 0)), pl.BlockSpec((8, CHUNK), lambda i: (0, 0))],
        scratch_shapes=[pltpu.VMEM((8, CHUNK), F32)],
        compiler_params=_params(1))(x, mod, w_all, cos, sa, sb, bf_row, emat, ind)


def _swa_fwd(q_aug, k_aug, v_aug, sink_rows, tq):
    s = q_aug.shape[0]
    r = tq // WINDOW
    gw = GROUP * CHUNK

    def body(q_ref, kc_ref, kp_ref, vc_ref, vp_ref, sink_ref, o_ref, qb_ref):
        i = pl.program_id(1)
        st = GROUP * WINDOW
        qloc = lax.broadcasted_iota(jnp.int32, (st, 2 * WINDOW), 0) & (WINDOW - 1)
        col = lax.broadcasted_iota(jnp.int32, (st, 2 * WINDOW), 1)
        band = (col > qloc) & (col <= qloc + WINDOW)
        head = jnp.right_shift(lax.broadcasted_iota(jnp.int32, (st, 1), 0), 7)
        sink = jnp.zeros((st, 1), F32)
        for g in range(GROUP):
            sink = jnp.where(head == g, jnp.max(sink_ref[0, g:g + 1, :], axis=1, keepdims=True), sink)
        lane = _lane(CHUNK)
        for sub in range(r):
            rows = slice(WINDOW * sub, WINDOW * (sub + 1))
            q = jnp.concatenate([q_ref[rows, CHUNK * g:CHUNK * (g + 1)] for g in range(GROUP)], axis=0)
            if sub == 0:
                k = jnp.concatenate([kp_ref[...], kc_ref[rows, :]], axis=0)
                v = jnp.concatenate([vp_ref[...], vc_ref[rows, :]], axis=0)
                valid = band & ((col >= WINDOW) | (i > 0))
            else:
                both = slice(WINDOW * (sub - 1), WINDOW * (sub + 1))
                k, v, valid = kc_ref[both, :], vc_ref[both, :], band
            sc = jnp.where(valid, _dot_nt(q, k), NEG)
            m = jnp.maximum(jnp.max(sc, axis=1, keepdims=True), sink)
            acc = _dot(jnp.exp(sc - m).astype(BF16), v)
            denom = _lane_sum(acc, 64, 65) + jnp.exp(sink - m)
            out = acc / denom
            aug = _place3(lane, 67, -(m + jnp.log(denom)), q.astype(F32)).astype(BF16)
            hrows = lambda a, g: a[WINDOW * g:WINDOW * (g + 1), :]
            for g in range(GROUP):
                qb_ref[rows, CHUNK * g:CHUNK * (g + 1)] = hrows(aug, g)
            for pp in range(GROUP // 2):
                o_ref[rows, CHUNK * pp:CHUNK * (pp + 1)] = _pair(hrows(out, 2 * pp), hrows(out, 2 * pp + 1))

    return pl.pallas_call(
        body, name="swa_fwd", grid=(KV_GROUPS, s // tq),
        out_shape=[jax.ShapeDtypeStruct((s, 512), F32), jax.ShapeDtypeStruct((s, AUG_W), BF16)],
        in_specs=[pl.BlockSpec((tq, gw), lambda kv, i: (i, kv)),
                  pl.BlockSpec((tq, CHUNK), lambda kv, i: (i, kv)),
                  pl.BlockSpec((WINDOW, CHUNK), lambda kv, i: (jnp.maximum(i * r - 1, 0), kv)),
                  pl.BlockSpec((tq, CHUNK), lambda kv, i: (i, kv)),
                  pl.BlockSpec((WINDOW, CHUNK), lambda kv, i: (jnp.maximum(i * r - 1, 0), kv)),
                  pl.BlockSpec((1, 8, CHUNK), lambda kv, i: (kv, 0, 0))],
        out_specs=[pl.BlockSpec((tq, GROUP * HEAD_DIM), lambda kv, i: (i, kv)),
                   pl.BlockSpec((tq, gw), lambda kv, i: (i, kv))],
        compiler_params=_params(2))(q_aug, k_aug, k_aug, v_aug, v_aug, sink_rows)


def _swa_bwd(k_aug, v_aug, q_bwd, do_aug, cos, sa, sb, sink_rows, tk):
    s = k_aug.shape[0]
    r = tk // WINDOW
    nt = s // tk
    nb = s // WINDOW
    gw = GROUP * CHUNK

    def body(k_ref, v_ref, q_ref, qn_ref, do_ref, don_ref, cos_ref, sa_ref, sb_ref, sink_ref,
             da_ref, dsink_ref, carry_ref, acc_ref):
        j = pl.program_id(0)

        @pl.when(j == 0)
        def _():
            carry_ref[...] = jnp.zeros_like(carry_ref)
            dsink_ref[...] = jnp.zeros_like(dsink_ref)

        acc_ref[...] = jnp.zeros_like(acc_ref)
        acc_ref[0:WINDOW, :] = carry_ref[...]
        dk_col, dv_col = SEG["ka"][0] - SEG["qa"][0], SEG["va"][0] - SEG["qa"][0]
        st = GROUP * WINDOW
        key = lax.broadcasted_iota(jnp.int32, (WINDOW, 2 * st), 0)
        col = lax.broadcasted_iota(jnp.int32, (WINDOW, 2 * st), 1)
        qloc = col & (WINDOW - 1)
        band = ((col < st) & (key <= qloc)) | ((col >= st) & (key > qloc))
        lane = _lane(CHUNK)
        low = lane < HEAD_DIM
        cos_t, sa_t, sb_t = cos_ref[...], sa_ref[...], sb_ref[...]
        for kv, sub in [(kv, sub) for kv in range(KV_GROUPS) for sub in range(r)]:
            heads = [slice(CHUNK * (GROUP * kv + g), CHUNK * (GROUP * kv + g + 1)) for g in range(GROUP)]
            kvs = slice(CHUNK * kv, CHUNK * (kv + 1))
            rows = slice(WINDOW * sub, WINDOW * (sub + 1))
            nxt = slice(WINDOW * (sub + 1), WINDOW * (sub + 2))
            k, v = k_ref[rows, kvs], v_ref[rows, kvs]
            q_cur, do_cur = [q_ref[rows, cs] for cs in heads], [do_ref[rows, cs] for cs in heads]
            if sub < r - 1:
                q_nxt, do_nxt, valid = [q_ref[nxt, cs] for cs in heads], [do_ref[nxt, cs] for cs in heads], band
            else:
                q_nxt, do_nxt = [qn_ref[:, cs] for cs in heads], [don_ref[:, cs] for cs in heads]
                valid = band & ((col < st) | (j < nt - 1))
            q = jnp.concatenate(q_cur + q_nxt, axis=0)
            do = jnp.concatenate(do_cur + do_nxt, axis=0)
            pt = jnp.exp(jnp.where(valid, _dot_nt(k, q), NEG))
            ds = (pt * _dot_nt(v, do)).astype(BF16)
            dv = _dot(pt.astype(BF16), do)
            dk = _dot(ds, q)
            dq = _dot_tn(ds, k)
            for g, cs in enumerate(heads):
                acc_ref[rows, cs] += dq[WINDOW * g:WINDOW * (g + 1), :]
                dqn = dq[st + WINDOW * g:st + WINDOW * (g + 1), :]
                if sub < r - 1:
                    acc_ref[nxt, cs] += dqn
                else:
                    carry_ref[:, cs] = dqn
                sink = jnp.max(sink_ref[kv, g:g + 1, :], axis=1, keepdims=True)
                p_sink = jnp.exp(sink + _lane_sum(q_cur[g].astype(F32), 67, 70))
                term = jnp.sum(p_sink * _lane_sum(do_cur[g].astype(F32), 64, 67), axis=0, keepdims=True)
                dsink_ref[kv, g:g + 1, :] += jnp.broadcast_to(term, (1, CHUNK))
            da_ref[rows, dk_col + CHUNK * kv:dk_col + CHUNK * (kv + 1)] = _rope_inv(
                jnp.where(low, dk, 0.0), cos_t[rows, :], sa_t[rows, :], sb_t[rows, :]).astype(BF16)
            da_ref[rows, dv_col + CHUNK * kv:dv_col + CHUNK * (kv + 1)] = jnp.where(low, dv, 0.0).astype(BF16)
        for pp in range(N_HEADS // 2):
            d = _pair(acc_ref[:, CHUNK * 2 * pp:CHUNK * (2 * pp + 1)], acc_ref[:, CHUNK * (2 * pp + 1):CHUNK * (2 * pp + 2)])
            da_ref[:, CHUNK * pp:CHUNK * (pp + 1)] = (_rope_inv(d, cos_t, sa_t, sb_t) * Q_SCALE).astype(BF16)

    cur = lambda w: pl.BlockSpec((tk, w), lambda j: (j, 0))
    nxt = pl.BlockSpec((WINDOW, AUG_W), lambda j: (jnp.minimum((j + 1) * r, nb - 1), 0))
    whole = pl.BlockSpec((KV_GROUPS, 8, CHUNK), lambda j: (0, 0, 0))
    return pl.pallas_call(
        body, name="swa_bwd", grid=(nt,),
        out_shape=[jax.ShapeDtypeStruct((s, 1024), BF16), jax.ShapeDtypeStruct((KV_GROUPS, 8, CHUNK), F32)],
        in_specs=[cur(KV_GROUPS * CHUNK), cur(KV_GROUPS * CHUNK), cur(AUG_W), nxt, cur(AUG_W), nxt,
                  cur(CHUNK), cur(CHUNK), cur(CHUNK), whole],
        out_specs=[cur(1024), whole],
        scratch_shapes=[pltpu.VMEM((WINDOW, AUG_W), F32), pltpu.VMEM((tk, AUG_W), F32)],
        compiler_params=_params(1))(k_aug, v_aug, q_bwd, q_bwd, do_aug, do_aug, cos, sa, sb, sink_rows)


def _fox_fwd(first_key, q_aug, k_aug, v_aug, t):
    s = q_aug.shape[0]
    pw = 2 * CHUNK

    def body(lo_ref, q_ref, k_ref, v_ref, o_ref, qb_ref):
        i = pl.program_id(1)
        first = lo_ref[pl.program_id(0), i]
        row = lax.broadcasted_iota(jnp.int32, (t, t), 0)
        col = lax.broadcasted_iota(jnp.int32, (t, t), 1)
        lane = _lane(CHUNK)
        heads = (slice(0, CHUNK), slice(CHUNK, pw))
        qs = [q_ref[:, hs] for hs in heads]

        def step(jb, carry, masked):
            rows = pl.ds(pl.multiple_of(jb * t, t), t)
            scs = [_dot_nt(qs[hh], k_ref[rows, hs]) for hh, hs in enumerate(heads)]
            if masked:
                scs = [jnp.where(col <= row, sc, NEG) for sc in scs]
            m_news = [jnp.maximum(carry[2 * hh], jnp.max(scs[hh], axis=1, keepdims=True)) for hh in range(2)]
            ps = [jnp.exp(scs[hh] - m_news[hh]).astype(BF16) for hh in range(2)]
            new = []
            for hh, hs in enumerate(heads):
                m, acc = carry[2 * hh], carry[2 * hh + 1]
                new += [m_news[hh], jnp.exp(m - m_news[hh]) * acc + _dot(ps[hh], v_ref[rows, hs])]
            return tuple(new)

        init = (jnp.full((t, 1), NEG, F32), jnp.zeros((t, CHUNK), F32)) * 2
        carry = step(i, lax.fori_loop(first, i, lambda jb, c: step(jb, c, False), init), True)
        outs = []
        for hh, hs in enumerate(heads):
            m, acc = carry[2 * hh], carry[2 * hh + 1]
            denom = _lane_sum(acc, 64, 65)
            outs.append(acc / denom)
            lse = m + jnp.log(denom)
            qb_ref[:, hs] = _place3(lane, 67, -lse, qs[hh].astype(F32)).astype(BF16)
        o_ref[...] = _pair(outs[0], outs[1])

    return pl.pallas_call(
        body, name="fox_fwd", grid=(N_HEADS // 2, s // t),
        out_shape=[jax.ShapeDtypeStruct((s, 512), F32), jax.ShapeDtypeStruct((s, AUG_W), BF16)],
        in_specs=[pl.BlockSpec(memory_space=pltpu.SMEM),
                  pl.BlockSpec((t, pw), lambda p, i: (i, p)),
                  pl.BlockSpec((s, pw), lambda p, i: (0, p)),
                  pl.BlockSpec((s, pw), lambda p, i: (0, p))],
        out_specs=[pl.BlockSpec((t, CHUNK), lambda p, i: (i, p)), pl.BlockSpec((t, pw), lambda p, i: (i, p))],
        compiler_params=_params(2))(first_key, q_aug, k_aug, v_aug)


def _fox_block_ranges(bounds, norms, r):
    cum_first = bounds[0::r, 0, :N_HEADS]
    cum_last = bounds[r - 1::r, 1, :N_HEADS]
    n = cum_first.shape[0]
    reach = 2.0 * 1.02 * jnp.sqrt(norms[0, :N_HEADS] * norms[1, :N_HEADS]) + UNDERFLOW
    blk = jnp.arange(n, dtype=jnp.int32)
    decay = cum_last[None, :, :] - cum_first[:, None, :]
    skip = (decay > reach[None, None, :]) & (blk[None, :, None] < blk[:, None, None])
    first = jnp.min(jnp.where(skip, n, blk[None, :, None]), axis=1)
    first = jnp.minimum(first[:, 0::2], first[:, 1::2]).T
    needed = (first[:, :, None] <= blk[None, None, :]) & (blk[None, :, None] >= blk[None, None, :])
    last = jnp.max(jnp.where(needed, blk[None, :, None], blk[None, None, :]), axis=1)
    return first.astype(jnp.int32), last.astype(jnp.int32)


def _fox_bwd(last_query, k_aug, v_aug, q_bwd, do_aug, t):
    s = k_aug.shape[0]
    n = s // t
    pw = 2 * CHUNK

    def body(hi_ref, k_ref, v_ref, q_ref, do_ref, dk_ref, dv_ref, dck_ref, dq_ref, dcq_ref, dq_scr):
        j = pl.program_id(1)
        last = hi_ref[pl.program_id(0), j]

        @pl.when(j == 0)
        def _():
            dq_scr[...] = jnp.zeros_like(dq_scr)

        row = lax.broadcasted_iota(jnp.int32, (t, t), 0)
        col = lax.broadcasted_iota(jnp.int32, (t, t), 1)
        lane = _lane(CHUNK)
        heads = (slice(0, CHUNK), slice(CHUNK, pw))
        ks = [k_ref[:, hs] for hs in heads]
        vs = [v_ref[:, hs] for hs in heads]

        def step(ib, carry, masked):
            rows = pl.ds(pl.multiple_of(ib * t, t), t)
            new = []
            for hh, hs in enumerate(heads):
                dv, dk = carry[2 * hh], carry[2 * hh + 1]
                q, do = q_ref[rows, hs], do_ref[rows, hs]
                st = _dot_nt(ks[hh], q)
                if masked:
                    st = jnp.where(row <= col, st, NEG)
                pt = jnp.exp(st)
                ds = (pt * _dot_nt(vs[hh], do)).astype(BF16)
                new += [dv + _dot(pt.astype(BF16), do), dk + _dot(ds, q)]
                dq_scr[rows, hs] += _dot_tn(ds, ks[hh])
            return tuple(new)

        zero = jnp.zeros((t, CHUNK), F32)
        carry = lax.fori_loop(j + 1, last + 1, lambda ib, c: step(ib, c, False), step(j, (zero,) * 4, True))
        dvs, dks = (carry[0], carry[2]), (carry[1], carry[3])
        dk_ref[...] = _pair(dks[0], dks[1]).astype(BF16)
        dv_ref[...] = _pair(dvs[0], dvs[1]).astype(BF16)
        dck_ref[0] = jnp.where(lane == 0, pltpu.roll(dks[0], 64, 1),
                               jnp.where(lane == 1, pltpu.roll(dks[1], 65, 1), 0.0))

        @pl.when(j == n - 1)
        def _():
            for ib in range(n):
                rows = slice(t * ib, t * (ib + 1))
                d0, d1 = dq_scr[rows, 0:CHUNK], dq_scr[rows, CHUNK:pw]
                dq_ref[rows, :] = (_pair(d0, d1) * Q_SCALE).astype(BF16)
                dcq_ref[0, rows, :] = jnp.where(lane == 0, pltpu.roll(d0, CHUNK - 67, 1),
                                                jnp.where(lane == 1, pltpu.roll(d1, CHUNK - 66, 1), 0.0))

    return pl.pallas_call(
        body, name="fox_bwd", grid=(N_HEADS // 2, n),
        out_shape=[jax.ShapeDtypeStruct((s, 512), BF16), jax.ShapeDtypeStruct((s, 512), BF16),
                   jax.ShapeDtypeStruct((N_HEADS // 2, s, CHUNK), F32), jax.ShapeDtypeStruct((s, 512), BF16),
                   jax.ShapeDtypeStruct((N_HEADS // 2, s, CHUNK), F32)],
        in_specs=[pl.BlockSpec(memory_space=pltpu.SMEM),
                  pl.BlockSpec((t, pw), lambda p, j: (j, p)), pl.BlockSpec((t, pw), lambda p, j: (j, p)),
                  pl.BlockSpec((s, pw), lambda p, j: (0, p)), pl.BlockSpec((s, pw), lambda p, j: (0, p))],
        out_specs=[pl.BlockSpec((t, CHUNK), lambda p, j: (j, p)), pl.BlockSpec((t, CHUNK), lambda p, j: (j, p)),
                   pl.BlockSpec((1, t, CHUNK), lambda p, j: (p, j, 0)), pl.BlockSpec((s, CHUNK), lambda p, j: (0, p)),
                   pl.BlockSpec((1, s, CHUNK), lambda p, j: (p, 0, 0))],
        scratch_shapes=[pltpu.VMEM((s, pw), F32)],
        compiler_params=_params(2))(last_query, k_aug, v_aug, q_bwd, do_aug)


def _fgate_bwd(dcq, dck, xf, tm):
    s = xf.shape[0]
    nt = s // tm

    def body(dq_ref, dc_ref, xf_ref, df_ref, dbf_ref, carry_ref):
        @pl.when(pl.program_id(0) == 0)
        def _():
            carry_ref[...] = jnp.zeros_like(carry_ref)
            dbf_ref[...] = jnp.zeros_like(dbf_ref)

        row = lax.broadcasted_iota(jnp.int32, (tm, tm), 0)
        col = lax.broadcasted_iota(jnp.int32, (tm, tm), 1)
        tri = jnp.where(col >= row, 1.0, 0.0).astype(BF16)
        dcum = dq_ref[0] - dc_ref[0]
        for p in range(1, N_HEADS // 2):
            dcum = dcum + pltpu.roll(dq_ref[p] - dc_ref[p], 2 * p, 1)
        hi, mid, lo = _split3(dcum)
        dlogf = _dot(tri, hi.astype(BF16)) + _dot(tri, mid.astype(BF16)) + _dot(tri, lo.astype(BF16))
        dlogf = dlogf + carry_ref[0:1, :]
        carry_ref[0:1, :] += jnp.sum(dcum, axis=0, keepdims=True)
        df = dlogf * _sigmoid(-xf_ref[...])
        df_ref[...] = df.astype(BF16)
        dbf_ref[0:1, :] += jnp.sum(df, axis=0, keepdims=True)

    rev = pl.BlockSpec((tm, CHUNK), lambda i: (nt - 1 - i, 0))
    rev4 = pl.BlockSpec((N_HEADS // 2, tm, CHUNK), lambda i: (0, nt - 1 - i, 0))
    return pl.pallas_call(
        body, name="fgate_bwd", grid=(nt,),
        out_shape=[jax.ShapeDtypeStruct((s, CHUNK), BF16), jax.ShapeDtypeStruct((8, CHUNK), F32)],
        in_specs=[rev4, rev4, rev], out_specs=[rev, pl.BlockSpec((8, CHUNK), lambda i: (0, 0))],
        scratch_shapes=[pltpu.VMEM((8, CHUNK), F32)],
        compiler_params=_params(1))(dcq, dck, xf)


def _post(x, tgt, att_a, att_b, za, zb, ga, gb, woa_t, wob_t, wout, vec, ind, emat, tm):
    s = x.shape[0]

    def body(x_ref, t_ref, aa_ref, ab_ref, za_ref, zb_ref, ga_ref, gb_ref, woat_ref, wobt_ref, wout_ref,
             vec_ref, ind_ref, e_ref,
             dx1_ref, ua_ref, ub_ref, mg_ref, do_ref, dya_ref, dyb_ref, doa_ref, dob_ref, dzg_ref, acc_ref):
        @pl.when(pl.program_id(0) == 0)
        def _():
            acc_ref[...] = jnp.zeros_like(acc_ref)

        gate, gfin = vec_ref[0:1, :], vec_ref[1:2, :]
        inv_d = 1.0 / D_MODEL

        def branch_fwd(att_ref, z_ref, w_ref, u_ref):
            att, z = att_ref[...], z_ref[...]
            sz = _sigmoid(z)
            silu = z * sz
            u = (att * silu).astype(BF16)
            u_ref[...] = u.T
            return att, z, sz, silu, _dot_nt(u, w_ref[...])

        att_a, z_a, sz_a, silu_a, y_a = branch_fwd(aa_ref, za_ref, woat_ref, ua_ref)
        att_b, z_b, sz_b, silu_b, y_b = branch_fwd(ab_ref, zb_ref, wobt_ref, ub_ref)
        sg_a, sg_b = _sigmoid(ga_ref[...]), _sigmoid(gb_ref[...])
        merged = (sg_a * y_a + sg_b * y_b).astype(BF16)
        mg_ref[...] = merged.T
        o = _dot(merged, wout_ref[...])
        x1 = x_ref[...] + gate * o
        rstd = lax.rsqrt(jnp.mean(x1 * x1, axis=-1, keepdims=True) + NORM_EPS)
        xh = x1 * rstd
        diff = xh * gfin - t_ref[...]
        acc_ref[2:3, :] += (0.5 * inv_d) * jnp.sum(diff * diff, axis=0, keepdims=True)
        dy = diff * inv_d
        acc_ref[1:2, :] += jnp.sum(dy * xh, axis=0, keepdims=True)
        dyg = dy * gfin
        dx1 = rstd * (dyg - xh * jnp.mean(dyg * xh, axis=-1, keepdims=True))
        dx1_ref[...] = dx1
        acc_ref[0:1, :] += jnp.sum(dx1 * o, axis=0, keepdims=True)
        d_o = (dx1 * gate).astype(BF16)
        do_ref[...] = d_o
        dmg = _dot_nt(d_o, wout_ref[...])

        lane = _lane(CHUNK)
        low = lane < HEAD_DIM
        lane_all = _lane(AUG_W) & (CHUNK - 1)

        zg0 = SEG["za"][0]

        def branch_bwd(sg, y, att, z, sz, silu, wt_ref, dy_ref, g_name, z_name, doaug_ref):
            dyb = (dmg * sg).astype(BF16)
            dy_ref[...] = dyb
            g_off, z_off = SEG[g_name][0] - zg0, SEG[z_name][0] - zg0
            dzg_ref[:, g_off:g_off + D_MODEL] = (dmg * y * sg * (1.0 - sg)).astype(BF16)
            du = _dot(dyb, wt_ref[...])
            datt = du * silu
            dzg_ref[:, z_off:z_off + 512] = (du * att * (sz * (1.0 + z * (1.0 - sz)))).astype(BF16)
            extra = _spread3(-_dot_split(datt * att, ind_ref[...]), e_ref)
            for p in range(N_HEADS // 2):
                c = datt[:, CHUNK * p:CHUNK * (p + 1)]
                for hh in range(2):
                    hd = 2 * p + hh
                    src = c if hh == 0 else pltpu.roll(c, HEAD_DIM, 1)
                    doaug_ref[:, CHUNK * hd:CHUNK * (hd + 1)] = jnp.where(
                        low, src, extra[:, CHUNK * hd:CHUNK * (hd + 1)]).astype(BF16)

        branch_bwd(sg_a, y_a, att_a, z_a, sz_a, silu_a, woat_ref, dya_ref, "ga", "za", doa_ref)
        branch_bwd(sg_b, y_b, att_b, z_b, sz_b, silu_b, wobt_ref, dyb_ref, "gb", "zb", dob_ref)

    row_blk = lambda w: pl.BlockSpec((tm, w), lambda i: (i, 0))
    full = lambda a: pl.BlockSpec(a.shape, lambda i: (0,) * a.ndim)
    sds = lambda w, dt: jax.ShapeDtypeStruct((s, w), dt)
    tds = lambda w: jax.ShapeDtypeStruct((w, s), BF16)
    col_blk = lambda w: pl.BlockSpec((w, tm), lambda i: (0, i))
    out_shape = [sds(D_MODEL, F32), tds(512), tds(512), tds(D_MODEL), sds(D_MODEL, BF16),
                 sds(D_MODEL, BF16), sds(D_MODEL, BF16), sds(AUG_W, BF16), sds(AUG_W, BF16), sds(3072, BF16),
                 jax.ShapeDtypeStruct((8, D_MODEL), F32)]
    ins = [x, tgt, att_a, att_b, za, zb, ga, gb, woa_t, wob_t, wout, vec, ind, emat]
    in_specs = [row_blk(a.shape[1]) for a in ins[:8]] + [full(a) for a in ins[8:]]
    out_specs = ([row_blk(D_MODEL), col_blk(512), col_blk(512), col_blk(D_MODEL)]
                 + [row_blk(o.shape[1]) for o in out_shape[4:-1]] + [pl.BlockSpec((8, D_MODEL), lambda i: (0, 0))])
    return pl.pallas_call(body, name="post", grid=(s // tm,), out_shape=out_shape, in_specs=in_specs,
                          out_specs=out_specs, compiler_params=_params(1))(*ins)


def _bwd_pre(dproj, w_all, x, dx1, mod, tm, chip_halves, modes):
    s = x.shape[0]
    ng, nh = len(DPROJ_GROUPS), len(chip_halves)
    last = s // tm - 1

    def body(*refs):
        dp_refs = refs[:ng]
        wt_ref, x_ref, dx1_ref, mod_ref = refs[ng:ng + 4]
        send_refs = refs[ng + 4:ng + 4 + nh]
        gx_ref, acc_ref = refs[ng + 4 + nh:ng + 6 + nh]
        recv_refs = refs[ng + 6 + nh:ng + 6 + 2 * nh]
        sems = refs[ng + 6 + 2 * nh:]

        @pl.when(pl.program_id(0) == 0)
        def _():
            acc_ref[...] = jnp.zeros_like(acc_ref)
            _exchange_start(_exchange_copies(send_refs, recv_refs, modes, *sems))

        dh = None
        for dp_ref, (_, off, w) in zip(dp_refs, DPROJ_GROUPS):
            part = _dot_nt(dp_ref[...], wt_ref[:, off:off + w])
            dh = part if dh is None else dh + part
        xv = x_ref[...]
        rstd = lax.rsqrt(jnp.mean(xv * xv, axis=-1, keepdims=True) + NORM_EPS)
        xh = xv * rstd
        gn = mod_ref[0:1, :]
        acc_ref[0:1, :] += jnp.sum(dh, axis=0, keepdims=True)
        acc_ref[1:2, :] += jnp.sum(dh * (xh * gn), axis=0, keepdims=True)
        dn = dh * (1.0 + mod_ref[1:2, :])
        acc_ref[2:3, :] += jnp.sum(dn * xh, axis=0, keepdims=True)
        dxh = dn * gn
        gx_ref[...] = dx1_ref[...] + rstd * (dxh - xh * jnp.mean(dxh * xh, axis=-1, keepdims=True))

        @pl.when(pl.program_id(0) == last)
        def _():
            _exchange_wait(_exchange_copies(send_refs, recv_refs, modes, *sems))

    row_blk = lambda w: pl.BlockSpec((tm, w), lambda i: (i, 0))
    full = lambda a: pl.BlockSpec(a.shape, lambda i: (0,) * a.ndim)
    outs = pl.pallas_call(
        body, name="bwd_pre", grid=(s // tm,),
        out_shape=[jax.ShapeDtypeStruct((s, D_MODEL), F32), jax.ShapeDtypeStruct((8, D_MODEL), F32)]
        + _exchange_out_shapes(chip_halves, modes),
        in_specs=[row_blk(w) for _, _, w in DPROJ_GROUPS] + [_resident(w_all), row_blk(D_MODEL), row_blk(D_MODEL), full(mod)]
        + [ANY_SPEC] * nh,
        out_specs=[row_blk(D_MODEL), pl.BlockSpec((8, D_MODEL), lambda i: (0, 0))] + [ANY_SPEC] * nh,
        scratch_shapes=_exchange_sems(nh),
        compiler_params=_params(1))(*dproj, w_all, x, dx1, mod, *chip_halves)
    return outs[0], outs[1], outs[2:]


def _accumulate_tokens(a, b_ref, acc_ref, o_ref, last):
    @pl.when(pl.program_id(0) == 0)
    def _():
        acc_ref[...] = jnp.zeros_like(acc_ref)

    acc_ref[...] += _dot(a, b_ref[...])

    @pl.when(pl.program_id(0) == last)
    def _():
        o_ref[...] = acc_ref[...].astype(BF16)


def _token_matmul_multi(a_t, bs, tk, name):
    m, s = a_t.shape
    nb = len(bs)
    last = s // tk - 1

    def body(*refs):
        a_ref, b_refs, o_refs, acc_refs = refs[0], refs[1:1 + nb], refs[1 + nb:1 + 2 * nb], refs[1 + 2 * nb:]
        a = a_ref[...]
        for b_ref, o_ref, acc_ref in zip(b_refs, o_refs, acc_refs):
            _accumulate_tokens(a, b_ref, acc_ref, o_ref, last)

    return pl.pallas_call(
        body, name=name, grid=(s // tk,),
        out_shape=[jax.ShapeDtypeStruct((m, b.shape[1]), BF16) for b in bs],
        in_specs=[pl.BlockSpec((m, tk), lambda k: (0, k))] + [pl.BlockSpec((tk, b.shape[1]), lambda k: (k, 0)) for b in bs],
        out_specs=[pl.BlockSpec((m, b.shape[1]), lambda k: (0, 0)) for b in bs],
        scratch_shapes=[pltpu.VMEM((m, b.shape[1]), F32) for b in bs],
        compiler_params=_params(1))(a_t, *bs)


def _adam_math(g, w, m, v):
    m2 = ADAM_B1 * m + (1.0 - ADAM_B1) * g
    v2 = ADAM_B2 * v + (1.0 - ADAM_B2) * (g * g)
    delta = -ADAM_LR * ((m2 / ADAM_C1) / (jnp.sqrt(v2 / ADAM_C2) + ADAM_EPS) + ADAM_WD * w)
    return delta, m2, v2


def _adamw_parts(full, parts, others, w, m, v, tr, name):
    _, rws, cols = w.shape

    def body(oth_ref, f_ref, p_ref, w_ref, m_ref, v_ref, g_ref, d_ref, m2_ref, v2_ref):
        g = None
        for chip in range(N_CHIPS):
            part = (1.0 - oth_ref[chip]) * f_ref[chip] + oth_ref[chip] * p_ref[chip].astype(F32)
            g = part if g is None else g + part
        g_ref[0] = g
        d_ref[0], m2_ref[0], v2_ref[0] = _adam_math(g, w_ref[0], m_ref[0], v_ref[0])

    blk = pl.BlockSpec((1, tr, cols), lambda i: (0, i, 0))
    slots = pl.BlockSpec((N_CHIPS, tr, cols), lambda i: (0, i, 0))
    o = jax.ShapeDtypeStruct((1, rws, cols), F32)
    return pl.pallas_call(
        body, name=name, grid=(rws // tr,), out_shape=[o, o, o, o],
        in_specs=[pl.BlockSpec(memory_space=pltpu.SMEM), slots, slots, blk, blk, blk],
        out_specs=[blk, blk, blk, blk], compiler_params=_params(1))(others, full, parts, w, m, v)


def _adamw_devices(parts, w, m, v, tr, name):
    _, rws, cols = w.shape

    def body(p_ref, w_ref, m_ref, v_ref, g_ref, d_ref, m2_ref, v2_ref):
        g = p_ref[0].astype(F32)
        for dev in range(1, N_DEV):
            g = g + p_ref[dev].astype(F32)
        g_ref[0] = g
        d_ref[0], m2_ref[0], v2_ref[0] = _adam_math(g, w_ref[0], m_ref[0], v_ref[0])

    blk = pl.BlockSpec((1, tr, cols), lambda i: (0, i, 0))
    o = jax.ShapeDtypeStruct((1, rws, cols), F32)
    return pl.pallas_call(
        body, name=name, grid=(rws // tr,), out_shape=[o, o, o, o],
        in_specs=[pl.BlockSpec((N_DEV, tr, cols), lambda i: (0, i, 0)), blk, blk, blk],
        out_specs=[blk, blk, blk, blk], compiler_params=_params(1))(parts, w, m, v)


def _adamw_ada(c_lanes, d_rows, w, m, v, tr):
    _, rws, cols = w.shape

    def body(c_ref, d_ref, w_ref, m_ref, v_ref, g_ref, dl_ref, m2_ref, v2_ref):
        for k in range(cols // CHUNK):
            cs = slice(CHUNK * k, CHUNK * (k + 1))
            g = c_ref[0] * d_ref[0:1, cs]
            for b in range(1, N_DEV):
                g = g + c_ref[b] * d_ref[b:b + 1, cs]
            g_ref[0, :, cs] = g
            dl_ref[0, :, cs], m2_ref[0, :, cs], v2_ref[0, :, cs] = _adam_math(
                g, w_ref[0, :, cs], m_ref[0, :, cs], v_ref[0, :, cs])

    blk = pl.BlockSpec((1, tr, cols), lambda i: (0, i, 0))
    o = jax.ShapeDtypeStruct((1, rws, cols), F32)
    return pl.pallas_call(
        body, name="adamw_ada", grid=(rws // tr,), out_shape=[o, o, o, o],
        in_specs=[pl.BlockSpec((N_DEV, tr, CHUNK), lambda i: (0, i, 0)), pl.BlockSpec((N_DEV, cols), lambda i: (0, 0)),
                  blk, blk, blk],
        out_specs=[blk, blk, blk, blk], compiler_params=_params(1))(c_lanes, d_rows, w, m, v)


SMALL = (("b_ada", 0, 0, 3 * D_MODEL), ("g_norm", 1, 0, D_MODEL), ("b_f", 3, 0, N_HEADS), ("sinks", 4, 0, N_HEADS),
         ("g_final", 1, D_MODEL, D_MODEL))
LOSS_ROW = 2


def _pack_small(pre_acc, post_acc, dbf_acc, dsink):
    def body(pre_ref, post_ref, dbf_ref, dsink_ref, o_ref):
        o_ref[...] = jnp.zeros_like(o_ref)
        d = D_MODEL
        o_ref[0:1, 0:d], o_ref[0:1, d:2 * d], o_ref[0:1, 2 * d:3 * d] = pre_ref[0:1, :], pre_ref[1:2, :], post_ref[0:1, :]
        o_ref[1:2, 0:d], o_ref[1:2, d:2 * d] = pre_ref[2:3, :], post_ref[1:2, :]
        o_ref[LOSS_ROW:LOSS_ROW + 1, 0:d] = post_ref[2:3, :]
        o_ref[3:4, 0:CHUNK] = dbf_ref[0:1, :]
        lane = _lane(CHUNK)
        sinks = jnp.zeros((1, CHUNK), F32)
        for kv in range(KV_GROUPS):
            for g in range(GROUP):
                sinks = jnp.where(lane == GROUP * kv + g, dsink_ref[kv, g:g + 1, :], sinks)
        o_ref[4:5, 0:CHUNK] = sinks

    return pl.pallas_call(body, name="pack_small", out_shape=jax.ShapeDtypeStruct((8, 3 * D_MODEL), F32),
                          compiler_params=_params(0))(pre_acc, post_acc, dbf_acc, dsink)


def _adamw_small(packs, params):
    n = len(SMALL)

    def body(*refs):
        p_ref, wmv = refs[0], refs[1:1 + 3 * n]
        outs, loss_ref, tot_ref = refs[1 + 3 * n:1 + 7 * n], refs[1 + 7 * n], refs[2 + 7 * n]
        tot = p_ref[0]
        for dev in range(1, N_DEV):
            tot = tot + p_ref[dev]
        tot_ref[...] = tot
        for i, (_, row, lo, w) in enumerate(SMALL):
            g = tot_ref[row:row + 1, lo:lo + w]
            o = outs[4 * i:4 * i + 4]
            o[0][...] = g
            o[1][...], o[2][...], o[3][...] = _adam_math(g, wmv[3 * i][...], wmv[3 * i + 1][...], wmv[3 * i + 2][...])
        loss_ref[...] = jnp.broadcast_to(jnp.sum(tot_ref[LOSS_ROW:LOSS_ROW + 1, 0:D_MODEL], axis=1, keepdims=True),
                                         loss_ref.shape)

    flat = [a for wmv in params for a in wmv]
    out_shape = [jax.ShapeDtypeStruct(wmv[0].shape, F32) for wmv in params for _ in range(4)]
    res = pl.pallas_call(body, name="adamw_small", out_shape=out_shape + [jax.ShapeDtypeStruct((8, CHUNK), F32)],
                         scratch_shapes=[pltpu.VMEM((8, 3 * D_MODEL), F32)],
                         compiler_params=_params(0))(packs, *flat)
    return [tuple(res[4 * i:4 * i + 4]) for i in range(n)], res[4 * n]


def _tile(s, want):
    return min(s, want)


def _layout_pieces():
    orig = {"qa": 0, "ka": 512, "va": 640, "za": 768, "qb": 1280, "kb": 1792, "vb": 2304, "f": 2816, "zb": 2824,
            "ga": 3336, "gb": 4360}
    pieces = []
    for name, (off, w) in SEG.items():
        if name in ("ka", "va"):
            pieces += [(orig[name] + HEAD_DIM * kv, orig[name] + HEAD_DIM * (kv + 1), off + CHUNK * kv) for kv in range(KV_GROUPS)]
        else:
            pieces.append((orig[name], orig[name] + (N_HEADS if name == "f" else w), off))
    return pieces


def _assemble_w_all(win_g):
    cols, pos = [], 0
    for lo, hi, new in sorted(_layout_pieces(), key=lambda t: t[2]):
        if new > pos:
            cols.append(jnp.zeros((D_MODEL, new - pos), win_g.dtype))
        col = lo
        while col < hi:
            dev = col // IN_SHARD
            end = min(hi, (dev + 1) * IN_SHARD)
            cols.append(win_g[dev, :, col - dev * IN_SHARD:end - dev * IN_SHARD])
            col = end
        pos = new + hi - lo
    cols.append(jnp.zeros((D_MODEL, PROJ_W - pos), win_g.dtype))
    return jnp.concatenate(cols, axis=1)


def _grad_slot(dw_groups, dev):
    lo_d, hi_d = dev * IN_SHARD, (dev + 1) * IN_SHARD
    cols = []
    for lo, hi, new in sorted(_layout_pieces()):
        a, b = max(lo, lo_d), min(hi, hi_d)
        if a < b:
            arr, off = next((g, o) for g, (_, o, w) in zip(dw_groups, DPROJ_GROUPS) if o <= new < o + w)
            cols.append(arr[:, new - off + a - lo:new - off + b - lo])
    return jnp.concatenate(cols, axis=1)


def kernel(x, c, positions, w_ada, b_ada, g_norm, w_in, b_f, sinks, w_o_swa, w_o_fox, w_out, g_final, loss_target, m_w_ada, m_b_ada, m_g_norm, m_w_in, m_b_f, m_sinks, m_w_o_swa, m_w_o_fox, m_w_out, m_g_final, v_w_ada, v_b_ada, v_g_norm, v_w_in, v_b_f, v_sinks, v_w_o_swa, v_w_o_fox, v_w_out, v_g_final):
    s = x.shape[1]
    tm = _tile(s, 256)
    ta = _tile(s, 512)
    me = 4 * lax.axis_index("x") + 2 * lax.axis_index("y") + lax.axis_index("c")
    x2, tgt = x[0], loss_target[0]

    inv_freq = np.power(np.float32(ROPE_THETA), -np.arange(0, HEAD_DIM, 2, dtype=np.float32) / HEAD_DIM)
    inv_freq = jnp.asarray(np.tile(inv_freq, CHUNK // (HEAD_DIM // 2))[None, :], F32)
    (cos, sa, sb), (win_g, woa_g, wob_g, wout_g), c_all = _rope_tables_and_gather(
        positions.reshape(s, 1).astype(F32), inv_freq, tm,
        [w_in[0].astype(BF16), w_o_swa[0].T.astype(BF16), w_o_fox[0].T.astype(BF16), w_out[0].astype(BF16)],
        jnp.broadcast_to(c, (8, D_MODEL)))
    c_all = c_all[:, 0, :]
    b_shard = lax.dynamic_slice(b_ada, (0, me * ADA_SHARD), (1, ADA_SHARD))
    ada_part = _ada_fwd(c_all, w_ada[0], b_shard)
    (ada_g,) = _exchange([ada_part], ["gather"], "gather_ada")
    ada = lax.dynamic_index_in_dim(ada_g, me, axis=1, keepdims=False).reshape(3 * D_MODEL)
    shift, scale, gate = ada[:D_MODEL], ada[D_MODEL:2 * D_MODEL], ada[2 * D_MODEL:]
    woa_t = woa_g.reshape(D_MODEL, 512)
    wob_t = wob_g.reshape(D_MODEL, 512)
    wout = wout_g.reshape(D_MODEL, D_MODEL)
    w_all = _assemble_w_all(win_g)

    zrow = jnp.zeros((1, D_MODEL), F32)
    mod = jnp.concatenate([g_norm, scale[None], shift[None], zrow, zrow, zrow, zrow, zrow], axis=0)
    bf_row = jnp.pad(b_f, ((0, 0), (0, CHUNK - N_HEADS)))
    emat_np = np.zeros((3 * CHUNK, AUG_W), np.float32)
    ind_np = np.zeros((512, CHUNK), np.float32)
    for hd in range(N_HEADS):
        for part in range(3):
            emat_np[CHUNK * part + hd, CHUNK * hd + 64 + part] = 1.0
        ind_np[HEAD_DIM * hd:HEAD_DIM * (hd + 1), hd] = 1.0
    emat, ind = jnp.asarray(emat_np, BF16), jnp.asarray(ind_np, BF16)
    sink_rows = jnp.broadcast_to(jnp.pad(sinks.reshape(KV_GROUPS, GROUP), ((0, 0), (0, 8 - GROUP)))[:, :, None],
                                 (KV_GROUPS, 8, CHUNK))

    (h_t, qa, ka, va, za, qb, kb, vb, zb, ga, gb, xf, bounds, norms) = _fwd_proj(
        x2, mod, w_all, cos, sa, sb, bf_row, emat, ind, ta)
    first_key, last_query = _fox_block_ranges(bounds, norms, 1)
    att_a, qa_bwd = _swa_fwd(qa, ka, va, sink_rows, ta)
    att_b, qb_bwd = _fox_fwd(first_key, qb, kb, vb, ta)
    vec = jnp.concatenate([gate[None], g_final[None], zrow, zrow, zrow, zrow, zrow, zrow], axis=0)
    (dx1, ua_t, ub_t, merged_t, d_o, dya, dyb, doa, dob, dzg, post_acc) = _post(
        x2, tgt, att_a, att_b, za, zb, ga, gb, woa_t, wob_t, wout, vec, ind, emat, tm)

    da, dsink = _swa_bwd(ka, va, qa_bwd, doa, cos, sa, sb, sink_rows, ta)
    dkb, dvb, dck, dqb, dcq = _fox_bwd(last_query, kb, vb, qb_bwd, dob, ta)
    dfb, dbf_acc = _fgate_bwd(dcq, dck, xf, ta)
    dproj = [da, dqb, dkb, dvb, dzg, dfb]
    dw_zg, = _token_matmul_multi(h_t, [dzg], ta, "dw_in_zg")
    dw_a, dw_qb, dw_kb, dw_vb, dw_f = _token_matmul_multi(h_t, [da, dqb, dkb, dvb, dfb], ta, "dw_in_rest")
    dw_all = [dw_a, dw_qb, dw_kb, dw_vb, dw_zg, dw_f]

    core, chip = lax.axis_index("c"), 2 * lax.axis_index("x") + lax.axis_index("y")
    win_slots = jnp.stack([jnp.stack([_grad_slot(dw_all, 2 * ch + co) for ch in range(N_CHIPS)]) for co in range(2)])
    (dwoa, dwob, dwout), (win_theirs,) = _token_matmuls_and_swap(
        [(ua_t, dya), (ub_t, dyb), (merged_t, d_o)], ta, [win_slots])
    win_full, win_half = _chip_partial(jnp.reshape(core, (1,)).astype(jnp.int32), win_slots, win_theirs, 256,
                                       "chip_partial_w_in")
    col_slots = lambda g: g.reshape(512, N_DEV, 128).transpose(1, 0, 2)
    grad_x, pre_acc, (p_win, p_woa, p_wob, p_wout) = _bwd_pre(
        dproj, w_all, x2, dx1, mod, ta,
        [win_half, col_slots(dwoa), col_slots(dwob), dwout.reshape(N_DEV, 128, D_MODEL)],
        ["chips", "devices", "devices", "devices"])
    (packs,) = _exchange([_pack_small(pre_acc, post_acc, dbf_acc, dsink)], ["gather"], "gather_small_grads")
    others = jnp.where(jnp.arange(N_CHIPS) == chip, 0.0, 1.0).astype(F32)

    g_win, d_win, m_win, v_win = _adamw_parts(win_full, p_win, others, w_in, m_w_in, v_w_in, 128, "adamw_w_in")
    g_woa, d_woa, m_woa, v_woa = _adamw_devices(p_woa, w_o_swa, m_w_o_swa, v_w_o_swa, 128, "adamw_w_o_swa")
    g_wob, d_wob, m_wob, v_wob = _adamw_devices(p_wob, w_o_fox, m_w_o_fox, v_w_o_fox, 128, "adamw_w_o_fox")
    g_wout, d_wout, m_wout, v_wout = _adamw_devices(p_wout, w_out, m_w_out, v_w_out, 128, "adamw_w_out")
    d_ada_rows = lax.dynamic_slice(packs[:, 0, :], (0, me * ADA_SHARD), (N_DEV, ADA_SHARD))
    c_lanes = jnp.broadcast_to(c_all[:, :, None], (N_DEV, D_MODEL, CHUNK))
    g_wada, d_wada, m_wada, v_wada = _adamw_ada(c_lanes, d_ada_rows, w_ada, m_w_ada, v_w_ada, 256)

    row = lambda a: a.reshape(1, D_MODEL)
    small, loss_rows = _adamw_small(packs, [(b_ada, m_b_ada, v_b_ada), (g_norm, m_g_norm, v_g_norm), (b_f, m_b_f, v_b_f),
                                            (sinks, m_sinks, v_sinks), (row(g_final), row(m_g_final), row(v_g_final))])
    bada_o, gn_o, bf_o, sk_o, gf_o = small

    outs = []
    for k, big in enumerate(((g_wada, g_win, g_woa, g_wob, g_wout), (d_wada, d_win, d_woa, d_wob, d_wout),
                             (m_wada, m_win, m_woa, m_wob, m_wout), (v_wada, v_win, v_woa, v_wob, v_wout))):
        wada_o, win_o, woa_o, wob_o, wout_o = big
        outs += [wada_o, bada_o[k], gn_o[k], win_o, bf_o[k], sk_o[k], woa_o, wob_o, wout_o, gf_o[k].reshape(D_MODEL)]
    return (loss_rows[0, 0], grad_x[None], *outs)
```

```python
import numpy as np
import jax
import jax.numpy as jnp
from jax import lax
from jax.experimental import pallas as pl
from jax.experimental.pallas import tpu as pltpu

F32 = jnp.float32
BF16 = jnp.bfloat16

D_MODEL = 1024
HEAD_DIM = 64
N_HEADS = 8
KV_GROUPS = 2
GROUP = N_HEADS // KV_GROUPS
WINDOW = 128
CHUNK = 128
AUG_W = N_HEADS * CHUNK
N_DEV = 8
IN_SHARD = 673
ADA_SHARD = 384
NORM_EPS = 1e-6
ROPE_THETA = 10000.0
Q_SCALE = HEAD_DIM ** -0.5
NEG = -1e30
UNDERFLOW = 105.0

ADAM_LR = 0.001
ADAM_B1 = 0.9
ADAM_B2 = 0.999
ADAM_EPS = 1e-08
ADAM_WD = 0.01
ADAM_STEP = 10
ADAM_C1 = 1.0 - ADAM_B1 ** ADAM_STEP
ADAM_C2 = 1.0 - ADAM_B2 ** ADAM_STEP

SEG = {}
_off = 0
for _name, _w in (("qa", 512), ("ka", 256), ("va", 256), ("qb", 512), ("kb", 512), ("vb", 512),
                  ("za", 512), ("zb", 512), ("ga", 1024), ("gb", 1024), ("f", 128)):
    SEG[_name] = (_off, _w)
    _off += _w
PROJ_W = _off
DPROJ_GROUPS = (("a", SEG["qa"][0], 1024), ("qb", SEG["qb"][0], 512), ("kb", SEG["kb"][0], 512),
                ("vb", SEG["vb"][0], 512), ("zg", SEG["za"][0], 3072), ("f", SEG["f"][0], 128))

VMEM_LIMIT = 56 * 1024 * 1024


def _params(n_axes):
    return pltpu.CompilerParams(dimension_semantics=("arbitrary",) * n_axes, vmem_limit_bytes=VMEM_LIMIT)


def _resident(a):
    return pl.BlockSpec(a.shape, lambda i: (0, 0), pipeline_mode=pl.Buffered(1))


def _dot(a, b):
    return jnp.dot(a, b, preferred_element_type=F32)


def _dot_nt(a, b):
    return lax.dot_general(a, b, (((1,), (1,)), ((), ())), preferred_element_type=F32)


def _dot_tn(a, b):
    return lax.dot_general(a, b, (((0,), (0,)), ((), ())), preferred_element_type=F32)


def _lane(n):
    return lax.broadcasted_iota(jnp.int32, (1, n), 1)


def _split3(x):
    hi = x.astype(BF16).astype(F32)
    r = x - hi
    mid = r.astype(BF16).astype(F32)
    lo = (r - mid).astype(BF16).astype(F32)
    return hi, mid, lo


def _dot_split(x, b):
    hi, mid, lo = _split3(x)
    return _dot(hi.astype(BF16), b) + _dot(mid.astype(BF16), b) + _dot(lo.astype(BF16), b)


def _spread3(x, e3_ref):
    return _dot(jnp.concatenate(_split3(x), axis=1).astype(BF16), e3_ref[...])


def _place3(lane, base, x, other):
    hi, mid, lo = _split3(x)
    return jnp.where(lane == base, hi, jnp.where(lane == base + 1, mid, jnp.where(lane == base + 2, lo, other)))


def _lane_sum(x, lo, hi):
    lane = _lane(x.shape[1])
    return jnp.sum(jnp.where((lane >= lo) & (lane < hi), x, 0.0), axis=1, keepdims=True)


def _ones_lanes(lo, hi):
    lane = _lane(CHUNK)
    return jnp.where((lane >= lo) & (lane < hi), 1.0, 0.0).astype(F32)


def _rope(c, cos, sa, sb):
    return c * cos + pltpu.roll(c, CHUNK - 32, 1) * sa + pltpu.roll(c, 32, 1) * sb


def _rope_inv(d, cos, sa, sb):
    return d * cos - (pltpu.roll(d, CHUNK - 32, 1) * sa + pltpu.roll(d, 32, 1) * sb)


def _pair(c0, c1):
    return jnp.where(_lane(CHUNK) < HEAD_DIM, c0, pltpu.roll(c1, HEAD_DIM, 1))


def _sigmoid(x):
    return 0.5 * jnp.tanh(0.5 * x) + 0.5


N_CHIPS = 4
ANY_SPEC = pl.BlockSpec(memory_space=pl.ANY)


def _exchange(arrays, modes, name):
    n = len(arrays)

    def body(*refs):
        copies = _exchange_copies(refs[:n], refs[n:2 * n], modes, *refs[2 * n:])
        _exchange_start(copies)
        _exchange_wait(copies)

    return pl.pallas_call(
        body, name=name, out_shape=_exchange_out_shapes(arrays, modes),
        in_specs=[ANY_SPEC] * n, out_specs=[ANY_SPEC] * n, scratch_shapes=_exchange_sems(n),
    )(*arrays)


def _exchange_out_shapes(arrays, modes):
    return [jax.ShapeDtypeStruct((N_DEV,) + a.shape if md == "gather" else a.shape, a.dtype)
            for a, md in zip(arrays, modes)]


def _exchange_sems(n):
    return [pltpu.SemaphoreType.DMA((n, N_DEV - 1)), pltpu.SemaphoreType.DMA((n, N_DEV - 1)),
            pltpu.SemaphoreType.DMA((n,))]


def _exchange_copies(srcs, dsts, modes, send_sems, recv_sems, loc_sems):
    n = len(srcs)
    x, y, c = lax.axis_index("x"), lax.axis_index("y"), lax.axis_index("c")

    def slot(i, px, py, pc):
        return 2 * px + py if modes[i] == "chips" else 4 * px + 2 * py + pc

    def src_of(i, px, py, pc):
        return srcs[i] if modes[i] == "gather" else srcs[i].at[slot(i, px, py, pc)]

    local = [pltpu.make_async_copy(src_of(i, x, y, c), dsts[i].at[slot(i, x, y, c)], loc_sems.at[i])
             for i in range(n)]
    sends, recvs = [], []
    for r in (1, 2, 4, 3, 5, 6, 7):
        px = 1 - x if r & 4 else x
        py = 1 - y if r & 2 else y
        pc = 1 - c if r & 1 else c
        for i in range(n):
            if modes[i] == "chips" and r & 1:
                continue
            sems = dict(send_sem=send_sems.at[i, r - 1], recv_sem=recv_sems.at[i, r - 1],
                        device_id=(px, py, pc), device_id_type=pl.DeviceIdType.MESH)
            sends.append(pltpu.make_async_remote_copy(
                src_ref=src_of(i, px, py, pc), dst_ref=dsts[i].at[slot(i, x, y, c)], **sems))
            recvs.append(pltpu.make_async_remote_copy(
                src_ref=src_of(i, px, py, pc), dst_ref=dsts[i].at[slot(i, px, py, pc)], **sems))
    return local, sends, recvs


def _exchange_start(copies):
    local, sends, _ = copies
    for cp in local + sends:
        cp.start()


def _exchange_wait(copies):
    local, sends, recvs = copies
    for cp in recvs:
        cp.wait_recv()
    for cp in sends:
        cp.wait_send()
    for cp in local:
        cp.wait()


def _gather2_parts(srcs, dsts, send_sems, recv_sems, loc_sems):
    n = len(srcs)
    x, y, c = lax.axis_index("x"), lax.axis_index("y"), lax.axis_index("c")
    me, sibling = (x, y, c), (x, y, 1 - c)
    chips = [(1 - x, y), (x, 1 - y), (1 - x, 1 - y)]

    def rows(i, dev):
        return dsts[i].at[4 * dev[0] + 2 * dev[1] + dev[2]]

    def copy(i, k, block, to, src=None):
        return pltpu.make_async_remote_copy(
            src_ref=rows(i, block) if src is None else src, dst_ref=rows(i, block),
            send_sem=send_sems.at[i, k], recv_sem=recv_sems.at[i, k],
            device_id=to, device_id_type=pl.DeviceIdType.MESH)

    local = [pltpu.make_async_copy(srcs[i], rows(i, me), loc_sems.at[i]) for i in range(n)]
    first = []
    for i in range(n):
        first.append(copy(i, 0, me, sibling, src=srcs[i]))
        first += [copy(i, 1 + j, me, (*chip, c), src=srcs[i]) for j, chip in enumerate(chips)]
    return n, c, me, sibling, chips, copy, local, first


def _gather2_start(srcs, dsts, send_sems, recv_sems, loc_sems):
    *_, local, first = _gather2_parts(srcs, dsts, send_sems, recv_sems, loc_sems)
    for cp in local + first:
        cp.start()


def _gather2_finish(srcs, dsts, send_sems, recv_sems, loc_sems):
    n, c, me, sibling, chips, copy, local, first = _gather2_parts(srcs, dsts, send_sems, recv_sems, loc_sems)
    passed = []
    for j, chip in enumerate(chips):
        for i in range(n):
            copy(i, 1 + j, (*chip, c), me).wait_recv()
            cp = copy(i, 4 + j, (*chip, c), sibling)
            cp.start()
            passed.append(cp)
    for i in range(n):
        copy(i, 0, sibling, me).wait_recv()
        for j, chip in enumerate(chips):
            copy(i, 4 + j, (*chip, 1 - c), me).wait_recv()
    for cp in first + passed:
        cp.wait_send()
    for cp in local:
        cp.wait()


def _sibling_copies(srcs, dsts, send_sems, recv_sems):
    x, y, c = lax.axis_index("x"), lax.axis_index("y"), lax.axis_index("c")
    return [pltpu.make_async_remote_copy(
        src_ref=srcs[i].at[1 - c, k], dst_ref=dsts[i].at[k], send_sem=send_sems.at[i, k], recv_sem=recv_sems.at[i, k],
        device_id=(x, y, 1 - c), device_id_type=pl.DeviceIdType.MESH)
        for i in range(len(srcs)) for k in range(N_CHIPS)]


def _token_matmuls_and_swap(pairs, tk, swap):
    npair, nsw = len(pairs), len(swap)
    s = pairs[0][0].shape[1]
    last = s // tk - 1

    def body(*refs):
        a_refs, b_refs = refs[0:2 * npair:2], refs[1:2 * npair:2]
        src = refs[2 * npair:2 * npair + nsw]
        o_refs = refs[2 * npair + nsw:3 * npair + nsw]
        dst = refs[3 * npair + nsw:3 * npair + 2 * nsw]
        sems = refs[3 * npair + 2 * nsw:3 * npair + 2 * nsw + 2]
        acc_refs = refs[3 * npair + 2 * nsw + 2:]

        @pl.when(pl.program_id(0) == 0)
        def _():
            for cp in _sibling_copies(src, dst, *sems):
                cp.start()

        for a_ref, b_ref, o_ref, acc_ref in zip(a_refs, b_refs, o_refs, acc_refs):
            _accumulate_tokens(a_ref[...], b_ref, acc_ref, o_ref, last)

        @pl.when(pl.program_id(0) == last)
        def _():
            for cp in _sibling_copies(src, dst, *sems):
                cp.wait()

    in_specs, args = [], []
    for a_t, b in pairs:
        in_specs += [pl.BlockSpec((a_t.shape[0], tk), lambda k: (0, k)), pl.BlockSpec((tk, b.shape[1]), lambda k: (k, 0))]
        args += [a_t, b]
    outs = pl.pallas_call(
        body, name="dw_out_projs_swap", grid=(s // tk,),
        out_shape=[jax.ShapeDtypeStruct((a_t.shape[0], b.shape[1]), BF16) for a_t, b in pairs]
        + [jax.ShapeDtypeStruct(a.shape[1:], a.dtype) for a in swap],
        in_specs=in_specs + [ANY_SPEC] * nsw,
        out_specs=[pl.BlockSpec((a_t.shape[0], b.shape[1]), lambda k: (0, 0)) for a_t, b in pairs] + [ANY_SPEC] * nsw,
        scratch_shapes=[pltpu.SemaphoreType.DMA((nsw, N_CHIPS)), pltpu.SemaphoreType.DMA((nsw, N_CHIPS))]
        + [pltpu.VMEM((a_t.shape[0], b.shape[1]), F32) for a_t, b in pairs],
        compiler_params=_params(1))(*args, *swap)
    return outs[:npair], outs[npair:]


def _chip_partial(core, slots, theirs, tr, name):
    _, k, rws, cols = slots.shape

    def body(core_ref, a_ref, b_ref, f_ref, h_ref):
        sm = a_ref[0].astype(F32) + b_ref[...].astype(F32)
        f_ref[...] = sm
        h_ref[...] = sm.astype(BF16)

    blk = pl.BlockSpec((1, tr, cols), lambda j, i, core_ref: (j, i, 0))
    mine = pl.BlockSpec((1, 1, tr, cols), lambda j, i, core_ref: (core_ref[0], j, i, 0))
    return pl.pallas_call(
        body, name=name,
        grid_spec=pltpu.PrefetchScalarGridSpec(num_scalar_prefetch=1, grid=(k, rws // tr),
                                               in_specs=[mine, blk], out_specs=[blk, blk]),
        out_shape=[jax.ShapeDtypeStruct(theirs.shape, F32), jax.ShapeDtypeStruct(theirs.shape, BF16)],
        compiler_params=_params(2))(core, slots, theirs)


def _ada_fwd(c_all, w_shard, b_shard):
    def body(c_ref, w_ref, b_ref, o_ref):
        ch, cm, cl = [t.astype(BF16) for t in _split3(c_ref[...])]
        wh, wm, wl = [t.astype(BF16) for t in _split3(w_ref[...])]
        acc = _dot(ch, wh) + _dot(ch, wm) + _dot(cm, wh) + _dot(ch, wl) + _dot(cl, wh) + _dot(cm, wm)
        o_ref[...] = acc + b_ref[...]

    return pl.pallas_call(body, name="ada_fwd", out_shape=jax.ShapeDtypeStruct((N_DEV, ADA_SHARD), F32),
                          compiler_params=_params(0))(c_all, w_shard, b_shard)


def _rope_tables_and_gather(pos_col, inv_freq, tm, weights, c_rows):
    s = pos_col.shape[0]
    nw = len(weights)
    last = s // tm - 1

    def body(*refs):
        p_ref, f_ref = refs[:2]
        w_src, c_src = refs[2:2 + nw], refs[2 + nw:3 + nw]
        cos_ref, sa_ref, sb_ref = refs[3 + nw:6 + nw]
        w_dst, c_dst = refs[6 + nw:6 + 2 * nw], refs[6 + 2 * nw:7 + 2 * nw]
        w_sems, c_sems = refs[7 + 2 * nw:10 + 2 * nw], refs[10 + 2 * nw:]

        @pl.when(pl.program_id(0) == 0)
        def _():
            _gather2_start(w_src, w_dst, *w_sems)
            _exchange_start(_exchange_copies(c_src, c_dst, ["gather"], *c_sems))

        ang = p_ref[...] * f_ref[...]
        sin = jnp.sin(ang)
        first_half = (_lane(CHUNK) & (HEAD_DIM - 1)) < HEAD_DIM // 2
        cos_ref[...] = jnp.cos(ang)
        sa_ref[...] = jnp.where(first_half, -sin, 0.0)
        sb_ref[...] = jnp.where(first_half, 0.0, sin)

        @pl.when(pl.program_id(0) == last)
        def _():
            _exchange_wait(_exchange_copies(c_src, c_dst, ["gather"], *c_sems))
            _gather2_finish(w_src, w_dst, *w_sems)

    tab = jax.ShapeDtypeStruct((s, CHUNK), F32)
    blk = pl.BlockSpec((tm, CHUNK), lambda i: (i, 0))
    gathered = [jax.ShapeDtypeStruct((N_DEV,) + a.shape, a.dtype) for a in list(weights) + [c_rows]]
    outs = pl.pallas_call(
        body, name="rope_tables_gather", grid=(s // tm,), out_shape=[tab, tab, tab] + gathered,
        in_specs=[pl.BlockSpec((tm, 1), lambda i: (i, 0)), pl.BlockSpec((1, CHUNK), lambda i: (0, 0))]
        + [ANY_SPEC] * (nw + 1),
        out_specs=[blk, blk, blk] + [ANY_SPEC] * (nw + 1),
        scratch_shapes=_exchange_sems(nw) + _exchange_sems(1),
        compiler_params=_params(1))(pos_col, inv_freq, *weights, c_rows)
    return outs[:3], outs[3:3 + nw], outs[3 + nw]


def _fwd_proj(x, mod, w_all, cos, sa, sb, bf_row, emat, ind, tm):
    s = x.shape[0]

    def body(x_ref, mod_ref, w_ref, cos_ref, sa_ref, sb_ref, bf_ref, e_ref, ind_ref,
             h_ref, qa_ref, ka_ref, va_ref, za_ref, qb_ref, kb_ref, vb_ref, zb_ref, ga_ref, gb_ref, f_ref,
             bnd_ref, nrm_ref, carry_ref):
        @pl.when(pl.program_id(0) == 0)
        def _():
            carry_ref[...] = jnp.zeros_like(carry_ref)

        xv = x_ref[...]
        rstd = lax.rsqrt(jnp.mean(xv * xv, axis=-1, keepdims=True) + NORM_EPS)
        h = (xv * rstd * mod_ref[0:1, :]) * (1.0 + mod_ref[1:2, :]) + mod_ref[2:3, :]
        hb = h.astype(BF16)
        h_ref[...] = hb.T

        def seg(name):
            off, w = SEG[name]
            return _dot(hb, w_ref[:, off:off + w])

        lane = _lane(CHUNK)
        low = lane < HEAD_DIM
        q_ones = _ones_lanes(64, 67)
        k_ones = _ones_lanes(67, 70)
        cos_t, sa_t, sb_t = cos_ref[...], sa_ref[...], sb_ref[...]

        def write_heads(ref, nat, extra_of, rope, scale):
            for p in range(N_HEADS // 2):
                c = nat[:, CHUNK * p:CHUNK * (p + 1)]
                if rope:
                    c = _rope(c, cos_t, sa_t, sb_t)
                if scale != 1.0:
                    c = c * scale
                for hh in range(2):
                    hd = 2 * p + hh
                    src = c if hh == 0 else pltpu.roll(c, HEAD_DIM, 1)
                    ref[:, CHUNK * hd:CHUNK * (hd + 1)] = jnp.where(low, src, extra_of(hd)).astype(BF16)

        write_heads(qa_ref, seg("qa"), lambda hd: q_ones, True, Q_SCALE)
        ka = seg("ka")
        va = seg("va")
        for kv in range(KV_GROUPS):
            sl = slice(CHUNK * kv, CHUNK * (kv + 1))
            ka_ref[:, sl] = jnp.where(low, _rope(ka[:, sl], cos_t, sa_t, sb_t), k_ones).astype(BF16)
            va_ref[:, sl] = jnp.where(low, va[:, sl], q_ones).astype(BF16)
        za_ref[...] = seg("za")
        zb_ref[...] = seg("zb")
        ga_ref[...] = seg("ga")
        gb_ref[...] = seg("gb")

        xf = seg("f") + bf_ref[...]
        f_ref[...] = xf
        logf = jnp.minimum(xf, 0.0) - jnp.log1p(jnp.exp(-jnp.abs(xf)))
        row = lax.broadcasted_iota(jnp.int32, (tm, tm), 0)
        col = lax.broadcasted_iota(jnp.int32, (tm, tm), 1)
        tri = jnp.where(col <= row, 1.0, 0.0).astype(BF16)
        hi, mid, lo = _split3(logf)
        cum = _dot(tri, hi.astype(BF16)) + _dot(tri, mid.astype(BF16)) + _dot(tri, lo.astype(BF16))
        cum = cum + carry_ref[0:1, :]
        carry_ref[0:1, :] += jnp.sum(logf, axis=0, keepdims=True)
        lane_all = _lane(AUG_W) & (CHUNK - 1)
        bias = _spread3(-cum, e_ref) + jnp.where((lane_all >= 67) & (lane_all < 70), 1.0, 0.0)

        qb, kb = seg("qb"), seg("kb")
        write_heads(qb_ref, qb, lambda hd: q_ones, False, Q_SCALE)
        write_heads(kb_ref, kb, lambda hd: bias[:, CHUNK * hd:CHUNK * (hd + 1)], False, 1.0)
        write_heads(vb_ref, seg("vb"), lambda hd: q_ones, False, 1.0)

        rid = lax.broadcasted_iota(jnp.int32, (tm, CHUNK), 0)
        bnd_ref[...] = jnp.zeros_like(bnd_ref)
        bnd_ref[0, 0:1, :] = jnp.sum(jnp.where(rid == 0, cum, 0.0), axis=0, keepdims=True)
        bnd_ref[0, 1:2, :] = jnp.sum(jnp.where(rid == tm - 1, cum, 0.0), axis=0, keepdims=True)

        @pl.when(pl.program_id(0) == 0)
        def _():
            nrm_ref[...] = jnp.zeros_like(nrm_ref)

        for r_, nat in ((0, qb * Q_SCALE), (1, kb)):
            sq = _dot((nat * nat).astype(BF16), ind_ref[...])
            nrm_ref[r_:r_ + 1, :] = jnp.maximum(nrm_ref[r_:r_ + 1, :], jnp.max(sq, axis=0, keepdims=True))

    row_blk = lambda w: pl.BlockSpec((tm, w), lambda i: (i, 0))
    full = lambda a: pl.BlockSpec(a.shape, lambda i: (0,) * a.ndim)
    sds = lambda w, dt: jax.ShapeDtypeStruct((s, w), dt)
    out_shape = [sds(AUG_W, BF16), sds(KV_GROUPS * CHUNK, BF16), sds(KV_GROUPS * CHUNK, BF16),
                 sds(512, F32), sds(AUG_W, BF16), sds(AUG_W, BF16), sds(AUG_W, BF16), sds(512, F32),
                 sds(D_MODEL, F32), sds(D_MODEL, F32), sds(CHUNK, F32)]
    small = [jax.ShapeDtypeStruct((s // tm, 8, CHUNK), F32), jax.ShapeDtypeStruct((8, CHUNK), F32)]
    return pl.pallas_call(
        body, name="fwd_proj", grid=(s // tm,),
        out_shape=[jax.ShapeDtypeStruct((D_MODEL, s), BF16)] + out_shape + small,
        in_specs=[row_blk(D_MODEL), full(mod), _resident(w_all), row_blk(CHUNK), row_blk(CHUNK), row_blk(CHUNK),
                  full(bf_row), full(emat), full(ind)],
        out_specs=[pl.BlockSpec((D_MODEL, tm), lambda i: (0, i))] + [row_blk(o.shape[1]) for o in out_shape] + [
            pl.BlockSpec((1, 8, CHUNK), lambda i: (i, 0, 0)), pl.BlockSpec((8, CHUNK), lambda i: (0, 0))],
        scratch_shapes=[pltpu.VMEM((8, CHUNK), F32)],
        compiler_params=_params(1))(x, mod, w_all, cos, sa, sb, bf_row, emat, ind)


def _swa_fwd(q_aug, k_aug, v_aug, sink_rows, tq):
    s = q_aug.shape[0]
    r = tq // WINDOW
    gw = GROUP * CHUNK

    def body(q_ref, kc_ref, kp_ref, vc_ref, vp_ref, sink_ref, o_ref, qb_ref):
        i = pl.program_id(1)
        st = GROUP * WINDOW
        qloc = lax.broadcasted_iota(jnp.int32, (st, 2 * WINDOW), 0) & (WINDOW - 1)
        col = lax.broadcasted_iota(jnp.int32, (st, 2 * WINDOW), 1)
        band = (col > qloc) & (col <= qloc + WINDOW)
        head = jnp.right_shift(lax.broadcasted_iota(jnp.int32, (st, 1), 0), 7)
        sink = jnp.zeros((st, 1), F32)
        for g in range(GROUP):
            sink = jnp.where(head == g, jnp.max(sink_ref[0, g:g + 1, :], axis=1, keepdims=True), sink)
        lane = _lane(CHUNK)
        for sub in range(r):
            rows = slice(WINDOW * sub, WINDOW * (sub + 1))
            q = jnp.concatenate([q_ref[rows, CHUNK * g:CHUNK * (g + 1)] for g in range(GROUP)], axis=0)
            if sub == 0:
                k = jnp.concatenate([kp_ref[...], kc_ref[rows, :]], axis=0)
                v = jnp.concatenate([vp_ref[...], vc_ref[rows, :]], axis=0)
                valid = band & ((col >= WINDOW) | (i > 0))
            else:
                both = slice(WINDOW * (sub - 1), WINDOW * (sub + 1))
                k, v, valid = kc_ref[both, :], vc_ref[both, :], band
            sc = jnp.where(valid, _dot_nt(q, k), NEG)
            m = jnp.maximum(jnp.max(sc, axis=1, keepdims=True), sink)
            acc = _dot(jnp.exp((sc - m).astype(BF16)), v)
            denom = _lane_sum(acc, 64, 65) + jnp.exp(sink - m)
            out = acc / denom
            aug = _place3(lane, 67, -(m + jnp.log(denom)), q.astype(F32)).astype(BF16)
            hrows = lambda a, g: a[WINDOW * g:WINDOW * (g + 1), :]
            for g in range(GROUP):
                qb_ref[rows, CHUNK * g:CHUNK * (g + 1)] = hrows(aug, g)
            for pp in range(GROUP // 2):
                o_ref[rows, CHUNK * pp:CHUNK * (pp + 1)] = _pair(hrows(out, 2 * pp), hrows(out, 2 * pp + 1))

    return pl.pallas_call(
        body, name="swa_fwd", grid=(KV_GROUPS, s // tq),
        out_shape=[jax.ShapeDtypeStruct((s, 512), F32), jax.ShapeDtypeStruct((s, AUG_W), BF16)],
        in_specs=[pl.BlockSpec((tq, gw), lambda kv, i: (i, kv)),
                  pl.BlockSpec((tq, CHUNK), lambda kv, i: (i, kv)),
                  pl.BlockSpec((WINDOW, CHUNK), lambda kv, i: (jnp.maximum(i * r - 1, 0), kv)),
                  pl.BlockSpec((tq, CHUNK), lambda kv, i: (i, kv)),
                  pl.BlockSpec((WINDOW, CHUNK), lambda kv, i: (jnp.maximum(i * r - 1, 0), kv)),
                  pl.BlockSpec((1, 8, CHUNK), lambda kv, i: (kv, 0, 0))],
        out_specs=[pl.BlockSpec((tq, GROUP * HEAD_DIM), lambda kv, i: (i, kv)),
                   pl.BlockSpec((tq, gw), lambda kv, i: (i, kv))],
        compiler_params=_params(2))(q_aug, k_aug, k_aug, v_aug, v_aug, sink_rows)


def _swa_bwd(k_aug, v_aug, q_bwd, do_aug, cos, sa, sb, sink_rows, tk):
    s = k_aug.shape[0]
    r = tk // WINDOW
    nt = s // tk
    nb = s // WINDOW
    gw = GROUP * CHUNK

    def body(k_ref, v_ref, q_ref, qn_ref, do_ref, don_ref, cos_ref, sa_ref, sb_ref, sink_ref,
             da_ref, dsink_ref, carry_ref, acc_ref):
        j = pl.program_id(0)

        @pl.when(j == 0)
        def _():
            carry_ref[...] = jnp.zeros_like(carry_ref)
            dsink_ref[...] = jnp.zeros_like(dsink_ref)

        acc_ref[...] = jnp.zeros_like(acc_ref)
        acc_ref[0:WINDOW, :] = carry_ref[...]
        dk_col, dv_col = SEG["ka"][0] - SEG["qa"][0], SEG["va"][0] - SEG["qa"][0]
        st = GROUP * WINDOW
        key = lax.broadcasted_iota(jnp.int32, (WINDOW, 2 * st), 0)
        col = lax.broadcasted_iota(jnp.int32, (WINDOW, 2 * st), 1)
        qloc = col & (WINDOW - 1)
        band = ((col < st) & (key <= qloc)) | ((col >= st) & (key > qloc))
        lane = _lane(CHUNK)
        low = lane < HEAD_DIM
        cos_t, sa_t, sb_t = cos_ref[...], sa_ref[...], sb_ref[...]
        for kv, sub in [(kv, sub) for kv in range(KV_GROUPS) for sub in range(r)]:
            heads = [slice(CHUNK * (GROUP * kv + g), CHUNK * (GROUP * kv + g + 1)) for g in range(GROUP)]
            kvs = slice(CHUNK * kv, CHUNK * (kv + 1))
            rows = slice(WINDOW * sub, WINDOW * (sub + 1))
            nxt = slice(WINDOW * (sub + 1), WINDOW * (sub + 2))
            k, v = k_ref[rows, kvs], v_ref[rows, kvs]
            q_cur, do_cur = [q_ref[rows, cs] for cs in heads], [do_ref[rows, cs] for cs in heads]
            if sub < r - 1:
                q_nxt, do_nxt, valid = [q_ref[nxt, cs] for cs in heads], [do_ref[nxt, cs] for cs in heads], band
            else:
                q_nxt, do_nxt = [qn_ref[:, cs] for cs in heads], [don_ref[:, cs] for cs in heads]
                valid = band & ((col < st) | (j < nt - 1))
            q = jnp.concatenate(q_cur + q_nxt, axis=0)
            do = jnp.concatenate(do_cur + do_nxt, axis=0)
            pt = jnp.exp(jnp.where(valid, _dot_nt(k, q), NEG))
            ds = (pt * _dot_nt(v, do)).astype(BF16)
            dv = _dot(pt.astype(BF16), do)
            dk = _dot(ds, q)
            dq = _dot_tn(ds, k)
            for g, cs in enumerate(heads):
                acc_ref[rows, cs] += dq[WINDOW * g:WINDOW * (g + 1), :]
                dqn = dq[st + WINDOW * g:st + WINDOW * (g + 1), :]
                if sub < r - 1:
                    acc_ref[nxt, cs] += dqn
                else:
                    carry_ref[:, cs] = dqn
                sink = jnp.max(sink_ref[kv, g:g + 1, :], axis=1, keepdims=True)
                p_sink = jnp.exp(sink + _lane_sum(q_cur[g].astype(F32), 67, 70))
                term = jnp.sum(p_sink * _lane_sum(do_cur[g].astype(F32), 64, 67), axis=0, keepdims=True)
                dsink_ref[kv, g:g + 1, :] += jnp.broadcast_to(term, (1, CHUNK))
            da_ref[rows, dk_col + CHUNK * kv:dk_col + CHUNK * (kv + 1)] = _rope_inv(
                jnp.where(low, dk, 0.0), cos_t[rows, :], sa_t[rows, :], sb_t[rows, :]).astype(BF16)
            da_ref[rows, dv_col + CHUNK * kv:dv_col + CHUNK * (kv + 1)] = jnp.where(low, dv, 0.0).astype(BF16)
        for pp in range(N_HEADS // 2):
            d = _pair(acc_ref[:, CHUNK * 2 * pp:CHUNK * (2 * pp + 1)], acc_ref[:, CHUNK * (2 * pp + 1):CHUNK * (2 * pp + 2)])
            da_ref[:, CHUNK * pp:CHUNK * (pp + 1)] = (_rope_inv(d, cos_t, sa_t, sb_t) * Q_SCALE).astype(BF16)

    cur = lambda w: pl.BlockSpec((tk, w), lambda j: (j, 0))
    nxt = pl.BlockSpec((WINDOW, AUG_W), lambda j: (jnp.minimum((j + 1) * r, nb - 1), 0))
    whole = pl.BlockSpec((KV_GROUPS, 8, CHUNK), lambda j: (0, 0, 0))
    return pl.pallas_call(
        body, name="swa_bwd", grid=(nt,),
        out_shape=[jax.ShapeDtypeStruct((s, 1024), BF16), jax.ShapeDtypeStruct((KV_GROUPS, 8, CHUNK), F32)],
        in_specs=[cur(KV_GROUPS * CHUNK), cur(KV_GROUPS * CHUNK), cur(AUG_W), nxt, cur(AUG_W), nxt,
                  cur(CHUNK), cur(CHUNK), cur(CHUNK), whole],
        out_specs=[cur(1024), whole],
        scratch_shapes=[pltpu.VMEM((WINDOW, AUG_W), F32), pltpu.VMEM((tk, AUG_W), F32)],
        compiler_params=_params(1))(k_aug, v_aug, q_bwd, q_bwd, do_aug, do_aug, cos, sa, sb, sink_rows)


def _fox_fwd(first_key, q_aug, k_aug, v_aug, t):
    s = q_aug.shape[0]
    pw = 2 * CHUNK

    def body(lo_ref, q_ref, k_ref, v_ref, o_ref, qb_ref):
        i = pl.program_id(1)
        first = lo_ref[pl.program_id(0), i]
        row = lax.broadcasted_iota(jnp.int32, (t, t), 0)
        col = lax.broadcasted_iota(jnp.int32, (t, t), 1)
        lane = _lane(CHUNK)
        heads = (slice(0, CHUNK), slice(CHUNK, pw))
        qs = [q_ref[:, hs] for hs in heads]

        def step(jb, carry, masked):
            rows = pl.ds(pl.multiple_of(jb * t, t), t)
            scs = [_dot_nt(qs[hh], k_ref[rows, hs]) for hh, hs in enumerate(heads)]
            if masked:
                scs = [jnp.where(col <= row, sc, NEG) for sc in scs]
            m_news = [jnp.maximum(carry[2 * hh], jnp.max(scs[hh], axis=1, keepdims=True)) for hh in range(2)]
            ps = [jnp.exp((scs[hh] - m_news[hh]).astype(BF16)) for hh in range(2)]
            new = []
            for hh, hs in enumerate(heads):
                m, acc = carry[2 * hh], carry[2 * hh + 1]
                new += [m_news[hh], jnp.exp(m - m_news[hh]) * acc + _dot(ps[hh], v_ref[rows, hs])]
            return tuple(new)

        init = (jnp.full((t, 1), NEG, F32), jnp.zeros((t, CHUNK), F32)) * 2
        carry = step(i, lax.fori_loop(first, i, lambda jb, c: step(jb, c, False), init), True)
        outs = []
        for hh, hs in enumerate(heads):
            m, acc = carry[2 * hh], carry[2 * hh + 1]
            denom = _lane_sum(acc, 64, 65)
            outs.append(acc / denom)
            lse = m + jnp.log(denom)
            qb_ref[:, hs] = _place3(lane, 67, -lse, qs[hh].astype(F32)).astype(BF16)
        o_ref[...] = _pair(outs[0], outs[1])

    return pl.pallas_call(
        body, name="fox_fwd", grid=(N_HEADS // 2, s // t),
        out_shape=[jax.ShapeDtypeStruct((s, 512), F32), jax.ShapeDtypeStruct((s, AUG_W), BF16)],
        in_specs=[pl.BlockSpec(memory_space=pltpu.SMEM),
                  pl.BlockSpec((t, pw), lambda p, i: (i, p)),
                  pl.BlockSpec((s, pw), lambda p, i: (0, p)),
                  pl.BlockSpec((s, pw), lambda p, i: (0, p))],
        out_specs=[pl.BlockSpec((t, CHUNK), lambda p, i: (i, p)), pl.BlockSpec((t, pw), lambda p, i: (i, p))],
        compiler_params=_params(2))(first_key, q_aug, k_aug, v_aug)


def _fox_block_ranges(bounds, norms, r):
    cum_first = bounds[0::r, 0, :N_HEADS]
    cum_last = bounds[r - 1::r, 1, :N_HEADS]
    n = cum_first.shape[0]
    reach = 2.0 * 1.02 * jnp.sqrt(norms[0, :N_HEADS] * norms[1, :N_HEADS]) + UNDERFLOW
    blk = jnp.arange(n, dtype=jnp.int32)
    decay = cum_last[None, :, :] - cum_first[:, None, :]
    skip = (decay > reach[None, None, :]) & (blk[None, :, None] < blk[:, None, None])
    first = jnp.min(jnp.where(skip, n, blk[None, :, None]), axis=1)
    first = jnp.minimum(first[:, 0::2], first[:, 1::2]).T
    needed = (first[:, :, None] <= blk[None, None, :]) & (blk[None, :, None] >= blk[None, None, :])
    last = jnp.max(jnp.where(needed, blk[None, :, None], blk[None, None, :]), axis=1)
    return first.astype(jnp.int32), last.astype(jnp.int32)


def _fox_bwd(last_query, k_aug, v_aug, q_bwd, do_aug, t):
    s = k_aug.shape[0]
    n = s // t
    pw = 2 * CHUNK

    def body(hi_ref, k_ref, v_ref, q_ref, do_ref, dk_ref, dv_ref, dck_ref, dq_ref, dcq_ref, dq_scr):
        j = pl.program_id(1)
        last = hi_ref[pl.program_id(0), j]

        @pl.when(j == 0)
        def _():
            dq_scr[...] = jnp.zeros_like(dq_scr)

        row = lax.broadcasted_iota(jnp.int32, (t, t), 0)
        col = lax.broadcasted_iota(jnp.int32, (t, t), 1)
        lane = _lane(CHUNK)
        heads = (slice(0, CHUNK), slice(CHUNK, pw))
        ks = [k_ref[:, hs] for hs in heads]
        vs = [v_ref[:, hs] for hs in heads]

        def step(ib, carry, masked):
            rows = pl.ds(pl.multiple_of(ib * t, t), t)
            new = []
            for hh, hs in enumerate(heads):
                dv, dk = carry[2 * hh], carry[2 * hh + 1]
                q, do = q_ref[rows, hs], do_ref[rows, hs]
                st = _dot_nt(ks[hh], q)
                if masked:
                    st = jnp.where(row <= col, st, NEG)
                pt = jnp.exp(st)
                ds = (pt * _dot_nt(vs[hh], do)).astype(BF16)
                new += [dv + _dot(pt.astype(BF16), do), dk + _dot(ds, q)]
                dq_scr[rows, hs] += _dot_tn(ds, ks[hh])
            return tuple(new)

        zero = jnp.zeros((t, CHUNK), F32)
        carry = lax.fori_loop(j + 1, last + 1, lambda ib, c: step(ib, c, False), step(j, (zero,) * 4, True))
        dvs, dks = (carry[0], carry[2]), (carry[1], carry[3])
        dk_ref[...] = _pair(dks[0], dks[1]).astype(BF16)
        dv_ref[...] = _pair(dvs[0], dvs[1]).astype(BF16)
        dck_ref[0] = jnp.where(lane == 0, pltpu.roll(dks[0], 64, 1),
                               jnp.where(lane == 1, pltpu.roll(dks[1], 65, 1), 0.0))

        @pl.when(j == n - 1)
        def _():
            for ib in range(n):
                rows = slice(t * ib, t * (ib + 1))
                d0, d1 = dq_scr[rows, 0:CHUNK], dq_scr[rows, CHUNK:pw]
                dq_ref[rows, :] = (_pair(d0, d1) * Q_SCALE).astype(BF16)
                dcq_ref[0, rows, :] = jnp.where(lane == 0, pltpu.roll(d0, CHUNK - 67, 1),
                                                jnp.where(lane == 1, pltpu.roll(d1, CHUNK - 66, 1), 0.0))

    return pl.pallas_call(
        body, name="fox_bwd", grid=(N_HEADS // 2, n),
        out_shape=[jax.ShapeDtypeStruct((s, 512), BF16), jax.ShapeDtypeStruct((s, 512), BF16),
                   jax.ShapeDtypeStruct((N_HEADS // 2, s, CHUNK), F32), jax.ShapeDtypeStruct((s, 512), BF16),
                   jax.ShapeDtypeStruct((N_HEADS // 2, s, CHUNK), F32)],
        in_specs=[pl.BlockSpec(memory_space=pltpu.SMEM),
                  pl.BlockSpec((t, pw), lambda p, j: (j, p)), pl.BlockSpec((t, pw), lambda p, j: (j, p)),
                  pl.BlockSpec((s, pw), lambda p, j: (0, p)), pl.BlockSpec((s, pw), lambda p, j: (0, p))],
        out_specs=[pl.BlockSpec((t, CHUNK), lambda p, j: (j, p)), pl.BlockSpec((t, CHUNK), lambda p, j: (j, p)),
                   pl.BlockSpec((1, t, CHUNK), lambda p, j: (p, j, 0)), pl.BlockSpec((s, CHUNK), lambda p, j: (0, p)),
                   pl.BlockSpec((1, s, CHUNK), lambda p, j: (p, 0, 0))],
        scratch_shapes=[pltpu.VMEM((s, pw), F32)],
        compiler_params=_params(2))(last_query, k_aug, v_aug, q_bwd, do_aug)


def _fgate_bwd(dcq, dck, xf, tm):
    s = xf.shape[0]
    nt = s // tm

    def body(dq_ref, dc_ref, xf_ref, df_ref, dbf_ref, carry_ref):
        @pl.when(pl.program_id(0) == 0)
        def _():
            carry_ref[...] = jnp.zeros_like(carry_ref)
            dbf_ref[...] = jnp.zeros_like(dbf_ref)

        row = lax.broadcasted_iota(jnp.int32, (tm, tm), 0)
        col = lax.broadcasted_iota(jnp.int32, (tm, tm), 1)
        tri = jnp.where(col >= row, 1.0, 0.0).astype(BF16)
        dcum = dq_ref[0] - dc_ref[0]
        for p in range(1, N_HEADS // 2):
            dcum = dcum + pltpu.roll(dq_ref[p] - dc_ref[p], 2 * p, 1)
        hi, mid, lo = _split3(dcum)
        dlogf = _dot(tri, hi.astype(BF16)) + _dot(tri, mid.astype(BF16)) + _dot(tri, lo.astype(BF16))
        dlogf = dlogf + carry_ref[0:1, :]
        carry_ref[0:1, :] += jnp.sum(dcum, axis=0, keepdims=True)
        df = dlogf * _sigmoid(-xf_ref[...])
        df_ref[...] = df.astype(BF16)
        dbf_ref[0:1, :] += jnp.sum(df, axis=0, keepdims=True)

    rev = pl.BlockSpec((tm, CHUNK), lambda i: (nt - 1 - i, 0))
    rev4 = pl.BlockSpec((N_HEADS // 2, tm, CHUNK), lambda i: (0, nt - 1 - i, 0))
    return pl.pallas_call(
        body, name="fgate_bwd", grid=(nt,),
        out_shape=[jax.ShapeDtypeStruct((s, CHUNK), BF16), jax.ShapeDtypeStruct((8, CHUNK), F32)],
        in_specs=[rev4, rev4, rev], out_specs=[rev, pl.BlockSpec((8, CHUNK), lambda i: (0, 0))],
        scratch_shapes=[pltpu.VMEM((8, CHUNK), F32)],
        compiler_params=_params(1))(dcq, dck, xf)


def _post(x, tgt, att_a, att_b, za, zb, ga, gb, woa_t, wob_t, wout, vec, ind, emat, tm):
    s = x.shape[0]

    def body(x_ref, t_ref, aa_ref, ab_ref, za_ref, zb_ref, ga_ref, gb_ref, woat_ref, wobt_ref, wout_ref,
             vec_ref, ind_ref, e_ref,
             dx1_ref, ua_ref, ub_ref, mg_ref, do_ref, dya_ref, dyb_ref, doa_ref, dob_ref, dzg_ref, acc_ref):
        @pl.when(pl.program_id(0) == 0)
        def _():
            acc_ref[...] = jnp.zeros_like(acc_ref)

        gate, gfin = vec_ref[0:1, :], vec_ref[1:2, :]
        inv_d = 1.0 / D_MODEL

        def branch_fwd(att_ref, z_ref, w_ref, u_ref):
            att, z = att_ref[...], z_ref[...]
            sz = _sigmoid(z)
            silu = z * sz
            u = (att * silu).astype(BF16)
            u_ref[...] = u.T
            return att, z, sz, silu, _dot_nt(u, w_ref[...])

        att_a, z_a, sz_a, silu_a, y_a = branch_fwd(aa_ref, za_ref, woat_ref, ua_ref)
        att_b, z_b, sz_b, silu_b, y_b = branch_fwd(ab_ref, zb_ref, wobt_ref, ub_ref)
        sg_a, sg_b = _sigmoid(ga_ref[...]), _sigmoid(gb_ref[...])
        merged = (sg_a * y_a + sg_b * y_b).astype(BF16)
        mg_ref[...] = merged.T
        o = _dot(merged, wout_ref[...])
        x1 = x_ref[...] + gate * o
        rstd = lax.rsqrt(jnp.mean(x1 * x1, axis=-1, keepdims=True) + NORM_EPS)
        xh = x1 * rstd
        diff = xh * gfin - t_ref[...]
        acc_ref[2:3, :] += (0.5 * inv_d) * jnp.sum(diff * diff, axis=0, keepdims=True)
        dy = diff * inv_d
        acc_ref[1:2, :] += jnp.sum(dy * xh, axis=0, keepdims=True)
        dyg = dy * gfin
        dx1 = rstd * (dyg - xh * jnp.mean(dyg * xh, axis=-1, keepdims=True))
        dx1_ref[...] = dx1
        acc_ref[0:1, :] += jnp.sum(dx1 * o, axis=0, keepdims=True)
        d_o = (dx1 * gate).astype(BF16)
        do_ref[...] = d_o
        dmg = _dot_nt(d_o, wout_ref[...])

        lane = _lane(CHUNK)
        low = lane < HEAD_DIM
        lane_all = _lane(AUG_W) & (CHUNK - 1)

        zg0 = SEG["za"][0]

        def branch_bwd(sg, y, att, z, sz, silu, wt_ref, dy_ref, g_name, z_name, doaug_ref):
            dyb = (dmg * sg).astype(BF16)
            dy_ref[...] = dyb
            g_off, z_off = SEG[g_name][0] - zg0, SEG[z_name][0] - zg0
            dzg_ref[:, g_off:g_off + D_MODEL] = (dmg * y * sg * (1.0 - sg)).astype(BF16)
            du = _dot(dyb, wt_ref[...])
            datt = du * silu
            dzg_ref[:, z_off:z_off + 512] = (du * att * (sz * (1.0 + z * (1.0 - sz)))).astype(BF16)
            extra = _spread3(-_dot_split(datt * att, ind_ref[...]), e_ref)
            for p in range(N_HEADS // 2):
                c = datt[:, CHUNK * p:CHUNK * (p + 1)]
                for hh in range(2):
                    hd = 2 * p + hh
                    src = c if hh == 0 else pltpu.roll(c, HEAD_DIM, 1)
                    doaug_ref[:, CHUNK * hd:CHUNK * (hd + 1)] = jnp.where(
                        low, src, extra[:, CHUNK * hd:CHUNK * (hd + 1)]).astype(BF16)

        branch_bwd(sg_a, y_a, att_a, z_a, sz_a, silu_a, woat_ref, dya_ref, "ga", "za", doa_ref)
        branch_bwd(sg_b, y_b, att_b, z_b, sz_b, silu_b, wobt_ref, dyb_ref, "gb", "zb", dob_ref)

    row_blk = lambda w: pl.BlockSpec((tm, w), lambda i: (i, 0))
    full = lambda a: pl.BlockSpec(a.shape, lambda i: (0,) * a.ndim)
    sds = lambda w, dt: jax.ShapeDtypeStruct((s, w), dt)
    tds = lambda w: jax.ShapeDtypeStruct((w, s), BF16)
    col_blk = lambda w: pl.BlockSpec((w, tm), lambda i: (0, i))
    out_shape = [sds(D_MODEL, F32), tds(512), tds(512), tds(D_MODEL), sds(D_MODEL, BF16),
                 sds(D_MODEL, BF16), sds(D_MODEL, BF16), sds(AUG_W, BF16), sds(AUG_W, BF16), sds(3072, BF16),
                 jax.ShapeDtypeStruct((8, D_MODEL), F32)]
    ins = [x, tgt, att_a, att_b, za, zb, ga, gb, woa_t, wob_t, wout, vec, ind, emat]
    in_specs = [row_blk(a.shape[1]) for a in ins[:8]] + [full(a) for a in ins[8:]]
    out_specs = ([row_blk(D_MODEL), col_blk(512), col_blk(512), col_blk(D_MODEL)]
                 + [row_blk(o.shape[1]) for o in out_shape[4:-1]] + [pl.BlockSpec((8, D_MODEL), lambda i: (0, 0))])
    return pl.pallas_call(body, name="post", grid=(s // tm,), out_shape=out_shape, in_specs=in_specs,
                          out_specs=out_specs, compiler_params=_params(1))(*ins)


def _bwd_pre(dproj, w_all, x, dx1, mod, tm, chip_halves, modes):
    s = x.shape[0]
    ng, nh = len(DPROJ_GROUPS), len(chip_halves)
    last = s // tm - 1

    def body(*refs):
        dp_refs = refs[:ng]
        wt_ref, x_ref, dx1_ref, mod_ref = refs[ng:ng + 4]
        send_refs = refs[ng + 4:ng + 4 + nh]
        gx_ref, acc_ref = refs[ng + 4 + nh:ng + 6 + nh]
        recv_refs = refs[ng + 6 + nh:ng + 6 + 2 * nh]
        sems = refs[ng + 6 + 2 * nh:]

        @pl.when(pl.program_id(0) == 0)
        def _():
            acc_ref[...] = jnp.zeros_like(acc_ref)
            _exchange_start(_exchange_copies(send_refs, recv_refs, modes, *sems))

        dh = None
        for dp_ref, (_, off, w) in zip(dp_refs, DPROJ_GROUPS):
            part = _dot_nt(dp_ref[...], wt_ref[:, off:off + w])
            dh = part if dh is None else dh + part
        xv = x_ref[...]
        rstd = lax.rsqrt(jnp.mean(xv * xv, axis=-1, keepdims=True) + NORM_EPS)
        xh = xv * rstd
        gn = mod_ref[0:1, :]
        acc_ref[0:1, :] += jnp.sum(dh, axis=0, keepdims=True)
        acc_ref[1:2, :] += jnp.sum(dh * (xh * gn), axis=0, keepdims=True)
        dn = dh * (1.0 + mod_ref[1:2, :])
        acc_ref[2:3, :] += jnp.sum(dn * xh, axis=0, keepdims=True)
        dxh = dn * gn
        gx_ref[...] = dx1_ref[...] + rstd * (dxh - xh * jnp.mean(dxh * xh, axis=-1, keepdims=True))

        @pl.when(pl.program_id(0) == last)
        def _():
            _exchange_wait(_exchange_copies(send_refs, recv_refs, modes, *sems))

    row_blk = lambda w: pl.BlockSpec((tm, w), lambda i: (i, 0))
    full = lambda a: pl.BlockSpec(a.shape, lambda i: (0,) * a.ndim)
    outs = pl.pallas_call(
        body, name="bwd_pre", grid=(s // tm,),
        out_shape=[jax.ShapeDtypeStruct((s, D_MODEL), F32), jax.ShapeDtypeStruct((8, D_MODEL), F32)]
        + _exchange_out_shapes(chip_halves, modes),
        in_specs=[row_blk(w) for _, _, w in DPROJ_GROUPS] + [_resident(w_all), row_blk(D_MODEL), row_blk(D_MODEL), full(mod)]
        + [ANY_SPEC] * nh,
        out_specs=[row_blk(D_MODEL), pl.BlockSpec((8, D_MODEL), lambda i: (0, 0))] + [ANY_SPEC] * nh,
        scratch_shapes=_exchange_sems(nh),
        compiler_params=_params(1))(*dproj, w_all, x, dx1, mod, *chip_halves)
    return outs[0], outs[1], outs[2:]


def _accumulate_tokens(a, b_ref, acc_ref, o_ref, last):
    @pl.when(pl.program_id(0) == 0)
    def _():
        acc_ref[...] = jnp.zeros_like(acc_ref)

    acc_ref[...] += _dot(a, b_ref[...])

    @pl.when(pl.program_id(0) == last)
    def _():
        o_ref[...] = acc_ref[...].astype(BF16)


def _token_matmul_multi(a_t, bs, tk, name):
    m, s = a_t.shape
    nb = len(bs)
    last = s // tk - 1

    def body(*refs):
        a_ref, b_refs, o_refs, acc_refs = refs[0], refs[1:1 + nb], refs[1 + nb:1 + 2 * nb], refs[1 + 2 * nb:]
        a = a_ref[...]
        for b_ref, o_ref, acc_ref in zip(b_refs, o_refs, acc_refs):
            _accumulate_tokens(a, b_ref, acc_ref, o_ref, last)

    return pl.pallas_call(
        body, name=name, grid=(s // tk,),
        out_shape=[jax.ShapeDtypeStruct((m, b.shape[1]), BF16) for b in bs],
        in_specs=[pl.BlockSpec((m, tk), lambda k: (0, k))] + [pl.BlockSpec((tk, b.shape[1]), lambda k: (k, 0)) for b in bs],
        out_specs=[pl.BlockSpec((m, b.shape[1]), lambda k: (0, 0)) for b in bs],
        scratch_shapes=[pltpu.VMEM((m, b.shape[1]), F32) for b in bs],
        compiler_params=_params(1))(a_t, *bs)


def _adam_math(g, w, m, v):
    m2 = ADAM_B1 * m + (1.0 - ADAM_B1) * g
    v2 = ADAM_B2 * v + (1.0 - ADAM_B2) * (g * g)
    delta = -ADAM_LR * ((m2 / ADAM_C1) / (jnp.sqrt(v2 / ADAM_C2) + ADAM_EPS) + ADAM_WD * w)
    return delta, m2, v2


def _adamw_parts(full, parts, others, w, m, v, tr, name):
    _, rws, cols = w.shape

    def body(oth_ref, f_ref, p_ref, w_ref, m_ref, v_ref, g_ref, d_ref, m2_ref, v2_ref):
        g = None
        for chip in range(N_CHIPS):
            part = (1.0 - oth_ref[chip]) * f_ref[chip] + oth_ref[chip] * p_ref[chip].astype(F32)
            g = part if g is None else g + part
        g_ref[0] = g
        d_ref[0], m2_ref[0], v2_ref[0] = _adam_math(g, w_ref[0], m_ref[0], v_ref[0])

    blk = pl.BlockSpec((1, tr, cols), lambda i: (0, i, 0))
    slots = pl.BlockSpec((N_CHIPS, tr, cols), lambda i: (0, i, 0))
    o = jax.ShapeDtypeStruct((1, rws, cols), F32)
    return pl.pallas_call(
        body, name=name, grid=(rws // tr,), out_shape=[o, o, o, o],
        in_specs=[pl.BlockSpec(memory_space=pltpu.SMEM), slots, slots, blk, blk, blk],
        out_specs=[blk, blk, blk, blk], compiler_params=_params(1))(others, full, parts, w, m, v)


def _adamw_devices(parts, w, m, v, tr, name):
    _, rws, cols = w.shape

    def body(p_ref, w_ref, m_ref, v_ref, g_ref, d_ref, m2_ref, v2_ref):
        g = p_ref[0].astype(F32)
        for dev in range(1, N_DEV):
            g = g + p_ref[dev].astype(F32)
        g_ref[0] = g
        d_ref[0], m2_ref[0], v2_ref[0] = _adam_math(g, w_ref[0], m_ref[0], v_ref[0])

    blk = pl.BlockSpec((1, tr, cols), lambda i: (0, i, 0))
    o = jax.ShapeDtypeStruct((1, rws, cols), F32)
    return pl.pallas_call(
        body, name=name, grid=(rws // tr,), out_shape=[o, o, o, o],
        in_specs=[pl.BlockSpec((N_DEV, tr, cols), lambda i: (0, i, 0)), blk, blk, blk],
        out_specs=[blk, blk, blk, blk], compiler_params=_params(1))(parts, w, m, v)


def _adamw_ada(c_lanes, d_rows, w, m, v, tr):
    _, rws, cols = w.shape

    def body(c_ref, d_ref, w_ref, m_ref, v_ref, g_ref, dl_ref, m2_ref, v2_ref):
        for k in range(cols // CHUNK):
            cs = slice(CHUNK * k, CHUNK * (k + 1))
            g = c_ref[0] * d_ref[0:1, cs]
            for b in range(1, N_DEV):
                g = g + c_ref[b] * d_ref[b:b + 1, cs]
            g_ref[0, :, cs] = g
            dl_ref[0, :, cs], m2_ref[0, :, cs], v2_ref[0, :, cs] = _adam_math(
                g, w_ref[0, :, cs], m_ref[0, :, cs], v_ref[0, :, cs])

    blk = pl.BlockSpec((1, tr, cols), lambda i: (0, i, 0))
    o = jax.ShapeDtypeStruct((1, rws, cols), F32)
    return pl.pallas_call(
        body, name="adamw_ada", grid=(rws // tr,), out_shape=[o, o, o, o],
        in_specs=[pl.BlockSpec((N_DEV, tr, CHUNK), lambda i: (0, i, 0)), pl.BlockSpec((N_DEV, cols), lambda i: (0, 0)),
                  blk, blk, blk],
        out_specs=[blk, blk, blk, blk], compiler_params=_params(1))(c_lanes, d_rows, w, m, v)


SMALL = (("b_ada", 0, 0, 3 * D_MODEL), ("g_norm", 1, 0, D_MODEL), ("b_f", 3, 0, N_HEADS), ("sinks", 4, 0, N_HEADS),
         ("g_final", 1, D_MODEL, D_MODEL))
LOSS_ROW = 2


def _pack_small(pre_acc, post_acc, dbf_acc, dsink):
    def body(pre_ref, post_ref, dbf_ref, dsink_ref, o_ref):
        o_ref[...] = jnp.zeros_like(o_ref)
        d = D_MODEL
        o_ref[0:1, 0:d], o_ref[0:1, d:2 * d], o_ref[0:1, 2 * d:3 * d] = pre_ref[0:1, :], pre_ref[1:2, :], post_ref[0:1, :]
        o_ref[1:2, 0:d], o_ref[1:2, d:2 * d] = pre_ref[2:3, :], post_ref[1:2, :]
        o_ref[LOSS_ROW:LOSS_ROW + 1, 0:d] = post_ref[2:3, :]
        o_ref[3:4, 0:CHUNK] = dbf_ref[0:1, :]
        lane = _lane(CHUNK)
        sinks = jnp.zeros((1, CHUNK), F32)
        for kv in range(KV_GROUPS):
            for g in range(GROUP):
                sinks = jnp.where(lane == GROUP * kv + g, dsink_ref[kv, g:g + 1, :], sinks)
        o_ref[4:5, 0:CHUNK] = sinks

    return pl.pallas_call(body, name="pack_small", out_shape=jax.ShapeDtypeStruct((8, 3 * D_MODEL), F32),
                          compiler_params=_params(0))(pre_acc, post_acc, dbf_acc, dsink)


def _adamw_small(packs, params):
    n = len(SMALL)

    def body(*refs):
        p_ref, wmv = refs[0], refs[1:1 + 3 * n]
        outs, loss_ref, tot_ref = refs[1 + 3 * n:1 + 7 * n], refs[1 + 7 * n], refs[2 + 7 * n]
        tot = p_ref[0]
        for dev in range(1, N_DEV):
            tot = tot + p_ref[dev]
        tot_ref[...] = tot
        for i, (_, row, lo, w) in enumerate(SMALL):
            g = tot_ref[row:row + 1, lo:lo + w]
            o = outs[4 * i:4 * i + 4]
            o[0][...] = g
            o[1][...], o[2][...], o[3][...] = _adam_math(g, wmv[3 * i][...], wmv[3 * i + 1][...], wmv[3 * i + 2][...])
        loss_ref[...] = jnp.broadcast_to(jnp.sum(tot_ref[LOSS_ROW:LOSS_ROW + 1, 0:D_MODEL], axis=1, keepdims=True),
                                         loss_ref.shape)

    flat = [a for wmv in params for a in wmv]
    out_shape = [jax.ShapeDtypeStruct(wmv[0].shape, F32) for wmv in params for _ in range(4)]
    res = pl.pallas_call(body, name="adamw_small", out_shape=out_shape + [jax.ShapeDtypeStruct((8, CHUNK), F32)],
                         scratch_shapes=[pltpu.VMEM((8, 3 * D_MODEL), F32)],
                         compiler_params=_params(0))(packs, *flat)
    return [tuple(res[4 * i:4 * i + 4]) for i in range(n)], res[4 * n]


def _tile(s, want):
    return min(s, want)


def _layout_pieces():
    orig = {"qa": 0, "ka": 512, "va": 640, "za": 768, "qb": 1280, "kb": 1792, "vb": 2304, "f": 2816, "zb": 2824,
            "ga": 3336, "gb": 4360}
    pieces = []
    for name, (off, w) in SEG.items():
        if name in ("ka", "va"):
            pieces += [(orig[name] + HEAD_DIM * kv, orig[name] + HEAD_DIM * (kv + 1), off + CHUNK * kv) for kv in range(KV_GROUPS)]
        else:
            pieces.append((orig[name], orig[name] + (N_HEADS if name == "f" else w), off))
    return pieces


def _assemble_w_all(win_g):
    cols, pos = [], 0
    for lo, hi, new in sorted(_layout_pieces(), key=lambda t: t[2]):
        if new > pos:
            cols.append(jnp.zeros((D_MODEL, new - pos), win_g.dtype))
        col = lo
        while col < hi:
            dev = col // IN_SHARD
            end = min(hi, (dev + 1) * IN_SHARD)
            cols.append(win_g[dev, :, col - dev * IN_SHARD:end - dev * IN_SHARD])
            col = end
        pos = new + hi - lo
    cols.append(jnp.zeros((D_MODEL, PROJ_W - pos), win_g.dtype))
    return jnp.concatenate(cols, axis=1)


def _grad_slot(dw_groups, dev):
    lo_d, hi_d = dev * IN_SHARD, (dev + 1) * IN_SHARD
    cols = []
    for lo, hi, new in sorted(_layout_pieces()):
        a, b = max(lo, lo_d), min(hi, hi_d)
        if a < b:
            arr, off = next((g, o) for g, (_, o, w) in zip(dw_groups, DPROJ_GROUPS) if o <= new < o + w)
            cols.append(arr[:, new - off + a - lo:new - off + b - lo])
    return jnp.concatenate(cols, axis=1)


def kernel(x, c, positions, w_ada, b_ada, g_norm, w_in, b_f, sinks, w_o_swa, w_o_fox, w_out, g_final, loss_target, m_w_ada, m_b_ada, m_g_norm, m_w_in, m_b_f, m_sinks, m_w_o_swa, m_w_o_fox, m_w_out, m_g_final, v_w_ada, v_b_ada, v_g_norm, v_w_in, v_b_f, v_sinks, v_w_o_swa, v_w_o_fox, v_w_out, v_g_final):
    s = x.shape[1]
    tm = _tile(s, 256)
    ta = _tile(s, 512)
    me = 4 * lax.axis_index("x") + 2 * lax.axis_index("y") + lax.axis_index("c")
    x2, tgt = x[0], loss_target[0]

    inv_freq = np.power(np.float32(ROPE_THETA), -np.arange(0, HEAD_DIM, 2, dtype=np.float32) / HEAD_DIM)
    inv_freq = jnp.asarray(np.tile(inv_freq, CHUNK // (HEAD_DIM // 2))[None, :], F32)
    (cos, sa, sb), (win_g, woa_g, wob_g, wout_g), c_all = _rope_tables_and_gather(
        positions.reshape(s, 1).astype(F32), inv_freq, tm,
        [w_in[0].astype(BF16), w_o_swa[0].T.astype(BF16), w_o_fox[0].T.astype(BF16), w_out[0].astype(BF16)],
        jnp.broadcast_to(c, (8, D_MODEL)))
    c_all = c_all[:, 0, :]
    b_shard = lax.dynamic_slice(b_ada, (0, me * ADA_SHARD), (1, ADA_SHARD))
    ada_part = _ada_fwd(c_all, w_ada[0], b_shard)
    (ada_g,) = _exchange([ada_part], ["gather"], "gather_ada")
    ada = lax.dynamic_index_in_dim(ada_g, me, axis=1, keepdims=False).reshape(3 * D_MODEL)
    shift, scale, gate = ada[:D_MODEL], ada[D_MODEL:2 * D_MODEL], ada[2 * D_MODEL:]
    woa_t = woa_g.reshape(D_MODEL, 512)
    wob_t = wob_g.reshape(D_MODEL, 512)
    wout = wout_g.reshape(D_MODEL, D_MODEL)
    w_all = _assemble_w_all(win_g)

    zrow = jnp.zeros((1, D_MODEL), F32)
    mod = jnp.concatenate([g_norm, scale[None], shift[None], zrow, zrow, zrow, zrow, zrow], axis=0)
    bf_row = jnp.pad(b_f, ((0, 0), (0, CHUNK - N_HEADS)))
    emat_np = np.zeros((3 * CHUNK, AUG_W), np.float32)
    ind_np = np.zeros((512, CHUNK), np.float32)
    for hd in range(N_HEADS):
        for part in range(3):
            emat_np[CHUNK * part + hd, CHUNK * hd + 64 + part] = 1.0
        ind_np[HEAD_DIM * hd:HEAD_DIM * (hd + 1), hd] = 1.0
    emat, ind = jnp.asarray(emat_np, BF16), jnp.asarray(ind_np, BF16)
    sink_rows = jnp.broadcast_to(jnp.pad(sinks.reshape(KV_GROUPS, GROUP), ((0, 0), (0, 8 - GROUP)))[:, :, None],
                                 (KV_GROUPS, 8, CHUNK))

    (h_t, qa, ka, va, za, qb, kb, vb, zb, ga, gb, xf, bounds, norms) = _fwd_proj(
        x2, mod, w_all, cos, sa, sb, bf_row, emat, ind, ta)
    first_key, last_query = _fox_block_ranges(bounds, norms, 1)
    att_a, qa_bwd = _swa_fwd(qa, ka, va, sink_rows, ta)
    att_b, qb_bwd = _fox_fwd(first_key, qb, kb, vb, ta)
    vec = jnp.concatenate([gate[None], g_final[None], zrow, zrow, zrow, zrow, zrow, zrow], axis=0)
    (dx1, ua_t, ub_t, merged_t, d_o, dya, dyb, doa, dob, dzg, post_acc) = _post(
        x2, tgt, att_a, att_b, za, zb, ga, gb, woa_t, wob_t, wout, vec, ind, emat, tm)

    da, dsink = _swa_bwd(ka, va, qa_bwd, doa, cos, sa, sb, sink_rows, ta)
    dkb, dvb, dck, dqb, dcq = _fox_bwd(last_query, kb, vb, qb_bwd, dob, ta)
    dfb, dbf_acc = _fgate_bwd(dcq, dck, xf, ta)
    dproj = [da, dqb, dkb, dvb, dzg, dfb]
    dw_zg, = _token_matmul_multi(h_t, [dzg], ta, "dw_in_zg")
    dw_a, dw_qb, dw_kb, dw_vb, dw_f = _token_matmul_multi(h_t, [da, dqb, dkb, dvb, dfb], ta, "dw_in_rest")
    dw_all = [dw_a, dw_qb, dw_kb, dw_vb, dw_zg, dw_f]

    core, chip = lax.axis_index("c"), 2 * lax.axis_index("x") + lax.axis_index("y")
    win_slots = jnp.stack([jnp.stack([_grad_slot(dw_all, 2 * ch + co) for ch in range(N_CHIPS)]) for co in range(2)])
    (dwoa, dwob, dwout), (win_theirs,) = _token_matmuls_and_swap(
        [(ua_t, dya), (ub_t, dyb), (merged_t, d_o)], ta, [win_slots])
    win_full, win_half = _chip_partial(jnp.reshape(core, (1,)).astype(jnp.int32), win_slots, win_theirs, 256,
                                       "chip_partial_w_in")
    col_slots = lambda g: g.reshape(512, N_DEV, 128).transpose(1, 0, 2)
    grad_x, pre_acc, (p_win, p_woa, p_wob, p_wout) = _bwd_pre(
        dproj, w_all, x2, dx1, mod, ta,
        [win_half, col_slots(dwoa), col_slots(dwob), dwout.reshape(N_DEV, 128, D_MODEL)],
        ["chips", "devices", "devices", "devices"])
    (packs,) = _exchange([_pack_small(pre_acc, post_acc, dbf_acc, dsink)], ["gather"], "gather_small_grads")
    others = jnp.where(jnp.arange(N_CHIPS) == chip, 0.0, 1.0).astype(F32)

    g_win, d_win, m_win, v_win = _adamw_parts(win_full, p_win, others, w_in, m_w_in, v_w_in, 128, "adamw_w_in")
    g_woa, d_woa, m_woa, v_woa = _adamw_devices(p_woa, w_o_swa, m_w_o_swa, v_w_o_swa, 128, "adamw_w_o_swa")
    g_wob, d_wob, m_wob, v_wob = _adamw_devices(p_wob, w_o_fox, m_w_o_fox, v_w_o_fox, 128, "adamw_w_o_fox")
    g_wout, d_wout, m_wout, v_wout = _adamw_devices(p_wout, w_out, m_w_out, v_w_out, 128, "adamw_w_out")
    d_ada_rows = lax.dynamic_slice(packs[:, 0, :], (0, me * ADA_SHARD), (N_DEV, ADA_SHARD))
    c_lanes = jnp.broadcast_to(c_all[:, :, None], (N_DEV, D_MODEL, CHUNK))
    g_wada, d_wada, m_wada, v_wada = _adamw_ada(c_lanes, d_ada_rows, w_ada, m_w_ada, v_w_ada, 256)

    row = lambda a: a.reshape(1, D_MODEL)
    small, loss_rows = _adamw_small(packs, [(b_ada, m_b_ada, v_b_ada), (g_norm, m_g_norm, v_g_norm), (b_f, m_b_f, v_b_f),
                                            (sinks, m_sinks, v_sinks), (row(g_final), row(m_g_final), row(v_g_final))])
    bada_o, gn_o, bf_o, sk_o, gf_o = small

    outs = []
    for k, big in enumerate(((g_wada, g_win, g_woa, g_wob, g_wout), (d_wada, d_win, d_woa, d_wob, d_wout),
                             (m_wada, m_win, m_woa, m_wob, m_wout), (v_wada, v_win, v_woa, v_wob, v_wout))):
        wada_o, win_o, woa_o, wob_o, wout_o = big
        outs += [wada_o, bada_o[k], gn_o[k], win_o, bf_o[k], sk_o[k], woa_o, wob_o, wout_o, gf_o[k].reshape(D_MODEL)]
    return (loss_rows[0, 0], grad_x[None], *outs)
```

```python
import numpy as np
import jax
import jax.numpy as jnp
from jax import lax
from jax.experimental import pallas as pl
from jax.experimental.pallas import tpu as pltpu

F32 = jnp.float32
BF16 = jnp.bfloat16

D_MODEL = 1024
HEAD_DIM = 64
N_HEADS = 8
KV_GROUPS = 2
GROUP = N_HEADS // KV_GROUPS
WINDOW = 128
CHUNK = 128
AUG_W = N_HEADS * CHUNK
N_DEV = 8
IN_SHARD = 673
ADA_SHARD = 384
NORM_EPS = 1e-6
ROPE_THETA = 10000.0
Q_SCALE = HEAD_DIM ** -0.5
NEG = -1e30
UNDERFLOW = 105.0

ADAM_LR = 0.001
ADAM_B1 = 0.9
ADAM_B2 = 0.999
ADAM_EPS = 1e-08
ADAM_WD = 0.01
ADAM_STEP = 10
ADAM_C1 = 1.0 - ADAM_B1 ** ADAM_STEP
ADAM_C2 = 1.0 - ADAM_B2 ** ADAM_STEP

SEG = {}
_off = 0
for _name, _w in (("qa", 512), ("ka", 256), ("va", 256), ("qb", 512), ("kb", 512), ("vb", 512),
                  ("za", 512), ("zb", 512), ("ga", 1024), ("gb", 1024), ("f", 128)):
    SEG[_name] = (_off, _w)
    _off += _w
PROJ_W = _off
DPROJ_GROUPS = (("a", SEG["qa"][0], 1024), ("qb", SEG["qb"][0], 512), ("kb", SEG["kb"][0], 512),
                ("vb", SEG["vb"][0], 512), ("zg", SEG["za"][0], 3072), ("f", SEG["f"][0], 128))

VMEM_LIMIT = 56 * 1024 * 1024


def _params(n_axes):
    return pltpu.CompilerParams(dimension_semantics=("arbitrary",) * n_axes, vmem_limit_bytes=VMEM_LIMIT)


def _resident(a):
    return pl.BlockSpec(a.shape, lambda i: (0, 0), pipeline_mode=pl.Buffered(1))


def _dot(a, b):
    return jnp.dot(a, b, preferred_element_type=F32)


def _dot_nt(a, b):
    return lax.dot_general(a, b, (((1,), (1,)), ((), ())), preferred_element_type=F32)


def _dot_tn(a, b):
    return lax.dot_general(a, b, (((0,), (0,)), ((), ())), preferred_element_type=F32)


def _lane(n):
    return lax.broadcasted_iota(jnp.int32, (1, n), 1)


def _split3(x):
    hi = x.astype(BF16).astype(F32)
    r = x - hi
    mid = r.astype(BF16).astype(F32)
    lo = (r - mid).astype(BF16).astype(F32)
    return hi, mid, lo


def _dot_split(x, b):
    hi, mid, lo = _split3(x)
    return _dot(hi.astype(BF16), b) + _dot(mid.astype(BF16), b) + _dot(lo.astype(BF16), b)


def _spread3(x, e3_ref):
    return _dot(jnp.concatenate(_split3(x), axis=1).astype(BF16), e3_ref[...])


def _place3(lane, base, x, other):
    hi, mid, lo = _split3(x)
    return jnp.where(lane == base, hi, jnp.where(lane == base + 1, mid, jnp.where(lane == base + 2, lo, other)))


def _lane_sum(x, lo, hi):
    lane = _lane(x.shape[1])
    return jnp.sum(jnp.where((lane >= lo) & (lane < hi), x, 0.0), axis=1, keepdims=True)


def _ones_lanes(lo, hi):
    lane = _lane(CHUNK)
    return jnp.where((lane >= lo) & (lane < hi), 1.0, 0.0).astype(F32)


def _rope(c, cos, sa, sb):
    return c * cos + pltpu.roll(c, CHUNK - 32, 1) * sa + pltpu.roll(c, 32, 1) * sb


def _rope_inv(d, cos, sa, sb):
    return d * cos - (pltpu.roll(d, CHUNK - 32, 1) * sa + pltpu.roll(d, 32, 1) * sb)


def _pair(c0, c1):
    return jnp.where(_lane(CHUNK) < HEAD_DIM, c0, pltpu.roll(c1, HEAD_DIM, 1))


def _sigmoid(x):
    return 0.5 * jnp.tanh(0.5 * x) + 0.5


N_CHIPS = 4
ANY_SPEC = pl.BlockSpec(memory_space=pl.ANY)


def _exchange(arrays, modes, name):
    n = len(arrays)

    def body(*refs):
        copies = _exchange_copies(refs[:n], refs[n:2 * n], modes, *refs[2 * n:])
        _exchange_start(copies)
        _exchange_wait(copies)

    return pl.pallas_call(
        body, name=name, out_shape=_exchange_out_shapes(arrays, modes),
        in_specs=[ANY_SPEC] * n, out_specs=[ANY_SPEC] * n, scratch_shapes=_exchange_sems(n),
    )(*arrays)


def _exchange_out_shapes(arrays, modes):
    return [jax.ShapeDtypeStruct((N_DEV,) + a.shape if md == "gather" else a.shape, a.dtype)
            for a, md in zip(arrays, modes)]


def _exchange_sems(n):
    return [pltpu.SemaphoreType.DMA((n, N_DEV - 1)), pltpu.SemaphoreType.DMA((n, N_DEV - 1)),
            pltpu.SemaphoreType.DMA((n,))]


def _exchange_copies(srcs, dsts, modes, send_sems, recv_sems, loc_sems):
    n = len(srcs)
    x, y, c = lax.axis_index("x"), lax.axis_index("y"), lax.axis_index("c")

    def slot(i, px, py, pc):
        return 2 * px + py if modes[i] == "chips" else 4 * px + 2 * py + pc

    def src_of(i, px, py, pc):
        return srcs[i] if modes[i] == "gather" else srcs[i].at[slot(i, px, py, pc)]

    local = [pltpu.make_async_copy(src_of(i, x, y, c), dsts[i].at[slot(i, x, y, c)], loc_sems.at[i])
             for i in range(n)]
    sends, recvs = [], []
    for r in (1, 2, 4, 3, 5, 6, 7):
        px = 1 - x if r & 4 else x
        py = 1 - y if r & 2 else y
        pc = 1 - c if r & 1 else c
        for i in range(n):
            if modes[i] == "chips" and r & 1:
                continue
            sems = dict(send_sem=send_sems.at[i, r - 1], recv_sem=recv_sems.at[i, r - 1],
                        device_id=(px, py, pc), device_id_type=pl.DeviceIdType.MESH)
            sends.append(pltpu.make_async_remote_copy(
                src_ref=src_of(i, px, py, pc), dst_ref=dsts[i].at[slot(i, x, y, c)], **sems))
            recvs.append(pltpu.make_async_remote_copy(
                src_ref=src_of(i, px, py, pc), dst_ref=dsts[i].at[slot(i, px, py, pc)], **sems))
    return local, sends, recvs


def _exchange_start(copies):
    local, sends, _ = copies
    for cp in local + sends:
        cp.start()


def _exchange_wait(copies):
    local, sends, recvs = copies
    for cp in recvs:
        cp.wait_recv()
    for cp in sends:
        cp.wait_send()
    for cp in local:
        cp.wait()


def _gather2_parts(srcs, dsts, send_sems, recv_sems, loc_sems):
    n = len(srcs)
    x, y, c = lax.axis_index("x"), lax.axis_index("y"), lax.axis_index("c")
    me, sibling = (x, y, c), (x, y, 1 - c)
    chips = [(1 - x, y), (x, 1 - y), (1 - x, 1 - y)]

    def rows(i, dev):
        return dsts[i].at[4 * dev[0] + 2 * dev[1] + dev[2]]

    def copy(i, k, block, to, src=None):
        return pltpu.make_async_remote_copy(
            src_ref=rows(i, block) if src is None else src, dst_ref=rows(i, block),
            send_sem=send_sems.at[i, k], recv_sem=recv_sems.at[i, k],
            device_id=to, device_id_type=pl.DeviceIdType.MESH)

    local = [pltpu.make_async_copy(srcs[i], rows(i, me), loc_sems.at[i]) for i in range(n)]
    first = []
    for i in range(n):
        first.append(copy(i, 0, me, sibling, src=srcs[i]))
        first += [copy(i, 1 + j, me, (*chip, c), src=srcs[i]) for j, chip in enumerate(chips)]
    return n, c, me, sibling, chips, copy, local, first


def _gather2_start(srcs, dsts, send_sems, recv_sems, loc_sems):
    *_, local, first = _gather2_parts(srcs, dsts, send_sems, recv_sems, loc_sems)
    for cp in local + first:
        cp.start()


def _gather2_finish(srcs, dsts, send_sems, recv_sems, loc_sems):
    n, c, me, sibling, chips, copy, local, first = _gather2_parts(srcs, dsts, send_sems, recv_sems, loc_sems)
    passed = []
    for j, chip in enumerate(chips):
        for i in range(n):
            copy(i, 1 + j, (*chip, c), me).wait_recv()
            cp = copy(i, 4 + j, (*chip, c), sibling)
            cp.start()
            passed.append(cp)
    for i in range(n):
        copy(i, 0, sibling, me).wait_recv()
        for j, chip in enumerate(chips):
            copy(i, 4 + j, (*chip, 1 - c), me).wait_recv()
    for cp in first + passed:
        cp.wait_send()
    for cp in local:
        cp.wait()


def _sibling_copies(srcs, dsts, send_sems, recv_sems):
    x, y, c = lax.axis_index("x"), lax.axis_index("y"), lax.axis_index("c")
    return [pltpu.make_async_remote_copy(
        src_ref=srcs[i].at[1 - c, k], dst_ref=dsts[i].at[k], send_sem=send_sems.at[i, k], recv_sem=recv_sems.at[i, k],
        device_id=(x, y, 1 - c), device_id_type=pl.DeviceIdType.MESH)
        for i in range(len(srcs)) for k in range(N_CHIPS)]


def _token_matmuls_and_swap(pairs, tk, swap):
    npair, nsw = len(pairs), len(swap)
    s = pairs[0][0].shape[1]
    last = s // tk - 1

    def body(*refs):
        a_refs, b_refs = refs[0:2 * npair:2], refs[1:2 * npair:2]
        src = refs[2 * npair:2 * npair + nsw]
        o_refs = refs[2 * npair + nsw:3 * npair + nsw]
        dst = refs[3 * npair + nsw:3 * npair + 2 * nsw]
        sems = refs[3 * npair + 2 * nsw:3 * npair + 2 * nsw + 2]
        acc_refs = refs[3 * npair + 2 * nsw + 2:]

        @pl.when(pl.program_id(0) == 0)
        def _():
            for cp in _sibling_copies(src, dst, *sems):
                cp.start()

        for a_ref, b_ref, o_ref, acc_ref in zip(a_refs, b_refs, o_refs, acc_refs):
            _accumulate_tokens(a_ref[...], b_ref, acc_ref, o_ref, last)

        @pl.when(pl.program_id(0) == last)
        def _():
            for cp in _sibling_copies(src, dst, *sems):
                cp.wait()

    in_specs, args = [], []
    for a_t, b in pairs:
        in_specs += [pl.BlockSpec((a_t.shape[0], tk), lambda k: (0, k)), pl.BlockSpec((tk, b.shape[1]), lambda k: (k, 0))]
        args += [a_t, b]
    outs = pl.pallas_call(
        body, name="dw_out_projs_swap", grid=(s // tk,),
        out_shape=[jax.ShapeDtypeStruct((a_t.shape[0], b.shape[1]), BF16) for a_t, b in pairs]
        + [jax.ShapeDtypeStruct(a.shape[1:], a.dtype) for a in swap],
        in_specs=in_specs + [ANY_SPEC] * nsw,
        out_specs=[pl.BlockSpec((a_t.shape[0], b.shape[1]), lambda k: (0, 0)) for a_t, b in pairs] + [ANY_SPEC] * nsw,
        scratch_shapes=[pltpu.SemaphoreType.DMA((nsw, N_CHIPS)), pltpu.SemaphoreType.DMA((nsw, N_CHIPS))]
        + [pltpu.VMEM((a_t.shape[0], b.shape[1]), F32) for a_t, b in pairs],
        compiler_params=_params(1))(*args, *swap)
    return outs[:npair], outs[npair:]


def _chip_partial(core, slots, theirs, tr, name):
    _, k, rws, cols = slots.shape

    def body(core_ref, a_ref, b_ref, f_ref, h_ref):
        sm = a_ref[0].astype(F32) + b_ref[...].astype(F32)
        f_ref[...] = sm
        h_ref[...] = sm.astype(BF16)

    blk = pl.BlockSpec((1, tr, cols), lambda j, i, core_ref: (j, i, 0))
    mine = pl.BlockSpec((1, 1, tr, cols), lambda j, i, core_ref: (core_ref[0], j, i, 0))
    return pl.pallas_call(
        body, name=name,
        grid_spec=pltpu.PrefetchScalarGridSpec(num_scalar_prefetch=1, grid=(k, rws // tr),
                                               in_specs=[mine, blk], out_specs=[blk, blk]),
        out_shape=[jax.ShapeDtypeStruct(theirs.shape, F32), jax.ShapeDtypeStruct(theirs.shape, BF16)],
        compiler_params=_params(2))(core, slots, theirs)


def _ada_fwd(c_all, w_shard, b_shard):
    def body(c_ref, w_ref, b_ref, o_ref):
        ch, cm, cl = [t.astype(BF16) for t in _split3(c_ref[...])]
        wh, wm, wl = [t.astype(BF16) for t in _split3(w_ref[...])]
        acc = _dot(ch, wh) + _dot(ch, wm) + _dot(cm, wh) + _dot(ch, wl) + _dot(cl, wh) + _dot(cm, wm)
        o_ref[...] = acc + b_ref[...]

    return pl.pallas_call(body, name="ada_fwd", out_shape=jax.ShapeDtypeStruct((N_DEV, ADA_SHARD), F32),
                          compiler_params=_params(0))(c_all, w_shard, b_shard)


def _rope_tables_and_gather(pos_col, inv_freq, tm, weights, c_rows):
    s = pos_col.shape[0]
    nw = len(weights)
    last = s // tm - 1

    def body(*refs):
        p_ref, f_ref = refs[:2]
        w_src, c_src = refs[2:2 + nw], refs[2 + nw:3 + nw]
        cos_ref, sa_ref, sb_ref = refs[3 + nw:6 + nw]
        w_dst, c_dst = refs[6 + nw:6 + 2 * nw], refs[6 + 2 * nw:7 + 2 * nw]
        w_sems, c_sems = refs[7 + 2 * nw:10 + 2 * nw], refs[10 + 2 * nw:]

        @pl.when(pl.program_id(0) == 0)
        def _():
            _gather2_start(w_src, w_dst, *w_sems)
            _exchange_start(_exchange_copies(c_src, c_dst, ["gather"], *c_sems))

        ang = p_ref[...] * f_ref[...]
        sin = jnp.sin(ang)
        first_half = (_lane(CHUNK) & (HEAD_DIM - 1)) < HEAD_DIM // 2
        cos_ref[...] = jnp.cos(ang)
        sa_ref[...] = jnp.where(first_half, -sin, 0.0)
        sb_ref[...] = jnp.where(first_half, 0.0, sin)

        @pl.when(pl.program_id(0) == last)
        def _():
            _exchange_wait(_exchange_copies(c_src, c_dst, ["gather"], *c_sems))
            _gather2_finish(w_src, w_dst, *w_sems)

    tab = jax.ShapeDtypeStruct((s, CHUNK), F32)
    blk = pl.BlockSpec((tm, CHUNK), lambda i: (i, 0))
    gathered = [jax.ShapeDtypeStruct((N_DEV,) + a.shape, a.dtype) for a in list(weights) + [c_rows]]
    outs = pl.pallas_call(
        body, name="rope_tables_gather", grid=(s // tm,), out_shape=[tab, tab, tab] + gathered,
        in_specs=[pl.BlockSpec((tm, 1), lambda i: (i, 0)), pl.BlockSpec((1, CHUNK), lambda i: (0, 0))]
        + [ANY_SPEC] * (nw + 1),
        out_specs=[blk, blk, blk] + [ANY_SPEC] * (nw + 1),
        scratch_shapes=_exchange_sems(nw) + _exchange_sems(1),
        compiler_params=_params(1))(pos_col, inv_freq, *weights, c_rows)
    return outs[:3], outs[3:3 + nw], outs[3 + nw]


def _fwd_proj(x, mod, w_all, cos, sa, sb, bf_row, emat, ind, tm):
    s = x.shape[0]

    def body(x_ref, mod_ref, w_ref, cos_ref, sa_ref, sb_ref, bf_ref, e_ref, ind_ref,
             h_ref, qa_ref, ka_ref, va_ref, za_ref, qb_ref, kb_ref, vb_ref, zb_ref, ga_ref, gb_ref, f_ref,
             bnd_ref, nrm_ref, carry_ref):
        @pl.when(pl.program_id(0) == 0)
        def _():
            carry_ref[...] = jnp.zeros_like(carry_ref)

        xv = x_ref[...]
        rstd = lax.rsqrt(jnp.mean(xv * xv, axis=-1, keepdims=True) + NORM_EPS)
        h = (xv * rstd * mod_ref[0:1, :]) * (1.0 + mod_ref[1:2, :]) + mod_ref[2:3, :]
        hb = h.astype(BF16)
        h_ref[...] = hb.T

        def seg(name):
            off, w = SEG[name]
            return _dot(hb, w_ref[:, off:off + w])

        lane = _lane(CHUNK)
        low = lane < HEAD_DIM
        q_ones = _ones_lanes(64, 67)
        k_ones = _ones_lanes(67, 70)
        cos_t, sa_t, sb_t = cos_ref[...], sa_ref[...], sb_ref[...]

        def write_heads(ref, nat, extra_of, rope, scale):
            for p in range(N_HEADS // 2):
                c = nat[:, CHUNK * p:CHUNK * (p + 1)]
                if rope:
                    c = _rope(c, cos_t, sa_t, sb_t)
                if scale != 1.0:
                    c = c * scale
                for hh in range(2):
                    hd = 2 * p + hh
                    src = c if hh == 0 else pltpu.roll(c, HEAD_DIM, 1)
                    ref[:, CHUNK * hd:CHUNK * (hd + 1)] = jnp.where(low, src, extra_of(hd)).astype(BF16)

        write_heads(qa_ref, seg("qa"), lambda hd: q_ones, True, Q_SCALE)
        ka = seg("ka")
        va = seg("va")
        for kv in range(KV_GROUPS):
            sl = slice(CHUNK * kv, CHUNK * (kv + 1))
            ka_ref[:, sl] = jnp.where(low, _rope(ka[:, sl], cos_t, sa_t, sb_t), k_ones).astype(BF16)
            va_ref[:, sl] = jnp.where(low, va[:, sl], q_ones).astype(BF16)
        za_ref[...] = seg("za")
        zb_ref[...] = seg("zb")
        ga_ref[...] = seg("ga")
        gb_ref[...] = seg("gb")

        xf = seg("f") + bf_ref[...]
        f_ref[...] = xf
        logf = jnp.minimum(xf, 0.0) - jnp.log1p(jnp.exp(-jnp.abs(xf)))
        row = lax.broadcasted_iota(jnp.int32, (tm, tm), 0)
        col = lax.broadcasted_iota(jnp.int32, (tm, tm), 1)
        tri = jnp.where(col <= row, 1.0, 0.0).astype(BF16)
        hi, mid, lo = _split3(logf)
        cum = _dot(tri, hi.astype(BF16)) + _dot(tri, mid.astype(BF16)) + _dot(tri, lo.astype(BF16))
        cum = cum + carry_ref[0:1, :]
        carry_ref[0:1, :] += jnp.sum(logf, axis=0, keepdims=True)
        lane_all = _lane(AUG_W) & (CHUNK - 1)
        bias = _spread3(-cum, e_ref) + jnp.where((lane_all >= 67) & (lane_all < 70), 1.0, 0.0)

        qb, kb = seg("qb"), seg("kb")
        write_heads(qb_ref, qb, lambda hd: q_ones, False, Q_SCALE)
        write_heads(kb_ref, kb, lambda hd: bias[:, CHUNK * hd:CHUNK * (hd + 1)], False, 1.0)
        write_heads(vb_ref, seg("vb"), lambda hd: q_ones, False, 1.0)

        rid = lax.broadcasted_iota(jnp.int32, (tm, CHUNK), 0)
        bnd_ref[...] = jnp.zeros_like(bnd_ref)
        bnd_ref[0, 0:1, :] = jnp.sum(jnp.where(rid == 0, cum, 0.0), axis=0, keepdims=True)
        bnd_ref[0, 1:2, :] = jnp.sum(jnp.where(rid == tm - 1, cum, 0.0), axis=0, keepdims=True)

        @pl.when(pl.program_id(0) == 0)
        def _():
            nrm_ref[...] = jnp.zeros_like(nrm_ref)

        for r_, nat in ((0, qb * Q_SCALE), (1, kb)):
            sq = _dot((nat * nat).astype(BF16), ind_ref[...])
            nrm_ref[r_:r_ + 1, :] = jnp.maximum(nrm_ref[r_:r_ + 1, :], jnp.max(sq, axis=0, keepdims=True))

    row_blk = lambda w: pl.BlockSpec((tm, w), lambda i: (i, 0))
    full = lambda a: pl.BlockSpec(a.shape, lambda i: (0,) * a.ndim)
    sds = lambda w, dt: jax.ShapeDtypeStruct((s, w), dt)
    out_shape = [sds(AUG_W, BF16), sds(KV_GROUPS * CHUNK, BF16), sds(KV_GROUPS * CHUNK, BF16),
                 sds(512, F32), sds(AUG_W, BF16), sds(AUG_W, BF16), sds(AUG_W, BF16), sds(512, F32),
                 sds(D_MODEL, F32), sds(D_MODEL, F32), sds(CHUNK, F32)]
    small = [jax.ShapeDtypeStruct((s // tm, 8, CHUNK), F32), jax.ShapeDtypeStruct((8, CHUNK), F32)]
    return pl.pallas_call(
        body, name="fwd_proj", grid=(s // tm,),
        out_shape=[jax.ShapeDtypeStruct((D_MODEL, s), BF16)] + out_shape + small,
        in_specs=[row_blk(D_MODEL), full(mod), _resident(w_all), row_blk(CHUNK), row_blk(CHUNK), row_blk(CHUNK),
                  full(bf_row), full(emat), full(ind)],
        out_specs=[pl.BlockSpec((D_MODEL, tm), lambda i: (0, i))] + [row_blk(o.shape[1]) for o in out_shape] + [
            pl.BlockSpec((1, 8, CHUNK), lambda i: (i, 0, 0)), pl.BlockSpec((8, CHUNK), lambda i: (0, 0))],
        scratch_shapes=[pltpu.VMEM((8, CHUNK), F32)],
        compiler_params=_params(1))(x, mod, w_all, cos, sa, sb, bf_row, emat, ind)


def _swa_fwd(q_aug, k_aug, v_aug, sink_rows, tq):
    s = q_aug.shape[0]
    r = tq // WINDOW
    gw = GROUP * CHUNK

    def body(q_ref, kc_ref, kp_ref, vc_ref, vp_ref, sink_ref, o_ref, qb_ref):
        i = pl.program_id(1)
        st = GROUP * WINDOW
        qloc = lax.broadcasted_iota(jnp.int32, (st, 2 * WINDOW), 0) & (WINDOW - 1)
        col = lax.broadcasted_iota(jnp.int32, (st, 2 * WINDOW), 1)
        band = (col > qloc) & (col <= qloc + WINDOW)
        head = jnp.right_shift(lax.broadcasted_iota(jnp.int32, (st, 1), 0), 7)
        sink = jnp.zeros((st, 1), F32)
        for g in range(GROUP):
            sink = jnp.where(head == g, jnp.max(sink_ref[0, g:g + 1, :], axis=1, keepdims=True), sink)
        lane = _lane(CHUNK)
        for sub in range(r):
            rows = slice(WINDOW * sub, WINDOW * (sub + 1))
            q = jnp.concatenate([q_ref[rows, CHUNK * g:CHUNK * (g + 1)] for g in range(GROUP)], axis=0)
            if sub == 0:
                k = jnp.concatenate([kp_ref[...], kc_ref[rows, :]], axis=0)
                v = jnp.concatenate([vp_ref[...], vc_ref[rows, :]], axis=0)
                valid = band & ((col >= WINDOW) | (i > 0))
            else:
                both = slice(WINDOW * (sub - 1), WINDOW * (sub + 1))
                k, v, valid = kc_ref[both, :], vc_ref[both, :], band
            sc = jnp.where(valid, _dot_nt(q, k), NEG)
            m = jnp.maximum(jnp.max(sc, axis=1, keepdims=True), sink)
            acc = _dot(jnp.exp((sc - m).astype(BF16)), v)
            denom = _lane_sum(acc, 64, 65) + jnp.exp(sink - m)
            out = acc * (1.0 / denom)
            aug = _place3(lane, 67, -(m + jnp.log(denom)), q.astype(F32)).astype(BF16)
            hrows = lambda a, g: a[WINDOW * g:WINDOW * (g + 1), :]
            for g in range(GROUP):
                qb_ref[rows, CHUNK * g:CHUNK * (g + 1)] = hrows(aug, g)
            for pp in range(GROUP // 2):
                o_ref[rows, CHUNK * pp:CHUNK * (pp + 1)] = _pair(hrows(out, 2 * pp), hrows(out, 2 * pp + 1))

    return pl.pallas_call(
        body, name="swa_fwd", grid=(KV_GROUPS, s // tq),
        out_shape=[jax.ShapeDtypeStruct((s, 512), F32), jax.ShapeDtypeStruct((s, AUG_W), BF16)],
        in_specs=[pl.BlockSpec((tq, gw), lambda kv, i: (i, kv)),
                  pl.BlockSpec((tq, CHUNK), lambda kv, i: (i, kv)),
                  pl.BlockSpec((WINDOW, CHUNK), lambda kv, i: (jnp.maximum(i * r - 1, 0), kv)),
                  pl.BlockSpec((tq, CHUNK), lambda kv, i: (i, kv)),
                  pl.BlockSpec((WINDOW, CHUNK), lambda kv, i: (jnp.maximum(i * r - 1, 0), kv)),
                  pl.BlockSpec((1, 8, CHUNK), lambda kv, i: (kv, 0, 0))],
        out_specs=[pl.BlockSpec((tq, GROUP * HEAD_DIM), lambda kv, i: (i, kv)),
                   pl.BlockSpec((tq, gw), lambda kv, i: (i, kv))],
        compiler_params=_params(2))(q_aug, k_aug, k_aug, v_aug, v_aug, sink_rows)


def _swa_bwd(k_aug, v_aug, q_bwd, do_aug, cos, sa, sb, sink_rows, tk):
    s = k_aug.shape[0]
    r = tk // WINDOW
    nt = s // tk
    nb = s // WINDOW
    gw = GROUP * CHUNK

    def body(k_ref, v_ref, q_ref, qn_ref, do_ref, don_ref, cos_ref, sa_ref, sb_ref, sink_ref,
             da_ref, dsink_ref, carry_ref, acc_ref):
        j = pl.program_id(0)

        @pl.when(j == 0)
        def _():
            carry_ref[...] = jnp.zeros_like(carry_ref)
            dsink_ref[...] = jnp.zeros_like(dsink_ref)

        acc_ref[...] = jnp.zeros_like(acc_ref)
        acc_ref[0:WINDOW, :] = carry_ref[...]
        dk_col, dv_col = SEG["ka"][0] - SEG["qa"][0], SEG["va"][0] - SEG["qa"][0]
        st = GROUP * WINDOW
        key = lax.broadcasted_iota(jnp.int32, (WINDOW, 2 * st), 0)
        col = lax.broadcasted_iota(jnp.int32, (WINDOW, 2 * st), 1)
        qloc = col & (WINDOW - 1)
        band = ((col < st) & (key <= qloc)) | ((col >= st) & (key > qloc))
        lane = _lane(CHUNK)
        low = lane < HEAD_DIM
        cos_t, sa_t, sb_t = cos_ref[...], sa_ref[...], sb_ref[...]
        for kv, sub in [(kv, sub) for kv in range(KV_GROUPS) for sub in range(r)]:
            heads = [slice(CHUNK * (GROUP * kv + g), CHUNK * (GROUP * kv + g + 1)) for g in range(GROUP)]
            kvs = slice(CHUNK * kv, CHUNK * (kv + 1))
            rows = slice(WINDOW * sub, WINDOW * (sub + 1))
            nxt = slice(WINDOW * (sub + 1), WINDOW * (sub + 2))
            k, v = k_ref[rows, kvs], v_ref[rows, kvs]
            q_cur, do_cur = [q_ref[rows, cs] for cs in heads], [do_ref[rows, cs] for cs in heads]
            if sub < r - 1:
                q_nxt, do_nxt, valid = [q_ref[nxt, cs] for cs in heads], [do_ref[nxt, cs] for cs in heads], band
            else:
                q_nxt, do_nxt = [qn_ref[:, cs] for cs in heads], [don_ref[:, cs] for cs in heads]
                valid = band & ((col < st) | (j < nt - 1))
            q = jnp.concatenate(q_cur + q_nxt, axis=0)
            do = jnp.concatenate(do_cur + do_nxt, axis=0)
            pt = jnp.exp(jnp.where(valid, _dot_nt(k, q), NEG))
            ds = (pt * _dot_nt(v, do)).astype(BF16)
            dv = _dot(pt.astype(BF16), do)
            dk = _dot(ds, q)
            dq = _dot_tn(ds, k)
            for g, cs in enumerate(heads):
                acc_ref[rows, cs] += dq[WINDOW * g:WINDOW * (g + 1), :]
                dqn = dq[st + WINDOW * g:st + WINDOW * (g + 1), :]
                if sub < r - 1:
                    acc_ref[nxt, cs] += dqn
                else:
                    carry_ref[:, cs] = dqn
                sink = jnp.max(sink_ref[kv, g:g + 1, :], axis=1, keepdims=True)
                p_sink = jnp.exp(sink + _lane_sum(q_cur[g].astype(F32), 67, 70))
                term = jnp.sum(p_sink * _lane_sum(do_cur[g].astype(F32), 64, 67), axis=0, keepdims=True)
                dsink_ref[kv, g:g + 1, :] += jnp.broadcast_to(term, (1, CHUNK))
            da_ref[rows, dk_col + CHUNK * kv:dk_col + CHUNK * (kv + 1)] = _rope_inv(
                jnp.where(low, dk, 0.0), cos_t[rows, :], sa_t[rows, :], sb_t[rows, :]).astype(BF16)
            da_ref[rows, dv_col + CHUNK * kv:dv_col + CHUNK * (kv + 1)] = jnp.where(low, dv, 0.0).astype(BF16)
        for pp in range(N_HEADS // 2):
            d = _pair(acc_ref[:, CHUNK * 2 * pp:CHUNK * (2 * pp + 1)], acc_ref[:, CHUNK * (2 * pp + 1):CHUNK * (2 * pp + 2)])
            da_ref[:, CHUNK * pp:CHUNK * (pp + 1)] = (_rope_inv(d, cos_t, sa_t, sb_t) * Q_SCALE).astype(BF16)

    cur = lambda w: pl.BlockSpec((tk, w), lambda j: (j, 0))
    nxt = pl.BlockSpec((WINDOW, AUG_W), lambda j: (jnp.minimum((j + 1) * r, nb - 1), 0))
    whole = pl.BlockSpec((KV_GROUPS, 8, CHUNK), lambda j: (0, 0, 0))
    return pl.pallas_call(
        body, name="swa_bwd", grid=(nt,),
        out_shape=[jax.ShapeDtypeStruct((s, 1024), BF16), jax.ShapeDtypeStruct((KV_GROUPS, 8, CHUNK), F32)],
        in_specs=[cur(KV_GROUPS * CHUNK), cur(KV_GROUPS * CHUNK), cur(AUG_W), nxt, cur(AUG_W), nxt,
                  cur(CHUNK), cur(CHUNK), cur(CHUNK), whole],
        out_specs=[cur(1024), whole],
        scratch_shapes=[pltpu.VMEM((WINDOW, AUG_W), F32), pltpu.VMEM((tk, AUG_W), F32)],
        compiler_params=_params(1))(k_aug, v_aug, q_bwd, q_bwd, do_aug, do_aug, cos, sa, sb, sink_rows)


def _fox_fwd(first_key, q_aug, k_aug, v_aug, t):
    s = q_aug.shape[0]
    pw = 2 * CHUNK

    def body(lo_ref, q_ref, k_ref, v_ref, o_ref, qb_ref):
        i = pl.program_id(1)
        first = lo_ref[pl.program_id(0), i]
        row = lax.broadcasted_iota(jnp.int32, (t, t), 0)
        col = lax.broadcasted_iota(jnp.int32, (t, t), 1)
        lane = _lane(CHUNK)
        heads = (slice(0, CHUNK), slice(CHUNK, pw))
        qs = [q_ref[:, hs] for hs in heads]

        def step(jb, carry, masked):
            rows = pl.ds(pl.multiple_of(jb * t, t), t)
            scs = [_dot_nt(qs[hh], k_ref[rows, hs]) for hh, hs in enumerate(heads)]
            if masked:
                scs = [jnp.where(col <= row, sc, NEG) for sc in scs]
            m_news = [jnp.maximum(carry[2 * hh], jnp.max(scs[hh], axis=1, keepdims=True)) for hh in range(2)]
            ps = [jnp.exp((scs[hh] - m_news[hh]).astype(BF16)) for hh in range(2)]
            new = []
            for hh, hs in enumerate(heads):
                m, acc = carry[2 * hh], carry[2 * hh + 1]
                new += [m_news[hh], jnp.exp(m - m_news[hh]) * acc + _dot(ps[hh], v_ref[rows, hs])]
            return tuple(new)

        init = (jnp.full((t, 1), NEG, F32), jnp.zeros((t, CHUNK), F32)) * 2
        carry = step(i, lax.fori_loop(first, i, lambda jb, c: step(jb, c, False), init), True)
        outs = []
        for hh, hs in enumerate(heads):
            m, acc = carry[2 * hh], carry[2 * hh + 1]
            denom = _lane_sum(acc, 64, 65)
            outs.append(acc * (1.0 / denom))
            lse = m + jnp.log(denom)
            qb_ref[:, hs] = _place3(lane, 67, -lse, qs[hh].astype(F32)).astype(BF16)
        o_ref[...] = _pair(outs[0], outs[1])

    return pl.pallas_call(
        body, name="fox_fwd", grid=(N_HEADS // 2, s // t),
        out_shape=[jax.ShapeDtypeStruct((s, 512), F32), jax.ShapeDtypeStruct((s, AUG_W), BF16)],
        in_specs=[pl.BlockSpec(memory_space=pltpu.SMEM),
                  pl.BlockSpec((t, pw), lambda p, i: (i, p)),
                  pl.BlockSpec((s, pw), lambda p, i: (0, p)),
                  pl.BlockSpec((s, pw), lambda p, i: (0, p))],
        out_specs=[pl.BlockSpec((t, CHUNK), lambda p, i: (i, p)), pl.BlockSpec((t, pw), lambda p, i: (i, p))],
        compiler_params=_params(2))(first_key, q_aug, k_aug, v_aug)


def _fox_block_ranges(bounds, norms, r):
    cum_first = bounds[0::r, 0, :N_HEADS]
    cum_last = bounds[r - 1::r, 1, :N_HEADS]
    n = cum_first.shape[0]
    reach = 2.0 * 1.02 * jnp.sqrt(norms[0, :N_HEADS] * norms[1, :N_HEADS]) + UNDERFLOW
    blk = jnp.arange(n, dtype=jnp.int32)
    decay = cum_last[None, :, :] - cum_first[:, None, :]
    skip = (decay > reach[None, None, :]) & (blk[None, :, None] < blk[:, None, None])
    first = jnp.min(jnp.where(skip, n, blk[None, :, None]), axis=1)
    first = jnp.minimum(first[:, 0::2], first[:, 1::2]).T
    needed = (first[:, :, None] <= blk[None, None, :]) & (blk[None, :, None] >= blk[None, None, :])
    last = jnp.max(jnp.where(needed, blk[None, :, None], blk[None, None, :]), axis=1)
    return first.astype(jnp.int32), last.astype(jnp.int32)


def _fox_bwd(last_query, k_aug, v_aug, q_bwd, do_aug, t):
    s = k_aug.shape[0]
    n = s // t
    pw = 2 * CHUNK

    def body(hi_ref, k_ref, v_ref, q_ref, do_ref, dk_ref, dv_ref, dck_ref, dq_ref, dcq_ref, dq_scr):
        j = pl.program_id(1)
        last = hi_ref[pl.program_id(0), j]

        @pl.when(j == 0)
        def _():
            dq_scr[...] = jnp.zeros_like(dq_scr)

        row = lax.broadcasted_iota(jnp.int32, (t, t), 0)
        col = lax.broadcasted_iota(jnp.int32, (t, t), 1)
        lane = _lane(CHUNK)
        heads = (slice(0, CHUNK), slice(CHUNK, pw))
        ks = [k_ref[:, hs] for hs in heads]
        vs = [v_ref[:, hs] for hs in heads]

        def step(ib, carry, masked):
            rows = pl.ds(pl.multiple_of(ib * t, t), t)
            new = []
            for hh, hs in enumerate(heads):
                dv, dk = carry[2 * hh], carry[2 * hh + 1]
                q, do = q_ref[rows, hs], do_ref[rows, hs]
                st = _dot_nt(ks[hh], q)
                if masked:
                    st = jnp.where(row <= col, st, NEG)
                pt = jnp.exp(st)
                ds = (pt * _dot_nt(vs[hh], do)).astype(BF16)
                new += [dv + _dot(pt.astype(BF16), do), dk + _dot(ds, q)]
                dq_scr[rows, hs] += _dot_tn(ds, ks[hh])
            return tuple(new)

        zero = jnp.zeros((t, CHUNK), F32)
        carry = lax.fori_loop(j + 1, last + 1, lambda ib, c: step(ib, c, False), step(j, (zero,) * 4, True))
        dvs, dks = (carry[0], carry[2]), (carry[1], carry[3])
        dk_ref[...] = _pair(dks[0], dks[1]).astype(BF16)
        dv_ref[...] = _pair(dvs[0], dvs[1]).astype(BF16)
        dck_ref[0] = jnp.where(lane == 0, pltpu.roll(dks[0], 64, 1),
                               jnp.where(lane == 1, pltpu.roll(dks[1], 65, 1), 0.0))

        @pl.when(j == n - 1)
        def _():
            for ib in range(n):
                rows = slice(t * ib, t * (ib + 1))
                d0, d1 = dq_scr[rows, 0:CHUNK], dq_scr[rows, CHUNK:pw]
                dq_ref[rows, :] = (_pair(d0, d1) * Q_SCALE).astype(BF16)
                dcq_ref[0, rows, :] = jnp.where(lane == 0, pltpu.roll(d0, CHUNK - 67, 1),
                                                jnp.where(lane == 1, pltpu.roll(d1, CHUNK - 66, 1), 0.0))

    return pl.pallas_call(
        body, name="fox_bwd", grid=(N_HEADS // 2, n),
        out_shape=[jax.ShapeDtypeStruct((s, 512), BF16), jax.ShapeDtypeStruct((s, 512), BF16),
                   jax.ShapeDtypeStruct((N_HEADS // 2, s, CHUNK), F32), jax.ShapeDtypeStruct((s, 512), BF16),
                   jax.ShapeDtypeStruct((N_HEADS // 2, s, CHUNK), F32)],
        in_specs=[pl.BlockSpec(memory_space=pltpu.SMEM),
                  pl.BlockSpec((t, pw), lambda p, j: (j, p)), pl.BlockSpec((t, pw), lambda p, j: (j, p)),
                  pl.BlockSpec((s, pw), lambda p, j: (0, p)), pl.BlockSpec((s, pw), lambda p, j: (0, p))],
        out_specs=[pl.BlockSpec((t, CHUNK), lambda p, j: (j, p)), pl.BlockSpec((t, CHUNK), lambda p, j: (j, p)),
                   pl.BlockSpec((1, t, CHUNK), lambda p, j: (p, j, 0)), pl.BlockSpec((s, CHUNK), lambda p, j: (0, p)),
                   pl.BlockSpec((1, s, CHUNK), lambda p, j: (p, 0, 0))],
        scratch_shapes=[pltpu.VMEM((s, pw), F32)],
        compiler_params=_params(2))(last_query, k_aug, v_aug, q_bwd, do_aug)


def _fgate_bwd(dcq, dck, xf, tm):
    s = xf.shape[0]
    nt = s // tm

    def body(dq_ref, dc_ref, xf_ref, df_ref, dbf_ref, carry_ref):
        @pl.when(pl.program_id(0) == 0)
        def _():
            carry_ref[...] = jnp.zeros_like(carry_ref)
            dbf_ref[...] = jnp.zeros_like(dbf_ref)

        row = lax.broadcasted_iota(jnp.int32, (tm, tm), 0)
        col = lax.broadcasted_iota(jnp.int32, (tm, tm), 1)
        tri = jnp.where(col >= row, 1.0, 0.0).astype(BF16)
        dcum = dq_ref[0] - dc_ref[0]
        for p in range(1, N_HEADS // 2):
            dcum = dcum + pltpu.roll(dq_ref[p] - dc_ref[p], 2 * p, 1)
        hi, mid, lo = _split3(dcum)
        dlogf = _dot(tri, hi.astype(BF16)) + _dot(tri, mid.astype(BF16)) + _dot(tri, lo.astype(BF16))
        dlogf = dlogf + carry_ref[0:1, :]
        carry_ref[0:1, :] += jnp.sum(dcum, axis=0, keepdims=True)
        df = dlogf / (1.0 + jnp.exp(xf_ref[...]))
        df_ref[...] = df.astype(BF16)
        dbf_ref[0:1, :] += jnp.sum(df, axis=0, keepdims=True)

    rev = pl.BlockSpec((tm, CHUNK), lambda i: (nt - 1 - i, 0))
    rev4 = pl.BlockSpec((N_HEADS // 2, tm, CHUNK), lambda i: (0, nt - 1 - i, 0))
    return pl.pallas_call(
        body, name="fgate_bwd", grid=(nt,),
        out_shape=[jax.ShapeDtypeStruct((s, CHUNK), BF16), jax.ShapeDtypeStruct((8, CHUNK), F32)],
        in_specs=[rev4, rev4, rev], out_specs=[rev, pl.BlockSpec((8, CHUNK), lambda i: (0, 0))],
        scratch_shapes=[pltpu.VMEM((8, CHUNK), F32)],
        compiler_params=_params(1))(dcq, dck, xf)


def _post(x, tgt, att_a, att_b, za, zb, ga, gb, woa_t, wob_t, wout, vec, ind, emat, tm):
    s = x.shape[0]

    def body(x_ref, t_ref, aa_ref, ab_ref, za_ref, zb_ref, ga_ref, gb_ref, woat_ref, wobt_ref, wout_ref,
             vec_ref, ind_ref, e_ref,
             dx1_ref, ua_ref, ub_ref, mg_ref, do_ref, dya_ref, dyb_ref, doa_ref, dob_ref, dzg_ref, acc_ref):
        @pl.when(pl.program_id(0) == 0)
        def _():
            acc_ref[...] = jnp.zeros_like(acc_ref)

        gate, gfin = vec_ref[0:1, :], vec_ref[1:2, :]
        inv_d = 1.0 / D_MODEL

        def branch_fwd(att_ref, z_ref, w_ref, u_ref):
            att, z = att_ref[...], z_ref[...]
            sz = _sigmoid(z)
            silu = z * sz
            u = (att * silu).astype(BF16)
            u_ref[...] = u.T
            return att, z, sz, silu, _dot_nt(u, w_ref[...])

        att_a, z_a, sz_a, silu_a, y_a = branch_fwd(aa_ref, za_ref, woat_ref, ua_ref)
        att_b, z_b, sz_b, silu_b, y_b = branch_fwd(ab_ref, zb_ref, wobt_ref, ub_ref)
        sg_a, sg_b = _sigmoid(ga_ref[...]), _sigmoid(gb_ref[...])
        merged = (sg_a * y_a + sg_b * y_b).astype(BF16)
        mg_ref[...] = merged.T
        o = _dot(merged, wout_ref[...])
        x1 = x_ref[...] + gate * o
        rstd = lax.rsqrt(jnp.mean(x1 * x1, axis=-1, keepdims=True) + NORM_EPS)
        xh = x1 * rstd
        diff = xh * gfin - t_ref[...]
        acc_ref[2:3, :] += (0.5 * inv_d) * jnp.sum(diff * diff, axis=0, keepdims=True)
        dy = diff * inv_d
        acc_ref[1:2, :] += jnp.sum(dy * xh, axis=0, keepdims=True)
        dyg = dy * gfin
        dx1 = rstd * (dyg - xh * jnp.mean(dyg * xh, axis=-1, keepdims=True))
        dx1_ref[...] = dx1
        acc_ref[0:1, :] += jnp.sum(dx1 * o, axis=0, keepdims=True)
        d_o = (dx1 * gate).astype(BF16)
        do_ref[...] = d_o
        dmg = _dot_nt(d_o, wout_ref[...])

        lane = _lane(CHUNK)
        low = lane < HEAD_DIM
        lane_all = _lane(AUG_W) & (CHUNK - 1)

        zg0 = SEG["za"][0]

        def branch_bwd(sg, y, att, z, sz, silu, wt_ref, dy_ref, g_name, z_name, doaug_ref):
            dyb = (dmg * sg).astype(BF16)
            dy_ref[...] = dyb
            g_off, z_off = SEG[g_name][0] - zg0, SEG[z_name][0] - zg0
            dzg_ref[:, g_off:g_off + D_MODEL] = (dmg * y * sg * (1.0 - sg)).astype(BF16)
            du = _dot(dyb, wt_ref[...])
            datt = du * silu
            dzg_ref[:, z_off:z_off + 512] = (du * att * (sz * (1.0 + z * (1.0 - sz)))).astype(BF16)
            extra = _spread3(-_dot_split(datt * att, ind_ref[...]), e_ref)
            for p in range(N_HEADS // 2):
                c = datt[:, CHUNK * p:CHUNK * (p + 1)]
                for hh in range(2):
                    hd = 2 * p + hh
                    src = c if hh == 0 else pltpu.roll(c, HEAD_DIM, 1)
                    doaug_ref[:, CHUNK * hd:CHUNK * (hd + 1)] = jnp.where(
                        low, src, extra[:, CHUNK * hd:CHUNK * (hd + 1)]).astype(BF16)

        branch_bwd(sg_a, y_a, att_a, z_a, sz_a, silu_a, woat_ref, dya_ref, "ga", "za", doa_ref)
        branch_bwd(sg_b, y_b, att_b, z_b, sz_b, silu_b, wobt_ref, dyb_ref, "gb", "zb", dob_ref)

    row_blk = lambda w: pl.BlockSpec((tm, w), lambda i: (i, 0))
    full = lambda a: pl.BlockSpec(a.shape, lambda i: (0,) * a.ndim)
    sds = lambda w, dt: jax.ShapeDtypeStruct((s, w), dt)
    tds = lambda w: jax.ShapeDtypeStruct((w, s), BF16)
    col_blk = lambda w: pl.BlockSpec((w, tm), lambda i: (0, i))
    out_shape = [sds(D_MODEL, F32), tds(512), tds(512), tds(D_MODEL), sds(D_MODEL, BF16),
                 sds(D_MODEL, BF16), sds(D_MODEL, BF16), sds(AUG_W, BF16), sds(AUG_W, BF16), sds(3072, BF16),
                 jax.ShapeDtypeStruct((8, D_MODEL), F32)]
    ins = [x, tgt, att_a, att_b, za, zb, ga, gb, woa_t, wob_t, wout, vec, ind, emat]
    in_specs = [row_blk(a.shape[1]) for a in ins[:8]] + [full(a) for a in ins[8:]]
    out_specs = ([row_blk(D_MODEL), col_blk(512), col_blk(512), col_blk(D_MODEL)]
                 + [row_blk(o.shape[1]) for o in out_shape[4:-1]] + [pl.BlockSpec((8, D_MODEL), lambda i: (0, 0))])
    return pl.pallas_call(body, name="post", grid=(s // tm,), out_shape=out_shape, in_specs=in_specs,
                          out_specs=out_specs, compiler_params=_params(1))(*ins)


def _bwd_pre(dproj, w_all, x, dx1, mod, tm, chip_halves, modes):
    s = x.shape[0]
    ng, nh = len(DPROJ_GROUPS), len(chip_halves)
    last = s // tm - 1

    def body(*refs):
        dp_refs = refs[:ng]
        wt_ref, x_ref, dx1_ref, mod_ref = refs[ng:ng + 4]
        send_refs = refs[ng + 4:ng + 4 + nh]
        gx_ref, acc_ref = refs[ng + 4 + nh:ng + 6 + nh]
        recv_refs = refs[ng + 6 + nh:ng + 6 + 2 * nh]
        sems = refs[ng + 6 + 2 * nh:]

        @pl.when(pl.program_id(0) == 0)
        def _():
            acc_ref[...] = jnp.zeros_like(acc_ref)
            _exchange_start(_exchange_copies(send_refs, recv_refs, modes, *sems))

        dh = None
        for dp_ref, (_, off, w) in zip(dp_refs, DPROJ_GROUPS):
            part = _dot_nt(dp_ref[...], wt_ref[:, off:off + w])
            dh = part if dh is None else dh + part
        xv = x_ref[...]
        rstd = lax.rsqrt(jnp.mean(xv * xv, axis=-1, keepdims=True) + NORM_EPS)
        xh = xv * rstd
        gn = mod_ref[0:1, :]
        acc_ref[0:1, :] += jnp.sum(dh, axis=0, keepdims=True)
        acc_ref[1:2, :] += jnp.sum(dh * (xh * gn), axis=0, keepdims=True)
        dn = dh * (1.0 + mod_ref[1:2, :])
        acc_ref[2:3, :] += jnp.sum(dn * xh, axis=0, keepdims=True)
        dxh = dn * gn
        gx_ref[...] = dx1_ref[...] + rstd * (dxh - xh * jnp.mean(dxh * xh, axis=-1, keepdims=True))

        @pl.when(pl.program_id(0) == last)
        def _():
            _exchange_wait(_exchange_copies(send_refs, recv_refs, modes, *sems))

    row_blk = lambda w: pl.BlockSpec((tm, w), lambda i: (i, 0))
    full = lambda a: pl.BlockSpec(a.shape, lambda i: (0,) * a.ndim)
    outs = pl.pallas_call(
        body, name="bwd_pre", grid=(s // tm,),
        out_shape=[jax.ShapeDtypeStruct((s, D_MODEL), F32), jax.ShapeDtypeStruct((8, D_MODEL), F32)]
        + _exchange_out_shapes(chip_halves, modes),
        in_specs=[row_blk(w) for _, _, w in DPROJ_GROUPS] + [_resident(w_all), row_blk(D_MODEL), row_blk(D_MODEL), full(mod)]
        + [ANY_SPEC] * nh,
        out_specs=[row_blk(D_MODEL), pl.BlockSpec((8, D_MODEL), lambda i: (0, 0))] + [ANY_SPEC] * nh,
        scratch_shapes=_exchange_sems(nh),
        compiler_params=_params(1))(*dproj, w_all, x, dx1, mod, *chip_halves)
    return outs[0], outs[1], outs[2:]


def _accumulate_tokens(a, b_ref, acc_ref, o_ref, last):
    @pl.when(pl.program_id(0) == 0)
    def _():
        acc_ref[...] = jnp.zeros_like(acc_ref)

    acc_ref[...] += _dot(a, b_ref[...])

    @pl.when(pl.program_id(0) == last)
    def _():
        o_ref[...] = acc_ref[...].astype(BF16)


def _token_matmul_multi(a_t, bs, tk, name):
    m, s = a_t.shape
    nb = len(bs)
    last = s // tk - 1

    def body(*refs):
        a_ref, b_refs, o_refs, acc_refs = refs[0], refs[1:1 + nb], refs[1 + nb:1 + 2 * nb], refs[1 + 2 * nb:]
        a = a_ref[...]
        for b_ref, o_ref, acc_ref in zip(b_refs, o_refs, acc_refs):
            _accumulate_tokens(a, b_ref, acc_ref, o_ref, last)

    return pl.pallas_call(
        body, name=name, grid=(s // tk,),
        out_shape=[jax.ShapeDtypeStruct((m, b.shape[1]), BF16) for b in bs],
        in_specs=[pl.BlockSpec((m, tk), lambda k: (0, k))] + [pl.BlockSpec((tk, b.shape[1]), lambda k: (k, 0)) for b in bs],
        out_specs=[pl.BlockSpec((m, b.shape[1]), lambda k: (0, 0)) for b in bs],
        scratch_shapes=[pltpu.VMEM((m, b.shape[1]), F32) for b in bs],
        compiler_params=_params(1))(a_t, *bs)


def _adam_math(g, w, m, v):
    m2 = ADAM_B1 * m + (1.0 - ADAM_B1) * g
    v2 = ADAM_B2 * v + (1.0 - ADAM_B2) * (g * g)
    delta = -ADAM_LR * ((m2 / ADAM_C1) / (jnp.sqrt(v2 / ADAM_C2) + ADAM_EPS) + ADAM_WD * w)
    return delta, m2, v2


def _adamw_parts(full, parts, others, w, m, v, tr, name):
    _, rws, cols = w.shape

    def body(oth_ref, f_ref, p_ref, w_ref, m_ref, v_ref, g_ref, d_ref, m2_ref, v2_ref):
        g = None
        for chip in range(N_CHIPS):
            part = (1.0 - oth_ref[chip]) * f_ref[chip] + oth_ref[chip] * p_ref[chip].astype(F32)
            g = part if g is None else g + part
        g_ref[0] = g
        d_ref[0], m2_ref[0], v2_ref[0] = _adam_math(g, w_ref[0], m_ref[0], v_ref[0])

    blk = pl.BlockSpec((1, tr, cols), lambda i: (0, i, 0))
    slots = pl.BlockSpec((N_CHIPS, tr, cols), lambda i: (0, i, 0))
    o = jax.ShapeDtypeStruct((1, rws, cols), F32)
    return pl.pallas_call(
        body, name=name, grid=(rws // tr,), out_shape=[o, o, o, o],
        in_specs=[pl.BlockSpec(memory_space=pltpu.SMEM), slots, slots, blk, blk, blk],
        out_specs=[blk, blk, blk, blk], compiler_params=_params(1))(others, full, parts, w, m, v)


def _adamw_devices(parts, w, m, v, tr, name):
    _, rws, cols = w.shape

    def body(p_ref, w_ref, m_ref, v_ref, g_ref, d_ref, m2_ref, v2_ref):
        g = p_ref[0].astype(F32)
        for dev in range(1, N_DEV):
            g = g + p_ref[dev].astype(F32)
        g_ref[0] = g
        d_ref[0], m2_ref[0], v2_ref[0] = _adam_math(g, w_ref[0], m_ref[0], v_ref[0])

    blk = pl.BlockSpec((1, tr, cols), lambda i: (0, i, 0))
    o = jax.ShapeDtypeStruct((1, rws, cols), F32)
    return pl.pallas_call(
        body, name=name, grid=(rws // tr,), out_shape=[o, o, o, o],
        in_specs=[pl.BlockSpec((N_DEV, tr, cols), lambda i: (0, i, 0)), blk, blk, blk],
        out_specs=[blk, blk, blk, blk], compiler_params=_params(1))(parts, w, m, v)


def _adamw_ada(c_lanes, d_rows, w, m, v, tr):
    _, rws, cols = w.shape

    def body(c_ref, d_ref, w_ref, m_ref, v_ref, g_ref, dl_ref, m2_ref, v2_ref):
        for k in range(cols // CHUNK):
            cs = slice(CHUNK * k, CHUNK * (k + 1))
            g = c_ref[0] * d_ref[0:1, cs]
            for b in range(1, N_DEV):
                g = g + c_ref[b] * d_ref[b:b + 1, cs]
            g_ref[0, :, cs] = g
            dl_ref[0, :, cs], m2_ref[0, :, cs], v2_ref[0, :, cs] = _adam_math(
                g, w_ref[0, :, cs], m_ref[0, :, cs], v_ref[0, :, cs])

    blk = pl.BlockSpec((1, tr, cols), lambda i: (0, i, 0))
    o = jax.ShapeDtypeStruct((1, rws, cols), F32)
    return pl.pallas_call(
        body, name="adamw_ada", grid=(rws // tr,), out_shape=[o, o, o, o],
        in_specs=[pl.BlockSpec((N_DEV, tr, CHUNK), lambda i: (0, i, 0)), pl.BlockSpec((N_DEV, cols), lambda i: (0, 0)),
                  blk, blk, blk],
        out_specs=[blk, blk, blk, blk], compiler_params=_params(1))(c_lanes, d_rows, w, m, v)


SMALL = (("b_ada", 0, 0, 3 * D_MODEL), ("g_norm", 1, 0, D_MODEL), ("b_f", 3, 0, N_HEADS), ("sinks", 4, 0, N_HEADS),
         ("g_final", 1, D_MODEL, D_MODEL))
LOSS_ROW = 2


def _pack_small(pre_acc, post_acc, dbf_acc, dsink):
    def body(pre_ref, post_ref, dbf_ref, dsink_ref, o_ref):
        o_ref[...] = jnp.zeros_like(o_ref)
        d = D_MODEL
        o_ref[0:1, 0:d], o_ref[0:1, d:2 * d], o_ref[0:1, 2 * d:3 * d] = pre_ref[0:1, :], pre_ref[1:2, :], post_ref[0:1, :]
        o_ref[1:2, 0:d], o_ref[1:2, d:2 * d] = pre_ref[2:3, :], post_ref[1:2, :]
        o_ref[LOSS_ROW:LOSS_ROW + 1, 0:d] = post_ref[2:3, :]
        o_ref[3:4, 0:CHUNK] = dbf_ref[0:1, :]
        lane = _lane(CHUNK)
        sinks = jnp.zeros((1, CHUNK), F32)
        for kv in range(KV_GROUPS):
            for g in range(GROUP):
                sinks = jnp.where(lane == GROUP * kv + g, dsink_ref[kv, g:g + 1, :], sinks)
        o_ref[4:5, 0:CHUNK] = sinks

    return pl.pallas_call(body, name="pack_small", out_shape=jax.ShapeDtypeStruct((8, 3 * D_MODEL), F32),
                          compiler_params=_params(0))(pre_acc, post_acc, dbf_acc, dsink)


def _adamw_small(packs, params):
    n = len(SMALL)

    def body(*refs):
        p_ref, wmv = refs[0], refs[1:1 + 3 * n]
        outs, loss_ref, tot_ref = refs[1 + 3 * n:1 + 7 * n], refs[1 + 7 * n], refs[2 + 7 * n]
        tot = p_ref[0]
        for dev in range(1, N_DEV):
            tot = tot + p_ref[dev]
        tot_ref[...] = tot
        for i, (_, row, lo, w) in enumerate(SMALL):
            g = tot_ref[row:row + 1, lo:lo + w]
            o = outs[4 * i:4 * i + 4]
            o[0][...] = g
            o[1][...], o[2][...], o[3][...] = _adam_math(g, wmv[3 * i][...], wmv[3 * i + 1][...], wmv[3 * i + 2][...])
        loss_ref[...] = jnp.broadcast_to(jnp.sum(tot_ref[LOSS_ROW:LOSS_ROW + 1, 0:D_MODEL], axis=1, keepdims=True),
                                         loss_ref.shape)

    flat = [a for wmv in params for a in wmv]
    out_shape = [jax.ShapeDtypeStruct(wmv[0].shape, F32) for wmv in params for _ in range(4)]
    res = pl.pallas_call(body, name="adamw_small", out_shape=out_shape + [jax.ShapeDtypeStruct((8, CHUNK), F32)],
                         scratch_shapes=[pltpu.VMEM((8, 3 * D_MODEL), F32)],
                         compiler_params=_params(0))(packs, *flat)
    return [tuple(res[4 * i:4 * i + 4]) for i in range(n)], res[4 * n]


def _tile(s, want):
    return min(s, want)


def _layout_pieces():
    orig = {"qa": 0, "ka": 512, "va": 640, "za": 768, "qb": 1280, "kb": 1792, "vb": 2304, "f": 2816, "zb": 2824,
            "ga": 3336, "gb": 4360}
    pieces = []
    for name, (off, w) in SEG.items():
        if name in ("ka", "va"):
            pieces += [(orig[name] + HEAD_DIM * kv, orig[name] + HEAD_DIM * (kv + 1), off + CHUNK * kv) for kv in range(KV_GROUPS)]
        else:
            pieces.append((orig[name], orig[name] + (N_HEADS if name == "f" else w), off))
    return pieces


def _assemble_w_all(win_g):
    cols, pos = [], 0
    for lo, hi, new in sorted(_layout_pieces(), key=lambda t: t[2]):
        if new > pos:
            cols.append(jnp.zeros((D_MODEL, new - pos), win_g.dtype))
        col = lo
        while col < hi:
            dev = col // IN_SHARD
            end = min(hi, (dev + 1) * IN_SHARD)
            cols.append(win_g[dev, :, col - dev * IN_SHARD:end - dev * IN_SHARD])
            col = end
        pos = new + hi - lo
    cols.append(jnp.zeros((D_MODEL, PROJ_W - pos), win_g.dtype))
    return jnp.concatenate(cols, axis=1)


def _grad_slot(dw_groups, dev):
    lo_d, hi_d = dev * IN_SHARD, (dev + 1) * IN_SHARD
    cols = []
    for lo, hi, new in sorted(_layout_pieces()):
        a, b = max(lo, lo_d), min(hi, hi_d)
        if a < b:
            arr, off = next((g, o) for g, (_, o, w) in zip(dw_groups, DPROJ_GROUPS) if o <= new < o + w)
            cols.append(arr[:, new - off + a - lo:new - off + b - lo])
    return jnp.concatenate(cols, axis=1)


def kernel(x, c, positions, w_ada, b_ada, g_norm, w_in, b_f, sinks, w_o_swa, w_o_fox, w_out, g_final, loss_target, m_w_ada, m_b_ada, m_g_norm, m_w_in, m_b_f, m_sinks, m_w_o_swa, m_w_o_fox, m_w_out, m_g_final, v_w_ada, v_b_ada, v_g_norm, v_w_in, v_b_f, v_sinks, v_w_o_swa, v_w_o_fox, v_w_out, v_g_final):
    s = x.shape[1]
    tm = _tile(s, 256)
    ta = _tile(s, 512)
    me = 4 * lax.axis_index("x") + 2 * lax.axis_index("y") + lax.axis_index("c")
    x2, tgt = x[0], loss_target[0]

    inv_freq = np.power(np.float32(ROPE_THETA), -np.arange(0, HEAD_DIM, 2, dtype=np.float32) / HEAD_DIM)
    inv_freq = jnp.asarray(np.tile(inv_freq, CHUNK // (HEAD_DIM // 2))[None, :], F32)
    (cos, sa, sb), (win_g, woa_g, wob_g, wout_g), c_all = _rope_tables_and_gather(
        positions.reshape(s, 1).astype(F32), inv_freq, tm,
        [w_in[0].astype(BF16), w_o_swa[0].T.astype(BF16), w_o_fox[0].T.astype(BF16), w_out[0].astype(BF16)],
        jnp.broadcast_to(c, (8, D_MODEL)))
    c_all = c_all[:, 0, :]
    b_shard = lax.dynamic_slice(b_ada, (0, me * ADA_SHARD), (1, ADA_SHARD))
    ada_part = _ada_fwd(c_all, w_ada[0], b_shard)
    (ada_g,) = _exchange([ada_part], ["gather"], "gather_ada")
    ada = lax.dynamic_index_in_dim(ada_g, me, axis=1, keepdims=False).reshape(3 * D_MODEL)
    shift, scale, gate = ada[:D_MODEL], ada[D_MODEL:2 * D_MODEL], ada[2 * D_MODEL:]
    woa_t = woa_g.reshape(D_MODEL, 512)
    wob_t = wob_g.reshape(D_MODEL, 512)
    wout = wout_g.reshape(D_MODEL, D_MODEL)
    w_all = _assemble_w_all(win_g)

    zrow = jnp.zeros((1, D_MODEL), F32)
    mod = jnp.concatenate([g_norm, scale[None], shift[None], zrow, zrow, zrow, zrow, zrow], axis=0)
    bf_row = jnp.pad(b_f, ((0, 0), (0, CHUNK - N_HEADS)))
    emat_np = np.zeros((3 * CHUNK, AUG_W), np.float32)
    ind_np = np.zeros((512, CHUNK), np.float32)
    for hd in range(N_HEADS):
        for part in range(3):
            emat_np[CHUNK * part + hd, CHUNK * hd + 64 + part] = 1.0
        ind_np[HEAD_DIM * hd:HEAD_DIM * (hd + 1), hd] = 1.0
    emat, ind = jnp.asarray(emat_np, BF16), jnp.asarray(ind_np, BF16)
    sink_rows = jnp.broadcast_to(jnp.pad(sinks.reshape(KV_GROUPS, GROUP), ((0, 0), (0, 8 - GROUP)))[:, :, None],
                                 (KV_GROUPS, 8, CHUNK))

    (h_t, qa, ka, va, za, qb, kb, vb, zb, ga, gb, xf, bounds, norms) = _fwd_proj(
        x2, mod, w_all, cos, sa, sb, bf_row, emat, ind, ta)
    first_key, last_query = _fox_block_ranges(bounds, norms, 1)
    att_a, qa_bwd = _swa_fwd(qa, ka, va, sink_rows, ta)
    att_b, qb_bwd = _fox_fwd(first_key, qb, kb, vb, ta)
    vec = jnp.concatenate([gate[None], g_final[None], zrow, zrow, zrow, zrow, zrow, zrow], axis=0)
    (dx1, ua_t, ub_t, merged_t, d_o, dya, dyb, doa, dob, dzg, post_acc) = _post(
        x2, tgt, att_a, att_b, za, zb, ga, gb, woa_t, wob_t, wout, vec, ind, emat, tm)

    da, dsink = _swa_bwd(ka, va, qa_bwd, doa, cos, sa, sb, sink_rows, ta)
    dkb, dvb, dck, dqb, dcq = _fox_bwd(last_query, kb, vb, qb_bwd, dob, ta)
    dfb, dbf_acc = _fgate_bwd(dcq, dck, xf, ta)
    dproj = [da, dqb, dkb, dvb, dzg, dfb]
    dw_zg, = _token_matmul_multi(h_t, [dzg], ta, "dw_in_zg")
    dw_a, dw_qb, dw_kb, dw_vb, dw_f = _token_matmul_multi(h_t, [da, dqb, dkb, dvb, dfb], ta, "dw_in_rest")
    dw_all = [dw_a, dw_qb, dw_kb, dw_vb, dw_zg, dw_f]

    core, chip = lax.axis_index("c"), 2 * lax.axis_index("x") + lax.axis_index("y")
    win_slots = jnp.stack([jnp.stack([_grad_slot(dw_all, 2 * ch + co) for ch in range(N_CHIPS)]) for co in range(2)])
    (dwoa, dwob, dwout), (win_theirs,) = _token_matmuls_and_swap(
        [(ua_t, dya), (ub_t, dyb), (merged_t, d_o)], ta, [win_slots])
    win_full, win_half = _chip_partial(jnp.reshape(core, (1,)).astype(jnp.int32), win_slots, win_theirs, 256,
                                       "chip_partial_w_in")
    col_slots = lambda g: g.reshape(512, N_DEV, 128).transpose(1, 0, 2)
    grad_x, pre_acc, (p_win, p_woa, p_wob, p_wout) = _bwd_pre(
        dproj, w_all, x2, dx1, mod, ta,
        [win_half, col_slots(dwoa), col_slots(dwob), dwout.reshape(N_DEV, 128, D_MODEL)],
        ["chips", "devices", "devices", "devices"])
    (packs,) = _exchange([_pack_small(pre_acc, post_acc, dbf_acc, dsink)], ["gather"], "gather_small_grads")
    others = jnp.where(jnp.arange(N_CHIPS) == chip, 0.0, 1.0).astype(F32)

    g_win, d_win, m_win, v_win = _adamw_parts(win_full, p_win, others, w_in, m_w_in, v_w_in, 128, "adamw_w_in")
    g_woa, d_woa, m_woa, v_woa = _adamw_devices(p_woa, w_o_swa, m_w_o_swa, v_w_o_swa, 128, "adamw_w_o_swa")
    g_wob, d_wob, m_wob, v_wob = _adamw_devices(p_wob, w_o_fox, m_w_o_fox, v_w_o_fox, 128, "adamw_w_o_fox")
    g_wout, d_wout, m_wout, v_wout = _adamw_devices(p_wout, w_out, m_w_out, v_w_out, 128, "adamw_w_out")
    d_ada_rows = lax.dynamic_slice(packs[:, 0, :], (0, me * ADA_SHARD), (N_DEV, ADA_SHARD))
    c_lanes = jnp.broadcast_to(c_all[:, :, None], (N_DEV, D_MODEL, CHUNK))
    g_wada, d_wada, m_wada, v_wada = _adamw_ada(c_lanes, d_ada_rows, w_ada, m_w_ada, v_w_ada, 256)

    row = lambda a: a.reshape(1, D_MODEL)
    small, loss_rows = _adamw_small(packs, [(b_ada, m_b_ada, v_b_ada), (g_norm, m_g_norm, v_g_norm), (b_f, m_b_f, v_b_f),
                                            (sinks, m_sinks, v_sinks), (row(g_final), row(m_g_final), row(v_g_final))])
    bada_o, gn_o, bf_o, sk_o, gf_o = small

    outs = []
    for k, big in enumerate(((g_wada, g_win, g_woa, g_wob, g_wout), (d_wada, d_win, d_woa, d_wob, d_wout),
                             (m_wada, m_win, m_woa, m_wob, m_wout), (v_wada, v_win, v_woa, v_wob, v_wout))):
        wada_o, win_o, woa_o, wob_o, wout_o = big
        outs += [wada_o, bada_o[k], gn_o[k], win_o, bf_o[k], sk_o[k], woa_o, wob_o, wout_o, gf_o[k].reshape(D_MODEL)]
    return (loss_rows[0, 0], grad_x[None], *outs)
```

```python
import numpy as np
import jax
import jax.numpy as jnp
from jax import lax
from jax.experimental import pallas as pl
from jax.experimental.pallas import tpu as pltpu

F32 = jnp.float32
BF16 = jnp.bfloat16

D_MODEL = 1024
HEAD_DIM = 64
N_HEADS = 8
KV_GROUPS = 2
GROUP = N_HEADS // KV_GROUPS
WINDOW = 128
CHUNK = 128
AUG_W = N_HEADS * CHUNK
N_DEV = 8
IN_SHARD = 673
ADA_SHARD = 384
NORM_EPS = 1e-6
ROPE_THETA = 10000.0
Q_SCALE = HEAD_DIM ** -0.5
NEG = -1e30
UNDERFLOW = 105.0

ADAM_LR = 0.001
ADAM_B1 = 0.9
ADAM_B2 = 0.999
ADAM_EPS = 1e-08
ADAM_WD = 0.01
ADAM_STEP = 10
ADAM_C1 = 1.0 - ADAM_B1 ** ADAM_STEP
ADAM_C2 = 1.0 - ADAM_B2 ** ADAM_STEP

SEG = {}
_off = 0
for _name, _w in (("qa", 512), ("ka", 256), ("va", 256), ("qb", 512), ("kb", 512), ("vb", 512),
                  ("za", 512), ("zb", 512), ("ga", 1024), ("gb", 1024), ("f", 128)):
    SEG[_name] = (_off, _w)
    _off += _w
PROJ_W = _off
DPROJ_GROUPS = (("a", SEG["qa"][0], 1024), ("qb", SEG["qb"][0], 512), ("kb", SEG["kb"][0], 512),
                ("vb", SEG["vb"][0], 512), ("zg", SEG["za"][0], 3072), ("f", SEG["f"][0], 128))

VMEM_LIMIT = 56 * 1024 * 1024


def _params(n_axes):
    return pltpu.CompilerParams(dimension_semantics=("arbitrary",) * n_axes, vmem_limit_bytes=VMEM_LIMIT)


def _resident(a):
    return pl.BlockSpec(a.shape, lambda i: (0, 0), pipeline_mode=pl.Buffered(1))


def _dot(a, b):
    return jnp.dot(a, b, preferred_element_type=F32)


def _dot_nt(a, b):
    return lax.dot_general(a, b, (((1,), (1,)), ((), ())), preferred_element_type=F32)


def _dot_tn(a, b):
    return lax.dot_general(a, b, (((0,), (0,)), ((), ())), preferred_element_type=F32)


def _lane(n):
    return lax.broadcasted_iota(jnp.int32, (1, n), 1)


def _split3(x):
    hi = x.astype(BF16).astype(F32)
    r = x - hi
    mid = r.astype(BF16).astype(F32)
    lo = (r - mid).astype(BF16).astype(F32)
    return hi, mid, lo


def _dot_split(x, b):
    hi, mid, lo = _split3(x)
    return _dot(hi.astype(BF16), b) + _dot(mid.astype(BF16), b) + _dot(lo.astype(BF16), b)


def _spread3(x, e3_ref):
    return _dot(jnp.concatenate(_split3(x), axis=1).astype(BF16), e3_ref[...])


def _place3(lane, base, x, other):
    hi, mid, lo = _split3(x)
    return jnp.where(lane == base, hi, jnp.where(lane == base + 1, mid, jnp.where(lane == base + 2, lo, other)))


def _lane_sum(x, lo, hi):
    lane = _lane(x.shape[1])
    return jnp.sum(jnp.where((lane >= lo) & (lane < hi), x, 0.0), axis=1, keepdims=True)


def _ones_lanes(lo, hi):
    lane = _lane(CHUNK)
    return jnp.where((lane >= lo) & (lane < hi), 1.0, 0.0).astype(F32)


def _rope(c, cos, sa, sb):
    return c * cos + pltpu.roll(c, CHUNK - 32, 1) * sa + pltpu.roll(c, 32, 1) * sb


def _rope_inv(d, cos, sa, sb):
    return d * cos - (pltpu.roll(d, CHUNK - 32, 1) * sa + pltpu.roll(d, 32, 1) * sb)


def _pair(c0, c1):
    return jnp.where(_lane(CHUNK) < HEAD_DIM, c0, pltpu.roll(c1, HEAD_DIM, 1))


def _sigmoid(x):
    return 0.5 * jnp.tanh(0.5 * x) + 0.5


N_CHIPS = 4
ANY_SPEC = pl.BlockSpec(memory_space=pl.ANY)


def _exchange(arrays, modes, name):
    n = len(arrays)

    def body(*refs):
        copies = _exchange_copies(refs[:n], refs[n:2 * n], modes, *refs[2 * n:])
        _exchange_start(copies)
        _exchange_wait(copies)

    return pl.pallas_call(
        body, name=name, out_shape=_exchange_out_shapes(arrays, modes),
        in_specs=[ANY_SPEC] * n, out_specs=[ANY_SPEC] * n, scratch_shapes=_exchange_sems(n),
    )(*arrays)


def _exchange_out_shapes(arrays, modes):
    return [jax.ShapeDtypeStruct((N_DEV,) + a.shape if md == "gather" else a.shape, a.dtype)
            for a, md in zip(arrays, modes)]


def _exchange_sems(n):
    return [pltpu.SemaphoreType.DMA((n, N_DEV - 1)), pltpu.SemaphoreType.DMA((n, N_DEV - 1)),
            pltpu.SemaphoreType.DMA((n,))]


def _exchange_copies(srcs, dsts, modes, send_sems, recv_sems, loc_sems):
    n = len(srcs)
    x, y, c = lax.axis_index("x"), lax.axis_index("y"), lax.axis_index("c")

    def slot(i, px, py, pc):
        return 2 * px + py if modes[i] == "chips" else 4 * px + 2 * py + pc

    def src_of(i, px, py, pc):
        return srcs[i] if modes[i] == "gather" else srcs[i].at[slot(i, px, py, pc)]

    local = [pltpu.make_async_copy(src_of(i, x, y, c), dsts[i].at[slot(i, x, y, c)], loc_sems.at[i])
             for i in range(n)]
    sends, recvs = [], []
    for r in (1, 2, 4, 3, 5, 6, 7):
        px = 1 - x if r & 4 else x
        py = 1 - y if r & 2 else y
        pc = 1 - c if r & 1 else c
        for i in range(n):
            if modes[i] == "chips" and r & 1:
                continue
            sems = dict(send_sem=send_sems.at[i, r - 1], recv_sem=recv_sems.at[i, r - 1],
                        device_id=(px, py, pc), device_id_type=pl.DeviceIdType.MESH)
            sends.append(pltpu.make_async_remote_copy(
                src_ref=src_of(i, px, py, pc), dst_ref=dsts[i].at[slot(i, x, y, c)], **sems))
            recvs.append(pltpu.make_async_remote_copy(
                src_ref=src_of(i, px, py, pc), dst_ref=dsts[i].at[slot(i, px, py, pc)], **sems))
    return local, sends, recvs


def _exchange_start(copies):
    local, sends, _ = copies
    for cp in local + sends:
        cp.start()


def _exchange_wait(copies):
    local, sends, recvs = copies
    for cp in recvs:
        cp.wait_recv()
    for cp in sends:
        cp.wait_send()
    for cp in local:
        cp.wait()


def _gather2_parts(srcs, dsts, send_sems, recv_sems, loc_sems):
    n = len(srcs)
    x, y, c = lax.axis_index("x"), lax.axis_index("y"), lax.axis_index("c")
    me, sibling = (x, y, c), (x, y, 1 - c)
    chips = [(1 - x, y), (x, 1 - y), (1 - x, 1 - y)]

    def rows(i, dev):
        return dsts[i].at[4 * dev[0] + 2 * dev[1] + dev[2]]

    def copy(i, k, block, to, src=None):
        return pltpu.make_async_remote_copy(
            src_ref=rows(i, block) if src is None else src, dst_ref=rows(i, block),
            send_sem=send_sems.at[i, k], recv_sem=recv_sems.at[i, k],
            device_id=to, device_id_type=pl.DeviceIdType.MESH)

    local = [pltpu.make_async_copy(srcs[i], rows(i, me), loc_sems.at[i]) for i in range(n)]
    first = []
    for i in range(n):
        first.append(copy(i, 0, me, sibling, src=srcs[i]))
        first += [copy(i, 1 + j, me, (*chip, c), src=srcs[i]) for j, chip in enumerate(chips)]
    return n, c, me, sibling, chips, copy, local, first


def _gather2_start(srcs, dsts, send_sems, recv_sems, loc_sems):
    *_, local, first = _gather2_parts(srcs, dsts, send_sems, recv_sems, loc_sems)
    for cp in local + first:
        cp.start()


def _gather2_finish(srcs, dsts, send_sems, recv_sems, loc_sems):
    n, c, me, sibling, chips, copy, local, first = _gather2_parts(srcs, dsts, send_sems, recv_sems, loc_sems)
    passed = []
    for j, chip in enumerate(chips):
        for i in range(n):
            copy(i, 1 + j, (*chip, c), me).wait_recv()
            cp = copy(i, 4 + j, (*chip, c), sibling)
            cp.start()
            passed.append(cp)
    for i in range(n):
        copy(i, 0, sibling, me).wait_recv()
        for j, chip in enumerate(chips):
            copy(i, 4 + j, (*chip, 1 - c), me).wait_recv()
    for cp in first + passed:
        cp.wait_send()
    for cp in local:
        cp.wait()


def _sibling_copies(srcs, dsts, send_sems, recv_sems):
    x, y, c = lax.axis_index("x"), lax.axis_index("y"), lax.axis_index("c")
    return [pltpu.make_async_remote_copy(
        src_ref=srcs[i].at[1 - c, k], dst_ref=dsts[i].at[k], send_sem=send_sems.at[i, k], recv_sem=recv_sems.at[i, k],
        device_id=(x, y, 1 - c), device_id_type=pl.DeviceIdType.MESH)
        for i in range(len(srcs)) for k in range(N_CHIPS)]


def _token_matmuls_and_swap(pairs, tk, swap):
    npair, nsw = len(pairs), len(swap)
    s = pairs[0][0].shape[1]
    last = s // tk - 1

    def body(*refs):
        a_refs, b_refs = refs[0:2 * npair:2], refs[1:2 * npair:2]
        src = refs[2 * npair:2 * npair + nsw]
        o_refs = refs[2 * npair + nsw:3 * npair + nsw]
        dst = refs[3 * npair + nsw:3 * npair + 2 * nsw]
        sems = refs[3 * npair + 2 * nsw:3 * npair + 2 * nsw + 2]
        acc_refs = refs[3 * npair + 2 * nsw + 2:]

        @pl.when(pl.program_id(0) == 0)
        def _():
            for cp in _sibling_copies(src, dst, *sems):
                cp.start()

        for a_ref, b_ref, o_ref, acc_ref in zip(a_refs, b_refs, o_refs, acc_refs):
            _accumulate_tokens(a_ref[...], b_ref, acc_ref, o_ref, last)

        @pl.when(pl.program_id(0) == last)
        def _():
            for cp in _sibling_copies(src, dst, *sems):
                cp.wait()

    in_specs, args = [], []
    for a_t, b in pairs:
        in_specs += [pl.BlockSpec((a_t.shape[0], tk), lambda k: (0, k)), pl.BlockSpec((tk, b.shape[1]), lambda k: (k, 0))]
        args += [a_t, b]
    outs = pl.pallas_call(
        body, name="dw_out_projs_swap", grid=(s // tk,),
        out_shape=[jax.ShapeDtypeStruct((a_t.shape[0], b.shape[1]), BF16) for a_t, b in pairs]
        + [jax.ShapeDtypeStruct(a.shape[1:], a.dtype) for a in swap],
        in_specs=in_specs + [ANY_SPEC] * nsw,
        out_specs=[pl.BlockSpec((a_t.shape[0], b.shape[1]), lambda k: (0, 0)) for a_t, b in pairs] + [ANY_SPEC] * nsw,
        scratch_shapes=[pltpu.SemaphoreType.DMA((nsw, N_CHIPS)), pltpu.SemaphoreType.DMA((nsw, N_CHIPS))]
        + [pltpu.VMEM((a_t.shape[0], b.shape[1]), F32) for a_t, b in pairs],
        compiler_params=_params(1))(*args, *swap)
    return outs[:npair], outs[npair:]


def _chip_partial(core, slots, theirs, tr, name):
    _, k, rws, cols = slots.shape

    def body(core_ref, a_ref, b_ref, f_ref, h_ref):
        sm = a_ref[0].astype(F32) + b_ref[...].astype(F32)
        f_ref[...] = sm
        h_ref[...] = sm.astype(BF16)

    blk = pl.BlockSpec((1, tr, cols), lambda j, i, core_ref: (j, i, 0))
    mine = pl.BlockSpec((1, 1, tr, cols), lambda j, i, core_ref: (core_ref[0], j, i, 0))
    return pl.pallas_call(
        body, name=name,
        grid_spec=pltpu.PrefetchScalarGridSpec(num_scalar_prefetch=1, grid=(k, rws // tr),
                                               in_specs=[mine, blk], out_specs=[blk, blk]),
        out_shape=[jax.ShapeDtypeStruct(theirs.shape, F32), jax.ShapeDtypeStruct(theirs.shape, BF16)],
        compiler_params=_params(2))(core, slots, theirs)


def _ada_fwd(c_all, w_shard, b_shard):
    def body(c_ref, w_ref, b_ref, o_ref):
        ch, cm, cl = [t.astype(BF16) for t in _split3(c_ref[...])]
        wh, wm, wl = [t.astype(BF16) for t in _split3(w_ref[...])]
        acc = _dot(ch, wh) + _dot(ch, wm) + _dot(cm, wh) + _dot(ch, wl) + _dot(cl, wh) + _dot(cm, wm)
        o_ref[...] = acc + b_ref[...]

    return pl.pallas_call(body, name="ada_fwd", out_shape=jax.ShapeDtypeStruct((N_DEV, ADA_SHARD), F32),
                          compiler_params=_params(0))(c_all, w_shard, b_shard)


def _rope_tables_and_gather(pos_col, inv_freq, tm, weights, c_rows):
    s = pos_col.shape[0]
    nw = len(weights)
    last = s // tm - 1

    def body(*refs):
        p_ref, f_ref = refs[:2]
        w_src, c_src = refs[2:2 + nw], refs[2 + nw:3 + nw]
        cos_ref, sa_ref, sb_ref = refs[3 + nw:6 + nw]
        w_dst, c_dst = refs[6 + nw:6 + 2 * nw], refs[6 + 2 * nw:7 + 2 * nw]
        w_sems, c_sems = refs[7 + 2 * nw:10 + 2 * nw], refs[10 + 2 * nw:]

        @pl.when(pl.program_id(0) == 0)
        def _():
            _gather2_start(w_src, w_dst, *w_sems)
            _exchange_start(_exchange_copies(c_src, c_dst, ["gather"], *c_sems))

        ang = p_ref[...] * f_ref[...]
        sin = jnp.sin(ang)
        first_half = (_lane(CHUNK) & (HEAD_DIM - 1)) < HEAD_DIM // 2
        cos_ref[...] = jnp.cos(ang)
        sa_ref[...] = jnp.where(first_half, -sin, 0.0)
        sb_ref[...] = jnp.where(first_half, 0.0, sin)

        @pl.when(pl.program_id(0) == last)
        def _():
            _exchange_wait(_exchange_copies(c_src, c_dst, ["gather"], *c_sems))
            _gather2_finish(w_src, w_dst, *w_sems)

    tab = jax.ShapeDtypeStruct((s, CHUNK), F32)
    blk = pl.BlockSpec((tm, CHUNK), lambda i: (i, 0))
    gathered = [jax.ShapeDtypeStruct((N_DEV,) + a.shape, a.dtype) for a in list(weights) + [c_rows]]
    outs = pl.pallas_call(
        body, name="rope_tables_gather", grid=(s // tm,), out_shape=[tab, tab, tab] + gathered,
        in_specs=[pl.BlockSpec((tm, 1), lambda i: (i, 0)), pl.BlockSpec((1, CHUNK), lambda i: (0, 0))]
        + [ANY_SPEC] * (nw + 1),
        out_specs=[blk, blk, blk] + [ANY_SPEC] * (nw + 1),
        scratch_shapes=_exchange_sems(nw) + _exchange_sems(1),
        compiler_params=_params(1))(pos_col, inv_freq, *weights, c_rows)
    return outs[:3], outs[3:3 + nw], outs[3 + nw]


def _fwd_proj(x, mod, w_all, cos, sa, sb, bf_row, emat, ind, tm, late_shards):
    s = x.shape[0]
    nl = len(late_shards)
    last = s // tm - 1

    def body(x_ref, mod_ref, w_ref, cos_ref, sa_ref, sb_ref, bf_ref, e_ref, ind_ref, *rest):
        late_src = rest[:nl]
        (h_ref, qa_ref, ka_ref, va_ref, za_ref, qb_ref, kb_ref, vb_ref, zb_ref, ga_ref, gb_ref, f_ref,
         bnd_ref, nrm_ref) = rest[nl:nl + 14]
        late_dst = rest[nl + 14:2 * nl + 14]
        carry_ref, late_sems = rest[2 * nl + 14], rest[2 * nl + 15:]

        @pl.when(pl.program_id(0) == 0)
        def _():
            carry_ref[...] = jnp.zeros_like(carry_ref)
            _gather2_start(late_src, late_dst, *late_sems)

        xv = x_ref[...]
        rstd = lax.rsqrt(jnp.mean(xv * xv, axis=-1, keepdims=True) + NORM_EPS)
        h = (xv * rstd * mod_ref[0:1, :]) * (1.0 + mod_ref[1:2, :]) + mod_ref[2:3, :]
        hb = h.astype(BF16)
        h_ref[...] = hb.T

        def seg(name):
            off, w = SEG[name]
            return _dot(hb, w_ref[:, off:off + w])

        lane = _lane(CHUNK)
        low = lane < HEAD_DIM
        q_ones = _ones_lanes(64, 67)
        k_ones = _ones_lanes(67, 70)
        cos_t, sa_t, sb_t = cos_ref[...], sa_ref[...], sb_ref[...]

        def write_heads(ref, nat, extra_of, rope, scale):
            for p in range(N_HEADS // 2):
                c = nat[:, CHUNK * p:CHUNK * (p + 1)]
                if rope:
                    c = _rope(c, cos_t, sa_t, sb_t)
                if scale != 1.0:
                    c = c * scale
                for hh in range(2):
                    hd = 2 * p + hh
                    src = c if hh == 0 else pltpu.roll(c, HEAD_DIM, 1)
                    ref[:, CHUNK * hd:CHUNK * (hd + 1)] = jnp.where(low, src, extra_of(hd)).astype(BF16)

        write_heads(qa_ref, seg("qa"), lambda hd: q_ones, True, Q_SCALE)
        ka = seg("ka")
        va = seg("va")
        for kv in range(KV_GROUPS):
            sl = slice(CHUNK * kv, CHUNK * (kv + 1))
            ka_ref[:, sl] = jnp.where(low, _rope(ka[:, sl], cos_t, sa_t, sb_t), k_ones).astype(BF16)
            va_ref[:, sl] = jnp.where(low, va[:, sl], q_ones).astype(BF16)
        za_ref[...] = seg("za")
        zb_ref[...] = seg("zb")
        ga_ref[...] = seg("ga")
        gb_ref[...] = seg("gb")

        xf = seg("f") + bf_ref[...]
        f_ref[...] = xf
        logf = jnp.minimum(xf, 0.0) - jnp.log1p(jnp.exp(-jnp.abs(xf)))
        row = lax.broadcasted_iota(jnp.int32, (tm, tm), 0)
        col = lax.broadcasted_iota(jnp.int32, (tm, tm), 1)
        tri = jnp.where(col <= row, 1.0, 0.0).astype(BF16)
        hi, mid, lo = _split3(logf)
        cum = _dot(tri, hi.astype(BF16)) + _dot(tri, mid.astype(BF16)) + _dot(tri, lo.astype(BF16))
        cum = cum + carry_ref[0:1, :]
        carry_ref[0:1, :] += jnp.sum(logf, axis=0, keepdims=True)
        lane_all = _lane(AUG_W) & (CHUNK - 1)
        bias = _spread3(-cum, e_ref) + jnp.where((lane_all >= 67) & (lane_all < 70), 1.0, 0.0)

        qb, kb = seg("qb"), seg("kb")
        write_heads(qb_ref, qb, lambda hd: q_ones, False, Q_SCALE)
        write_heads(kb_ref, kb, lambda hd: bias[:, CHUNK * hd:CHUNK * (hd + 1)], False, 1.0)
        write_heads(vb_ref, seg("vb"), lambda hd: q_ones, False, 1.0)

        rid = lax.broadcasted_iota(jnp.int32, (tm, CHUNK), 0)
        bnd_ref[...] = jnp.zeros_like(bnd_ref)
        bnd_ref[0, 0:1, :] = jnp.sum(jnp.where(rid == 0, cum, 0.0), axis=0, keepdims=True)
        bnd_ref[0, 1:2, :] = jnp.sum(jnp.where(rid == tm - 1, cum, 0.0), axis=0, keepdims=True)

        @pl.when(pl.program_id(0) == 0)
        def _():
            nrm_ref[...] = jnp.zeros_like(nrm_ref)

        for r_, nat in ((0, qb * Q_SCALE), (1, kb)):
            sq = _dot((nat * nat).astype(BF16), ind_ref[...])
            nrm_ref[r_:r_ + 1, :] = jnp.maximum(nrm_ref[r_:r_ + 1, :], jnp.max(sq, axis=0, keepdims=True))

        @pl.when(pl.program_id(0) == last)
        def _():
            _gather2_finish(late_src, late_dst, *late_sems)

    row_blk = lambda w: pl.BlockSpec((tm, w), lambda i: (i, 0))
    full = lambda a: pl.BlockSpec(a.shape, lambda i: (0,) * a.ndim)
    sds = lambda w, dt: jax.ShapeDtypeStruct((s, w), dt)
    out_shape = [sds(AUG_W, BF16), sds(KV_GROUPS * CHUNK, BF16), sds(KV_GROUPS * CHUNK, BF16),
                 sds(512, F32), sds(AUG_W, BF16), sds(AUG_W, BF16), sds(AUG_W, BF16), sds(512, F32),
                 sds(D_MODEL, F32), sds(D_MODEL, F32), sds(CHUNK, F32)]
    small = [jax.ShapeDtypeStruct((s // tm, 8, CHUNK), F32), jax.ShapeDtypeStruct((8, CHUNK), F32)]
    gathered = [jax.ShapeDtypeStruct((N_DEV,) + a.shape, a.dtype) for a in late_shards]
    outs = pl.pallas_call(
        body, name="fwd_proj", grid=(s // tm,),
        out_shape=[jax.ShapeDtypeStruct((D_MODEL, s), BF16)] + out_shape + small + gathered,
        in_specs=[row_blk(D_MODEL), full(mod), _resident(w_all), row_blk(CHUNK), row_blk(CHUNK), row_blk(CHUNK),
                  full(bf_row), full(emat), full(ind)] + [ANY_SPEC] * nl,
        out_specs=[pl.BlockSpec((D_MODEL, tm), lambda i: (0, i))] + [row_blk(o.shape[1]) for o in out_shape] + [
            pl.BlockSpec((1, 8, CHUNK), lambda i: (i, 0, 0)), pl.BlockSpec((8, CHUNK), lambda i: (0, 0))]
        + [ANY_SPEC] * nl,
        scratch_shapes=[pltpu.VMEM((8, CHUNK), F32)] + _exchange_sems(nl),
        compiler_params=_params(1))(x, mod, w_all, cos, sa, sb, bf_row, emat, ind, *late_shards)
    return outs[:14], outs[14:]


def _swa_fwd(q_aug, k_aug, v_aug, sink_rows, tq):
    s = q_aug.shape[0]
    r = tq // WINDOW
    gw = GROUP * CHUNK

    def body(q_ref, kc_ref, kp_ref, vc_ref, vp_ref, sink_ref, o_ref, qb_ref):
        i = pl.program_id(1)
        st = GROUP * WINDOW
        qloc = lax.broadcasted_iota(jnp.int32, (st, 2 * WINDOW), 0) & (WINDOW - 1)
        col = lax.broadcasted_iota(jnp.int32, (st, 2 * WINDOW), 1)
        band = (col > qloc) & (col <= qloc + WINDOW)
        head = jnp.right_shift(lax.broadcasted_iota(jnp.int32, (st, 1), 0), 7)
        sink = jnp.zeros((st, 1), F32)
        for g in range(GROUP):
            sink = jnp.where(head == g, jnp.max(sink_ref[0, g:g + 1, :], axis=1, keepdims=True), sink)
        lane = _lane(CHUNK)
        for sub in range(r):
            rows = slice(WINDOW * sub, WINDOW * (sub + 1))
            q = jnp.concatenate([q_ref[rows, CHUNK * g:CHUNK * (g + 1)] for g in range(GROUP)], axis=0)
            if sub == 0:
                k = jnp.concatenate([kp_ref[...], kc_ref[rows, :]], axis=0)
                v = jnp.concatenate([vp_ref[...], vc_ref[rows, :]], axis=0)
                valid = band & ((col >= WINDOW) | (i > 0))
            else:
                both = slice(WINDOW * (sub - 1), WINDOW * (sub + 1))
                k, v, valid = kc_ref[both, :], vc_ref[both, :], band
            sc = jnp.where(valid, _dot_nt(q, k), NEG)
            m = jnp.maximum(jnp.max(sc, axis=1, keepdims=True), sink)
            acc = _dot(jnp.exp((sc - m).astype(BF16)), v)
            denom = _lane_sum(acc, 64, 65) + jnp.exp(sink - m)
            out = acc * (1.0 / denom)
            aug = _place3(lane, 67, -(m + jnp.log(denom)), q.astype(F32)).astype(BF16)
            hrows = lambda a, g: a[WINDOW * g:WINDOW * (g + 1), :]
            for g in range(GROUP):
                qb_ref[rows, CHUNK * g:CHUNK * (g + 1)] = hrows(aug, g)
            for pp in range(GROUP // 2):
                o_ref[rows, CHUNK * pp:CHUNK * (pp + 1)] = _pair(hrows(out, 2 * pp), hrows(out, 2 * pp + 1))

    return pl.pallas_call(
        body, name="swa_fwd", grid=(KV_GROUPS, s // tq),
        out_shape=[jax.ShapeDtypeStruct((s, 512), F32), jax.ShapeDtypeStruct((s, AUG_W), BF16)],
        in_specs=[pl.BlockSpec((tq, gw), lambda kv, i: (i, kv)),
                  pl.BlockSpec((tq, CHUNK), lambda kv, i: (i, kv)),
                  pl.BlockSpec((WINDOW, CHUNK), lambda kv, i: (jnp.maximum(i * r - 1, 0), kv)),
                  pl.BlockSpec((tq, CHUNK), lambda kv, i: (i, kv)),
                  pl.BlockSpec((WINDOW, CHUNK), lambda kv, i: (jnp.maximum(i * r - 1, 0), kv)),
                  pl.BlockSpec((1, 8, CHUNK), lambda kv, i: (kv, 0, 0))],
        out_specs=[pl.BlockSpec((tq, GROUP * HEAD_DIM), lambda kv, i: (i, kv)),
                   pl.BlockSpec((tq, gw), lambda kv, i: (i, kv))],
        compiler_params=_params(2))(q_aug, k_aug, k_aug, v_aug, v_aug, sink_rows)


def _swa_bwd(k_aug, v_aug, q_bwd, do_aug, cos, sa, sb, sink_rows, tk):
    s = k_aug.shape[0]
    r = tk // WINDOW
    nt = s // tk
    nb = s // WINDOW
    gw = GROUP * CHUNK

    def body(k_ref, v_ref, q_ref, qn_ref, do_ref, don_ref, cos_ref, sa_ref, sb_ref, sink_ref,
             da_ref, dsink_ref, carry_ref, acc_ref):
        j = pl.program_id(0)

        @pl.when(j == 0)
        def _():
            carry_ref[...] = jnp.zeros_like(carry_ref)
            dsink_ref[...] = jnp.zeros_like(dsink_ref)

        acc_ref[...] = jnp.zeros_like(acc_ref)
        acc_ref[0:WINDOW, :] = carry_ref[...]
        dk_col, dv_col = SEG["ka"][0] - SEG["qa"][0], SEG["va"][0] - SEG["qa"][0]
        st = GROUP * WINDOW
        key = lax.broadcasted_iota(jnp.int32, (WINDOW, 2 * st), 0)
        col = lax.broadcasted_iota(jnp.int32, (WINDOW, 2 * st), 1)
        qloc = col & (WINDOW - 1)
        band = ((col < st) & (key <= qloc)) | ((col >= st) & (key > qloc))
        lane = _lane(CHUNK)
        low = lane < HEAD_DIM
        cos_t, sa_t, sb_t = cos_ref[...], sa_ref[...], sb_ref[...]
        for kv, sub in [(kv, sub) for kv in range(KV_GROUPS) for sub in range(r)]:
            heads = [slice(CHUNK * (GROUP * kv + g), CHUNK * (GROUP * kv + g + 1)) for g in range(GROUP)]
            kvs = slice(CHUNK * kv, CHUNK * (kv + 1))
            rows = slice(WINDOW * sub, WINDOW * (sub + 1))
            nxt = slice(WINDOW * (sub + 1), WINDOW * (sub + 2))
            k, v = k_ref[rows, kvs], v_ref[rows, kvs]
            q_cur, do_cur = [q_ref[rows, cs] for cs in heads], [do_ref[rows, cs] for cs in heads]
            if sub < r - 1:
                q_nxt, do_nxt, valid = [q_ref[nxt, cs] for cs in heads], [do_ref[nxt, cs] for cs in heads], band
            else:
                q_nxt, do_nxt = [qn_ref[:, cs] for cs in heads], [don_ref[:, cs] for cs in heads]
                valid = band & ((col < st) | (j < nt - 1))
            q = jnp.concatenate(q_cur + q_nxt, axis=0)
            do = jnp.concatenate(do_cur + do_nxt, axis=0)
            pt = jnp.exp(jnp.where(valid, _dot_nt(k, q), NEG))
            ds = (pt * _dot_nt(v, do)).astype(BF16)
            dv = _dot(pt.astype(BF16), do)
            dk = _dot(ds, q)
            dq = _dot_tn(ds, k)
            for g, cs in enumerate(heads):
                acc_ref[rows, cs] += dq[WINDOW * g:WINDOW * (g + 1), :]
                dqn = dq[st + WINDOW * g:st + WINDOW * (g + 1), :]
                if sub < r - 1:
                    acc_ref[nxt, cs] += dqn
                else:
                    carry_ref[:, cs] = dqn
                sink = jnp.max(sink_ref[kv, g:g + 1, :], axis=1, keepdims=True)
                p_sink = jnp.exp(sink + _lane_sum(q_cur[g].astype(F32), 67, 70))
                term = jnp.sum(p_sink * _lane_sum(do_cur[g].astype(F32), 64, 67), axis=0, keepdims=True)
                dsink_ref[kv, g:g + 1, :] += jnp.broadcast_to(term, (1, CHUNK))
            da_ref[rows, dk_col + CHUNK * kv:dk_col + CHUNK * (kv + 1)] = _rope_inv(
                jnp.where(low, dk, 0.0), cos_t[rows, :], sa_t[rows, :], sb_t[rows, :]).astype(BF16)
            da_ref[rows, dv_col + CHUNK * kv:dv_col + CHUNK * (kv + 1)] = jnp.where(low, dv, 0.0).astype(BF16)
        for pp in range(N_HEADS // 2):
            d = _pair(acc_ref[:, CHUNK * 2 * pp:CHUNK * (2 * pp + 1)], acc_ref[:, CHUNK * (2 * pp + 1):CHUNK * (2 * pp + 2)])
            da_ref[:, CHUNK * pp:CHUNK * (pp + 1)] = (_rope_inv(d, cos_t, sa_t, sb_t) * Q_SCALE).astype(BF16)

    cur = lambda w: pl.BlockSpec((tk, w), lambda j: (j, 0))
    nxt = pl.BlockSpec((WINDOW, AUG_W), lambda j: (jnp.minimum((j + 1) * r, nb - 1), 0))
    whole = pl.BlockSpec((KV_GROUPS, 8, CHUNK), lambda j: (0, 0, 0))
    return pl.pallas_call(
        body, name="swa_bwd", grid=(nt,),
        out_shape=[jax.ShapeDtypeStruct((s, 1024), BF16), jax.ShapeDtypeStruct((KV_GROUPS, 8, CHUNK), F32)],
        in_specs=[cur(KV_GROUPS * CHUNK), cur(KV_GROUPS * CHUNK), cur(AUG_W), nxt, cur(AUG_W), nxt,
                  cur(CHUNK), cur(CHUNK), cur(CHUNK), whole],
        out_specs=[cur(1024), whole],
        scratch_shapes=[pltpu.VMEM((WINDOW, AUG_W), F32), pltpu.VMEM((tk, AUG_W), F32)],
        compiler_params=_params(1))(k_aug, v_aug, q_bwd, q_bwd, do_aug, do_aug, cos, sa, sb, sink_rows)


def _fox_fwd(first_key, q_aug, k_aug, v_aug, t):
    s = q_aug.shape[0]
    pw = 2 * CHUNK

    def body(lo_ref, q_ref, k_ref, v_ref, o_ref, qb_ref):
        i = pl.program_id(1)
        first = lo_ref[pl.program_id(0), i]
        row = lax.broadcasted_iota(jnp.int32, (t, t), 0)
        col = lax.broadcasted_iota(jnp.int32, (t, t), 1)
        lane = _lane(CHUNK)
        heads = (slice(0, CHUNK), slice(CHUNK, pw))
        qs = [q_ref[:, hs] for hs in heads]

        def step(jb, carry, masked):
            rows = pl.ds(pl.multiple_of(jb * t, t), t)
            scs = [_dot_nt(qs[hh], k_ref[rows, hs]) for hh, hs in enumerate(heads)]
            if masked:
                scs = [jnp.where(col <= row, sc, NEG) for sc in scs]
            m_news = [jnp.maximum(carry[2 * hh], jnp.max(scs[hh], axis=1, keepdims=True)) for hh in range(2)]
            ps = [jnp.exp((scs[hh] - m_news[hh]).astype(BF16)) for hh in range(2)]
            new = []
            for hh, hs in enumerate(heads):
                m, acc = carry[2 * hh], carry[2 * hh + 1]
                new += [m_news[hh], jnp.exp(m - m_news[hh]) * acc + _dot(ps[hh], v_ref[rows, hs])]
            return tuple(new)

        init = (jnp.full((t, 1), NEG, F32), jnp.zeros((t, CHUNK), F32)) * 2
        carry = step(i, lax.fori_loop(first, i, lambda jb, c: step(jb, c, False), init), True)
        outs = []
        for hh, hs in enumerate(heads):
            m, acc = carry[2 * hh], carry[2 * hh + 1]
            denom = _lane_sum(acc, 64, 65)
            outs.append(acc * (1.0 / denom))
            lse = m + jnp.log(denom)
            qb_ref[:, hs] = _place3(lane, 67, -lse, qs[hh].astype(F32)).astype(BF16)
        o_ref[...] = _pair(outs[0], outs[1])

    return pl.pallas_call(
        body, name="fox_fwd", grid=(N_HEADS // 2, s // t),
        out_shape=[jax.ShapeDtypeStruct((s, 512), F32), jax.ShapeDtypeStruct((s, AUG_W), BF16)],
        in_specs=[pl.BlockSpec(memory_space=pltpu.SMEM),
                  pl.BlockSpec((t, pw), lambda p, i: (i, p)),
                  pl.BlockSpec((s, pw), lambda p, i: (0, p)),
                  pl.BlockSpec((s, pw), lambda p, i: (0, p))],
        out_specs=[pl.BlockSpec((t, CHUNK), lambda p, i: (i, p)), pl.BlockSpec((t, pw), lambda p, i: (i, p))],
        compiler_params=_params(2))(first_key, q_aug, k_aug, v_aug)


def _fox_block_ranges(bounds, norms, r):
    cum_first = bounds[0::r, 0, :N_HEADS]
    cum_last = bounds[r - 1::r, 1, :N_HEADS]
    n = cum_first.shape[0]
    reach = 2.0 * 1.02 * jnp.sqrt(norms[0, :N_HEADS] * norms[1, :N_HEADS]) + UNDERFLOW
    blk = jnp.arange(n, dtype=jnp.int32)
    decay = cum_last[None, :, :] - cum_first[:, None, :]
    skip = (decay > reach[None, None, :]) & (blk[None, :, None] < blk[:, None, None])
    first = jnp.min(jnp.where(skip, n, blk[None, :, None]), axis=1)
    first = jnp.minimum(first[:, 0::2], first[:, 1::2]).T
    needed = (first[:, :, None] <= blk[None, None, :]) & (blk[None, :, None] >= blk[None, None, :])
    last = jnp.max(jnp.where(needed, blk[None, :, None], blk[None, None, :]), axis=1)
    return first.astype(jnp.int32), last.astype(jnp.int32)


def _fox_bwd(last_query, k_aug, v_aug, q_bwd, do_aug, t):
    s = k_aug.shape[0]
    n = s // t
    pw = 2 * CHUNK

    def body(hi_ref, k_ref, v_ref, q_ref, do_ref, dk_ref, dv_ref, dck_ref, dq_ref, dcq_ref, dq_scr):
        j = pl.program_id(1)
        last = hi_ref[pl.program_id(0), j]

        @pl.when(j == 0)
        def _():
            dq_scr[...] = jnp.zeros_like(dq_scr)

        row = lax.broadcasted_iota(jnp.int32, (t, t), 0)
        col = lax.broadcasted_iota(jnp.int32, (t, t), 1)
        lane = _lane(CHUNK)
        heads = (slice(0, CHUNK), slice(CHUNK, pw))
        ks = [k_ref[:, hs] for hs in heads]
        vs = [v_ref[:, hs] for hs in heads]

        def step(ib, carry, masked):
            rows = pl.ds(pl.multiple_of(ib * t, t), t)
            new = []
            for hh, hs in enumerate(heads):
                dv, dk = carry[2 * hh], carry[2 * hh + 1]
                q, do = q_ref[rows, hs], do_ref[rows, hs]
                st = _dot_nt(ks[hh], q)
                if masked:
                    st = jnp.where(row <= col, st, NEG)
                pt = jnp.exp(st)
                ds = (pt * _dot_nt(vs[hh], do)).astype(BF16)
                new += [dv + _dot(pt.astype(BF16), do), dk + _dot(ds, q)]
                dq_scr[rows, hs] += _dot_tn(ds, ks[hh])
            return tuple(new)

        zero = jnp.zeros((t, CHUNK), F32)
        carry = lax.fori_loop(j + 1, last + 1, lambda ib, c: step(ib, c, False), step(j, (zero,) * 4, True))
        dvs, dks = (carry[0], carry[2]), (carry[1], carry[3])
        dk_ref[...] = _pair(dks[0], dks[1]).astype(BF16)
        dv_ref[...] = _pair(dvs[0], dvs[1]).astype(BF16)
        dck_ref[0] = jnp.where(lane == 0, pltpu.roll(dks[0], 64, 1),
                               jnp.where(lane == 1, pltpu.roll(dks[1], 65, 1), 0.0))

        @pl.when(j == n - 1)
        def _():
            for ib in range(n):
                rows = slice(t * ib, t * (ib + 1))
                d0, d1 = dq_scr[rows, 0:CHUNK], dq_scr[rows, CHUNK:pw]
                dq_ref[rows, :] = (_pair(d0, d1) * Q_SCALE).astype(BF16)
                dcq_ref[0, rows, :] = jnp.where(lane == 0, pltpu.roll(d0, CHUNK - 67, 1),
                                                jnp.where(lane == 1, pltpu.roll(d1, CHUNK - 66, 1), 0.0))

    return pl.pallas_call(
        body, name="fox_bwd", grid=(N_HEADS // 2, n),
        out_shape=[jax.ShapeDtypeStruct((s, 512), BF16), jax.ShapeDtypeStruct((s, 512), BF16),
                   jax.ShapeDtypeStruct((N_HEADS // 2, s, CHUNK), F32), jax.ShapeDtypeStruct((s, 512), BF16),
                   jax.ShapeDtypeStruct((N_HEADS // 2, s, CHUNK), F32)],
        in_specs=[pl.BlockSpec(memory_space=pltpu.SMEM),
                  pl.BlockSpec((t, pw), lambda p, j: (j, p)), pl.BlockSpec((t, pw), lambda p, j: (j, p)),
                  pl.BlockSpec((s, pw), lambda p, j: (0, p)), pl.BlockSpec((s, pw), lambda p, j: (0, p))],
        out_specs=[pl.BlockSpec((t, CHUNK), lambda p, j: (j, p)), pl.BlockSpec((t, CHUNK), lambda p, j: (j, p)),
                   pl.BlockSpec((1, t, CHUNK), lambda p, j: (p, j, 0)), pl.BlockSpec((s, CHUNK), lambda p, j: (0, p)),
                   pl.BlockSpec((1, s, CHUNK), lambda p, j: (p, 0, 0))],
        scratch_shapes=[pltpu.VMEM((s, pw), F32)],
        compiler_params=_params(2))(last_query, k_aug, v_aug, q_bwd, do_aug)


def _fgate_bwd(dcq, dck, xf, tm):
    s = xf.shape[0]
    nt = s // tm

    def body(dq_ref, dc_ref, xf_ref, df_ref, dbf_ref, carry_ref):
        @pl.when(pl.program_id(0) == 0)
        def _():
            carry_ref[...] = jnp.zeros_like(carry_ref)
            dbf_ref[...] = jnp.zeros_like(dbf_ref)

        row = lax.broadcasted_iota(jnp.int32, (tm, tm), 0)
        col = lax.broadcasted_iota(jnp.int32, (tm, tm), 1)
        tri = jnp.where(col >= row, 1.0, 0.0).astype(BF16)
        dcum = dq_ref[0] - dc_ref[0]
        for p in range(1, N_HEADS // 2):
            dcum = dcum + pltpu.roll(dq_ref[p] - dc_ref[p], 2 * p, 1)
        hi, mid, lo = _split3(dcum)
        dlogf = _dot(tri, hi.astype(BF16)) + _dot(tri, mid.astype(BF16)) + _dot(tri, lo.astype(BF16))
        dlogf = dlogf + carry_ref[0:1, :]
        carry_ref[0:1, :] += jnp.sum(dcum, axis=0, keepdims=True)
        df = dlogf / (1.0 + jnp.exp(xf_ref[...]))
        df_ref[...] = df.astype(BF16)
        dbf_ref[0:1, :] += jnp.sum(df, axis=0, keepdims=True)

    rev = pl.BlockSpec((tm, CHUNK), lambda i: (nt - 1 - i, 0))
    rev4 = pl.BlockSpec((N_HEADS // 2, tm, CHUNK), lambda i: (0, nt - 1 - i, 0))
    return pl.pallas_call(
        body, name="fgate_bwd", grid=(nt,),
        out_shape=[jax.ShapeDtypeStruct((s, CHUNK), BF16), jax.ShapeDtypeStruct((8, CHUNK), F32)],
        in_specs=[rev4, rev4, rev], out_specs=[rev, pl.BlockSpec((8, CHUNK), lambda i: (0, 0))],
        scratch_shapes=[pltpu.VMEM((8, CHUNK), F32)],
        compiler_params=_params(1))(dcq, dck, xf)


def _post(x, tgt, att_a, att_b, za, zb, ga, gb, woa_t, wob_t, wout, vec, ind, emat, tm):
    s = x.shape[0]

    def body(x_ref, t_ref, aa_ref, ab_ref, za_ref, zb_ref, ga_ref, gb_ref, woat_ref, wobt_ref, wout_ref,
             vec_ref, ind_ref, e_ref,
             dx1_ref, ua_ref, ub_ref, mg_ref, do_ref, dya_ref, dyb_ref, doa_ref, dob_ref, dzg_ref, acc_ref):
        @pl.when(pl.program_id(0) == 0)
        def _():
            acc_ref[...] = jnp.zeros_like(acc_ref)

        gate, gfin = vec_ref[0:1, :], vec_ref[1:2, :]
        inv_d = 1.0 / D_MODEL

        def branch_fwd(att_ref, z_ref, w_ref, u_ref):
            att, z = att_ref[...], z_ref[...]
            sz = _sigmoid(z)
            silu = z * sz
            u = (att * silu).astype(BF16)
            u_ref[...] = u.T
            return att, z, sz, silu, _dot_nt(u, w_ref[...])

        att_a, z_a, sz_a, silu_a, y_a = branch_fwd(aa_ref, za_ref, woat_ref, ua_ref)
        att_b, z_b, sz_b, silu_b, y_b = branch_fwd(ab_ref, zb_ref, wobt_ref, ub_ref)
        sg_a, sg_b = _sigmoid(ga_ref[...]), _sigmoid(gb_ref[...])
        merged = (sg_a * y_a + sg_b * y_b).astype(BF16)
        mg_ref[...] = merged.T
        o = _dot(merged, wout_ref[...])
        x1 = x_ref[...] + gate * o
        rstd = lax.rsqrt(jnp.mean(x1 * x1, axis=-1, keepdims=True) + NORM_EPS)
        xh = x1 * rstd
        diff = xh * gfin - t_ref[...]
        acc_ref[2:3, :] += (0.5 * inv_d) * jnp.sum(diff * diff, axis=0, keepdims=True)
        dy = diff * inv_d
        acc_ref[1:2, :] += jnp.sum(dy * xh, axis=0, keepdims=True)
        dyg = dy * gfin
        dx1 = rstd * (dyg - xh * jnp.mean(dyg * xh, axis=-1, keepdims=True))
        dx1_ref[...] = dx1
        acc_ref[0:1, :] += jnp.sum(dx1 * o, axis=0, keepdims=True)
        d_o = (dx1 * gate).astype(BF16)
        do_ref[...] = d_o
        dmg = _dot_nt(d_o, wout_ref[...])

        lane = _lane(CHUNK)
        low = lane < HEAD_DIM
        lane_all = _lane(AUG_W) & (CHUNK - 1)

        zg0 = SEG["za"][0]

        def branch_bwd(sg, y, att, z, sz, silu, wt_ref, dy_ref, g_name, z_name, doaug_ref):
            dyb = (dmg * sg).astype(BF16)
            dy_ref[...] = dyb
            g_off, z_off = SEG[g_name][0] - zg0, SEG[z_name][0] - zg0
            dzg_ref[:, g_off:g_off + D_MODEL] = (dmg * y * sg * (1.0 - sg)).astype(BF16)
            du = _dot(dyb, wt_ref[...])
            datt = du * silu
            dzg_ref[:, z_off:z_off + 512] = (du * att * (sz * (1.0 + z * (1.0 - sz)))).astype(BF16)
            extra = _spread3(-_dot_split(datt * att, ind_ref[...]), e_ref)
            for p in range(N_HEADS // 2):
                c = datt[:, CHUNK * p:CHUNK * (p + 1)]
                for hh in range(2):
                    hd = 2 * p + hh
                    src = c if hh == 0 else pltpu.roll(c, HEAD_DIM, 1)
                    doaug_ref[:, CHUNK * hd:CHUNK * (hd + 1)] = jnp.where(
                        low, src, extra[:, CHUNK * hd:CHUNK * (hd + 1)]).astype(BF16)

        branch_bwd(sg_a, y_a, att_a, z_a, sz_a, silu_a, woat_ref, dya_ref, "ga", "za", doa_ref)
        branch_bwd(sg_b, y_b, att_b, z_b, sz_b, silu_b, wobt_ref, dyb_ref, "gb", "zb", dob_ref)

    row_blk = lambda w: pl.BlockSpec((tm, w), lambda i: (i, 0))
    full = lambda a: pl.BlockSpec(a.shape, lambda i: (0,) * a.ndim)
    sds = lambda w, dt: jax.ShapeDtypeStruct((s, w), dt)
    tds = lambda w: jax.ShapeDtypeStruct((w, s), BF16)
    col_blk = lambda w: pl.BlockSpec((w, tm), lambda i: (0, i))
    out_shape = [sds(D_MODEL, F32), tds(512), tds(512), tds(D_MODEL), sds(D_MODEL, BF16),
                 sds(D_MODEL, BF16), sds(D_MODEL, BF16), sds(AUG_W, BF16), sds(AUG_W, BF16), sds(3072, BF16),
                 jax.ShapeDtypeStruct((8, D_MODEL), F32)]
    ins = [x, tgt, att_a, att_b, za, zb, ga, gb, woa_t, wob_t, wout, vec, ind, emat]
    in_specs = [row_blk(a.shape[1]) for a in ins[:8]] + [full(a) for a in ins[8:]]
    out_specs = ([row_blk(D_MODEL), col_blk(512), col_blk(512), col_blk(D_MODEL)]
                 + [row_blk(o.shape[1]) for o in out_shape[4:-1]] + [pl.BlockSpec((8, D_MODEL), lambda i: (0, 0))])
    return pl.pallas_call(body, name="post", grid=(s // tm,), out_shape=out_shape, in_specs=in_specs,
                          out_specs=out_specs, compiler_params=_params(1))(*ins)


def _bwd_pre(dproj, w_all, x, dx1, mod, tm, chip_halves, modes):
    s = x.shape[0]
    ng, nh = len(DPROJ_GROUPS), len(chip_halves)
    last = s // tm - 1

    def body(*refs):
        dp_refs = refs[:ng]
        wt_ref, x_ref, dx1_ref, mod_ref = refs[ng:ng + 4]
        send_refs = refs[ng + 4:ng + 4 + nh]
        gx_ref, acc_ref = refs[ng + 4 + nh:ng + 6 + nh]
        recv_refs = refs[ng + 6 + nh:ng + 6 + 2 * nh]
        sems = refs[ng + 6 + 2 * nh:]

        @pl.when(pl.program_id(0) == 0)
        def _():
            acc_ref[...] = jnp.zeros_like(acc_ref)
            _exchange_start(_exchange_copies(send_refs, recv_refs, modes, *sems))

        dh = None
        for dp_ref, (_, off, w) in zip(dp_refs, DPROJ_GROUPS):
            part = _dot_nt(dp_ref[...], wt_ref[:, off:off + w])
            dh = part if dh is None else dh + part
        xv = x_ref[...]
        rstd = lax.rsqrt(jnp.mean(xv * xv, axis=-1, keepdims=True) + NORM_EPS)
        xh = xv * rstd
        gn = mod_ref[0:1, :]
        acc_ref[0:1, :] += jnp.sum(dh, axis=0, keepdims=True)
        acc_ref[1:2, :] += jnp.sum(dh * (xh * gn), axis=0, keepdims=True)
        dn = dh * (1.0 + mod_ref[1:2, :])
        acc_ref[2:3, :] += jnp.sum(dn * xh, axis=0, keepdims=True)
        dxh = dn * gn
        gx_ref[...] = dx1_ref[...] + rstd * (dxh - xh * jnp.mean(dxh * xh, axis=-1, keepdims=True))

        @pl.when(pl.program_id(0) == last)
        def _():
            _exchange_wait(_exchange_copies(send_refs, recv_refs, modes, *sems))

    row_blk = lambda w: pl.BlockSpec((tm, w), lambda i: (i, 0))
    full = lambda a: pl.BlockSpec(a.shape, lambda i: (0,) * a.ndim)
    outs = pl.pallas_call(
        body, name="bwd_pre", grid=(s // tm,),
        out_shape=[jax.ShapeDtypeStruct((s, D_MODEL), F32), jax.ShapeDtypeStruct((8, D_MODEL), F32)]
        + _exchange_out_shapes(chip_halves, modes),
        in_specs=[row_blk(w) for _, _, w in DPROJ_GROUPS] + [_resident(w_all), row_blk(D_MODEL), row_blk(D_MODEL), full(mod)]
        + [ANY_SPEC] * nh,
        out_specs=[row_blk(D_MODEL), pl.BlockSpec((8, D_MODEL), lambda i: (0, 0))] + [ANY_SPEC] * nh,
        scratch_shapes=_exchange_sems(nh),
        compiler_params=_params(1))(*dproj, w_all, x, dx1, mod, *chip_halves)
    return outs[0], outs[1], outs[2:]


def _accumulate_tokens(a, b_ref, acc_ref, o_ref, last):
    @pl.when(pl.program_id(0) == 0)
    def _():
        acc_ref[...] = jnp.zeros_like(acc_ref)

    acc_ref[...] += _dot(a, b_ref[...])

    @pl.when(pl.program_id(0) == last)
    def _():
        o_ref[...] = acc_ref[...].astype(BF16)


def _token_matmul_multi(a_t, bs, tk, name):
    m, s = a_t.shape
    nb = len(bs)
    last = s // tk - 1

    def body(*refs):
        a_ref, b_refs, o_refs, acc_refs = refs[0], refs[1:1 + nb], refs[1 + nb:1 + 2 * nb], refs[1 + 2 * nb:]
        a = a_ref[...]
        for b_ref, o_ref, acc_ref in zip(b_refs, o_refs, acc_refs):
            _accumulate_tokens(a, b_ref, acc_ref, o_ref, last)

    return pl.pallas_call(
        body, name=name, grid=(s // tk,),
        out_shape=[jax.ShapeDtypeStruct((m, b.shape[1]), BF16) for b in bs],
        in_specs=[pl.BlockSpec((m, tk), lambda k: (0, k))] + [pl.BlockSpec((tk, b.shape[1]), lambda k: (k, 0)) for b in bs],
        out_specs=[pl.BlockSpec((m, b.shape[1]), lambda k: (0, 0)) for b in bs],
        scratch_shapes=[pltpu.VMEM((m, b.shape[1]), F32) for b in bs],
        compiler_params=_params(1))(a_t, *bs)


def _adam_math(g, w, m, v):
    m2 = ADAM_B1 * m + (1.0 - ADAM_B1) * g
    v2 = ADAM_B2 * v + (1.0 - ADAM_B2) * (g * g)
    delta = -ADAM_LR * ((m2 / ADAM_C1) / (jnp.sqrt(v2 / ADAM_C2) + ADAM_EPS) + ADAM_WD * w)
    return delta, m2, v2


def _adamw_parts(full, parts, others, w, m, v, tr, name):
    _, rws, cols = w.shape

    def body(oth_ref, f_ref, p_ref, w_ref, m_ref, v_ref, g_ref, d_ref, m2_ref, v2_ref):
        g = None
        for chip in range(N_CHIPS):
            part = (1.0 - oth_ref[chip]) * f_ref[chip] + oth_ref[chip] * p_ref[chip].astype(F32)
            g = part if g is None else g + part
        g_ref[0] = g
        d_ref[0], m2_ref[0], v2_ref[0] = _adam_math(g, w_ref[0], m_ref[0], v_ref[0])

    blk = pl.BlockSpec((1, tr, cols), lambda i: (0, i, 0))
    slots = pl.BlockSpec((N_CHIPS, tr, cols), lambda i: (0, i, 0))
    o = jax.ShapeDtypeStruct((1, rws, cols), F32)
    return pl.pallas_call(
        body, name=name, grid=(rws // tr,), out_shape=[o, o, o, o],
        in_specs=[pl.BlockSpec(memory_space=pltpu.SMEM), slots, slots, blk, blk, blk],
        out_specs=[blk, blk, blk, blk], compiler_params=_params(1))(others, full, parts, w, m, v)


def _adamw_devices(parts, w, m, v, tr, name):
    _, rws, cols = w.shape

    def body(p_ref, w_ref, m_ref, v_ref, g_ref, d_ref, m2_ref, v2_ref):
        g = p_ref[0].astype(F32)
        for dev in range(1, N_DEV):
            g = g + p_ref[dev].astype(F32)
        g_ref[0] = g
        d_ref[0], m2_ref[0], v2_ref[0] = _adam_math(g, w_ref[0], m_ref[0], v_ref[0])

    blk = pl.BlockSpec((1, tr, cols), lambda i: (0, i, 0))
    o = jax.ShapeDtypeStruct((1, rws, cols), F32)
    return pl.pallas_call(
        body, name=name, grid=(rws // tr,), out_shape=[o, o, o, o],
        in_specs=[pl.BlockSpec((N_DEV, tr, cols), lambda i: (0, i, 0)), blk, blk, blk],
        out_specs=[blk, blk, blk, blk], compiler_params=_params(1))(parts, w, m, v)


def _adamw_ada(c_lanes, d_rows, w, m, v, tr):
    _, rws, cols = w.shape

    def body(c_ref, d_ref, w_ref, m_ref, v_ref, g_ref, dl_ref, m2_ref, v2_ref):
        for k in range(cols // CHUNK):
            cs = slice(CHUNK * k, CHUNK * (k + 1))
            g = c_ref[0] * d_ref[0:1, cs]
            for b in range(1, N_DEV):
                g = g + c_ref[b] * d_ref[b:b + 1, cs]
            g_ref[0, :, cs] = g
            dl_ref[0, :, cs], m2_ref[0, :, cs], v2_ref[0, :, cs] = _adam_math(
                g, w_ref[0, :, cs], m_ref[0, :, cs], v_ref[0, :, cs])

    blk = pl.BlockSpec((1, tr, cols), lambda i: (0, i, 0))
    o = jax.ShapeDtypeStruct((1, rws, cols), F32)
    return pl.pallas_call(
        body, name="adamw_ada", grid=(rws // tr,), out_shape=[o, o, o, o],
        in_specs=[pl.BlockSpec((N_DEV, tr, CHUNK), lambda i: (0, i, 0)), pl.BlockSpec((N_DEV, cols), lambda i: (0, 0)),
                  blk, blk, blk],
        out_specs=[blk, blk, blk, blk], compiler_params=_params(1))(c_lanes, d_rows, w, m, v)


SMALL = (("b_ada", 0, 0, 3 * D_MODEL), ("g_norm", 1, 0, D_MODEL), ("b_f", 3, 0, N_HEADS), ("sinks", 4, 0, N_HEADS),
         ("g_final", 1, D_MODEL, D_MODEL))
LOSS_ROW = 2


def _pack_small(pre_acc, post_acc, dbf_acc, dsink):
    def body(pre_ref, post_ref, dbf_ref, dsink_ref, o_ref):
        o_ref[...] = jnp.zeros_like(o_ref)
        d = D_MODEL
        o_ref[0:1, 0:d], o_ref[0:1, d:2 * d], o_ref[0:1, 2 * d:3 * d] = pre_ref[0:1, :], pre_ref[1:2, :], post_ref[0:1, :]
        o_ref[1:2, 0:d], o_ref[1:2, d:2 * d] = pre_ref[2:3, :], post_ref[1:2, :]
        o_ref[LOSS_ROW:LOSS_ROW + 1, 0:d] = post_ref[2:3, :]
        o_ref[3:4, 0:CHUNK] = dbf_ref[0:1, :]
        lane = _lane(CHUNK)
        sinks = jnp.zeros((1, CHUNK), F32)
        for kv in range(KV_GROUPS):
            for g in range(GROUP):
                sinks = jnp.where(lane == GROUP * kv + g, dsink_ref[kv, g:g + 1, :], sinks)
        o_ref[4:5, 0:CHUNK] = sinks

    return pl.pallas_call(body, name="pack_small", out_shape=jax.ShapeDtypeStruct((8, 3 * D_MODEL), F32),
                          compiler_params=_params(0))(pre_acc, post_acc, dbf_acc, dsink)


def _adamw_small(packs, params):
    n = len(SMALL)

    def body(*refs):
        p_ref, wmv = refs[0], refs[1:1 + 3 * n]
        outs, loss_ref, tot_ref = refs[1 + 3 * n:1 + 7 * n], refs[1 + 7 * n], refs[2 + 7 * n]
        tot = p_ref[0]
        for dev in range(1, N_DEV):
            tot = tot + p_ref[dev]
        tot_ref[...] = tot
        for i, (_, row, lo, w) in enumerate(SMALL):
            g = tot_ref[row:row + 1, lo:lo + w]
            o = outs[4 * i:4 * i + 4]
            o[0][...] = g
            o[1][...], o[2][...], o[3][...] = _adam_math(g, wmv[3 * i][...], wmv[3 * i + 1][...], wmv[3 * i + 2][...])
        loss_ref[...] = jnp.broadcast_to(jnp.sum(tot_ref[LOSS_ROW:LOSS_ROW + 1, 0:D_MODEL], axis=1, keepdims=True),
                                         loss_ref.shape)

    flat = [a for wmv in params for a in wmv]
    out_shape = [jax.ShapeDtypeStruct(wmv[0].shape, F32) for wmv in params for _ in range(4)]
    res = pl.pallas_call(body, name="adamw_small", out_shape=out_shape + [jax.ShapeDtypeStruct((8, CHUNK), F32)],
                         scratch_shapes=[pltpu.VMEM((8, 3 * D_MODEL), F32)],
                         compiler_params=_params(0))(packs, *flat)
    return [tuple(res[4 * i:4 * i + 4]) for i in range(n)], res[4 * n]


def _tile(s, want):
    return min(s, want)


def _layout_pieces():
    orig = {"qa": 0, "ka": 512, "va": 640, "za": 768, "qb": 1280, "kb": 1792, "vb": 2304, "f": 2816, "zb": 2824,
            "ga": 3336, "gb": 4360}
    pieces = []
    for name, (off, w) in SEG.items():
        if name in ("ka", "va"):
            pieces += [(orig[name] + HEAD_DIM * kv, orig[name] + HEAD_DIM * (kv + 1), off + CHUNK * kv) for kv in range(KV_GROUPS)]
        else:
            pieces.append((orig[name], orig[name] + (N_HEADS if name == "f" else w), off))
    return pieces


def _assemble_w_all(win_g):
    cols, pos = [], 0
    for lo, hi, new in sorted(_layout_pieces(), key=lambda t: t[2]):
        if new > pos:
            cols.append(jnp.zeros((D_MODEL, new - pos), win_g.dtype))
        col = lo
        while col < hi:
            dev = col // IN_SHARD
            end = min(hi, (dev + 1) * IN_SHARD)
            cols.append(win_g[dev, :, col - dev * IN_SHARD:end - dev * IN_SHARD])
            col = end
        pos = new + hi - lo
    cols.append(jnp.zeros((D_MODEL, PROJ_W - pos), win_g.dtype))
    return jnp.concatenate(cols, axis=1)


def _grad_slot(dw_groups, dev):
    lo_d, hi_d = dev * IN_SHARD, (dev + 1) * IN_SHARD
    cols = []
    for lo, hi, new in sorted(_layout_pieces()):
        a, b = max(lo, lo_d), min(hi, hi_d)
        if a < b:
            arr, off = next((g, o) for g, (_, o, w) in zip(dw_groups, DPROJ_GROUPS) if o <= new < o + w)
            cols.append(arr[:, new - off + a - lo:new - off + b - lo])
    return jnp.concatenate(cols, axis=1)


def kernel(x, c, positions, w_ada, b_ada, g_norm, w_in, b_f, sinks, w_o_swa, w_o_fox, w_out, g_final, loss_target, m_w_ada, m_b_ada, m_g_norm, m_w_in, m_b_f, m_sinks, m_w_o_swa, m_w_o_fox, m_w_out, m_g_final, v_w_ada, v_b_ada, v_g_norm, v_w_in, v_b_f, v_sinks, v_w_o_swa, v_w_o_fox, v_w_out, v_g_final):
    s = x.shape[1]
    tm = _tile(s, 256)
    ta = _tile(s, 512)
    me = 4 * lax.axis_index("x") + 2 * lax.axis_index("y") + lax.axis_index("c")
    x2, tgt = x[0], loss_target[0]

    inv_freq = np.power(np.float32(ROPE_THETA), -np.arange(0, HEAD_DIM, 2, dtype=np.float32) / HEAD_DIM)
    inv_freq = jnp.asarray(np.tile(inv_freq, CHUNK // (HEAD_DIM // 2))[None, :], F32)
    (cos, sa, sb), (win_g,), c_all = _rope_tables_and_gather(
        positions.reshape(s, 1).astype(F32), inv_freq, tm, [w_in[0].astype(BF16)], jnp.broadcast_to(c, (8, D_MODEL)))
    c_all = c_all[:, 0, :]
    b_shard = lax.dynamic_slice(b_ada, (0, me * ADA_SHARD), (1, ADA_SHARD))
    ada_part = _ada_fwd(c_all, w_ada[0], b_shard)
    (ada_g,) = _exchange([ada_part], ["gather"], "gather_ada")
    ada = lax.dynamic_index_in_dim(ada_g, me, axis=1, keepdims=False).reshape(3 * D_MODEL)
    shift, scale, gate = ada[:D_MODEL], ada[D_MODEL:2 * D_MODEL], ada[2 * D_MODEL:]
    w_all = _assemble_w_all(win_g)

    zrow = jnp.zeros((1, D_MODEL), F32)
    mod = jnp.concatenate([g_norm, scale[None], shift[None], zrow, zrow, zrow, zrow, zrow], axis=0)
    bf_row = jnp.pad(b_f, ((0, 0), (0, CHUNK - N_HEADS)))
    emat_np = np.zeros((3 * CHUNK, AUG_W), np.float32)
    ind_np = np.zeros((512, CHUNK), np.float32)
    for hd in range(N_HEADS):
        for part in range(3):
            emat_np[CHUNK * part + hd, CHUNK * hd + 64 + part] = 1.0
        ind_np[HEAD_DIM * hd:HEAD_DIM * (hd + 1), hd] = 1.0
    emat, ind = jnp.asarray(emat_np, BF16), jnp.asarray(ind_np, BF16)
    sink_rows = jnp.broadcast_to(jnp.pad(sinks.reshape(KV_GROUPS, GROUP), ((0, 0), (0, 8 - GROUP)))[:, :, None],
                                 (KV_GROUPS, 8, CHUNK))

    (h_t, qa, ka, va, za, qb, kb, vb, zb, ga, gb, xf, bounds, norms), (woa_g, wob_g, wout_g) = _fwd_proj(
        x2, mod, w_all, cos, sa, sb, bf_row, emat, ind, ta,
        [w_o_swa[0].T.astype(BF16), w_o_fox[0].T.astype(BF16), w_out[0].astype(BF16)])
    woa_t, wob_t, wout = woa_g.reshape(D_MODEL, 512), wob_g.reshape(D_MODEL, 512), wout_g.reshape(D_MODEL, D_MODEL)
    first_key, last_query = _fox_block_ranges(bounds, norms, 1)
    att_a, qa_bwd = _swa_fwd(qa, ka, va, sink_rows, ta)
    att_b, qb_bwd = _fox_fwd(first_key, qb, kb, vb, ta)
    vec = jnp.concatenate([gate[None], g_final[None], zrow, zrow, zrow, zrow, zrow, zrow], axis=0)
    (dx1, ua_t, ub_t, merged_t, d_o, dya, dyb, doa, dob, dzg, post_acc) = _post(
        x2, tgt, att_a, att_b, za, zb, ga, gb, woa_t, wob_t, wout, vec, ind, emat, tm)

    da, dsink = _swa_bwd(ka, va, qa_bwd, doa, cos, sa, sb, sink_rows, ta)
    dkb, dvb, dck, dqb, dcq = _fox_bwd(last_query, kb, vb, qb_bwd, dob, ta)
    dfb, dbf_acc = _fgate_bwd(dcq, dck, xf, ta)
    dproj = [da, dqb, dkb, dvb, dzg, dfb]
    dw_zg, = _token_matmul_multi(h_t, [dzg], ta, "dw_in_zg")
    dw_a, dw_qb, dw_kb, dw_vb, dw_f = _token_matmul_multi(h_t, [da, dqb, dkb, dvb, dfb], ta, "dw_in_rest")
    dw_all = [dw_a, dw_qb, dw_kb, dw_vb, dw_zg, dw_f]

    core, chip = lax.axis_index("c"), 2 * lax.axis_index("x") + lax.axis_index("y")
    win_slots = jnp.stack([jnp.stack([_grad_slot(dw_all, 2 * ch + co) for ch in range(N_CHIPS)]) for co in range(2)])
    (dwoa, dwob, dwout), (win_theirs,) = _token_matmuls_and_swap(
        [(ua_t, dya), (ub_t, dyb), (merged_t, d_o)], ta, [win_slots])
    win_full, win_half = _chip_partial(jnp.reshape(core, (1,)).astype(jnp.int32), win_slots, win_theirs, 256,
                                       "chip_partial_w_in")
    col_slots = lambda g: g.reshape(512, N_DEV, 128).transpose(1, 0, 2)
    grad_x, pre_acc, (p_win, p_woa, p_wob, p_wout) = _bwd_pre(
        dproj, w_all, x2, dx1, mod, ta,
        [win_half, col_slots(dwoa), col_slots(dwob), dwout.reshape(N_DEV, 128, D_MODEL)],
        ["chips", "devices", "devices", "devices"])
    (packs,) = _exchange([_pack_small(pre_acc, post_acc, dbf_acc, dsink)], ["gather"], "gather_small_grads")
    others = jnp.where(jnp.arange(N_CHIPS) == chip, 0.0, 1.0).astype(F32)

    g_win, d_win, m_win, v_win = _adamw_parts(win_full, p_win, others, w_in, m_w_in, v_w_in, 128, "adamw_w_in")
    g_woa, d_woa, m_woa, v_woa = _adamw_devices(p_woa, w_o_swa, m_w_o_swa, v_w_o_swa, 128, "adamw_w_o_swa")
    g_wob, d_wob, m_wob, v_wob = _adamw_devices(p_wob, w_o_fox, m_w_o_fox, v_w_o_fox, 128, "adamw_w_o_fox")
    g_wout, d_wout, m_wout, v_wout = _adamw_devices(p_wout, w_out, m_w_out, v_w_out, 128, "adamw_w_out")
    d_ada_rows = lax.dynamic_slice(packs[:, 0, :], (0, me * ADA_SHARD), (N_DEV, ADA_SHARD))
    c_lanes = jnp.broadcast_to(c_all[:, :, None], (N_DEV, D_MODEL, CHUNK))
    g_wada, d_wada, m_wada, v_wada = _adamw_ada(c_lanes, d_ada_rows, w_ada, m_w_ada, v_w_ada, 256)

    row = lambda a: a.reshape(1, D_MODEL)
    small, loss_rows = _adamw_small(packs, [(b_ada, m_b_ada, v_b_ada), (g_norm, m_g_norm, v_g_norm), (b_f, m_b_f, v_b_f),
                                            (sinks, m_sinks, v_sinks), (row(g_final), row(m_g_final), row(v_g_final))])
    bada_o, gn_o, bf_o, sk_o, gf_o = small

    outs = []
    for k, big in enumerate(((g_wada, g_win, g_woa, g_wob, g_wout), (d_wada, d_win, d_woa, d_wob, d_wout),
                             (m_wada, m_win, m_woa, m_wob, m_wout), (v_wada, v_win, v_woa, v_wob, v_wout))):
        wada_o, win_o, woa_o, wob_o, wout_o = big
        outs += [wada_o, bada_o[k], gn_o[k], win_o, bf_o[k], sk_o[k], woa_o, wob_o, wout_o, gf_o[k].reshape(D_MODEL)]
    return (loss_rows[0, 0], grad_x[None], *outs)
```

```python
import numpy as np
import jax
import jax.numpy as jnp
from jax import lax
from jax.experimental import pallas as pl
from jax.experimental.pallas import tpu as pltpu

F32 = jnp.float32
BF16 = jnp.bfloat16

D_MODEL = 1024
HEAD_DIM = 64
N_HEADS = 8
KV_GROUPS = 2
GROUP = N_HEADS // KV_GROUPS
WINDOW = 128
CHUNK = 128
AUG_W = N_HEADS * CHUNK
N_DEV = 8
IN_SHARD = 673
ADA_SHARD = 384
NORM_EPS = 1e-6
ROPE_THETA = 10000.0
Q_SCALE = HEAD_DIM ** -0.5
NEG = -1e30
UNDERFLOW = 105.0

ADAM_LR = 0.001
ADAM_B1 = 0.9
ADAM_B2 = 0.999
ADAM_EPS = 1e-08
ADAM_WD = 0.01
ADAM_STEP = 10
ADAM_C1 = 1.0 - ADAM_B1 ** ADAM_STEP
ADAM_C2 = 1.0 - ADAM_B2 ** ADAM_STEP

SEG = {}
_off = 0
for _name, _w in (("qa", 512), ("ka", 256), ("va", 256), ("qb", 512), ("kb", 512), ("vb", 512),
                  ("za", 512), ("zb", 512), ("ga", 1024), ("gb", 1024), ("f", 128)):
    SEG[_name] = (_off, _w)
    _off += _w
PROJ_W = _off
DPROJ_GROUPS = (("a", SEG["qa"][0], 1024), ("qb", SEG["qb"][0], 512), ("kb", SEG["kb"][0], 512),
                ("vb", SEG["vb"][0], 512), ("zg", SEG["za"][0], 3072), ("f", SEG["f"][0], 128))

VMEM_LIMIT = 56 * 1024 * 1024


def _params(n_axes):
    return pltpu.CompilerParams(dimension_semantics=("arbitrary",) * n_axes, vmem_limit_bytes=VMEM_LIMIT)


def _resident(a):
    return pl.BlockSpec(a.shape, lambda i: (0, 0), pipeline_mode=pl.Buffered(1))


def _dot(a, b):
    return jnp.dot(a, b, preferred_element_type=F32)


def _dot_nt(a, b):
    return lax.dot_general(a, b, (((1,), (1,)), ((), ())), preferred_element_type=F32)


def _dot_tn(a, b):
    return lax.dot_general(a, b, (((0,), (0,)), ((), ())), preferred_element_type=F32)


def _lane(n):
    return lax.broadcasted_iota(jnp.int32, (1, n), 1)


def _split3(x):
    hi = x.astype(BF16).astype(F32)
    r = x - hi
    mid = r.astype(BF16).astype(F32)
    lo = (r - mid).astype(BF16).astype(F32)
    return hi, mid, lo


def _dot_split(x, b):
    hi, mid, lo = _split3(x)
    return _dot(hi.astype(BF16), b) + _dot(mid.astype(BF16), b) + _dot(lo.astype(BF16), b)


def _spread3(x, e3_ref):
    return _dot(jnp.concatenate(_split3(x), axis=1).astype(BF16), e3_ref[...])


def _place3(lane, base, x, other):
    hi, mid, lo = _split3(x)
    return jnp.where(lane == base, hi, jnp.where(lane == base + 1, mid, jnp.where(lane == base + 2, lo, other)))


def _lane_sum(x, lo, hi):
    lane = _lane(x.shape[1])
    return jnp.sum(jnp.where((lane >= lo) & (lane < hi), x, 0.0), axis=1, keepdims=True)


def _ones_lanes(lo, hi):
    lane = _lane(CHUNK)
    return jnp.where((lane >= lo) & (lane < hi), 1.0, 0.0).astype(F32)


def _rope(c, cos, sa, sb):
    return c * cos + pltpu.roll(c, CHUNK - 32, 1) * sa + pltpu.roll(c, 32, 1) * sb


def _rope_inv(d, cos, sa, sb):
    return d * cos - (pltpu.roll(d, CHUNK - 32, 1) * sa + pltpu.roll(d, 32, 1) * sb)


def _pair(c0, c1):
    return jnp.where(_lane(CHUNK) < HEAD_DIM, c0, pltpu.roll(c1, HEAD_DIM, 1))


def _sigmoid(x):
    return 0.5 * jnp.tanh(0.5 * x) + 0.5


N_CHIPS = 4
ANY_SPEC = pl.BlockSpec(memory_space=pl.ANY)


def _exchange(arrays, modes, name):
    n = len(arrays)

    def body(*refs):
        copies = _exchange_copies(refs[:n], refs[n:2 * n], modes, *refs[2 * n:])
        _exchange_start(copies)
        _exchange_wait(copies)

    return pl.pallas_call(
        body, name=name, out_shape=_exchange_out_shapes(arrays, modes),
        in_specs=[ANY_SPEC] * n, out_specs=[ANY_SPEC] * n, scratch_shapes=_exchange_sems(n),
    )(*arrays)


def _exchange_out_shapes(arrays, modes):
    return [jax.ShapeDtypeStruct((N_DEV,) + a.shape if md == "gather" else a.shape, a.dtype)
            for a, md in zip(arrays, modes)]


def _exchange_sems(n):
    return [pltpu.SemaphoreType.DMA((n, N_DEV - 1)), pltpu.SemaphoreType.DMA((n, N_DEV - 1)),
            pltpu.SemaphoreType.DMA((n,))]


def _exchange_copies(srcs, dsts, modes, send_sems, recv_sems, loc_sems):
    n = len(srcs)
    x, y, c = lax.axis_index("x"), lax.axis_index("y"), lax.axis_index("c")

    def slot(i, px, py, pc):
        return 2 * px + py if modes[i] == "chips" else 4 * px + 2 * py + pc

    def src_of(i, px, py, pc):
        return srcs[i] if modes[i] == "gather" else srcs[i].at[slot(i, px, py, pc)]

    local = [pltpu.make_async_copy(src_of(i, x, y, c), dsts[i].at[slot(i, x, y, c)], loc_sems.at[i])
             for i in range(n)]
    sends, recvs = [], []
    for r in (1, 2, 4, 3, 5, 6, 7):
        px = 1 - x if r & 4 else x
        py = 1 - y if r & 2 else y
        pc = 1 - c if r & 1 else c
        for i in range(n):
            if modes[i] == "chips" and r & 1:
                continue
            sems = dict(send_sem=send_sems.at[i, r - 1], recv_sem=recv_sems.at[i, r - 1],
                        device_id=(px, py, pc), device_id_type=pl.DeviceIdType.MESH)
            sends.append(pltpu.make_async_remote_copy(
                src_ref=src_of(i, px, py, pc), dst_ref=dsts[i].at[slot(i, x, y, c)], **sems))
            recvs.append(pltpu.make_async_remote_copy(
                src_ref=src_of(i, px, py, pc), dst_ref=dsts[i].at[slot(i, px, py, pc)], **sems))
    return local, sends, recvs


def _exchange_start(copies):
    local, sends, _ = copies
    for cp in local + sends:
        cp.start()


def _exchange_wait(copies):
    local, sends, recvs = copies
    for cp in recvs:
        cp.wait_recv()
    for cp in sends:
        cp.wait_send()
    for cp in local:
        cp.wait()


def _gather2_parts(srcs, dsts, send_sems, recv_sems, loc_sems):
    n = len(srcs)
    x, y, c = lax.axis_index("x"), lax.axis_index("y"), lax.axis_index("c")
    me, sibling = (x, y, c), (x, y, 1 - c)
    chips = [(1 - x, y), (x, 1 - y), (1 - x, 1 - y)]

    def rows(i, dev):
        return dsts[i].at[4 * dev[0] + 2 * dev[1] + dev[2]]

    def copy(i, k, block, to, src=None):
        return pltpu.make_async_remote_copy(
            src_ref=rows(i, block) if src is None else src, dst_ref=rows(i, block),
            send_sem=send_sems.at[i, k], recv_sem=recv_sems.at[i, k],
            device_id=to, device_id_type=pl.DeviceIdType.MESH)

    local = [pltpu.make_async_copy(srcs[i], rows(i, me), loc_sems.at[i]) for i in range(n)]
    first = []
    for i in range(n):
        first.append(copy(i, 0, me, sibling, src=srcs[i]))
        first += [copy(i, 1 + j, me, (*chip, c), src=srcs[i]) for j, chip in enumerate(chips)]
    return n, c, me, sibling, chips, copy, local, first


def _gather2_start(srcs, dsts, send_sems, recv_sems, loc_sems):
    *_, local, first = _gather2_parts(srcs, dsts, send_sems, recv_sems, loc_sems)
    for cp in local + first:
        cp.start()


def _gather2_finish(srcs, dsts, send_sems, recv_sems, loc_sems):
    n, c, me, sibling, chips, copy, local, first = _gather2_parts(srcs, dsts, send_sems, recv_sems, loc_sems)
    passed = []
    for j, chip in enumerate(chips):
        for i in range(n):
            copy(i, 1 + j, (*chip, c), me).wait_recv()
            cp = copy(i, 4 + j, (*chip, c), sibling)
            cp.start()
            passed.append(cp)
    for i in range(n):
        copy(i, 0, sibling, me).wait_recv()
        for j, chip in enumerate(chips):
            copy(i, 4 + j, (*chip, 1 - c), me).wait_recv()
    for cp in first + passed:
        cp.wait_send()
    for cp in local:
        cp.wait()


def _sibling_copies(srcs, dsts, send_sems, recv_sems):
    x, y, c = lax.axis_index("x"), lax.axis_index("y"), lax.axis_index("c")
    return [pltpu.make_async_remote_copy(
        src_ref=srcs[i].at[1 - c, k], dst_ref=dsts[i].at[k], send_sem=send_sems.at[i, k], recv_sem=recv_sems.at[i, k],
        device_id=(x, y, 1 - c), device_id_type=pl.DeviceIdType.MESH)
        for i in range(len(srcs)) for k in range(N_CHIPS)]


def _token_matmuls_and_swap(pairs, tk, swap):
    npair, nsw = len(pairs), len(swap)
    s = pairs[0][0].shape[1]
    last = s // tk - 1

    def body(*refs):
        a_refs, b_refs = refs[0:2 * npair:2], refs[1:2 * npair:2]
        src = refs[2 * npair:2 * npair + nsw]
        o_refs = refs[2 * npair + nsw:3 * npair + nsw]
        dst = refs[3 * npair + nsw:3 * npair + 2 * nsw]
        sems = refs[3 * npair + 2 * nsw:3 * npair + 2 * nsw + 2]
        acc_refs = refs[3 * npair + 2 * nsw + 2:]

        @pl.when(pl.program_id(0) == 0)
        def _():
            for cp in _sibling_copies(src, dst, *sems):
                cp.start()

        for a_ref, b_ref, o_ref, acc_ref in zip(a_refs, b_refs, o_refs, acc_refs):
            _accumulate_tokens(a_ref[...], b_ref, acc_ref, o_ref, last)

        @pl.when(pl.program_id(0) == last)
        def _():
            for cp in _sibling_copies(src, dst, *sems):
                cp.wait()

    in_specs, args = [], []
    for a_t, b in pairs:
        in_specs += [pl.BlockSpec((a_t.shape[0], tk), lambda k: (0, k)), pl.BlockSpec((tk, b.shape[1]), lambda k: (k, 0))]
        args += [a_t, b]
    outs = pl.pallas_call(
        body, name="dw_out_projs_swap", grid=(s // tk,),
        out_shape=[jax.ShapeDtypeStruct((a_t.shape[0], b.shape[1]), BF16) for a_t, b in pairs]
        + [jax.ShapeDtypeStruct(a.shape[1:], a.dtype) for a in swap],
        in_specs=in_specs + [ANY_SPEC] * nsw,
        out_specs=[pl.BlockSpec((a_t.shape[0], b.shape[1]), lambda k: (0, 0)) for a_t, b in pairs] + [ANY_SPEC] * nsw,
        scratch_shapes=[pltpu.SemaphoreType.DMA((nsw, N_CHIPS)), pltpu.SemaphoreType.DMA((nsw, N_CHIPS))]
        + [pltpu.VMEM((a_t.shape[0], b.shape[1]), F32) for a_t, b in pairs],
        compiler_params=_params(1))(*args, *swap)
    return outs[:npair], outs[npair:]


def _chip_partial(core, slots, theirs, tr, name):
    _, k, rws, cols = slots.shape

    def body(core_ref, a_ref, b_ref, f_ref, h_ref):
        sm = a_ref[0].astype(F32) + b_ref[...].astype(F32)
        f_ref[...] = sm
        h_ref[...] = sm.astype(BF16)

    blk = pl.BlockSpec((1, tr, cols), lambda j, i, core_ref: (j, i, 0))
    mine = pl.BlockSpec((1, 1, tr, cols), lambda j, i, core_ref: (core_ref[0], j, i, 0))
    return pl.pallas_call(
        body, name=name,
        grid_spec=pltpu.PrefetchScalarGridSpec(num_scalar_prefetch=1, grid=(k, rws // tr),
                                               in_specs=[mine, blk], out_specs=[blk, blk]),
        out_shape=[jax.ShapeDtypeStruct(theirs.shape, F32), jax.ShapeDtypeStruct(theirs.shape, BF16)],
        compiler_params=_params(2))(core, slots, theirs)


def _ada_fwd(c_all, w_shard, b_shard):
    def body(c_ref, w_ref, b_ref, o_ref):
        ch, cm, cl = [t.astype(BF16) for t in _split3(c_ref[...])]
        wh, wm, wl = [t.astype(BF16) for t in _split3(w_ref[...])]
        acc = _dot(ch, wh) + _dot(ch, wm) + _dot(cm, wh) + _dot(ch, wl) + _dot(cl, wh) + _dot(cm, wm)
        o_ref[...] = acc + b_ref[...]

    return pl.pallas_call(body, name="ada_fwd", out_shape=jax.ShapeDtypeStruct((N_DEV, ADA_SHARD), F32),
                          compiler_params=_params(0))(c_all, w_shard, b_shard)


def _rope_tables_and_gather(pos_col, inv_freq, tm, weights, c_rows):
    s = pos_col.shape[0]
    nw = len(weights)
    last = s // tm - 1

    def body(*refs):
        p_ref, f_ref = refs[:2]
        w_src, c_src = refs[2:2 + nw], refs[2 + nw:3 + nw]
        cos_ref, sa_ref, sb_ref = refs[3 + nw:6 + nw]
        w_dst, c_dst = refs[6 + nw:6 + 2 * nw], refs[6 + 2 * nw:7 + 2 * nw]
        w_sems, c_sems = refs[7 + 2 * nw:10 + 2 * nw], refs[10 + 2 * nw:]

        @pl.when(pl.program_id(0) == 0)
        def _():
            _gather2_start(w_src, w_dst, *w_sems)
            _exchange_start(_exchange_copies(c_src, c_dst, ["gather"], *c_sems))

        ang = p_ref[...] * f_ref[...]
        sin = jnp.sin(ang)
        first_half = (_lane(CHUNK) & (HEAD_DIM - 1)) < HEAD_DIM // 2
        cos_ref[...] = jnp.cos(ang)
        sa_ref[...] = jnp.where(first_half, -sin, 0.0)
        sb_ref[...] = jnp.where(first_half, 0.0, sin)

        @pl.when(pl.program_id(0) == last)
        def _():
            _exchange_wait(_exchange_copies(c_src, c_dst, ["gather"], *c_sems))
            _gather2_finish(w_src, w_dst, *w_sems)

    tab = jax.ShapeDtypeStruct((s, CHUNK), F32)
    blk = pl.BlockSpec((tm, CHUNK), lambda i: (i, 0))
    gathered = [jax.ShapeDtypeStruct((N_DEV,) + a.shape, a.dtype) for a in list(weights) + [c_rows]]
    outs = pl.pallas_call(
        body, name="rope_tables_gather", grid=(s // tm,), out_shape=[tab, tab, tab] + gathered,
        in_specs=[pl.BlockSpec((tm, 1), lambda i: (i, 0)), pl.BlockSpec((1, CHUNK), lambda i: (0, 0))]
        + [ANY_SPEC] * (nw + 1),
        out_specs=[blk, blk, blk] + [ANY_SPEC] * (nw + 1),
        scratch_shapes=_exchange_sems(nw) + _exchange_sems(1),
        compiler_params=_params(1))(pos_col, inv_freq, *weights, c_rows)
    return outs[:3], outs[3:3 + nw], outs[3 + nw]


def _fwd_proj(x, mod, w_all, cos, sa, sb, bf_row, emat, ind, tm, late_shards):
    s = x.shape[0]
    nl = len(late_shards)
    last = s // tm - 1

    def body(x_ref, mod_ref, w_ref, cos_ref, sa_ref, sb_ref, bf_ref, e_ref, ind_ref, *rest):
        late_src = rest[:nl]
        (h_ref, qa_ref, ka_ref, va_ref, za_ref, qb_ref, kb_ref, vb_ref, zb_ref, ga_ref, gb_ref, f_ref,
         bnd_ref, nrm_ref) = rest[nl:nl + 14]
        late_dst = rest[nl + 14:2 * nl + 14]
        carry_ref, late_sems = rest[2 * nl + 14], rest[2 * nl + 15:]

        @pl.when(pl.program_id(0) == 0)
        def _():
            carry_ref[...] = jnp.zeros_like(carry_ref)
            _gather2_start(late_src, late_dst, *late_sems)

        xv = x_ref[...]
        rstd = lax.rsqrt(jnp.mean(xv * xv, axis=-1, keepdims=True) + NORM_EPS)
        h = (xv * rstd * mod_ref[0:1, :]) * (1.0 + mod_ref[1:2, :]) + mod_ref[2:3, :]
        hb = h.astype(BF16)
        h_ref[...] = hb.T

        def seg(name):
            off, w = SEG[name]
            return _dot(hb, w_ref[:, off:off + w])

        lane = _lane(CHUNK)
        low = lane < HEAD_DIM
        q_ones = _ones_lanes(64, 67)
        k_ones = _ones_lanes(67, 70)
        cos_t, sa_t, sb_t = cos_ref[...], sa_ref[...], sb_ref[...]

        def write_heads(ref, nat, extra_of, rope, scale):
            for p in range(N_HEADS // 2):
                c = nat[:, CHUNK * p:CHUNK * (p + 1)]
                if rope:
                    c = _rope(c, cos_t, sa_t, sb_t)
                if scale != 1.0:
                    c = c * scale
                for hh in range(2):
                    hd = 2 * p + hh
                    src = c if hh == 0 else pltpu.roll(c, HEAD_DIM, 1)
                    ref[:, CHUNK * hd:CHUNK * (hd + 1)] = jnp.where(low, src, extra_of(hd)).astype(BF16)

        write_heads(qa_ref, seg("qa"), lambda hd: q_ones, True, Q_SCALE)
        ka = seg("ka")
        va = seg("va")
        for kv in range(KV_GROUPS):
            sl = slice(CHUNK * kv, CHUNK * (kv + 1))
            ka_ref[:, sl] = jnp.where(low, _rope(ka[:, sl], cos_t, sa_t, sb_t), k_ones).astype(BF16)
            va_ref[:, sl] = jnp.where(low, va[:, sl], q_ones).astype(BF16)
        za_ref[...] = seg("za")
        zb_ref[...] = seg("zb")
        ga_ref[...] = seg("ga")
        gb_ref[...] = seg("gb")

        xf = seg("f") + bf_ref[...]
        f_ref[...] = xf
        logf = jnp.minimum(xf, 0.0) - jnp.log1p(jnp.exp(-jnp.abs(xf)))
        row = lax.broadcasted_iota(jnp.int32, (tm, tm), 0)
        col = lax.broadcasted_iota(jnp.int32, (tm, tm), 1)
        tri = jnp.where(col <= row, 1.0, 0.0).astype(BF16)
        hi, mid, lo = _split3(logf)
        cum = _dot(tri, hi.astype(BF16)) + _dot(tri, mid.astype(BF16)) + _dot(tri, lo.astype(BF16))
        cum = cum + carry_ref[0:1, :]
        carry_ref[0:1, :] += jnp.sum(logf, axis=0, keepdims=True)
        lane_all = _lane(AUG_W) & (CHUNK - 1)
        bias = _spread3(-cum, e_ref) + jnp.where((lane_all >= 67) & (lane_all < 70), 1.0, 0.0)

        qb, kb = seg("qb"), seg("kb")
        write_heads(qb_ref, qb, lambda hd: q_ones, False, Q_SCALE)
        write_heads(kb_ref, kb, lambda hd: bias[:, CHUNK * hd:CHUNK * (hd + 1)], False, 1.0)
        write_heads(vb_ref, seg("vb"), lambda hd: q_ones, False, 1.0)

        rid = lax.broadcasted_iota(jnp.int32, (tm, CHUNK), 0)
        bnd_ref[...] = jnp.zeros_like(bnd_ref)
        bnd_ref[0, 0:1, :] = jnp.sum(jnp.where(rid == 0, cum, 0.0), axis=0, keepdims=True)
        bnd_ref[0, 1:2, :] = jnp.sum(jnp.where(rid == tm - 1, cum, 0.0), axis=0, keepdims=True)

        @pl.when(pl.program_id(0) == 0)
        def _():
            nrm_ref[...] = jnp.zeros_like(nrm_ref)

        for r_, nat in ((0, qb * Q_SCALE), (1, kb)):
            sq = _dot((nat * nat).astype(BF16), ind_ref[...])
            nrm_ref[r_:r_ + 1, :] = jnp.maximum(nrm_ref[r_:r_ + 1, :], jnp.max(sq, axis=0, keepdims=True))

        @pl.when(pl.program_id(0) == last)
        def _():
            _gather2_finish(late_src, late_dst, *late_sems)

    row_blk = lambda w: pl.BlockSpec((tm, w), lambda i: (i, 0))
    full = lambda a: pl.BlockSpec(a.shape, lambda i: (0,) * a.ndim)
    sds = lambda w, dt: jax.ShapeDtypeStruct((s, w), dt)
    out_shape = [sds(AUG_W, BF16), sds(KV_GROUPS * CHUNK, BF16), sds(KV_GROUPS * CHUNK, BF16),
                 sds(512, F32), sds(AUG_W, BF16), sds(AUG_W, BF16), sds(AUG_W, BF16), sds(512, F32),
                 sds(D_MODEL, F32), sds(D_MODEL, F32), sds(CHUNK, F32)]
    small = [jax.ShapeDtypeStruct((s // tm, 8, CHUNK), F32), jax.ShapeDtypeStruct((8, CHUNK), F32)]
    gathered = [jax.ShapeDtypeStruct((N_DEV,) + a.shape, a.dtype) for a in late_shards]
    outs = pl.pallas_call(
        body, name="fwd_proj", grid=(s // tm,),
        out_shape=[jax.ShapeDtypeStruct((D_MODEL, s), BF16)] + out_shape + small + gathered,
        in_specs=[row_blk(D_MODEL), full(mod), _resident(w_all), row_blk(CHUNK), row_blk(CHUNK), row_blk(CHUNK),
                  full(bf_row), full(emat), full(ind)] + [ANY_SPEC] * nl,
        out_specs=[pl.BlockSpec((D_MODEL, tm), lambda i: (0, i))] + [row_blk(o.shape[1]) for o in out_shape] + [
            pl.BlockSpec((1, 8, CHUNK), lambda i: (i, 0, 0)), pl.BlockSpec((8, CHUNK), lambda i: (0, 0))]
        + [ANY_SPEC] * nl,
        scratch_shapes=[pltpu.VMEM((8, CHUNK), F32)] + _exchange_sems(nl),
        compiler_params=_params(1))(x, mod, w_all, cos, sa, sb, bf_row, emat, ind, *late_shards)
    return outs[:14], outs[14:]


def _swa_fwd(q_aug, k_aug, v_aug, sink_rows, tq):
    s = q_aug.shape[0]
    r = tq // WINDOW
    gw = GROUP * CHUNK

    def body(q_ref, kc_ref, kp_ref, vc_ref, vp_ref, sink_ref, o_ref, qb_ref):
        i = pl.program_id(1)
        st = GROUP * WINDOW
        qloc = lax.broadcasted_iota(jnp.int32, (st, 2 * WINDOW), 0) & (WINDOW - 1)
        col = lax.broadcasted_iota(jnp.int32, (st, 2 * WINDOW), 1)
        band = (col > qloc) & (col <= qloc + WINDOW)
        head = jnp.right_shift(lax.broadcasted_iota(jnp.int32, (st, 1), 0), 7)
        sink = jnp.zeros((st, 1), F32)
        for g in range(GROUP):
            sink = jnp.where(head == g, jnp.max(sink_ref[0, g:g + 1, :], axis=1, keepdims=True), sink)
        lane = _lane(CHUNK)
        for sub in range(r):
            rows = slice(WINDOW * sub, WINDOW * (sub + 1))
            q = jnp.concatenate([q_ref[rows, CHUNK * g:CHUNK * (g + 1)] for g in range(GROUP)], axis=0)
            if sub == 0:
                k = jnp.concatenate([kp_ref[...], kc_ref[rows, :]], axis=0)
                v = jnp.concatenate([vp_ref[...], vc_ref[rows, :]], axis=0)
                valid = band & ((col >= WINDOW) | (i > 0))
            else:
                both = slice(WINDOW * (sub - 1), WINDOW * (sub + 1))
                k, v, valid = kc_ref[both, :], vc_ref[both, :], band
            sc = jnp.where(valid, _dot_nt(q, k), NEG)
            m = jnp.maximum(jnp.max(sc, axis=1, keepdims=True), sink)
            acc = _dot(jnp.exp((sc - m).astype(BF16)), v)
            denom = _lane_sum(acc, 64, 65) + jnp.exp(sink - m)
            out = acc * (1.0 / denom)
            aug = _place3(lane, 67, -(m + jnp.log(denom)), q.astype(F32)).astype(BF16)
            hrows = lambda a, g: a[WINDOW * g:WINDOW * (g + 1), :]
            for g in range(GROUP):
                qb_ref[rows, CHUNK * g:CHUNK * (g + 1)] = hrows(aug, g)
            for pp in range(GROUP // 2):
                o_ref[rows, CHUNK * pp:CHUNK * (pp + 1)] = _pair(hrows(out, 2 * pp), hrows(out, 2 * pp + 1))

    return pl.pallas_call(
        body, name="swa_fwd", grid=(KV_GROUPS, s // tq),
        out_shape=[jax.ShapeDtypeStruct((s, 512), F32), jax.ShapeDtypeStruct((s, AUG_W), BF16)],
        in_specs=[pl.BlockSpec((tq, gw), lambda kv, i: (i, kv)),
                  pl.BlockSpec((tq, CHUNK), lambda kv, i: (i, kv)),
                  pl.BlockSpec((WINDOW, CHUNK), lambda kv, i: (jnp.maximum(i * r - 1, 0), kv)),
                  pl.BlockSpec((tq, CHUNK), lambda kv, i: (i, kv)),
                  pl.BlockSpec((WINDOW, CHUNK), lambda kv, i: (jnp.maximum(i * r - 1, 0), kv)),
                  pl.BlockSpec((1, 8, CHUNK), lambda kv, i: (kv, 0, 0))],
        out_specs=[pl.BlockSpec((tq, GROUP * HEAD_DIM), lambda kv, i: (i, kv)),
                   pl.BlockSpec((tq, gw), lambda kv, i: (i, kv))],
        compiler_params=_params(2))(q_aug, k_aug, k_aug, v_aug, v_aug, sink_rows)


def _swa_bwd(k_aug, v_aug, q_bwd, do_aug, cos, sa, sb, sink_rows, tk):
    s = k_aug.shape[0]
    r = tk // WINDOW
    nt = s // tk
    nb = s // WINDOW

    def body(k_ref, v_ref, q_ref, qn_ref, do_ref, don_ref, cos_ref, sa_ref, sb_ref, sink_ref,
             da_ref, dsink_ref, carry_ref, acc_ref):
        j = pl.program_id(0)

        @pl.when(j == 0)
        def _():
            carry_ref[...] = jnp.zeros_like(carry_ref)
            dsink_ref[...] = jnp.zeros_like(dsink_ref)

        acc_ref[...] = jnp.zeros_like(acc_ref)
        acc_ref[0:WINDOW, :] = carry_ref[...]
        dk_col, dv_col = SEG["ka"][0] - SEG["qa"][0], SEG["va"][0] - SEG["qa"][0]
        st = GROUP * WINDOW
        key = lax.broadcasted_iota(jnp.int32, (WINDOW, 2 * st), 0)
        col = lax.broadcasted_iota(jnp.int32, (WINDOW, 2 * st), 1)
        qloc = col & (WINDOW - 1)
        band = ((col < st) & (key <= qloc)) | ((col >= st) & (key > qloc))
        lane = _lane(CHUNK)
        low = lane < HEAD_DIM
        cos_t, sa_t, sb_t = cos_ref[...], sa_ref[...], sb_ref[...]
        for kv, sub in [(kv, sub) for kv in range(KV_GROUPS) for sub in range(r)]:
            heads = [slice(CHUNK * (GROUP * kv + g), CHUNK * (GROUP * kv + g + 1)) for g in range(GROUP)]
            kvs = slice(CHUNK * kv, CHUNK * (kv + 1))
            rows = slice(WINDOW * sub, WINDOW * (sub + 1))
            nxt = slice(WINDOW * (sub + 1), WINDOW * (sub + 2))
            k, v = k_ref[rows, kvs], v_ref[rows, kvs]
            q_cur, do_cur = [q_ref[rows, cs] for cs in heads], [do_ref[rows, cs] for cs in heads]
            if sub < r - 1:
                q_nxt, do_nxt, valid = [q_ref[nxt, cs] for cs in heads], [do_ref[nxt, cs] for cs in heads], band
            else:
                q_nxt, do_nxt = [qn_ref[:, cs] for cs in heads], [don_ref[:, cs] for cs in heads]
                valid = band & ((col < st) | (j < nt - 1))
            q = jnp.concatenate(q_cur + q_nxt, axis=0)
            do = jnp.concatenate(do_cur + do_nxt, axis=0)
            pt = jnp.exp(jnp.where(valid, _dot_nt(k, q), NEG))
            ds = (pt * _dot_nt(v, do)).astype(BF16)
            dv = _dot(pt.astype(BF16), do)
            dk = _dot(ds, q)
            dq = _dot_tn(ds, k)
            for g, cs in enumerate(heads):
                acc_ref[rows, cs] += dq[WINDOW * g:WINDOW * (g + 1), :]
                dqn = dq[st + WINDOW * g:st + WINDOW * (g + 1), :]
                if sub < r - 1:
                    acc_ref[nxt, cs] += dqn
                else:
                    carry_ref[:, cs] = dqn
                sink = jnp.max(sink_ref[kv, g:g + 1, :], axis=1, keepdims=True)
                p_sink = jnp.exp(sink + _lane_sum(q_cur[g].astype(F32), 67, 70))
                term = jnp.sum(p_sink * _lane_sum(do_cur[g].astype(F32), 64, 67), axis=0, keepdims=True)
                dsink_ref[kv, g:g + 1, :] += jnp.broadcast_to(term, (1, CHUNK))
            da_ref[rows, dk_col + CHUNK * kv:dk_col + CHUNK * (kv + 1)] = _rope_inv(
                jnp.where(low, dk, 0.0), cos_t[rows, :], sa_t[rows, :], sb_t[rows, :]).astype(BF16)
            da_ref[rows, dv_col + CHUNK * kv:dv_col + CHUNK * (kv + 1)] = jnp.where(low, dv, 0.0).astype(BF16)
        for pp in range(N_HEADS // 2):
            d = _pair(acc_ref[:, CHUNK * 2 * pp:CHUNK * (2 * pp + 1)], acc_ref[:, CHUNK * (2 * pp + 1):CHUNK * (2 * pp + 2)])
            da_ref[:, CHUNK * pp:CHUNK * (pp + 1)] = (_rope_inv(d, cos_t, sa_t, sb_t) * Q_SCALE).astype(BF16)

    cur = lambda w: pl.BlockSpec((tk, w), lambda j: (j, 0))
    nxt = pl.BlockSpec((WINDOW, AUG_W), lambda j: (jnp.minimum((j + 1) * r, nb - 1), 0))
    whole = pl.BlockSpec((KV_GROUPS, 8, CHUNK), lambda j: (0, 0, 0))
    return pl.pallas_call(
        body, name="swa_bwd", grid=(nt,),
        out_shape=[jax.ShapeDtypeStruct((s, 1024), BF16), jax.ShapeDtypeStruct((KV_GROUPS, 8, CHUNK), F32)],
        in_specs=[cur(KV_GROUPS * CHUNK), cur(KV_GROUPS * CHUNK), cur(AUG_W), nxt, cur(AUG_W), nxt,
                  cur(CHUNK), cur(CHUNK), cur(CHUNK), whole],
        out_specs=[cur(1024), whole],
        scratch_shapes=[pltpu.VMEM((WINDOW, AUG_W), F32), pltpu.VMEM((tk, AUG_W), F32)],
        compiler_params=_params(1))(k_aug, v_aug, q_bwd, q_bwd, do_aug, do_aug, cos, sa, sb, sink_rows)


def _fox_fwd(first_key, q_aug, k_aug, v_aug, t):
    s = q_aug.shape[0]
    pw = 2 * CHUNK

    def body(lo_ref, q_ref, k_ref, v_ref, o_ref, qb_ref):
        i = pl.program_id(1)
        first = lo_ref[pl.program_id(0), i]
        row = lax.broadcasted_iota(jnp.int32, (t, t), 0)
        col = lax.broadcasted_iota(jnp.int32, (t, t), 1)
        lane = _lane(CHUNK)
        heads = (slice(0, CHUNK), slice(CHUNK, pw))
        qs = [q_ref[:, hs] for hs in heads]

        def step(jb, carry, masked):
            rows = pl.ds(pl.multiple_of(jb * t, t), t)
            scs = [_dot_nt(qs[hh], k_ref[rows, hs]) for hh, hs in enumerate(heads)]
            if masked:
                scs = [jnp.where(col <= row, sc, NEG) for sc in scs]
            m_news = [jnp.maximum(carry[2 * hh], jnp.max(scs[hh], axis=1, keepdims=True)) for hh in range(2)]
            ps = [jnp.exp((scs[hh] - m_news[hh]).astype(BF16)) for hh in range(2)]
            new = []
            for hh, hs in enumerate(heads):
                m, acc = carry[2 * hh], carry[2 * hh + 1]
                new += [m_news[hh], jnp.exp(m - m_news[hh]) * acc + _dot(ps[hh], v_ref[rows, hs])]
            return tuple(new)

        init = (jnp.full((t, 1), NEG, F32), jnp.zeros((t, CHUNK), F32)) * 2
        carry = step(i, lax.fori_loop(first, i, lambda jb, c: step(jb, c, False), init), True)
        outs = []
        for hh, hs in enumerate(heads):
            m, acc = carry[2 * hh], carry[2 * hh + 1]
            denom = _lane_sum(acc, 64, 65)
            outs.append(acc * (1.0 / denom))
            lse = m + jnp.log(denom)
            qb_ref[:, hs] = _place3(lane, 67, -lse, qs[hh].astype(F32)).astype(BF16)
        o_ref[...] = _pair(outs[0], outs[1])

    return pl.pallas_call(
        body, name="fox_fwd", grid=(N_HEADS // 2, s // t),
        out_shape=[jax.ShapeDtypeStruct((s, 512), F32), jax.ShapeDtypeStruct((s, AUG_W), BF16)],
        in_specs=[pl.BlockSpec(memory_space=pltpu.SMEM),
                  pl.BlockSpec((t, pw), lambda p, i: (i, p)),
                  pl.BlockSpec((s, pw), lambda p, i: (0, p)),
                  pl.BlockSpec((s, pw), lambda p, i: (0, p))],
        out_specs=[pl.BlockSpec((t, CHUNK), lambda p, i: (i, p)), pl.BlockSpec((t, pw), lambda p, i: (i, p))],
        compiler_params=_params(2))(first_key, q_aug, k_aug, v_aug)


def _fox_block_ranges(bounds, norms, r):
    cum_first = bounds[0::r, 0, :N_HEADS]
    cum_last = bounds[r - 1::r, 1, :N_HEADS]
    n = cum_first.shape[0]
    reach = 2.0 * 1.02 * jnp.sqrt(norms[0, :N_HEADS] * norms[1, :N_HEADS]) + UNDERFLOW
    blk = jnp.arange(n, dtype=jnp.int32)
    decay = cum_last[None, :, :] - cum_first[:, None, :]
    skip = (decay > reach[None, None, :]) & (blk[None, :, None] < blk[:, None, None])
    first = jnp.min(jnp.where(skip, n, blk[None, :, None]), axis=1)
    first = jnp.minimum(first[:, 0::2], first[:, 1::2]).T
    needed = (first[:, :, None] <= blk[None, None, :]) & (blk[None, :, None] >= blk[None, None, :])
    last = jnp.max(jnp.where(needed, blk[None, :, None], blk[None, None, :]), axis=1)
    return first.astype(jnp.int32), last.astype(jnp.int32)


def _fox_bwd(last_query, k_aug, v_aug, q_bwd, do_aug, t):
    s = k_aug.shape[0]
    n = s // t
    pw = 2 * CHUNK

    def body(hi_ref, k_ref, v_ref, q_ref, do_ref, dk_ref, dv_ref, dck_ref, dq_ref, dcq_ref, dq_scr):
        j = pl.program_id(1)
        last = hi_ref[pl.program_id(0), j]

        @pl.when(j == 0)
        def _():
            dq_scr[...] = jnp.zeros_like(dq_scr)

        row = lax.broadcasted_iota(jnp.int32, (t, t), 0)
        col = lax.broadcasted_iota(jnp.int32, (t, t), 1)
        lane = _lane(CHUNK)
        heads = (slice(0, CHUNK), slice(CHUNK, pw))
        ks = [k_ref[:, hs] for hs in heads]
        vs = [v_ref[:, hs] for hs in heads]

        def step(ib, carry, masked):
            rows = pl.ds(pl.multiple_of(ib * t, t), t)
            new = []
            for hh, hs in enumerate(heads):
                dv, dk = carry[2 * hh], carry[2 * hh + 1]
                q, do = q_ref[rows, hs], do_ref[rows, hs]
                st = _dot_nt(ks[hh], q)
                if masked:
                    st = jnp.where(row <= col, st, NEG)
                pt = jnp.exp(st)
                ds = (pt * _dot_nt(vs[hh], do)).astype(BF16)
                new += [dv + _dot(pt.astype(BF16), do), dk + _dot(ds, q)]
                dq_scr[rows, hs] += _dot_tn(ds, ks[hh])
            return tuple(new)

        zero = jnp.zeros((t, CHUNK), F32)
        carry = lax.fori_loop(j + 1, last + 1, lambda ib, c: step(ib, c, False), step(j, (zero,) * 4, True))
        dvs, dks = (carry[0], carry[2]), (carry[1], carry[3])
        dk_ref[...] = _pair(dks[0], dks[1]).astype(BF16)
        dv_ref[...] = _pair(dvs[0], dvs[1]).astype(BF16)
        dck_ref[0] = jnp.where(lane == 0, pltpu.roll(dks[0], 64, 1),
                               jnp.where(lane == 1, pltpu.roll(dks[1], 65, 1), 0.0))

        @pl.when(j == n - 1)
        def _():
            for ib in range(n):
                rows = slice(t * ib, t * (ib + 1))
                d0, d1 = dq_scr[rows, 0:CHUNK], dq_scr[rows, CHUNK:pw]
                dq_ref[rows, :] = (_pair(d0, d1) * Q_SCALE).astype(BF16)
                dcq_ref[0, rows, :] = jnp.where(lane == 0, pltpu.roll(d0, CHUNK - 67, 1),
                                                jnp.where(lane == 1, pltpu.roll(d1, CHUNK - 66, 1), 0.0))

    return pl.pallas_call(
        body, name="fox_bwd", grid=(N_HEADS // 2, n),
        out_shape=[jax.ShapeDtypeStruct((s, 512), BF16), jax.ShapeDtypeStruct((s, 512), BF16),
                   jax.ShapeDtypeStruct((N_HEADS // 2, s, CHUNK), F32), jax.ShapeDtypeStruct((s, 512), BF16),
                   jax.ShapeDtypeStruct((N_HEADS // 2, s, CHUNK), F32)],
        in_specs=[pl.BlockSpec(memory_space=pltpu.SMEM),
                  pl.BlockSpec((t, pw), lambda p, j: (j, p)), pl.BlockSpec((t, pw), lambda p, j: (j, p)),
                  pl.BlockSpec((s, pw), lambda p, j: (0, p)), pl.BlockSpec((s, pw), lambda p, j: (0, p))],
        out_specs=[pl.BlockSpec((t, CHUNK), lambda p, j: (j, p)), pl.BlockSpec((t, CHUNK), lambda p, j: (j, p)),
                   pl.BlockSpec((1, t, CHUNK), lambda p, j: (p, j, 0)), pl.BlockSpec((s, CHUNK), lambda p, j: (0, p)),
                   pl.BlockSpec((1, s, CHUNK), lambda p, j: (p, 0, 0))],
        scratch_shapes=[pltpu.VMEM((s, pw), F32)],
        compiler_params=_params(2))(last_query, k_aug, v_aug, q_bwd, do_aug)


def _fgate_bwd(dcq, dck, xf, tm):
    s = xf.shape[0]
    nt = s // tm

    def body(dq_ref, dc_ref, xf_ref, df_ref, dbf_ref, carry_ref):
        @pl.when(pl.program_id(0) == 0)
        def _():
            carry_ref[...] = jnp.zeros_like(carry_ref)
            dbf_ref[...] = jnp.zeros_like(dbf_ref)

        row = lax.broadcasted_iota(jnp.int32, (tm, tm), 0)
        col = lax.broadcasted_iota(jnp.int32, (tm, tm), 1)
        tri = jnp.where(col >= row, 1.0, 0.0).astype(BF16)
        dcum = dq_ref[0] - dc_ref[0]
        for p in range(1, N_HEADS // 2):
            dcum = dcum + pltpu.roll(dq_ref[p] - dc_ref[p], 2 * p, 1)
        hi, mid, lo = _split3(dcum)
        dlogf = _dot(tri, hi.astype(BF16)) + _dot(tri, mid.astype(BF16)) + _dot(tri, lo.astype(BF16))
        dlogf = dlogf + carry_ref[0:1, :]
        carry_ref[0:1, :] += jnp.sum(dcum, axis=0, keepdims=True)
        df = dlogf / (1.0 + jnp.exp(xf_ref[...]))
        df_ref[...] = df.astype(BF16)
        dbf_ref[0:1, :] += jnp.sum(df, axis=0, keepdims=True)

    rev = pl.BlockSpec((tm, CHUNK), lambda i: (nt - 1 - i, 0))
    rev4 = pl.BlockSpec((N_HEADS // 2, tm, CHUNK), lambda i: (0, nt - 1 - i, 0))
    return pl.pallas_call(
        body, name="fgate_bwd", grid=(nt,),
        out_shape=[jax.ShapeDtypeStruct((s, CHUNK), BF16), jax.ShapeDtypeStruct((8, CHUNK), F32)],
        in_specs=[rev4, rev4, rev], out_specs=[rev, pl.BlockSpec((8, CHUNK), lambda i: (0, 0))],
        scratch_shapes=[pltpu.VMEM((8, CHUNK), F32)],
        compiler_params=_params(1))(dcq, dck, xf)


def _post(x, tgt, att_a, att_b, za, zb, ga, gb, woa_t, wob_t, wout, vec, ind, emat, tm):
    s = x.shape[0]

    def body(x_ref, t_ref, aa_ref, ab_ref, za_ref, zb_ref, ga_ref, gb_ref, woat_ref, wobt_ref, wout_ref,
             vec_ref, ind_ref, e_ref,
             dx1_ref, ua_ref, ub_ref, mg_ref, do_ref, dya_ref, dyb_ref, doa_ref, dob_ref, dzg_ref, acc_ref):
        @pl.when(pl.program_id(0) == 0)
        def _():
            acc_ref[...] = jnp.zeros_like(acc_ref)

        gate, gfin = vec_ref[0:1, :], vec_ref[1:2, :]
        inv_d = 1.0 / D_MODEL

        def branch_fwd(att_ref, z_ref, w_ref, u_ref):
            att, z = att_ref[...], z_ref[...]
            sz = _sigmoid(z)
            silu = z * sz
            u = (att * silu).astype(BF16)
            u_ref[...] = u.T
            return att, z, sz, silu, _dot_nt(u, w_ref[...])

        att_a, z_a, sz_a, silu_a, y_a = branch_fwd(aa_ref, za_ref, woat_ref, ua_ref)
        att_b, z_b, sz_b, silu_b, y_b = branch_fwd(ab_ref, zb_ref, wobt_ref, ub_ref)
        sg_a, sg_b = _sigmoid(ga_ref[...]), _sigmoid(gb_ref[...])
        merged = (sg_a * y_a + sg_b * y_b).astype(BF16)
        mg_ref[...] = merged.T
        o = _dot(merged, wout_ref[...])
        x1 = x_ref[...] + gate * o
        rstd = lax.rsqrt(jnp.mean(x1 * x1, axis=-1, keepdims=True) + NORM_EPS)
        xh = x1 * rstd
        diff = xh * gfin - t_ref[...]
        acc_ref[2:3, :] += (0.5 * inv_d) * jnp.sum(diff * diff, axis=0, keepdims=True)
        dy = diff * inv_d
        acc_ref[1:2, :] += jnp.sum(dy * xh, axis=0, keepdims=True)
        dyg = dy * gfin
        dx1 = rstd * (dyg - xh * jnp.mean(dyg * xh, axis=-1, keepdims=True))
        dx1_ref[...] = dx1
        acc_ref[0:1, :] += jnp.sum(dx1 * o, axis=0, keepdims=True)
        d_o = (dx1 * gate).astype(BF16)
        do_ref[...] = d_o
        dmg = _dot_nt(d_o, wout_ref[...])

        lane = _lane(CHUNK)
        low = lane < HEAD_DIM
        zg0 = SEG["za"][0]

        def branch_bwd(sg, y, att, z, sz, silu, wt_ref, dy_ref, g_name, z_name, doaug_ref):
            dyb = (dmg * sg).astype(BF16)
            dy_ref[...] = dyb
            g_off, z_off = SEG[g_name][0] - zg0, SEG[z_name][0] - zg0
            dzg_ref[:, g_off:g_off + D_MODEL] = (dmg * y * sg * (1.0 - sg)).astype(BF16)
            du = _dot(dyb, wt_ref[...])
            datt = du * silu
            dzg_ref[:, z_off:z_off + 512] = (du * att * (sz * (1.0 + z * (1.0 - sz)))).astype(BF16)
            extra = _spread3(-_dot_split(datt * att, ind_ref[...]), e_ref)
            for p in range(N_HEADS // 2):
                c = datt[:, CHUNK * p:CHUNK * (p + 1)]
                for hh in range(2):
                    hd = 2 * p + hh
                    src = c if hh == 0 else pltpu.roll(c, HEAD_DIM, 1)
                    doaug_ref[:, CHUNK * hd:CHUNK * (hd + 1)] = jnp.where(
                        low, src, extra[:, CHUNK * hd:CHUNK * (hd + 1)]).astype(BF16)

        branch_bwd(sg_a, y_a, att_a, z_a, sz_a, silu_a, woat_ref, dya_ref, "ga", "za", doa_ref)
        branch_bwd(sg_b, y_b, att_b, z_b, sz_b, silu_b, wobt_ref, dyb_ref, "gb", "zb", dob_ref)

    row_blk = lambda w: pl.BlockSpec((tm, w), lambda i: (i, 0))
    full = lambda a: pl.BlockSpec(a.shape, lambda i: (0,) * a.ndim)
    sds = lambda w, dt: jax.ShapeDtypeStruct((s, w), dt)
    tds = lambda w: jax.ShapeDtypeStruct((w, s), BF16)
    col_blk = lambda w: pl.BlockSpec((w, tm), lambda i: (0, i))
    out_shape = [sds(D_MODEL, F32), tds(512), tds(512), tds(D_MODEL), sds(D_MODEL, BF16),
                 sds(D_MODEL, BF16), sds(D_MODEL, BF16), sds(AUG_W, BF16), sds(AUG_W, BF16), sds(3072, BF16),
                 jax.ShapeDtypeStruct((8, D_MODEL), F32)]
    ins = [x, tgt, att_a, att_b, za, zb, ga, gb, woa_t, wob_t, wout, vec, ind, emat]
    in_specs = [row_blk(a.shape[1]) for a in ins[:8]] + [full(a) for a in ins[8:]]
    out_specs = ([row_blk(D_MODEL), col_blk(512), col_blk(512), col_blk(D_MODEL)]
                 + [row_blk(o.shape[1]) for o in out_shape[4:-1]] + [pl.BlockSpec((8, D_MODEL), lambda i: (0, 0))])
    return pl.pallas_call(body, name="post", grid=(s // tm,), out_shape=out_shape, in_specs=in_specs,
                          out_specs=out_specs, compiler_params=_params(1))(*ins)


def _bwd_pre(dproj, w_all, x, dx1, mod, tm, chip_halves, modes):
    s = x.shape[0]
    ng, nh = len(DPROJ_GROUPS), len(chip_halves)
    last = s // tm - 1

    def body(*refs):
        dp_refs = refs[:ng]
        wt_ref, x_ref, dx1_ref, mod_ref = refs[ng:ng + 4]
        send_refs = refs[ng + 4:ng + 4 + nh]
        gx_ref, acc_ref = refs[ng + 4 + nh:ng + 6 + nh]
        recv_refs = refs[ng + 6 + nh:ng + 6 + 2 * nh]
        sems = refs[ng + 6 + 2 * nh:]

        @pl.when(pl.program_id(0) == 0)
        def _():
            acc_ref[...] = jnp.zeros_like(acc_ref)
            _exchange_start(_exchange_copies(send_refs, recv_refs, modes, *sems))

        dh = None
        for dp_ref, (_, off, w) in zip(dp_refs, DPROJ_GROUPS):
            part = _dot_nt(dp_ref[...], wt_ref[:, off:off + w])
            dh = part if dh is None else dh + part
        xv = x_ref[...]
        rstd = lax.rsqrt(jnp.mean(xv * xv, axis=-1, keepdims=True) + NORM_EPS)
        xh = xv * rstd
        gn = mod_ref[0:1, :]
        acc_ref[0:1, :] += jnp.sum(dh, axis=0, keepdims=True)
        acc_ref[1:2, :] += jnp.sum(dh * (xh * gn), axis=0, keepdims=True)
        dn = dh * (1.0 + mod_ref[1:2, :])
        acc_ref[2:3, :] += jnp.sum(dn * xh, axis=0, keepdims=True)
        dxh = dn * gn
        gx_ref[...] = dx1_ref[...] + rstd * (dxh - xh * jnp.mean(dxh * xh, axis=-1, keepdims=True))

        @pl.when(pl.program_id(0) == last)
        def _():
            _exchange_wait(_exchange_copies(send_refs, recv_refs, modes, *sems))

    row_blk = lambda w: pl.BlockSpec((tm, w), lambda i: (i, 0))
    full = lambda a: pl.BlockSpec(a.shape, lambda i: (0,) * a.ndim)
    outs = pl.pallas_call(
        body, name="bwd_pre", grid=(s // tm,),
        out_shape=[jax.ShapeDtypeStruct((s, D_MODEL), F32), jax.ShapeDtypeStruct((8, D_MODEL), F32)]
        + _exchange_out_shapes(chip_halves, modes),
        in_specs=[row_blk(w) for _, _, w in DPROJ_GROUPS] + [_resident(w_all), row_blk(D_MODEL), row_blk(D_MODEL), full(mod)]
        + [ANY_SPEC] * nh,
        out_specs=[row_blk(D_MODEL), pl.BlockSpec((8, D_MODEL), lambda i: (0, 0))] + [ANY_SPEC] * nh,
        scratch_shapes=_exchange_sems(nh),
        compiler_params=_params(1))(*dproj, w_all, x, dx1, mod, *chip_halves)
    return outs[0], outs[1], outs[2:]


def _accumulate_tokens(a, b_ref, acc_ref, o_ref, last):
    @pl.when(pl.program_id(0) == 0)
    def _():
        acc_ref[...] = jnp.zeros_like(acc_ref)

    acc_ref[...] += _dot(a, b_ref[...])

    @pl.when(pl.program_id(0) == last)
    def _():
        o_ref[...] = acc_ref[...].astype(BF16)


def _token_matmul_multi(a_t, bs, tk, name):
    m, s = a_t.shape
    nb = len(bs)
    last = s // tk - 1

    def body(*refs):
        a_ref, b_refs, o_refs, acc_refs = refs[0], refs[1:1 + nb], refs[1 + nb:1 + 2 * nb], refs[1 + 2 * nb:]
        a = a_ref[...]
        for b_ref, o_ref, acc_ref in zip(b_refs, o_refs, acc_refs):
            _accumulate_tokens(a, b_ref, acc_ref, o_ref, last)

    return pl.pallas_call(
        body, name=name, grid=(s // tk,),
        out_shape=[jax.ShapeDtypeStruct((m, b.shape[1]), BF16) for b in bs],
        in_specs=[pl.BlockSpec((m, tk), lambda k: (0, k))] + [pl.BlockSpec((tk, b.shape[1]), lambda k: (k, 0)) for b in bs],
        out_specs=[pl.BlockSpec((m, b.shape[1]), lambda k: (0, 0)) for b in bs],
        scratch_shapes=[pltpu.VMEM((m, b.shape[1]), F32) for b in bs],
        compiler_params=_params(1))(a_t, *bs)


def _adam_math(g, w, m, v):
    m2 = ADAM_B1 * m + (1.0 - ADAM_B1) * g
    v2 = ADAM_B2 * v + (1.0 - ADAM_B2) * (g * g)
    delta = -ADAM_LR * ((m2 / ADAM_C1) / (jnp.sqrt(v2 / ADAM_C2) + ADAM_EPS) + ADAM_WD * w)
    return delta, m2, v2


def _adamw_parts(full, parts, others, w, m, v, tr, name):
    _, rws, cols = w.shape

    def body(oth_ref, f_ref, p_ref, w_ref, m_ref, v_ref, g_ref, d_ref, m2_ref, v2_ref):
        g = None
        for chip in range(N_CHIPS):
            part = (1.0 - oth_ref[chip]) * f_ref[chip] + oth_ref[chip] * p_ref[chip].astype(F32)
            g = part if g is None else g + part
        g_ref[0] = g
        d_ref[0], m2_ref[0], v2_ref[0] = _adam_math(g, w_ref[0], m_ref[0], v_ref[0])

    blk = pl.BlockSpec((1, tr, cols), lambda i: (0, i, 0))
    slots = pl.BlockSpec((N_CHIPS, tr, cols), lambda i: (0, i, 0))
    o = jax.ShapeDtypeStruct((1, rws, cols), F32)
    return pl.pallas_call(
        body, name=name, grid=(rws // tr,), out_shape=[o, o, o, o],
        in_specs=[pl.BlockSpec(memory_space=pltpu.SMEM), slots, slots, blk, blk, blk],
        out_specs=[blk, blk, blk, blk], compiler_params=_params(1))(others, full, parts, w, m, v)


def _adamw_devices(parts, w, m, v, tr, name):
    _, rws, cols = w.shape

    def body(p_ref, w_ref, m_ref, v_ref, g_ref, d_ref, m2_ref, v2_ref):
        g = p_ref[0].astype(F32)
        for dev in range(1, N_DEV):
            g = g + p_ref[dev].astype(F32)
        g_ref[0] = g
        d_ref[0], m2_ref[0], v2_ref[0] = _adam_math(g, w_ref[0], m_ref[0], v_ref[0])

    blk = pl.BlockSpec((1, tr, cols), lambda i: (0, i, 0))
    o = jax.ShapeDtypeStruct((1, rws, cols), F32)
    return pl.pallas_call(
        body, name=name, grid=(rws // tr,), out_shape=[o, o, o, o],
        in_specs=[pl.BlockSpec((N_DEV, tr, cols), lambda i: (0, i, 0)), blk, blk, blk],
        out_specs=[blk, blk, blk, blk], compiler_params=_params(1))(parts, w, m, v)


def _adamw_ada(c_lanes, d_rows, w, m, v, tr):
    _, rws, cols = w.shape

    def body(c_ref, d_ref, w_ref, m_ref, v_ref, g_ref, dl_ref, m2_ref, v2_ref):
        for k in range(cols // CHUNK):
            cs = slice(CHUNK * k, CHUNK * (k + 1))
            g = c_ref[0] * d_ref[0:1, cs]
            for b in range(1, N_DEV):
                g = g + c_ref[b] * d_ref[b:b + 1, cs]
            g_ref[0, :, cs] = g
            dl_ref[0, :, cs], m2_ref[0, :, cs], v2_ref[0, :, cs] = _adam_math(
                g, w_ref[0, :, cs], m_ref[0, :, cs], v_ref[0, :, cs])

    blk = pl.BlockSpec((1, tr, cols), lambda i: (0, i, 0))
    o = jax.ShapeDtypeStruct((1, rws, cols), F32)
    return pl.pallas_call(
        body, name="adamw_ada", grid=(rws // tr,), out_shape=[o, o, o, o],
        in_specs=[pl.BlockSpec((N_DEV, tr, CHUNK), lambda i: (0, i, 0)), pl.BlockSpec((N_DEV, cols), lambda i: (0, 0)),
                  blk, blk, blk],
        out_specs=[blk, blk, blk, blk], compiler_params=_params(1))(c_lanes, d_rows, w, m, v)


SMALL = (("b_ada", 0, 0, 3 * D_MODEL), ("g_norm", 1, 0, D_MODEL), ("b_f", 3, 0, N_HEADS), ("sinks", 4, 0, N_HEADS),
         ("g_final", 1, D_MODEL, D_MODEL))
LOSS_ROW = 2


def _pack_small(pre_acc, post_acc, dbf_acc, dsink):
    def body(pre_ref, post_ref, dbf_ref, dsink_ref, o_ref):
        o_ref[...] = jnp.zeros_like(o_ref)
        d = D_MODEL
        o_ref[0:1, 0:d], o_ref[0:1, d:2 * d], o_ref[0:1, 2 * d:3 * d] = pre_ref[0:1, :], pre_ref[1:2, :], post_ref[0:1, :]
        o_ref[1:2, 0:d], o_ref[1:2, d:2 * d] = pre_ref[2:3, :], post_ref[1:2, :]
        o_ref[LOSS_ROW:LOSS_ROW + 1, 0:d] = post_ref[2:3, :]
        o_ref[3:4, 0:CHUNK] = dbf_ref[0:1, :]
        lane = _lane(CHUNK)
        sinks = jnp.zeros((1, CHUNK), F32)
        for kv in range(KV_GROUPS):
            for g in range(GROUP):
                sinks = jnp.where(lane == GROUP * kv + g, dsink_ref[kv, g:g + 1, :], sinks)
        o_ref[4:5, 0:CHUNK] = sinks

    return pl.pallas_call(body, name="pack_small", out_shape=jax.ShapeDtypeStruct((8, 3 * D_MODEL), F32),
                          compiler_params=_params(0))(pre_acc, post_acc, dbf_acc, dsink)


def _adamw_small(packs, params):
    n = len(SMALL)

    def body(*refs):
        p_ref, wmv = refs[0], refs[1:1 + 3 * n]
        outs, loss_ref, tot_ref = refs[1 + 3 * n:1 + 7 * n], refs[1 + 7 * n], refs[2 + 7 * n]
        tot = p_ref[0]
        for dev in range(1, N_DEV):
            tot = tot + p_ref[dev]
        tot_ref[...] = tot
        for i, (_, row, lo, w) in enumerate(SMALL):
            g = tot_ref[row:row + 1, lo:lo + w]
            o = outs[4 * i:4 * i + 4]
            o[0][...] = g
            o[1][...], o[2][...], o[3][...] = _adam_math(g, wmv[3 * i][...], wmv[3 * i + 1][...], wmv[3 * i + 2][...])
        loss_ref[...] = jnp.broadcast_to(jnp.sum(tot_ref[LOSS_ROW:LOSS_ROW + 1, 0:D_MODEL], axis=1, keepdims=True),
                                         loss_ref.shape)

    flat = [a for wmv in params for a in wmv]
    out_shape = [jax.ShapeDtypeStruct(wmv[0].shape, F32) for wmv in params for _ in range(4)]
    res = pl.pallas_call(body, name="adamw_small", out_shape=out_shape + [jax.ShapeDtypeStruct((8, CHUNK), F32)],
                         scratch_shapes=[pltpu.VMEM((8, 3 * D_MODEL), F32)],
                         compiler_params=_params(0))(packs, *flat)
    return [tuple(res[4 * i:4 * i + 4]) for i in range(n)], res[4 * n]


def _tile(s, want):
    return min(s, want)


def _layout_pieces():
    orig = {"qa": 0, "ka": 512, "va": 640, "za": 768, "qb": 1280, "kb": 1792, "vb": 2304, "f": 2816, "zb": 2824,
            "ga": 3336, "gb": 4360}
    pieces = []
    for name, (off, w) in SEG.items():
        if name in ("ka", "va"):
            pieces += [(orig[name] + HEAD_DIM * kv, orig[name] + HEAD_DIM * (kv + 1), off + CHUNK * kv) for kv in range(KV_GROUPS)]
        else:
            pieces.append((orig[name], orig[name] + (N_HEADS if name == "f" else w), off))
    return pieces


def _assemble_w_all(win_g):
    cols, pos = [], 0
    for lo, hi, new in sorted(_layout_pieces(), key=lambda t: t[2]):
        if new > pos:
            cols.append(jnp.zeros((D_MODEL, new - pos), win_g.dtype))
        col = lo
        while col < hi:
            dev = col // IN_SHARD
            end = min(hi, (dev + 1) * IN_SHARD)
            cols.append(win_g[dev, :, col - dev * IN_SHARD:end - dev * IN_SHARD])
            col = end
        pos = new + hi - lo
    cols.append(jnp.zeros((D_MODEL, PROJ_W - pos), win_g.dtype))
    return jnp.concatenate(cols, axis=1)


def _grad_slot(dw_groups, dev):
    lo_d, hi_d = dev * IN_SHARD, (dev + 1) * IN_SHARD
    cols = []
    for lo, hi, new in sorted(_layout_pieces()):
        a, b = max(lo, lo_d), min(hi, hi_d)
        if a < b:
            arr, off = next((g, o) for g, (_, o, w) in zip(dw_groups, DPROJ_GROUPS) if o <= new < o + w)
            cols.append(arr[:, new - off + a - lo:new - off + b - lo])
    return jnp.concatenate(cols, axis=1)


def kernel(x, c, positions, w_ada, b_ada, g_norm, w_in, b_f, sinks, w_o_swa, w_o_fox, w_out, g_final, loss_target, m_w_ada, m_b_ada, m_g_norm, m_w_in, m_b_f, m_sinks, m_w_o_swa, m_w_o_fox, m_w_out, m_g_final, v_w_ada, v_b_ada, v_g_norm, v_w_in, v_b_f, v_sinks, v_w_o_swa, v_w_o_fox, v_w_out, v_g_final):
    s = x.shape[1]
    tm = _tile(s, 256)
    ta = _tile(s, 512)
    me = 4 * lax.axis_index("x") + 2 * lax.axis_index("y") + lax.axis_index("c")
    x2, tgt = x[0], loss_target[0]

    inv_freq = np.power(np.float32(ROPE_THETA), -np.arange(0, HEAD_DIM, 2, dtype=np.float32) / HEAD_DIM)
    inv_freq = jnp.asarray(np.tile(inv_freq, CHUNK // (HEAD_DIM // 2))[None, :], F32)
    (cos, sa, sb), (win_g,), c_all = _rope_tables_and_gather(
        positions.reshape(s, 1).astype(F32), inv_freq, tm, [w_in[0].astype(BF16)], jnp.broadcast_to(c, (8, D_MODEL)))
    c_all = c_all[:, 0, :]
    b_shard = lax.dynamic_slice(b_ada, (0, me * ADA_SHARD), (1, ADA_SHARD))
    ada_part = _ada_fwd(c_all, w_ada[0], b_shard)
    (ada_g,) = _exchange([ada_part], ["gather"], "gather_ada")
    ada = lax.dynamic_index_in_dim(ada_g, me, axis=1, keepdims=False).reshape(3 * D_MODEL)
    shift, scale, gate = ada[:D_MODEL], ada[D_MODEL:2 * D_MODEL], ada[2 * D_MODEL:]
    w_all = _assemble_w_all(win_g)

    zrow = jnp.zeros((1, D_MODEL), F32)
    mod = jnp.concatenate([g_norm, scale[None], shift[None], zrow, zrow, zrow, zrow, zrow], axis=0)
    bf_row = jnp.pad(b_f, ((0, 0), (0, CHUNK - N_HEADS)))
    emat_np = np.zeros((3 * CHUNK, AUG_W), np.float32)
    ind_np = np.zeros((512, CHUNK), np.float32)
    for hd in range(N_HEADS):
        for part in range(3):
            emat_np[CHUNK * part + hd, CHUNK * hd + 64 + part] = 1.0
        ind_np[HEAD_DIM * hd:HEAD_DIM * (hd + 1), hd] = 1.0
    emat, ind = jnp.asarray(emat_np, BF16), jnp.asarray(ind_np, BF16)
    sink_rows = jnp.broadcast_to(jnp.pad(sinks.reshape(KV_GROUPS, GROUP), ((0, 0), (0, 8 - GROUP)))[:, :, None],
                                 (KV_GROUPS, 8, CHUNK))

    (h_t, qa, ka, va, za, qb, kb, vb, zb, ga, gb, xf, bounds, norms), (woa_g, wob_g, wout_g) = _fwd_proj(
        x2, mod, w_all, cos, sa, sb, bf_row, emat, ind, ta,
        [w_o_swa[0].T.astype(BF16), w_o_fox[0].T.astype(BF16), w_out[0].astype(BF16)])
    woa_t, wob_t, wout = woa_g.reshape(D_MODEL, 512), wob_g.reshape(D_MODEL, 512), wout_g.reshape(D_MODEL, D_MODEL)
    first_key, last_query = _fox_block_ranges(bounds, norms, 1)
    att_a, qa_bwd = _swa_fwd(qa, ka, va, sink_rows, _tile(s, 1024))
    att_b, qb_bwd = _fox_fwd(first_key, qb, kb, vb, ta)
    vec = jnp.concatenate([gate[None], g_final[None], zrow, zrow, zrow, zrow, zrow, zrow], axis=0)
    (dx1, ua_t, ub_t, merged_t, d_o, dya, dyb, doa, dob, dzg, post_acc) = _post(
        x2, tgt, att_a, att_b, za, zb, ga, gb, woa_t, wob_t, wout, vec, ind, emat, tm)

    da, dsink = _swa_bwd(ka, va, qa_bwd, doa, cos, sa, sb, sink_rows, ta)
    dkb, dvb, dck, dqb, dcq = _fox_bwd(last_query, kb, vb, qb_bwd, dob, ta)
    dfb, dbf_acc = _fgate_bwd(dcq, dck, xf, ta)
    dproj = [da, dqb, dkb, dvb, dzg, dfb]
    dw_zg, = _token_matmul_multi(h_t, [dzg], ta, "dw_in_zg")
    dw_a, dw_qb, dw_kb, dw_vb, dw_f = _token_matmul_multi(h_t, [da, dqb, dkb, dvb, dfb], ta, "dw_in_rest")
    dw_all = [dw_a, dw_qb, dw_kb, dw_vb, dw_zg, dw_f]

    core, chip = lax.axis_index("c"), 2 * lax.axis_index("x") + lax.axis_index("y")
    win_slots = jnp.stack([jnp.stack([_grad_slot(dw_all, 2 * ch + co) for ch in range(N_CHIPS)]) for co in range(2)])
    (dwoa, dwob, dwout), (win_theirs,) = _token_matmuls_and_swap(
        [(ua_t, dya), (ub_t, dyb), (merged_t, d_o)], ta, [win_slots])
    win_full, win_half = _chip_partial(jnp.reshape(core, (1,)).astype(jnp.int32), win_slots, win_theirs, 256,
                                       "chip_partial_w_in")
    col_slots = lambda g: g.reshape(512, N_DEV, 128).transpose(1, 0, 2)
    grad_x, pre_acc, (p_win, p_woa, p_wob, p_wout) = _bwd_pre(
        dproj, w_all, x2, dx1, mod, ta,
        [win_half, col_slots(dwoa), col_slots(dwob), dwout.reshape(N_DEV, 128, D_MODEL)],
        ["chips", "devices", "devices", "devices"])
    (packs,) = _exchange([_pack_small(pre_acc, post_acc, dbf_acc, dsink)], ["gather"], "gather_small_grads")
    others = jnp.where(jnp.arange(N_CHIPS) == chip, 0.0, 1.0).astype(F32)

    g_win, d_win, m_win, v_win = _adamw_parts(win_full, p_win, others, w_in, m_w_in, v_w_in, 128, "adamw_w_in")
    g_woa, d_woa, m_woa, v_woa = _adamw_devices(p_woa, w_o_swa, m_w_o_swa, v_w_o_swa, 128, "adamw_w_o_swa")
    g_wob, d_wob, m_wob, v_wob = _adamw_devices(p_wob, w_o_fox, m_w_o_fox, v_w_o_fox, 128, "adamw_w_o_fox")
    g_wout, d_wout, m_wout, v_wout = _adamw_devices(p_wout, w_out, m_w_out, v_w_out, 128, "adamw_w_out")
    d_ada_rows = lax.dynamic_slice(packs[:, 0, :], (0, me * ADA_SHARD), (N_DEV, ADA_SHARD))
    c_lanes = jnp.broadcast_to(c_all[:, :, None], (N_DEV, D_MODEL, CHUNK))
    g_wada, d_wada, m_wada, v_wada = _adamw_ada(c_lanes, d_ada_rows, w_ada, m_w_ada, v_w_ada, 256)

    row = lambda a: a.reshape(1, D_MODEL)
    small, loss_rows = _adamw_small(packs, [(b_ada, m_b_ada, v_b_ada), (g_norm, m_g_norm, v_g_norm), (b_f, m_b_f, v_b_f),
                                            (sinks, m_sinks, v_sinks), (row(g_final), row(m_g_final), row(v_g_final))])
    bada_o, gn_o, bf_o, sk_o, gf_o = small

    outs = []
    for k, big in enumerate(((g_wada, g_win, g_woa, g_wob, g_wout), (d_wada, d_win, d_woa, d_wob, d_wout),
                             (m_wada, m_win, m_woa, m_wob, m_wout), (v_wada, v_win, v_woa, v_wob, v_wout))):
        wada_o, win_o, woa_o, wob_o, wout_o = big
        outs += [wada_o, bada_o[k], gn_o[k], win_o, bf_o[k], sk_o[k], woa_o, wob_o, wout_o, gf_o[k].reshape(D_MODEL)]
    return (loss_rows[0, 0], grad_x[None], *outs)
```

```python
import numpy as np
import jax
import jax.numpy as jnp
from jax import lax
from jax.experimental import pallas as pl
from jax.experimental.pallas import tpu as pltpu

F32 = jnp.float32
BF16 = jnp.bfloat16

D_MODEL = 1024
HEAD_DIM = 64
N_HEADS = 8
KV_GROUPS = 2
GROUP = N_HEADS // KV_GROUPS
WINDOW = 128
CHUNK = 128
AUG_W = N_HEADS * CHUNK
N_DEV = 8
IN_SHARD = 673
ADA_SHARD = 384
NORM_EPS = 1e-6
ROPE_THETA = 10000.0
Q_SCALE = HEAD_DIM ** -0.5
NEG = -1e30
UNDERFLOW = 105.0

ADAM_LR = 0.001
ADAM_B1 = 0.9
ADAM_B2 = 0.999
ADAM_EPS = 1e-08
ADAM_WD = 0.01
ADAM_STEP = 10
ADAM_C1 = 1.0 - ADAM_B1 ** ADAM_STEP
ADAM_C2 = 1.0 - ADAM_B2 ** ADAM_STEP

SEG = {}
_off = 0
for _name, _w in (("qa", 512), ("ka", 256), ("va", 256), ("qb", 512), ("kb", 512), ("vb", 512),
                  ("za", 512), ("zb", 512), ("ga", 1024), ("gb", 1024), ("f", 128)):
    SEG[_name] = (_off, _w)
    _off += _w
PROJ_W = _off
DPROJ_GROUPS = (("a", SEG["qa"][0], 1024), ("qb", SEG["qb"][0], 512), ("kb", SEG["kb"][0], 512),
                ("vb", SEG["vb"][0], 512), ("zg", SEG["za"][0], 3072), ("f", SEG["f"][0], 128))

VMEM_LIMIT = 56 * 1024 * 1024


def _params(n_axes):
    return pltpu.CompilerParams(dimension_semantics=("arbitrary",) * n_axes, vmem_limit_bytes=VMEM_LIMIT)


def _resident(a):
    return pl.BlockSpec(a.shape, lambda i: (0, 0), pipeline_mode=pl.Buffered(1))


def _dot(a, b):
    return jnp.dot(a, b, preferred_element_type=F32)


def _dot_nt(a, b):
    return lax.dot_general(a, b, (((1,), (1,)), ((), ())), preferred_element_type=F32)


def _dot_tn(a, b):
    return lax.dot_general(a, b, (((0,), (0,)), ((), ())), preferred_element_type=F32)


def _lane(n):
    return lax.broadcasted_iota(jnp.int32, (1, n), 1)


def _split3(x):
    hi = x.astype(BF16).astype(F32)
    r = x - hi
    mid = r.astype(BF16).astype(F32)
    lo = (r - mid).astype(BF16).astype(F32)
    return hi, mid, lo


def _dot_split(x, b):
    hi, mid, lo = _split3(x)
    return _dot(hi.astype(BF16), b) + _dot(mid.astype(BF16), b) + _dot(lo.astype(BF16), b)


def _spread3(x, e3_ref):
    return _dot(jnp.concatenate(_split3(x), axis=1).astype(BF16), e3_ref[...])


def _place3(lane, base, x, other):
    hi, mid, lo = _split3(x)
    return jnp.where(lane == base, hi, jnp.where(lane == base + 1, mid, jnp.where(lane == base + 2, lo, other)))


def _lane_sum(x, lo, hi):
    lane = _lane(x.shape[1])
    return jnp.sum(jnp.where((lane >= lo) & (lane < hi), x, 0.0), axis=1, keepdims=True)


def _ones_lanes(lo, hi):
    lane = _lane(CHUNK)
    return jnp.where((lane >= lo) & (lane < hi), 1.0, 0.0).astype(F32)


def _rope(c, cos, sa, sb):
    return c * cos + pltpu.roll(c, CHUNK - 32, 1) * sa + pltpu.roll(c, 32, 1) * sb


def _rope_inv(d, cos, sa, sb):
    return d * cos - (pltpu.roll(d, CHUNK - 32, 1) * sa + pltpu.roll(d, 32, 1) * sb)


def _pair(c0, c1):
    return jnp.where(_lane(CHUNK) < HEAD_DIM, c0, pltpu.roll(c1, HEAD_DIM, 1))


def _sigmoid(x):
    return 0.5 * jnp.tanh(0.5 * x) + 0.5


N_CHIPS = 4
ANY_SPEC = pl.BlockSpec(memory_space=pl.ANY)


def _exchange(arrays, modes, name):
    n = len(arrays)

    def body(*refs):
        copies = _exchange_copies(refs[:n], refs[n:2 * n], modes, *refs[2 * n:])
        _exchange_start(copies)
        _exchange_wait(copies)

    return pl.pallas_call(
        body, name=name, out_shape=_exchange_out_shapes(arrays, modes),
        in_specs=[ANY_SPEC] * n, out_specs=[ANY_SPEC] * n, scratch_shapes=_exchange_sems(n),
    )(*arrays)


def _exchange_out_shapes(arrays, modes):
    return [jax.ShapeDtypeStruct((N_DEV,) + a.shape if md == "gather" else a.shape, a.dtype)
            for a, md in zip(arrays, modes)]


def _exchange_sems(n):
    return [pltpu.SemaphoreType.DMA((n, N_DEV - 1)), pltpu.SemaphoreType.DMA((n, N_DEV - 1)),
            pltpu.SemaphoreType.DMA((n,))]


def _exchange_copies(srcs, dsts, modes, send_sems, recv_sems, loc_sems):
    n = len(srcs)
    x, y, c = lax.axis_index("x"), lax.axis_index("y"), lax.axis_index("c")

    def slot(i, px, py, pc):
        return 2 * px + py if modes[i] == "chips" else 4 * px + 2 * py + pc

    def src_of(i, px, py, pc):
        return srcs[i] if modes[i] == "gather" else srcs[i].at[slot(i, px, py, pc)]

    local = [pltpu.make_async_copy(src_of(i, x, y, c), dsts[i].at[slot(i, x, y, c)], loc_sems.at[i])
             for i in range(n)]
    sends, recvs = [], []
    for r in (1, 2, 4, 3, 5, 6, 7):
        px = 1 - x if r & 4 else x
        py = 1 - y if r & 2 else y
        pc = 1 - c if r & 1 else c
        for i in range(n):
            if modes[i] == "chips" and r & 1:
                continue
            sems = dict(send_sem=send_sems.at[i, r - 1], recv_sem=recv_sems.at[i, r - 1],
                        device_id=(px, py, pc), device_id_type=pl.DeviceIdType.MESH)
            sends.append(pltpu.make_async_remote_copy(
                src_ref=src_of(i, px, py, pc), dst_ref=dsts[i].at[slot(i, x, y, c)], **sems))
            recvs.append(pltpu.make_async_remote_copy(
                src_ref=src_of(i, px, py, pc), dst_ref=dsts[i].at[slot(i, px, py, pc)], **sems))
    return local, sends, recvs


def _exchange_start(copies):
    local, sends, _ = copies
    for cp in local + sends:
        cp.start()


def _exchange_wait(copies):
    local, sends, recvs = copies
    for cp in recvs:
        cp.wait_recv()
    for cp in sends:
        cp.wait_send()
    for cp in local:
        cp.wait()


def _gather2_parts(srcs, dsts, send_sems, recv_sems, loc_sems):
    n = len(srcs)
    x, y, c = lax.axis_index("x"), lax.axis_index("y"), lax.axis_index("c")
    me, sibling = (x, y, c), (x, y, 1 - c)
    chips = [(1 - x, y), (x, 1 - y), (1 - x, 1 - y)]

    def rows(i, dev):
        return dsts[i].at[4 * dev[0] + 2 * dev[1] + dev[2]]

    def copy(i, k, block, to, src=None):
        return pltpu.make_async_remote_copy(
            src_ref=rows(i, block) if src is None else src, dst_ref=rows(i, block),
            send_sem=send_sems.at[i, k], recv_sem=recv_sems.at[i, k],
            device_id=to, device_id_type=pl.DeviceIdType.MESH)

    local = [pltpu.make_async_copy(srcs[i], rows(i, me), loc_sems.at[i]) for i in range(n)]
    first = []
    for i in range(n):
        first.append(copy(i, 0, me, sibling, src=srcs[i]))
        first += [copy(i, 1 + j, me, (*chip, c), src=srcs[i]) for j, chip in enumerate(chips)]
    return n, c, me, sibling, chips, copy, local, first


def _gather2_start(srcs, dsts, send_sems, recv_sems, loc_sems):
    *_, local, first = _gather2_parts(srcs, dsts, send_sems, recv_sems, loc_sems)
    for cp in local + first:
        cp.start()


def _gather2_finish(srcs, dsts, send_sems, recv_sems, loc_sems):
    n, c, me, sibling, chips, copy, local, first = _gather2_parts(srcs, dsts, send_sems, recv_sems, loc_sems)
    passed = []
    for j, chip in enumerate(chips):
        for i in range(n):
            copy(i, 1 + j, (*chip, c), me).wait_recv()
            cp = copy(i, 4 + j, (*chip, c), sibling)
            cp.start()
            passed.append(cp)
    for i in range(n):
        copy(i, 0, sibling, me).wait_recv()
        for j, chip in enumerate(chips):
            copy(i, 4 + j, (*chip, 1 - c), me).wait_recv()
    for cp in first + passed:
        cp.wait_send()
    for cp in local:
        cp.wait()


def _sibling_copies(srcs, dsts, send_sems, recv_sems):
    x, y, c = lax.axis_index("x"), lax.axis_index("y"), lax.axis_index("c")
    return [pltpu.make_async_remote_copy(
        src_ref=srcs[i].at[1 - c, k], dst_ref=dsts[i].at[k], send_sem=send_sems.at[i, k], recv_sem=recv_sems.at[i, k],
        device_id=(x, y, 1 - c), device_id_type=pl.DeviceIdType.MESH)
        for i in range(len(srcs)) for k in range(N_CHIPS)]


def _token_matmuls_and_swap(pairs, tk, swap):
    npair, nsw = len(pairs), len(swap)
    s = pairs[0][0].shape[1]
    last = s // tk - 1

    def body(*refs):
        a_refs, b_refs = refs[0:2 * npair:2], refs[1:2 * npair:2]
        src = refs[2 * npair:2 * npair + nsw]
        o_refs = refs[2 * npair + nsw:3 * npair + nsw]
        dst = refs[3 * npair + nsw:3 * npair + 2 * nsw]
        sems = refs[3 * npair + 2 * nsw:3 * npair + 2 * nsw + 2]
        acc_refs = refs[3 * npair + 2 * nsw + 2:]

        @pl.when(pl.program_id(0) == 0)
        def _():
            for cp in _sibling_copies(src, dst, *sems):
                cp.start()

        for a_ref, b_ref, o_ref, acc_ref in zip(a_refs, b_refs, o_refs, acc_refs):
            _accumulate_tokens(a_ref[...], b_ref, acc_ref, o_ref, last)

        @pl.when(pl.program_id(0) == last)
        def _():
            for cp in _sibling_copies(src, dst, *sems):
                cp.wait()

    in_specs, args = [], []
    for a_t, b in pairs:
        in_specs += [pl.BlockSpec((a_t.shape[0], tk), lambda k: (0, k)), pl.BlockSpec((tk, b.shape[1]), lambda k: (k, 0))]
        args += [a_t, b]
    outs = pl.pallas_call(
        body, name="dw_out_projs_swap", grid=(s // tk,),
        out_shape=[jax.ShapeDtypeStruct((a_t.shape[0], b.shape[1]), BF16) for a_t, b in pairs]
        + [jax.ShapeDtypeStruct(a.shape[1:], a.dtype) for a in swap],
        in_specs=in_specs + [ANY_SPEC] * nsw,
        out_specs=[pl.BlockSpec((a_t.shape[0], b.shape[1]), lambda k: (0, 0)) for a_t, b in pairs] + [ANY_SPEC] * nsw,
        scratch_shapes=[pltpu.SemaphoreType.DMA((nsw, N_CHIPS)), pltpu.SemaphoreType.DMA((nsw, N_CHIPS))]
        + [pltpu.VMEM((a_t.shape[0], b.shape[1]), F32) for a_t, b in pairs],
        compiler_params=_params(1))(*args, *swap)
    return outs[:npair], outs[npair:]


def _chip_partial(core, slots, theirs, tr, name):
    _, k, rws, cols = slots.shape

    def body(core_ref, a_ref, b_ref, f_ref, h_ref):
        sm = a_ref[0].astype(F32) + b_ref[...].astype(F32)
        f_ref[...] = sm
        h_ref[...] = sm.astype(BF16)

    blk = pl.BlockSpec((1, tr, cols), lambda j, i, core_ref: (j, i, 0))
    mine = pl.BlockSpec((1, 1, tr, cols), lambda j, i, core_ref: (core_ref[0], j, i, 0))
    return pl.pallas_call(
        body, name=name,
        grid_spec=pltpu.PrefetchScalarGridSpec(num_scalar_prefetch=1, grid=(k, rws // tr),
                                               in_specs=[mine, blk], out_specs=[blk, blk]),
        out_shape=[jax.ShapeDtypeStruct(theirs.shape, F32), jax.ShapeDtypeStruct(theirs.shape, BF16)],
        compiler_params=_params(2))(core, slots, theirs)


def _ada_fwd(c_all, w_shard, b_shard):
    def body(c_ref, w_ref, b_ref, o_ref):
        ch, cm, cl = [t.astype(BF16) for t in _split3(c_ref[...])]
        wh, wm, wl = [t.astype(BF16) for t in _split3(w_ref[...])]
        acc = _dot(ch, wh) + _dot(ch, wm) + _dot(cm, wh) + _dot(ch, wl) + _dot(cl, wh) + _dot(cm, wm)
        o_ref[...] = acc + b_ref[...]

    return pl.pallas_call(body, name="ada_fwd", out_shape=jax.ShapeDtypeStruct((N_DEV, ADA_SHARD), F32),
                          compiler_params=_params(0))(c_all, w_shard, b_shard)


def _rope_tables_and_gather(pos_col, inv_freq, tm, weights, c_rows):
    s = pos_col.shape[0]
    nw = len(weights)
    last = s // tm - 1

    def body(*refs):
        p_ref, f_ref = refs[:2]
        w_src, c_src = refs[2:2 + nw], refs[2 + nw:3 + nw]
        cos_ref, sa_ref, sb_ref = refs[3 + nw:6 + nw]
        w_dst, c_dst = refs[6 + nw:6 + 2 * nw], refs[6 + 2 * nw:7 + 2 * nw]
        w_sems, c_sems = refs[7 + 2 * nw:10 + 2 * nw], refs[10 + 2 * nw:]

        @pl.when(pl.program_id(0) == 0)
        def _():
            _gather2_start(w_src, w_dst, *w_sems)
            _exchange_start(_exchange_copies(c_src, c_dst, ["gather"], *c_sems))

        ang = p_ref[...] * f_ref[...]
        sin = jnp.sin(ang)
        first_half = (_lane(CHUNK) & (HEAD_DIM - 1)) < HEAD_DIM // 2
        cos_ref[...] = jnp.cos(ang)
        sa_ref[...] = jnp.where(first_half, -sin, 0.0)
        sb_ref[...] = jnp.where(first_half, 0.0, sin)

        @pl.when(pl.program_id(0) == last)
        def _():
            _exchange_wait(_exchange_copies(c_src, c_dst, ["gather"], *c_sems))
            _gather2_finish(w_src, w_dst, *w_sems)

    tab = jax.ShapeDtypeStruct((s, CHUNK), F32)
    blk = pl.BlockSpec((tm, CHUNK), lambda i: (i, 0))
    gathered = [jax.ShapeDtypeStruct((N_DEV,) + a.shape, a.dtype) for a in list(weights) + [c_rows]]
    outs = pl.pallas_call(
        body, name="rope_tables_gather", grid=(s // tm,), out_shape=[tab, tab, tab] + gathered,
        in_specs=[pl.BlockSpec((tm, 1), lambda i: (i, 0)), pl.BlockSpec((1, CHUNK), lambda i: (0, 0))]
        + [ANY_SPEC] * (nw + 1),
        out_specs=[blk, blk, blk] + [ANY_SPEC] * (nw + 1),
        scratch_shapes=_exchange_sems(nw) + _exchange_sems(1),
        compiler_params=_params(1))(pos_col, inv_freq, *weights, c_rows)
    return outs[:3], outs[3:3 + nw], outs[3 + nw]


def _fwd_proj(x, mod, w_all, cos, sa, sb, bf_row, emat, ind, tm, late_shards):
    s = x.shape[0]
    nl = len(late_shards)
    last = s // tm - 1

    def body(x_ref, mod_ref, w_ref, cos_ref, sa_ref, sb_ref, bf_ref, e_ref, ind_ref, *rest):
        late_src = rest[:nl]
        (h_ref, qa_ref, ka_ref, va_ref, za_ref, qb_ref, kb_ref, vb_ref, zb_ref, ga_ref, gb_ref, f_ref,
         bnd_ref, nrm_ref) = rest[nl:nl + 14]
        late_dst = rest[nl + 14:2 * nl + 14]
        carry_ref, late_sems = rest[2 * nl + 14], rest[2 * nl + 15:]

        @pl.when(pl.program_id(0) == 0)
        def _():
            carry_ref[...] = jnp.zeros_like(carry_ref)
            _gather2_start(late_src, late_dst, *late_sems)

        xv = x_ref[...]
        rstd = lax.rsqrt(jnp.mean(xv * xv, axis=-1, keepdims=True) + NORM_EPS)
        h = (xv * rstd * mod_ref[0:1, :]) * (1.0 + mod_ref[1:2, :]) + mod_ref[2:3, :]
        hb = h.astype(BF16)
        h_ref[...] = hb.T

        def seg(name):
            off, w = SEG[name]
            return _dot(hb, w_ref[:, off:off + w])

        lane = _lane(CHUNK)
        low = lane < HEAD_DIM
        q_ones = _ones_lanes(64, 67)
        k_ones = _ones_lanes(67, 70)
        cos_t, sa_t, sb_t = cos_ref[...], sa_ref[...], sb_ref[...]

        def write_heads(ref, nat, extra_of, rope, scale):
            for p in range(N_HEADS // 2):
                c = nat[:, CHUNK * p:CHUNK * (p + 1)]
                if rope:
                    c = _rope(c, cos_t, sa_t, sb_t)
                if scale != 1.0:
                    c = c * scale
                for hh in range(2):
                    hd = 2 * p + hh
                    src = c if hh == 0 else pltpu.roll(c, HEAD_DIM, 1)
                    ref[:, CHUNK * hd:CHUNK * (hd + 1)] = jnp.where(low, src, extra_of(hd)).astype(BF16)

        write_heads(qa_ref, seg("qa"), lambda hd: q_ones, True, Q_SCALE)
        ka = seg("ka")
        va = seg("va")
        for kv in range(KV_GROUPS):
            sl = slice(CHUNK * kv, CHUNK * (kv + 1))
            ka_ref[:, sl] = jnp.where(low, _rope(ka[:, sl], cos_t, sa_t, sb_t), k_ones).astype(BF16)
            va_ref[:, sl] = jnp.where(low, va[:, sl], q_ones).astype(BF16)
        za_ref[...] = seg("za")
        zb_ref[...] = seg("zb")
        ga_ref[...] = seg("ga")
        gb_ref[...] = seg("gb")

        xf = seg("f") + bf_ref[...]
        f_ref[...] = xf
        logf = jnp.minimum(xf, 0.0) - jnp.log1p(jnp.exp(-jnp.abs(xf)))
        row = lax.broadcasted_iota(jnp.int32, (tm, tm), 0)
        col = lax.broadcasted_iota(jnp.int32, (tm, tm), 1)
        tri = jnp.where(col <= row, 1.0, 0.0).astype(BF16)
        hi, mid, lo = _split3(logf)
        cum = _dot(tri, hi.astype(BF16)) + _dot(tri, mid.astype(BF16)) + _dot(tri, lo.astype(BF16))
        cum = cum + carry_ref[0:1, :]
        carry_ref[0:1, :] += jnp.sum(logf, axis=0, keepdims=True)
        lane_all = _lane(AUG_W) & (CHUNK - 1)
        bias = _spread3(-cum, e_ref) + jnp.where((lane_all >= 67) & (lane_all < 70), 1.0, 0.0)

        qb, kb = seg("qb"), seg("kb")
        write_heads(qb_ref, qb, lambda hd: q_ones, False, Q_SCALE)
        write_heads(kb_ref, kb, lambda hd: bias[:, CHUNK * hd:CHUNK * (hd + 1)], False, 1.0)
        write_heads(vb_ref, seg("vb"), lambda hd: q_ones, False, 1.0)

        rid = lax.broadcasted_iota(jnp.int32, (tm, CHUNK), 0)
        bnd_ref[...] = jnp.zeros_like(bnd_ref)
        bnd_ref[0, 0:1, :] = jnp.sum(jnp.where(rid == 0, cum, 0.0), axis=0, keepdims=True)
        bnd_ref[0, 1:2, :] = jnp.sum(jnp.where(rid == tm - 1, cum, 0.0), axis=0, keepdims=True)

        @pl.when(pl.program_id(0) == 0)
        def _():
            nrm_ref[...] = jnp.zeros_like(nrm_ref)

        for r_, nat in ((0, qb * Q_SCALE), (1, kb)):
            sq = _dot((nat * nat).astype(BF16), ind_ref[...])
            nrm_ref[r_:r_ + 1, :] = jnp.maximum(nrm_ref[r_:r_ + 1, :], jnp.max(sq, axis=0, keepdims=True))

        @pl.when(pl.program_id(0) == last)
        def _():
            _gather2_finish(late_src, late_dst, *late_sems)

    row_blk = lambda w: pl.BlockSpec((tm, w), lambda i: (i, 0))
    full = lambda a: pl.BlockSpec(a.shape, lambda i: (0,) * a.ndim)
    sds = lambda w, dt: jax.ShapeDtypeStruct((s, w), dt)
    out_shape = [sds(AUG_W, BF16), sds(KV_GROUPS * CHUNK, BF16), sds(KV_GROUPS * CHUNK, BF16),
                 sds(512, F32), sds(AUG_W, BF16), sds(AUG_W, BF16), sds(AUG_W, BF16), sds(512, F32),
                 sds(D_MODEL, F32), sds(D_MODEL, F32), sds(CHUNK, F32)]
    small = [jax.ShapeDtypeStruct((s // tm, 8, CHUNK), F32), jax.ShapeDtypeStruct((8, CHUNK), F32)]
    gathered = [jax.ShapeDtypeStruct((N_DEV,) + a.shape, a.dtype) for a in late_shards]
    outs = pl.pallas_call(
        body, name="fwd_proj", grid=(s // tm,),
        out_shape=[jax.ShapeDtypeStruct((D_MODEL, s), BF16)] + out_shape + small + gathered,
        in_specs=[row_blk(D_MODEL), full(mod), _resident(w_all), row_blk(CHUNK), row_blk(CHUNK), row_blk(CHUNK),
                  full(bf_row), full(emat), full(ind)] + [ANY_SPEC] * nl,
        out_specs=[pl.BlockSpec((D_MODEL, tm), lambda i: (0, i))] + [row_blk(o.shape[1]) for o in out_shape] + [
            pl.BlockSpec((1, 8, CHUNK), lambda i: (i, 0, 0)), pl.BlockSpec((8, CHUNK), lambda i: (0, 0))]
        + [ANY_SPEC] * nl,
        scratch_shapes=[pltpu.VMEM((8, CHUNK), F32)] + _exchange_sems(nl),
        compiler_params=_params(1))(x, mod, w_all, cos, sa, sb, bf_row, emat, ind, *late_shards)
    return outs[:14], outs[14:]


def _swa_fwd(q_aug, k_aug, v_aug, sink_rows, tq):
    s = q_aug.shape[0]
    r = tq // WINDOW
    gw = GROUP * CHUNK

    def body(q_ref, kc_ref, kp_ref, vc_ref, vp_ref, sink_ref, o_ref, qb_ref):
        i = pl.program_id(1)
        st = GROUP * WINDOW
        qloc = lax.broadcasted_iota(jnp.int32, (st, 2 * WINDOW), 0) & (WINDOW - 1)
        col = lax.broadcasted_iota(jnp.int32, (st, 2 * WINDOW), 1)
        band = (col > qloc) & (col <= qloc + WINDOW)
        head = jnp.right_shift(lax.broadcasted_iota(jnp.int32, (st, 1), 0), 7)
        sink = jnp.zeros((st, 1), F32)
        for g in range(GROUP):
            sink = jnp.where(head == g, jnp.max(sink_ref[0, g:g + 1, :], axis=1, keepdims=True), sink)
        lane = _lane(CHUNK)
        for sub in range(r):
            rows = slice(WINDOW * sub, WINDOW * (sub + 1))
            q = jnp.concatenate([q_ref[rows, CHUNK * g:CHUNK * (g + 1)] for g in range(GROUP)], axis=0)
            if sub == 0:
                k = jnp.concatenate([kp_ref[...], kc_ref[rows, :]], axis=0)
                v = jnp.concatenate([vp_ref[...], vc_ref[rows, :]], axis=0)
                valid = band & ((col >= WINDOW) | (i > 0))
            else:
                both = slice(WINDOW * (sub - 1), WINDOW * (sub + 1))
                k, v, valid = kc_ref[both, :], vc_ref[both, :], band
            sc = jnp.where(valid, _dot_nt(q, k), NEG)
            m = jnp.maximum(jnp.max(sc, axis=1, keepdims=True), sink)
            acc = _dot(jnp.exp((sc - m).astype(BF16)), v)
            denom = _lane_sum(acc, 64, 65) + jnp.exp(sink - m)
            out = acc * (1.0 / denom)
            aug = _place3(lane, 67, -(m + jnp.log(denom)), q.astype(F32)).astype(BF16)
            hrows = lambda a, g: a[WINDOW * g:WINDOW * (g + 1), :]
            for g in range(GROUP):
                qb_ref[rows, CHUNK * g:CHUNK * (g + 1)] = hrows(aug, g)
            for pp in range(GROUP // 2):
                o_ref[rows, CHUNK * pp:CHUNK * (pp + 1)] = _pair(hrows(out, 2 * pp), hrows(out, 2 * pp + 1))

    return pl.pallas_call(
        body, name="swa_fwd", grid=(KV_GROUPS, s // tq),
        out_shape=[jax.ShapeDtypeStruct((s, 512), F32), jax.ShapeDtypeStruct((s, AUG_W), BF16)],
        in_specs=[pl.BlockSpec((tq, gw), lambda kv, i: (i, kv)),
                  pl.BlockSpec((tq, CHUNK), lambda kv, i: (i, kv)),
                  pl.BlockSpec((WINDOW, CHUNK), lambda kv, i: (jnp.maximum(i * r - 1, 0), kv)),
                  pl.BlockSpec((tq, CHUNK), lambda kv, i: (i, kv)),
                  pl.BlockSpec((WINDOW, CHUNK), lambda kv, i: (jnp.maximum(i * r - 1, 0), kv)),
                  pl.BlockSpec((1, 8, CHUNK), lambda kv, i: (kv, 0, 0))],
        out_specs=[pl.BlockSpec((tq, GROUP * HEAD_DIM), lambda kv, i: (i, kv)),
                   pl.BlockSpec((tq, gw), lambda kv, i: (i, kv))],
        compiler_params=_params(2))(q_aug, k_aug, k_aug, v_aug, v_aug, sink_rows)


def _swa_bwd(k_aug, v_aug, q_bwd, do_aug, cos, sa, sb, sink_rows, tk):
    s = k_aug.shape[0]
    r = tk // WINDOW
    nt = s // tk
    nb = s // WINDOW

    def body(k_ref, v_ref, q_ref, qn_ref, do_ref, don_ref, cos_ref, sa_ref, sb_ref, sink_ref,
             da_ref, dsink_ref, carry_ref, acc_ref):
        j = pl.program_id(0)

        @pl.when(j == 0)
        def _():
            carry_ref[...] = jnp.zeros_like(carry_ref)
            dsink_ref[...] = jnp.zeros_like(dsink_ref)

        acc_ref[...] = jnp.zeros_like(acc_ref)
        acc_ref[0:WINDOW, :] = carry_ref[...]
        dk_col, dv_col = SEG["ka"][0] - SEG["qa"][0], SEG["va"][0] - SEG["qa"][0]
        st = GROUP * WINDOW
        key = lax.broadcasted_iota(jnp.int32, (WINDOW, 2 * st), 0)
        col = lax.broadcasted_iota(jnp.int32, (WINDOW, 2 * st), 1)
        qloc = col & (WINDOW - 1)
        band = ((col < st) & (key <= qloc)) | ((col >= st) & (key > qloc))
        lane = _lane(CHUNK)
        low = lane < HEAD_DIM
        cos_t, sa_t, sb_t = cos_ref[...], sa_ref[...], sb_ref[...]
        for kv, sub in [(kv, sub) for kv in range(KV_GROUPS) for sub in range(r)]:
            heads = [slice(CHUNK * (GROUP * kv + g), CHUNK * (GROUP * kv + g + 1)) for g in range(GROUP)]
            kvs = slice(CHUNK * kv, CHUNK * (kv + 1))
            rows = slice(WINDOW * sub, WINDOW * (sub + 1))
            nxt = slice(WINDOW * (sub + 1), WINDOW * (sub + 2))
            k, v = k_ref[rows, kvs], v_ref[rows, kvs]
            q_cur, do_cur = [q_ref[rows, cs] for cs in heads], [do_ref[rows, cs] for cs in heads]
            if sub < r - 1:
                q_nxt, do_nxt, valid = [q_ref[nxt, cs] for cs in heads], [do_ref[nxt, cs] for cs in heads], band
            else:
                q_nxt, do_nxt = [qn_ref[:, cs] for cs in heads], [don_ref[:, cs] for cs in heads]
                valid = band & ((col < st) | (j < nt - 1))
            q = jnp.concatenate(q_cur + q_nxt, axis=0)
            do = jnp.concatenate(do_cur + do_nxt, axis=0)
            pt = jnp.exp(jnp.where(valid, _dot_nt(k, q), NEG))
            ds = (pt * _dot_nt(v, do)).astype(BF16)
            dv = _dot(pt.astype(BF16), do)
            dk = _dot(ds, q)
            dq = _dot_tn(ds, k)
            for g, cs in enumerate(heads):
                acc_ref[rows, cs] += dq[WINDOW * g:WINDOW * (g + 1), :]
                dqn = dq[st + WINDOW * g:st + WINDOW * (g + 1), :]
                if sub < r - 1:
                    acc_ref[nxt, cs] += dqn
                else:
                    carry_ref[:, cs] = dqn
                sink = jnp.max(sink_ref[kv, g:g + 1, :], axis=1, keepdims=True)
                p_sink = jnp.exp(sink + _lane_sum(q_cur[g].astype(F32), 67, 70))
                term = jnp.sum(p_sink * _lane_sum(do_cur[g].astype(F32), 64, 67), axis=0, keepdims=True)
                dsink_ref[kv, g:g + 1, :] += jnp.broadcast_to(term, (1, CHUNK))
            da_ref[rows, dk_col + CHUNK * kv:dk_col + CHUNK * (kv + 1)] = _rope_inv(
                jnp.where(low, dk, 0.0), cos_t[rows, :], sa_t[rows, :], sb_t[rows, :]).astype(BF16)
            da_ref[rows, dv_col + CHUNK * kv:dv_col + CHUNK * (kv + 1)] = jnp.where(low, dv, 0.0).astype(BF16)
        for pp in range(N_HEADS // 2):
            d = _pair(acc_ref[:, CHUNK * 2 * pp:CHUNK * (2 * pp + 1)], acc_ref[:, CHUNK * (2 * pp + 1):CHUNK * (2 * pp + 2)])
            da_ref[:, CHUNK * pp:CHUNK * (pp + 1)] = (_rope_inv(d, cos_t, sa_t, sb_t) * Q_SCALE).astype(BF16)

    cur = lambda w: pl.BlockSpec((tk, w), lambda j: (j, 0))
    nxt = pl.BlockSpec((WINDOW, AUG_W), lambda j: (jnp.minimum((j + 1) * r, nb - 1), 0))
    whole = pl.BlockSpec((KV_GROUPS, 8, CHUNK), lambda j: (0, 0, 0))
    return pl.pallas_call(
        body, name="swa_bwd", grid=(nt,),
        out_shape=[jax.ShapeDtypeStruct((s, 1024), BF16), jax.ShapeDtypeStruct((KV_GROUPS, 8, CHUNK), F32)],
        in_specs=[cur(KV_GROUPS * CHUNK), cur(KV_GROUPS * CHUNK), cur(AUG_W), nxt, cur(AUG_W), nxt,
                  cur(CHUNK), cur(CHUNK), cur(CHUNK), whole],
        out_specs=[cur(1024), whole],
        scratch_shapes=[pltpu.VMEM((WINDOW, AUG_W), F32), pltpu.VMEM((tk, AUG_W), F32)],
        compiler_params=_params(1))(k_aug, v_aug, q_bwd, q_bwd, do_aug, do_aug, cos, sa, sb, sink_rows)


def _fox_fwd(first_key, q_aug, k_aug, v_aug, t):
    s = q_aug.shape[0]
    pw = 2 * CHUNK

    def body(lo_ref, q_ref, k_ref, v_ref, o_ref, qb_ref):
        i = pl.program_id(1)
        first = lo_ref[pl.program_id(0), i]
        row = lax.broadcasted_iota(jnp.int32, (t, t), 0)
        col = lax.broadcasted_iota(jnp.int32, (t, t), 1)
        lane = _lane(CHUNK)
        heads = (slice(0, CHUNK), slice(CHUNK, pw))
        qs = [q_ref[:, hs] for hs in heads]

        def step(jb, carry, masked):
            rows = pl.ds(pl.multiple_of(jb * t, t), t)
            scs = [_dot_nt(qs[hh], k_ref[rows, hs]) for hh, hs in enumerate(heads)]
            if masked:
                scs = [jnp.where(col <= row, sc, NEG) for sc in scs]
            m_news = [jnp.maximum(carry[2 * hh], jnp.max(scs[hh], axis=1, keepdims=True)) for hh in range(2)]
            ps = [jnp.exp((scs[hh] - m_news[hh]).astype(BF16)) for hh in range(2)]
            new = []
            for hh, hs in enumerate(heads):
                m, acc = carry[2 * hh], carry[2 * hh + 1]
                new += [m_news[hh], jnp.exp(m - m_news[hh]) * acc + _dot(ps[hh], v_ref[rows, hs])]
            return tuple(new)

        init = (jnp.full((t, 1), NEG, F32), jnp.zeros((t, CHUNK), F32)) * 2
        carry = step(i, lax.fori_loop(first, i, lambda jb, c: step(jb, c, False), init), True)
        outs = []
        for hh, hs in enumerate(heads):
            m, acc = carry[2 * hh], carry[2 * hh + 1]
            denom = _lane_sum(acc, 64, 65)
            outs.append(acc * (1.0 / denom))
            lse = m + jnp.log(denom)
            qb_ref[:, hs] = _place3(lane, 67, -lse, qs[hh].astype(F32)).astype(BF16)
        o_ref[...] = _pair(outs[0], outs[1])

    return pl.pallas_call(
        body, name="fox_fwd", grid=(N_HEADS // 2, s // t),
        out_shape=[jax.ShapeDtypeStruct((s, 512), F32), jax.ShapeDtypeStruct((s, AUG_W), BF16)],
        in_specs=[pl.BlockSpec(memory_space=pltpu.SMEM),
                  pl.BlockSpec((t, pw), lambda p, i: (i, p)),
                  pl.BlockSpec((s, pw), lambda p, i: (0, p)),
                  pl.BlockSpec((s, pw), lambda p, i: (0, p))],
        out_specs=[pl.BlockSpec((t, CHUNK), lambda p, i: (i, p)), pl.BlockSpec((t, pw), lambda p, i: (i, p))],
        compiler_params=_params(2))(first_key, q_aug, k_aug, v_aug)


def _fox_block_ranges(bounds, norms, r):
    cum_first = bounds[0::r, 0, :N_HEADS]
    cum_last = bounds[r - 1::r, 1, :N_HEADS]
    n = cum_first.shape[0]
    reach = 2.0 * 1.02 * jnp.sqrt(norms[0, :N_HEADS] * norms[1, :N_HEADS]) + UNDERFLOW
    blk = jnp.arange(n, dtype=jnp.int32)
    decay = cum_last[None, :, :] - cum_first[:, None, :]
    skip = (decay > reach[None, None, :]) & (blk[None, :, None] < blk[:, None, None])
    first = jnp.min(jnp.where(skip, n, blk[None, :, None]), axis=1)
    first = jnp.minimum(first[:, 0::2], first[:, 1::2]).T
    needed = (first[:, :, None] <= blk[None, None, :]) & (blk[None, :, None] >= blk[None, None, :])
    last = jnp.max(jnp.where(needed, blk[None, :, None], blk[None, None, :]), axis=1)
    return first.astype(jnp.int32), last.astype(jnp.int32)


def _fox_bwd(last_query, k_aug, v_aug, q_bwd, do_aug, t):
    s = k_aug.shape[0]
    n = s // t
    pw = 2 * CHUNK

    def body(hi_ref, k_ref, v_ref, q_ref, do_ref, dk_ref, dv_ref, dck_ref, dq_ref, dcq_ref, dq_scr):
        j = pl.program_id(1)
        last = hi_ref[pl.program_id(0), j]

        @pl.when(j == 0)
        def _():
            dq_scr[...] = jnp.zeros_like(dq_scr)

        row = lax.broadcasted_iota(jnp.int32, (t, t), 0)
        col = lax.broadcasted_iota(jnp.int32, (t, t), 1)
        lane = _lane(CHUNK)
        heads = (slice(0, CHUNK), slice(CHUNK, pw))
        ks = [k_ref[:, hs] for hs in heads]
        vs = [v_ref[:, hs] for hs in heads]

        def step(ib, carry, masked):
            rows = pl.ds(pl.multiple_of(ib * t, t), t)
            new = []
            for hh, hs in enumerate(heads):
                dv, dk = carry[2 * hh], carry[2 * hh + 1]
                q, do = q_ref[rows, hs], do_ref[rows, hs]
                st = _dot_nt(ks[hh], q)
                if masked:
                    st = jnp.where(row <= col, st, NEG)
                pt = jnp.exp(st)
                ds = (pt * _dot_nt(vs[hh], do)).astype(BF16)
                new += [dv + _dot(pt.astype(BF16), do), dk + _dot(ds, q)]
                dq_scr[rows, hs] += _dot_tn(ds, ks[hh])
            return tuple(new)

        zero = jnp.zeros((t, CHUNK), F32)
        carry = lax.fori_loop(j + 1, last + 1, lambda ib, c: step(ib, c, False), step(j, (zero,) * 4, True))
        dvs, dks = (carry[0], carry[2]), (carry[1], carry[3])
        dk_ref[...] = _pair(dks[0], dks[1]).astype(BF16)
        dv_ref[...] = _pair(dvs[0], dvs[1]).astype(BF16)
        dck_ref[0] = jnp.where(lane == 0, pltpu.roll(dks[0], 64, 1),
                               jnp.where(lane == 1, pltpu.roll(dks[1], 65, 1), 0.0))

        @pl.when(j == n - 1)
        def _():
            for ib in range(n):
                rows = slice(t * ib, t * (ib + 1))
                d0, d1 = dq_scr[rows, 0:CHUNK], dq_scr[rows, CHUNK:pw]
                dq_ref[rows, :] = (_pair(d0, d1) * Q_SCALE).astype(BF16)
                dcq_ref[0, rows, :] = jnp.where(lane == 0, pltpu.roll(d0, CHUNK - 67, 1),
                                                jnp.where(lane == 1, pltpu.roll(d1, CHUNK - 66, 1), 0.0))

    return pl.pallas_call(
        body, name="fox_bwd", grid=(N_HEADS // 2, n),
        out_shape=[jax.ShapeDtypeStruct((s, 512), BF16), jax.ShapeDtypeStruct((s, 512), BF16),
                   jax.ShapeDtypeStruct((N_HEADS // 2, s, CHUNK), F32), jax.ShapeDtypeStruct((s, 512), BF16),
                   jax.ShapeDtypeStruct((N_HEADS // 2, s, CHUNK), F32)],
        in_specs=[pl.BlockSpec(memory_space=pltpu.SMEM),
                  pl.BlockSpec((t, pw), lambda p, j: (j, p)), pl.BlockSpec((t, pw), lambda p, j: (j, p)),
                  pl.BlockSpec((s, pw), lambda p, j: (0, p)), pl.BlockSpec((s, pw), lambda p, j: (0, p))],
        out_specs=[pl.BlockSpec((t, CHUNK), lambda p, j: (j, p)), pl.BlockSpec((t, CHUNK), lambda p, j: (j, p)),
                   pl.BlockSpec((1, t, CHUNK), lambda p, j: (p, j, 0)), pl.BlockSpec((s, CHUNK), lambda p, j: (0, p)),
                   pl.BlockSpec((1, s, CHUNK), lambda p, j: (p, 0, 0))],
        scratch_shapes=[pltpu.VMEM((s, pw), F32)],
        compiler_params=_params(2))(last_query, k_aug, v_aug, q_bwd, do_aug)


def _fgate_bwd(dcq, dck, xf, tm):
    s = xf.shape[0]
    nt = s // tm

    def body(dq_ref, dc_ref, xf_ref, df_ref, dbf_ref, carry_ref):
        @pl.when(pl.program_id(0) == 0)
        def _():
            carry_ref[...] = jnp.zeros_like(carry_ref)
            dbf_ref[...] = jnp.zeros_like(dbf_ref)

        row = lax.broadcasted_iota(jnp.int32, (tm, tm), 0)
        col = lax.broadcasted_iota(jnp.int32, (tm, tm), 1)
        tri = jnp.where(col >= row, 1.0, 0.0).astype(BF16)
        dcum = dq_ref[0] - dc_ref[0]
        for p in range(1, N_HEADS // 2):
            dcum = dcum + pltpu.roll(dq_ref[p] - dc_ref[p], 2 * p, 1)
        hi, mid, lo = _split3(dcum)
        dlogf = _dot(tri, hi.astype(BF16)) + _dot(tri, mid.astype(BF16)) + _dot(tri, lo.astype(BF16))
        dlogf = dlogf + carry_ref[0:1, :]
        carry_ref[0:1, :] += jnp.sum(dcum, axis=0, keepdims=True)
        df = dlogf / (1.0 + jnp.exp(xf_ref[...]))
        df_ref[...] = df.astype(BF16)
        dbf_ref[0:1, :] += jnp.sum(df, axis=0, keepdims=True)

    rev = pl.BlockSpec((tm, CHUNK), lambda i: (nt - 1 - i, 0))
    rev4 = pl.BlockSpec((N_HEADS // 2, tm, CHUNK), lambda i: (0, nt - 1 - i, 0))
    return pl.pallas_call(
        body, name="fgate_bwd", grid=(nt,),
        out_shape=[jax.ShapeDtypeStruct((s, CHUNK), BF16), jax.ShapeDtypeStruct((8, CHUNK), F32)],
        in_specs=[rev4, rev4, rev], out_specs=[rev, pl.BlockSpec((8, CHUNK), lambda i: (0, 0))],
        scratch_shapes=[pltpu.VMEM((8, CHUNK), F32)],
        compiler_params=_params(1))(dcq, dck, xf)


def _post(x, tgt, att_a, att_b, za, zb, ga, gb, woa_t, wob_t, wout, vec, ind, emat, tm):
    s = x.shape[0]

    def body(x_ref, t_ref, aa_ref, ab_ref, za_ref, zb_ref, ga_ref, gb_ref, woat_ref, wobt_ref, wout_ref,
             vec_ref, ind_ref, e_ref,
             dx1_ref, ua_ref, ub_ref, mg_ref, do_ref, dya_ref, dyb_ref, doa_ref, dob_ref, dzg_ref, acc_ref):
        @pl.when(pl.program_id(0) == 0)
        def _():
            acc_ref[...] = jnp.zeros_like(acc_ref)

        gate, gfin = vec_ref[0:1, :], vec_ref[1:2, :]
        inv_d = 1.0 / D_MODEL

        def branch_fwd(att_ref, z_ref, w_ref, u_ref):
            att, z = att_ref[...], z_ref[...]
            sz = _sigmoid(z)
            silu = z * sz
            u = (att * silu).astype(BF16)
            u_ref[...] = u.T
            return att, z, sz, silu, _dot_nt(u, w_ref[...])

        att_a, z_a, sz_a, silu_a, y_a = branch_fwd(aa_ref, za_ref, woat_ref, ua_ref)
        att_b, z_b, sz_b, silu_b, y_b = branch_fwd(ab_ref, zb_ref, wobt_ref, ub_ref)
        sg_a, sg_b = _sigmoid(ga_ref[...]), _sigmoid(gb_ref[...])
        merged = (sg_a * y_a + sg_b * y_b).astype(BF16)
        mg_ref[...] = merged.T
        o = _dot(merged, wout_ref[...])
        x1 = x_ref[...] + gate * o
        rstd = lax.rsqrt(jnp.mean(x1 * x1, axis=-1, keepdims=True) + NORM_EPS)
        xh = x1 * rstd
        diff = xh * gfin - t_ref[...]
        acc_ref[2:3, :] += (0.5 * inv_d) * jnp.sum(diff * diff, axis=0, keepdims=True)
        dy = diff * inv_d
        acc_ref[1:2, :] += jnp.sum(dy * xh, axis=0, keepdims=True)
        dyg = dy * gfin
        dx1 = rstd * (dyg - xh * jnp.mean(dyg * xh, axis=-1, keepdims=True))
        dx1_ref[...] = dx1
        acc_ref[0:1, :] += jnp.sum(dx1 * o, axis=0, keepdims=True)
        d_o = (dx1 * gate).astype(BF16)
        do_ref[...] = d_o
        dmg = _dot_nt(d_o, wout_ref[...])

        lane = _lane(CHUNK)
        low = lane < HEAD_DIM
        zg0 = SEG["za"][0]

        def branch_bwd(sg, y, att, z, sz, silu, wt_ref, dy_ref, g_name, z_name, doaug_ref):
            dyb = (dmg * sg).astype(BF16)
            dy_ref[...] = dyb
            g_off, z_off = SEG[g_name][0] - zg0, SEG[z_name][0] - zg0
            dzg_ref[:, g_off:g_off + D_MODEL] = (dmg * y * sg * (1.0 - sg)).astype(BF16)
            du = _dot(dyb, wt_ref[...])
            datt = du * silu
            dzg_ref[:, z_off:z_off + 512] = (du * att * (sz * (1.0 + z * (1.0 - sz)))).astype(BF16)
            extra = _spread3(-_dot_split(datt * att, ind_ref[...]), e_ref)
            for p in range(N_HEADS // 2):
                c = datt[:, CHUNK * p:CHUNK * (p + 1)]
                for hh in range(2):
                    hd = 2 * p + hh
                    src = c if hh == 0 else pltpu.roll(c, HEAD_DIM, 1)
                    doaug_ref[:, CHUNK * hd:CHUNK * (hd + 1)] = jnp.where(
                        low, src, extra[:, CHUNK * hd:CHUNK * (hd + 1)]).astype(BF16)

        branch_bwd(sg_a, y_a, att_a, z_a, sz_a, silu_a, woat_ref, dya_ref, "ga", "za", doa_ref)
        branch_bwd(sg_b, y_b, att_b, z_b, sz_b, silu_b, wobt_ref, dyb_ref, "gb", "zb", dob_ref)

    row_blk = lambda w: pl.BlockSpec((tm, w), lambda i: (i, 0))
    full = lambda a: pl.BlockSpec(a.shape, lambda i: (0,) * a.ndim)
    sds = lambda w, dt: jax.ShapeDtypeStruct((s, w), dt)
    tds = lambda w: jax.ShapeDtypeStruct((w, s), BF16)
    col_blk = lambda w: pl.BlockSpec((w, tm), lambda i: (0, i))
    out_shape = [sds(D_MODEL, F32), tds(512), tds(512), tds(D_MODEL), sds(D_MODEL, BF16),
                 sds(D_MODEL, BF16), sds(D_MODEL, BF16), sds(AUG_W, BF16), sds(AUG_W, BF16), sds(3072, BF16),
                 jax.ShapeDtypeStruct((8, D_MODEL), F32)]
    ins = [x, tgt, att_a, att_b, za, zb, ga, gb, woa_t, wob_t, wout, vec, ind, emat]
    in_specs = [row_blk(a.shape[1]) for a in ins[:8]] + [full(a) for a in ins[8:]]
    out_specs = ([row_blk(D_MODEL), col_blk(512), col_blk(512), col_blk(D_MODEL)]
                 + [row_blk(o.shape[1]) for o in out_shape[4:-1]] + [pl.BlockSpec((8, D_MODEL), lambda i: (0, 0))])
    return pl.pallas_call(body, name="post", grid=(s // tm,), out_shape=out_shape, in_specs=in_specs,
                          out_specs=out_specs, compiler_params=_params(1))(*ins)


def _bwd_pre(dproj, w_all, x, dx1, mod, tm, chip_halves, modes):
    s = x.shape[0]
    ng, nh = len(DPROJ_GROUPS), len(chip_halves)
    last = s // tm - 1

    def body(*refs):
        dp_refs = refs[:ng]
        wt_ref, x_ref, dx1_ref, mod_ref = refs[ng:ng + 4]
        send_refs = refs[ng + 4:ng + 4 + nh]
        gx_ref, acc_ref = refs[ng + 4 + nh:ng + 6 + nh]
        recv_refs = refs[ng + 6 + nh:ng + 6 + 2 * nh]
        sems = refs[ng + 6 + 2 * nh:]

        @pl.when(pl.program_id(0) == 0)
        def _():
            acc_ref[...] = jnp.zeros_like(acc_ref)
            _exchange_start(_exchange_copies(send_refs, recv_refs, modes, *sems))

        dh = None
        for dp_ref, (_, off, w) in zip(dp_refs, DPROJ_GROUPS):
            part = _dot_nt(dp_ref[...], wt_ref[:, off:off + w])
            dh = part if dh is None else dh + part
        xv = x_ref[...]
        rstd = lax.rsqrt(jnp.mean(xv * xv, axis=-1, keepdims=True) + NORM_EPS)
        xh = xv * rstd
        gn = mod_ref[0:1, :]
        acc_ref[0:1, :] += jnp.sum(dh, axis=0, keepdims=True)
        acc_ref[1:2, :] += jnp.sum(dh * (xh * gn), axis=0, keepdims=True)
        dn = dh * (1.0 + mod_ref[1:2, :])
        acc_ref[2:3, :] += jnp.sum(dn * xh, axis=0, keepdims=True)
        dxh = dn * gn
        gx_ref[...] = dx1_ref[...] + rstd * (dxh - xh * jnp.mean(dxh * xh, axis=-1, keepdims=True))

        @pl.when(pl.program_id(0) == last)
        def _():
            _exchange_wait(_exchange_copies(send_refs, recv_refs, modes, *sems))

    row_blk = lambda w: pl.BlockSpec((tm, w), lambda i: (i, 0))
    full = lambda a: pl.BlockSpec(a.shape, lambda i: (0,) * a.ndim)
    outs = pl.pallas_call(
        body, name="bwd_pre", grid=(s // tm,),
        out_shape=[jax.ShapeDtypeStruct((s, D_MODEL), F32), jax.ShapeDtypeStruct((8, D_MODEL), F32)]
        + _exchange_out_shapes(chip_halves, modes),
        in_specs=[row_blk(w) for _, _, w in DPROJ_GROUPS] + [_resident(w_all), row_blk(D_MODEL), row_blk(D_MODEL), full(mod)]
        + [ANY_SPEC] * nh,
        out_specs=[row_blk(D_MODEL), pl.BlockSpec((8, D_MODEL), lambda i: (0, 0))] + [ANY_SPEC] * nh,
        scratch_shapes=_exchange_sems(nh),
        compiler_params=_params(1))(*dproj, w_all, x, dx1, mod, *chip_halves)
    return outs[0], outs[1], outs[2:]


def _accumulate_tokens(a, b_ref, acc_ref, o_ref, last):
    @pl.when(pl.program_id(0) == 0)
    def _():
        acc_ref[...] = jnp.zeros_like(acc_ref)

    acc_ref[...] += _dot(a, b_ref[...])

    @pl.when(pl.program_id(0) == last)
    def _():
        o_ref[...] = acc_ref[...].astype(BF16)


def _token_matmul_multi(a_t, bs, tk, name):
    m, s = a_t.shape
    nb = len(bs)
    last = s // tk - 1

    def body(*refs):
        a_ref, b_refs, o_refs, acc_refs = refs[0], refs[1:1 + nb], refs[1 + nb:1 + 2 * nb], refs[1 + 2 * nb:]
        a = a_ref[...]
        for b_ref, o_ref, acc_ref in zip(b_refs, o_refs, acc_refs):
            _accumulate_tokens(a, b_ref, acc_ref, o_ref, last)

    return pl.pallas_call(
        body, name=name, grid=(s // tk,),
        out_shape=[jax.ShapeDtypeStruct((m, b.shape[1]), BF16) for b in bs],
        in_specs=[pl.BlockSpec((m, tk), lambda k: (0, k))] + [pl.BlockSpec((tk, b.shape[1]), lambda k: (k, 0)) for b in bs],
        out_specs=[pl.BlockSpec((m, b.shape[1]), lambda k: (0, 0)) for b in bs],
        scratch_shapes=[pltpu.VMEM((m, b.shape[1]), F32) for b in bs],
        compiler_params=_params(1))(a_t, *bs)


def _adam_math(g, w, m, v):
    m2 = ADAM_B1 * m + (1.0 - ADAM_B1) * g
    v2 = ADAM_B2 * v + (1.0 - ADAM_B2) * (g * g)
    delta = -ADAM_LR * ((m2 / ADAM_C1) / (jnp.sqrt(v2 / ADAM_C2) + ADAM_EPS) + ADAM_WD * w)
    return delta, m2, v2


def _adamw_parts(full, parts, others, w, m, v, tr, name):
    _, rws, cols = w.shape

    def body(oth_ref, f_ref, p_ref, w_ref, m_ref, v_ref, g_ref, d_ref, m2_ref, v2_ref):
        g = None
        for chip in range(N_CHIPS):
            part = (1.0 - oth_ref[chip]) * f_ref[chip] + oth_ref[chip] * p_ref[chip].astype(F32)
            g = part if g is None else g + part
        g_ref[0] = g
        d_ref[0], m2_ref[0], v2_ref[0] = _adam_math(g, w_ref[0], m_ref[0], v_ref[0])

    blk = pl.BlockSpec((1, tr, cols), lambda i: (0, i, 0))
    slots = pl.BlockSpec((N_CHIPS, tr, cols), lambda i: (0, i, 0))
    o = jax.ShapeDtypeStruct((1, rws, cols), F32)
    return pl.pallas_call(
        body, name=name, grid=(rws // tr,), out_shape=[o, o, o, o],
        in_specs=[pl.BlockSpec(memory_space=pltpu.SMEM), slots, slots, blk, blk, blk],
        out_specs=[blk, blk, blk, blk], compiler_params=_params(1))(others, full, parts, w, m, v)


def _adamw_devices(parts, w, m, v, tr, name):
    _, rws, cols = w.shape

    def body(p_ref, w_ref, m_ref, v_ref, g_ref, d_ref, m2_ref, v2_ref):
        g = p_ref[0].astype(F32)
        for dev in range(1, N_DEV):
            g = g + p_ref[dev].astype(F32)
        g_ref[0] = g
        d_ref[0], m2_ref[0], v2_ref[0] = _adam_math(g, w_ref[0], m_ref[0], v_ref[0])

    blk = pl.BlockSpec((1, tr, cols), lambda i: (0, i, 0))
    o = jax.ShapeDtypeStruct((1, rws, cols), F32)
    return pl.pallas_call(
        body, name=name, grid=(rws // tr,), out_shape=[o, o, o, o],
        in_specs=[pl.BlockSpec((N_DEV, tr, cols), lambda i: (0, i, 0)), blk, blk, blk],
        out_specs=[blk, blk, blk, blk], compiler_params=_params(1))(parts, w, m, v)


def _adamw_ada(c_lanes, d_rows, w, m, v, tr):
    _, rws, cols = w.shape

    def body(c_ref, d_ref, w_ref, m_ref, v_ref, g_ref, dl_ref, m2_ref, v2_ref):
        for k in range(cols // CHUNK):
            cs = slice(CHUNK * k, CHUNK * (k + 1))
            g = c_ref[0] * d_ref[0:1, cs]
            for b in range(1, N_DEV):
                g = g + c_ref[b] * d_ref[b:b + 1, cs]
            g_ref[0, :, cs] = g
            dl_ref[0, :, cs], m2_ref[0, :, cs], v2_ref[0, :, cs] = _adam_math(
                g, w_ref[0, :, cs], m_ref[0, :, cs], v_ref[0, :, cs])

    blk = pl.BlockSpec((1, tr, cols), lambda i: (0, i, 0))
    o = jax.ShapeDtypeStruct((1, rws, cols), F32)
    return pl.pallas_call(
        body, name="adamw_ada", grid=(rws // tr,), out_shape=[o, o, o, o],
        in_specs=[pl.BlockSpec((N_DEV, tr, CHUNK), lambda i: (0, i, 0)), pl.BlockSpec((N_DEV, cols), lambda i: (0, 0)),
                  blk, blk, blk],
        out_specs=[blk, blk, blk, blk], compiler_params=_params(1))(c_lanes, d_rows, w, m, v)


SMALL = (("b_ada", 0, 0, 3 * D_MODEL), ("g_norm", 1, 0, D_MODEL), ("b_f", 3, 0, N_HEADS), ("sinks", 4, 0, N_HEADS),
         ("g_final", 1, D_MODEL, D_MODEL))
LOSS_ROW = 2


def _pack_small(pre_acc, post_acc, dbf_acc, dsink):
    def body(pre_ref, post_ref, dbf_ref, dsink_ref, o_ref):
        o_ref[...] = jnp.zeros_like(o_ref)
        d = D_MODEL
        o_ref[0:1, 0:d], o_ref[0:1, d:2 * d], o_ref[0:1, 2 * d:3 * d] = pre_ref[0:1, :], pre_ref[1:2, :], post_ref[0:1, :]
        o_ref[1:2, 0:d], o_ref[1:2, d:2 * d] = pre_ref[2:3, :], post_ref[1:2, :]
        o_ref[LOSS_ROW:LOSS_ROW + 1, 0:d] = post_ref[2:3, :]
        o_ref[3:4, 0:CHUNK] = dbf_ref[0:1, :]
        lane = _lane(CHUNK)
        sinks = jnp.zeros((1, CHUNK), F32)
        for kv in range(KV_GROUPS):
            for g in range(GROUP):
                sinks = jnp.where(lane == GROUP * kv + g, dsink_ref[kv, g:g + 1, :], sinks)
        o_ref[4:5, 0:CHUNK] = sinks

    return pl.pallas_call(body, name="pack_small", out_shape=jax.ShapeDtypeStruct((8, 3 * D_MODEL), F32),
                          compiler_params=_params(0))(pre_acc, post_acc, dbf_acc, dsink)


def _adamw_small(packs, params):
    n = len(SMALL)

    def body(*refs):
        p_ref, wmv = refs[0], refs[1:1 + 3 * n]
        outs, loss_ref, tot_ref = refs[1 + 3 * n:1 + 7 * n], refs[1 + 7 * n], refs[2 + 7 * n]
        tot = p_ref[0]
        for dev in range(1, N_DEV):
            tot = tot + p_ref[dev]
        tot_ref[...] = tot
        for i, (_, row, lo, w) in enumerate(SMALL):
            g = tot_ref[row:row + 1, lo:lo + w]
            o = outs[4 * i:4 * i + 4]
            o[0][...] = g
            o[1][...], o[2][...], o[3][...] = _adam_math(g, wmv[3 * i][...], wmv[3 * i + 1][...], wmv[3 * i + 2][...])
        loss_ref[...] = jnp.broadcast_to(jnp.sum(tot_ref[LOSS_ROW:LOSS_ROW + 1, 0:D_MODEL], axis=1, keepdims=True),
                                         loss_ref.shape)

    flat = [a for wmv in params for a in wmv]
    out_shape = [jax.ShapeDtypeStruct(wmv[0].shape, F32) for wmv in params for _ in range(4)]
    res = pl.pallas_call(body, name="adamw_small", out_shape=out_shape + [jax.ShapeDtypeStruct((8, CHUNK), F32)],
                         scratch_shapes=[pltpu.VMEM((8, 3 * D_MODEL), F32)],
                         compiler_params=_params(0))(packs, *flat)
    return [tuple(res[4 * i:4 * i + 4]) for i in range(n)], res[4 * n]


def _tile(s, want):
    return min(s, want)


def _layout_pieces():
    orig = {"qa": 0, "ka": 512, "va": 640, "za": 768, "qb": 1280, "kb": 1792, "vb": 2304, "f": 2816, "zb": 2824,
            "ga": 3336, "gb": 4360}
    pieces = []
    for name, (off, w) in SEG.items():
        if name in ("ka", "va"):
            pieces += [(orig[name] + HEAD_DIM * kv, orig[name] + HEAD_DIM * (kv + 1), off + CHUNK * kv) for kv in range(KV_GROUPS)]
        else:
            pieces.append((orig[name], orig[name] + (N_HEADS if name == "f" else w), off))
    return pieces


def _assemble_w_all(win_g):
    cols, pos = [], 0
    for lo, hi, new in sorted(_layout_pieces(), key=lambda t: t[2]):
        if new > pos:
            cols.append(jnp.zeros((D_MODEL, new - pos), win_g.dtype))
        col = lo
        while col < hi:
            dev = col // IN_SHARD
            end = min(hi, (dev + 1) * IN_SHARD)
            cols.append(win_g[dev, :, col - dev * IN_SHARD:end - dev * IN_SHARD])
            col = end
        pos = new + hi - lo
    cols.append(jnp.zeros((D_MODEL, PROJ_W - pos), win_g.dtype))
    return jnp.concatenate(cols, axis=1)


def _grad_slot(dw_groups, dev):
    lo_d, hi_d = dev * IN_SHARD, (dev + 1) * IN_SHARD
    cols = []
    for lo, hi, new in sorted(_layout_pieces()):
        a, b = max(lo, lo_d), min(hi, hi_d)
        if a < b:
            arr, off = next((g, o) for g, (_, o, w) in zip(dw_groups, DPROJ_GROUPS) if o <= new < o + w)
            cols.append(arr[:, new - off + a - lo:new - off + b - lo])
    return jnp.concatenate(cols, axis=1)


def kernel(x, c, positions, w_ada, b_ada, g_norm, w_in, b_f, sinks, w_o_swa, w_o_fox, w_out, g_final, loss_target, m_w_ada, m_b_ada, m_g_norm, m_w_in, m_b_f, m_sinks, m_w_o_swa, m_w_o_fox, m_w_out, m_g_final, v_w_ada, v_b_ada, v_g_norm, v_w_in, v_b_f, v_sinks, v_w_o_swa, v_w_o_fox, v_w_out, v_g_final):
    s = x.shape[1]
    tm = _tile(s, 256)
    ta = _tile(s, 512)
    me = 4 * lax.axis_index("x") + 2 * lax.axis_index("y") + lax.axis_index("c")
    x2, tgt = x[0], loss_target[0]

    inv_freq = np.power(np.float32(ROPE_THETA), -np.arange(0, HEAD_DIM, 2, dtype=np.float32) / HEAD_DIM)
    inv_freq = jnp.asarray(np.tile(inv_freq, CHUNK // (HEAD_DIM // 2))[None, :], F32)
    (cos, sa, sb), (win_g,), c_all = _rope_tables_and_gather(
        positions.reshape(s, 1).astype(F32), inv_freq, tm, [w_in[0].astype(BF16)], jnp.broadcast_to(c, (8, D_MODEL)))
    c_all = c_all[:, 0, :]
    b_shard = lax.dynamic_slice(b_ada, (0, me * ADA_SHARD), (1, ADA_SHARD))
    ada_part = _ada_fwd(c_all, w_ada[0], b_shard)
    (ada_g,) = _exchange([ada_part], ["gather"], "gather_ada")
    ada = lax.dynamic_index_in_dim(ada_g, me, axis=1, keepdims=False).reshape(3 * D_MODEL)
    shift, scale, gate = ada[:D_MODEL], ada[D_MODEL:2 * D_MODEL], ada[2 * D_MODEL:]
    w_all = _assemble_w_all(win_g)

    zrow = jnp.zeros((1, D_MODEL), F32)
    mod = jnp.concatenate([g_norm, scale[None], shift[None], zrow, zrow, zrow, zrow, zrow], axis=0)
    bf_row = jnp.pad(b_f, ((0, 0), (0, CHUNK - N_HEADS)))
    emat_np = np.zeros((3 * CHUNK, AUG_W), np.float32)
    ind_np = np.zeros((512, CHUNK), np.float32)
    for hd in range(N_HEADS):
        for part in range(3):
            emat_np[CHUNK * part + hd, CHUNK * hd + 64 + part] = 1.0
        ind_np[HEAD_DIM * hd:HEAD_DIM * (hd + 1), hd] = 1.0
    emat, ind = jnp.asarray(emat_np, BF16), jnp.asarray(ind_np, BF16)
    sink_rows = jnp.broadcast_to(jnp.pad(sinks.reshape(KV_GROUPS, GROUP), ((0, 0), (0, 8 - GROUP)))[:, :, None],
                                 (KV_GROUPS, 8, CHUNK))

    (h_t, qa, ka, va, za, qb, kb, vb, zb, ga, gb, xf, bounds, norms), (woa_g, wob_g, wout_g) = _fwd_proj(
        x2, mod, w_all, cos, sa, sb, bf_row, emat, ind, ta,
        [w_o_swa[0].T.astype(BF16), w_o_fox[0].T.astype(BF16), w_out[0].astype(BF16)])
    woa_t, wob_t, wout = woa_g.reshape(D_MODEL, 512), wob_g.reshape(D_MODEL, 512), wout_g.reshape(D_MODEL, D_MODEL)
    first_key, last_query = _fox_block_ranges(bounds, norms, 1)
    att_a, qa_bwd = _swa_fwd(qa, ka, va, sink_rows, _tile(s, 1024))
    att_b, qb_bwd = _fox_fwd(first_key, qb, kb, vb, ta)
    vec = jnp.concatenate([gate[None], g_final[None], zrow, zrow, zrow, zrow, zrow, zrow], axis=0)
    (dx1, ua_t, ub_t, merged_t, d_o, dya, dyb, doa, dob, dzg, post_acc) = _post(
        x2, tgt, att_a, att_b, za, zb, ga, gb, woa_t, wob_t, wout, vec, ind, emat, tm)

    da, dsink = _swa_bwd(ka, va, qa_bwd, doa, cos, sa, sb, sink_rows, ta)
    dkb, dvb, dck, dqb, dcq = _fox_bwd(last_query, kb, vb, qb_bwd, dob, ta)
    dfb, dbf_acc = _fgate_bwd(dcq, dck, xf, ta)
    dproj = [da, dqb, dkb, dvb, dzg, dfb]
    tw = _tile(s, 1024)
    dw_zg, = _token_matmul_multi(h_t, [dzg], tw, "dw_in_zg")
    dw_a, dw_qb, dw_kb, dw_vb, dw_f = _token_matmul_multi(h_t, [da, dqb, dkb, dvb, dfb], tw, "dw_in_rest")
    dw_all = [dw_a, dw_qb, dw_kb, dw_vb, dw_zg, dw_f]

    core, chip = lax.axis_index("c"), 2 * lax.axis_index("x") + lax.axis_index("y")
    win_slots = jnp.stack([jnp.stack([_grad_slot(dw_all, 2 * ch + co) for ch in range(N_CHIPS)]) for co in range(2)])
    (dwoa, dwob, dwout), (win_theirs,) = _token_matmuls_and_swap(
        [(ua_t, dya), (ub_t, dyb), (merged_t, d_o)], ta, [win_slots])
    win_full, win_half = _chip_partial(jnp.reshape(core, (1,)).astype(jnp.int32), win_slots, win_theirs, 256,
                                       "chip_partial_w_in")
    col_slots = lambda g: g.reshape(512, N_DEV, 128).transpose(1, 0, 2)
    grad_x, pre_acc, (p_win, p_woa, p_wob, p_wout) = _bwd_pre(
        dproj, w_all, x2, dx1, mod, ta,
        [win_half, col_slots(dwoa), col_slots(dwob), dwout.reshape(N_DEV, 128, D_MODEL)],
        ["chips", "devices", "devices", "devices"])
    (packs,) = _exchange([_pack_small(pre_acc, post_acc, dbf_acc, dsink)], ["gather"], "gather_small_grads")
    others = jnp.where(jnp.arange(N_CHIPS) == chip, 0.0, 1.0).astype(F32)

    g_win, d_win, m_win, v_win = _adamw_parts(win_full, p_win, others, w_in, m_w_in, v_w_in, 128, "adamw_w_in")
    g_woa, d_woa, m_woa, v_woa = _adamw_devices(p_woa, w_o_swa, m_w_o_swa, v_w_o_swa, 128, "adamw_w_o_swa")
    g_wob, d_wob, m_wob, v_wob = _adamw_devices(p_wob, w_o_fox, m_w_o_fox, v_w_o_fox, 128, "adamw_w_o_fox")
    g_wout, d_wout, m_wout, v_wout = _adamw_devices(p_wout, w_out, m_w_out, v_w_out, 128, "adamw_w_out")
    d_ada_rows = lax.dynamic_slice(packs[:, 0, :], (0, me * ADA_SHARD), (N_DEV, ADA_SHARD))
    c_lanes = jnp.broadcast_to(c_all[:, :, None], (N_DEV, D_MODEL, CHUNK))
    g_wada, d_wada, m_wada, v_wada = _adamw_ada(c_lanes, d_ada_rows, w_ada, m_w_ada, v_w_ada, 256)

    row = lambda a: a.reshape(1, D_MODEL)
    small, loss_rows = _adamw_small(packs, [(b_ada, m_b_ada, v_b_ada), (g_norm, m_g_norm, v_g_norm), (b_f, m_b_f, v_b_f),
                                            (sinks, m_sinks, v_sinks), (row(g_final), row(m_g_final), row(v_g_final))])
    bada_o, gn_o, bf_o, sk_o, gf_o = small

    outs = []
    for k, big in enumerate(((g_wada, g_win, g_woa, g_wob, g_wout), (d_wada, d_win, d_woa, d_wob, d_wout),
                             (m_wada, m_win, m_woa, m_wob, m_wout), (v_wada, v_win, v_woa, v_wob, v_wout))):
        wada_o, win_o, woa_o, wob_o, wout_o = big
        outs += [wada_o, bada_o[k], gn_o[k], win_o, bf_o[k], sk_o[k], woa_o, wob_o, wout_o, gf_o[k].reshape(D_MODEL)]
    return (loss_rows[0, 0], grad_x[None], *outs)
```

```python
import numpy as np
import jax
import jax.numpy as jnp
from jax import lax
from jax.experimental import pallas as pl
from jax.experimental.pallas import tpu as pltpu

F32 = jnp.float32
BF16 = jnp.bfloat16

D_MODEL = 1024
HEAD_DIM = 64
N_HEADS = 8
KV_GROUPS = 2
GROUP = N_HEADS // KV_GROUPS
WINDOW = 128
CHUNK = 128
AUG_W = N_HEADS * CHUNK
N_DEV = 8
IN_SHARD = 673
ADA_SHARD = 384
NORM_EPS = 1e-6
ROPE_THETA = 10000.0
Q_SCALE = HEAD_DIM ** -0.5
NEG = -1e30
UNDERFLOW = 105.0

ADAM_LR = 0.001
ADAM_B1 = 0.9
ADAM_B2 = 0.999
ADAM_EPS = 1e-08
ADAM_WD = 0.01
ADAM_STEP = 10
ADAM_C1 = 1.0 - ADAM_B1 ** ADAM_STEP
ADAM_C2 = 1.0 - ADAM_B2 ** ADAM_STEP

SEG = {}
_off = 0
for _name, _w in (("qa", 512), ("ka", 256), ("va", 256), ("qb", 512), ("kb", 512), ("vb", 512),
                  ("za", 512), ("zb", 512), ("ga", 1024), ("gb", 1024), ("f", 128)):
    SEG[_name] = (_off, _w)
    _off += _w
PROJ_W = _off
DPROJ_GROUPS = (("a", SEG["qa"][0], 1024), ("qb", SEG["qb"][0], 512), ("kb", SEG["kb"][0], 512),
                ("vb", SEG["vb"][0], 512), ("zg", SEG["za"][0], 3072), ("f", SEG["f"][0], 128))

VMEM_LIMIT = 56 * 1024 * 1024


def _params(n_axes):
    return pltpu.CompilerParams(dimension_semantics=("arbitrary",) * n_axes, vmem_limit_bytes=VMEM_LIMIT)


def _resident(a):
    return pl.BlockSpec(a.shape, lambda i: (0, 0), pipeline_mode=pl.Buffered(1))


def _dot(a, b):
    return jnp.dot(a, b, preferred_element_type=F32)


def _dot_nt(a, b):
    return lax.dot_general(a, b, (((1,), (1,)), ((), ())), preferred_element_type=F32)


def _dot_tn(a, b):
    return lax.dot_general(a, b, (((0,), (0,)), ((), ())), preferred_element_type=F32)


def _lane(n):
    return lax.broadcasted_iota(jnp.int32, (1, n), 1)


def _split3(x):
    hi = x.astype(BF16).astype(F32)
    r = x - hi
    mid = r.astype(BF16).astype(F32)
    lo = (r - mid).astype(BF16).astype(F32)
    return hi, mid, lo


def _dot_split(x, b):
    hi, mid, lo = _split3(x)
    return _dot(hi.astype(BF16), b) + _dot(mid.astype(BF16), b) + _dot(lo.astype(BF16), b)


def _spread3(x, e3_ref):
    return _dot(jnp.concatenate(_split3(x), axis=1).astype(BF16), e3_ref[...])


def _place3(lane, base, x, other):
    hi, mid, lo = _split3(x)
    return jnp.where(lane == base, hi, jnp.where(lane == base + 1, mid, jnp.where(lane == base + 2, lo, other)))


def _lane_sum(x, lo, hi):
    lane = _lane(x.shape[1])
    return jnp.sum(jnp.where((lane >= lo) & (lane < hi), x, 0.0), axis=1, keepdims=True)


def _ones_lanes(lo, hi):
    lane = _lane(CHUNK)
    return jnp.where((lane >= lo) & (lane < hi), 1.0, 0.0).astype(F32)


def _rope(c, cos, sa, sb):
    return c * cos + pltpu.roll(c, CHUNK - 32, 1) * sa + pltpu.roll(c, 32, 1) * sb


def _rope_inv(d, cos, sa, sb):
    return d * cos - (pltpu.roll(d, CHUNK - 32, 1) * sa + pltpu.roll(d, 32, 1) * sb)


def _pair(c0, c1):
    return jnp.where(_lane(CHUNK) < HEAD_DIM, c0, pltpu.roll(c1, HEAD_DIM, 1))


def _sigmoid(x):
    return 0.5 * jnp.tanh(0.5 * x) + 0.5


N_CHIPS = 4
ANY_SPEC = pl.BlockSpec(memory_space=pl.ANY)


def _exchange(arrays, modes, name):
    n = len(arrays)

    def body(*refs):
        copies = _exchange_copies(refs[:n], refs[n:2 * n], modes, *refs[2 * n:])
        _exchange_start(copies)
        _exchange_wait(copies)

    return pl.pallas_call(
        body, name=name, out_shape=_exchange_out_shapes(arrays, modes),
        in_specs=[ANY_SPEC] * n, out_specs=[ANY_SPEC] * n, scratch_shapes=_exchange_sems(n),
    )(*arrays)


def _exchange_out_shapes(arrays, modes):
    return [jax.ShapeDtypeStruct((N_DEV,) + a.shape if md == "gather" else a.shape, a.dtype)
            for a, md in zip(arrays, modes)]


def _exchange_sems(n):
    return [pltpu.SemaphoreType.DMA((n, N_DEV - 1)), pltpu.SemaphoreType.DMA((n, N_DEV - 1)),
            pltpu.SemaphoreType.DMA((n,))]


def _exchange_copies(srcs, dsts, modes, send_sems, recv_sems, loc_sems):
    n = len(srcs)
    x, y, c = lax.axis_index("x"), lax.axis_index("y"), lax.axis_index("c")

    def slot(i, px, py, pc):
        return 2 * px + py if modes[i] == "chips" else 4 * px + 2 * py + pc

    def src_of(i, px, py, pc):
        return srcs[i] if modes[i] == "gather" else srcs[i].at[slot(i, px, py, pc)]

    local = [pltpu.make_async_copy(src_of(i, x, y, c), dsts[i].at[slot(i, x, y, c)], loc_sems.at[i])
             for i in range(n)]
    sends, recvs = [], []
    for r in (1, 2, 4, 3, 5, 6, 7):
        px = 1 - x if r & 4 else x
        py = 1 - y if r & 2 else y
        pc = 1 - c if r & 1 else c
        for i in range(n):
            if modes[i] == "chips" and r & 1:
                continue
            sems = dict(send_sem=send_sems.at[i, r - 1], recv_sem=recv_sems.at[i, r - 1],
                        device_id=(px, py, pc), device_id_type=pl.DeviceIdType.MESH)
            sends.append(pltpu.make_async_remote_copy(
                src_ref=src_of(i, px, py, pc), dst_ref=dsts[i].at[slot(i, x, y, c)], **sems))
            recvs.append(pltpu.make_async_remote_copy(
                src_ref=src_of(i, px, py, pc), dst_ref=dsts[i].at[slot(i, px, py, pc)], **sems))
    return local, sends, recvs


def _exchange_start(copies):
    local, sends, _ = copies
    for cp in local + sends:
        cp.start()


def _exchange_wait(copies):
    local, sends, recvs = copies
    for cp in recvs:
        cp.wait_recv()
    for cp in sends:
        cp.wait_send()
    for cp in local:
        cp.wait()


def _gather2_parts(srcs, dsts, send_sems, recv_sems, loc_sems):
    n = len(srcs)
    x, y, c = lax.axis_index("x"), lax.axis_index("y"), lax.axis_index("c")
    me, sibling = (x, y, c), (x, y, 1 - c)
    chips = [(1 - x, y), (x, 1 - y), (1 - x, 1 - y)]

    def rows(i, dev):
        return dsts[i].at[4 * dev[0] + 2 * dev[1] + dev[2]]

    def copy(i, k, block, to, src=None):
        return pltpu.make_async_remote_copy(
            src_ref=rows(i, block) if src is None else src, dst_ref=rows(i, block),
            send_sem=send_sems.at[i, k], recv_sem=recv_sems.at[i, k],
            device_id=to, device_id_type=pl.DeviceIdType.MESH)

    local = [pltpu.make_async_copy(srcs[i], rows(i, me), loc_sems.at[i]) for i in range(n)]
    first = []
    for i in range(n):
        first.append(copy(i, 0, me, sibling, src=srcs[i]))
        first += [copy(i, 1 + j, me, (*chip, c), src=srcs[i]) for j, chip in enumerate(chips)]
    return n, c, me, sibling, chips, copy, local, first


def _gather2_start(srcs, dsts, send_sems, recv_sems, loc_sems):
    *_, local, first = _gather2_parts(srcs, dsts, send_sems, recv_sems, loc_sems)
    for cp in local + first:
        cp.start()


def _gather2_finish(srcs, dsts, send_sems, recv_sems, loc_sems):
    n, c, me, sibling, chips, copy, local, first = _gather2_parts(srcs, dsts, send_sems, recv_sems, loc_sems)
    passed = []
    for j, chip in enumerate(chips):
        for i in range(n):
            copy(i, 1 + j, (*chip, c), me).wait_recv()
            cp = copy(i, 4 + j, (*chip, c), sibling)
            cp.start()
            passed.append(cp)
    for i in range(n):
        copy(i, 0, sibling, me).wait_recv()
        for j, chip in enumerate(chips):
            copy(i, 4 + j, (*chip, 1 - c), me).wait_recv()
    for cp in first + passed:
        cp.wait_send()
    for cp in local:
        cp.wait()


def _sibling_copies(srcs, dsts, send_sems, recv_sems):
    x, y, c = lax.axis_index("x"), lax.axis_index("y"), lax.axis_index("c")
    return [pltpu.make_async_remote_copy(
        src_ref=srcs[i].at[1 - c, k], dst_ref=dsts[i].at[k], send_sem=send_sems.at[i, k], recv_sem=recv_sems.at[i, k],
        device_id=(x, y, 1 - c), device_id_type=pl.DeviceIdType.MESH)
        for i in range(len(srcs)) for k in range(N_CHIPS)]


def _token_matmuls_and_swap(pairs, tk, swap):
    npair, nsw = len(pairs), len(swap)
    s = pairs[0][0].shape[1]
    last = s // tk - 1

    def body(*refs):
        a_refs, b_refs = refs[0:2 * npair:2], refs[1:2 * npair:2]
        src = refs[2 * npair:2 * npair + nsw]
        o_refs = refs[2 * npair + nsw:3 * npair + nsw]
        dst = refs[3 * npair + nsw:3 * npair + 2 * nsw]
        sems = refs[3 * npair + 2 * nsw:3 * npair + 2 * nsw + 2]
        acc_refs = refs[3 * npair + 2 * nsw + 2:]

        @pl.when(pl.program_id(0) == 0)
        def _():
            for cp in _sibling_copies(src, dst, *sems):
                cp.start()

        for a_ref, b_ref, o_ref, acc_ref in zip(a_refs, b_refs, o_refs, acc_refs):
            _accumulate_tokens(a_ref[...], b_ref, acc_ref, o_ref, last)

        @pl.when(pl.program_id(0) == last)
        def _():
            for cp in _sibling_copies(src, dst, *sems):
                cp.wait()

    in_specs, args = [], []
    for a_t, b in pairs:
        in_specs += [pl.BlockSpec((a_t.shape[0], tk), lambda k: (0, k)), pl.BlockSpec((tk, b.shape[1]), lambda k: (k, 0))]
        args += [a_t, b]
    outs = pl.pallas_call(
        body, name="dw_out_projs_swap", grid=(s // tk,),
        out_shape=[jax.ShapeDtypeStruct((a_t.shape[0], b.shape[1]), BF16) for a_t, b in pairs]
        + [jax.ShapeDtypeStruct(a.shape[1:], a.dtype) for a in swap],
        in_specs=in_specs + [ANY_SPEC] * nsw,
        out_specs=[pl.BlockSpec((a_t.shape[0], b.shape[1]), lambda k: (0, 0)) for a_t, b in pairs] + [ANY_SPEC] * nsw,
        scratch_shapes=[pltpu.SemaphoreType.DMA((nsw, N_CHIPS)), pltpu.SemaphoreType.DMA((nsw, N_CHIPS))]
        + [pltpu.VMEM((a_t.shape[0], b.shape[1]), F32) for a_t, b in pairs],
        compiler_params=_params(1))(*args, *swap)
    return outs[:npair], outs[npair:]


def _chip_partial(core, slots, theirs, tr, name):
    _, k, rws, cols = slots.shape

    def body(core_ref, a_ref, b_ref, f_ref, h_ref):
        sm = a_ref[0].astype(F32) + b_ref[...].astype(F32)
        f_ref[...] = sm
        h_ref[...] = sm.astype(BF16)

    blk = pl.BlockSpec((1, tr, cols), lambda j, i, core_ref: (j, i, 0))
    mine = pl.BlockSpec((1, 1, tr, cols), lambda j, i, core_ref: (core_ref[0], j, i, 0))
    return pl.pallas_call(
        body, name=name,
        grid_spec=pltpu.PrefetchScalarGridSpec(num_scalar_prefetch=1, grid=(k, rws // tr),
                                               in_specs=[mine, blk], out_specs=[blk, blk]),
        out_shape=[jax.ShapeDtypeStruct(theirs.shape, F32), jax.ShapeDtypeStruct(theirs.shape, BF16)],
        compiler_params=_params(2))(core, slots, theirs)


def _ada_fwd(c_all, w_shard, b_shard):
    def body(c_ref, w_ref, b_ref, o_ref):
        ch, cm, cl = [t.astype(BF16) for t in _split3(c_ref[...])]
        wh, wm, wl = [t.astype(BF16) for t in _split3(w_ref[...])]
        acc = _dot(ch, wh) + _dot(ch, wm) + _dot(cm, wh) + _dot(ch, wl) + _dot(cl, wh) + _dot(cm, wm)
        o_ref[...] = acc + b_ref[...]

    return pl.pallas_call(body, name="ada_fwd", out_shape=jax.ShapeDtypeStruct((N_DEV, ADA_SHARD), F32),
                          compiler_params=_params(0))(c_all, w_shard, b_shard)


def _rope_tables_and_gather(pos_col, inv_freq, tm, weights, c_rows):
    s = pos_col.shape[0]
    nw = len(weights)
    last = s // tm - 1

    def body(*refs):
        p_ref, f_ref = refs[:2]
        w_src, c_src = refs[2:2 + nw], refs[2 + nw:3 + nw]
        cos_ref, sa_ref, sb_ref = refs[3 + nw:6 + nw]
        w_dst, c_dst = refs[6 + nw:6 + 2 * nw], refs[6 + 2 * nw:7 + 2 * nw]
        w_sems, c_sems = refs[7 + 2 * nw:10 + 2 * nw], refs[10 + 2 * nw:]

        @pl.when(pl.program_id(0) == 0)
        def _():
            _gather2_start(w_src, w_dst, *w_sems)
            _exchange_start(_exchange_copies(c_src, c_dst, ["gather"], *c_sems))

        ang = p_ref[...] * f_ref[...]
        sin = jnp.sin(ang)
        first_half = (_lane(CHUNK) & (HEAD_DIM - 1)) < HEAD_DIM // 2
        cos_ref[...] = jnp.cos(ang)
        sa_ref[...] = jnp.where(first_half, -sin, 0.0)
        sb_ref[...] = jnp.where(first_half, 0.0, sin)

        @pl.when(pl.program_id(0) == last)
        def _():
            _exchange_wait(_exchange_copies(c_src, c_dst, ["gather"], *c_sems))
            _gather2_finish(w_src, w_dst, *w_sems)

    tab = jax.ShapeDtypeStruct((s, CHUNK), F32)
    blk = pl.BlockSpec((tm, CHUNK), lambda i: (i, 0))
    gathered = [jax.ShapeDtypeStruct((N_DEV,) + a.shape, a.dtype) for a in list(weights) + [c_rows]]
    outs = pl.pallas_call(
        body, name="rope_tables_gather", grid=(s // tm,), out_shape=[tab, tab, tab] + gathered,
        in_specs=[pl.BlockSpec((tm, 1), lambda i: (i, 0)), pl.BlockSpec((1, CHUNK), lambda i: (0, 0))]
        + [ANY_SPEC] * (nw + 1),
        out_specs=[blk, blk, blk] + [ANY_SPEC] * (nw + 1),
        scratch_shapes=_exchange_sems(nw) + _exchange_sems(1),
        compiler_params=_params(1))(pos_col, inv_freq, *weights, c_rows)
    return outs[:3], outs[3:3 + nw], outs[3 + nw]


def _fwd_proj(x, mod, w_all, cos, sa, sb, bf_row, emat, ind, tm, late_shards):
    s = x.shape[0]
    nl = len(late_shards)
    last = s // tm - 1

    def body(x_ref, mod_ref, w_ref, cos_ref, sa_ref, sb_ref, bf_ref, e_ref, ind_ref, *rest):
        late_src = rest[:nl]
        (h_ref, qa_ref, ka_ref, va_ref, za_ref, qb_ref, kb_ref, vb_ref, zb_ref, ga_ref, gb_ref, f_ref,
         bnd_ref, nrm_ref) = rest[nl:nl + 14]
        late_dst = rest[nl + 14:2 * nl + 14]
        carry_ref, late_sems = rest[2 * nl + 14], rest[2 * nl + 15:]

        @pl.when(pl.program_id(0) == 0)
        def _():
            carry_ref[...] = jnp.zeros_like(carry_ref)
            _gather2_start(late_src, late_dst, *late_sems)

        xv = x_ref[...]
        rstd = lax.rsqrt(jnp.mean(xv * xv, axis=-1, keepdims=True) + NORM_EPS)
        h = (xv * rstd * mod_ref[0:1, :]) * (1.0 + mod_ref[1:2, :]) + mod_ref[2:3, :]
        hb = h.astype(BF16)
        h_ref[...] = hb.T

        def seg(name):
            off, w = SEG[name]
            return _dot(hb, w_ref[:, off:off + w])

        lane = _lane(CHUNK)
        low = lane < HEAD_DIM
        q_ones = _ones_lanes(64, 67)
        k_ones = _ones_lanes(67, 70)
        cos_t, sa_t, sb_t = cos_ref[...], sa_ref[...], sb_ref[...]

        def write_heads(ref, nat, extra_of, rope, scale):
            for p in range(N_HEADS // 2):
                c = nat[:, CHUNK * p:CHUNK * (p + 1)]
                if rope:
                    c = _rope(c, cos_t, sa_t, sb_t)
                if scale != 1.0:
                    c = c * scale
                for hh in range(2):
                    hd = 2 * p + hh
                    src = c if hh == 0 else pltpu.roll(c, HEAD_DIM, 1)
                    ref[:, CHUNK * hd:CHUNK * (hd + 1)] = jnp.where(low, src, extra_of(hd)).astype(BF16)

        write_heads(qa_ref, seg("qa"), lambda hd: q_ones, True, Q_SCALE)
        ka = seg("ka")
        va = seg("va")
        for kv in range(KV_GROUPS):
            sl = slice(CHUNK * kv, CHUNK * (kv + 1))
            ka_ref[:, sl] = jnp.where(low, _rope(ka[:, sl], cos_t, sa_t, sb_t), k_ones).astype(BF16)
            va_ref[:, sl] = jnp.where(low, va[:, sl], q_ones).astype(BF16)
        za_ref[...] = seg("za")
        zb_ref[...] = seg("zb")
        ga_ref[...] = seg("ga")
        gb_ref[...] = seg("gb")

        xf = seg("f") + bf_ref[...]
        f_ref[...] = xf
        logf = jnp.minimum(xf, 0.0) - jnp.log1p(jnp.exp(-jnp.abs(xf)))
        row = lax.broadcasted_iota(jnp.int32, (tm, tm), 0)
        col = lax.broadcasted_iota(jnp.int32, (tm, tm), 1)
        tri = jnp.where(col <= row, 1.0, 0.0).astype(BF16)
        hi, mid, lo = _split3(logf)
        cum = _dot(tri, hi.astype(BF16)) + _dot(tri, mid.astype(BF16)) + _dot(tri, lo.astype(BF16))
        cum = cum + carry_ref[0:1, :]
        carry_ref[0:1, :] += jnp.sum(logf, axis=0, keepdims=True)
        lane_all = _lane(AUG_W) & (CHUNK - 1)
        bias = _spread3(-cum, e_ref) + jnp.where((lane_all >= 67) & (lane_all < 70), 1.0, 0.0)

        qb, kb = seg("qb"), seg("kb")
        write_heads(qb_ref, qb, lambda hd: q_ones, False, Q_SCALE)
        write_heads(kb_ref, kb, lambda hd: bias[:, CHUNK * hd:CHUNK * (hd + 1)], False, 1.0)
        write_heads(vb_ref, seg("vb"), lambda hd: q_ones, False, 1.0)

        rid = lax.broadcasted_iota(jnp.int32, (tm, CHUNK), 0)
        bnd_ref[...] = jnp.zeros_like(bnd_ref)
        bnd_ref[0, 0:1, :] = jnp.sum(jnp.where(rid == 0, cum, 0.0), axis=0, keepdims=True)
        bnd_ref[0, 1:2, :] = jnp.sum(jnp.where(rid == tm - 1, cum, 0.0), axis=0, keepdims=True)

        @pl.when(pl.program_id(0) == 0)
        def _():
            nrm_ref[...] = jnp.zeros_like(nrm_ref)

        for r_, nat in ((0, qb * Q_SCALE), (1, kb)):
            sq = _dot((nat * nat).astype(BF16), ind_ref[...])
            nrm_ref[r_:r_ + 1, :] = jnp.maximum(nrm_ref[r_:r_ + 1, :], jnp.max(sq, axis=0, keepdims=True))

        @pl.when(pl.program_id(0) == last)
        def _():
            _gather2_finish(late_src, late_dst, *late_sems)

    row_blk = lambda w: pl.BlockSpec((tm, w), lambda i: (i, 0))
    full = lambda a: pl.BlockSpec(a.shape, lambda i: (0,) * a.ndim)
    sds = lambda w, dt: jax.ShapeDtypeStruct((s, w), dt)
    out_shape = [sds(AUG_W, BF16), sds(KV_GROUPS * CHUNK, BF16), sds(KV_GROUPS * CHUNK, BF16),
                 sds(512, F32), sds(AUG_W, BF16), sds(AUG_W, BF16), sds(AUG_W, BF16), sds(512, F32),
                 sds(D_MODEL, F32), sds(D_MODEL, F32), sds(CHUNK, F32)]
    small = [jax.ShapeDtypeStruct((s // tm, 8, CHUNK), F32), jax.ShapeDtypeStruct((8, CHUNK), F32)]
    gathered = [jax.ShapeDtypeStruct((N_DEV,) + a.shape, a.dtype) for a in late_shards]
    outs = pl.pallas_call(
        body, name="fwd_proj", grid=(s // tm,),
        out_shape=[jax.ShapeDtypeStruct((D_MODEL, s), BF16)] + out_shape + small + gathered,
        in_specs=[row_blk(D_MODEL), full(mod), _resident(w_all), row_blk(CHUNK), row_blk(CHUNK), row_blk(CHUNK),
                  full(bf_row), full(emat), full(ind)] + [ANY_SPEC] * nl,
        out_specs=[pl.BlockSpec((D_MODEL, tm), lambda i: (0, i))] + [row_blk(o.shape[1]) for o in out_shape] + [
            pl.BlockSpec((1, 8, CHUNK), lambda i: (i, 0, 0)), pl.BlockSpec((8, CHUNK), lambda i: (0, 0))]
        + [ANY_SPEC] * nl,
        scratch_shapes=[pltpu.VMEM((8, CHUNK), F32)] + _exchange_sems(nl),
        compiler_params=_params(1))(x, mod, w_all, cos, sa, sb, bf_row, emat, ind, *late_shards)
    return outs[:14], outs[14:]


def _swa_fwd(q_aug, k_aug, v_aug, sink_rows, tq):
    s = q_aug.shape[0]
    r = tq // WINDOW
    gw = GROUP * CHUNK

    def body(q_ref, kc_ref, kp_ref, vc_ref, vp_ref, sink_ref, o_ref, qb_ref):
        i = pl.program_id(1)
        st = GROUP * WINDOW
        qloc = lax.broadcasted_iota(jnp.int32, (st, 2 * WINDOW), 0) & (WINDOW - 1)
        col = lax.broadcasted_iota(jnp.int32, (st, 2 * WINDOW), 1)
        band = (col > qloc) & (col <= qloc + WINDOW)
        head = jnp.right_shift(lax.broadcasted_iota(jnp.int32, (st, 1), 0), 7)
        sink = jnp.zeros((st, 1), F32)
        for g in range(GROUP):
            sink = jnp.where(head == g, jnp.max(sink_ref[0, g:g + 1, :], axis=1, keepdims=True), sink)
        lane = _lane(CHUNK)
        for sub in range(r):
            rows = slice(WINDOW * sub, WINDOW * (sub + 1))
            q = jnp.concatenate([q_ref[rows, CHUNK * g:CHUNK * (g + 1)] for g in range(GROUP)], axis=0)
            if sub == 0:
                k = jnp.concatenate([kp_ref[...], kc_ref[rows, :]], axis=0)
                v = jnp.concatenate([vp_ref[...], vc_ref[rows, :]], axis=0)
                valid = band & ((col >= WINDOW) | (i > 0))
            else:
                both = slice(WINDOW * (sub - 1), WINDOW * (sub + 1))
                k, v, valid = kc_ref[both, :], vc_ref[both, :], band
            sc = jnp.where(valid, _dot_nt(q, k), NEG)
            m = jnp.maximum(jnp.max(sc, axis=1, keepdims=True), sink)
            acc = _dot(jnp.exp((sc - m).astype(BF16)), v)
            denom = _lane_sum(acc, 64, 65) + jnp.exp(sink - m)
            out = acc * (1.0 / denom)
            aug = _place3(lane, 67, -(m + jnp.log(denom)), q.astype(F32)).astype(BF16)
            hrows = lambda a, g: a[WINDOW * g:WINDOW * (g + 1), :]
            for g in range(GROUP):
                qb_ref[rows, CHUNK * g:CHUNK * (g + 1)] = hrows(aug, g)
            for pp in range(GROUP // 2):
                o_ref[rows, CHUNK * pp:CHUNK * (pp + 1)] = _pair(hrows(out, 2 * pp), hrows(out, 2 * pp + 1))

    return pl.pallas_call(
        body, name="swa_fwd", grid=(KV_GROUPS, s // tq),
        out_shape=[jax.ShapeDtypeStruct((s, 512), F32), jax.ShapeDtypeStruct((s, AUG_W), BF16)],
        in_specs=[pl.BlockSpec((tq, gw), lambda kv, i: (i, kv)),
                  pl.BlockSpec((tq, CHUNK), lambda kv, i: (i, kv)),
                  pl.BlockSpec((WINDOW, CHUNK), lambda kv, i: (jnp.maximum(i * r - 1, 0), kv)),
                  pl.BlockSpec((tq, CHUNK), lambda kv, i: (i, kv)),
                  pl.BlockSpec((WINDOW, CHUNK), lambda kv, i: (jnp.maximum(i * r - 1, 0), kv)),
                  pl.BlockSpec((1, 8, CHUNK), lambda kv, i: (kv, 0, 0))],
        out_specs=[pl.BlockSpec((tq, GROUP * HEAD_DIM), lambda kv, i: (i, kv)),
                   pl.BlockSpec((tq, gw), lambda kv, i: (i, kv))],
        compiler_params=_params(2))(q_aug, k_aug, k_aug, v_aug, v_aug, sink_rows)


def _swa_bwd(k_aug, v_aug, q_bwd, do_aug, cos, sa, sb, sink_rows, tk):
    s = k_aug.shape[0]
    r = tk // WINDOW
    nt = s // tk
    nb = s // WINDOW

    def body(k_ref, v_ref, q_ref, qn_ref, do_ref, don_ref, cos_ref, sa_ref, sb_ref, sink_ref,
             da_ref, dsink_ref, carry_ref, acc_ref):
        j = pl.program_id(0)

        @pl.when(j == 0)
        def _():
            carry_ref[...] = jnp.zeros_like(carry_ref)
            dsink_ref[...] = jnp.zeros_like(dsink_ref)

        acc_ref[...] = jnp.zeros_like(acc_ref)
        acc_ref[0:WINDOW, :] = carry_ref[...]
        dk_col, dv_col = SEG["ka"][0] - SEG["qa"][0], SEG["va"][0] - SEG["qa"][0]
        st = GROUP * WINDOW
        key = lax.broadcasted_iota(jnp.int32, (WINDOW, 2 * st), 0)
        col = lax.broadcasted_iota(jnp.int32, (WINDOW, 2 * st), 1)
        qloc = col & (WINDOW - 1)
        band = ((col < st) & (key <= qloc)) | ((col >= st) & (key > qloc))
        lane = _lane(CHUNK)
        low = lane < HEAD_DIM
        cos_t, sa_t, sb_t = cos_ref[...], sa_ref[...], sb_ref[...]
        for kv, sub in [(kv, sub) for kv in range(KV_GROUPS) for sub in range(r)]:
            heads = [slice(CHUNK * (GROUP * kv + g), CHUNK * (GROUP * kv + g + 1)) for g in range(GROUP)]
            kvs = slice(CHUNK * kv, CHUNK * (kv + 1))
            rows = slice(WINDOW * sub, WINDOW * (sub + 1))
            nxt = slice(WINDOW * (sub + 1), WINDOW * (sub + 2))
            k, v = k_ref[rows, kvs], v_ref[rows, kvs]
            q_cur, do_cur = [q_ref[rows, cs] for cs in heads], [do_ref[rows, cs] for cs in heads]
            if sub < r - 1:
                q_nxt, do_nxt, valid = [q_ref[nxt, cs] for cs in heads], [do_ref[nxt, cs] for cs in heads], band
            else:
                q_nxt, do_nxt = [qn_ref[:, cs] for cs in heads], [don_ref[:, cs] for cs in heads]
                valid = band & ((col < st) | (j < nt - 1))
            q = jnp.concatenate(q_cur + q_nxt, axis=0)
            do = jnp.concatenate(do_cur + do_nxt, axis=0)
            pt = jnp.exp(jnp.where(valid, _dot_nt(k, q), NEG))
            ds = (pt * _dot_nt(v, do)).astype(BF16)
            dv = _dot(pt.astype(BF16), do)
            dk = _dot(ds, q)
            dq = _dot_tn(ds, k)
            for g, cs in enumerate(heads):
                acc_ref[rows, cs] += dq[WINDOW * g:WINDOW * (g + 1), :]
                dqn = dq[st + WINDOW * g:st + WINDOW * (g + 1), :]
                if sub < r - 1:
                    acc_ref[nxt, cs] += dqn
                else:
                    carry_ref[:, cs] = dqn
                sink = jnp.max(sink_ref[kv, g:g + 1, :], axis=1, keepdims=True)
                p_sink = jnp.exp(sink + _lane_sum(q_cur[g].astype(F32), 67, 70))
                term = jnp.sum(p_sink * _lane_sum(do_cur[g].astype(F32), 64, 67), axis=0, keepdims=True)
                dsink_ref[kv, g:g + 1, :] += jnp.broadcast_to(term, (1, CHUNK))
            da_ref[rows, dk_col + CHUNK * kv:dk_col + CHUNK * (kv + 1)] = _rope_inv(
                jnp.where(low, dk, 0.0), cos_t[rows, :], sa_t[rows, :], sb_t[rows, :]).astype(BF16)
            da_ref[rows, dv_col + CHUNK * kv:dv_col + CHUNK * (kv + 1)] = jnp.where(low, dv, 0.0).astype(BF16)
        for pp in range(N_HEADS // 2):
            d = _pair(acc_ref[:, CHUNK * 2 * pp:CHUNK * (2 * pp + 1)], acc_ref[:, CHUNK * (2 * pp + 1):CHUNK * (2 * pp + 2)])
            da_ref[:, CHUNK * pp:CHUNK * (pp + 1)] = (_rope_inv(d, cos_t, sa_t, sb_t) * Q_SCALE).astype(BF16)

    cur = lambda w: pl.BlockSpec((tk, w), lambda j: (j, 0))
    nxt = pl.BlockSpec((WINDOW, AUG_W), lambda j: (jnp.minimum((j + 1) * r, nb - 1), 0))
    whole = pl.BlockSpec((KV_GROUPS, 8, CHUNK), lambda j: (0, 0, 0))
    return pl.pallas_call(
        body, name="swa_bwd", grid=(nt,),
        out_shape=[jax.ShapeDtypeStruct((s, 1024), BF16), jax.ShapeDtypeStruct((KV_GROUPS, 8, CHUNK), F32)],
        in_specs=[cur(KV_GROUPS * CHUNK), cur(KV_GROUPS * CHUNK), cur(AUG_W), nxt, cur(AUG_W), nxt,
                  cur(CHUNK), cur(CHUNK), cur(CHUNK), whole],
        out_specs=[cur(1024), whole],
        scratch_shapes=[pltpu.VMEM((WINDOW, AUG_W), F32), pltpu.VMEM((tk, AUG_W), F32)],
        compiler_params=_params(1))(k_aug, v_aug, q_bwd, q_bwd, do_aug, do_aug, cos, sa, sb, sink_rows)


def _fox_fwd(first_key, q_aug, k_aug, v_aug, t):
    s = q_aug.shape[0]
    pw = 2 * CHUNK

    def body(lo_ref, q_ref, k_ref, v_ref, o_ref, qb_ref):
        i = pl.program_id(1)
        first = lo_ref[pl.program_id(0), i]
        row = lax.broadcasted_iota(jnp.int32, (t, t), 0)
        col = lax.broadcasted_iota(jnp.int32, (t, t), 1)
        lane = _lane(CHUNK)
        heads = (slice(0, CHUNK), slice(CHUNK, pw))
        qs = [q_ref[:, hs] for hs in heads]

        def step(jb, carry, masked):
            rows = pl.ds(pl.multiple_of(jb * t, t), t)
            scs = [_dot_nt(qs[hh], k_ref[rows, hs]) for hh, hs in enumerate(heads)]
            if masked:
                scs = [jnp.where(col <= row, sc, NEG) for sc in scs]
            m_news = [jnp.maximum(carry[2 * hh], jnp.max(scs[hh], axis=1, keepdims=True)) for hh in range(2)]
            ps = [jnp.exp((scs[hh] - m_news[hh]).astype(BF16)) for hh in range(2)]
            new = []
            for hh, hs in enumerate(heads):
                m, acc = carry[2 * hh], carry[2 * hh + 1]
                new += [m_news[hh], jnp.exp(m - m_news[hh]) * acc + _dot(ps[hh], v_ref[rows, hs])]
            return tuple(new)

        init = (jnp.full((t, 1), NEG, F32), jnp.zeros((t, CHUNK), F32)) * 2
        carry = step(i, lax.fori_loop(first, i, lambda jb, c: step(jb, c, False), init), True)
        outs = []
        for hh, hs in enumerate(heads):
            m, acc = carry[2 * hh], carry[2 * hh + 1]
            denom = _lane_sum(acc, 64, 65)
            outs.append(acc * (1.0 / denom))
            lse = m + jnp.log(denom)
            qb_ref[:, hs] = _place3(lane, 67, -lse, qs[hh].astype(F32)).astype(BF16)
        o_ref[...] = _pair(outs[0], outs[1])

    return pl.pallas_call(
        body, name="fox_fwd", grid=(N_HEADS // 2, s // t),
        out_shape=[jax.ShapeDtypeStruct((s, 512), F32), jax.ShapeDtypeStruct((s, AUG_W), BF16)],
        in_specs=[pl.BlockSpec(memory_space=pltpu.SMEM),
                  pl.BlockSpec((t, pw), lambda p, i: (i, p)),
                  pl.BlockSpec((s, pw), lambda p, i: (0, p)),
                  pl.BlockSpec((s, pw), lambda p, i: (0, p))],
        out_specs=[pl.BlockSpec((t, CHUNK), lambda p, i: (i, p)), pl.BlockSpec((t, pw), lambda p, i: (i, p))],
        compiler_params=_params(2))(first_key, q_aug, k_aug, v_aug)


def _fox_block_ranges(bounds, norms, r):
    cum_first = bounds[0::r, 0, :N_HEADS]
    cum_last = bounds[r - 1::r, 1, :N_HEADS]
    n = cum_first.shape[0]
    reach = 2.0 * 1.02 * jnp.sqrt(norms[0, :N_HEADS] * norms[1, :N_HEADS]) + UNDERFLOW
    blk = jnp.arange(n, dtype=jnp.int32)
    decay = cum_last[None, :, :] - cum_first[:, None, :]
    skip = (decay > reach[None, None, :]) & (blk[None, :, None] < blk[:, None, None])
    first = jnp.min(jnp.where(skip, n, blk[None, :, None]), axis=1)
    first = jnp.minimum(first[:, 0::2], first[:, 1::2]).T
    needed = (first[:, :, None] <= blk[None, None, :]) & (blk[None, :, None] >= blk[None, None, :])
    last = jnp.max(jnp.where(needed, blk[None, :, None], blk[None, None, :]), axis=1)
    return first.astype(jnp.int32), last.astype(jnp.int32)


def _fox_bwd(last_query, k_aug, v_aug, q_bwd, do_aug, t):
    s = k_aug.shape[0]
    n = s // t
    pw = 2 * CHUNK

    def body(hi_ref, k_ref, v_ref, q_ref, do_ref, dk_ref, dv_ref, dck_ref, dq_ref, dcq_ref, dq_scr):
        j = pl.program_id(1)
        last = hi_ref[pl.program_id(0), j]

        @pl.when(j == 0)
        def _():
            dq_scr[...] = jnp.zeros_like(dq_scr)

        row = lax.broadcasted_iota(jnp.int32, (t, t), 0)
        col = lax.broadcasted_iota(jnp.int32, (t, t), 1)
        lane = _lane(CHUNK)
        heads = (slice(0, CHUNK), slice(CHUNK, pw))
        ks = [k_ref[:, hs] for hs in heads]
        vs = [v_ref[:, hs] for hs in heads]

        def step(ib, carry, masked):
            rows = pl.ds(pl.multiple_of(ib * t, t), t)
            new = []
            for hh, hs in enumerate(heads):
                dv, dk = carry[2 * hh], carry[2 * hh + 1]
                q, do = q_ref[rows, hs], do_ref[rows, hs]
                st = _dot_nt(ks[hh], q)
                if masked:
                    st = jnp.where(row <= col, st, NEG)
                pt = jnp.exp(st)
                ds = (pt * _dot_nt(vs[hh], do)).astype(BF16)
                new += [dv + _dot(pt.astype(BF16), do), dk + _dot(ds, q)]
                dq_scr[rows, hs] += _dot_tn(ds, ks[hh])
            return tuple(new)

        zero = jnp.zeros((t, CHUNK), F32)
        carry = lax.fori_loop(j + 1, last + 1, lambda ib, c: step(ib, c, False), step(j, (zero,) * 4, True))
        dvs, dks = (carry[0], carry[2]), (carry[1], carry[3])
        dk_ref[...] = _pair(dks[0], dks[1]).astype(BF16)
        dv_ref[...] = _pair(dvs[0], dvs[1]).astype(BF16)
        dck_ref[0] = jnp.where(lane == 0, pltpu.roll(dks[0], 64, 1),
                               jnp.where(lane == 1, pltpu.roll(dks[1], 65, 1), 0.0))

        @pl.when(j == n - 1)
        def _():
            for ib in range(n):
                rows = slice(t * ib, t * (ib + 1))
                d0, d1 = dq_scr[rows, 0:CHUNK], dq_scr[rows, CHUNK:pw]
                dq_ref[rows, :] = (_pair(d0, d1) * Q_SCALE).astype(BF16)
                dcq_ref[0, rows, :] = jnp.where(lane == 0, pltpu.roll(d0, CHUNK - 67, 1),
                                                jnp.where(lane == 1, pltpu.roll(d1, CHUNK - 66, 1), 0.0))

    return pl.pallas_call(
        body, name="fox_bwd", grid=(N_HEADS // 2, n),
        out_shape=[jax.ShapeDtypeStruct((s, 512), BF16), jax.ShapeDtypeStruct((s, 512), BF16),
                   jax.ShapeDtypeStruct((N_HEADS // 2, s, CHUNK), F32), jax.ShapeDtypeStruct((s, 512), BF16),
                   jax.ShapeDtypeStruct((N_HEADS // 2, s, CHUNK), F32)],
        in_specs=[pl.BlockSpec(memory_space=pltpu.SMEM),
                  pl.BlockSpec((t, pw), lambda p, j: (j, p)), pl.BlockSpec((t, pw), lambda p, j: (j, p)),
                  pl.BlockSpec((s, pw), lambda p, j: (0, p)), pl.BlockSpec((s, pw), lambda p, j: (0, p))],
        out_specs=[pl.BlockSpec((t, CHUNK), lambda p, j: (j, p)), pl.BlockSpec((t, CHUNK), lambda p, j: (j, p)),
                   pl.BlockSpec((1, t, CHUNK), lambda p, j: (p, j, 0)), pl.BlockSpec((s, CHUNK), lambda p, j: (0, p)),
                   pl.BlockSpec((1, s, CHUNK), lambda p, j: (p, 0, 0))],
        scratch_shapes=[pltpu.VMEM((s, pw), F32)],
        compiler_params=_params(2))(last_query, k_aug, v_aug, q_bwd, do_aug)


def _fgate_bwd(dcq, dck, xf, tm):
    s = xf.shape[0]
    nt = s // tm

    def body(dq_ref, dc_ref, xf_ref, df_ref, dbf_ref, carry_ref):
        @pl.when(pl.program_id(0) == 0)
        def _():
            carry_ref[...] = jnp.zeros_like(carry_ref)
            dbf_ref[...] = jnp.zeros_like(dbf_ref)

        row = lax.broadcasted_iota(jnp.int32, (tm, tm), 0)
        col = lax.broadcasted_iota(jnp.int32, (tm, tm), 1)
        tri = jnp.where(col >= row, 1.0, 0.0).astype(BF16)
        dcum = dq_ref[0] - dc_ref[0]
        for p in range(1, N_HEADS // 2):
            dcum = dcum + pltpu.roll(dq_ref[p] - dc_ref[p], 2 * p, 1)
        hi, mid, lo = _split3(dcum)
        dlogf = _dot(tri, hi.astype(BF16)) + _dot(tri, mid.astype(BF16)) + _dot(tri, lo.astype(BF16))
        dlogf = dlogf + carry_ref[0:1, :]
        carry_ref[0:1, :] += jnp.sum(dcum, axis=0, keepdims=True)
        df = dlogf / (1.0 + jnp.exp(xf_ref[...]))
        df_ref[...] = df.astype(BF16)
        dbf_ref[0:1, :] += jnp.sum(df, axis=0, keepdims=True)

    rev = pl.BlockSpec((tm, CHUNK), lambda i: (nt - 1 - i, 0))
    rev4 = pl.BlockSpec((N_HEADS // 2, tm, CHUNK), lambda i: (0, nt - 1 - i, 0))
    return pl.pallas_call(
        body, name="fgate_bwd", grid=(nt,),
        out_shape=[jax.ShapeDtypeStruct((s, CHUNK), BF16), jax.ShapeDtypeStruct((8, CHUNK), F32)],
        in_specs=[rev4, rev4, rev], out_specs=[rev, pl.BlockSpec((8, CHUNK), lambda i: (0, 0))],
        scratch_shapes=[pltpu.VMEM((8, CHUNK), F32)],
        compiler_params=_params(1))(dcq, dck, xf)


def _post(x, tgt, att_a, att_b, za, zb, ga, gb, woa_t, wob_t, wout, vec, ind, emat, tm):
    s = x.shape[0]

    def body(x_ref, t_ref, aa_ref, ab_ref, za_ref, zb_ref, ga_ref, gb_ref, woat_ref, wobt_ref, wout_ref,
             vec_ref, ind_ref, e_ref,
             dx1_ref, ua_ref, ub_ref, mg_ref, do_ref, dya_ref, dyb_ref, doa_ref, dob_ref, dzg_ref, acc_ref):
        @pl.when(pl.program_id(0) == 0)
        def _():
            acc_ref[...] = jnp.zeros_like(acc_ref)

        gate, gfin = vec_ref[0:1, :], vec_ref[1:2, :]
        inv_d = 1.0 / D_MODEL

        def branch_fwd(att_ref, z_ref, w_ref, u_ref):
            att, z = att_ref[...], z_ref[...]
            sz = _sigmoid(z)
            silu = z * sz
            u = (att * silu).astype(BF16)
            u_ref[...] = u.T
            return att, z, sz, silu, _dot_nt(u, w_ref[...])

        att_a, z_a, sz_a, silu_a, y_a = branch_fwd(aa_ref, za_ref, woat_ref, ua_ref)
        att_b, z_b, sz_b, silu_b, y_b = branch_fwd(ab_ref, zb_ref, wobt_ref, ub_ref)
        sg_a, sg_b = _sigmoid(ga_ref[...]), _sigmoid(gb_ref[...])
        merged = (sg_a * y_a + sg_b * y_b).astype(BF16)
        mg_ref[...] = merged.T
        o = _dot(merged, wout_ref[...])
        x1 = x_ref[...] + gate * o
        rstd = lax.rsqrt(jnp.mean(x1 * x1, axis=-1, keepdims=True) + NORM_EPS)
        xh = x1 * rstd
        diff = xh * gfin - t_ref[...]
        acc_ref[2:3, :] += (0.5 * inv_d) * jnp.sum(diff * diff, axis=0, keepdims=True)
        dy = diff * inv_d
        acc_ref[1:2, :] += jnp.sum(dy * xh, axis=0, keepdims=True)
        dyg = dy * gfin
        dx1 = rstd * (dyg - xh * jnp.mean(dyg * xh, axis=-1, keepdims=True))
        dx1_ref[...] = dx1
        acc_ref[0:1, :] += jnp.sum(dx1 * o, axis=0, keepdims=True)
        d_o = (dx1 * gate).astype(BF16)
        do_ref[...] = d_o
        dmg = _dot_nt(d_o, wout_ref[...])

        lane = _lane(CHUNK)
        low = lane < HEAD_DIM
        zg0 = SEG["za"][0]

        def branch_bwd(sg, y, att, z, sz, silu, wt_ref, dy_ref, g_name, z_name, doaug_ref):
            dyb = (dmg * sg).astype(BF16)
            dy_ref[...] = dyb
            g_off, z_off = SEG[g_name][0] - zg0, SEG[z_name][0] - zg0
            dzg_ref[:, g_off:g_off + D_MODEL] = (dmg * y * sg * (1.0 - sg)).astype(BF16)
            du = _dot(dyb, wt_ref[...])
            datt = du * silu
            dzg_ref[:, z_off:z_off + 512] = (du * att * (sz * (1.0 + z * (1.0 - sz)))).astype(BF16)
            extra = _spread3(-_dot_split(datt * att, ind_ref[...]), e_ref)
            for p in range(N_HEADS // 2):
                c = datt[:, CHUNK * p:CHUNK * (p + 1)]
                for hh in range(2):
                    hd = 2 * p + hh
                    src = c if hh == 0 else pltpu.roll(c, HEAD_DIM, 1)
                    doaug_ref[:, CHUNK * hd:CHUNK * (hd + 1)] = jnp.where(
                        low, src, extra[:, CHUNK * hd:CHUNK * (hd + 1)]).astype(BF16)

        branch_bwd(sg_a, y_a, att_a, z_a, sz_a, silu_a, woat_ref, dya_ref, "ga", "za", doa_ref)
        branch_bwd(sg_b, y_b, att_b, z_b, sz_b, silu_b, wobt_ref, dyb_ref, "gb", "zb", dob_ref)

    row_blk = lambda w: pl.BlockSpec((tm, w), lambda i: (i, 0))
    full = lambda a: pl.BlockSpec(a.shape, lambda i: (0,) * a.ndim)
    sds = lambda w, dt: jax.ShapeDtypeStruct((s, w), dt)
    tds = lambda w: jax.ShapeDtypeStruct((w, s), BF16)
    col_blk = lambda w: pl.BlockSpec((w, tm), lambda i: (0, i))
    out_shape = [sds(D_MODEL, F32), tds(512), tds(512), tds(D_MODEL), sds(D_MODEL, BF16),
                 sds(D_MODEL, BF16), sds(D_MODEL, BF16), sds(AUG_W, BF16), sds(AUG_W, BF16), sds(3072, BF16),
                 jax.ShapeDtypeStruct((8, D_MODEL), F32)]
    ins = [x, tgt, att_a, att_b, za, zb, ga, gb, woa_t, wob_t, wout, vec, ind, emat]
    in_specs = [row_blk(a.shape[1]) for a in ins[:8]] + [full(a) for a in ins[8:]]
    out_specs = ([row_blk(D_MODEL), col_blk(512), col_blk(512), col_blk(D_MODEL)]
                 + [row_blk(o.shape[1]) for o in out_shape[4:-1]] + [pl.BlockSpec((8, D_MODEL), lambda i: (0, 0))])
    return pl.pallas_call(body, name="post", grid=(s // tm,), out_shape=out_shape, in_specs=in_specs,
                          out_specs=out_specs, compiler_params=_params(1))(*ins)


def _bwd_pre(dproj, w_all, x, dx1, mod, tm, chip_halves, modes):
    s = x.shape[0]
    ng, nh = len(DPROJ_GROUPS), len(chip_halves)
    last = s // tm - 1

    def body(*refs):
        dp_refs = refs[:ng]
        wt_ref, x_ref, dx1_ref, mod_ref = refs[ng:ng + 4]
        send_refs = refs[ng + 4:ng + 4 + nh]
        gx_ref, acc_ref = refs[ng + 4 + nh:ng + 6 + nh]
        recv_refs = refs[ng + 6 + nh:ng + 6 + 2 * nh]
        sems = refs[ng + 6 + 2 * nh:]

        @pl.when(pl.program_id(0) == 0)
        def _():
            acc_ref[...] = jnp.zeros_like(acc_ref)
            _exchange_start(_exchange_copies(send_refs, recv_refs, modes, *sems))

        dh = None
        for dp_ref, (_, off, w) in zip(dp_refs, DPROJ_GROUPS):
            part = _dot_nt(dp_ref[...], wt_ref[:, off:off + w])
            dh = part if dh is None else dh + part
        xv = x_ref[...]
        rstd = lax.rsqrt(jnp.mean(xv * xv, axis=-1, keepdims=True) + NORM_EPS)
        xh = xv * rstd
        gn = mod_ref[0:1, :]
        acc_ref[0:1, :] += jnp.sum(dh, axis=0, keepdims=True)
        acc_ref[1:2, :] += jnp.sum(dh * (xh * gn), axis=0, keepdims=True)
        dn = dh * (1.0 + mod_ref[1:2, :])
        acc_ref[2:3, :] += jnp.sum(dn * xh, axis=0, keepdims=True)
        dxh = dn * gn
        gx_ref[...] = dx1_ref[...] + rstd * (dxh - xh * jnp.mean(dxh * xh, axis=-1, keepdims=True))

        @pl.when(pl.program_id(0) == last)
        def _():
            _exchange_wait(_exchange_copies(send_refs, recv_refs, modes, *sems))

    row_blk = lambda w: pl.BlockSpec((tm, w), lambda i: (i, 0))
    full = lambda a: pl.BlockSpec(a.shape, lambda i: (0,) * a.ndim)
    outs = pl.pallas_call(
        body, name="bwd_pre", grid=(s // tm,),
        out_shape=[jax.ShapeDtypeStruct((s, D_MODEL), F32), jax.ShapeDtypeStruct((8, D_MODEL), F32)]
        + _exchange_out_shapes(chip_halves, modes),
        in_specs=[row_blk(w) for _, _, w in DPROJ_GROUPS] + [_resident(w_all), row_blk(D_MODEL), row_blk(D_MODEL), full(mod)]
        + [ANY_SPEC] * nh,
        out_specs=[row_blk(D_MODEL), pl.BlockSpec((8, D_MODEL), lambda i: (0, 0))] + [ANY_SPEC] * nh,
        scratch_shapes=_exchange_sems(nh),
        compiler_params=_params(1))(*dproj, w_all, x, dx1, mod, *chip_halves)
    return outs[0], outs[1], outs[2:]


def _accumulate_tokens(a, b_ref, acc_ref, o_ref, last):
    @pl.when(pl.program_id(0) == 0)
    def _():
        acc_ref[...] = jnp.zeros_like(acc_ref)

    acc_ref[...] += _dot(a, b_ref[...])

    @pl.when(pl.program_id(0) == last)
    def _():
        o_ref[...] = acc_ref[...].astype(BF16)


def _token_matmul_multi(a_t, bs, tk, name):
    m, s = a_t.shape
    nb = len(bs)
    last = s // tk - 1

    def body(*refs):
        a_ref, b_refs, o_refs, acc_refs = refs[0], refs[1:1 + nb], refs[1 + nb:1 + 2 * nb], refs[1 + 2 * nb:]
        a = a_ref[...]
        for b_ref, o_ref, acc_ref in zip(b_refs, o_refs, acc_refs):
            _accumulate_tokens(a, b_ref, acc_ref, o_ref, last)

    return pl.pallas_call(
        body, name=name, grid=(s // tk,),
        out_shape=[jax.ShapeDtypeStruct((m, b.shape[1]), BF16) for b in bs],
        in_specs=[pl.BlockSpec((m, tk), lambda k: (0, k))] + [pl.BlockSpec((tk, b.shape[1]), lambda k: (k, 0)) for b in bs],
        out_specs=[pl.BlockSpec((m, b.shape[1]), lambda k: (0, 0)) for b in bs],
        scratch_shapes=[pltpu.VMEM((m, b.shape[1]), F32) for b in bs],
        compiler_params=_params(1))(a_t, *bs)


def _adam_math(g, w, m, v):
    m2 = ADAM_B1 * m + (1.0 - ADAM_B1) * g
    v2 = ADAM_B2 * v + (1.0 - ADAM_B2) * (g * g)
    delta = -ADAM_LR * ((m2 / ADAM_C1) / (jnp.sqrt(v2 / ADAM_C2) + ADAM_EPS) + ADAM_WD * w)
    return delta, m2, v2


def _adamw_parts(full, parts, others, w, m, v, tr, name):
    _, rws, cols = w.shape

    def body(oth_ref, f_ref, p_ref, w_ref, m_ref, v_ref, g_ref, d_ref, m2_ref, v2_ref):
        g = None
        for chip in range(N_CHIPS):
            part = (1.0 - oth_ref[chip]) * f_ref[chip] + oth_ref[chip] * p_ref[chip].astype(F32)
            g = part if g is None else g + part
        g_ref[0] = g
        d_ref[0], m2_ref[0], v2_ref[0] = _adam_math(g, w_ref[0], m_ref[0], v_ref[0])

    blk = pl.BlockSpec((1, tr, cols), lambda i: (0, i, 0))
    slots = pl.BlockSpec((N_CHIPS, tr, cols), lambda i: (0, i, 0))
    o = jax.ShapeDtypeStruct((1, rws, cols), F32)
    return pl.pallas_call(
        body, name=name, grid=(rws // tr,), out_shape=[o, o, o, o],
        in_specs=[pl.BlockSpec(memory_space=pltpu.SMEM), slots, slots, blk, blk, blk],
        out_specs=[blk, blk, blk, blk], compiler_params=_params(1))(others, full, parts, w, m, v)


def _adamw_devices(parts, w, m, v, tr, name):
    _, rws, cols = w.shape

    def body(p_ref, w_ref, m_ref, v_ref, g_ref, d_ref, m2_ref, v2_ref):
        g = p_ref[0].astype(F32)
        for dev in range(1, N_DEV):
            g = g + p_ref[dev].astype(F32)
        g_ref[0] = g
        d_ref[0], m2_ref[0], v2_ref[0] = _adam_math(g, w_ref[0], m_ref[0], v_ref[0])

    blk = pl.BlockSpec((1, tr, cols), lambda i: (0, i, 0))
    o = jax.ShapeDtypeStruct((1, rws, cols), F32)
    return pl.pallas_call(
        body, name=name, grid=(rws // tr,), out_shape=[o, o, o, o],
        in_specs=[pl.BlockSpec((N_DEV, tr, cols), lambda i: (0, i, 0)), blk, blk, blk],
        out_specs=[blk, blk, blk, blk], compiler_params=_params(1))(parts, w, m, v)


def _adamw_ada(c_lanes, d_rows, w, m, v, tr):
    _, rws, cols = w.shape

    def body(c_ref, d_ref, w_ref, m_ref, v_ref, g_ref, dl_ref, m2_ref, v2_ref):
        for k in range(cols // CHUNK):
            cs = slice(CHUNK * k, CHUNK * (k + 1))
            g = c_ref[0] * d_ref[0:1, cs]
            for b in range(1, N_DEV):
                g = g + c_ref[b] * d_ref[b:b + 1, cs]
            g_ref[0, :, cs] = g
            dl_ref[0, :, cs], m2_ref[0, :, cs], v2_ref[0, :, cs] = _adam_math(
                g, w_ref[0, :, cs], m_ref[0, :, cs], v_ref[0, :, cs])

    blk = pl.BlockSpec((1, tr, cols), lambda i: (0, i, 0))
    o = jax.ShapeDtypeStruct((1, rws, cols), F32)
    return pl.pallas_call(
        body, name="adamw_ada", grid=(rws // tr,), out_shape=[o, o, o, o],
        in_specs=[pl.BlockSpec((N_DEV, tr, CHUNK), lambda i: (0, i, 0)), pl.BlockSpec((N_DEV, cols), lambda i: (0, 0)),
                  blk, blk, blk],
        out_specs=[blk, blk, blk, blk], compiler_params=_params(1))(c_lanes, d_rows, w, m, v)


SMALL = (("b_ada", 0, 0, 3 * D_MODEL), ("g_norm", 1, 0, D_MODEL), ("b_f", 3, 0, N_HEADS), ("sinks", 4, 0, N_HEADS),
         ("g_final", 1, D_MODEL, D_MODEL))
LOSS_ROW = 2


def _pack_small(pre_acc, post_acc, dbf_acc, dsink):
    def body(pre_ref, post_ref, dbf_ref, dsink_ref, o_ref):
        o_ref[...] = jnp.zeros_like(o_ref)
        d = D_MODEL
        o_ref[0:1, 0:d], o_ref[0:1, d:2 * d], o_ref[0:1, 2 * d:3 * d] = pre_ref[0:1, :], pre_ref[1:2, :], post_ref[0:1, :]
        o_ref[1:2, 0:d], o_ref[1:2, d:2 * d] = pre_ref[2:3, :], post_ref[1:2, :]
        o_ref[LOSS_ROW:LOSS_ROW + 1, 0:d] = post_ref[2:3, :]
        o_ref[3:4, 0:CHUNK] = dbf_ref[0:1, :]
        lane = _lane(CHUNK)
        sinks = jnp.zeros((1, CHUNK), F32)
        for kv in range(KV_GROUPS):
            for g in range(GROUP):
                sinks = jnp.where(lane == GROUP * kv + g, dsink_ref[kv, g:g + 1, :], sinks)
        o_ref[4:5, 0:CHUNK] = sinks

    return pl.pallas_call(body, name="pack_small", out_shape=jax.ShapeDtypeStruct((8, 3 * D_MODEL), F32),
                          compiler_params=_params(0))(pre_acc, post_acc, dbf_acc, dsink)


def _adamw_small(packs, params):
    n = len(SMALL)

    def body(*refs):
        p_ref, wmv = refs[0], refs[1:1 + 3 * n]
        outs, loss_ref, tot_ref = refs[1 + 3 * n:1 + 7 * n], refs[1 + 7 * n], refs[2 + 7 * n]
        tot = p_ref[0]
        for dev in range(1, N_DEV):
            tot = tot + p_ref[dev]
        tot_ref[...] = tot
        for i, (_, row, lo, w) in enumerate(SMALL):
            g = tot_ref[row:row + 1, lo:lo + w]
            o = outs[4 * i:4 * i + 4]
            o[0][...] = g
            o[1][...], o[2][...], o[3][...] = _adam_math(g, wmv[3 * i][...], wmv[3 * i + 1][...], wmv[3 * i + 2][...])
        loss_ref[...] = jnp.broadcast_to(jnp.sum(tot_ref[LOSS_ROW:LOSS_ROW + 1, 0:D_MODEL], axis=1, keepdims=True),
                                         loss_ref.shape)

    flat = [a for wmv in params for a in wmv]
    out_shape = [jax.ShapeDtypeStruct(wmv[0].shape, F32) for wmv in params for _ in range(4)]
    res = pl.pallas_call(body, name="adamw_small", out_shape=out_shape + [jax.ShapeDtypeStruct((8, CHUNK), F32)],
                         scratch_shapes=[pltpu.VMEM((8, 3 * D_MODEL), F32)],
                         compiler_params=_params(0))(packs, *flat)
    return [tuple(res[4 * i:4 * i + 4]) for i in range(n)], res[4 * n]


def _tile(s, want):
    return min(s, want)


def _layout_pieces():
    orig = {"qa": 0, "ka": 512, "va": 640, "za": 768, "qb": 1280, "kb": 1792, "vb": 2304, "f": 2816, "zb": 2824,
            "ga": 3336, "gb": 4360}
    pieces = []
    for name, (off, w) in SEG.items():
        if name in ("ka", "va"):
            pieces += [(orig[name] + HEAD_DIM * kv, orig[name] + HEAD_DIM * (kv + 1), off + CHUNK * kv) for kv in range(KV_GROUPS)]
        else:
            pieces.append((orig[name], orig[name] + (N_HEADS if name == "f" else w), off))
    return pieces


def _assemble_w_all(win_g):
    cols, pos = [], 0
    for lo, hi, new in sorted(_layout_pieces(), key=lambda t: t[2]):
        if new > pos:
            cols.append(jnp.zeros((D_MODEL, new - pos), win_g.dtype))
        col = lo
        while col < hi:
            dev = col // IN_SHARD
            end = min(hi, (dev + 1) * IN_SHARD)
            cols.append(win_g[dev, :, col - dev * IN_SHARD:end - dev * IN_SHARD])
            col = end
        pos = new + hi - lo
    cols.append(jnp.zeros((D_MODEL, PROJ_W - pos), win_g.dtype))
    return jnp.concatenate(cols, axis=1)


def _grad_slot(dw_groups, dev):
    lo_d, hi_d = dev * IN_SHARD, (dev + 1) * IN_SHARD
    cols = []
    for lo, hi, new in sorted(_layout_pieces()):
        a, b = max(lo, lo_d), min(hi, hi_d)
        if a < b:
            arr, off = next((g, o) for g, (_, o, w) in zip(dw_groups, DPROJ_GROUPS) if o <= new < o + w)
            cols.append(arr[:, new - off + a - lo:new - off + b - lo])
    return jnp.concatenate(cols, axis=1)


def kernel(x, c, positions, w_ada, b_ada, g_norm, w_in, b_f, sinks, w_o_swa, w_o_fox, w_out, g_final, loss_target, m_w_ada, m_b_ada, m_g_norm, m_w_in, m_b_f, m_sinks, m_w_o_swa, m_w_o_fox, m_w_out, m_g_final, v_w_ada, v_b_ada, v_g_norm, v_w_in, v_b_f, v_sinks, v_w_o_swa, v_w_o_fox, v_w_out, v_g_final):
    s = x.shape[1]
    tm = _tile(s, 256)
    ta = _tile(s, 512)
    me = 4 * lax.axis_index("x") + 2 * lax.axis_index("y") + lax.axis_index("c")
    x2, tgt = x[0], loss_target[0]

    inv_freq = np.power(np.float32(ROPE_THETA), -np.arange(0, HEAD_DIM, 2, dtype=np.float32) / HEAD_DIM)
    inv_freq = jnp.asarray(np.tile(inv_freq, CHUNK // (HEAD_DIM // 2))[None, :], F32)
    (cos, sa, sb), (win_g,), c_all = _rope_tables_and_gather(
        positions.reshape(s, 1).astype(F32), inv_freq, tm, [w_in[0].astype(BF16)], jnp.broadcast_to(c, (8, D_MODEL)))
    c_all = c_all[:, 0, :]
    b_shard = lax.dynamic_slice(b_ada, (0, me * ADA_SHARD), (1, ADA_SHARD))
    ada_part = _ada_fwd(c_all, w_ada[0], b_shard)
    (ada_g,) = _exchange([ada_part], ["gather"], "gather_ada")
    ada = lax.dynamic_index_in_dim(ada_g, me, axis=1, keepdims=False).reshape(3 * D_MODEL)
    shift, scale, gate = ada[:D_MODEL], ada[D_MODEL:2 * D_MODEL], ada[2 * D_MODEL:]
    w_all = _assemble_w_all(win_g)

    zrow = jnp.zeros((1, D_MODEL), F32)
    mod = jnp.concatenate([g_norm, scale[None], shift[None], zrow, zrow, zrow, zrow, zrow], axis=0)
    bf_row = jnp.pad(b_f, ((0, 0), (0, CHUNK - N_HEADS)))
    emat_np = np.zeros((3 * CHUNK, AUG_W), np.float32)
    ind_np = np.zeros((512, CHUNK), np.float32)
    for hd in range(N_HEADS):
        for part in range(3):
            emat_np[CHUNK * part + hd, CHUNK * hd + 64 + part] = 1.0
        ind_np[HEAD_DIM * hd:HEAD_DIM * (hd + 1), hd] = 1.0
    emat, ind = jnp.asarray(emat_np, BF16), jnp.asarray(ind_np, BF16)
    sink_rows = jnp.broadcast_to(jnp.pad(sinks.reshape(KV_GROUPS, GROUP), ((0, 0), (0, 8 - GROUP)))[:, :, None],
                                 (KV_GROUPS, 8, CHUNK))

    (h_t, qa, ka, va, za, qb, kb, vb, zb, ga, gb, xf, bounds, norms), (woa_g, wob_g, wout_g) = _fwd_proj(
        x2, mod, w_all, cos, sa, sb, bf_row, emat, ind, ta,
        [w_o_swa[0].T.astype(BF16), w_o_fox[0].T.astype(BF16), w_out[0].astype(BF16)])
    woa_t, wob_t, wout = woa_g.reshape(D_MODEL, 512), wob_g.reshape(D_MODEL, 512), wout_g.reshape(D_MODEL, D_MODEL)
    first_key, last_query = _fox_block_ranges(bounds, norms, 1)
    att_a, qa_bwd = _swa_fwd(qa, ka, va, sink_rows, _tile(s, 1024))
    att_b, qb_bwd = _fox_fwd(first_key, qb, kb, vb, ta)
    vec = jnp.concatenate([gate[None], g_final[None], zrow, zrow, zrow, zrow, zrow, zrow], axis=0)
    (dx1, ua_t, ub_t, merged_t, d_o, dya, dyb, doa, dob, dzg, post_acc) = _post(
        x2, tgt, att_a, att_b, za, zb, ga, gb, woa_t, wob_t, wout, vec, ind, emat, tm)

    da, dsink = _swa_bwd(ka, va, qa_bwd, doa, cos, sa, sb, sink_rows, ta)
    dkb, dvb, dck, dqb, dcq = _fox_bwd(last_query, kb, vb, qb_bwd, dob, ta)
    dfb, dbf_acc = _fgate_bwd(dcq, dck, xf, ta)
    dproj = [da, dqb, dkb, dvb, dzg, dfb]
    tw = _tile(s, 1024)
    dw_zg, = _token_matmul_multi(h_t, [dzg], tw, "dw_in_zg")
    dw_a, dw_qb, dw_kb, dw_vb, dw_f = _token_matmul_multi(h_t, [da, dqb, dkb, dvb, dfb], tw, "dw_in_rest")
    dw_all = [dw_a, dw_qb, dw_kb, dw_vb, dw_zg, dw_f]

    core, chip = lax.axis_index("c"), 2 * lax.axis_index("x") + lax.axis_index("y")
    win_slots = jnp.stack([jnp.stack([_grad_slot(dw_all, 2 * ch + co) for ch in range(N_CHIPS)]) for co in range(2)])
    (dwoa, dwob, dwout), (win_theirs,) = _token_matmuls_and_swap(
        [(ua_t, dya), (ub_t, dyb), (merged_t, d_o)], tw, [win_slots])
    win_full, win_half = _chip_partial(jnp.reshape(core, (1,)).astype(jnp.int32), win_slots, win_theirs, 256,
                                       "chip_partial_w_in")
    col_slots = lambda g: g.reshape(512, N_DEV, 128).transpose(1, 0, 2)
    grad_x, pre_acc, (p_win, p_woa, p_wob, p_wout) = _bwd_pre(
        dproj, w_all, x2, dx1, mod, ta,
        [win_half, col_slots(dwoa), col_slots(dwob), dwout.reshape(N_DEV, 128, D_MODEL)],
        ["chips", "devices", "devices", "devices"])
    (packs,) = _exchange([_pack_small(pre_acc, post_acc, dbf_acc, dsink)], ["gather"], "gather_small_grads")
    others = jnp.where(jnp.arange(N_CHIPS) == chip, 0.0, 1.0).astype(F32)

    g_win, d_win, m_win, v_win = _adamw_parts(win_full, p_win, others, w_in, m_w_in, v_w_in, 128, "adamw_w_in")
    g_woa, d_woa, m_woa, v_woa = _adamw_devices(p_woa, w_o_swa, m_w_o_swa, v_w_o_swa, 128, "adamw_w_o_swa")
    g_wob, d_wob, m_wob, v_wob = _adamw_devices(p_wob, w_o_fox, m_w_o_fox, v_w_o_fox, 128, "adamw_w_o_fox")
    g_wout, d_wout, m_wout, v_wout = _adamw_devices(p_wout, w_out, m_w_out, v_w_out, 128, "adamw_w_out")
    d_ada_rows = lax.dynamic_slice(packs[:, 0, :], (0, me * ADA_SHARD), (N_DEV, ADA_SHARD))
    c_lanes = jnp.broadcast_to(c_all[:, :, None], (N_DEV, D_MODEL, CHUNK))
    g_wada, d_wada, m_wada, v_wada = _adamw_ada(c_lanes, d_ada_rows, w_ada, m_w_ada, v_w_ada, 256)

    row = lambda a: a.reshape(1, D_MODEL)
    small, loss_rows = _adamw_small(packs, [(b_ada, m_b_ada, v_b_ada), (g_norm, m_g_norm, v_g_norm), (b_f, m_b_f, v_b_f),
                                            (sinks, m_sinks, v_sinks), (row(g_final), row(m_g_final), row(v_g_final))])
    bada_o, gn_o, bf_o, sk_o, gf_o = small

    outs = []
    for k, big in enumerate(((g_wada, g_win, g_woa, g_wob, g_wout), (d_wada, d_win, d_woa, d_wob, d_wout),
                             (m_wada, m_win, m_woa, m_wob, m_wout), (v_wada, v_win, v_woa, v_wob, v_wout))):
        wada_o, win_o, woa_o, wob_o, wout_o = big
        outs += [wada_o, bada_o[k], gn_o[k], win_o, bf_o[k], sk_o[k], woa_o, wob_o, wout_o, gf_o[k].reshape(D_MODEL)]
    return (loss_rows[0, 0], grad_x[None], *outs)
```

```python
import numpy as np
import jax
import jax.numpy as jnp
from jax import lax
from jax.experimental import pallas as pl
from jax.experimental.pallas import tpu as pltpu

F32 = jnp.float32
BF16 = jnp.bfloat16

D_MODEL = 1024
HEAD_DIM = 64
N_HEADS = 8
KV_GROUPS = 2
GROUP = N_HEADS // KV_GROUPS
WINDOW = 128
CHUNK = 128
AUG_W = N_HEADS * CHUNK
N_DEV = 8
IN_SHARD = 673
ADA_SHARD = 384
NORM_EPS = 1e-6
ROPE_THETA = 10000.0
Q_SCALE = HEAD_DIM ** -0.5
NEG = -1e30
UNDERFLOW = 105.0

ADAM_LR = 0.001
ADAM_B1 = 0.9
ADAM_B2 = 0.999
ADAM_EPS = 1e-08
ADAM_WD = 0.01
ADAM_STEP = 10
ADAM_C1 = 1.0 - ADAM_B1 ** ADAM_STEP
ADAM_C2 = 1.0 - ADAM_B2 ** ADAM_STEP

SEG = {}
_off = 0
for _name, _w in (("qa", 512), ("ka", 256), ("va", 256), ("qb", 512), ("kb", 512), ("vb", 512),
                  ("za", 512), ("zb", 512), ("ga", 1024), ("gb", 1024), ("f", 128)):
    SEG[_name] = (_off, _w)
    _off += _w
PROJ_W = _off
DPROJ_GROUPS = (("a", SEG["qa"][0], 1024), ("qb", SEG["qb"][0], 512), ("kb", SEG["kb"][0], 512),
                ("vb", SEG["vb"][0], 512), ("zg", SEG["za"][0], 3072), ("f", SEG["f"][0], 128))

VMEM_LIMIT = 56 * 1024 * 1024


def _params(n_axes):
    return pltpu.CompilerParams(dimension_semantics=("arbitrary",) * n_axes, vmem_limit_bytes=VMEM_LIMIT)


def _resident(a):
    return pl.BlockSpec(a.shape, lambda i: (0, 0), pipeline_mode=pl.Buffered(1))


def _dot(a, b):
    return jnp.dot(a, b, preferred_element_type=F32)


def _dot_nt(a, b):
    return lax.dot_general(a, b, (((1,), (1,)), ((), ())), preferred_element_type=F32)


def _dot_tn(a, b):
    return lax.dot_general(a, b, (((0,), (0,)), ((), ())), preferred_element_type=F32)


def _lane(n):
    return lax.broadcasted_iota(jnp.int32, (1, n), 1)


def _split3(x):
    hi = x.astype(BF16).astype(F32)
    r = x - hi
    mid = r.astype(BF16).astype(F32)
    lo = (r - mid).astype(BF16).astype(F32)
    return hi, mid, lo


def _dot_split(x, b):
    hi, mid, lo = _split3(x)
    return _dot(hi.astype(BF16), b) + _dot(mid.astype(BF16), b) + _dot(lo.astype(BF16), b)


def _spread3(x, e3_ref):
    return _dot(jnp.concatenate(_split3(x), axis=1).astype(BF16), e3_ref[...])


def _place3(lane, base, x, other):
    hi, mid, lo = _split3(x)
    return jnp.where(lane == base, hi, jnp.where(lane == base + 1, mid, jnp.where(lane == base + 2, lo, other)))


def _lane_sum(x, lo, hi):
    lane = _lane(x.shape[1])
    return jnp.sum(jnp.where((lane >= lo) & (lane < hi), x, 0.0), axis=1, keepdims=True)


def _ones_lanes(lo, hi):
    lane = _lane(CHUNK)
    return jnp.where((lane >= lo) & (lane < hi), 1.0, 0.0).astype(F32)


def _rope(c, cos, sa, sb):
    return c * cos + pltpu.roll(c, CHUNK - 32, 1) * sa + pltpu.roll(c, 32, 1) * sb


def _rope_inv(d, cos, sa, sb):
    return d * cos - (pltpu.roll(d, CHUNK - 32, 1) * sa + pltpu.roll(d, 32, 1) * sb)


def _pair(c0, c1):
    return jnp.where(_lane(CHUNK) < HEAD_DIM, c0, pltpu.roll(c1, HEAD_DIM, 1))


def _sigmoid(x):
    return 0.5 * jnp.tanh(0.5 * x) + 0.5


N_CHIPS = 4
ANY_SPEC = pl.BlockSpec(memory_space=pl.ANY)


def _exchange(arrays, modes, name):
    n = len(arrays)

    def body(*refs):
        copies = _exchange_copies(refs[:n], refs[n:2 * n], modes, *refs[2 * n:])
        _exchange_start(copies)
        _exchange_wait(copies)

    return pl.pallas_call(
        body, name=name, out_shape=_exchange_out_shapes(arrays, modes),
        in_specs=[ANY_SPEC] * n, out_specs=[ANY_SPEC] * n, scratch_shapes=_exchange_sems(n),
    )(*arrays)


def _exchange_out_shapes(arrays, modes):
    return [jax.ShapeDtypeStruct((N_DEV,) + a.shape if md == "gather" else a.shape, a.dtype)
            for a, md in zip(arrays, modes)]


def _exchange_sems(n):
    return [pltpu.SemaphoreType.DMA((n, N_DEV - 1)), pltpu.SemaphoreType.DMA((n, N_DEV - 1)),
            pltpu.SemaphoreType.DMA((n,))]


def _exchange_copies(srcs, dsts, modes, send_sems, recv_sems, loc_sems):
    n = len(srcs)
    x, y, c = lax.axis_index("x"), lax.axis_index("y"), lax.axis_index("c")

    def slot(i, px, py, pc):
        return 2 * px + py if modes[i] == "chips" else 4 * px + 2 * py + pc

    def src_of(i, px, py, pc):
        return srcs[i] if modes[i] == "gather" else srcs[i].at[slot(i, px, py, pc)]

    local = [pltpu.make_async_copy(src_of(i, x, y, c), dsts[i].at[slot(i, x, y, c)], loc_sems.at[i])
             for i in range(n)]
    sends, recvs = [], []
    for r in (1, 2, 4, 3, 5, 6, 7):
        px = 1 - x if r & 4 else x
        py = 1 - y if r & 2 else y
        pc = 1 - c if r & 1 else c
        for i in range(n):
            if modes[i] == "chips" and r & 1:
                continue
            sems = dict(send_sem=send_sems.at[i, r - 1], recv_sem=recv_sems.at[i, r - 1],
                        device_id=(px, py, pc), device_id_type=pl.DeviceIdType.MESH)
            sends.append(pltpu.make_async_remote_copy(
                src_ref=src_of(i, px, py, pc), dst_ref=dsts[i].at[slot(i, x, y, c)], **sems))
            recvs.append(pltpu.make_async_remote_copy(
                src_ref=src_of(i, px, py, pc), dst_ref=dsts[i].at[slot(i, px, py, pc)], **sems))
    return local, sends, recvs


def _exchange_start(copies):
    local, sends, _ = copies
    for cp in local + sends:
        cp.start()


def _exchange_wait(copies):
    local, sends, recvs = copies
    for cp in recvs:
        cp.wait_recv()
    for cp in sends:
        cp.wait_send()
    for cp in local:
        cp.wait()


def _gather2_parts(srcs, dsts, send_sems, recv_sems, loc_sems):
    n = len(srcs)
    x, y, c = lax.axis_index("x"), lax.axis_index("y"), lax.axis_index("c")
    me, sibling = (x, y, c), (x, y, 1 - c)
    chips = [(1 - x, y), (x, 1 - y), (1 - x, 1 - y)]

    def rows(i, dev):
        return dsts[i].at[4 * dev[0] + 2 * dev[1] + dev[2]]

    def copy(i, k, block, to, src=None):
        return pltpu.make_async_remote_copy(
            src_ref=rows(i, block) if src is None else src, dst_ref=rows(i, block),
            send_sem=send_sems.at[i, k], recv_sem=recv_sems.at[i, k],
            device_id=to, device_id_type=pl.DeviceIdType.MESH)

    local = [pltpu.make_async_copy(srcs[i], rows(i, me), loc_sems.at[i]) for i in range(n)]
    first = []
    for i in range(n):
        first.append(copy(i, 0, me, sibling, src=srcs[i]))
        first += [copy(i, 1 + j, me, (*chip, c), src=srcs[i]) for j, chip in enumerate(chips)]
    return n, c, me, sibling, chips, copy, local, first


def _gather2_start(srcs, dsts, send_sems, recv_sems, loc_sems):
    *_, local, first = _gather2_parts(srcs, dsts, send_sems, recv_sems, loc_sems)
    for cp in local + first:
        cp.start()


def _gather2_finish(srcs, dsts, send_sems, recv_sems, loc_sems):
    n, c, me, sibling, chips, copy, local, first = _gather2_parts(srcs, dsts, send_sems, recv_sems, loc_sems)
    passed = []
    for j, chip in enumerate(chips):
        for i in range(n):
            copy(i, 1 + j, (*chip, c), me).wait_recv()
            cp = copy(i, 4 + j, (*chip, c), sibling)
            cp.start()
            passed.append(cp)
    for i in range(n):
        copy(i, 0, sibling, me).wait_recv()
        for j, chip in enumerate(chips):
            copy(i, 4 + j, (*chip, 1 - c), me).wait_recv()
    for cp in first + passed:
        cp.wait_send()
    for cp in local:
        cp.wait()


def _sibling_copies(srcs, dsts, send_sems, recv_sems):
    x, y, c = lax.axis_index("x"), lax.axis_index("y"), lax.axis_index("c")
    return [pltpu.make_async_remote_copy(
        src_ref=srcs[i].at[1 - c, k], dst_ref=dsts[i].at[k], send_sem=send_sems.at[i, k], recv_sem=recv_sems.at[i, k],
        device_id=(x, y, 1 - c), device_id_type=pl.DeviceIdType.MESH)
        for i in range(len(srcs)) for k in range(N_CHIPS)]


def _token_matmuls_and_swap(pairs, tk, swap):
    npair, nsw = len(pairs), len(swap)
    s = pairs[0][0].shape[1]
    last = s // tk - 1

    def body(*refs):
        a_refs, b_refs = refs[0:2 * npair:2], refs[1:2 * npair:2]
        src = refs[2 * npair:2 * npair + nsw]
        o_refs = refs[2 * npair + nsw:3 * npair + nsw]
        dst = refs[3 * npair + nsw:3 * npair + 2 * nsw]
        sems = refs[3 * npair + 2 * nsw:3 * npair + 2 * nsw + 2]
        acc_refs = refs[3 * npair + 2 * nsw + 2:]

        @pl.when(pl.program_id(0) == 0)
        def _():
            for cp in _sibling_copies(src, dst, *sems):
                cp.start()

        for a_ref, b_ref, o_ref, acc_ref in zip(a_refs, b_refs, o_refs, acc_refs):
            _accumulate_tokens(a_ref[...], b_ref, acc_ref, o_ref, last)

        @pl.when(pl.program_id(0) == last)
        def _():
            for cp in _sibling_copies(src, dst, *sems):
                cp.wait()

    in_specs, args = [], []
    for a_t, b in pairs:
        in_specs += [pl.BlockSpec((a_t.shape[0], tk), lambda k: (0, k)), pl.BlockSpec((tk, b.shape[1]), lambda k: (k, 0))]
        args += [a_t, b]
    outs = pl.pallas_call(
        body, name="dw_out_projs_swap", grid=(s // tk,),
        out_shape=[jax.ShapeDtypeStruct((a_t.shape[0], b.shape[1]), BF16) for a_t, b in pairs]
        + [jax.ShapeDtypeStruct(a.shape[1:], a.dtype) for a in swap],
        in_specs=in_specs + [ANY_SPEC] * nsw,
        out_specs=[pl.BlockSpec((a_t.shape[0], b.shape[1]), lambda k: (0, 0)) for a_t, b in pairs] + [ANY_SPEC] * nsw,
        scratch_shapes=[pltpu.SemaphoreType.DMA((nsw, N_CHIPS)), pltpu.SemaphoreType.DMA((nsw, N_CHIPS))]
        + [pltpu.VMEM((a_t.shape[0], b.shape[1]), F32) for a_t, b in pairs],
        compiler_params=_params(1))(*args, *swap)
    return outs[:npair], outs[npair:]


def _chip_partial(core, slots, theirs, tr, name):
    _, k, rws, cols = slots.shape

    def body(core_ref, a_ref, b_ref, f_ref, h_ref):
        sm = a_ref[0].astype(F32) + b_ref[...].astype(F32)
        f_ref[...] = sm
        h_ref[...] = sm.astype(BF16)

    blk = pl.BlockSpec((1, tr, cols), lambda j, i, core_ref: (j, i, 0))
    mine = pl.BlockSpec((1, 1, tr, cols), lambda j, i, core_ref: (core_ref[0], j, i, 0))
    return pl.pallas_call(
        body, name=name,
        grid_spec=pltpu.PrefetchScalarGridSpec(num_scalar_prefetch=1, grid=(k, rws // tr),
                                               in_specs=[mine, blk], out_specs=[blk, blk]),
        out_shape=[jax.ShapeDtypeStruct(theirs.shape, F32), jax.ShapeDtypeStruct(theirs.shape, BF16)],
        compiler_params=_params(2))(core, slots, theirs)


def _ada_fwd(c_all, w_shard, b_shard):
    def body(c_ref, w_ref, b_ref, o_ref):
        ch, cm, cl = [t.astype(BF16) for t in _split3(c_ref[...])]
        wh, wm, wl = [t.astype(BF16) for t in _split3(w_ref[...])]
        acc = _dot(ch, wh) + _dot(ch, wm) + _dot(cm, wh) + _dot(ch, wl) + _dot(cl, wh) + _dot(cm, wm)
        o_ref[...] = acc + b_ref[...]

    return pl.pallas_call(body, name="ada_fwd", out_shape=jax.ShapeDtypeStruct((N_DEV, ADA_SHARD), F32),
                          compiler_params=_params(0))(c_all, w_shard, b_shard)


def _rope_tables_and_gather(pos_col, inv_freq, tm, weights, c_rows):
    s = pos_col.shape[0]
    nw = len(weights)
    last = s // tm - 1

    def body(*refs):
        p_ref, f_ref = refs[:2]
        w_src, c_src = refs[2:2 + nw], refs[2 + nw:3 + nw]
        cos_ref, sa_ref, sb_ref = refs[3 + nw:6 + nw]
        w_dst, c_dst = refs[6 + nw:6 + 2 * nw], refs[6 + 2 * nw:7 + 2 * nw]
        w_sems, c_sems = refs[7 + 2 * nw:10 + 2 * nw], refs[10 + 2 * nw:]

        @pl.when(pl.program_id(0) == 0)
        def _():
            _gather2_start(w_src, w_dst, *w_sems)
            _exchange_start(_exchange_copies(c_src, c_dst, ["gather"], *c_sems))

        ang = p_ref[...] * f_ref[...]
        sin = jnp.sin(ang)
        first_half = (_lane(CHUNK) & (HEAD_DIM - 1)) < HEAD_DIM // 2
        cos_ref[...] = jnp.cos(ang)
        sa_ref[...] = jnp.where(first_half, -sin, 0.0)
        sb_ref[...] = jnp.where(first_half, 0.0, sin)

        @pl.when(pl.program_id(0) == last)
        def _():
            _exchange_wait(_exchange_copies(c_src, c_dst, ["gather"], *c_sems))
            _gather2_finish(w_src, w_dst, *w_sems)

    tab = jax.ShapeDtypeStruct((s, CHUNK), F32)
    blk = pl.BlockSpec((tm, CHUNK), lambda i: (i, 0))
    gathered = [jax.ShapeDtypeStruct((N_DEV,) + a.shape, a.dtype) for a in list(weights) + [c_rows]]
    outs = pl.pallas_call(
        body, name="rope_tables_gather", grid=(s // tm,), out_shape=[tab, tab, tab] + gathered,
        in_specs=[pl.BlockSpec((tm, 1), lambda i: (i, 0)), pl.BlockSpec((1, CHUNK), lambda i: (0, 0))]
        + [ANY_SPEC] * (nw + 1),
        out_specs=[blk, blk, blk] + [ANY_SPEC] * (nw + 1),
        scratch_shapes=_exchange_sems(nw) + _exchange_sems(1),
        compiler_params=_params(1))(pos_col, inv_freq, *weights, c_rows)
    return outs[:3], outs[3:3 + nw], outs[3 + nw]


def _fwd_proj(x, mod, w_all, cos, sa, sb, bf_row, emat, ind, tm, late_shards):
    s = x.shape[0]
    nl = len(late_shards)
    last = s // tm - 1

    def body(x_ref, mod_ref, w_ref, cos_ref, sa_ref, sb_ref, bf_ref, e_ref, ind_ref, *rest):
        late_src = rest[:nl]
        (h_ref, qa_ref, ka_ref, va_ref, za_ref, qb_ref, kb_ref, vb_ref, zb_ref, ga_ref, gb_ref, f_ref,
         bnd_ref, nrm_ref) = rest[nl:nl + 14]
        late_dst = rest[nl + 14:2 * nl + 14]
        carry_ref, late_sems = rest[2 * nl + 14], rest[2 * nl + 15:]

        @pl.when(pl.program_id(0) == 0)
        def _():
            carry_ref[...] = jnp.zeros_like(carry_ref)
            _gather2_start(late_src, late_dst, *late_sems)

        xv = x_ref[...]
        rstd = lax.rsqrt(jnp.mean(xv * xv, axis=-1, keepdims=True) + NORM_EPS)
        h = (xv * rstd * mod_ref[0:1, :]) * (1.0 + mod_ref[1:2, :]) + mod_ref[2:3, :]
        hb = h.astype(BF16)
        h_ref[...] = hb.T

        def seg(name):
            off, w = SEG[name]
            return _dot(hb, w_ref[:, off:off + w])

        lane = _lane(CHUNK)
        low = lane < HEAD_DIM
        q_ones = _ones_lanes(64, 67)
        k_ones = _ones_lanes(67, 70)
        cos_t, sa_t, sb_t = cos_ref[...], sa_ref[...], sb_ref[...]

        def write_heads(ref, nat, extra_of, rope, scale):
            for p in range(N_HEADS // 2):
                c = nat[:, CHUNK * p:CHUNK * (p + 1)]
                if rope:
                    c = _rope(c, cos_t, sa_t, sb_t)
                if scale != 1.0:
                    c = c * scale
                for hh in range(2):
                    hd = 2 * p + hh
                    src = c if hh == 0 else pltpu.roll(c, HEAD_DIM, 1)
                    ref[:, CHUNK * hd:CHUNK * (hd + 1)] = jnp.where(low, src, extra_of(hd)).astype(BF16)

        write_heads(qa_ref, seg("qa"), lambda hd: q_ones, True, Q_SCALE)
        ka = seg("ka")
        va = seg("va")
        for kv in range(KV_GROUPS):
            sl = slice(CHUNK * kv, CHUNK * (kv + 1))
            ka_ref[:, sl] = jnp.where(low, _rope(ka[:, sl], cos_t, sa_t, sb_t), k_ones).astype(BF16)
            va_ref[:, sl] = jnp.where(low, va[:, sl], q_ones).astype(BF16)
        za_ref[...] = seg("za")
        zb_ref[...] = seg("zb")
        ga_ref[...] = seg("ga")
        gb_ref[...] = seg("gb")

        xf = seg("f") + bf_ref[...]
        f_ref[...] = xf
        logf = jnp.minimum(xf, 0.0) - jnp.log1p(jnp.exp(-jnp.abs(xf)))
        row = lax.broadcasted_iota(jnp.int32, (tm, tm), 0)
        col = lax.broadcasted_iota(jnp.int32, (tm, tm), 1)
        tri = jnp.where(col <= row, 1.0, 0.0).astype(BF16)
        hi, mid, lo = _split3(logf)
        cum = _dot(tri, hi.astype(BF16)) + _dot(tri, mid.astype(BF16)) + _dot(tri, lo.astype(BF16))
        cum = cum + carry_ref[0:1, :]
        carry_ref[0:1, :] += jnp.sum(logf, axis=0, keepdims=True)
        lane_all = _lane(AUG_W) & (CHUNK - 1)
        bias = _spread3(-cum, e_ref) + jnp.where((lane_all >= 67) & (lane_all < 70), 1.0, 0.0)

        qb, kb = seg("qb"), seg("kb")
        write_heads(qb_ref, qb, lambda hd: q_ones, False, Q_SCALE)
        write_heads(kb_ref, kb, lambda hd: bias[:, CHUNK * hd:CHUNK * (hd + 1)], False, 1.0)
        write_heads(vb_ref, seg("vb"), lambda hd: q_ones, False, 1.0)

        rid = lax.broadcasted_iota(jnp.int32, (tm, CHUNK), 0)
        bnd_ref[...] = jnp.zeros_like(bnd_ref)
        bnd_ref[0, 0:1, :] = jnp.sum(jnp.where(rid == 0, cum, 0.0), axis=0, keepdims=True)
        bnd_ref[0, 1:2, :] = jnp.sum(jnp.where(rid == tm - 1, cum, 0.0), axis=0, keepdims=True)

        @pl.when(pl.program_id(0) == 0)
        def _():
            nrm_ref[...] = jnp.zeros_like(nrm_ref)

        for r_, nat in ((0, qb * Q_SCALE), (1, kb)):
            sq = _dot((nat * nat).astype(BF16), ind_ref[...])
            nrm_ref[r_:r_ + 1, :] = jnp.maximum(nrm_ref[r_:r_ + 1, :], jnp.max(sq, axis=0, keepdims=True))

        @pl.when(pl.program_id(0) == last)
        def _():
            _gather2_finish(late_src, late_dst, *late_sems)

    row_blk = lambda w: pl.BlockSpec((tm, w), lambda i: (i, 0))
    full = lambda a: pl.BlockSpec(a.shape, lambda i: (0,) * a.ndim)
    sds = lambda w, dt: jax.ShapeDtypeStruct((s, w), dt)
    out_shape = [sds(AUG_W, BF16), sds(KV_GROUPS * CHUNK, BF16), sds(KV_GROUPS * CHUNK, BF16),
                 sds(512, F32), sds(AUG_W, BF16), sds(AUG_W, BF16), sds(AUG_W, BF16), sds(512, F32),
                 sds(D_MODEL, F32), sds(D_MODEL, F32), sds(CHUNK, F32)]
    small = [jax.ShapeDtypeStruct((s // tm, 8, CHUNK), F32), jax.ShapeDtypeStruct((8, CHUNK), F32)]
    gathered = [jax.ShapeDtypeStruct((N_DEV,) + a.shape, a.dtype) for a in late_shards]
    outs = pl.pallas_call(
        body, name="fwd_proj", grid=(s // tm,),
        out_shape=[jax.ShapeDtypeStruct((D_MODEL, s), BF16)] + out_shape + small + gathered,
        in_specs=[row_blk(D_MODEL), full(mod), _resident(w_all), row_blk(CHUNK), row_blk(CHUNK), row_blk(CHUNK),
                  full(bf_row), full(emat), full(ind)] + [ANY_SPEC] * nl,
        out_specs=[pl.BlockSpec((D_MODEL, tm), lambda i: (0, i))] + [row_blk(o.shape[1]) for o in out_shape] + [
            pl.BlockSpec((1, 8, CHUNK), lambda i: (i, 0, 0)), pl.BlockSpec((8, CHUNK), lambda i: (0, 0))]
        + [ANY_SPEC] * nl,
        scratch_shapes=[pltpu.VMEM((8, CHUNK), F32)] + _exchange_sems(nl),
        compiler_params=_params(1))(x, mod, w_all, cos, sa, sb, bf_row, emat, ind, *late_shards)
    return outs[:14], outs[14:]


def _swa_fwd(q_aug, k_aug, v_aug, sink_rows, tq):
    s = q_aug.shape[0]
    r = tq // WINDOW
    gw = GROUP * CHUNK

    def body(q_ref, kc_ref, kp_ref, vc_ref, vp_ref, sink_ref, o_ref, qb_ref):
        i = pl.program_id(1)
        st = GROUP * WINDOW
        qloc = lax.broadcasted_iota(jnp.int32, (st, 2 * WINDOW), 0) & (WINDOW - 1)
        col = lax.broadcasted_iota(jnp.int32, (st, 2 * WINDOW), 1)
        band = (col > qloc) & (col <= qloc + WINDOW)
        head = jnp.right_shift(lax.broadcasted_iota(jnp.int32, (st, 1), 0), 7)
        sink = jnp.zeros((st, 1), F32)
        for g in range(GROUP):
            sink = jnp.where(head == g, jnp.max(sink_ref[0, g:g + 1, :], axis=1, keepdims=True), sink)
        lane = _lane(CHUNK)
        for sub in range(r):
            rows = slice(WINDOW * sub, WINDOW * (sub + 1))
            q = jnp.concatenate([q_ref[rows, CHUNK * g:CHUNK * (g + 1)] for g in range(GROUP)], axis=0)
            if sub == 0:
                k = jnp.concatenate([kp_ref[...], kc_ref[rows, :]], axis=0)
                v = jnp.concatenate([vp_ref[...], vc_ref[rows, :]], axis=0)
                valid = band & ((col >= WINDOW) | (i > 0))
            else:
                both = slice(WINDOW * (sub - 1), WINDOW * (sub + 1))
                k, v, valid = kc_ref[both, :], vc_ref[both, :], band
            sc = jnp.where(valid, _dot_nt(q, k), NEG)
            m = jnp.maximum(jnp.max(sc, axis=1, keepdims=True), sink)
            acc = _dot(jnp.exp((sc - m).astype(BF16)), v)
            denom = _lane_sum(acc, 64, 65) + jnp.exp(sink - m)
            out = acc * (1.0 / denom)
            aug = _place3(lane, 67, -(m + jnp.log(denom)), q.astype(F32)).astype(BF16)
            hrows = lambda a, g: a[WINDOW * g:WINDOW * (g + 1), :]
            for g in range(GROUP):
                qb_ref[rows, CHUNK * g:CHUNK * (g + 1)] = hrows(aug, g)
            for pp in range(GROUP // 2):
                o_ref[rows, CHUNK * pp:CHUNK * (pp + 1)] = _pair(hrows(out, 2 * pp), hrows(out, 2 * pp + 1))

    return pl.pallas_call(
        body, name="swa_fwd", grid=(KV_GROUPS, s // tq),
        out_shape=[jax.ShapeDtypeStruct((s, 512), F32), jax.ShapeDtypeStruct((s, AUG_W), BF16)],
        in_specs=[pl.BlockSpec((tq, gw), lambda kv, i: (i, kv)),
                  pl.BlockSpec((tq, CHUNK), lambda kv, i: (i, kv)),
                  pl.BlockSpec((WINDOW, CHUNK), lambda kv, i: (jnp.maximum(i * r - 1, 0), kv)),
                  pl.BlockSpec((tq, CHUNK), lambda kv, i: (i, kv)),
                  pl.BlockSpec((WINDOW, CHUNK), lambda kv, i: (jnp.maximum(i * r - 1, 0), kv)),
                  pl.BlockSpec((1, 8, CHUNK), lambda kv, i: (kv, 0, 0))],
        out_specs=[pl.BlockSpec((tq, GROUP * HEAD_DIM), lambda kv, i: (i, kv)),
                   pl.BlockSpec((tq, gw), lambda kv, i: (i, kv))],
        compiler_params=_params(2))(q_aug, k_aug, k_aug, v_aug, v_aug, sink_rows)


def _swa_bwd(k_aug, v_aug, q_bwd, do_aug, cos, sa, sb, sink_rows, tk):
    s = k_aug.shape[0]
    r = tk // WINDOW
    nt = s // tk
    nb = s // WINDOW

    def body(k_ref, v_ref, q_ref, qn_ref, do_ref, don_ref, cos_ref, sa_ref, sb_ref, sink_ref,
             da_ref, dsink_ref, carry_ref, acc_ref):
        j = pl.program_id(0)

        @pl.when(j == 0)
        def _():
            carry_ref[...] = jnp.zeros_like(carry_ref)
            dsink_ref[...] = jnp.zeros_like(dsink_ref)

        acc_ref[...] = jnp.zeros_like(acc_ref)
        acc_ref[0:WINDOW, :] = carry_ref[...]
        dk_col, dv_col = SEG["ka"][0] - SEG["qa"][0], SEG["va"][0] - SEG["qa"][0]
        st = GROUP * WINDOW
        key = lax.broadcasted_iota(jnp.int32, (WINDOW, 2 * st), 0)
        col = lax.broadcasted_iota(jnp.int32, (WINDOW, 2 * st), 1)
        qloc = col & (WINDOW - 1)
        band = ((col < st) & (key <= qloc)) | ((col >= st) & (key > qloc))
        lane = _lane(CHUNK)
        low = lane < HEAD_DIM
        cos_t, sa_t, sb_t = cos_ref[...], sa_ref[...], sb_ref[...]
        for kv, sub in [(kv, sub) for kv in range(KV_GROUPS) for sub in range(r)]:
            heads = [slice(CHUNK * (GROUP * kv + g), CHUNK * (GROUP * kv + g + 1)) for g in range(GROUP)]
            kvs = slice(CHUNK * kv, CHUNK * (kv + 1))
            rows = slice(WINDOW * sub, WINDOW * (sub + 1))
            nxt = slice(WINDOW * (sub + 1), WINDOW * (sub + 2))
            k, v = k_ref[rows, kvs], v_ref[rows, kvs]
            q_cur, do_cur = [q_ref[rows, cs] for cs in heads], [do_ref[rows, cs] for cs in heads]
            if sub < r - 1:
                q_nxt, do_nxt, valid = [q_ref[nxt, cs] for cs in heads], [do_ref[nxt, cs] for cs in heads], band
            else:
                q_nxt, do_nxt = [qn_ref[:, cs] for cs in heads], [don_ref[:, cs] for cs in heads]
                valid = band & ((col < st) | (j < nt - 1))
            q = jnp.concatenate(q_cur + q_nxt, axis=0)
            do = jnp.concatenate(do_cur + do_nxt, axis=0)
            pt = jnp.exp(jnp.where(valid, _dot_nt(k, q), NEG))
            ds = (pt * _dot_nt(v, do)).astype(BF16)
            dv = _dot(pt.astype(BF16), do)
            dk = _dot(ds, q)
            dq = _dot_tn(ds, k)
            for g, cs in enumerate(heads):
                acc_ref[rows, cs] += dq[WINDOW * g:WINDOW * (g + 1), :]
                dqn = dq[st + WINDOW * g:st + WINDOW * (g + 1), :]
                if sub < r - 1:
                    acc_ref[nxt, cs] += dqn
                else:
                    carry_ref[:, cs] = dqn
                sink = jnp.max(sink_ref[kv, g:g + 1, :], axis=1, keepdims=True)
                p_sink = jnp.exp(sink + _lane_sum(q_cur[g].astype(F32), 67, 70))
                term = jnp.sum(p_sink * _lane_sum(do_cur[g].astype(F32), 64, 67), axis=0, keepdims=True)
                dsink_ref[kv, g:g + 1, :] += jnp.broadcast_to(term, (1, CHUNK))
            da_ref[rows, dk_col + CHUNK * kv:dk_col + CHUNK * (kv + 1)] = _rope_inv(
                jnp.where(low, dk, 0.0), cos_t[rows, :], sa_t[rows, :], sb_t[rows, :]).astype(BF16)
            da_ref[rows, dv_col + CHUNK * kv:dv_col + CHUNK * (kv + 1)] = jnp.where(low, dv, 0.0).astype(BF16)
        for pp in range(N_HEADS // 2):
            d = _pair(acc_ref[:, CHUNK * 2 * pp:CHUNK * (2 * pp + 1)], acc_ref[:, CHUNK * (2 * pp + 1):CHUNK * (2 * pp + 2)])
            da_ref[:, CHUNK * pp:CHUNK * (pp + 1)] = (_rope_inv(d, cos_t, sa_t, sb_t) * Q_SCALE).astype(BF16)

    cur = lambda w: pl.BlockSpec((tk, w), lambda j: (j, 0))
    nxt = pl.BlockSpec((WINDOW, AUG_W), lambda j: (jnp.minimum((j + 1) * r, nb - 1), 0))
    whole = pl.BlockSpec((KV_GROUPS, 8, CHUNK), lambda j: (0, 0, 0))
    return pl.pallas_call(
        body, name="swa_bwd", grid=(nt,),
        out_shape=[jax.ShapeDtypeStruct((s, 1024), BF16), jax.ShapeDtypeStruct((KV_GROUPS, 8, CHUNK), F32)],
        in_specs=[cur(KV_GROUPS * CHUNK), cur(KV_GROUPS * CHUNK), cur(AUG_W), nxt, cur(AUG_W), nxt,
                  cur(CHUNK), cur(CHUNK), cur(CHUNK), whole],
        out_specs=[cur(1024), whole],
        scratch_shapes=[pltpu.VMEM((WINDOW, AUG_W), F32), pltpu.VMEM((tk, AUG_W), F32)],
        compiler_params=_params(1))(k_aug, v_aug, q_bwd, q_bwd, do_aug, do_aug, cos, sa, sb, sink_rows)


def _fox_fwd(first_key, q_aug, k_aug, v_aug, t):
    s = q_aug.shape[0]
    pw = 2 * CHUNK

    def body(lo_ref, q_ref, k_ref, v_ref, o_ref, qb_ref):
        i = pl.program_id(1)
        first = lo_ref[pl.program_id(0), i]
        row = lax.broadcasted_iota(jnp.int32, (t, t), 0)
        col = lax.broadcasted_iota(jnp.int32, (t, t), 1)
        lane = _lane(CHUNK)
        heads = (slice(0, CHUNK), slice(CHUNK, pw))
        qs = [q_ref[:, hs] for hs in heads]

        def step(jb, carry, masked):
            rows = pl.ds(pl.multiple_of(jb * t, t), t)
            scs = [_dot_nt(qs[hh], k_ref[rows, hs]) for hh, hs in enumerate(heads)]
            if masked:
                scs = [jnp.where(col <= row, sc, NEG) for sc in scs]
            m_news = [jnp.maximum(carry[2 * hh], jnp.max(scs[hh], axis=1, keepdims=True)) for hh in range(2)]
            ps = [jnp.exp((scs[hh] - m_news[hh]).astype(BF16)) for hh in range(2)]
            new = []
            for hh, hs in enumerate(heads):
                m, acc = carry[2 * hh], carry[2 * hh + 1]
                new += [m_news[hh], jnp.exp(m - m_news[hh]) * acc + _dot(ps[hh], v_ref[rows, hs])]
            return tuple(new)

        init = (jnp.full((t, 1), NEG, F32), jnp.zeros((t, CHUNK), F32)) * 2
        carry = step(i, lax.fori_loop(first, i, lambda jb, c: step(jb, c, False), init), True)
        outs = []
        for hh, hs in enumerate(heads):
            m, acc = carry[2 * hh], carry[2 * hh + 1]
            denom = _lane_sum(acc, 64, 65)
            outs.append(acc * (1.0 / denom))
            lse = m + jnp.log(denom)
            qb_ref[:, hs] = _place3(lane, 67, -lse, qs[hh].astype(F32)).astype(BF16)
        o_ref[...] = _pair(outs[0], outs[1])

    return pl.pallas_call(
        body, name="fox_fwd", grid=(N_HEADS // 2, s // t),
        out_shape=[jax.ShapeDtypeStruct((s, 512), F32), jax.ShapeDtypeStruct((s, AUG_W), BF16)],
        in_specs=[pl.BlockSpec(memory_space=pltpu.SMEM),
                  pl.BlockSpec((t, pw), lambda p, i: (i, p)),
                  pl.BlockSpec((s, pw), lambda p, i: (0, p)),
                  pl.BlockSpec((s, pw), lambda p, i: (0, p))],
        out_specs=[pl.BlockSpec((t, CHUNK), lambda p, i: (i, p)), pl.BlockSpec((t, pw), lambda p, i: (i, p))],
        compiler_params=_params(2))(first_key, q_aug, k_aug, v_aug)


def _fox_block_ranges(bounds, norms, r):
    cum_first = bounds[0::r, 0, :N_HEADS]
    cum_last = bounds[r - 1::r, 1, :N_HEADS]
    n = cum_first.shape[0]
    reach = 2.0 * 1.02 * jnp.sqrt(norms[0, :N_HEADS] * norms[1, :N_HEADS]) + UNDERFLOW
    blk = jnp.arange(n, dtype=jnp.int32)
    decay = cum_last[None, :, :] - cum_first[:, None, :]
    skip = (decay > reach[None, None, :]) & (blk[None, :, None] < blk[:, None, None])
    first = jnp.min(jnp.where(skip, n, blk[None, :, None]), axis=1)
    first = jnp.minimum(first[:, 0::2], first[:, 1::2]).T
    needed = (first[:, :, None] <= blk[None, None, :]) & (blk[None, :, None] >= blk[None, None, :])
    last = jnp.max(jnp.where(needed, blk[None, :, None], blk[None, None, :]), axis=1)
    return first.astype(jnp.int32), last.astype(jnp.int32)


def _fox_bwd(last_query, k_aug, v_aug, q_bwd, do_aug, t):
    s = k_aug.shape[0]
    n = s // t
    pw = 2 * CHUNK

    def body(hi_ref, k_ref, v_ref, q_ref, do_ref, dk_ref, dv_ref, dck_ref, dq_ref, dcq_ref, dq_scr):
        j = pl.program_id(1)
        last = hi_ref[pl.program_id(0), j]

        @pl.when(j == 0)
        def _():
            dq_scr[...] = jnp.zeros_like(dq_scr)

        row = lax.broadcasted_iota(jnp.int32, (t, t), 0)
        col = lax.broadcasted_iota(jnp.int32, (t, t), 1)
        lane = _lane(CHUNK)
        heads = (slice(0, CHUNK), slice(CHUNK, pw))
        ks = [k_ref[:, hs] for hs in heads]
        vs = [v_ref[:, hs] for hs in heads]

        def step(ib, carry, masked):
            rows = pl.ds(pl.multiple_of(ib * t, t), t)
            new = []
            for hh, hs in enumerate(heads):
                dv, dk = carry[2 * hh], carry[2 * hh + 1]
                q, do = q_ref[rows, hs], do_ref[rows, hs]
                st = _dot_nt(ks[hh], q)
                if masked:
                    st = jnp.where(row <= col, st, NEG)
                pt = jnp.exp(st)
                ds = (pt * _dot_nt(vs[hh], do)).astype(BF16)
                new += [dv + _dot(pt.astype(BF16), do), dk + _dot(ds, q)]
                dq_scr[rows, hs] += _dot_tn(ds, ks[hh])
            return tuple(new)

        zero = jnp.zeros((t, CHUNK), F32)
        carry = lax.fori_loop(j + 1, last + 1, lambda ib, c: step(ib, c, False), step(j, (zero,) * 4, True))
        dvs, dks = (carry[0], carry[2]), (carry[1], carry[3])
        dk_ref[...] = _pair(dks[0], dks[1]).astype(BF16)
        dv_ref[...] = _pair(dvs[0], dvs[1]).astype(BF16)
        dck_ref[0] = jnp.where(lane == 0, pltpu.roll(dks[0], 64, 1),
                               jnp.where(lane == 1, pltpu.roll(dks[1], 65, 1), 0.0))

        @pl.when(j == n - 1)
        def _():
            for ib in range(n):
                rows = slice(t * ib, t * (ib + 1))
                d0, d1 = dq_scr[rows, 0:CHUNK], dq_scr[rows, CHUNK:pw]
                dq_ref[rows, :] = (_pair(d0, d1) * Q_SCALE).astype(BF16)
                dcq_ref[0, rows, :] = jnp.where(lane == 0, pltpu.roll(d0, CHUNK - 67, 1),
                                                jnp.where(lane == 1, pltpu.roll(d1, CHUNK - 66, 1), 0.0))

    return pl.pallas_call(
        body, name="fox_bwd", grid=(N_HEADS // 2, n),
        out_shape=[jax.ShapeDtypeStruct((s, 512), BF16), jax.ShapeDtypeStruct((s, 512), BF16),
                   jax.ShapeDtypeStruct((N_HEADS // 2, s, CHUNK), F32), jax.ShapeDtypeStruct((s, 512), BF16),
                   jax.ShapeDtypeStruct((N_HEADS // 2, s, CHUNK), F32)],
        in_specs=[pl.BlockSpec(memory_space=pltpu.SMEM),
                  pl.BlockSpec((t, pw), lambda p, j: (j, p)), pl.BlockSpec((t, pw), lambda p, j: (j, p)),
                  pl.BlockSpec((s, pw), lambda p, j: (0, p)), pl.BlockSpec((s, pw), lambda p, j: (0, p))],
        out_specs=[pl.BlockSpec((t, CHUNK), lambda p, j: (j, p)), pl.BlockSpec((t, CHUNK), lambda p, j: (j, p)),
                   pl.BlockSpec((1, t, CHUNK), lambda p, j: (p, j, 0)), pl.BlockSpec((s, CHUNK), lambda p, j: (0, p)),
                   pl.BlockSpec((1, s, CHUNK), lambda p, j: (p, 0, 0))],
        scratch_shapes=[pltpu.VMEM((s, pw), F32)],
        compiler_params=_params(2))(last_query, k_aug, v_aug, q_bwd, do_aug)


def _fgate_bwd(dcq, dck, xf, tm):
    s = xf.shape[0]
    nt = s // tm

    def body(dq_ref, dc_ref, xf_ref, df_ref, dbf_ref, carry_ref):
        @pl.when(pl.program_id(0) == 0)
        def _():
            carry_ref[...] = jnp.zeros_like(carry_ref)
            dbf_ref[...] = jnp.zeros_like(dbf_ref)

        row = lax.broadcasted_iota(jnp.int32, (tm, tm), 0)
        col = lax.broadcasted_iota(jnp.int32, (tm, tm), 1)
        tri = jnp.where(col >= row, 1.0, 0.0).astype(BF16)
        dcum = dq_ref[0] - dc_ref[0]
        for p in range(1, N_HEADS // 2):
            dcum = dcum + pltpu.roll(dq_ref[p] - dc_ref[p], 2 * p, 1)
        hi, mid, lo = _split3(dcum)
        dlogf = _dot(tri, hi.astype(BF16)) + _dot(tri, mid.astype(BF16)) + _dot(tri, lo.astype(BF16))
        dlogf = dlogf + carry_ref[0:1, :]
        carry_ref[0:1, :] += jnp.sum(dcum, axis=0, keepdims=True)
        df = dlogf / (1.0 + jnp.exp(xf_ref[...]))
        df_ref[...] = df.astype(BF16)
        dbf_ref[0:1, :] += jnp.sum(df, axis=0, keepdims=True)

    rev = pl.BlockSpec((tm, CHUNK), lambda i: (nt - 1 - i, 0))
    rev4 = pl.BlockSpec((N_HEADS // 2, tm, CHUNK), lambda i: (0, nt - 1 - i, 0))
    return pl.pallas_call(
        body, name="fgate_bwd", grid=(nt,),
        out_shape=[jax.ShapeDtypeStruct((s, CHUNK), BF16), jax.ShapeDtypeStruct((8, CHUNK), F32)],
        in_specs=[rev4, rev4, rev], out_specs=[rev, pl.BlockSpec((8, CHUNK), lambda i: (0, 0))],
        scratch_shapes=[pltpu.VMEM((8, CHUNK), F32)],
        compiler_params=_params(1))(dcq, dck, xf)


def _post(x, tgt, att_a, att_b, za, zb, ga, gb, woa_t, wob_t, wout, vec, ind, emat, tm):
    s = x.shape[0]

    def body(x_ref, t_ref, aa_ref, ab_ref, za_ref, zb_ref, ga_ref, gb_ref, woat_ref, wobt_ref, wout_ref,
             vec_ref, ind_ref, e_ref,
             dx1_ref, ua_ref, ub_ref, mg_ref, do_ref, dya_ref, dyb_ref, doa_ref, dob_ref, dzg_ref, acc_ref):
        @pl.when(pl.program_id(0) == 0)
        def _():
            acc_ref[...] = jnp.zeros_like(acc_ref)

        gate, gfin = vec_ref[0:1, :], vec_ref[1:2, :]
        inv_d = 1.0 / D_MODEL

        def branch_fwd(att_ref, z_ref, w_ref, u_ref):
            att, z = att_ref[...], z_ref[...]
            sz = _sigmoid(z)
            silu = z * sz
            u = (att * silu).astype(BF16)
            u_ref[...] = u.T
            return att, z, sz, silu, _dot_nt(u, w_ref[...])

        att_a, z_a, sz_a, silu_a, y_a = branch_fwd(aa_ref, za_ref, woat_ref, ua_ref)
        att_b, z_b, sz_b, silu_b, y_b = branch_fwd(ab_ref, zb_ref, wobt_ref, ub_ref)
        sg_a, sg_b = _sigmoid(ga_ref[...]), _sigmoid(gb_ref[...])
        merged = (sg_a * y_a + sg_b * y_b).astype(BF16)
        mg_ref[...] = merged.T
        o = _dot(merged, wout_ref[...])
        x1 = x_ref[...] + gate * o
        rstd = lax.rsqrt(jnp.mean(x1 * x1, axis=-1, keepdims=True) + NORM_EPS)
        xh = x1 * rstd
        diff = xh * gfin - t_ref[...]
        acc_ref[2:3, :] += (0.5 * inv_d) * jnp.sum(diff * diff, axis=0, keepdims=True)
        dy = diff * inv_d
        acc_ref[1:2, :] += jnp.sum(dy * xh, axis=0, keepdims=True)
        dyg = dy * gfin
        dx1 = rstd * (dyg - xh * jnp.mean(dyg * xh, axis=-1, keepdims=True))
        dx1_ref[...] = dx1
        acc_ref[0:1, :] += jnp.sum(dx1 * o, axis=0, keepdims=True)
        d_o = (dx1 * gate).astype(BF16)
        do_ref[...] = d_o
        dmg = _dot_nt(d_o, wout_ref[...])

        lane = _lane(CHUNK)
        low = lane < HEAD_DIM
        zg0 = SEG["za"][0]

        def branch_bwd(sg, y, att, z, sz, silu, wt_ref, dy_ref, g_name, z_name, doaug_ref):
            dyb = (dmg * sg).astype(BF16)
            dy_ref[...] = dyb
            g_off, z_off = SEG[g_name][0] - zg0, SEG[z_name][0] - zg0
            dzg_ref[:, g_off:g_off + D_MODEL] = (dmg * y * sg * (1.0 - sg)).astype(BF16)
            du = _dot(dyb, wt_ref[...])
            datt = du * silu
            dzg_ref[:, z_off:z_off + 512] = (du * att * (sz * (1.0 + z * (1.0 - sz)))).astype(BF16)
            extra = _spread3(-_dot_split(datt * att, ind_ref[...]), e_ref)
            for p in range(N_HEADS // 2):
                c = datt[:, CHUNK * p:CHUNK * (p + 1)]
                for hh in range(2):
                    hd = 2 * p + hh
                    src = c if hh == 0 else pltpu.roll(c, HEAD_DIM, 1)
                    doaug_ref[:, CHUNK * hd:CHUNK * (hd + 1)] = jnp.where(
                        low, src, extra[:, CHUNK * hd:CHUNK * (hd + 1)]).astype(BF16)

        branch_bwd(sg_a, y_a, att_a, z_a, sz_a, silu_a, woat_ref, dya_ref, "ga", "za", doa_ref)
        branch_bwd(sg_b, y_b, att_b, z_b, sz_b, silu_b, wobt_ref, dyb_ref, "gb", "zb", dob_ref)

    row_blk = lambda w: pl.BlockSpec((tm, w), lambda i: (i, 0))
    full = lambda a: pl.BlockSpec(a.shape, lambda i: (0,) * a.ndim)
    sds = lambda w, dt: jax.ShapeDtypeStruct((s, w), dt)
    tds = lambda w: jax.ShapeDtypeStruct((w, s), BF16)
    col_blk = lambda w: pl.BlockSpec((w, tm), lambda i: (0, i))
    out_shape = [sds(D_MODEL, F32), tds(512), tds(512), tds(D_MODEL), sds(D_MODEL, BF16),
                 sds(D_MODEL, BF16), sds(D_MODEL, BF16), sds(AUG_W, BF16), sds(AUG_W, BF16), sds(3072, BF16),
                 jax.ShapeDtypeStruct((8, D_MODEL), F32)]
    ins = [x, tgt, att_a, att_b, za, zb, ga, gb, woa_t, wob_t, wout, vec, ind, emat]
    in_specs = [row_blk(a.shape[1]) for a in ins[:8]] + [full(a) for a in ins[8:]]
    out_specs = ([row_blk(D_MODEL), col_blk(512), col_blk(512), col_blk(D_MODEL)]
                 + [row_blk(o.shape[1]) for o in out_shape[4:-1]] + [pl.BlockSpec((8, D_MODEL), lambda i: (0, 0))])
    return pl.pallas_call(body, name="post", grid=(s // tm,), out_shape=out_shape, in_specs=in_specs,
                          out_specs=out_specs, compiler_params=_params(1))(*ins)


def _bwd_pre(dproj, w_all, x, dx1, mod, tm, chip_halves, modes):
    s = x.shape[0]
    ng, nh = len(DPROJ_GROUPS), len(chip_halves)
    last = s // tm - 1

    def body(*refs):
        dp_refs = refs[:ng]
        wt_ref, x_ref, dx1_ref, mod_ref = refs[ng:ng + 4]
        send_refs = refs[ng + 4:ng + 4 + nh]
        gx_ref, acc_ref = refs[ng + 4 + nh:ng + 6 + nh]
        recv_refs = refs[ng + 6 + nh:ng + 6 + 2 * nh]
        sems = refs[ng + 6 + 2 * nh:]

        @pl.when(pl.program_id(0) == 0)
        def _():
            acc_ref[...] = jnp.zeros_like(acc_ref)
            _exchange_start(_exchange_copies(send_refs, recv_refs, modes, *sems))

        dh = None
        for dp_ref, (_, off, w) in zip(dp_refs, DPROJ_GROUPS):
            part = _dot_nt(dp_ref[...], wt_ref[:, off:off + w])
            dh = part if dh is None else dh + part
        xv = x_ref[...]
        rstd = lax.rsqrt(jnp.mean(xv * xv, axis=-1, keepdims=True) + NORM_EPS)
        xh = xv * rstd
        gn = mod_ref[0:1, :]
        acc_ref[0:1, :] += jnp.sum(dh, axis=0, keepdims=True)
        acc_ref[1:2, :] += jnp.sum(dh * (xh * gn), axis=0, keepdims=True)
        dn = dh * (1.0 + mod_ref[1:2, :])
        acc_ref[2:3, :] += jnp.sum(dn * xh, axis=0, keepdims=True)
        dxh = dn * gn
        gx_ref[...] = dx1_ref[...] + rstd * (dxh - xh * jnp.mean(dxh * xh, axis=-1, keepdims=True))

        @pl.when(pl.program_id(0) == last)
        def _():
            _exchange_wait(_exchange_copies(send_refs, recv_refs, modes, *sems))

    row_blk = lambda w: pl.BlockSpec((tm, w), lambda i: (i, 0))
    full = lambda a: pl.BlockSpec(a.shape, lambda i: (0,) * a.ndim)
    outs = pl.pallas_call(
        body, name="bwd_pre", grid=(s // tm,),
        out_shape=[jax.ShapeDtypeStruct((s, D_MODEL), F32), jax.ShapeDtypeStruct((8, D_MODEL), F32)]
        + _exchange_out_shapes(chip_halves, modes),
        in_specs=[row_blk(w) for _, _, w in DPROJ_GROUPS] + [_resident(w_all), row_blk(D_MODEL), row_blk(D_MODEL), full(mod)]
        + [ANY_SPEC] * nh,
        out_specs=[row_blk(D_MODEL), pl.BlockSpec((8, D_MODEL), lambda i: (0, 0))] + [ANY_SPEC] * nh,
        scratch_shapes=_exchange_sems(nh),
        compiler_params=_params(1))(*dproj, w_all, x, dx1, mod, *chip_halves)
    return outs[0], outs[1], outs[2:]


def _accumulate_tokens(a, b_ref, acc_ref, o_ref, last):
    @pl.when(pl.program_id(0) == 0)
    def _():
        acc_ref[...] = jnp.zeros_like(acc_ref)

    acc_ref[...] += _dot(a, b_ref[...])

    @pl.when(pl.program_id(0) == last)
    def _():
        o_ref[...] = acc_ref[...].astype(BF16)


def _token_matmul_multi(a_t, bs, tk, name):
    m, s = a_t.shape
    nb = len(bs)
    last = s // tk - 1

    def body(*refs):
        a_ref, b_refs, o_refs, acc_refs = refs[0], refs[1:1 + nb], refs[1 + nb:1 + 2 * nb], refs[1 + 2 * nb:]
        a = a_ref[...]
        for b_ref, o_ref, acc_ref in zip(b_refs, o_refs, acc_refs):
            _accumulate_tokens(a, b_ref, acc_ref, o_ref, last)

    return pl.pallas_call(
        body, name=name, grid=(s // tk,),
        out_shape=[jax.ShapeDtypeStruct((m, b.shape[1]), BF16) for b in bs],
        in_specs=[pl.BlockSpec((m, tk), lambda k: (0, k))] + [pl.BlockSpec((tk, b.shape[1]), lambda k: (k, 0)) for b in bs],
        out_specs=[pl.BlockSpec((m, b.shape[1]), lambda k: (0, 0)) for b in bs],
        scratch_shapes=[pltpu.VMEM((m, b.shape[1]), F32) for b in bs],
        compiler_params=_params(1))(a_t, *bs)


def _adam_math(g, w, m, v):
    m2 = ADAM_B1 * m + (1.0 - ADAM_B1) * g
    v2 = ADAM_B2 * v + (1.0 - ADAM_B2) * (g * g)
    delta = -ADAM_LR * ((m2 / ADAM_C1) / (jnp.sqrt(v2 / ADAM_C2) + ADAM_EPS) + ADAM_WD * w)
    return delta, m2, v2


def _adamw_parts(full, parts, others, w, m, v, tr, name):
    _, rws, cols = w.shape

    def body(oth_ref, f_ref, p_ref, w_ref, m_ref, v_ref, g_ref, d_ref, m2_ref, v2_ref):
        g = None
        for chip in range(N_CHIPS):
            part = (1.0 - oth_ref[chip]) * f_ref[chip] + oth_ref[chip] * p_ref[chip].astype(F32)
            g = part if g is None else g + part
        g_ref[0] = g
        d_ref[0], m2_ref[0], v2_ref[0] = _adam_math(g, w_ref[0], m_ref[0], v_ref[0])

    blk = pl.BlockSpec((1, tr, cols), lambda i: (0, i, 0))
    slots = pl.BlockSpec((N_CHIPS, tr, cols), lambda i: (0, i, 0))
    o = jax.ShapeDtypeStruct((1, rws, cols), F32)
    return pl.pallas_call(
        body, name=name, grid=(rws // tr,), out_shape=[o, o, o, o],
        in_specs=[pl.BlockSpec(memory_space=pltpu.SMEM), slots, slots, blk, blk, blk],
        out_specs=[blk, blk, blk, blk], compiler_params=_params(1))(others, full, parts, w, m, v)


def _adamw_devices(parts, w, m, v, tr, name):
    _, rws, cols = w.shape

    def body(p_ref, w_ref, m_ref, v_ref, g_ref, d_ref, m2_ref, v2_ref):
        g = p_ref[0].astype(F32)
        for dev in range(1, N_DEV):
            g = g + p_ref[dev].astype(F32)
        g_ref[0] = g
        d_ref[0], m2_ref[0], v2_ref[0] = _adam_math(g, w_ref[0], m_ref[0], v_ref[0])

    blk = pl.BlockSpec((1, tr, cols), lambda i: (0, i, 0))
    o = jax.ShapeDtypeStruct((1, rws, cols), F32)
    return pl.pallas_call(
        body, name=name, grid=(rws // tr,), out_shape=[o, o, o, o],
        in_specs=[pl.BlockSpec((N_DEV, tr, cols), lambda i: (0, i, 0)), blk, blk, blk],
        out_specs=[blk, blk, blk, blk], compiler_params=_params(1))(parts, w, m, v)


def _adamw_ada(c_lanes, d_rows, w, m, v, tr):
    _, rws, cols = w.shape

    def body(c_ref, d_ref, w_ref, m_ref, v_ref, g_ref, dl_ref, m2_ref, v2_ref):
        for k in range(cols // CHUNK):
            cs = slice(CHUNK * k, CHUNK * (k + 1))
            g = c_ref[0] * d_ref[0:1, cs]
            for b in range(1, N_DEV):
                g = g + c_ref[b] * d_ref[b:b + 1, cs]
            g_ref[0, :, cs] = g
            dl_ref[0, :, cs], m2_ref[0, :, cs], v2_ref[0, :, cs] = _adam_math(
                g, w_ref[0, :, cs], m_ref[0, :, cs], v_ref[0, :, cs])

    blk = pl.BlockSpec((1, tr, cols), lambda i: (0, i, 0))
    o = jax.ShapeDtypeStruct((1, rws, cols), F32)
    return pl.pallas_call(
        body, name="adamw_ada", grid=(rws // tr,), out_shape=[o, o, o, o],
        in_specs=[pl.BlockSpec((N_DEV, tr, CHUNK), lambda i: (0, i, 0)), pl.BlockSpec((N_DEV, cols), lambda i: (0, 0)),
                  blk, blk, blk],
        out_specs=[blk, blk, blk, blk], compiler_params=_params(1))(c_lanes, d_rows, w, m, v)


SMALL = (("b_ada", 0, 0, 3 * D_MODEL), ("g_norm", 1, 0, D_MODEL), ("b_f", 3, 0, N_HEADS), ("sinks", 4, 0, N_HEADS),
         ("g_final", 1, D_MODEL, D_MODEL))
LOSS_ROW = 2


def _pack_small(pre_acc, post_acc, dbf_acc, dsink):
    def body(pre_ref, post_ref, dbf_ref, dsink_ref, o_ref):
        o_ref[...] = jnp.zeros_like(o_ref)
        d = D_MODEL
        o_ref[0:1, 0:d], o_ref[0:1, d:2 * d], o_ref[0:1, 2 * d:3 * d] = pre_ref[0:1, :], pre_ref[1:2, :], post_ref[0:1, :]
        o_ref[1:2, 0:d], o_ref[1:2, d:2 * d] = pre_ref[2:3, :], post_ref[1:2, :]
        o_ref[LOSS_ROW:LOSS_ROW + 1, 0:d] = post_ref[2:3, :]
        o_ref[3:4, 0:CHUNK] = dbf_ref[0:1, :]
        lane = _lane(CHUNK)
        sinks = jnp.zeros((1, CHUNK), F32)
        for kv in range(KV_GROUPS):
            for g in range(GROUP):
                sinks = jnp.where(lane == GROUP * kv + g, dsink_ref[kv, g:g + 1, :], sinks)
        o_ref[4:5, 0:CHUNK] = sinks

    return pl.pallas_call(body, name="pack_small", out_shape=jax.ShapeDtypeStruct((8, 3 * D_MODEL), F32),
                          compiler_params=_params(0))(pre_acc, post_acc, dbf_acc, dsink)


def _adamw_small(packs, params):
    n = len(SMALL)

    def body(*refs):
        p_ref, wmv = refs[0], refs[1:1 + 3 * n]
        outs, loss_ref, tot_ref = refs[1 + 3 * n:1 + 7 * n], refs[1 + 7 * n], refs[2 + 7 * n]
        tot = p_ref[0]
        for dev in range(1, N_DEV):
            tot = tot + p_ref[dev]
        tot_ref[...] = tot
        for i, (_, row, lo, w) in enumerate(SMALL):
            g = tot_ref[row:row + 1, lo:lo + w]
            o = outs[4 * i:4 * i + 4]
            o[0][...] = g
            o[1][...], o[2][...], o[3][...] = _adam_math(g, wmv[3 * i][...], wmv[3 * i + 1][...], wmv[3 * i + 2][...])
        loss_ref[...] = jnp.broadcast_to(jnp.sum(tot_ref[LOSS_ROW:LOSS_ROW + 1, 0:D_MODEL], axis=1, keepdims=True),
                                         loss_ref.shape)

    flat = [a for wmv in params for a in wmv]
    out_shape = [jax.ShapeDtypeStruct(wmv[0].shape, F32) for wmv in params for _ in range(4)]
    res = pl.pallas_call(body, name="adamw_small", out_shape=out_shape + [jax.ShapeDtypeStruct((8, CHUNK), F32)],
                         scratch_shapes=[pltpu.VMEM((8, 3 * D_MODEL), F32)],
                         compiler_params=_params(0))(packs, *flat)
    return [tuple(res[4 * i:4 * i + 4]) for i in range(n)], res[4 * n]


def _tile(s, want):
    return min(s, want)


def _layout_pieces():
    orig = {"qa": 0, "ka": 512, "va": 640, "za": 768, "qb": 1280, "kb": 1792, "vb": 2304, "f": 2816, "zb": 2824,
            "ga": 3336, "gb": 4360}
    pieces = []
    for name, (off, w) in SEG.items():
        if name in ("ka", "va"):
            pieces += [(orig[name] + HEAD_DIM * kv, orig[name] + HEAD_DIM * (kv + 1), off + CHUNK * kv) for kv in range(KV_GROUPS)]
        else:
            pieces.append((orig[name], orig[name] + (N_HEADS if name == "f" else w), off))
    return pieces


def _assemble_w_all(win_g):
    cols, pos = [], 0
    for lo, hi, new in sorted(_layout_pieces(), key=lambda t: t[2]):
        if new > pos:
            cols.append(jnp.zeros((D_MODEL, new - pos), win_g.dtype))
        col = lo
        while col < hi:
            dev = col // IN_SHARD
            end = min(hi, (dev + 1) * IN_SHARD)
            cols.append(win_g[dev, :, col - dev * IN_SHARD:end - dev * IN_SHARD])
            col = end
        pos = new + hi - lo
    cols.append(jnp.zeros((D_MODEL, PROJ_W - pos), win_g.dtype))
    return jnp.concatenate(cols, axis=1)


def _grad_slot(dw_groups, dev):
    lo_d, hi_d = dev * IN_SHARD, (dev + 1) * IN_SHARD
    cols = []
    for lo, hi, new in sorted(_layout_pieces()):
        a, b = max(lo, lo_d), min(hi, hi_d)
        if a < b:
            arr, off = next((g, o) for g, (_, o, w) in zip(dw_groups, DPROJ_GROUPS) if o <= new < o + w)
            cols.append(arr[:, new - off + a - lo:new - off + b - lo])
    return jnp.concatenate(cols, axis=1)


def kernel(x, c, positions, w_ada, b_ada, g_norm, w_in, b_f, sinks, w_o_swa, w_o_fox, w_out, g_final, loss_target, m_w_ada, m_b_ada, m_g_norm, m_w_in, m_b_f, m_sinks, m_w_o_swa, m_w_o_fox, m_w_out, m_g_final, v_w_ada, v_b_ada, v_g_norm, v_w_in, v_b_f, v_sinks, v_w_o_swa, v_w_o_fox, v_w_out, v_g_final):
    s = x.shape[1]
    tm = _tile(s, 256)
    ta = _tile(s, 512)
    me = 4 * lax.axis_index("x") + 2 * lax.axis_index("y") + lax.axis_index("c")
    x2, tgt = x[0], loss_target[0]

    inv_freq = np.power(np.float32(ROPE_THETA), -np.arange(0, HEAD_DIM, 2, dtype=np.float32) / HEAD_DIM)
    inv_freq = jnp.asarray(np.tile(inv_freq, CHUNK // (HEAD_DIM // 2))[None, :], F32)
    (cos, sa, sb), (win_g,), c_all = _rope_tables_and_gather(
        positions.reshape(s, 1).astype(F32), inv_freq, tm, [w_in[0].astype(BF16)], jnp.broadcast_to(c, (8, D_MODEL)))
    c_all = c_all[:, 0, :]
    b_shard = lax.dynamic_slice(b_ada, (0, me * ADA_SHARD), (1, ADA_SHARD))
    ada_part = _ada_fwd(c_all, w_ada[0], b_shard)
    (ada_g,) = _exchange([ada_part], ["gather"], "gather_ada")
    ada = lax.dynamic_index_in_dim(ada_g, me, axis=1, keepdims=False).reshape(3 * D_MODEL)
    shift, scale, gate = ada[:D_MODEL], ada[D_MODEL:2 * D_MODEL], ada[2 * D_MODEL:]
    w_all = _assemble_w_all(win_g)

    zrow = jnp.zeros((1, D_MODEL), F32)
    mod = jnp.concatenate([g_norm, scale[None], shift[None], zrow, zrow, zrow, zrow, zrow], axis=0)
    bf_row = jnp.pad(b_f, ((0, 0), (0, CHUNK - N_HEADS)))
    emat_np = np.zeros((3 * CHUNK, AUG_W), np.float32)
    ind_np = np.zeros((512, CHUNK), np.float32)
    for hd in range(N_HEADS):
        for part in range(3):
            emat_np[CHUNK * part + hd, CHUNK * hd + 64 + part] = 1.0
        ind_np[HEAD_DIM * hd:HEAD_DIM * (hd + 1), hd] = 1.0
    emat, ind = jnp.asarray(emat_np, BF16), jnp.asarray(ind_np, BF16)
    sink_rows = jnp.broadcast_to(jnp.pad(sinks.reshape(KV_GROUPS, GROUP), ((0, 0), (0, 8 - GROUP)))[:, :, None],
                                 (KV_GROUPS, 8, CHUNK))

    (h_t, qa, ka, va, za, qb, kb, vb, zb, ga, gb, xf, bounds, norms), (woa_g, wob_g, wout_g) = _fwd_proj(
        x2, mod, w_all, cos, sa, sb, bf_row, emat, ind, ta,
        [w_o_swa[0].T.astype(BF16), w_o_fox[0].T.astype(BF16), w_out[0].astype(BF16)])
    woa_t, wob_t, wout = woa_g.reshape(D_MODEL, 512), wob_g.reshape(D_MODEL, 512), wout_g.reshape(D_MODEL, D_MODEL)
    first_key, last_query = _fox_block_ranges(bounds, norms, 1)
    att_a, qa_bwd = _swa_fwd(qa, ka, va, sink_rows, _tile(s, 1024))
    att_b, qb_bwd = _fox_fwd(first_key, qb, kb, vb, ta)
    vec = jnp.concatenate([gate[None], g_final[None], zrow, zrow, zrow, zrow, zrow, zrow], axis=0)
    (dx1, ua_t, ub_t, merged_t, d_o, dya, dyb, doa, dob, dzg, post_acc) = _post(
        x2, tgt, att_a, att_b, za, zb, ga, gb, woa_t, wob_t, wout, vec, ind, emat, tm)

    da, dsink = _swa_bwd(ka, va, qa_bwd, doa, cos, sa, sb, sink_rows, ta)
    dkb, dvb, dck, dqb, dcq = _fox_bwd(last_query, kb, vb, qb_bwd, dob, ta)
    dfb, dbf_acc = _fgate_bwd(dcq, dck, xf, ta)
    dproj = [da, dqb, dkb, dvb, dzg, dfb]
    tw = _tile(s, 1024)
    dw_zg, = _token_matmul_multi(h_t, [dzg], tw, "dw_in_zg")
    dw_a, dw_qb, dw_kb, dw_vb, dw_f = _token_matmul_multi(h_t, [da, dqb, dkb, dvb, dfb], tw, "dw_in_rest")
    dw_all = [dw_a, dw_qb, dw_kb, dw_vb, dw_zg, dw_f]

    core, chip = lax.axis_index("c"), 2 * lax.axis_index("x") + lax.axis_index("y")
    win_slots = jnp.stack([jnp.stack([_grad_slot(dw_all, 2 * ch + co) for ch in range(N_CHIPS)]) for co in range(2)])
    (dwoa, dwob, dwout), (win_theirs,) = _token_matmuls_and_swap(
        [(ua_t, dya), (ub_t, dyb), (merged_t, d_o)], ta, [win_slots])
    win_full, win_half = _chip_partial(jnp.reshape(core, (1,)).astype(jnp.int32), win_slots, win_theirs, 1024,
                                       "chip_partial_w_in")
    col_slots = lambda g: g.reshape(512, N_DEV, 128).transpose(1, 0, 2)
    grad_x, pre_acc, (p_win, p_woa, p_wob, p_wout) = _bwd_pre(
        dproj, w_all, x2, dx1, mod, ta,
        [win_half, col_slots(dwoa), col_slots(dwob), dwout.reshape(N_DEV, 128, D_MODEL)],
        ["chips", "devices", "devices", "devices"])
    (packs,) = _exchange([_pack_small(pre_acc, post_acc, dbf_acc, dsink)], ["gather"], "gather_small_grads")
    others = jnp.where(jnp.arange(N_CHIPS) == chip, 0.0, 1.0).astype(F32)

    g_win, d_win, m_win, v_win = _adamw_parts(win_full, p_win, others, w_in, m_w_in, v_w_in, 256, "adamw_w_in")
    g_woa, d_woa, m_woa, v_woa = _adamw_devices(p_woa, w_o_swa, m_w_o_swa, v_w_o_swa, 128, "adamw_w_o_swa")
    g_wob, d_wob, m_wob, v_wob = _adamw_devices(p_wob, w_o_fox, m_w_o_fox, v_w_o_fox, 128, "adamw_w_o_fox")
    g_wout, d_wout, m_wout, v_wout = _adamw_devices(p_wout, w_out, m_w_out, v_w_out, 128, "adamw_w_out")
    d_ada_rows = lax.dynamic_slice(packs[:, 0, :], (0, me * ADA_SHARD), (N_DEV, ADA_SHARD))
    c_lanes = jnp.broadcast_to(c_all[:, :, None], (N_DEV, D_MODEL, CHUNK))
    g_wada, d_wada, m_wada, v_wada = _adamw_ada(c_lanes, d_ada_rows, w_ada, m_w_ada, v_w_ada, 256)

    row = lambda a: a.reshape(1, D_MODEL)
    small, loss_rows = _adamw_small(packs, [(b_ada, m_b_ada, v_b_ada), (g_norm, m_g_norm, v_g_norm), (b_f, m_b_f, v_b_f),
                                            (sinks, m_sinks, v_sinks), (row(g_final), row(m_g_final), row(v_g_final))])
    bada_o, gn_o, bf_o, sk_o, gf_o = small

    outs = []
    for k, big in enumerate(((g_wada, g_win, g_woa, g_wob, g_wout), (d_wada, d_win, d_woa, d_wob, d_wout),
                             (m_wada, m_win, m_woa, m_wob, m_wout), (v_wada, v_win, v_woa, v_wob, v_wout))):
        wada_o, win_o, woa_o, wob_o, wout_o = big
        outs += [wada_o, bada_o[k], gn_o[k], win_o, bf_o[k], sk_o[k], woa_o, wob_o, wout_o, gf_o[k].reshape(D_MODEL)]
    return (loss_rows[0, 0], grad_x[None], *outs)
```

```python
import numpy as np
import jax
import jax.numpy as jnp
from jax import lax
from jax.experimental import pallas as pl
from jax.experimental.pallas import tpu as pltpu

F32 = jnp.float32
BF16 = jnp.bfloat16

D_MODEL = 1024
HEAD_DIM = 64
N_HEADS = 8
KV_GROUPS = 2
GROUP = N_HEADS // KV_GROUPS
WINDOW = 128
CHUNK = 128
AUG_W = N_HEADS * CHUNK
N_DEV = 8
IN_SHARD = 673
ADA_SHARD = 384
NORM_EPS = 1e-6
ROPE_THETA = 10000.0
Q_SCALE = HEAD_DIM ** -0.5
NEG = -1e30
UNDERFLOW = 105.0

ADAM_LR = 0.001
ADAM_B1 = 0.9
ADAM_B2 = 0.999
ADAM_EPS = 1e-08
ADAM_WD = 0.01
ADAM_STEP = 10
ADAM_C1 = 1.0 - ADAM_B1 ** ADAM_STEP
ADAM_C2 = 1.0 - ADAM_B2 ** ADAM_STEP

SEG = {}
_off = 0
for _name, _w in (("qa", 512), ("ka", 256), ("va", 256), ("qb", 512), ("kb", 512), ("vb", 512),
                  ("za", 512), ("zb", 512), ("ga", 1024), ("gb", 1024), ("f", 128)):
    SEG[_name] = (_off, _w)
    _off += _w
PROJ_W = _off
DPROJ_GROUPS = (("a", SEG["qa"][0], 1024), ("qb", SEG["qb"][0], 512), ("kb", SEG["kb"][0], 512),
                ("vb", SEG["vb"][0], 512), ("zg", SEG["za"][0], 3072), ("f", SEG["f"][0], 128))

VMEM_LIMIT = 56 * 1024 * 1024


def _params(n_axes):
    return pltpu.CompilerParams(dimension_semantics=("arbitrary",) * n_axes, vmem_limit_bytes=VMEM_LIMIT)


def _resident(a):
    return pl.BlockSpec(a.shape, lambda i: (0, 0), pipeline_mode=pl.Buffered(1))


def _dot(a, b):
    return jnp.dot(a, b, preferred_element_type=F32)


def _dot_nt(a, b):
    return lax.dot_general(a, b, (((1,), (1,)), ((), ())), preferred_element_type=F32)


def _dot_tn(a, b):
    return lax.dot_general(a, b, (((0,), (0,)), ((), ())), preferred_element_type=F32)


def _lane(n):
    return lax.broadcasted_iota(jnp.int32, (1, n), 1)


def _split3(x):
    hi = x.astype(BF16).astype(F32)
    r = x - hi
    mid = r.astype(BF16).astype(F32)
    lo = (r - mid).astype(BF16).astype(F32)
    return hi, mid, lo


def _dot_split(x, b):
    hi, mid, lo = _split3(x)
    return _dot(hi.astype(BF16), b) + _dot(mid.astype(BF16), b) + _dot(lo.astype(BF16), b)


def _spread3(x, e3_ref):
    return _dot(jnp.concatenate(_split3(x), axis=1).astype(BF16), e3_ref[...])


def _place3(lane, base, x, other):
    hi, mid, lo = _split3(x)
    return jnp.where(lane == base, hi, jnp.where(lane == base + 1, mid, jnp.where(lane == base + 2, lo, other)))


def _lane_sum(x, lo, hi):
    lane = _lane(x.shape[1])
    return jnp.sum(jnp.where((lane >= lo) & (lane < hi), x, 0.0), axis=1, keepdims=True)


def _ones_lanes(lo, hi):
    lane = _lane(CHUNK)
    return jnp.where((lane >= lo) & (lane < hi), 1.0, 0.0).astype(F32)


def _rope(c, cos, sa, sb):
    return c * cos + pltpu.roll(c, CHUNK - 32, 1) * sa + pltpu.roll(c, 32, 1) * sb


def _rope_inv(d, cos, sa, sb):
    return d * cos - (pltpu.roll(d, CHUNK - 32, 1) * sa + pltpu.roll(d, 32, 1) * sb)


def _pair(c0, c1):
    return jnp.where(_lane(CHUNK) < HEAD_DIM, c0, pltpu.roll(c1, HEAD_DIM, 1))


def _sigmoid(x):
    return 0.5 * jnp.tanh(0.5 * x) + 0.5


N_CHIPS = 4
ANY_SPEC = pl.BlockSpec(memory_space=pl.ANY)


def _exchange(arrays, modes, name):
    n = len(arrays)

    def body(*refs):
        copies = _exchange_copies(refs[:n], refs[n:2 * n], modes, *refs[2 * n:])
        _exchange_start(copies)
        _exchange_wait(copies)

    return pl.pallas_call(
        body, name=name, out_shape=_exchange_out_shapes(arrays, modes),
        in_specs=[ANY_SPEC] * n, out_specs=[ANY_SPEC] * n, scratch_shapes=_exchange_sems(n),
    )(*arrays)


def _exchange_out_shapes(arrays, modes):
    return [jax.ShapeDtypeStruct((N_DEV,) + a.shape if md == "gather" else a.shape, a.dtype)
            for a, md in zip(arrays, modes)]


def _exchange_sems(n):
    return [pltpu.SemaphoreType.DMA((n, N_DEV - 1)), pltpu.SemaphoreType.DMA((n, N_DEV - 1)),
            pltpu.SemaphoreType.DMA((n,))]


def _exchange_copies(srcs, dsts, modes, send_sems, recv_sems, loc_sems):
    n = len(srcs)
    x, y, c = lax.axis_index("x"), lax.axis_index("y"), lax.axis_index("c")

    def slot(i, px, py, pc):
        return 2 * px + py if modes[i] == "chips" else 4 * px + 2 * py + pc

    def src_of(i, px, py, pc):
        return srcs[i] if modes[i] == "gather" else srcs[i].at[slot(i, px, py, pc)]

    local = [pltpu.make_async_copy(src_of(i, x, y, c), dsts[i].at[slot(i, x, y, c)], loc_sems.at[i])
             for i in range(n)]
    sends, recvs = [], []
    for r in (1, 2, 4, 3, 5, 6, 7):
        px = 1 - x if r & 4 else x
        py = 1 - y if r & 2 else y
        pc = 1 - c if r & 1 else c
        for i in range(n):
            if modes[i] == "chips" and r & 1:
                continue
            sems = dict(send_sem=send_sems.at[i, r - 1], recv_sem=recv_sems.at[i, r - 1],
                        device_id=(px, py, pc), device_id_type=pl.DeviceIdType.MESH)
            sends.append(pltpu.make_async_remote_copy(
                src_ref=src_of(i, px, py, pc), dst_ref=dsts[i].at[slot(i, x, y, c)], **sems))
            recvs.append(pltpu.make_async_remote_copy(
                src_ref=src_of(i, px, py, pc), dst_ref=dsts[i].at[slot(i, px, py, pc)], **sems))
    return local, sends, recvs


def _exchange_start(copies):
    local, sends, _ = copies
    for cp in local + sends:
        cp.start()


def _exchange_wait(copies):
    local, sends, recvs = copies
    for cp in recvs:
        cp.wait_recv()
    for cp in sends:
        cp.wait_send()
    for cp in local:
        cp.wait()


def _gather2_parts(srcs, dsts, send_sems, recv_sems, loc_sems):
    n = len(srcs)
    x, y, c = lax.axis_index("x"), lax.axis_index("y"), lax.axis_index("c")
    me, sibling = (x, y, c), (x, y, 1 - c)
    chips = [(1 - x, y), (x, 1 - y), (1 - x, 1 - y)]

    def rows(i, dev):
        return dsts[i].at[4 * dev[0] + 2 * dev[1] + dev[2]]

    def copy(i, k, block, to, src=None):
        return pltpu.make_async_remote_copy(
            src_ref=rows(i, block) if src is None else src, dst_ref=rows(i, block),
            send_sem=send_sems.at[i, k], recv_sem=recv_sems.at[i, k],
            device_id=to, device_id_type=pl.DeviceIdType.MESH)

    local = [pltpu.make_async_copy(srcs[i], rows(i, me), loc_sems.at[i]) for i in range(n)]
    first = []
    for i in range(n):
        first.append(copy(i, 0, me, sibling, src=srcs[i]))
        first += [copy(i, 1 + j, me, (*chip, c), src=srcs[i]) for j, chip in enumerate(chips)]
    return n, c, me, sibling, chips, copy, local, first


def _gather2_start(srcs, dsts, send_sems, recv_sems, loc_sems):
    *_, local, first = _gather2_parts(srcs, dsts, send_sems, recv_sems, loc_sems)
    for cp in local + first:
        cp.start()


def _gather2_finish(srcs, dsts, send_sems, recv_sems, loc_sems):
    n, c, me, sibling, chips, copy, local, first = _gather2_parts(srcs, dsts, send_sems, recv_sems, loc_sems)
    passed = []
    for j, chip in enumerate(chips):
        for i in range(n):
            copy(i, 1 + j, (*chip, c), me).wait_recv()
            cp = copy(i, 4 + j, (*chip, c), sibling)
            cp.start()
            passed.append(cp)
    for i in range(n):
        copy(i, 0, sibling, me).wait_recv()
        for j, chip in enumerate(chips):
            copy(i, 4 + j, (*chip, 1 - c), me).wait_recv()
    for cp in first + passed:
        cp.wait_send()
    for cp in local:
        cp.wait()


def _sibling_copies(srcs, dsts, send_sems, recv_sems):
    x, y, c = lax.axis_index("x"), lax.axis_index("y"), lax.axis_index("c")
    return [pltpu.make_async_remote_copy(
        src_ref=srcs[i].at[1 - c, k], dst_ref=dsts[i].at[k], send_sem=send_sems.at[i, k], recv_sem=recv_sems.at[i, k],
        device_id=(x, y, 1 - c), device_id_type=pl.DeviceIdType.MESH)
        for i in range(len(srcs)) for k in range(N_CHIPS)]


def _token_matmuls_and_swap(pairs, tk, swap):
    npair, nsw = len(pairs), len(swap)
    s = pairs[0][0].shape[1]
    last = s // tk - 1

    def body(*refs):
        a_refs, b_refs = refs[0:2 * npair:2], refs[1:2 * npair:2]
        src = refs[2 * npair:2 * npair + nsw]
        o_refs = refs[2 * npair + nsw:3 * npair + nsw]
        dst = refs[3 * npair + nsw:3 * npair + 2 * nsw]
        sems = refs[3 * npair + 2 * nsw:3 * npair + 2 * nsw + 2]
        acc_refs = refs[3 * npair + 2 * nsw + 2:]

        @pl.when(pl.program_id(0) == 0)
        def _():
            for cp in _sibling_copies(src, dst, *sems):
                cp.start()

        for a_ref, b_ref, o_ref, acc_ref in zip(a_refs, b_refs, o_refs, acc_refs):
            _accumulate_tokens(a_ref[...], b_ref, acc_ref, o_ref, last)

        @pl.when(pl.program_id(0) == last)
        def _():
            for cp in _sibling_copies(src, dst, *sems):
                cp.wait()

    in_specs, args = [], []
    for a_t, b in pairs:
        in_specs += [pl.BlockSpec((a_t.shape[0], tk), lambda k: (0, k)), pl.BlockSpec((tk, b.shape[1]), lambda k: (k, 0))]
        args += [a_t, b]
    outs = pl.pallas_call(
        body, name="dw_out_projs_swap", grid=(s // tk,),
        out_shape=[jax.ShapeDtypeStruct((a_t.shape[0], b.shape[1]), BF16) for a_t, b in pairs]
        + [jax.ShapeDtypeStruct(a.shape[1:], a.dtype) for a in swap],
        in_specs=in_specs + [ANY_SPEC] * nsw,
        out_specs=[pl.BlockSpec((a_t.shape[0], b.shape[1]), lambda k: (0, 0)) for a_t, b in pairs] + [ANY_SPEC] * nsw,
        scratch_shapes=[pltpu.SemaphoreType.DMA((nsw, N_CHIPS)), pltpu.SemaphoreType.DMA((nsw, N_CHIPS))]
        + [pltpu.VMEM((a_t.shape[0], b.shape[1]), F32) for a_t, b in pairs],
        compiler_params=_params(1))(*args, *swap)
    return outs[:npair], outs[npair:]


def _chip_partial(core, slots, theirs, tr, name):
    _, k, rws, cols = slots.shape

    def body(core_ref, a_ref, b_ref, f_ref, h_ref):
        sm = a_ref[0].astype(F32) + b_ref[...].astype(F32)
        f_ref[...] = sm
        h_ref[...] = sm.astype(BF16)

    blk = pl.BlockSpec((1, tr, cols), lambda j, i, core_ref: (j, i, 0))
    mine = pl.BlockSpec((1, 1, tr, cols), lambda j, i, core_ref: (core_ref[0], j, i, 0))
    return pl.pallas_call(
        body, name=name,
        grid_spec=pltpu.PrefetchScalarGridSpec(num_scalar_prefetch=1, grid=(k, rws // tr),
                                               in_specs=[mine, blk], out_specs=[blk, blk]),
        out_shape=[jax.ShapeDtypeStruct(theirs.shape, F32), jax.ShapeDtypeStruct(theirs.shape, BF16)],
        compiler_params=_params(2))(core, slots, theirs)


def _ada_fwd(c_all, w_shard, b_shard):
    def body(c_ref, w_ref, b_ref, o_ref):
        ch, cm, cl = [t.astype(BF16) for t in _split3(c_ref[...])]
        wh, wm, wl = [t.astype(BF16) for t in _split3(w_ref[...])]
        acc = _dot(ch, wh) + _dot(ch, wm) + _dot(cm, wh) + _dot(ch, wl) + _dot(cl, wh) + _dot(cm, wm)
        o_ref[...] = acc + b_ref[...]

    return pl.pallas_call(body, name="ada_fwd", out_shape=jax.ShapeDtypeStruct((N_DEV, ADA_SHARD), F32),
                          compiler_params=_params(0))(c_all, w_shard, b_shard)


def _rope_tables_and_gather(pos_col, inv_freq, tm, weights, c_rows):
    s = pos_col.shape[0]
    nw = len(weights)
    last = s // tm - 1

    def body(*refs):
        p_ref, f_ref = refs[:2]
        w_src, c_src = refs[2:2 + nw], refs[2 + nw:3 + nw]
        cos_ref, sa_ref, sb_ref = refs[3 + nw:6 + nw]
        w_dst, c_dst = refs[6 + nw:6 + 2 * nw], refs[6 + 2 * nw:7 + 2 * nw]
        w_sems, c_sems = refs[7 + 2 * nw:10 + 2 * nw], refs[10 + 2 * nw:]

        @pl.when(pl.program_id(0) == 0)
        def _():
            _gather2_start(w_src, w_dst, *w_sems)
            _exchange_start(_exchange_copies(c_src, c_dst, ["gather"], *c_sems))

        ang = p_ref[...] * f_ref[...]
        sin = jnp.sin(ang)
        first_half = (_lane(CHUNK) & (HEAD_DIM - 1)) < HEAD_DIM // 2
        cos_ref[...] = jnp.cos(ang)
        sa_ref[...] = jnp.where(first_half, -sin, 0.0)
        sb_ref[...] = jnp.where(first_half, 0.0, sin)

        @pl.when(pl.program_id(0) == last)
        def _():
            _exchange_wait(_exchange_copies(c_src, c_dst, ["gather"], *c_sems))
            _gather2_finish(w_src, w_dst, *w_sems)

    tab = jax.ShapeDtypeStruct((s, CHUNK), F32)
    blk = pl.BlockSpec((tm, CHUNK), lambda i: (i, 0))
    gathered = [jax.ShapeDtypeStruct((N_DEV,) + a.shape, a.dtype) for a in list(weights) + [c_rows]]
    outs = pl.pallas_call(
        body, name="rope_tables_gather", grid=(s // tm,), out_shape=[tab, tab, tab] + gathered,
        in_specs=[pl.BlockSpec((tm, 1), lambda i: (i, 0)), pl.BlockSpec((1, CHUNK), lambda i: (0, 0))]
        + [ANY_SPEC] * (nw + 1),
        out_specs=[blk, blk, blk] + [ANY_SPEC] * (nw + 1),
        scratch_shapes=_exchange_sems(nw) + _exchange_sems(1),
        compiler_params=_params(1))(pos_col, inv_freq, *weights, c_rows)
    return outs[:3], outs[3:3 + nw], outs[3 + nw]


def _fwd_proj(x, mod, w_all, cos, sa, sb, bf_row, emat, ind, tm, late_shards):
    s = x.shape[0]
    nl = len(late_shards)
    last = s // tm - 1

    def body(x_ref, mod_ref, w_ref, cos_ref, sa_ref, sb_ref, bf_ref, e_ref, ind_ref, *rest):
        late_src = rest[:nl]
        (h_ref, qa_ref, ka_ref, va_ref, za_ref, qb_ref, kb_ref, vb_ref, zb_ref, ga_ref, gb_ref, f_ref,
         bnd_ref, nrm_ref) = rest[nl:nl + 14]
        late_dst = rest[nl + 14:2 * nl + 14]
        carry_ref, late_sems = rest[2 * nl + 14], rest[2 * nl + 15:]

        @pl.when(pl.program_id(0) == 0)
        def _():
            carry_ref[...] = jnp.zeros_like(carry_ref)
            _gather2_start(late_src, late_dst, *late_sems)

        xv = x_ref[...]
        rstd = lax.rsqrt(jnp.mean(xv * xv, axis=-1, keepdims=True) + NORM_EPS)
        h = (xv * rstd * mod_ref[0:1, :]) * (1.0 + mod_ref[1:2, :]) + mod_ref[2:3, :]
        hb = h.astype(BF16)
        h_ref[...] = hb.T

        def seg(name):
            off, w = SEG[name]
            return _dot(hb, w_ref[:, off:off + w])

        lane = _lane(CHUNK)
        low = lane < HEAD_DIM
        q_ones = _ones_lanes(64, 67)
        k_ones = _ones_lanes(67, 70)
        cos_t, sa_t, sb_t = cos_ref[...], sa_ref[...], sb_ref[...]

        def write_heads(ref, nat, extra_of, rope, scale):
            for p in range(N_HEADS // 2):
                c = nat[:, CHUNK * p:CHUNK * (p + 1)]
                if rope:
                    c = _rope(c, cos_t, sa_t, sb_t)
                if scale != 1.0:
                    c = c * scale
                for hh in range(2):
                    hd = 2 * p + hh
                    src = c if hh == 0 else pltpu.roll(c, HEAD_DIM, 1)
                    ref[:, CHUNK * hd:CHUNK * (hd + 1)] = jnp.where(low, src, extra_of(hd)).astype(BF16)

        write_heads(qa_ref, seg("qa"), lambda hd: q_ones, True, Q_SCALE)
        ka = seg("ka")
        va = seg("va")
        for kv in range(KV_GROUPS):
            sl = slice(CHUNK * kv, CHUNK * (kv + 1))
            ka_ref[:, sl] = jnp.where(low, _rope(ka[:, sl], cos_t, sa_t, sb_t), k_ones).astype(BF16)
            va_ref[:, sl] = jnp.where(low, va[:, sl], q_ones).astype(BF16)
        za_ref[...] = seg("za")
        zb_ref[...] = seg("zb")
        ga_ref[...] = seg("ga")
        gb_ref[...] = seg("gb")

        xf = seg("f") + bf_ref[...]
        f_ref[...] = xf
        logf = jnp.minimum(xf, 0.0) - jnp.log1p(jnp.exp(-jnp.abs(xf)))
        row = lax.broadcasted_iota(jnp.int32, (tm, tm), 0)
        col = lax.broadcasted_iota(jnp.int32, (tm, tm), 1)
        tri = jnp.where(col <= row, 1.0, 0.0).astype(BF16)
        hi, mid, lo = _split3(logf)
        cum = _dot(tri, hi.astype(BF16)) + _dot(tri, mid.astype(BF16)) + _dot(tri, lo.astype(BF16))
        cum = cum + carry_ref[0:1, :]
        carry_ref[0:1, :] += jnp.sum(logf, axis=0, keepdims=True)
        lane_all = _lane(AUG_W) & (CHUNK - 1)
        bias = _spread3(-cum, e_ref) + jnp.where((lane_all >= 67) & (lane_all < 70), 1.0, 0.0)

        qb, kb = seg("qb"), seg("kb")
        write_heads(qb_ref, qb, lambda hd: q_ones, False, Q_SCALE)
        write_heads(kb_ref, kb, lambda hd: bias[:, CHUNK * hd:CHUNK * (hd + 1)], False, 1.0)
        write_heads(vb_ref, seg("vb"), lambda hd: q_ones, False, 1.0)

        rid = lax.broadcasted_iota(jnp.int32, (tm, CHUNK), 0)
        bnd_ref[...] = jnp.zeros_like(bnd_ref)
        bnd_ref[0, 0:1, :] = jnp.sum(jnp.where(rid == 0, cum, 0.0), axis=0, keepdims=True)
        bnd_ref[0, 1:2, :] = jnp.sum(jnp.where(rid == tm - 1, cum, 0.0), axis=0, keepdims=True)

        @pl.when(pl.program_id(0) == 0)
        def _():
            nrm_ref[...] = jnp.zeros_like(nrm_ref)

        for r_, nat in ((0, qb * Q_SCALE), (1, kb)):
            sq = _dot((nat * nat).astype(BF16), ind_ref[...])
            nrm_ref[r_:r_ + 1, :] = jnp.maximum(nrm_ref[r_:r_ + 1, :], jnp.max(sq, axis=0, keepdims=True))

        @pl.when(pl.program_id(0) == last)
        def _():
            _gather2_finish(late_src, late_dst, *late_sems)

    row_blk = lambda w: pl.BlockSpec((tm, w), lambda i: (i, 0))
    full = lambda a: pl.BlockSpec(a.shape, lambda i: (0,) * a.ndim)
    sds = lambda w, dt: jax.ShapeDtypeStruct((s, w), dt)
    out_shape = [sds(AUG_W, BF16), sds(KV_GROUPS * CHUNK, BF16), sds(KV_GROUPS * CHUNK, BF16),
                 sds(512, F32), sds(AUG_W, BF16), sds(AUG_W, BF16), sds(AUG_W, BF16), sds(512, F32),
                 sds(D_MODEL, F32), sds(D_MODEL, F32), sds(CHUNK, F32)]
    small = [jax.ShapeDtypeStruct((s // tm, 8, CHUNK), F32), jax.ShapeDtypeStruct((8, CHUNK), F32)]
    gathered = [jax.ShapeDtypeStruct((N_DEV,) + a.shape, a.dtype) for a in late_shards]
    outs = pl.pallas_call(
        body, name="fwd_proj", grid=(s // tm,),
        out_shape=[jax.ShapeDtypeStruct((D_MODEL, s), BF16)] + out_shape + small + gathered,
        in_specs=[row_blk(D_MODEL), full(mod), _resident(w_all), row_blk(CHUNK), row_blk(CHUNK), row_blk(CHUNK),
                  full(bf_row), full(emat), full(ind)] + [ANY_SPEC] * nl,
        out_specs=[pl.BlockSpec((D_MODEL, tm), lambda i: (0, i))] + [row_blk(o.shape[1]) for o in out_shape] + [
            pl.BlockSpec((1, 8, CHUNK), lambda i: (i, 0, 0)), pl.BlockSpec((8, CHUNK), lambda i: (0, 0))]
        + [ANY_SPEC] * nl,
        scratch_shapes=[pltpu.VMEM((8, CHUNK), F32)] + _exchange_sems(nl),
        compiler_params=_params(1))(x, mod, w_all, cos, sa, sb, bf_row, emat, ind, *late_shards)
    return outs[:14], outs[14:]


def _swa_fwd(q_aug, k_aug, v_aug, sink_rows, tq):
    s = q_aug.shape[0]
    r = tq // WINDOW
    gw = GROUP * CHUNK

    def body(q_ref, kc_ref, kp_ref, vc_ref, vp_ref, sink_ref, o_ref, qb_ref):
        i = pl.program_id(1)
        st = GROUP * WINDOW
        qloc = lax.broadcasted_iota(jnp.int32, (st, 2 * WINDOW), 0) & (WINDOW - 1)
        col = lax.broadcasted_iota(jnp.int32, (st, 2 * WINDOW), 1)
        band = (col > qloc) & (col <= qloc + WINDOW)
        head = jnp.right_shift(lax.broadcasted_iota(jnp.int32, (st, 1), 0), 7)
        sink = jnp.zeros((st, 1), F32)
        for g in range(GROUP):
            sink = jnp.where(head == g, jnp.max(sink_ref[0, g:g + 1, :], axis=1, keepdims=True), sink)
        lane = _lane(CHUNK)
        for sub in range(r):
            rows = slice(WINDOW * sub, WINDOW * (sub + 1))
            q = jnp.concatenate([q_ref[rows, CHUNK * g:CHUNK * (g + 1)] for g in range(GROUP)], axis=0)
            if sub == 0:
                k = jnp.concatenate([kp_ref[...], kc_ref[rows, :]], axis=0)
                v = jnp.concatenate([vp_ref[...], vc_ref[rows, :]], axis=0)
                valid = band & ((col >= WINDOW) | (i > 0))
            else:
                both = slice(WINDOW * (sub - 1), WINDOW * (sub + 1))
                k, v, valid = kc_ref[both, :], vc_ref[both, :], band
            sc = jnp.where(valid, _dot_nt(q, k), NEG)
            m = jnp.maximum(jnp.max(sc, axis=1, keepdims=True), sink)
            acc = _dot(jnp.exp((sc - m).astype(BF16)), v)
            denom = _lane_sum(acc, 64, 65) + jnp.exp(sink - m)
            out = acc * (1.0 / denom)
            aug = _place3(lane, 67, -(m + jnp.log(denom)), q.astype(F32)).astype(BF16)
            hrows = lambda a, g: a[WINDOW * g:WINDOW * (g + 1), :]
            for g in range(GROUP):
                qb_ref[rows, CHUNK * g:CHUNK * (g + 1)] = hrows(aug, g)
            for pp in range(GROUP // 2):
                o_ref[rows, CHUNK * pp:CHUNK * (pp + 1)] = _pair(hrows(out, 2 * pp), hrows(out, 2 * pp + 1))

    return pl.pallas_call(
        body, name="swa_fwd", grid=(KV_GROUPS, s // tq),
        out_shape=[jax.ShapeDtypeStruct((s, 512), F32), jax.ShapeDtypeStruct((s, AUG_W), BF16)],
        in_specs=[pl.BlockSpec((tq, gw), lambda kv, i: (i, kv)),
                  pl.BlockSpec((tq, CHUNK), lambda kv, i: (i, kv)),
                  pl.BlockSpec((WINDOW, CHUNK), lambda kv, i: (jnp.maximum(i * r - 1, 0), kv)),
                  pl.BlockSpec((tq, CHUNK), lambda kv, i: (i, kv)),
                  pl.BlockSpec((WINDOW, CHUNK), lambda kv, i: (jnp.maximum(i * r - 1, 0), kv)),
                  pl.BlockSpec((1, 8, CHUNK), lambda kv, i: (kv, 0, 0))],
        out_specs=[pl.BlockSpec((tq, GROUP * HEAD_DIM), lambda kv, i: (i, kv)),
                   pl.BlockSpec((tq, gw), lambda kv, i: (i, kv))],
        compiler_params=_params(2))(q_aug, k_aug, k_aug, v_aug, v_aug, sink_rows)


def _swa_bwd(k_aug, v_aug, q_bwd, do_aug, cos, sa, sb, sink_rows, tk):
    s = k_aug.shape[0]
    r = tk // WINDOW
    nt = s // tk
    nb = s // WINDOW

    def body(k_ref, v_ref, q_ref, qn_ref, do_ref, don_ref, cos_ref, sa_ref, sb_ref, sink_ref,
             da_ref, dsink_ref, carry_ref, acc_ref):
        j = pl.program_id(0)

        @pl.when(j == 0)
        def _():
            carry_ref[...] = jnp.zeros_like(carry_ref)
            dsink_ref[...] = jnp.zeros_like(dsink_ref)

        acc_ref[...] = jnp.zeros_like(acc_ref)
        acc_ref[0:WINDOW, :] = carry_ref[...]
        dk_col, dv_col = SEG["ka"][0] - SEG["qa"][0], SEG["va"][0] - SEG["qa"][0]
        st = GROUP * WINDOW
        key = lax.broadcasted_iota(jnp.int32, (WINDOW, 2 * st), 0)
        col = lax.broadcasted_iota(jnp.int32, (WINDOW, 2 * st), 1)
        qloc = col & (WINDOW - 1)
        band = ((col < st) & (key <= qloc)) | ((col >= st) & (key > qloc))
        lane = _lane(CHUNK)
        low = lane < HEAD_DIM
        cos_t, sa_t, sb_t = cos_ref[...], sa_ref[...], sb_ref[...]
        for kv, sub in [(kv, sub) for kv in range(KV_GROUPS) for sub in range(r)]:
            heads = [slice(CHUNK * (GROUP * kv + g), CHUNK * (GROUP * kv + g + 1)) for g in range(GROUP)]
            kvs = slice(CHUNK * kv, CHUNK * (kv + 1))
            rows = slice(WINDOW * sub, WINDOW * (sub + 1))
            nxt = slice(WINDOW * (sub + 1), WINDOW * (sub + 2))
            k, v = k_ref[rows, kvs], v_ref[rows, kvs]
            q_cur, do_cur = [q_ref[rows, cs] for cs in heads], [do_ref[rows, cs] for cs in heads]
            if sub < r - 1:
                q_nxt, do_nxt, valid = [q_ref[nxt, cs] for cs in heads], [do_ref[nxt, cs] for cs in heads], band
            else:
                q_nxt, do_nxt = [qn_ref[:, cs] for cs in heads], [don_ref[:, cs] for cs in heads]
                valid = band & ((col < st) | (j < nt - 1))
            q = jnp.concatenate(q_cur + q_nxt, axis=0)
            do = jnp.concatenate(do_cur + do_nxt, axis=0)
            pt = jnp.exp(jnp.where(valid, _dot_nt(k, q), NEG))
            ds = (pt * _dot_nt(v, do)).astype(BF16)
            dv = _dot(pt.astype(BF16), do)
            dk = _dot(ds, q)
            dq = _dot_tn(ds, k)
            for g, cs in enumerate(heads):
                acc_ref[rows, cs] += dq[WINDOW * g:WINDOW * (g + 1), :]
                dqn = dq[st + WINDOW * g:st + WINDOW * (g + 1), :]
                if sub < r - 1:
                    acc_ref[nxt, cs] += dqn
                else:
                    carry_ref[:, cs] = dqn
                sink = jnp.max(sink_ref[kv, g:g + 1, :], axis=1, keepdims=True)
                p_sink = jnp.exp(sink + _lane_sum(q_cur[g].astype(F32), 67, 70))
                term = jnp.sum(p_sink * _lane_sum(do_cur[g].astype(F32), 64, 67), axis=0, keepdims=True)
                dsink_ref[kv, g:g + 1, :] += jnp.broadcast_to(term, (1, CHUNK))
            da_ref[rows, dk_col + CHUNK * kv:dk_col + CHUNK * (kv + 1)] = _rope_inv(
                jnp.where(low, dk, 0.0), cos_t[rows, :], sa_t[rows, :], sb_t[rows, :]).astype(BF16)
            da_ref[rows, dv_col + CHUNK * kv:dv_col + CHUNK * (kv + 1)] = jnp.where(low, dv, 0.0).astype(BF16)
        for pp in range(N_HEADS // 2):
            d = _pair(acc_ref[:, CHUNK * 2 * pp:CHUNK * (2 * pp + 1)], acc_ref[:, CHUNK * (2 * pp + 1):CHUNK * (2 * pp + 2)])
            da_ref[:, CHUNK * pp:CHUNK * (pp + 1)] = (_rope_inv(d, cos_t, sa_t, sb_t) * Q_SCALE).astype(BF16)

    cur = lambda w: pl.BlockSpec((tk, w), lambda j: (j, 0))
    nxt = pl.BlockSpec((WINDOW, AUG_W), lambda j: (jnp.minimum((j + 1) * r, nb - 1), 0))
    whole = pl.BlockSpec((KV_GROUPS, 8, CHUNK), lambda j: (0, 0, 0))
    return pl.pallas_call(
        body, name="swa_bwd", grid=(nt,),
        out_shape=[jax.ShapeDtypeStruct((s, 1024), BF16), jax.ShapeDtypeStruct((KV_GROUPS, 8, CHUNK), F32)],
        in_specs=[cur(KV_GROUPS * CHUNK), cur(KV_GROUPS * CHUNK), cur(AUG_W), nxt, cur(AUG_W), nxt,
                  cur(CHUNK), cur(CHUNK), cur(CHUNK), whole],
        out_specs=[cur(1024), whole],
        scratch_shapes=[pltpu.VMEM((WINDOW, AUG_W), F32), pltpu.VMEM((tk, AUG_W), F32)],
        compiler_params=_params(1))(k_aug, v_aug, q_bwd, q_bwd, do_aug, do_aug, cos, sa, sb, sink_rows)


def _causal_bias(t, keys_on_rows):
    idx = np.arange(t)
    ok = (idx[:, None] <= idx[None, :]) if keys_on_rows else (idx[None, :] <= idx[:, None])
    return jnp.asarray(np.where(ok, 0.0, NEG), F32)


def _fox_fwd(first_key, q_aug, k_aug, v_aug, t):
    s = q_aug.shape[0]
    pw = 2 * CHUNK

    def body(lo_ref, q_ref, k_ref, v_ref, mask_ref, o_ref, qb_ref):
        i = pl.program_id(1)
        first = lo_ref[pl.program_id(0), i]
        lane = _lane(CHUNK)
        heads = (slice(0, CHUNK), slice(CHUNK, pw))
        qs = [q_ref[:, hs] for hs in heads]

        def step(jb, carry, masked):
            rows = pl.ds(pl.multiple_of(jb * t, t), t)
            scs = [_dot_nt(qs[hh], k_ref[rows, hs]) for hh, hs in enumerate(heads)]
            if masked:
                scs = [sc + mask_ref[...] for sc in scs]
            m_news = [jnp.maximum(carry[2 * hh], jnp.max(scs[hh], axis=1, keepdims=True)) for hh in range(2)]
            ps = [jnp.exp((scs[hh] - m_news[hh]).astype(BF16)) for hh in range(2)]
            new = []
            for hh, hs in enumerate(heads):
                m, acc = carry[2 * hh], carry[2 * hh + 1]
                new += [m_news[hh], jnp.exp(m - m_news[hh]) * acc + _dot(ps[hh], v_ref[rows, hs])]
            return tuple(new)

        init = (jnp.full((t, 1), NEG, F32), jnp.zeros((t, CHUNK), F32)) * 2
        carry = step(i, lax.fori_loop(first, i, lambda jb, c: step(jb, c, False), init), True)
        outs = []
        for hh, hs in enumerate(heads):
            m, acc = carry[2 * hh], carry[2 * hh + 1]
            denom = _lane_sum(acc, 64, 65)
            outs.append(acc * (1.0 / denom))
            lse = m + jnp.log(denom)
            qb_ref[:, hs] = _place3(lane, 67, -lse, qs[hh].astype(F32)).astype(BF16)
        o_ref[...] = _pair(outs[0], outs[1])

    return pl.pallas_call(
        body, name="fox_fwd", grid=(N_HEADS // 2, s // t),
        out_shape=[jax.ShapeDtypeStruct((s, 512), F32), jax.ShapeDtypeStruct((s, AUG_W), BF16)],
        in_specs=[pl.BlockSpec(memory_space=pltpu.SMEM),
                  pl.BlockSpec((t, pw), lambda p, i: (i, p)),
                  pl.BlockSpec((s, pw), lambda p, i: (0, p)),
                  pl.BlockSpec((s, pw), lambda p, i: (0, p)),
                  pl.BlockSpec((t, t), lambda p, i: (0, 0))],
        out_specs=[pl.BlockSpec((t, CHUNK), lambda p, i: (i, p)), pl.BlockSpec((t, pw), lambda p, i: (i, p))],
        compiler_params=_params(2))(first_key, q_aug, k_aug, v_aug, _causal_bias(t, False))


def _fox_block_ranges(bounds, norms, r):
    cum_first = bounds[0::r, 0, :N_HEADS]
    cum_last = bounds[r - 1::r, 1, :N_HEADS]
    n = cum_first.shape[0]
    reach = 2.0 * 1.02 * jnp.sqrt(norms[0, :N_HEADS] * norms[1, :N_HEADS]) + UNDERFLOW
    blk = jnp.arange(n, dtype=jnp.int32)
    decay = cum_last[None, :, :] - cum_first[:, None, :]
    skip = (decay > reach[None, None, :]) & (blk[None, :, None] < blk[:, None, None])
    first = jnp.min(jnp.where(skip, n, blk[None, :, None]), axis=1)
    first = jnp.minimum(first[:, 0::2], first[:, 1::2]).T
    needed = (first[:, :, None] <= blk[None, None, :]) & (blk[None, :, None] >= blk[None, None, :])
    last = jnp.max(jnp.where(needed, blk[None, :, None], blk[None, None, :]), axis=1)
    return first.astype(jnp.int32), last.astype(jnp.int32)


def _fox_bwd(last_query, k_aug, v_aug, q_bwd, do_aug, t):
    s = k_aug.shape[0]
    n = s // t
    pw = 2 * CHUNK

    def body(hi_ref, k_ref, v_ref, q_ref, do_ref, mask_ref, dk_ref, dv_ref, dck_ref, dq_ref, dcq_ref, dq_scr):
        j = pl.program_id(1)
        last = hi_ref[pl.program_id(0), j]

        @pl.when(j == 0)
        def _():
            dq_scr[...] = jnp.zeros_like(dq_scr)

        lane = _lane(CHUNK)
        heads = (slice(0, CHUNK), slice(CHUNK, pw))
        ks = [k_ref[:, hs] for hs in heads]
        vs = [v_ref[:, hs] for hs in heads]

        def step(ib, carry, masked):
            rows = pl.ds(pl.multiple_of(ib * t, t), t)
            new = []
            for hh, hs in enumerate(heads):
                dv, dk = carry[2 * hh], carry[2 * hh + 1]
                q, do = q_ref[rows, hs], do_ref[rows, hs]
                st = _dot_nt(ks[hh], q)
                if masked:
                    st = st + mask_ref[...]
                pt = jnp.exp(st)
                ds = (pt * _dot_nt(vs[hh], do)).astype(BF16)
                new += [dv + _dot(pt.astype(BF16), do), dk + _dot(ds, q)]
                dq_scr[rows, hs] += _dot_tn(ds, ks[hh])
            return tuple(new)

        zero = jnp.zeros((t, CHUNK), F32)
        carry = lax.fori_loop(j + 1, last + 1, lambda ib, c: step(ib, c, False), step(j, (zero,) * 4, True))
        dvs, dks = (carry[0], carry[2]), (carry[1], carry[3])
        dk_ref[...] = _pair(dks[0], dks[1]).astype(BF16)
        dv_ref[...] = _pair(dvs[0], dvs[1]).astype(BF16)
        dck_ref[0] = jnp.where(lane == 0, pltpu.roll(dks[0], 64, 1),
                               jnp.where(lane == 1, pltpu.roll(dks[1], 65, 1), 0.0))

        @pl.when(j == n - 1)
        def _():
            for ib in range(n):
                rows = slice(t * ib, t * (ib + 1))
                d0, d1 = dq_scr[rows, 0:CHUNK], dq_scr[rows, CHUNK:pw]
                dq_ref[rows, :] = (_pair(d0, d1) * Q_SCALE).astype(BF16)
                dcq_ref[0, rows, :] = jnp.where(lane == 0, pltpu.roll(d0, CHUNK - 67, 1),
                                                jnp.where(lane == 1, pltpu.roll(d1, CHUNK - 66, 1), 0.0))

    return pl.pallas_call(
        body, name="fox_bwd", grid=(N_HEADS // 2, n),
        out_shape=[jax.ShapeDtypeStruct((s, 512), BF16), jax.ShapeDtypeStruct((s, 512), BF16),
                   jax.ShapeDtypeStruct((N_HEADS // 2, s, CHUNK), F32), jax.ShapeDtypeStruct((s, 512), BF16),
                   jax.ShapeDtypeStruct((N_HEADS // 2, s, CHUNK), F32)],
        in_specs=[pl.BlockSpec(memory_space=pltpu.SMEM),
                  pl.BlockSpec((t, pw), lambda p, j: (j, p)), pl.BlockSpec((t, pw), lambda p, j: (j, p)),
                  pl.BlockSpec((s, pw), lambda p, j: (0, p)), pl.BlockSpec((s, pw), lambda p, j: (0, p)),
                  pl.BlockSpec((t, t), lambda p, j: (0, 0))],
        out_specs=[pl.BlockSpec((t, CHUNK), lambda p, j: (j, p)), pl.BlockSpec((t, CHUNK), lambda p, j: (j, p)),
                   pl.BlockSpec((1, t, CHUNK), lambda p, j: (p, j, 0)), pl.BlockSpec((s, CHUNK), lambda p, j: (0, p)),
                   pl.BlockSpec((1, s, CHUNK), lambda p, j: (p, 0, 0))],
        scratch_shapes=[pltpu.VMEM((s, pw), F32)],
        compiler_params=_params(2))(last_query, k_aug, v_aug, q_bwd, do_aug, _causal_bias(t, True))


def _fgate_bwd(dcq, dck, xf, tm):
    s = xf.shape[0]
    nt = s // tm

    def body(dq_ref, dc_ref, xf_ref, df_ref, dbf_ref, carry_ref):
        @pl.when(pl.program_id(0) == 0)
        def _():
            carry_ref[...] = jnp.zeros_like(carry_ref)
            dbf_ref[...] = jnp.zeros_like(dbf_ref)

        row = lax.broadcasted_iota(jnp.int32, (tm, tm), 0)
        col = lax.broadcasted_iota(jnp.int32, (tm, tm), 1)
        tri = jnp.where(col >= row, 1.0, 0.0).astype(BF16)
        dcum = dq_ref[0] - dc_ref[0]
        for p in range(1, N_HEADS // 2):
            dcum = dcum + pltpu.roll(dq_ref[p] - dc_ref[p], 2 * p, 1)
        hi, mid, lo = _split3(dcum)
        dlogf = _dot(tri, hi.astype(BF16)) + _dot(tri, mid.astype(BF16)) + _dot(tri, lo.astype(BF16))
        dlogf = dlogf + carry_ref[0:1, :]
        carry_ref[0:1, :] += jnp.sum(dcum, axis=0, keepdims=True)
        df = dlogf / (1.0 + jnp.exp(xf_ref[...]))
        df_ref[...] = df.astype(BF16)
        dbf_ref[0:1, :] += jnp.sum(df, axis=0, keepdims=True)

    rev = pl.BlockSpec((tm, CHUNK), lambda i: (nt - 1 - i, 0))
    rev4 = pl.BlockSpec((N_HEADS // 2, tm, CHUNK), lambda i: (0, nt - 1 - i, 0))
    return pl.pallas_call(
        body, name="fgate_bwd", grid=(nt,),
        out_shape=[jax.ShapeDtypeStruct((s, CHUNK), BF16), jax.ShapeDtypeStruct((8, CHUNK), F32)],
        in_specs=[rev4, rev4, rev], out_specs=[rev, pl.BlockSpec((8, CHUNK), lambda i: (0, 0))],
        scratch_shapes=[pltpu.VMEM((8, CHUNK), F32)],
        compiler_params=_params(1))(dcq, dck, xf)


def _post(x, tgt, att_a, att_b, za, zb, ga, gb, woa_t, wob_t, wout, vec, ind, emat, tm):
    s = x.shape[0]

    def body(x_ref, t_ref, aa_ref, ab_ref, za_ref, zb_ref, ga_ref, gb_ref, woat_ref, wobt_ref, wout_ref,
             vec_ref, ind_ref, e_ref,
             dx1_ref, ua_ref, ub_ref, mg_ref, do_ref, dya_ref, dyb_ref, doa_ref, dob_ref, dzg_ref, acc_ref):
        @pl.when(pl.program_id(0) == 0)
        def _():
            acc_ref[...] = jnp.zeros_like(acc_ref)

        gate, gfin = vec_ref[0:1, :], vec_ref[1:2, :]
        inv_d = 1.0 / D_MODEL

        def branch_fwd(att_ref, z_ref, w_ref, u_ref):
            att, z = att_ref[...], z_ref[...]
            sz = _sigmoid(z)
            silu = z * sz
            u = (att * silu).astype(BF16)
            u_ref[...] = u.T
            return att, z, sz, silu, _dot_nt(u, w_ref[...])

        att_a, z_a, sz_a, silu_a, y_a = branch_fwd(aa_ref, za_ref, woat_ref, ua_ref)
        att_b, z_b, sz_b, silu_b, y_b = branch_fwd(ab_ref, zb_ref, wobt_ref, ub_ref)
        sg_a, sg_b = _sigmoid(ga_ref[...]), _sigmoid(gb_ref[...])
        merged = (sg_a * y_a + sg_b * y_b).astype(BF16)
        mg_ref[...] = merged.T
        o = _dot(merged, wout_ref[...])
        x1 = x_ref[...] + gate * o
        rstd = lax.rsqrt(jnp.mean(x1 * x1, axis=-1, keepdims=True) + NORM_EPS)
        xh = x1 * rstd
        diff = xh * gfin - t_ref[...]
        acc_ref[2:3, :] += (0.5 * inv_d) * jnp.sum(diff * diff, axis=0, keepdims=True)
        dy = diff * inv_d
        acc_ref[1:2, :] += jnp.sum(dy * xh, axis=0, keepdims=True)
        dyg = dy * gfin
        dx1 = rstd * (dyg - xh * jnp.mean(dyg * xh, axis=-1, keepdims=True))
        dx1_ref[...] = dx1
        acc_ref[0:1, :] += jnp.sum(dx1 * o, axis=0, keepdims=True)
        d_o = (dx1 * gate).astype(BF16)
        do_ref[...] = d_o
        dmg = _dot_nt(d_o, wout_ref[...])

        lane = _lane(CHUNK)
        low = lane < HEAD_DIM
        zg0 = SEG["za"][0]

        def branch_bwd(sg, y, att, z, sz, silu, wt_ref, dy_ref, g_name, z_name, doaug_ref):
            dyb = (dmg * sg).astype(BF16)
            dy_ref[...] = dyb
            g_off, z_off = SEG[g_name][0] - zg0, SEG[z_name][0] - zg0
            dzg_ref[:, g_off:g_off + D_MODEL] = (dmg * y * sg * (1.0 - sg)).astype(BF16)
            du = _dot(dyb, wt_ref[...])
            datt = du * silu
            dzg_ref[:, z_off:z_off + 512] = (du * att * (sz * (1.0 + z * (1.0 - sz)))).astype(BF16)
            extra = _spread3(-_dot_split(datt * att, ind_ref[...]), e_ref)
            for p in range(N_HEADS // 2):
                c = datt[:, CHUNK * p:CHUNK * (p + 1)]
                for hh in range(2):
                    hd = 2 * p + hh
                    src = c if hh == 0 else pltpu.roll(c, HEAD_DIM, 1)
                    doaug_ref[:, CHUNK * hd:CHUNK * (hd + 1)] = jnp.where(
                        low, src, extra[:, CHUNK * hd:CHUNK * (hd + 1)]).astype(BF16)

        branch_bwd(sg_a, y_a, att_a, z_a, sz_a, silu_a, woat_ref, dya_ref, "ga", "za", doa_ref)
        branch_bwd(sg_b, y_b, att_b, z_b, sz_b, silu_b, wobt_ref, dyb_ref, "gb", "zb", dob_ref)

    row_blk = lambda w: pl.BlockSpec((tm, w), lambda i: (i, 0))
    full = lambda a: pl.BlockSpec(a.shape, lambda i: (0,) * a.ndim)
    sds = lambda w, dt: jax.ShapeDtypeStruct((s, w), dt)
    tds = lambda w: jax.ShapeDtypeStruct((w, s), BF16)
    col_blk = lambda w: pl.BlockSpec((w, tm), lambda i: (0, i))
    out_shape = [sds(D_MODEL, F32), tds(512), tds(512), tds(D_MODEL), sds(D_MODEL, BF16),
                 sds(D_MODEL, BF16), sds(D_MODEL, BF16), sds(AUG_W, BF16), sds(AUG_W, BF16), sds(3072, BF16),
                 jax.ShapeDtypeStruct((8, D_MODEL), F32)]
    ins = [x, tgt, att_a, att_b, za, zb, ga, gb, woa_t, wob_t, wout, vec, ind, emat]
    in_specs = [row_blk(a.shape[1]) for a in ins[:8]] + [full(a) for a in ins[8:]]
    out_specs = ([row_blk(D_MODEL), col_blk(512), col_blk(512), col_blk(D_MODEL)]
                 + [row_blk(o.shape[1]) for o in out_shape[4:-1]] + [pl.BlockSpec((8, D_MODEL), lambda i: (0, 0))])
    return pl.pallas_call(body, name="post", grid=(s // tm,), out_shape=out_shape, in_specs=in_specs,
                          out_specs=out_specs, compiler_params=_params(1))(*ins)


def _bwd_pre(dproj, w_all, x, dx1, mod, tm, chip_halves, modes):
    s = x.shape[0]
    ng, nh = len(DPROJ_GROUPS), len(chip_halves)
    last = s // tm - 1

    def body(*refs):
        dp_refs = refs[:ng]
        wt_ref, x_ref, dx1_ref, mod_ref = refs[ng:ng + 4]
        send_refs = refs[ng + 4:ng + 4 + nh]
        gx_ref, acc_ref = refs[ng + 4 + nh:ng + 6 + nh]
        recv_refs = refs[ng + 6 + nh:ng + 6 + 2 * nh]
        sems = refs[ng + 6 + 2 * nh:]

        @pl.when(pl.program_id(0) == 0)
        def _():
            acc_ref[...] = jnp.zeros_like(acc_ref)
            _exchange_start(_exchange_copies(send_refs, recv_refs, modes, *sems))

        dh = None
        for dp_ref, (_, off, w) in zip(dp_refs, DPROJ_GROUPS):
            part = _dot_nt(dp_ref[...], wt_ref[:, off:off + w])
            dh = part if dh is None else dh + part
        xv = x_ref[...]
        rstd = lax.rsqrt(jnp.mean(xv * xv, axis=-1, keepdims=True) + NORM_EPS)
        xh = xv * rstd
        gn = mod_ref[0:1, :]
        acc_ref[0:1, :] += jnp.sum(dh, axis=0, keepdims=True)
        acc_ref[1:2, :] += jnp.sum(dh * (xh * gn), axis=0, keepdims=True)
        dn = dh * (1.0 + mod_ref[1:2, :])
        acc_ref[2:3, :] += jnp.sum(dn * xh, axis=0, keepdims=True)
        dxh = dn * gn
        gx_ref[...] = dx1_ref[...] + rstd * (dxh - xh * jnp.mean(dxh * xh, axis=-1, keepdims=True))

        @pl.when(pl.program_id(0) == last)
        def _():
            _exchange_wait(_exchange_copies(send_refs, recv_refs, modes, *sems))

    row_blk = lambda w: pl.BlockSpec((tm, w), lambda i: (i, 0))
    full = lambda a: pl.BlockSpec(a.shape, lambda i: (0,) * a.ndim)
    outs = pl.pallas_call(
        body, name="bwd_pre", grid=(s // tm,),
        out_shape=[jax.ShapeDtypeStruct((s, D_MODEL), F32), jax.ShapeDtypeStruct((8, D_MODEL), F32)]
        + _exchange_out_shapes(chip_halves, modes),
        in_specs=[row_blk(w) for _, _, w in DPROJ_GROUPS] + [_resident(w_all), row_blk(D_MODEL), row_blk(D_MODEL), full(mod)]
        + [ANY_SPEC] * nh,
        out_specs=[row_blk(D_MODEL), pl.BlockSpec((8, D_MODEL), lambda i: (0, 0))] + [ANY_SPEC] * nh,
        scratch_shapes=_exchange_sems(nh),
        compiler_params=_params(1))(*dproj, w_all, x, dx1, mod, *chip_halves)
    return outs[0], outs[1], outs[2:]


def _accumulate_tokens(a, b_ref, acc_ref, o_ref, last):
    @pl.when(pl.program_id(0) == 0)
    def _():
        acc_ref[...] = jnp.zeros_like(acc_ref)

    acc_ref[...] += _dot(a, b_ref[...])

    @pl.when(pl.program_id(0) == last)
    def _():
        o_ref[...] = acc_ref[...].astype(BF16)


def _token_matmul_multi(a_t, bs, tk, name):
    m, s = a_t.shape
    nb = len(bs)
    last = s // tk - 1

    def body(*refs):
        a_ref, b_refs, o_refs, acc_refs = refs[0], refs[1:1 + nb], refs[1 + nb:1 + 2 * nb], refs[1 + 2 * nb:]
        a = a_ref[...]
        for b_ref, o_ref, acc_ref in zip(b_refs, o_refs, acc_refs):
            _accumulate_tokens(a, b_ref, acc_ref, o_ref, last)

    return pl.pallas_call(
        body, name=name, grid=(s // tk,),
        out_shape=[jax.ShapeDtypeStruct((m, b.shape[1]), BF16) for b in bs],
        in_specs=[pl.BlockSpec((m, tk), lambda k: (0, k))] + [pl.BlockSpec((tk, b.shape[1]), lambda k: (k, 0)) for b in bs],
        out_specs=[pl.BlockSpec((m, b.shape[1]), lambda k: (0, 0)) for b in bs],
        scratch_shapes=[pltpu.VMEM((m, b.shape[1]), F32) for b in bs],
        compiler_params=_params(1))(a_t, *bs)


def _adam_math(g, w, m, v):
    m2 = ADAM_B1 * m + (1.0 - ADAM_B1) * g
    v2 = ADAM_B2 * v + (1.0 - ADAM_B2) * (g * g)
    delta = -ADAM_LR * ((m2 / ADAM_C1) / (jnp.sqrt(v2 / ADAM_C2) + ADAM_EPS) + ADAM_WD * w)
    return delta, m2, v2


def _adamw_parts(full, parts, others, w, m, v, tr, name):
    _, rws, cols = w.shape

    def body(oth_ref, f_ref, p_ref, w_ref, m_ref, v_ref, g_ref, d_ref, m2_ref, v2_ref):
        g = None
        for chip in range(N_CHIPS):
            part = (1.0 - oth_ref[chip]) * f_ref[chip] + oth_ref[chip] * p_ref[chip].astype(F32)
            g = part if g is None else g + part
        g_ref[0] = g
        d_ref[0], m2_ref[0], v2_ref[0] = _adam_math(g, w_ref[0], m_ref[0], v_ref[0])

    blk = pl.BlockSpec((1, tr, cols), lambda i: (0, i, 0))
    slots = pl.BlockSpec((N_CHIPS, tr, cols), lambda i: (0, i, 0))
    o = jax.ShapeDtypeStruct((1, rws, cols), F32)
    return pl.pallas_call(
        body, name=name, grid=(rws // tr,), out_shape=[o, o, o, o],
        in_specs=[pl.BlockSpec(memory_space=pltpu.SMEM), slots, slots, blk, blk, blk],
        out_specs=[blk, blk, blk, blk], compiler_params=_params(1))(others, full, parts, w, m, v)


def _adamw_devices(parts, w, m, v, tr, name):
    _, rws, cols = w.shape

    def body(p_ref, w_ref, m_ref, v_ref, g_ref, d_ref, m2_ref, v2_ref):
        g = p_ref[0].astype(F32)
        for dev in range(1, N_DEV):
            g = g + p_ref[dev].astype(F32)
        g_ref[0] = g
        d_ref[0], m2_ref[0], v2_ref[0] = _adam_math(g, w_ref[0], m_ref[0], v_ref[0])

    blk = pl.BlockSpec((1, tr, cols), lambda i: (0, i, 0))
    o = jax.ShapeDtypeStruct((1, rws, cols), F32)
    return pl.pallas_call(
        body, name=name, grid=(rws // tr,), out_shape=[o, o, o, o],
        in_specs=[pl.BlockSpec((N_DEV, tr, cols), lambda i: (0, i, 0)), blk, blk, blk],
        out_specs=[blk, blk, blk, blk], compiler_params=_params(1))(parts, w, m, v)


def _adamw_ada(c_lanes, d_rows, w, m, v, tr):
    _, rws, cols = w.shape

    def body(c_ref, d_ref, w_ref, m_ref, v_ref, g_ref, dl_ref, m2_ref, v2_ref):
        for k in range(cols // CHUNK):
            cs = slice(CHUNK * k, CHUNK * (k + 1))
            g = c_ref[0] * d_ref[0:1, cs]
            for b in range(1, N_DEV):
                g = g + c_ref[b] * d_ref[b:b + 1, cs]
            g_ref[0, :, cs] = g
            dl_ref[0, :, cs], m2_ref[0, :, cs], v2_ref[0, :, cs] = _adam_math(
                g, w_ref[0, :, cs], m_ref[0, :, cs], v_ref[0, :, cs])

    blk = pl.BlockSpec((1, tr, cols), lambda i: (0, i, 0))
    o = jax.ShapeDtypeStruct((1, rws, cols), F32)
    return pl.pallas_call(
        body, name="adamw_ada", grid=(rws // tr,), out_shape=[o, o, o, o],
        in_specs=[pl.BlockSpec((N_DEV, tr, CHUNK), lambda i: (0, i, 0)), pl.BlockSpec((N_DEV, cols), lambda i: (0, 0)),
                  blk, blk, blk],
        out_specs=[blk, blk, blk, blk], compiler_params=_params(1))(c_lanes, d_rows, w, m, v)


SMALL = (("b_ada", 0, 0, 3 * D_MODEL), ("g_norm", 1, 0, D_MODEL), ("b_f", 3, 0, N_HEADS), ("sinks", 4, 0, N_HEADS),
         ("g_final", 1, D_MODEL, D_MODEL))
LOSS_ROW = 2


def _pack_small(pre_acc, post_acc, dbf_acc, dsink):
    def body(pre_ref, post_ref, dbf_ref, dsink_ref, o_ref):
        o_ref[...] = jnp.zeros_like(o_ref)
        d = D_MODEL
        o_ref[0:1, 0:d], o_ref[0:1, d:2 * d], o_ref[0:1, 2 * d:3 * d] = pre_ref[0:1, :], pre_ref[1:2, :], post_ref[0:1, :]
        o_ref[1:2, 0:d], o_ref[1:2, d:2 * d] = pre_ref[2:3, :], post_ref[1:2, :]
        o_ref[LOSS_ROW:LOSS_ROW + 1, 0:d] = post_ref[2:3, :]
        o_ref[3:4, 0:CHUNK] = dbf_ref[0:1, :]
        lane = _lane(CHUNK)
        sinks = jnp.zeros((1, CHUNK), F32)
        for kv in range(KV_GROUPS):
            for g in range(GROUP):
                sinks = jnp.where(lane == GROUP * kv + g, dsink_ref[kv, g:g + 1, :], sinks)
        o_ref[4:5, 0:CHUNK] = sinks

    return pl.pallas_call(body, name="pack_small", out_shape=jax.ShapeDtypeStruct((8, 3 * D_MODEL), F32),
                          compiler_params=_params(0))(pre_acc, post_acc, dbf_acc, dsink)


def _adamw_small(packs, params):
    n = len(SMALL)

    def body(*refs):
        p_ref, wmv = refs[0], refs[1:1 + 3 * n]
        outs, loss_ref, tot_ref = refs[1 + 3 * n:1 + 7 * n], refs[1 + 7 * n], refs[2 + 7 * n]
        tot = p_ref[0]
        for dev in range(1, N_DEV):
            tot = tot + p_ref[dev]
        tot_ref[...] = tot
        for i, (_, row, lo, w) in enumerate(SMALL):
            g = tot_ref[row:row + 1, lo:lo + w]
            o = outs[4 * i:4 * i + 4]
            o[0][...] = g
            o[1][...], o[2][...], o[3][...] = _adam_math(g, wmv[3 * i][...], wmv[3 * i + 1][...], wmv[3 * i + 2][...])
        loss_ref[...] = jnp.broadcast_to(jnp.sum(tot_ref[LOSS_ROW:LOSS_ROW + 1, 0:D_MODEL], axis=1, keepdims=True),
                                         loss_ref.shape)

    flat = [a for wmv in params for a in wmv]
    out_shape = [jax.ShapeDtypeStruct(wmv[0].shape, F32) for wmv in params for _ in range(4)]
    res = pl.pallas_call(body, name="adamw_small", out_shape=out_shape + [jax.ShapeDtypeStruct((8, CHUNK), F32)],
                         scratch_shapes=[pltpu.VMEM((8, 3 * D_MODEL), F32)],
                         compiler_params=_params(0))(packs, *flat)
    return [tuple(res[4 * i:4 * i + 4]) for i in range(n)], res[4 * n]


def _tile(s, want):
    return min(s, want)


def _layout_pieces():
    orig = {"qa": 0, "ka": 512, "va": 640, "za": 768, "qb": 1280, "kb": 1792, "vb": 2304, "f": 2816, "zb": 2824,
            "ga": 3336, "gb": 4360}
    pieces = []
    for name, (off, w) in SEG.items():
        if name in ("ka", "va"):
            pieces += [(orig[name] + HEAD_DIM * kv, orig[name] + HEAD_DIM * (kv + 1), off + CHUNK * kv) for kv in range(KV_GROUPS)]
        else:
            pieces.append((orig[name], orig[name] + (N_HEADS if name == "f" else w), off))
    return pieces


def _assemble_w_all(win_g):
    cols, pos = [], 0
    for lo, hi, new in sorted(_layout_pieces(), key=lambda t: t[2]):
        if new > pos:
            cols.append(jnp.zeros((D_MODEL, new - pos), win_g.dtype))
        col = lo
        while col < hi:
            dev = col // IN_SHARD
            end = min(hi, (dev + 1) * IN_SHARD)
            cols.append(win_g[dev, :, col - dev * IN_SHARD:end - dev * IN_SHARD])
            col = end
        pos = new + hi - lo
    cols.append(jnp.zeros((D_MODEL, PROJ_W - pos), win_g.dtype))
    return jnp.concatenate(cols, axis=1)


def _grad_slot(dw_groups, dev):
    lo_d, hi_d = dev * IN_SHARD, (dev + 1) * IN_SHARD
    cols = []
    for lo, hi, new in sorted(_layout_pieces()):
        a, b = max(lo, lo_d), min(hi, hi_d)
        if a < b:
            arr, off = next((g, o) for g, (_, o, w) in zip(dw_groups, DPROJ_GROUPS) if o <= new < o + w)
            cols.append(arr[:, new - off + a - lo:new - off + b - lo])
    return jnp.concatenate(cols, axis=1)


def kernel(x, c, positions, w_ada, b_ada, g_norm, w_in, b_f, sinks, w_o_swa, w_o_fox, w_out, g_final, loss_target, m_w_ada, m_b_ada, m_g_norm, m_w_in, m_b_f, m_sinks, m_w_o_swa, m_w_o_fox, m_w_out, m_g_final, v_w_ada, v_b_ada, v_g_norm, v_w_in, v_b_f, v_sinks, v_w_o_swa, v_w_o_fox, v_w_out, v_g_final):
    s = x.shape[1]
    tm = _tile(s, 256)
    ta = _tile(s, 512)
    me = 4 * lax.axis_index("x") + 2 * lax.axis_index("y") + lax.axis_index("c")
    x2, tgt = x[0], loss_target[0]

    inv_freq = np.power(np.float32(ROPE_THETA), -np.arange(0, HEAD_DIM, 2, dtype=np.float32) / HEAD_DIM)
    inv_freq = jnp.asarray(np.tile(inv_freq, CHUNK // (HEAD_DIM // 2))[None, :], F32)
    (cos, sa, sb), (win_g,), c_all = _rope_tables_and_gather(
        positions.reshape(s, 1).astype(F32), inv_freq, tm, [w_in[0].astype(BF16)], jnp.broadcast_to(c, (8, D_MODEL)))
    c_all = c_all[:, 0, :]
    b_shard = lax.dynamic_slice(b_ada, (0, me * ADA_SHARD), (1, ADA_SHARD))
    ada_part = _ada_fwd(c_all, w_ada[0], b_shard)
    (ada_g,) = _exchange([ada_part], ["gather"], "gather_ada")
    ada = lax.dynamic_index_in_dim(ada_g, me, axis=1, keepdims=False).reshape(3 * D_MODEL)
    shift, scale, gate = ada[:D_MODEL], ada[D_MODEL:2 * D_MODEL], ada[2 * D_MODEL:]
    w_all = _assemble_w_all(win_g)

    zrow = jnp.zeros((1, D_MODEL), F32)
    mod = jnp.concatenate([g_norm, scale[None], shift[None], zrow, zrow, zrow, zrow, zrow], axis=0)
    bf_row = jnp.pad(b_f, ((0, 0), (0, CHUNK - N_HEADS)))
    emat_np = np.zeros((3 * CHUNK, AUG_W), np.float32)
    ind_np = np.zeros((512, CHUNK), np.float32)
    for hd in range(N_HEADS):
        for part in range(3):
            emat_np[CHUNK * part + hd, CHUNK * hd + 64 + part] = 1.0
        ind_np[HEAD_DIM * hd:HEAD_DIM * (hd + 1), hd] = 1.0
    emat, ind = jnp.asarray(emat_np, BF16), jnp.asarray(ind_np, BF16)
    sink_rows = jnp.broadcast_to(jnp.pad(sinks.reshape(KV_GROUPS, GROUP), ((0, 0), (0, 8 - GROUP)))[:, :, None],
                                 (KV_GROUPS, 8, CHUNK))

    (h_t, qa, ka, va, za, qb, kb, vb, zb, ga, gb, xf, bounds, norms), (woa_g, wob_g, wout_g) = _fwd_proj(
        x2, mod, w_all, cos, sa, sb, bf_row, emat, ind, ta,
        [w_o_swa[0].T.astype(BF16), w_o_fox[0].T.astype(BF16), w_out[0].astype(BF16)])
    woa_t, wob_t, wout = woa_g.reshape(D_MODEL, 512), wob_g.reshape(D_MODEL, 512), wout_g.reshape(D_MODEL, D_MODEL)
    first_key, last_query = _fox_block_ranges(bounds, norms, 1)
    att_a, qa_bwd = _swa_fwd(qa, ka, va, sink_rows, _tile(s, 1024))
    att_b, qb_bwd = _fox_fwd(first_key, qb, kb, vb, ta)
    vec = jnp.concatenate([gate[None], g_final[None], zrow, zrow, zrow, zrow, zrow, zrow], axis=0)
    (dx1, ua_t, ub_t, merged_t, d_o, dya, dyb, doa, dob, dzg, post_acc) = _post(
        x2, tgt, att_a, att_b, za, zb, ga, gb, woa_t, wob_t, wout, vec, ind, emat, tm)

    da, dsink = _swa_bwd(ka, va, qa_bwd, doa, cos, sa, sb, sink_rows, ta)
    dkb, dvb, dck, dqb, dcq = _fox_bwd(last_query, kb, vb, qb_bwd, dob, ta)
    dfb, dbf_acc = _fgate_bwd(dcq, dck, xf, ta)
    dproj = [da, dqb, dkb, dvb, dzg, dfb]
    tw = _tile(s, 1024)
    dw_zg, = _token_matmul_multi(h_t, [dzg], tw, "dw_in_zg")
    dw_a, dw_qb, dw_kb, dw_vb, dw_f = _token_matmul_multi(h_t, [da, dqb, dkb, dvb, dfb], tw, "dw_in_rest")
    dw_all = [dw_a, dw_qb, dw_kb, dw_vb, dw_zg, dw_f]

    core, chip = lax.axis_index("c"), 2 * lax.axis_index("x") + lax.axis_index("y")
    win_slots = jnp.stack([jnp.stack([_grad_slot(dw_all, 2 * ch + co) for ch in range(N_CHIPS)]) for co in range(2)])
    (dwoa, dwob, dwout), (win_theirs,) = _token_matmuls_and_swap(
        [(ua_t, dya), (ub_t, dyb), (merged_t, d_o)], ta, [win_slots])
    win_full, win_half = _chip_partial(jnp.reshape(core, (1,)).astype(jnp.int32), win_slots, win_theirs, 1024,
                                       "chip_partial_w_in")
    col_slots = lambda g: g.reshape(512, N_DEV, 128).transpose(1, 0, 2)
    grad_x, pre_acc, (p_win, p_woa, p_wob, p_wout) = _bwd_pre(
        dproj, w_all, x2, dx1, mod, ta,
        [win_half, col_slots(dwoa), col_slots(dwob), dwout.reshape(N_DEV, 128, D_MODEL)],
        ["chips", "devices", "devices", "devices"])
    (packs,) = _exchange([_pack_small(pre_acc, post_acc, dbf_acc, dsink)], ["gather"], "gather_small_grads")
    others = jnp.where(jnp.arange(N_CHIPS) == chip, 0.0, 1.0).astype(F32)

    g_win, d_win, m_win, v_win = _adamw_parts(win_full, p_win, others, w_in, m_w_in, v_w_in, 256, "adamw_w_in")
    g_woa, d_woa, m_woa, v_woa = _adamw_devices(p_woa, w_o_swa, m_w_o_swa, v_w_o_swa, 128, "adamw_w_o_swa")
    g_wob, d_wob, m_wob, v_wob = _adamw_devices(p_wob, w_o_fox, m_w_o_fox, v_w_o_fox, 128, "adamw_w_o_fox")
    g_wout, d_wout, m_wout, v_wout = _adamw_devices(p_wout, w_out, m_w_out, v_w_out, 128, "adamw_w_out")
    d_ada_rows = lax.dynamic_slice(packs[:, 0, :], (0, me * ADA_SHARD), (N_DEV, ADA_SHARD))
    c_lanes = jnp.broadcast_to(c_all[:, :, None], (N_DEV, D_MODEL, CHUNK))
    g_wada, d_wada, m_wada, v_wada = _adamw_ada(c_lanes, d_ada_rows, w_ada, m_w_ada, v_w_ada, 256)

    row = lambda a: a.reshape(1, D_MODEL)
    small, loss_rows = _adamw_small(packs, [(b_ada, m_b_ada, v_b_ada), (g_norm, m_g_norm, v_g_norm), (b_f, m_b_f, v_b_f),
                                            (sinks, m_sinks, v_sinks), (row(g_final), row(m_g_final), row(v_g_final))])
    bada_o, gn_o, bf_o, sk_o, gf_o = small

    outs = []
    for k, big in enumerate(((g_wada, g_win, g_woa, g_wob, g_wout), (d_wada, d_win, d_woa, d_wob, d_wout),
                             (m_wada, m_win, m_woa, m_wob, m_wout), (v_wada, v_win, v_woa, v_wob, v_wout))):
        wada_o, win_o, woa_o, wob_o, wout_o = big
        outs += [wada_o, bada_o[k], gn_o[k], win_o, bf_o[k], sk_o[k], woa_o, wob_o, wout_o, gf_o[k].reshape(D_MODEL)]
    return (loss_rows[0, 0], grad_x[None], *outs)
```
